```python
import math
import jax, jax.numpy as jnp
from jax import lax
import numpy as np

D_MODEL = 1024
BATCH = 8
SEQ = 2048
DEPTH = 4

CHUNK = 64
Q_BLOCK = 128
N_EVEN = (DEPTH + 1) // 2
N_ODD = DEPTH // 2

POOL_WINDOWS = (2, 4, 8, 16)
POOL_GROUPS = len(POOL_WINDOWS)
POOL_GROUP_WIDTH = D_MODEL // 8
POOL_WIDTH = POOL_GROUPS * POOL_GROUP_WIDTH
LRU_WIDTH = D_MODEL
LRU_HEADS = 8
LRU_HEAD_DIM = LRU_WIDTH // LRU_HEADS
CONV_WIDTH = 4
LRU_C = 8.0
EVEN_IN_WIDTH = POOL_WIDTH + 2 * LRU_WIDTH
EVEN_MIX_WIDTH = POOL_WIDTH + LRU_WIDTH
MLA_HEADS = 8
QK_NOPE_DIM = 128
QK_ROPE_DIM = 64
V_HEAD_DIM = 128
Q_LORA_RANK = 384
KV_LORA_RANK = 256
ODD_IN_WIDTH = Q_LORA_RANK + KV_LORA_RANK + QK_ROPE_DIM
ROPE_THETA = 10000.0
D_FF = 4 * D_MODEL
DEEPNORM_ALPHA = (2 * DEPTH) ** 0.25
DEEPNORM_BETA = (8 * DEPTH) ** -0.25
LN_EPS = 1e-5
RMS_EPS = 1e-6

kernel_name = "pool_rglru_mla_deepnorm_hybrid"


def layer_norm(x, g, b):
    xf = x.astype(jnp.float32)
    mu = jnp.mean(xf, axis=-1, keepdims=True)
    var = jnp.mean(jnp.square(xf - mu), axis=-1, keepdims=True)
    y = (xf - mu) * lax.rsqrt(var + LN_EPS) * g.astype(jnp.float32) + b.astype(jnp.float32)
    return y.astype(x.dtype)


def rms_norm(x, g):
    xf = x.astype(jnp.float32)
    y = xf * lax.rsqrt(jnp.mean(jnp.square(xf), axis=-1, keepdims=True) + RMS_EPS) * g.astype(jnp.float32)
    return y.astype(x.dtype)


def rope_tables(positions):
    inv_freq = ROPE_THETA ** (-jnp.arange(0, QK_ROPE_DIM, 2, dtype=jnp.float32) / QK_ROPE_DIM)
    ang = positions.astype(jnp.float32)[..., None] * inv_freq
    return jnp.cos(ang), jnp.sin(ang)


def apply_rope(x, cos, sin):
    xf = x.astype(jnp.float32)
    x1, x2 = jnp.split(xf, 2, axis=-1)
    return jnp.concatenate([x1 * cos - x2 * sin, x1 * sin + x2 * cos], axis=-1).astype(x.dtype)


def multiscale_pool(u, w_pool, pool_scale):
    b, s, _ = u.shape
    ug = u.astype(jnp.float32).reshape(b, s, POOL_GROUPS, POOL_GROUP_WIDTH)
    csum = jnp.cumsum(ug, axis=1)
    t = jnp.arange(s)
    diffs = []
    for g, w in enumerate(POOL_WINDOWS):
        cg = csum[:, :, g]
        lagged = jnp.pad(cg, ((0, 0), (w, 0), (0, 0)))[:, :s]
        count = jnp.minimum(t + 1, w).astype(jnp.float32)[None, :, None]
        diffs.append((cg - lagged) / count - ug[:, :, g])
    d = jnp.stack(diffs, axis=2)
    y = jnp.einsum('bsgc,gcd->bsgd', d, w_pool.astype(jnp.float32))
    y = y.reshape(b, s, POOL_WIDTH) * pool_scale.astype(jnp.float32)
    return y.astype(u.dtype)


def causal_depthwise_conv(u, w, bias):
    c = u.shape[-1]
    y = lax.conv_general_dilated(u, w[:, None, :].astype(u.dtype), window_strides=(1,),
                                 padding=[(CONV_WIDTH - 1, 0)],
                                 dimension_numbers=('NWC', 'WIO', 'NWC'),
                                 feature_group_count=c)
    return y + bias.astype(u.dtype)


def rg_lru(u, w_a, b_a, w_x, b_x, lam):
    b, s, _ = u.shape
    uf = u.astype(jnp.float32)
    uh = uf.reshape(b, s, LRU_HEADS, LRU_HEAD_DIM)
    r = jax.nn.sigmoid(jnp.einsum('bshc,hcd->bshd', uh, w_a.astype(jnp.float32)).reshape(b, s, LRU_WIDTH)
                       + b_a.astype(jnp.float32))
    i = jax.nn.sigmoid(jnp.einsum('bshc,hcd->bshd', uh, w_x.astype(jnp.float32)).reshape(b, s, LRU_WIDTH)
                       + b_x.astype(jnp.float32))
    log_a = -LRU_C * r * jax.nn.softplus(-lam.astype(jnp.float32))
    a = jnp.exp(log_a)
    mult = jnp.sqrt(-jnp.expm1(2.0 * log_a))
    xin = mult * (i * uf)

    def combine(lhs, rhs):
        a1, b1 = lhs
        a2, b2 = rhs
        return a1 * a2, a2 * b1 + b2

    _, h = lax.associative_scan(combine, (a, xin), axis=1)
    return h


def pool_lru_mixer(x, w_in, w_pool, pool_scale, conv_w, conv_b, w_a, b_a, w_x, b_x, lam, w_out):
    proj = x @ w_in
    u_pool = proj[..., :POOL_WIDTH]
    u_lru = proj[..., POOL_WIDTH:POOL_WIDTH + LRU_WIDTH]
    u_gate = proj[..., POOL_WIDTH + LRU_WIDTH:]
    y_pool = multiscale_pool(u_pool, w_pool, pool_scale)
    h = rg_lru(causal_depthwise_conv(u_lru, conv_w, conv_b), w_a, b_a, w_x, b_x, lam)
    y_lru = (h * jax.nn.gelu(u_gate.astype(jnp.float32))).astype(x.dtype)
    return jnp.concatenate([y_pool, y_lru], axis=-1) @ w_out


def mla_mixer(x, cos, sin, w_down, q_norm_g, kv_norm_g, w_qb, w_kvb, w_o):
    b, s, _ = x.shape
    down = x @ w_down
    cq = rms_norm(down[..., :Q_LORA_RANK], q_norm_g)
    ckv = rms_norm(down[..., Q_LORA_RANK:Q_LORA_RANK + KV_LORA_RANK], kv_norm_g)
    k_pe = apply_rope(down[..., Q_LORA_RANK + KV_LORA_RANK:], cos, sin)
    q = (cq @ w_qb).reshape(b, s, MLA_HEADS, QK_NOPE_DIM + QK_ROPE_DIM)
    q_nope = q[..., :QK_NOPE_DIM]
    q_pe = apply_rope(q[..., QK_NOPE_DIM:], cos[:, :, None, :], sin[:, :, None, :])
    kv = (ckv @ w_kvb).reshape(b, s, MLA_HEADS, QK_NOPE_DIM + V_HEAD_DIM)
    k_nope = kv[..., :QK_NOPE_DIM]
    v = kv[..., QK_NOPE_DIM:]
    scale = (QK_NOPE_DIM + QK_ROPE_DIM) ** -0.5
    chunk_id = jnp.arange(s) // CHUNK
    neg = jnp.finfo(jnp.float32).min
    outs = []
    for qs in range(0, s, Q_BLOCK):
        ke = qs + Q_BLOCK
        sc = (jnp.einsum('bqhd,bkhd->bhqk', q_nope[:, qs:ke], k_nope[:, :ke],
                         preferred_element_type=jnp.float32)
              + jnp.einsum('bqhr,bkr->bhqk', q_pe[:, qs:ke], k_pe[:, :ke],
                           preferred_element_type=jnp.float32)) * scale
        mask = chunk_id[:ke][None, :] <= chunk_id[qs:ke][:, None]
        p = jax.nn.softmax(jnp.where(mask, sc, neg), axis=-1).astype(v.dtype)
        outs.append(jnp.einsum('bhqk,bkhv->bqhv', p, v[:, :ke]))
    o = jnp.concatenate(outs, axis=1).reshape(b, s, MLA_HEADS * V_HEAD_DIM)
    return o @ w_o


def squared_relu_mlp(x, w1, w2):
    return jnp.square(jax.nn.relu(x @ w1)) @ w2


def _fwd_setup_inputs(seed: int = 0) -> dict:
    key = jax.random.key(seed)
    ks = jax.random.split(key, 28)
    f32 = jnp.float32
    nrm = lambda k, shp, sc: jax.random.normal(k, shp, f32) * sc
    x = jax.random.normal(ks[0], (BATCH, SEQ, D_MODEL), f32)
    offset = jax.random.randint(ks[1], (BATCH, 1), 0, 4096, dtype=jnp.int32)
    positions = (offset + jnp.arange(SEQ, dtype=jnp.int32)[None, :]).astype(jnp.int32)
    ln_mix_g = 1.0 + nrm(ks[2], (DEPTH, D_MODEL), 0.1)
    ln_mix_b = nrm(ks[3], (DEPTH, D_MODEL), 0.02)
    ln_ffn_g = 1.0 + nrm(ks[4], (DEPTH, D_MODEL), 0.1)
    ln_ffn_b = nrm(ks[5], (DEPTH, D_MODEL), 0.02)
    even_w_in = nrm(ks[6], (N_EVEN, D_MODEL, EVEN_IN_WIDTH), D_MODEL ** -0.5)
    pool_w = nrm(ks[7], (N_EVEN, POOL_GROUPS, POOL_GROUP_WIDTH, POOL_GROUP_WIDTH), POOL_GROUP_WIDTH ** -0.5)
    pool_scale = 1.0 + nrm(ks[8], (N_EVEN, POOL_WIDTH), 0.1)
    lru_conv_w = nrm(ks[9], (N_EVEN, CONV_WIDTH, LRU_WIDTH), CONV_WIDTH ** -0.5)
    lru_conv_b = nrm(ks[10], (N_EVEN, LRU_WIDTH), 0.02)
    lru_w_a = nrm(ks[11], (N_EVEN, LRU_HEADS, LRU_HEAD_DIM, LRU_HEAD_DIM), LRU_HEAD_DIM ** -0.5)
    lru_b_a = nrm(ks[12], (N_EVEN, LRU_WIDTH), 0.02)
    lru_w_x = nrm(ks[13], (N_EVEN, LRU_HEADS, LRU_HEAD_DIM, LRU_HEAD_DIM), LRU_HEAD_DIM ** -0.5)
    lru_b_x = nrm(ks[14], (N_EVEN, LRU_WIDTH), 0.02)
    a_c = jax.random.uniform(ks[15], (N_EVEN, LRU_WIDTH), f32, 0.9, 0.999)
    s_a = a_c ** (1.0 / LRU_C)
    lru_lambda = jnp.log(s_a) - jnp.log1p(-s_a)
    even_w_out = nrm(ks[16], (N_EVEN, EVEN_MIX_WIDTH, D_MODEL), EVEN_MIX_WIDTH ** -0.5 * DEEPNORM_BETA)
    mla_w_down = nrm(ks[17], (N_ODD, D_MODEL, ODD_IN_WIDTH), D_MODEL ** -0.5)
    mla_q_norm_g = 1.0 + nrm(ks[18], (N_ODD, Q_LORA_RANK), 0.1)
    mla_kv_norm_g = 1.0 + nrm(ks[19], (N_ODD, KV_LORA_RANK), 0.1)
    mla_w_qb = nrm(ks[20], (N_ODD, Q_LORA_RANK, MLA_HEADS * (QK_NOPE_DIM + QK_ROPE_DIM)), Q_LORA_RANK ** -0.5)
    mla_w_kvb = nrm(ks[21], (N_ODD, KV_LORA_RANK, MLA_HEADS * (QK_NOPE_DIM + V_HEAD_DIM)), KV_LORA_RANK ** -0.5)
    mla_w_o = nrm(ks[22], (N_ODD, MLA_HEADS * V_HEAD_DIM, D_MODEL), (MLA_HEADS * V_HEAD_DIM) ** -0.5 * DEEPNORM_BETA)
    mlp_w1 = nrm(ks[23], (DEPTH, D_MODEL, D_FF), D_MODEL ** -0.5)
    mlp_w2 = nrm(ks[24], (DEPTH, D_FF, D_MODEL), D_FF ** -0.5 * DEEPNORM_BETA)
    return {"x": x, "positions": positions,
            "ln_mix_g": ln_mix_g, "ln_mix_b": ln_mix_b, "ln_ffn_g": ln_ffn_g, "ln_ffn_b": ln_ffn_b,
            "even_w_in": even_w_in, "pool_w": pool_w, "pool_scale": pool_scale,
            "lru_conv_w": lru_conv_w, "lru_conv_b": lru_conv_b,
            "lru_w_a": lru_w_a, "lru_b_a": lru_b_a, "lru_w_x": lru_w_x, "lru_b_x": lru_b_x,
            "lru_lambda": lru_lambda, "even_w_out": even_w_out,
            "mla_w_down": mla_w_down, "mla_q_norm_g": mla_q_norm_g, "mla_kv_norm_g": mla_kv_norm_g,
            "mla_w_qb": mla_w_qb, "mla_w_kvb": mla_w_kvb, "mla_w_o": mla_w_o,
            "mlp_w1": mlp_w1, "mlp_w2": mlp_w2}


def _fwd_reference(x, positions, ln_mix_g, ln_mix_b, ln_ffn_g, ln_ffn_b,
              even_w_in, pool_w, pool_scale, lru_conv_w, lru_conv_b,
              lru_w_a, lru_b_a, lru_w_x, lru_b_x, lru_lambda, even_w_out,
              mla_w_down, mla_q_norm_g, mla_kv_norm_g, mla_w_qb, mla_w_kvb, mla_w_o,
              mlp_w1, mlp_w2):
    cos, sin = rope_tables(positions)
    for layer in range(DEPTH):
        j = layer // 2
        if layer % 2 == 0:
            mix = pool_lru_mixer(x, even_w_in[j], pool_w[j], pool_scale[j],
                                 lru_conv_w[j], lru_conv_b[j], lru_w_a[j], lru_b_a[j],
                                 lru_w_x[j], lru_b_x[j], lru_lambda[j], even_w_out[j])
        else:
            mix = mla_mixer(x, cos, sin, mla_w_down[j], mla_q_norm_g[j], mla_kv_norm_g[j],
                            mla_w_qb[j], mla_w_kvb[j], mla_w_o[j])
        x = layer_norm(DEEPNORM_ALPHA * x + mix, ln_mix_g[layer], ln_mix_b[layer])
        x = layer_norm(DEEPNORM_ALPHA * x + squared_relu_mlp(x, mlp_w1[layer], mlp_w2[layer]),
                       ln_ffn_g[layer], ln_ffn_b[layer])
    return x


import jax as _jax
import jax.numpy as _jnp

TWIN_FORMAT = 'train_step'
FWD_PARAMS = ['x', 'positions', 'ln_mix_g', 'ln_mix_b', 'ln_ffn_g', 'ln_ffn_b', 'even_w_in', 'pool_w', 'pool_scale', 'lru_conv_w', 'lru_conv_b', 'lru_w_a', 'lru_b_a', 'lru_w_x', 'lru_b_x', 'lru_lambda', 'even_w_out', 'mla_w_down', 'mla_q_norm_g', 'mla_kv_norm_g', 'mla_w_qb', 'mla_w_kvb', 'mla_w_o', 'mlp_w1', 'mlp_w2']
TWIN_WEIGHTS = ['ln_mix_g', 'ln_mix_b', 'ln_ffn_g', 'ln_ffn_b', 'even_w_in', 'pool_w', 'pool_scale', 'lru_conv_w', 'lru_conv_b', 'lru_w_a', 'lru_b_a', 'lru_w_x', 'lru_b_x', 'lru_lambda', 'even_w_out', 'mla_w_down', 'mla_q_norm_g', 'mla_kv_norm_g', 'mla_w_qb', 'mla_w_kvb', 'mla_w_o', 'mlp_w1', 'mlp_w2']
TWIN_DIFF_INPUT = 'x'
TWIN_INPUTS = ['x', 'positions', 'ln_mix_g', 'ln_mix_b', 'ln_ffn_g', 'ln_ffn_b', 'even_w_in', 'pool_w', 'pool_scale', 'lru_conv_w', 'lru_conv_b', 'lru_w_a', 'lru_b_a', 'lru_w_x', 'lru_b_x', 'lru_lambda', 'even_w_out', 'mla_w_down', 'mla_q_norm_g', 'mla_kv_norm_g', 'mla_w_qb', 'mla_w_kvb', 'mla_w_o', 'mlp_w1', 'mlp_w2', 'loss_target', 'm_ln_mix_g', 'm_ln_mix_b', 'm_ln_ffn_g', 'm_ln_ffn_b', 'm_even_w_in', 'm_pool_w', 'm_pool_scale', 'm_lru_conv_w', 'm_lru_conv_b', 'm_lru_w_a', 'm_lru_b_a', 'm_lru_w_x', 'm_lru_b_x', 'm_lru_lambda', 'm_even_w_out', 'm_mla_w_down', 'm_mla_q_norm_g', 'm_mla_kv_norm_g', 'm_mla_w_qb', 'm_mla_w_kvb', 'm_mla_w_o', 'm_mlp_w1', 'm_mlp_w2', 'v_ln_mix_g', 'v_ln_mix_b', 'v_ln_ffn_g', 'v_ln_ffn_b', 'v_even_w_in', 'v_pool_w', 'v_pool_scale', 'v_lru_conv_w', 'v_lru_conv_b', 'v_lru_w_a', 'v_lru_b_a', 'v_lru_w_x', 'v_lru_b_x', 'v_lru_lambda', 'v_even_w_out', 'v_mla_w_down', 'v_mla_q_norm_g', 'v_mla_kv_norm_g', 'v_mla_w_qb', 'v_mla_w_kvb', 'v_mla_w_o', 'v_mlp_w1', 'v_mlp_w2']
TWIN_OUTPUTS = ['loss', 'grad_x', 'grad_ln_mix_g', 'grad_ln_mix_b', 'grad_ln_ffn_g', 'grad_ln_ffn_b', 'grad_even_w_in', 'grad_pool_w', 'grad_pool_scale', 'grad_lru_conv_w', 'grad_lru_conv_b', 'grad_lru_w_a', 'grad_lru_b_a', 'grad_lru_w_x', 'grad_lru_b_x', 'grad_lru_lambda', 'grad_even_w_out', 'grad_mla_w_down', 'grad_mla_q_norm_g', 'grad_mla_kv_norm_g', 'grad_mla_w_qb', 'grad_mla_w_kvb', 'grad_mla_w_o', 'grad_mlp_w1', 'grad_mlp_w2', 'delta_ln_mix_g', 'delta_ln_mix_b', 'delta_ln_ffn_g', 'delta_ln_ffn_b', 'delta_even_w_in', 'delta_pool_w', 'delta_pool_scale', 'delta_lru_conv_w', 'delta_lru_conv_b', 'delta_lru_w_a', 'delta_lru_b_a', 'delta_lru_w_x', 'delta_lru_b_x', 'delta_lru_lambda', 'delta_even_w_out', 'delta_mla_w_down', 'delta_mla_q_norm_g', 'delta_mla_kv_norm_g', 'delta_mla_w_qb', 'delta_mla_w_kvb', 'delta_mla_w_o', 'delta_mlp_w1', 'delta_mlp_w2', 'new_m_ln_mix_g', 'new_m_ln_mix_b', 'new_m_ln_ffn_g', 'new_m_ln_ffn_b', 'new_m_even_w_in', 'new_m_pool_w', 'new_m_pool_scale', 'new_m_lru_conv_w', 'new_m_lru_conv_b', 'new_m_lru_w_a', 'new_m_lru_b_a', 'new_m_lru_w_x', 'new_m_lru_b_x', 'new_m_lru_lambda', 'new_m_even_w_out', 'new_m_mla_w_down', 'new_m_mla_q_norm_g', 'new_m_mla_kv_norm_g', 'new_m_mla_w_qb', 'new_m_mla_w_kvb', 'new_m_mla_w_o', 'new_m_mlp_w1', 'new_m_mlp_w2', 'new_v_ln_mix_g', 'new_v_ln_mix_b', 'new_v_ln_ffn_g', 'new_v_ln_ffn_b', 'new_v_even_w_in', 'new_v_pool_w', 'new_v_pool_scale', 'new_v_lru_conv_w', 'new_v_lru_conv_b', 'new_v_lru_w_a', 'new_v_lru_b_a', 'new_v_lru_w_x', 'new_v_lru_b_x', 'new_v_lru_lambda', 'new_v_even_w_out', 'new_v_mla_w_down', 'new_v_mla_q_norm_g', 'new_v_mla_kv_norm_g', 'new_v_mla_w_qb', 'new_v_mla_w_kvb', 'new_v_mla_w_o', 'new_v_mlp_w1', 'new_v_mlp_w2']
TWIN_LEAF_KINDS = {'loss': 'loss', 'grad_x': 'grad_x', 'grad_ln_mix_g': 'grad_w', 'grad_ln_mix_b': 'grad_w', 'grad_ln_ffn_g': 'grad_w', 'grad_ln_ffn_b': 'grad_w', 'grad_even_w_in': 'grad_w', 'grad_pool_w': 'grad_w', 'grad_pool_scale': 'grad_w', 'grad_lru_conv_w': 'grad_w', 'grad_lru_conv_b': 'grad_w', 'grad_lru_w_a': 'grad_w', 'grad_lru_b_a': 'grad_w', 'grad_lru_w_x': 'grad_w', 'grad_lru_b_x': 'grad_w', 'grad_lru_lambda': 'grad_w', 'grad_even_w_out': 'grad_w', 'grad_mla_w_down': 'grad_w', 'grad_mla_q_norm_g': 'grad_w', 'grad_mla_kv_norm_g': 'grad_w', 'grad_mla_w_qb': 'grad_w', 'grad_mla_w_kvb': 'grad_w', 'grad_mla_w_o': 'grad_w', 'grad_mlp_w1': 'grad_w', 'grad_mlp_w2': 'grad_w', 'delta_ln_mix_g': 'delta_w', 'delta_ln_mix_b': 'delta_w', 'delta_ln_ffn_g': 'delta_w', 'delta_ln_ffn_b': 'delta_w', 'delta_even_w_in': 'delta_w', 'delta_pool_w': 'delta_w', 'delta_pool_scale': 'delta_w', 'delta_lru_conv_w': 'delta_w', 'delta_lru_conv_b': 'delta_w', 'delta_lru_w_a': 'delta_w', 'delta_lru_b_a': 'delta_w', 'delta_lru_w_x': 'delta_w', 'delta_lru_b_x': 'delta_w', 'delta_lru_lambda': 'delta_w', 'delta_even_w_out': 'delta_w', 'delta_mla_w_down': 'delta_w', 'delta_mla_q_norm_g': 'delta_w', 'delta_mla_kv_norm_g': 'delta_w', 'delta_mla_w_qb': 'delta_w', 'delta_mla_w_kvb': 'delta_w', 'delta_mla_w_o': 'delta_w', 'delta_mlp_w1': 'delta_w', 'delta_mlp_w2': 'delta_w', 'new_m_ln_mix_g': 'new_m', 'new_m_ln_mix_b': 'new_m', 'new_m_ln_ffn_g': 'new_m', 'new_m_ln_ffn_b': 'new_m', 'new_m_even_w_in': 'new_m', 'new_m_pool_w': 'new_m', 'new_m_pool_scale': 'new_m', 'new_m_lru_conv_w': 'new_m', 'new_m_lru_conv_b': 'new_m', 'new_m_lru_w_a': 'new_m', 'new_m_lru_b_a': 'new_m', 'new_m_lru_w_x': 'new_m', 'new_m_lru_b_x': 'new_m', 'new_m_lru_lambda': 'new_m', 'new_m_even_w_out': 'new_m', 'new_m_mla_w_down': 'new_m', 'new_m_mla_q_norm_g': 'new_m', 'new_m_mla_kv_norm_g': 'new_m', 'new_m_mla_w_qb': 'new_m', 'new_m_mla_w_kvb': 'new_m', 'new_m_mla_w_o': 'new_m', 'new_m_mlp_w1': 'new_m', 'new_m_mlp_w2': 'new_m', 'new_v_ln_mix_g': 'new_v', 'new_v_ln_mix_b': 'new_v', 'new_v_ln_ffn_g': 'new_v', 'new_v_ln_ffn_b': 'new_v', 'new_v_even_w_in': 'new_v', 'new_v_pool_w': 'new_v', 'new_v_pool_scale': 'new_v', 'new_v_lru_conv_w': 'new_v', 'new_v_lru_conv_b': 'new_v', 'new_v_lru_w_a': 'new_v', 'new_v_lru_b_a': 'new_v', 'new_v_lru_w_x': 'new_v', 'new_v_lru_b_x': 'new_v', 'new_v_lru_lambda': 'new_v', 'new_v_even_w_out': 'new_v', 'new_v_mla_w_down': 'new_v', 'new_v_mla_q_norm_g': 'new_v', 'new_v_mla_kv_norm_g': 'new_v', 'new_v_mla_w_qb': 'new_v', 'new_v_mla_w_kvb': 'new_v', 'new_v_mla_w_o': 'new_v', 'new_v_mlp_w1': 'new_v', 'new_v_mlp_w2': 'new_v'}


def _forward(args):
    return _fwd_reference(*[args[k] for k in FWD_PARAMS])


def _output_shape():
    out = _jax.eval_shape(lambda: _forward(_fwd_setup_inputs(0)))
    return out.shape, out.dtype

N_MICROBATCH = 1
ADAM_LR = 0.001
ADAM_B1 = 0.9
ADAM_B2 = 0.999
ADAM_EPS = 1e-08
ADAM_WD = 0.01
ADAM_STEP = 10
PER_EXAMPLE_BATCH_AXIS = {'x': 0, 'positions': 0, 'loss_target': 0}
SHARED_INPUTS = []
_WEIGHT_DTYPES = {'ln_mix_g': _jnp.float32, 'ln_mix_b': _jnp.float32, 'ln_ffn_g': _jnp.float32, 'ln_ffn_b': _jnp.float32, 'even_w_in': _jnp.float32, 'pool_w': _jnp.float32, 'pool_scale': _jnp.float32, 'lru_conv_w': _jnp.float32, 'lru_conv_b': _jnp.float32, 'lru_w_a': _jnp.float32, 'lru_b_a': _jnp.float32, 'lru_w_x': _jnp.float32, 'lru_b_x': _jnp.float32, 'lru_lambda': _jnp.float32, 'even_w_out': _jnp.float32, 'mla_w_down': _jnp.float32, 'mla_q_norm_g': _jnp.float32, 'mla_kv_norm_g': _jnp.float32, 'mla_w_qb': _jnp.float32, 'mla_w_kvb': _jnp.float32, 'mla_w_o': _jnp.float32, 'mlp_w1': _jnp.float32, 'mlp_w2': _jnp.float32}
MOMENT_SCALE = {'ln_mix_g': 2.728605e+00, 'ln_mix_b': 8.624348e-01, 'ln_ffn_g': 9.278641e+00, 'ln_ffn_b': 2.221760e+00, 'even_w_in': 2.794577e-02, 'pool_w': 2.568885e-02, 'pool_scale': 2.473979e-02, 'lru_conv_w': 4.110270e-02, 'lru_conv_b': 3.263623e-01, 'lru_w_a': 7.528236e-03, 'lru_b_a': 6.607589e-03, 'lru_w_x': 1.411396e-02, 'lru_b_x': 1.338821e-02, 'lru_lambda': 1.434608e-02, 'even_w_out': 8.824414e-02, 'mla_w_down': 4.209347e-02, 'mla_q_norm_g': 7.912932e-03, 'mla_kv_norm_g': 7.219671e-02, 'mla_w_qb': 3.949295e-03, 'mla_w_kvb': 2.472227e-02, 'mla_w_o': 8.122484e-02, 'mlp_w1': 2.558656e-02, 'mlp_w2': 2.474472e-01}


def _to_microbatches(a, axis):
    t = _jnp.moveaxis(a, axis, 0)
    t = t.reshape((N_MICROBATCH, t.shape[0] // N_MICROBATCH) + t.shape[1:])
    return _jnp.moveaxis(t, 1, axis + 1)


def setup_inputs(seed: int = 0) -> dict:
    inp = _fwd_setup_inputs(seed)
    key = _jax.random.fold_in(_jax.random.key(seed), 7919)
    shape, _ = _output_shape()
    out = dict(inp)
    out["loss_target"] = _jax.random.normal(_jax.random.fold_in(key, 0), shape, _jnp.float32)
    for i, name in enumerate(TWIN_WEIGHTS):
        w = inp[name].astype(_jnp.float32)
        if MOMENT_SCALE is None:
            s = _jnp.sqrt(_jnp.mean(_jnp.square(w)) + 1e-30)
        else:
            s = MOMENT_SCALE[name]
        km, kv = _jax.random.split(_jax.random.fold_in(key, i + 1))
        out[name] = w
        out["m_" + name] = s * _jax.random.normal(km, w.shape, _jnp.float32)
        out["v_" + name] = (s * s) * _jax.random.uniform(kv, w.shape, _jnp.float32, 0.5, 1.5)
    if N_MICROBATCH > 1:
        for name, axis in PER_EXAMPLE_BATCH_AXIS.items():
            out[name] = _to_microbatches(out[name], axis)
    return {'x': out['x'], 'positions': out['positions'], 'ln_mix_g': out['ln_mix_g'], 'ln_mix_b': out['ln_mix_b'], 'ln_ffn_g': out['ln_ffn_g'], 'ln_ffn_b': out['ln_ffn_b'], 'even_w_in': out['even_w_in'], 'pool_w': out['pool_w'], 'pool_scale': out['pool_scale'], 'lru_conv_w': out['lru_conv_w'], 'lru_conv_b': out['lru_conv_b'], 'lru_w_a': out['lru_w_a'], 'lru_b_a': out['lru_b_a'], 'lru_w_x': out['lru_w_x'], 'lru_b_x': out['lru_b_x'], 'lru_lambda': out['lru_lambda'], 'even_w_out': out['even_w_out'], 'mla_w_down': out['mla_w_down'], 'mla_q_norm_g': out['mla_q_norm_g'], 'mla_kv_norm_g': out['mla_kv_norm_g'], 'mla_w_qb': out['mla_w_qb'], 'mla_w_kvb': out['mla_w_kvb'], 'mla_w_o': out['mla_w_o'], 'mlp_w1': out['mlp_w1'], 'mlp_w2': out['mlp_w2'], 'loss_target': out['loss_target'], 'm_ln_mix_g': out['m_ln_mix_g'], 'm_ln_mix_b': out['m_ln_mix_b'], 'm_ln_ffn_g': out['m_ln_ffn_g'], 'm_ln_ffn_b': out['m_ln_ffn_b'], 'm_even_w_in': out['m_even_w_in'], 'm_pool_w': out['m_pool_w'], 'm_pool_scale': out['m_pool_scale'], 'm_lru_conv_w': out['m_lru_conv_w'], 'm_lru_conv_b': out['m_lru_conv_b'], 'm_lru_w_a': out['m_lru_w_a'], 'm_lru_b_a': out['m_lru_b_a'], 'm_lru_w_x': out['m_lru_w_x'], 'm_lru_b_x': out['m_lru_b_x'], 'm_lru_lambda': out['m_lru_lambda'], 'm_even_w_out': out['m_even_w_out'], 'm_mla_w_down': out['m_mla_w_down'], 'm_mla_q_norm_g': out['m_mla_q_norm_g'], 'm_mla_kv_norm_g': out['m_mla_kv_norm_g'], 'm_mla_w_qb': out['m_mla_w_qb'], 'm_mla_w_kvb': out['m_mla_w_kvb'], 'm_mla_w_o': out['m_mla_w_o'], 'm_mlp_w1': out['m_mlp_w1'], 'm_mlp_w2': out['m_mlp_w2'], 'v_ln_mix_g': out['v_ln_mix_g'], 'v_ln_mix_b': out['v_ln_mix_b'], 'v_ln_ffn_g': out['v_ln_ffn_g'], 'v_ln_ffn_b': out['v_ln_ffn_b'], 'v_even_w_in': out['v_even_w_in'], 'v_pool_w': out['v_pool_w'], 'v_pool_scale': out['v_pool_scale'], 'v_lru_conv_w': out['v_lru_conv_w'], 'v_lru_conv_b': out['v_lru_conv_b'], 'v_lru_w_a': out['v_lru_w_a'], 'v_lru_b_a': out['v_lru_b_a'], 'v_lru_w_x': out['v_lru_w_x'], 'v_lru_b_x': out['v_lru_b_x'], 'v_lru_lambda': out['v_lru_lambda'], 'v_even_w_out': out['v_even_w_out'], 'v_mla_w_down': out['v_mla_w_down'], 'v_mla_q_norm_g': out['v_mla_q_norm_g'], 'v_mla_kv_norm_g': out['v_mla_kv_norm_g'], 'v_mla_w_qb': out['v_mla_w_qb'], 'v_mla_w_kvb': out['v_mla_w_kvb'], 'v_mla_w_o': out['v_mla_w_o'], 'v_mlp_w1': out['v_mlp_w1'], 'v_mlp_w2': out['v_mlp_w2']}


def _loss(weights, diff, rest, loss_target):
    with _jax.named_scope("forward"):
        args = {**rest, TWIN_DIFF_INPUT: diff, **{k: w.astype(_WEIGHT_DTYPES[k]) for k, w in weights.items()}}
        y = _forward(args)
    with _jax.named_scope("loss_head"):
        err = _jnp.square(y.astype(_jnp.float32) - loss_target)
        return 0.5 * _jnp.sum(_jnp.mean(err, axis=-1)) if err.ndim else 0.5 * err


def _adamw(w, g, m, v):
    m = ADAM_B1 * m + (1.0 - ADAM_B1) * g
    v = ADAM_B2 * v + (1.0 - ADAM_B2) * _jnp.square(g)
    m_hat = m / (1.0 - ADAM_B1 ** ADAM_STEP)
    v_hat = v / (1.0 - ADAM_B2 ** ADAM_STEP)
    delta = -ADAM_LR * (m_hat / (_jnp.sqrt(v_hat) + ADAM_EPS) + ADAM_WD * w)
    return delta, m, v


def reference(x, positions, ln_mix_g, ln_mix_b, ln_ffn_g, ln_ffn_b, even_w_in, pool_w, pool_scale, lru_conv_w, lru_conv_b, lru_w_a, lru_b_a, lru_w_x, lru_b_x, lru_lambda, even_w_out, mla_w_down, mla_q_norm_g, mla_kv_norm_g, mla_w_qb, mla_w_kvb, mla_w_o, mlp_w1, mlp_w2, loss_target, m_ln_mix_g, m_ln_mix_b, m_ln_ffn_g, m_ln_ffn_b, m_even_w_in, m_pool_w, m_pool_scale, m_lru_conv_w, m_lru_conv_b, m_lru_w_a, m_lru_b_a, m_lru_w_x, m_lru_b_x, m_lru_lambda, m_even_w_out, m_mla_w_down, m_mla_q_norm_g, m_mla_kv_norm_g, m_mla_w_qb, m_mla_w_kvb, m_mla_w_o, m_mlp_w1, m_mlp_w2, v_ln_mix_g, v_ln_mix_b, v_ln_ffn_g, v_ln_ffn_b, v_even_w_in, v_pool_w, v_pool_scale, v_lru_conv_w, v_lru_conv_b, v_lru_w_a, v_lru_b_a, v_lru_w_x, v_lru_b_x, v_lru_lambda, v_even_w_out, v_mla_w_down, v_mla_q_norm_g, v_mla_kv_norm_g, v_mla_w_qb, v_mla_w_kvb, v_mla_w_o, v_mlp_w1, v_mlp_w2):
    given = dict(x=x, positions=positions, ln_mix_g=ln_mix_g, ln_mix_b=ln_mix_b, ln_ffn_g=ln_ffn_g, ln_ffn_b=ln_ffn_b, even_w_in=even_w_in, pool_w=pool_w, pool_scale=pool_scale, lru_conv_w=lru_conv_w, lru_conv_b=lru_conv_b, lru_w_a=lru_w_a, lru_b_a=lru_b_a, lru_w_x=lru_w_x, lru_b_x=lru_b_x, lru_lambda=lru_lambda, even_w_out=even_w_out, mla_w_down=mla_w_down, mla_q_norm_g=mla_q_norm_g, mla_kv_norm_g=mla_kv_norm_g, mla_w_qb=mla_w_qb, mla_w_kvb=mla_w_kvb, mla_w_o=mla_w_o, mlp_w1=mlp_w1, mlp_w2=mlp_w2, loss_target=loss_target, m_ln_mix_g=m_ln_mix_g, m_ln_mix_b=m_ln_mix_b, m_ln_ffn_g=m_ln_ffn_g, m_ln_ffn_b=m_ln_ffn_b, m_even_w_in=m_even_w_in, m_pool_w=m_pool_w, m_pool_scale=m_pool_scale, m_lru_conv_w=m_lru_conv_w, m_lru_conv_b=m_lru_conv_b, m_lru_w_a=m_lru_w_a, m_lru_b_a=m_lru_b_a, m_lru_w_x=m_lru_w_x, m_lru_b_x=m_lru_b_x, m_lru_lambda=m_lru_lambda, m_even_w_out=m_even_w_out, m_mla_w_down=m_mla_w_down, m_mla_q_norm_g=m_mla_q_norm_g, m_mla_kv_norm_g=m_mla_kv_norm_g, m_mla_w_qb=m_mla_w_qb, m_mla_w_kvb=m_mla_w_kvb, m_mla_w_o=m_mla_w_o, m_mlp_w1=m_mlp_w1, m_mlp_w2=m_mlp_w2, v_ln_mix_g=v_ln_mix_g, v_ln_mix_b=v_ln_mix_b, v_ln_ffn_g=v_ln_ffn_g, v_ln_ffn_b=v_ln_ffn_b, v_even_w_in=v_even_w_in, v_pool_w=v_pool_w, v_pool_scale=v_pool_scale, v_lru_conv_w=v_lru_conv_w, v_lru_conv_b=v_lru_conv_b, v_lru_w_a=v_lru_w_a, v_lru_b_a=v_lru_b_a, v_lru_w_x=v_lru_w_x, v_lru_b_x=v_lru_b_x, v_lru_lambda=v_lru_lambda, v_even_w_out=v_even_w_out, v_mla_w_down=v_mla_w_down, v_mla_q_norm_g=v_mla_q_norm_g, v_mla_kv_norm_g=v_mla_kv_norm_g, v_mla_w_qb=v_mla_w_qb, v_mla_w_kvb=v_mla_w_kvb, v_mla_w_o=v_mla_w_o, v_mlp_w1=v_mlp_w1, v_mlp_w2=v_mlp_w2)
    weights = {n: given[n] for n in TWIN_WEIGHTS}
    shared = {n: given[n] for n in SHARED_INPUTS}
    per_example = {n: given[n] for n in ['x', 'positions']}
    grad_fn = _jax.value_and_grad(_loss, argnums=(0, 1))

    def one_microbatch(ex, loss_target):
        ex = dict(ex)
        diff = ex.pop(TWIN_DIFF_INPUT)
        return grad_fn(weights, diff, {**shared, **ex}, loss_target)

    if N_MICROBATCH == 1:
        loss, (grad_w, grad_x) = one_microbatch(per_example, given["loss_target"])
    else:
        def body(carry, xs):
            loss_sum, grad_sum = carry
            l_k, (gw_k, gx_k) = one_microbatch(xs[0], xs[1])
            with _jax.named_scope("update"):
                return (loss_sum + l_k, _jax.tree.map(_jnp.add, grad_sum, gw_k)), gx_k

        init = (_jnp.zeros((), _jnp.float32), _jax.tree.map(_jnp.zeros_like, weights))
        (loss, grad_w), grad_x = _jax.lax.scan(body, init, (per_example, given["loss_target"]))
    with _jax.named_scope("update"):
        delta_w, new_m, new_v = {}, {}, {}
        for n in TWIN_WEIGHTS:
            delta_w[n], new_m[n], new_v[n] = _adamw(weights[n], grad_w[n], given["m_" + n], given["v_" + n])
    return (loss, grad_x, *[grad_w[n] for n in TWIN_WEIGHTS], *[delta_w[n] for n in TWIN_WEIGHTS],
            *[new_m[n] for n in TWIN_WEIGHTS], *[new_v[n] for n in TWIN_WEIGHTS])
```

```python
import functools
import math

import jax
import jax.numpy as jnp
import numpy as np
from jax import lax
from jax.experimental import pallas as pl
from jax.experimental.pallas import tpu as pltpu

F32 = jnp.float32
BF16 = jnp.bfloat16

S = 2048
D = 1024
DEPTH = 4
N_CHIPS = 4
POOL_WINDOWS = (2, 4, 8, 16)
POOL_W = 512
LRU_W = 1024
LRU_HEADS = 8
HEAD = 128
EVEN_IN = 2560
EVEN_MIX = 1536
MLA_HEADS = 8
QK_NOPE = 128
QK_ROPE = 64
Q_RANK = 384
KV_RANK = 256
ODD_IN = 704
ODD_IN_PAD = 768
QHEAD_PAD = 256
D_FF = 4096
CHUNK = 64
ALPHA = (2 * DEPTH) ** 0.25
LN_EPS = 1e-5
RMS_EPS = 1e-6
ATT_SCALE = (QK_NOPE + QK_ROPE) ** -0.5
ROPE_THETA = 10000.0
LRU_C = 8.0
ADAM_LR = 0.001
ADAM_B1 = 0.9
ADAM_B2 = 0.999
ADAM_EPS = 1e-08
ADAM_WD = 0.01
ADAM_STEP = 10

VMEM_LIMIT = 56 * 1024 * 1024
MESH = pl.DeviceIdType.MESH

_NN = (((1,), (0,)), ((), ()))
_NT = (((1,), (1,)), ((), ()))
_TN = (((0,), (0,)), ((), ()))


def _params(sem=None, **kw):
    return pltpu.CompilerParams(dimension_semantics=sem, vmem_limit_bytes=VMEM_LIMIT, **kw)


def _dot(a, b, dims):
    return lax.dot_general(a.astype(BF16), b.astype(BF16), dims, preferred_element_type=F32)


def _mm(name, a, w4, layer, *, dims, a_block, a_map, w_map, grid, sem, outs, out_blocks, out_maps,
        red_axis=None, n_red=1, extras=(), extra_blocks=(), extra_maps=(), epilogue=None):
    n_extra = len(extras)
    n_out = len(outs)
    w_block = (None, None) + tuple(w4.shape[2:])

    def body(a_ref, w_ref, *rest):
        extra_refs = rest[:n_extra]
        out_refs = rest[n_extra:n_extra + n_out]
        p = _dot(a_ref[...], w_ref[...], dims)

        def finish(acc):
            vals = epilogue(acc, *[r[...] for r in extra_refs]) if epilogue else (acc,)
            for r, v in zip(out_refs, vals):
                r[...] = v.astype(r.dtype)

        if red_axis is None:
            finish(p)
        else:
            acc_ref = rest[-1]
            k = pl.program_id(red_axis)

            @pl.when(k == 0)
            def _():
                acc_ref[...] = p

            @pl.when(k > 0)
            def _():
                acc_ref[...] += p

            @pl.when(k == n_red - 1)
            def _():
                finish(acc_ref[...])

    scratch = [] if red_axis is None else [pltpu.VMEM(out_blocks[0], F32)]
    return pl.pallas_call(
        body, name=name, grid=grid,
        in_specs=[pl.BlockSpec(a_block, a_map), pl.BlockSpec(w_block, functools.partial(w_map, layer))]
        + [pl.BlockSpec(b, m) for b, m in zip(extra_blocks, extra_maps)],
        out_specs=[pl.BlockSpec(b, m) for b, m in zip(out_blocks, out_maps)],
        out_shape=outs, scratch_shapes=scratch, compiler_params=_params(sem),
    )(a, w4, *extras)


def _w_map_outer(layer, s, i):
    return (s, layer, 0, 0)


def _w_map_inner(layer, i, s):
    return (s, layer, 0, 0)


def mm_cols(name, a, w4, layer, out_dtypes, *, tm=512, epilogue=None):
    m, k = a.shape
    nb = w4.shape[3]
    outs = [jax.ShapeDtypeStruct((m, N_CHIPS * nb), dt) for dt in out_dtypes]
    return _mm(name, a, w4, layer, dims=_NN, a_block=(tm, k), a_map=lambda s, i: (i, 0), w_map=_w_map_outer,
               grid=(N_CHIPS, m // tm), sem=("parallel", "parallel"), outs=outs,
               out_blocks=[(tm, nb)] * len(outs), out_maps=[lambda s, i: (i, s)] * len(outs), epilogue=epilogue)


def mm_rows(name, a, w4, layer, out_dtype, *, tm=512):
    m = a.shape[0]
    kb, n = w4.shape[2:]
    outs = [jax.ShapeDtypeStruct((m, n), out_dtype)]
    return _mm(name, a, w4, layer, dims=_NN, a_block=(tm, kb), a_map=lambda i, s: (i, s), w_map=_w_map_inner,
               grid=(m // tm, N_CHIPS), sem=("parallel", "arbitrary"), outs=outs, out_blocks=[(tm, n)],
               out_maps=[lambda i, s: (i, 0)], red_axis=1, n_red=N_CHIPS)[0]


def mm_t_cols(name, g, w4, layer, out_dtype, *, tm=512, resid=None):
    m = g.shape[0]
    k, nb = w4.shape[2:]
    outs = [jax.ShapeDtypeStruct((m, k), out_dtype)]
    extras = () if resid is None else (resid,)
    epi = None if resid is None else (lambda acc, r: (acc + ALPHA * r,))
    return _mm(name, g, w4, layer, dims=_NT, a_block=(tm, nb), a_map=lambda i, s: (i, s), w_map=_w_map_inner,
               grid=(m // tm, N_CHIPS), sem=("parallel", "arbitrary"), outs=outs, out_blocks=[(tm, k)],
               out_maps=[lambda i, s: (i, 0)], red_axis=1, n_red=N_CHIPS, extras=extras,
               extra_blocks=[(tm, k)] * len(extras), extra_maps=[lambda i, s: (i, 0)] * len(extras), epilogue=epi)[0]


def mm_t_rows(name, g, w4, layer, out_dtype, *, tm=512, resid=None, gate=None):
    m, n = g.shape
    kb = w4.shape[2]
    outs = [jax.ShapeDtypeStruct((m, N_CHIPS * kb), out_dtype)]
    extras, epi = (), None
    if resid is not None:
        extras, epi = (resid,), (lambda acc, r: (acc + ALPHA * r,))
    if gate is not None:
        extras, epi = (gate,), (lambda acc, r: (acc * (2.0 * r.astype(F32)),))
    return _mm(name, g, w4, layer, dims=_NT, a_block=(tm, n), a_map=lambda s, i: (i, 0), w_map=_w_map_outer,
               grid=(N_CHIPS, m // tm), sem=("parallel", "parallel"), outs=outs, out_blocks=[(tm, kb)],
               out_maps=[lambda s, i: (i, s)], extras=extras, extra_blocks=[(tm, kb)] * len(extras),
               extra_maps=[lambda s, i: (i, s)] * len(extras), epilogue=epi)[0]


def mm_grad(name, a, g, layer, n_layers, prev, *, a_sharded, tm=512):
    m = a.shape[0]
    ka = a.shape[1] // N_CHIPS if a_sharded else a.shape[1]
    ng = g.shape[1] if a_sharded else g.shape[1] // N_CHIPS
    n_red = m // tm
    a_map = (lambda s, i: (i, s)) if a_sharded else (lambda s, i: (i, 0))
    g_map = (lambda s, i: (i, 0)) if a_sharded else (lambda s, i: (i, s))

    def body(a_ref, g_ref, *rest):
        out_ref = rest[-1]
        p = _dot(a_ref[...], g_ref[...], _TN)
        k = pl.program_id(1)

        @pl.when(k == 0)
        def _():
            out_ref[...] = p

        @pl.when(k > 0)
        def _():
            out_ref[...] += p

    in_specs = [pl.BlockSpec((tm, ka), a_map), pl.BlockSpec((tm, ng), g_map)]
    args = [a, g]
    aliases = {}
    if prev is not None:
        in_specs.append(pl.BlockSpec(memory_space=pl.ANY))
        args.append(prev)
        aliases = {2: 0}
    return pl.pallas_call(
        body, name=name, grid=(N_CHIPS, n_red), in_specs=in_specs,
        out_specs=pl.BlockSpec((None, None, ka, ng), lambda s, i: (s, layer, 0, 0)),
        out_shape=jax.ShapeDtypeStruct((N_CHIPS, n_layers, ka, ng), F32),
        input_output_aliases=aliases, compiler_params=_params(("parallel", "arbitrary")),
    )(*args)


def ln_fwd(name, x, s, g, b, *, tm=256):
    def body(x_ref, s_ref, g_ref, b_ref, y_ref, xhat_ref, rstd_ref):
        z = ALPHA * x_ref[...] + s_ref[...]
        mu = jnp.mean(z, axis=-1, keepdims=True)
        zc = z - mu
        var = jnp.mean(zc * zc, axis=-1, keepdims=True)
        rstd = lax.rsqrt(var + LN_EPS)
        xhat = zc * rstd
        y_ref[...] = xhat * g_ref[...] + b_ref[...]
        xhat_ref[...] = xhat
        rstd_ref[...] = rstd

    row = pl.BlockSpec((tm, D), lambda i: (i, 0))
    vec = pl.BlockSpec((1, D), lambda i: (0, 0))
    return pl.pallas_call(
        body, name=name, grid=(S // tm,), in_specs=[row, row, vec, vec],
        out_specs=[row, row, pl.BlockSpec((tm, 1), lambda i: (i, 0))],
        out_shape=[jax.ShapeDtypeStruct((S, D), F32), jax.ShapeDtypeStruct((S, D), F32),
                   jax.ShapeDtypeStruct((S, 1), F32)],
        compiler_params=_params(("parallel",)),
    )(x, s, g, b)


def ln_bwd(name, dy, xhat, rstd, g, *, tm=256):
    def body(dy_ref, xhat_ref, rstd_ref, g_ref, dz_ref, dg_ref, db_ref):
        dy_v = dy_ref[...]
        xh = xhat_ref[...]
        dxh = dy_v * g_ref[...]
        m1 = jnp.mean(dxh, axis=-1, keepdims=True)
        m2 = jnp.mean(dxh * xh, axis=-1, keepdims=True)
        dz_ref[...] = rstd_ref[...] * (dxh - m1 - xh * m2)
        pg = jnp.sum(dy_v * xh, axis=0, keepdims=True)
        pb = jnp.sum(dy_v, axis=0, keepdims=True)
        i = pl.program_id(0)

        @pl.when(i == 0)
        def _():
            dg_ref[...] = pg
            db_ref[...] = pb

        @pl.when(i > 0)
        def _():
            dg_ref[...] += pg
            db_ref[...] += pb

    row = pl.BlockSpec((tm, D), lambda i: (i, 0))
    vec = pl.BlockSpec((1, D), lambda i: (0, 0))
    return pl.pallas_call(
        body, name=name, grid=(S // tm,), in_specs=[row, row, pl.BlockSpec((tm, 1), lambda i: (i, 0)), vec],
        out_specs=[row, vec, vec],
        out_shape=[jax.ShapeDtypeStruct((S, D), F32), jax.ShapeDtypeStruct((1, D), F32),
                   jax.ShapeDtypeStruct((1, D), F32)],
        compiler_params=_params(("arbitrary",)),
    )(dy, xhat, rstd, g)


def loss_head(y, target, *, tm=256):
    def body(y_ref, t_ref, dy_ref, loss_ref):
        e = y_ref[...] - t_ref[...]
        dy_ref[...] = e * (1.0 / D)
        part = jnp.sum(jnp.sum(e * e, axis=-1, keepdims=True), axis=0, keepdims=True) * (0.5 / D)
        i = pl.program_id(0)

        @pl.when(i == 0)
        def _():
            loss_ref[...] = jnp.zeros_like(loss_ref)

        loss_ref[...] += jnp.broadcast_to(part, loss_ref.shape)

    row = pl.BlockSpec((tm, D), lambda i: (i, 0))
    return pl.pallas_call(
        body, name="loss_head", grid=(S // tm,), in_specs=[row, row],
        out_specs=[row, pl.BlockSpec((8, 128), lambda i: (0, 0))],
        out_shape=[jax.ShapeDtypeStruct((S, D), F32), jax.ShapeDtypeStruct((8, 128), F32)],
        compiler_params=_params(("arbitrary",)),
    )(y, target)


def _rows(shape):
    return lax.broadcasted_iota(jnp.int32, shape, 0)


def _shift_down(x, k):
    return jnp.where(_rows(x.shape) >= k, pltpu.roll(x, k, 0), 0.0)


def _shift_up(x, k):
    n = x.shape[0]
    return jnp.where(_rows(x.shape) < n - k, pltpu.roll(x, n - k, 0), 0.0)


def _pool_diff(u, w):
    acc, k = u, 1
    while k < w:
        acc = acc + _shift_down(acc, k)
        k *= 2
    cnt = jnp.minimum(_rows(u.shape) + 1, w).astype(F32)
    return acc / cnt - u, cnt


def pool_fwd(name, proj, pool_w, pool_scale):
    def body(u_ref, w_ref, sc_ref, y_ref):
        for g, w in enumerate(POOL_WINDOWS):
            cols = slice(g * HEAD, (g + 1) * HEAD)
            d, _ = _pool_diff(u_ref[:, cols], w)
            z = _dot(d, w_ref[g], _NN)
            y_ref[:, cols] = (z * sc_ref[:, cols]).astype(y_ref.dtype)

    return pl.pallas_call(
        body, name=name, grid=(1,),
        in_specs=[pl.BlockSpec((S, POOL_W), lambda i: (0, 0)),
                  pl.BlockSpec((4, HEAD, HEAD), lambda i: (0, 0, 0)),
                  pl.BlockSpec((1, POOL_W), lambda i: (0, 0))],
        out_specs=pl.BlockSpec((S, POOL_W), lambda i: (0, 0)),
        out_shape=jax.ShapeDtypeStruct((S, POOL_W), BF16),
        compiler_params=_params(("arbitrary",)),
    )(proj, pool_w, pool_scale)


def pool_bwd(name, proj, dycat, pool_w, pool_scale):
    def body(u_ref, dy_ref, w_ref, sc_ref, du_ref, dw_ref, dsc_ref):
        for g, w in enumerate(POOL_WINDOWS):
            cols = slice(g * HEAD, (g + 1) * HEAD)
            d, cnt = _pool_diff(u_ref[:, cols], w)
            dy = dy_ref[:, cols]
            z = _dot(d, w_ref[g], _NN)
            dsc_ref[:, cols] = jnp.sum(dy * z, axis=0, keepdims=True)
            dz = dy * sc_ref[:, cols]
            dw_ref[g] = _dot(d, dz, _TN)
            dd = _dot(dz, w_ref[g], _NT)
            acc, k = dd / cnt, 1
            while k < w:
                acc = acc + _shift_up(acc, k)
                k *= 2
            du_ref[:, cols] = (acc - dd).astype(du_ref.dtype)

    return pl.pallas_call(
        body, name=name, grid=(1,),
        in_specs=[pl.BlockSpec((S, POOL_W), lambda i: (0, 0)),
                  pl.BlockSpec((S, POOL_W), lambda i: (0, 0)),
                  pl.BlockSpec((4, HEAD, HEAD), lambda i: (0, 0, 0)),
                  pl.BlockSpec((1, POOL_W), lambda i: (0, 0))],
        out_specs=[pl.BlockSpec((S, POOL_W), lambda i: (0, 0)),
                   pl.BlockSpec((4, HEAD, HEAD), lambda i: (0, 0, 0)),
                   pl.BlockSpec((1, POOL_W), lambda i: (0, 0))],
        out_shape=[jax.ShapeDtypeStruct((S, POOL_W), BF16), jax.ShapeDtypeStruct((4, HEAD, HEAD), F32),
                   jax.ShapeDtypeStruct((1, POOL_W), F32)],
        compiler_params=_params(("arbitrary",)),
    )(proj, dycat, pool_w, pool_scale)


def _expm1(x):
    series = x * (1.0 + x * (0.5 + x * (1.0 / 6.0 + x * (1.0 / 24.0 + x * (1.0 / 120.0)))))
    return jnp.where(jnp.abs(x) < 0.05, series, jnp.exp(x) - 1.0)


def _softplus_neg(lam):
    e = jnp.exp(-jnp.abs(lam))
    log1p = jnp.where(e < 0.01, e * (1.0 - e * (0.5 - e * (1.0 / 3.0))), jnp.log(1.0 + e))
    return jnp.maximum(-lam, 0.0) + log1p


_GELU_C = math.sqrt(2.0 / math.pi)


def _gelu(x):
    t = jnp.tanh(_GELU_C * (x + 0.044715 * x * x * x))
    return 0.5 * x * (1.0 + t), t


def _gelu_grad(x, t):
    return 0.5 * (1.0 + t) + 0.5 * x * (1.0 - t * t) * _GELU_C * (1.0 + 3.0 * 0.044715 * x * x)


def _conv(u, cw, cb):
    return cw[3:4] * u + cw[2:3] * _shift_down(u, 1) + cw[1:2] * _shift_down(u, 2) + cw[0:1] * _shift_down(u, 3) + cb


def _lru_gates(cu, wa, ba, wx, bx, lam):
    r = jax.nn.sigmoid(_dot(cu, wa, _NN) + ba)
    i = jax.nn.sigmoid(_dot(cu, wx, _NN) + bx)
    sp = _softplus_neg(lam)
    log_a = (-LRU_C) * r * sp
    a = jnp.exp(log_a)
    mult = jnp.sqrt(-_expm1(2.0 * log_a))
    return r, i, sp, a, mult


def _scan(a_ref, b_ref, h_ref, *, reverse):
    n_blk = S // 8
    row8 = lax.broadcasted_iota(jnp.int32, (8, HEAD), 0)

    def step(j, carry):
        blk = (n_blk - 1 - j) if reverse else j
        r0 = pl.multiple_of(blk * 8, 8)
        a = a_ref[pl.ds(r0, 8), :]
        b = b_ref[pl.ds(r0, 8), :]
        for k in (1, 2, 4):
            if reverse:
                keep = row8 < 8 - k
                a_s, b_s = pltpu.roll(a, 8 - k, 0), pltpu.roll(b, 8 - k, 0)
            else:
                keep = row8 >= k
                a_s, b_s = pltpu.roll(a, k, 0), pltpu.roll(b, k, 0)
            b = jnp.where(keep, a * b_s + b, b)
            a = jnp.where(keep, a * a_s, a)
        h = b + a * carry
        h_ref[pl.ds(r0, 8), :] = h
        edge = h[0:1, :] if reverse else h[7:8, :]
        return jnp.broadcast_to(edge, (8, HEAD))

    lax.fori_loop(0, n_blk, step, jnp.zeros((8, HEAD), F32), unroll=4)


def _lru_specs():
    def col(off):
        return pl.BlockSpec((S, HEAD), lambda h: (0, off + h))
    vec = pl.BlockSpec((1, HEAD), lambda h: (0, h))
    mat = pl.BlockSpec((None, HEAD, HEAD), lambda h: (h, 0, 0))
    cw = pl.BlockSpec((4, HEAD), lambda h: (0, h))
    return col, vec, mat, cw


def lru_fwd(name, proj, conv_w, conv_b, w_a, b_a, w_x, b_x, lam):
    def body(u_ref, ug_ref, cw_ref, cb_ref, wa_ref, ba_ref, wx_ref, bx_ref, lam_ref, y_ref, h_ref, a_s, b_s):
        cu = _conv(u_ref[...], cw_ref[...], cb_ref[...])
        _, i, _, a, mult = _lru_gates(cu, wa_ref[...], ba_ref[...], wx_ref[...], bx_ref[...], lam_ref[...])
        a_s[...] = a
        b_s[...] = mult * (i * cu)
        _scan(a_s, b_s, h_ref, reverse=False)
        gl, _ = _gelu(ug_ref[...])
        y_ref[...] = (h_ref[...] * gl).astype(y_ref.dtype)

    col, vec, mat, cw = _lru_specs()
    out = pl.BlockSpec((S, HEAD), lambda h: (0, h))
    return pl.pallas_call(
        body, name=name, grid=(LRU_HEADS,),
        in_specs=[col(4), col(12), cw, vec, mat, vec, mat, vec, vec],
        out_specs=[out, out],
        out_shape=[jax.ShapeDtypeStruct((S, LRU_W), BF16), jax.ShapeDtypeStruct((S, LRU_W), F32)],
        scratch_shapes=[pltpu.VMEM((S, HEAD), F32), pltpu.VMEM((S, HEAD), F32)],
        compiler_params=_params(("parallel",)),
    )(proj, proj, conv_w, conv_b, w_a, b_a, w_x, b_x, lam)


def lru_bwd(name, proj, hstate, dycat, conv_w, conv_b, w_a, b_a, w_x, b_x, lam):
    def body(u_ref, ug_ref, h_ref, dy_ref, cw_ref, cb_ref, wa_ref, ba_ref, wx_ref, bx_ref, lam_ref,
             du_ref, dug_ref, dwa_ref, dwx_ref, dba_ref, dbx_ref, dlam_ref, dcw_ref, dcb_ref, a_s, b_s, g_s):
        u = u_ref[...]
        cw = cw_ref[...]
        cu = _conv(u, cw, cb_ref[...])
        lam_v = lam_ref[...]
        r, i, sp, a, mult = _lru_gates(cu, wa_ref[...], ba_ref[...], wx_ref[...], bx_ref[...], lam_v)
        ug = ug_ref[...]
        gl, t = _gelu(ug)
        dy = dy_ref[...]
        h = h_ref[...]
        dug_ref[...] = (dy * h * _gelu_grad(ug, t)).astype(dug_ref.dtype)
        a_s[...] = _shift_up(a, 1)
        b_s[...] = dy * gl
        _scan(a_s, b_s, g_s, reverse=True)
        dxin = g_s[...]
        da = dxin * _shift_down(h, 1)
        dmult = dxin * (i * cu)
        di = dxin * (mult * cu)
        dlog_a = da * a - dmult * (a * a) / mult
        dr_pre = dlog_a * ((-LRU_C) * sp) * (r * (1.0 - r))
        di_pre = di * (i * (1.0 - i))
        dsp = jnp.sum(dlog_a * ((-LRU_C) * r), axis=0, keepdims=True)
        dlam_ref[...] = dsp * (-jax.nn.sigmoid(-lam_v))
        dba_ref[...] = jnp.sum(dr_pre, axis=0, keepdims=True)
        dbx_ref[...] = jnp.sum(di_pre, axis=0, keepdims=True)
        dwa_ref[...] = _dot(cu, dr_pre, _TN)
        dwx_ref[...] = _dot(cu, di_pre, _TN)
        dcu = dxin * (mult * i) + _dot(dr_pre, wa_ref[...], _NT) + _dot(di_pre, wx_ref[...], _NT)
        dcb_ref[...] = jnp.sum(dcu, axis=0, keepdims=True)
        for k in range(4):
            dcw_ref[k:k + 1, :] = jnp.sum(dcu * (_shift_down(u, 3 - k) if k < 3 else u), axis=0, keepdims=True)
        du = cw[3:4] * dcu + cw[2:3] * _shift_up(dcu, 1) + cw[1:2] * _shift_up(dcu, 2) + cw[0:1] * _shift_up(dcu, 3)
        du_ref[...] = du.astype(du_ref.dtype)

    col, vec, mat, cw = _lru_specs()
    out = pl.BlockSpec((S, HEAD), lambda h: (0, h))
    big = jax.ShapeDtypeStruct((S, LRU_W), BF16)
    vec_shape = jax.ShapeDtypeStruct((1, LRU_W), F32)
    mat_shape = jax.ShapeDtypeStruct((LRU_HEADS, HEAD, HEAD), F32)
    return pl.pallas_call(
        body, name=name, grid=(LRU_HEADS,),
        in_specs=[col(4), col(12), out, col(4), cw, vec, mat, vec, mat, vec, vec],
        out_specs=[out, out, mat, mat, vec, vec, vec, cw, vec],
        out_shape=[big, big, mat_shape, mat_shape, vec_shape, vec_shape, vec_shape,
                   jax.ShapeDtypeStruct((4, LRU_W), F32), vec_shape],
        scratch_shapes=[pltpu.VMEM((S, HEAD), F32)] * 3,
        compiler_params=_params(("parallel",)),
    )(proj, proj, hstate, dycat, conv_w, conv_b, w_a, b_a, w_x, b_x, lam)


def rope_tables(pos_col, inv_freq):
    def body(pos_ref, f_ref, c_ref, s1_ref, s2_ref):
        ang = pos_ref[...].astype(F32) * f_ref[...]
        lane = lax.broadcasted_iota(jnp.int32, ang.shape, 1)
        cos, sin = jnp.cos(ang), jnp.sin(ang)
        c_ref[...] = jnp.where(lane < QK_ROPE, cos, 0.0)
        s1_ref[...] = jnp.where(lane < QK_ROPE // 2, -sin, 0.0)
        s2_ref[...] = jnp.where((lane >= QK_ROPE // 2) & (lane < QK_ROPE), sin, 0.0)

    tab = jax.ShapeDtypeStruct((S, HEAD), F32)
    return pl.pallas_call(
        body, name="rope_tables", grid=(1,),
        in_specs=[pl.BlockSpec((S, 1), lambda i: (0, 0)), pl.BlockSpec((1, HEAD), lambda i: (0, 0))],
        out_specs=[pl.BlockSpec((S, HEAD), lambda i: (0, 0))] * 3, out_shape=[tab, tab, tab],
        compiler_params=_params(("arbitrary",)),
    )(pos_col, inv_freq)


def _rope(v, c, s1, s2):
    return v * c + pltpu.roll(v, HEAD - QK_ROPE // 2, 1) * s1 + pltpu.roll(v, QK_ROPE // 2, 1) * s2


def _unrope(d, c, s1, s2):
    return d * c + pltpu.roll(d * s1, QK_ROPE // 2, 1) + pltpu.roll(d * s2, HEAD - QK_ROPE // 2, 1)


def _rms(x, g):
    rstd = lax.rsqrt(jnp.mean(x * x, axis=-1, keepdims=True) + RMS_EPS)
    return x * rstd, rstd


def mla_prep(name, down, gq, gkv, tabs, *, tm=256):
    def body(dn_ref, gq_ref, gkv_ref, c_ref, s1_ref, s2_ref, cq_ref, ckv_ref, kp_ref):
        xq, _ = _rms(dn_ref[:, :Q_RANK], None)
        cq_ref[...] = (xq * gq_ref[...]).astype(cq_ref.dtype)
        xkv, _ = _rms(dn_ref[:, Q_RANK:Q_RANK + KV_RANK], None)
        ckv_ref[...] = (xkv * gkv_ref[...]).astype(ckv_ref.dtype)
        kp = _rope(dn_ref[:, Q_RANK + KV_RANK:], c_ref[...], s1_ref[...], s2_ref[...])
        kp_ref[...] = kp.astype(kp_ref.dtype)

    tab = pl.BlockSpec((tm, HEAD), lambda i: (i, 0))
    return pl.pallas_call(
        body, name=name, grid=(S // tm,),
        in_specs=[pl.BlockSpec((tm, ODD_IN_PAD), lambda i: (i, 0)), pl.BlockSpec((1, Q_RANK), lambda i: (0, 0)),
                  pl.BlockSpec((1, KV_RANK), lambda i: (0, 0)), tab, tab, tab],
        out_specs=[pl.BlockSpec((tm, Q_RANK), lambda i: (i, 0)), pl.BlockSpec((tm, KV_RANK), lambda i: (i, 0)), tab],
        out_shape=[jax.ShapeDtypeStruct((S, Q_RANK), BF16), jax.ShapeDtypeStruct((S, KV_RANK), BF16),
                   jax.ShapeDtypeStruct((S, HEAD), BF16)],
        compiler_params=_params(("parallel",)),
    )(down, gq, gkv, *tabs)


def mla_prep_bwd(name, down, dcq, dckv, dkp, gq, gkv, tabs, *, tm=256):
    def body(dn_ref, dcq_ref, dckv_ref, dkp_ref, gq_ref, gkv_ref, c_ref, s1_ref, s2_ref, dd_ref, dgq_ref, dgkv_ref):
        i = pl.program_id(0)

        def rms_bwd(x, dy, g, dg_ref):
            xh, rstd = _rms(x, None)
            dxh = dy * g
            dx = rstd * (dxh - xh * jnp.mean(dxh * xh, axis=-1, keepdims=True))
            pg = jnp.sum(dy * xh, axis=0, keepdims=True)

            @pl.when(i == 0)
            def _():
                dg_ref[...] = pg

            @pl.when(i > 0)
            def _():
                dg_ref[...] += pg

            return dx

        dxq = rms_bwd(dn_ref[:, :Q_RANK], dcq_ref[...], gq_ref[...], dgq_ref)
        dd_ref[:, :Q_RANK] = dxq.astype(dd_ref.dtype)
        dxkv = rms_bwd(dn_ref[:, Q_RANK:Q_RANK + KV_RANK], dckv_ref[...], gkv_ref[...], dgkv_ref)
        dd_ref[:, Q_RANK:Q_RANK + KV_RANK] = dxkv.astype(dd_ref.dtype)
        dd_ref[:, Q_RANK + KV_RANK:] = _unrope(dkp_ref[...], c_ref[...], s1_ref[...], s2_ref[...]).astype(dd_ref.dtype)

    tab = pl.BlockSpec((tm, HEAD), lambda i: (i, 0))
    vq = pl.BlockSpec((1, Q_RANK), lambda i: (0, 0))
    vkv = pl.BlockSpec((1, KV_RANK), lambda i: (0, 0))
    return pl.pallas_call(
        body, name=name, grid=(S // tm,),
        in_specs=[pl.BlockSpec((tm, ODD_IN_PAD), lambda i: (i, 0)), pl.BlockSpec((tm, Q_RANK), lambda i: (i, 0)),
                  pl.BlockSpec((tm, KV_RANK), lambda i: (i, 0)), tab, vq, vkv, tab, tab, tab],
        out_specs=[pl.BlockSpec((tm, ODD_IN_PAD), lambda i: (i, 0)), vq, vkv],
        out_shape=[jax.ShapeDtypeStruct((S, ODD_IN_PAD), BF16), jax.ShapeDtypeStruct((1, Q_RANK), F32),
                   jax.ShapeDtypeStruct((1, KV_RANK), F32)],
        compiler_params=_params(("arbitrary",)),
    )(down, dcq, dckv, dkp, gq, gkv, *tabs)


def _attn_probs(q_ref, kv_ref, kp_ref, c_ref, s1_ref, s2_ref, i, tq):
    qn = q_ref[:, :HEAD]
    qp = _rope(q_ref[:, HEAD:], c_ref[...], s1_ref[...], s2_ref[...])
    kn = kv_ref[:, :HEAD]
    sc = (_dot(qn, kn, _NT) + _dot(qp, kp_ref[...], _NT)) * ATT_SCALE
    q_chunk = (i * tq + lax.broadcasted_iota(jnp.int32, sc.shape, 0)) // CHUNK
    k_chunk = lax.broadcasted_iota(jnp.int32, sc.shape, 1) // CHUNK
    sc = jnp.where(k_chunk <= q_chunk, sc, jnp.finfo(F32).min)
    e = jnp.exp(sc - jnp.max(sc, axis=-1, keepdims=True))
    p = e * (1.0 / jnp.sum(e, axis=-1, keepdims=True))
    return p, qn, qp, kn


def _attn_specs(tq):
    q = pl.BlockSpec((tq, QHEAD_PAD), lambda h, i: (i, h))
    kv = pl.BlockSpec((S, QHEAD_PAD), lambda h, i: (0, h))
    kp = pl.BlockSpec((S, HEAD), lambda h, i: (0, 0))
    tab = pl.BlockSpec((tq, HEAD), lambda h, i: (i, 0))
    o = pl.BlockSpec((tq, HEAD), lambda h, i: (i, h))
    return q, kv, kp, tab, o


def attn_fwd(name, q, kv, kp, tabs, *, tq=256):
    def body(q_ref, kv_ref, kp_ref, c_ref, s1_ref, s2_ref, o_ref):
        p, _, _, _ = _attn_probs(q_ref, kv_ref, kp_ref, c_ref, s1_ref, s2_ref, pl.program_id(1), tq)
        o_ref[...] = _dot(p, kv_ref[:, HEAD:], _NN).astype(o_ref.dtype)

    qs, kvs, kps, tab, os = _attn_specs(tq)
    return pl.pallas_call(
        body, name=name, grid=(MLA_HEADS, S // tq), in_specs=[qs, kvs, kps, tab, tab, tab], out_specs=os,
        out_shape=jax.ShapeDtypeStruct((S, MLA_HEADS * HEAD), BF16),
        compiler_params=_params(("parallel", "parallel")),
    )(q, kv, kp, *tabs)


def attn_bwd(name, q, kv, kp, do, tabs, *, tq=256):
    def body(q_ref, kv_ref, kp_ref, do_ref, c_ref, s1_ref, s2_ref, dq_ref, dkv_ref, dkp_ref):
        h, i = pl.program_id(0), pl.program_id(1)
        p, qn, qp, kn = _attn_probs(q_ref, kv_ref, kp_ref, c_ref, s1_ref, s2_ref, i, tq)
        do_v = do_ref[...]
        dp = _dot(do_v, kv_ref[:, HEAD:], _NT)
        ds = p * (dp - jnp.sum(p * dp, axis=-1, keepdims=True)) * ATT_SCALE
        ds = ds.astype(BF16)
        dq_ref[:, :HEAD] = _dot(ds, kn, _NN).astype(dq_ref.dtype)
        dqp = _unrope(_dot(ds, kp_ref[...], _NN), c_ref[...], s1_ref[...], s2_ref[...])
        dq_ref[:, HEAD:] = dqp.astype(dq_ref.dtype)
        dkn = _dot(ds, qn, _TN)
        dv = _dot(p, do_v, _TN)
        dkp = _dot(ds, qp, _TN)

        @pl.when(i == 0)
        def _():
            dkv_ref[:, :HEAD] = dkn
            dkv_ref[:, HEAD:] = dv

        @pl.when(i > 0)
        def _():
            dkv_ref[:, :HEAD] += dkn
            dkv_ref[:, HEAD:] += dv

        @pl.when((i == 0) & (h == 0))
        def _():
            dkp_ref[...] = dkp

        @pl.when((i > 0) | (h > 0))
        def _():
            dkp_ref[...] += dkp

    qs, kvs, kps, tab, os = _attn_specs(tq)
    return pl.pallas_call(
        body, name=name, grid=(MLA_HEADS, S // tq), in_specs=[qs, kvs, kps, os, tab, tab, tab],
        out_specs=[qs, kvs, kps],
        out_shape=[jax.ShapeDtypeStruct((S, MLA_HEADS * QHEAD_PAD), BF16),
                   jax.ShapeDtypeStruct((S, MLA_HEADS * QHEAD_PAD), F32), jax.ShapeDtypeStruct((S, HEAD), F32)],
        compiler_params=_params(("arbitrary", "arbitrary")),
    )(q, kv, kp, do, *tabs)


def adamw(name, w, g, m, v):
    rows, cols = w.shape
    tr = rows
    for cand in (512, 256, 128, 64, 32, 16, 8):
        if rows % cand == 0 and cand * cols * 4 <= 2 * 1024 * 1024:
            tr = cand
            break

    def body(w_ref, g_ref, m_ref, v_ref, d_ref, nm_ref, nv_ref):
        g_v = g_ref[...]
        nm = ADAM_B1 * m_ref[...] + (1.0 - ADAM_B1) * g_v
        nv = ADAM_B2 * v_ref[...] + (1.0 - ADAM_B2) * (g_v * g_v)
        m_hat = nm / (1.0 - ADAM_B1 ** ADAM_STEP)
        v_hat = nv / (1.0 - ADAM_B2 ** ADAM_STEP)
        d_ref[...] = (-ADAM_LR) * (m_hat / (jnp.sqrt(v_hat) + ADAM_EPS) + ADAM_WD * w_ref[...])
        nm_ref[...] = nm
        nv_ref[...] = nv

    blk = pl.BlockSpec((tr, cols), lambda i: (i, 0))
    shape = jax.ShapeDtypeStruct((rows, cols), F32)
    return pl.pallas_call(
        body, name=name, grid=(rows // tr,), in_specs=[blk] * 4, out_specs=[blk] * 3, out_shape=[shape] * 3,
        compiler_params=_params(("parallel",)),
    )(w, g, m, v)


def _local_step(x, pos_col, target, wg, sm):
    inv_freq = ROPE_THETA ** (-jnp.arange(0, QK_ROPE, 2, dtype=F32) / QK_ROPE)
    inv_freq = jnp.concatenate([inv_freq, inv_freq, jnp.zeros((HEAD - QK_ROPE,), F32)])[None, :]
    tabs = rope_tables(pos_col, inv_freq)
    saved = []
    for layer in range(DEPTH):
        j = layer // 2
        n = "l%d_" % layer
        sv = {"x": x}
        if layer % 2 == 0:
            proj = mm_cols(n + "proj", x, wg["w_in"], j, [F32])[0]
            y_pool = pool_fwd(n + "pool", proj, sm["pool_w"][j], sm["pool_scale"][j][None])
            y_lru, hstate = lru_fwd(n + "lru", proj, sm["conv_w"][j], sm["conv_b"][j][None], sm["w_a"][j],
                                    sm["b_a"][j][None], sm["w_x"][j], sm["b_x"][j][None], sm["lam"][j][None])
            ycat = jnp.concatenate([y_pool, y_lru], axis=1)
            mix = mm_rows(n + "mixout", ycat, wg["w_out"], j, F32)
            sv.update(proj=proj, hstate=hstate, ycat=ycat)
        else:
            down = mm_rows(n + "down", x, wg["w_down"], j, F32)
            cq, ckv, kp = mla_prep(n + "prep", down, sm["gq"][j][None], sm["gkv"][j][None], tabs)
            q = mm_cols(n + "q", cq, wg["w_qb"], j, [F32])[0]
            kv = mm_cols(n + "kv", ckv, wg["w_kvb"], j, [BF16])[0]
            o = attn_fwd(n + "attn", q, kv, kp, tabs)
            mix = mm_rows(n + "attnout", o, wg["w_o"], j, F32)
            sv.update(down=down, cq=cq, ckv=ckv, kp=kp, q=q, kv=kv, o=o)
        x1, xhat1, rstd1 = ln_fwd(n + "ln_mix", x, mix, sm["ln_mix_g"][layer][None], sm["ln_mix_b"][layer][None])
        act, relu = mm_cols(n + "mlp1", x1, wg["w1"], layer, [BF16, BF16],
                            epilogue=lambda acc: (jnp.square(jnp.maximum(acc, 0.0)), jnp.maximum(acc, 0.0)))
        mlp = mm_rows(n + "mlp2", act, wg["w2"], layer, F32)
        x2, xhat2, rstd2 = ln_fwd(n + "ln_ffn", x1, mlp, sm["ln_ffn_g"][layer][None], sm["ln_ffn_b"][layer][None])
        sv.update(xhat1=xhat1, rstd1=rstd1, x1=x1, act=act, relu=relu, xhat2=xhat2, rstd2=rstd2)
        saved.append(sv)
        x = x2

    dx, loss_tile = loss_head(x, target)
    gw = {k: None for k in wg}
    gs = {k: [None] * (DEPTH if k.startswith("ln_") else DEPTH // 2) for k in sm}
    for layer in reversed(range(DEPTH)):
        j = layer // 2
        n = "l%d_" % layer
        sv = saved[layer]
        dz, gs["ln_ffn_g"][layer], gs["ln_ffn_b"][layer] = ln_bwd(
            n + "ln_ffn_b", dx, sv["xhat2"], sv["rstd2"], sm["ln_ffn_g"][layer][None])
        dh = mm_t_rows(n + "mlp2_dx", dz, wg["w2"], layer, BF16, gate=sv["relu"])
        gw["w2"] = mm_grad(n + "mlp2_dw", sv["act"], dz, layer, DEPTH, gw["w2"], a_sharded=True)
        dx = mm_t_cols(n + "mlp1_dx", dh, wg["w1"], layer, F32, resid=dz)
        gw["w1"] = mm_grad(n + "mlp1_dw", sv["x1"], dh, layer, DEPTH, gw["w1"], a_sharded=False)
        dz, gs["ln_mix_g"][layer], gs["ln_mix_b"][layer] = ln_bwd(
            n + "ln_mix_b", dx, sv["xhat1"], sv["rstd1"], sm["ln_mix_g"][layer][None])
        if layer % 2 == 0:
            dycat = mm_t_rows(n + "mixout_dx", dz, wg["w_out"], j, F32)
            gw["w_out"] = mm_grad(n + "mixout_dw", sv["ycat"], dz, j, DEPTH // 2, gw["w_out"], a_sharded=True)
            du_pool, gs["pool_w"][j], gs["pool_scale"][j] = pool_bwd(
                n + "pool_b", sv["proj"], dycat, sm["pool_w"][j], sm["pool_scale"][j][None])
            (du_lru, du_gate, gs["w_a"][j], gs["w_x"][j], gs["b_a"][j], gs["b_x"][j], gs["lam"][j], gs["conv_w"][j],
             gs["conv_b"][j]) = lru_bwd(n + "lru_b", sv["proj"], sv["hstate"], dycat, sm["conv_w"][j],
                                        sm["conv_b"][j][None], sm["w_a"][j], sm["b_a"][j][None], sm["w_x"][j],
                                        sm["b_x"][j][None], sm["lam"][j][None])
            dproj = jnp.concatenate([du_pool, du_lru, du_gate], axis=1)
            gw["w_in"] = mm_grad(n + "proj_dw", sv["x"], dproj, j, DEPTH // 2, gw["w_in"], a_sharded=False)
            dx = mm_t_cols(n + "proj_dx", dproj, wg["w_in"], j, F32, resid=dz)
        else:
            do = mm_t_rows(n + "attnout_dx", dz, wg["w_o"], j, BF16)
            gw["w_o"] = mm_grad(n + "attnout_dw", sv["o"], dz, j, DEPTH // 2, gw["w_o"], a_sharded=True)
            dq, dkv, dkp = attn_bwd(n + "attn_b", sv["q"], sv["kv"], sv["kp"], do, tabs)
            gw["w_qb"] = mm_grad(n + "q_dw", sv["cq"], dq, j, DEPTH // 2, gw["w_qb"], a_sharded=False)
            dcq = mm_t_cols(n + "q_dx", dq, wg["w_qb"], j, F32)
            gw["w_kvb"] = mm_grad(n + "kv_dw", sv["ckv"], dkv, j, DEPTH // 2, gw["w_kvb"], a_sharded=False)
            dckv = mm_t_cols(n + "kv_dx", dkv, wg["w_kvb"], j, F32)
            ddown, gs["gq"][j], gs["gkv"][j] = mla_prep_bwd(
                n + "prep_b", sv["down"], dcq, dckv, dkp, sm["gq"][j][None], sm["gkv"][j][None], tabs)
            gw["w_down"] = mm_grad(n + "down_dw", sv["x"], ddown, j, DEPTH // 2, gw["w_down"], a_sharded=True)
            dx = mm_t_rows(n + "down_dx", ddown, wg["w_down"], j, F32, resid=dz)
    gs = {k: jnp.stack([a.reshape(sm[k].shape[1:]) for a in v]) for k, v in gs.items()}
    return loss_tile[0, 0], dx, gw, gs


def _place():
    x, y, c = lax.axis_index("x"), lax.axis_index("y"), lax.axis_index("c")
    chips = [(1 - x, y), (x, 1 - y), (1 - x, 1 - y)]
    return x, y, c, chips


def _hbm_call(body, name, args, out_shape, scratch, aliases=None):
    return pl.pallas_call(
        body, name=name, in_specs=[pl.BlockSpec(memory_space=pl.ANY)] * len(args),
        out_specs=[pl.BlockSpec(memory_space=pl.ANY)] * len(out_shape), out_shape=out_shape,
        scratch_shapes=scratch, input_output_aliases=aliases or {},
        compiler_params=pltpu.CompilerParams(has_side_effects=True),
    )(*args)


def all_gather_shards(arrs):
    n = len(arrs)

    def body(*refs):
        ins, outs = refs[:n], refs[n:2 * n]
        send_sems, recv_sems, local_sems = refs[2 * n:]
        x, y, c, chips = _place()
        me = 2 * x + y
        sibling = (x, y, 1 - c)

        def halves(k):
            hl = arrs[k].shape[0] // 2
            return pl.ds(c * hl, hl), pl.ds((1 - c) * hl, hl)

        def direct(k, j, to_chip, from_chip):
            mine, _ = halves(k)
            return pltpu.make_async_remote_copy(
                src_ref=ins[k].at[mine], dst_ref=outs[k].at[from_chip, mine], send_sem=send_sems.at[k, j],
                recv_sem=recv_sems.at[k, j], device_id=(*to_chip, c), device_id_type=MESH)

        def passed(k, j, chip_idx, half):
            return pltpu.make_async_remote_copy(
                src_ref=outs[k].at[chip_idx, half], dst_ref=outs[k].at[chip_idx, half], send_sem=send_sems.at[k, 3 + j],
                recv_sem=recv_sems.at[k, 3 + j], device_id=sibling, device_id_type=MESH)

        local = [pltpu.make_async_copy(ins[k], outs[k].at[me], local_sems.at[k]) for k in range(n)]
        for cp in local:
            cp.start()
        sends = [direct(k, j, chip, me) for k in range(n) for j, chip in enumerate(chips)]
        for cp in sends:
            cp.start()
        for k in range(n):
            mine, _ = halves(k)
            for j, (px, py) in enumerate(chips):
                direct(k, j, (px, py), 2 * px + py).wait_recv()
                fwd = passed(k, j, 2 * px + py, mine)
                fwd.start()
                sends.append(fwd)
        for k in range(n):
            _, other = halves(k)
            for j, (px, py) in enumerate(chips):
                passed(k, j, 2 * px + py, other).wait_recv()
        for cp in sends:
            cp.wait_send()
        for cp in local:
            cp.wait()

    out_shape = [jax.ShapeDtypeStruct((N_CHIPS,) + a.shape, a.dtype) for a in arrs]
    scratch = [pltpu.SemaphoreType.DMA((n, 6)), pltpu.SemaphoreType.DMA((n, 6)), pltpu.SemaphoreType.DMA((n,))]
    return _hbm_call(body, "all_gather_shards", arrs, out_shape, scratch)


def sibling_send_halves(grads):
    n = len(grads)

    def body(*refs):
        ins, outs = refs[:n], refs[n:2 * n]
        send_sems, recv_sems = refs[2 * n:]
        x, y, c, _ = _place()
        copies = []
        for k in range(n):
            nd, nl = grads[k].shape[:2]
            hl = nl // 2
            copies.append(pltpu.make_async_remote_copy(
                src_ref=ins[k].at[pl.ds(0, nd), pl.ds((1 - c) * hl, hl)], dst_ref=outs[k], send_sem=send_sems.at[k],
                recv_sem=recv_sems.at[k], device_id=(x, y, 1 - c), device_id_type=MESH))
        for cp in copies:
            cp.start()
        for cp in copies:
            cp.wait()

    out_shape = [jax.ShapeDtypeStruct((g.shape[0], g.shape[1] // 2) + g.shape[2:], g.dtype) for g in grads]
    scratch = [pltpu.SemaphoreType.DMA((n,)), pltpu.SemaphoreType.DMA((n,))]
    return _hbm_call(body, "sibling_send_halves", grads, out_shape, scratch)


def chip_exchange(parts):
    n = len(parts)

    def body(*refs):
        ins, outs = refs[:n], refs[n:2 * n]
        send_sems, recv_sems, local_sems = refs[2 * n:]
        x, y, c, chips = _place()
        me = 2 * x + y

        def slot(k, chip_idx):
            return ins[k].at[chip_idx] if parts[k].shape[0] == N_CHIPS else ins[k].at[0]

        def copy(k, j, to_chip, from_chip):
            return pltpu.make_async_remote_copy(
                src_ref=slot(k, 2 * to_chip[0] + to_chip[1]), dst_ref=outs[k].at[from_chip], send_sem=send_sems.at[k, j],
                recv_sem=recv_sems.at[k, j], device_id=(*to_chip, c), device_id_type=MESH)

        local = [pltpu.make_async_copy(slot(k, me), outs[k].at[me], local_sems.at[k]) for k in range(n)]
        for cp in local:
            cp.start()
        sends = [copy(k, j, chip, me) for k in range(n) for j, chip in enumerate(chips)]
        for cp in sends:
            cp.start()
        for k in range(n):
            for j, (px, py) in enumerate(chips):
                copy(k, j, (px, py), 2 * px + py).wait_recv()
        for cp in sends:
            cp.wait_send()
        for cp in local:
            cp.wait()

    out_shape = [jax.ShapeDtypeStruct((N_CHIPS,) + p.shape[1:], p.dtype) for p in parts]
    scratch = [pltpu.SemaphoreType.DMA((n, 3)), pltpu.SemaphoreType.DMA((n, 3)), pltpu.SemaphoreType.DMA((n,))]
    return _hbm_call(body, "chip_exchange", parts, out_shape, scratch)


def sibling_swap_halves(fulls):
    n = len(fulls)

    def body(*refs):
        outs = refs[n:2 * n]
        send_sems, recv_sems = refs[2 * n:]
        x, y, c, _ = _place()
        copies = []
        for k in range(n):
            hl = fulls[k].shape[0] // 2
            mine = outs[k].at[pl.ds(c * hl, hl)]
            copies.append((
                pltpu.make_async_remote_copy(src_ref=mine, dst_ref=mine, send_sem=send_sems.at[k], recv_sem=recv_sems.at[k],
                                             device_id=(x, y, 1 - c), device_id_type=MESH),
                pltpu.make_async_remote_copy(src_ref=mine, dst_ref=outs[k].at[pl.ds((1 - c) * hl, hl)], send_sem=send_sems.at[k],
                                             recv_sem=recv_sems.at[k], device_id=(x, y, 1 - c), device_id_type=MESH)))
        for send, _ in copies:
            send.start()
        for send, recv in copies:
            send.wait_send()
            recv.wait_recv()

    out_shape = [jax.ShapeDtypeStruct(f.shape, f.dtype) for f in fulls]
    scratch = [pltpu.SemaphoreType.DMA((n,)), pltpu.SemaphoreType.DMA((n,))]
    return _hbm_call(body, "sibling_swap_halves", fulls, out_shape, scratch, aliases={k: k for k in range(n)})


def _tile_rows(rows, cols, budget_bytes):
    best = None
    for t in range(8, rows + 1, 8):
        if rows % t == 0 and t * cols * 4 <= budget_bytes:
            best = t
    assert best is not None, (rows, cols)
    return best


def add_own_half(name, grad, recv, c_arr):
    nd, nl, rows, cols = grad.shape
    hl = nl // 2
    tr = _tile_rows(rows, cols, 2 * 1024 * 1024)

    def body(c_ref, g_ref, r_ref, o_ref):
        o_ref[...] = g_ref[...] + r_ref[...]

    blk = (None, None, tr, cols)
    return pl.pallas_call(
        body, name=name, out_shape=jax.ShapeDtypeStruct(recv.shape, F32),
        grid_spec=pltpu.PrefetchScalarGridSpec(
            num_scalar_prefetch=1, grid=(nd, hl, rows // tr),
            in_specs=[pl.BlockSpec(blk, lambda d, l, r, c_ref: (d, c_ref[0] * hl + l, r, 0)),
                      pl.BlockSpec(blk, lambda d, l, r, c_ref: (d, l, r, 0))],
            out_specs=pl.BlockSpec(blk, lambda d, l, r, c_ref: (d, l, r, 0))),
        compiler_params=_params(("parallel", "parallel", "parallel")),
    )(c_arr, grad, recv)


def sum_chips(name, recv, c_arr):
    _, hl, rows, cols = recv.shape
    tr = _tile_rows(rows, cols, 1024 * 1024)

    def body(c_ref, r_ref, o_ref):
        o_ref[...] = ((r_ref[0] + r_ref[1]) + r_ref[2]) + r_ref[3]

    return pl.pallas_call(
        body, name=name, out_shape=jax.ShapeDtypeStruct((2 * hl, rows, cols), F32),
        grid_spec=pltpu.PrefetchScalarGridSpec(
            num_scalar_prefetch=1, grid=(hl, rows // tr),
            in_specs=[pl.BlockSpec((N_CHIPS, None, tr, cols), lambda l, r, c_ref: (0, l, r, 0))],
            out_specs=pl.BlockSpec((None, tr, cols), lambda l, r, c_ref: (c_ref[0] * hl + l, r, 0))),
        compiler_params=_params(("parallel", "parallel")),
    )(c_arr, recv)


def _pack(arrs, rows_multiple):
    flat = []
    for a in arrs:
        v = a.reshape(-1).astype(F32)
        flat.append(jnp.pad(v, (0, (-v.shape[0]) % HEAD)))
    v = jnp.concatenate(flat)
    v = jnp.pad(v, (0, (-v.shape[0]) % (HEAD * rows_multiple)))
    return v.reshape(-1, HEAD)


def _unpack(packed, shapes):
    flat = packed.reshape(-1)
    out, off = [], 0
    for shp in shapes:
        size = int(np.prod(shp))
        out.append(flat[off:off + size].reshape(shp))
        off += size + (-size) % HEAD
    return out


BIG = ["even_w_in", "even_w_out", "mla_w_down", "mla_w_qb", "mla_w_kvb", "mla_w_o", "mlp_w1", "mlp_w2"]
BIG_KEY = {"even_w_in": "w_in", "even_w_out": "w_out", "mla_w_down": "w_down", "mla_w_qb": "w_qb",
           "mla_w_kvb": "w_kvb", "mla_w_o": "w_o", "mlp_w1": "w1", "mlp_w2": "w2"}
SMALL_KEY = {"ln_mix_g": "ln_mix_g", "ln_mix_b": "ln_mix_b", "ln_ffn_g": "ln_ffn_g", "ln_ffn_b": "ln_ffn_b",
             "pool_w": "pool_w", "pool_scale": "pool_scale", "lru_conv_w": "conv_w", "lru_conv_b": "conv_b",
             "lru_w_a": "w_a", "lru_b_a": "b_a", "lru_w_x": "w_x", "lru_b_x": "b_x", "lru_lambda": "lam",
             "mla_q_norm_g": "gq", "mla_kv_norm_g": "gkv"}
SMALL = list(SMALL_KEY)
SMALL_SHARDED = ["lru_conv_w", "mla_q_norm_g", "mla_kv_norm_g"]
WEIGHTS = ["ln_mix_g", "ln_mix_b", "ln_ffn_g", "ln_ffn_b", "even_w_in", "pool_w", "pool_scale", "lru_conv_w",
           "lru_conv_b", "lru_w_a", "lru_b_a", "lru_w_x", "lru_b_x", "lru_lambda", "even_w_out", "mla_w_down",
           "mla_q_norm_g", "mla_kv_norm_g", "mla_w_qb", "mla_w_kvb", "mla_w_o", "mlp_w1", "mlp_w2"]


def _pad_q_heads(w):
    lead = w.shape[:-1]
    w = w.reshape(lead + (2, QK_NOPE + QK_ROPE))
    w = jnp.pad(w, ((0, 0),) * len(lead) + ((0, 0), (0, QHEAD_PAD - QK_NOPE - QK_ROPE)))
    return w.reshape(lead + (2 * QHEAD_PAD,))


def _unpad_q_heads(g):
    lead = g.shape[:-1]
    return g.reshape(lead + (2, QHEAD_PAD))[..., :QK_NOPE + QK_ROPE].reshape(lead + (2 * (QK_NOPE + QK_ROPE),))


def _step(x, positions, loss_target, w, m, v):
    cx, cy, cc = lax.axis_index("x"), lax.axis_index("y"), lax.axis_index("c")
    chip = 2 * cx + cy
    c_arr = jnp.reshape(cc, (1,)).astype(jnp.int32)

    small_shard_shapes = [w[k].shape for k in SMALL_SHARDED]
    small_pack = _pack([w[k] for k in SMALL_SHARDED], 16).reshape(2, -1, HEAD)
    gathered = all_gather_shards([w[k].astype(BF16) for k in BIG] + [small_pack])
    wg = {BIG_KEY[k]: g for k, g in zip(BIG, gathered)}
    wg["w_down"] = jnp.pad(wg["w_down"], ((0, 0),) * 3 + ((0, ODD_IN_PAD - ODD_IN),))
    wg["w_qb"] = _pad_q_heads(wg["w_qb"])
    per_chip = [_unpack(gathered[-1][s], small_shard_shapes) for s in range(N_CHIPS)]
    sm = {SMALL_KEY[k]: w[k] for k in SMALL if k not in SMALL_SHARDED}
    sm["conv_w"] = jnp.concatenate([p[0] for p in per_chip], axis=-1)
    sm["gq"] = jnp.concatenate([p[1] for p in per_chip], axis=-1)
    sm["gkv"] = jnp.concatenate([p[2] for p in per_chip], axis=-1)

    loss, grad_x, gw, gs = _local_step(x[0], positions.reshape(S, 1), loss_target[0], wg, sm)
    loss = lax.psum(loss, ("x", "y", "c"))

    small_shapes = [gs[SMALL_KEY[k]].shape for k in SMALL]
    gs_pack = _pack([gs[SMALL_KEY[k]] for k in SMALL], 16)
    grads = [gw[BIG_KEY[k]] for k in BIG] + [gs_pack.reshape(1, 2, -1, HEAD)]
    names = [BIG_KEY[k] for k in BIG] + ["small"]
    recv = sibling_send_halves(grads)
    parts = [add_own_half("add_" + n, g, r, c_arr) for n, g, r in zip(names, grads, recv)]
    landed = chip_exchange(parts)
    halves = [sum_chips("sum_" + n, r, c_arr) for n, r in zip(names, landed)]
    reduced = sibling_swap_halves(halves)
    g_big = dict(zip(BIG, reduced[:-1]))
    g_big["mla_w_down"] = g_big["mla_w_down"][..., :ODD_IN]
    g_big["mla_w_qb"] = _unpad_q_heads(g_big["mla_w_qb"])
    g_small = dict(zip(SMALL, _unpack(reduced[-1], small_shapes)))
    for k in SMALL_SHARDED:
        width = w[k].shape[-1]
        g_small[k] = lax.dynamic_slice_in_dim(g_small[k], chip * width, width, axis=-1)
    grad = {**g_big, **g_small}

    delta, new_m, new_v = {}, {}, {}
    for k in BIG:
        shp = w[k].shape
        view = lambda a: a.reshape(-1, shp[-1])
        d, nm, nv = adamw("adamw_" + BIG_KEY[k], view(w[k]), view(grad[k]), view(m[k]), view(v[k]))
        delta[k], new_m[k], new_v[k] = d.reshape(shp), nm.reshape(shp), nv.reshape(shp)
    shapes = [w[k].shape for k in SMALL]
    d, nm, nv = adamw("adamw_small", *[_pack([t[k] for k in SMALL], 8) for t in (w, grad, m, v)])
    for k, dk, mk, vk in zip(SMALL, _unpack(d, shapes), _unpack(nm, shapes), _unpack(nv, shapes)):
        delta[k], new_m[k], new_v[k] = dk, mk, vk
    return (loss, grad_x[None], *[grad[k] for k in WEIGHTS], *[delta[k] for k in WEIGHTS],
            *[new_m[k] for k in WEIGHTS], *[new_v[k] for k in WEIGHTS])


def kernel(x, positions, ln_mix_g, ln_mix_b, ln_ffn_g, ln_ffn_b, even_w_in, pool_w, pool_scale, lru_conv_w, lru_conv_b, lru_w_a, lru_b_a, lru_w_x, lru_b_x, lru_lambda, even_w_out, mla_w_down, mla_q_norm_g, mla_kv_norm_g, mla_w_qb, mla_w_kvb, mla_w_o, mlp_w1, mlp_w2, loss_target, m_ln_mix_g, m_ln_mix_b, m_ln_ffn_g, m_ln_ffn_b, m_even_w_in, m_pool_w, m_pool_scale, m_lru_conv_w, m_lru_conv_b, m_lru_w_a, m_lru_b_a, m_lru_w_x, m_lru_b_x, m_lru_lambda, m_even_w_out, m_mla_w_down, m_mla_q_norm_g, m_mla_kv_norm_g, m_mla_w_qb, m_mla_w_kvb, m_mla_w_o, m_mlp_w1, m_mlp_w2, v_ln_mix_g, v_ln_mix_b, v_ln_ffn_g, v_ln_ffn_b, v_even_w_in, v_pool_w, v_pool_scale, v_lru_conv_w, v_lru_conv_b, v_lru_w_a, v_lru_b_a, v_lru_w_x, v_lru_b_x, v_lru_lambda, v_even_w_out, v_mla_w_down, v_mla_q_norm_g, v_mla_kv_norm_g, v_mla_w_qb, v_mla_w_kvb, v_mla_w_o, v_mlp_w1, v_mlp_w2):
    args = locals()
    w = {k: args[k] for k in WEIGHTS}
    m = {k: args["m_" + k] for k in WEIGHTS}
    v = {k: args["v_" + k] for k in WEIGHTS}
    return _step(x, positions, loss_target, w, m, v)
```

```python
import functools
import math

import jax
import jax.numpy as jnp
import numpy as np
from jax import lax
from jax.experimental import pallas as pl
from jax.experimental.pallas import tpu as pltpu

F32 = jnp.float32
BF16 = jnp.bfloat16

S = 2048
D = 1024
DEPTH = 4
N_CHIPS = 4
POOL_WINDOWS = (2, 4, 8, 16)
POOL_W = 512
LRU_W = 1024
LRU_HEADS = 8
HEAD = 128
EVEN_IN = 2560
EVEN_MIX = 1536
MLA_HEADS = 8
QK_NOPE = 128
QK_ROPE = 64
Q_RANK = 384
KV_RANK = 256
ODD_IN = 704
ODD_IN_PAD = 768
QHEAD_PAD = 256
D_FF = 4096
CHUNK = 64
ALPHA = (2 * DEPTH) ** 0.25
LN_EPS = 1e-5
RMS_EPS = 1e-6
ATT_SCALE = (QK_NOPE + QK_ROPE) ** -0.5
ROPE_THETA = 10000.0
LRU_C = 8.0
ADAM_LR = 0.001
ADAM_B1 = 0.9
ADAM_B2 = 0.999
ADAM_EPS = 1e-08
ADAM_WD = 0.01
ADAM_STEP = 10

VMEM_LIMIT = 56 * 1024 * 1024
MESH = pl.DeviceIdType.MESH

_NN = (((1,), (0,)), ((), ()))
_NT = (((1,), (1,)), ((), ()))
_TN = (((0,), (0,)), ((), ()))


def _params(sem=None, **kw):
    return pltpu.CompilerParams(dimension_semantics=sem, vmem_limit_bytes=VMEM_LIMIT, **kw)


def _dot(a, b, dims):
    return lax.dot_general(a.astype(BF16), b.astype(BF16), dims, preferred_element_type=F32)


def _mm(name, a, w4, layer, *, dims, a_block, a_map, w_map, grid, sem, outs, out_blocks, out_maps,
        red_axis=None, n_red=1, extras=(), extra_blocks=(), extra_maps=(), epilogue=None):
    n_extra = len(extras)
    n_out = len(outs)
    w_block = (None, None) + tuple(w4.shape[2:])

    def body(a_ref, w_ref, *rest):
        extra_refs = rest[:n_extra]
        out_refs = rest[n_extra:n_extra + n_out]
        p = _dot(a_ref[...], w_ref[...], dims)

        def finish(acc):
            vals = epilogue(acc, *[r[...] for r in extra_refs]) if epilogue else (acc,)
            for r, v in zip(out_refs, vals):
                r[...] = v.astype(r.dtype)

        if red_axis is None:
            finish(p)
        else:
            acc_ref = rest[-1]
            k = pl.program_id(red_axis)

            @pl.when(k == 0)
            def _():
                acc_ref[...] = p

            @pl.when(k > 0)
            def _():
                acc_ref[...] += p

            @pl.when(k == n_red - 1)
            def _():
                finish(acc_ref[...])

    scratch = [] if red_axis is None else [pltpu.VMEM(out_blocks[0], F32)]
    return pl.pallas_call(
        body, name=name, grid=grid,
        in_specs=[pl.BlockSpec(a_block, a_map), pl.BlockSpec(w_block, functools.partial(w_map, layer))]
        + [pl.BlockSpec(b, m) for b, m in zip(extra_blocks, extra_maps)],
        out_specs=[pl.BlockSpec(b, m) for b, m in zip(out_blocks, out_maps)],
        out_shape=outs, scratch_shapes=scratch, compiler_params=_params(sem),
    )(a, w4, *extras)


def _w_map_outer(layer, s, i):
    return (s, layer, 0, 0)


def _w_map_inner(layer, i, s):
    return (s, layer, 0, 0)


def mm_cols(name, a, w4, layer, out_dtypes, *, tm=512, epilogue=None):
    m, k = a.shape
    nb = w4.shape[3]
    outs = [jax.ShapeDtypeStruct((m, N_CHIPS * nb), dt) for dt in out_dtypes]
    return _mm(name, a, w4, layer, dims=_NN, a_block=(tm, k), a_map=lambda s, i: (i, 0), w_map=_w_map_outer,
               grid=(N_CHIPS, m // tm), sem=("parallel", "parallel"), outs=outs,
               out_blocks=[(tm, nb)] * len(outs), out_maps=[lambda s, i: (i, s)] * len(outs), epilogue=epilogue)


def mm_rows(name, a, w4, layer, out_dtype, *, tm=512):
    m = a.shape[0]
    kb, n = w4.shape[2:]
    outs = [jax.ShapeDtypeStruct((m, n), out_dtype)]
    return _mm(name, a, w4, layer, dims=_NN, a_block=(tm, kb), a_map=lambda i, s: (i, s), w_map=_w_map_inner,
               grid=(m // tm, N_CHIPS), sem=("parallel", "arbitrary"), outs=outs, out_blocks=[(tm, n)],
               out_maps=[lambda i, s: (i, 0)], red_axis=1, n_red=N_CHIPS)[0]


def mm_t_cols(name, g, w4, layer, out_dtype, *, tm=512, resid=None):
    m = g.shape[0]
    k, nb = w4.shape[2:]
    outs = [jax.ShapeDtypeStruct((m, k), out_dtype)]
    extras = () if resid is None else (resid,)
    epi = None if resid is None else (lambda acc, r: (acc + ALPHA * r,))
    return _mm(name, g, w4, layer, dims=_NT, a_block=(tm, nb), a_map=lambda i, s: (i, s), w_map=_w_map_inner,
               grid=(m // tm, N_CHIPS), sem=("parallel", "arbitrary"), outs=outs, out_blocks=[(tm, k)],
               out_maps=[lambda i, s: (i, 0)], red_axis=1, n_red=N_CHIPS, extras=extras,
               extra_blocks=[(tm, k)] * len(extras), extra_maps=[lambda i, s: (i, 0)] * len(extras), epilogue=epi)[0]


def mm_t_rows(name, g, w4, layer, out_dtype, *, tm=512, resid=None, gate=None):
    m, n = g.shape
    kb = w4.shape[2]
    outs = [jax.ShapeDtypeStruct((m, N_CHIPS * kb), out_dtype)]
    extras, epi = (), None
    if resid is not None:
        extras, epi = (resid,), (lambda acc, r: (acc + ALPHA * r,))
    if gate is not None:
        extras, epi = (gate,), (lambda acc, r: (acc * (2.0 * r.astype(F32)),))
    return _mm(name, g, w4, layer, dims=_NT, a_block=(tm, n), a_map=lambda s, i: (i, 0), w_map=_w_map_outer,
               grid=(N_CHIPS, m // tm), sem=("parallel", "parallel"), outs=outs, out_blocks=[(tm, kb)],
               out_maps=[lambda s, i: (i, s)], extras=extras, extra_blocks=[(tm, kb)] * len(extras),
               extra_maps=[lambda s, i: (i, s)] * len(extras), epilogue=epi)[0]


def mm_grad(name, a, g, layer, n_layers, prev, *, a_sharded, tm=512):
    m = a.shape[0]
    ka = a.shape[1] // N_CHIPS if a_sharded else a.shape[1]
    ng = g.shape[1] if a_sharded else g.shape[1] // N_CHIPS
    n_red = m // tm
    a_map = (lambda s, i: (i, s)) if a_sharded else (lambda s, i: (i, 0))
    g_map = (lambda s, i: (i, 0)) if a_sharded else (lambda s, i: (i, s))

    def body(a_ref, g_ref, *rest):
        out_ref = rest[-1]
        p = _dot(a_ref[...], g_ref[...], _TN)
        k = pl.program_id(1)

        @pl.when(k == 0)
        def _():
            out_ref[...] = p

        @pl.when(k > 0)
        def _():
            out_ref[...] += p

    in_specs = [pl.BlockSpec((tm, ka), a_map), pl.BlockSpec((tm, ng), g_map)]
    args = [a, g]
    aliases = {}
    if prev is not None:
        in_specs.append(pl.BlockSpec(memory_space=pl.ANY))
        args.append(prev)
        aliases = {2: 0}
    return pl.pallas_call(
        body, name=name, grid=(N_CHIPS, n_red), in_specs=in_specs,
        out_specs=pl.BlockSpec((None, None, ka, ng), lambda s, i: (s, layer, 0, 0)),
        out_shape=jax.ShapeDtypeStruct((N_CHIPS, n_layers, ka, ng), F32),
        input_output_aliases=aliases, compiler_params=_params(("parallel", "arbitrary")),
    )(*args)


def ln_fwd(name, x, s, g, b, *, tm=256):
    def body(x_ref, s_ref, g_ref, b_ref, y_ref, xhat_ref, rstd_ref):
        z = ALPHA * x_ref[...] + s_ref[...]
        mu = jnp.mean(z, axis=-1, keepdims=True)
        zc = z - mu
        var = jnp.mean(zc * zc, axis=-1, keepdims=True)
        rstd = lax.rsqrt(var + LN_EPS)
        xhat = zc * rstd
        y_ref[...] = xhat * g_ref[...] + b_ref[...]
        xhat_ref[...] = xhat
        rstd_ref[...] = rstd

    row = pl.BlockSpec((tm, D), lambda i: (i, 0))
    vec = pl.BlockSpec((1, D), lambda i: (0, 0))
    return pl.pallas_call(
        body, name=name, grid=(S // tm,), in_specs=[row, row, vec, vec],
        out_specs=[row, row, pl.BlockSpec((tm, 1), lambda i: (i, 0))],
        out_shape=[jax.ShapeDtypeStruct((S, D), F32), jax.ShapeDtypeStruct((S, D), F32),
                   jax.ShapeDtypeStruct((S, 1), F32)],
        compiler_params=_params(("parallel",)),
    )(x, s, g, b)


def ln_bwd(name, dy, xhat, rstd, g, *, tm=256):
    def body(dy_ref, xhat_ref, rstd_ref, g_ref, dz_ref, dg_ref, db_ref):
        dy_v = dy_ref[...]
        xh = xhat_ref[...]
        dxh = dy_v * g_ref[...]
        m1 = jnp.mean(dxh, axis=-1, keepdims=True)
        m2 = jnp.mean(dxh * xh, axis=-1, keepdims=True)
        dz_ref[...] = rstd_ref[...] * (dxh - m1 - xh * m2)
        pg = jnp.sum(dy_v * xh, axis=0, keepdims=True)
        pb = jnp.sum(dy_v, axis=0, keepdims=True)
        i = pl.program_id(0)

        @pl.when(i == 0)
        def _():
            dg_ref[...] = pg
            db_ref[...] = pb

        @pl.when(i > 0)
        def _():
            dg_ref[...] += pg
            db_ref[...] += pb

    row = pl.BlockSpec((tm, D), lambda i: (i, 0))
    vec = pl.BlockSpec((1, D), lambda i: (0, 0))
    return pl.pallas_call(
        body, name=name, grid=(S // tm,), in_specs=[row, row, pl.BlockSpec((tm, 1), lambda i: (i, 0)), vec],
        out_specs=[row, vec, vec],
        out_shape=[jax.ShapeDtypeStruct((S, D), F32), jax.ShapeDtypeStruct((1, D), F32),
                   jax.ShapeDtypeStruct((1, D), F32)],
        compiler_params=_params(("arbitrary",)),
    )(dy, xhat, rstd, g)


def loss_head(y, target, *, tm=256):
    def body(y_ref, t_ref, dy_ref, loss_ref):
        e = y_ref[...] - t_ref[...]
        dy_ref[...] = e * (1.0 / D)
        part = jnp.sum(jnp.sum(e * e, axis=-1, keepdims=True), axis=0, keepdims=True) * (0.5 / D)
        i = pl.program_id(0)

        @pl.when(i == 0)
        def _():
            loss_ref[...] = jnp.zeros_like(loss_ref)

        loss_ref[...] += jnp.broadcast_to(part, loss_ref.shape)

    row = pl.BlockSpec((tm, D), lambda i: (i, 0))
    return pl.pallas_call(
        body, name="loss_head", grid=(S // tm,), in_specs=[row, row],
        out_specs=[row, pl.BlockSpec((8, 128), lambda i: (0, 0))],
        out_shape=[jax.ShapeDtypeStruct((S, D), F32), jax.ShapeDtypeStruct((8, 128), F32)],
        compiler_params=_params(("arbitrary",)),
    )(y, target)


def _rows(shape):
    return lax.broadcasted_iota(jnp.int32, shape, 0)


def _shift_down(x, k):
    return jnp.where(_rows(x.shape) >= k, pltpu.roll(x, k, 0), 0.0)


def _shift_up(x, k):
    n = x.shape[0]
    return jnp.where(_rows(x.shape) < n - k, pltpu.roll(x, n - k, 0), 0.0)


def _pool_diff(u, w):
    acc, k = u, 1
    while k < w:
        acc = acc + _shift_down(acc, k)
        k *= 2
    cnt = jnp.minimum(_rows(u.shape) + 1, w).astype(F32)
    return acc / cnt - u, cnt


def pool_fwd(name, proj, pool_w, pool_scale):
    def body(u_ref, w_ref, sc_ref, y_ref):
        for g, w in enumerate(POOL_WINDOWS):
            cols = slice(g * HEAD, (g + 1) * HEAD)
            d, _ = _pool_diff(u_ref[:, cols], w)
            z = _dot(d, w_ref[g], _NN)
            y_ref[:, cols] = (z * sc_ref[:, cols]).astype(y_ref.dtype)

    return pl.pallas_call(
        body, name=name, grid=(1,),
        in_specs=[pl.BlockSpec((S, POOL_W), lambda i: (0, 0)),
                  pl.BlockSpec((4, HEAD, HEAD), lambda i: (0, 0, 0)),
                  pl.BlockSpec((1, POOL_W), lambda i: (0, 0))],
        out_specs=pl.BlockSpec((S, POOL_W), lambda i: (0, 0)),
        out_shape=jax.ShapeDtypeStruct((S, POOL_W), BF16),
        compiler_params=_params(("arbitrary",)),
    )(proj, pool_w, pool_scale)


def pool_bwd(name, proj, dycat, pool_w, pool_scale):
    def body(u_ref, dy_ref, w_ref, sc_ref, du_ref, dw_ref, dsc_ref):
        for g, w in enumerate(POOL_WINDOWS):
            cols = slice(g * HEAD, (g + 1) * HEAD)
            d, cnt = _pool_diff(u_ref[:, cols], w)
            dy = dy_ref[:, cols]
            z = _dot(d, w_ref[g], _NN)
            dsc_ref[:, cols] = jnp.sum(dy * z, axis=0, keepdims=True)
            dz = dy * sc_ref[:, cols]
            dw_ref[g] = _dot(d, dz, _TN)
            dd = _dot(dz, w_ref[g], _NT)
            acc, k = dd / cnt, 1
            while k < w:
                acc = acc + _shift_up(acc, k)
                k *= 2
            du_ref[:, cols] = (acc - dd).astype(du_ref.dtype)

    return pl.pallas_call(
        body, name=name, grid=(1,),
        in_specs=[pl.BlockSpec((S, POOL_W), lambda i: (0, 0)),
                  pl.BlockSpec((S, POOL_W), lambda i: (0, 0)),
                  pl.BlockSpec((4, HEAD, HEAD), lambda i: (0, 0, 0)),
                  pl.BlockSpec((1, POOL_W), lambda i: (0, 0))],
        out_specs=[pl.BlockSpec((S, POOL_W), lambda i: (0, 0)),
                   pl.BlockSpec((4, HEAD, HEAD), lambda i: (0, 0, 0)),
                   pl.BlockSpec((1, POOL_W), lambda i: (0, 0))],
        out_shape=[jax.ShapeDtypeStruct((S, POOL_W), BF16), jax.ShapeDtypeStruct((4, HEAD, HEAD), F32),
                   jax.ShapeDtypeStruct((1, POOL_W), F32)],
        compiler_params=_params(("arbitrary",)),
    )(proj, dycat, pool_w, pool_scale)


def _expm1(x):
    series = x * (1.0 + x * (0.5 + x * (1.0 / 6.0 + x * (1.0 / 24.0 + x * (1.0 / 120.0)))))
    return jnp.where(jnp.abs(x) < 0.05, series, jnp.exp(x) - 1.0)


def _softplus_neg(lam):
    e = jnp.exp(-jnp.abs(lam))
    log1p = jnp.where(e < 0.01, e * (1.0 - e * (0.5 - e * (1.0 / 3.0))), jnp.log(1.0 + e))
    return jnp.maximum(-lam, 0.0) + log1p


_GELU_C = math.sqrt(2.0 / math.pi)


def _gelu(x):
    t = jnp.tanh(_GELU_C * (x + 0.044715 * x * x * x))
    return 0.5 * x * (1.0 + t), t


def _gelu_grad(x, t):
    return 0.5 * (1.0 + t) + 0.5 * x * (1.0 - t * t) * _GELU_C * (1.0 + 3.0 * 0.044715 * x * x)


def _conv(u, cw, cb):
    return cw[3:4] * u + cw[2:3] * _shift_down(u, 1) + cw[1:2] * _shift_down(u, 2) + cw[0:1] * _shift_down(u, 3) + cb


def _lru_gates(cu, wa, ba, wx, bx, lam):
    r = jax.nn.sigmoid(_dot(cu, wa, _NN) + ba)
    i = jax.nn.sigmoid(_dot(cu, wx, _NN) + bx)
    sp = _softplus_neg(lam)
    log_a = (-LRU_C) * r * sp
    a = jnp.exp(log_a)
    mult = jnp.sqrt(-_expm1(2.0 * log_a))
    return r, i, sp, a, mult


def _scan(a_ref, b_ref, h_ref, *, reverse):
    n_blk = S // 8
    row8 = lax.broadcasted_iota(jnp.int32, (8, HEAD), 0)

    def step(j, carry):
        blk = (n_blk - 1 - j) if reverse else j
        r0 = pl.multiple_of(blk * 8, 8)
        a = a_ref[pl.ds(r0, 8), :]
        b = b_ref[pl.ds(r0, 8), :]
        for k in (1, 2, 4):
            if reverse:
                keep = row8 < 8 - k
                a_s, b_s = pltpu.roll(a, 8 - k, 0), pltpu.roll(b, 8 - k, 0)
            else:
                keep = row8 >= k
                a_s, b_s = pltpu.roll(a, k, 0), pltpu.roll(b, k, 0)
            b = jnp.where(keep, a * b_s + b, b)
            a = jnp.where(keep, a * a_s, a)
        h = b + a * carry
        h_ref[pl.ds(r0, 8), :] = h
        edge = h[0:1, :] if reverse else h[7:8, :]
        return jnp.broadcast_to(edge, (8, HEAD))

    lax.fori_loop(0, n_blk, step, jnp.zeros((8, HEAD), F32), unroll=4)


def _lru_specs():
    def col(off):
        return pl.BlockSpec((S, HEAD), lambda h: (0, off + h))
    vec = pl.BlockSpec((1, HEAD), lambda h: (0, h))
    mat = pl.BlockSpec((None, HEAD, HEAD), lambda h: (h, 0, 0))
    cw = pl.BlockSpec((4, HEAD), lambda h: (0, h))
    return col, vec, mat, cw


def lru_fwd(name, proj, conv_w, conv_b, w_a, b_a, w_x, b_x, lam):
    def body(u_ref, ug_ref, cw_ref, cb_ref, wa_ref, ba_ref, wx_ref, bx_ref, lam_ref, y_ref, h_ref, a_s, b_s):
        cu = _conv(u_ref[...], cw_ref[...], cb_ref[...])
        _, i, _, a, mult = _lru_gates(cu, wa_ref[...], ba_ref[...], wx_ref[...], bx_ref[...], lam_ref[...])
        a_s[...] = a
        b_s[...] = mult * (i * cu)
        _scan(a_s, b_s, h_ref, reverse=False)
        gl, _ = _gelu(ug_ref[...])
        y_ref[...] = (h_ref[...] * gl).astype(y_ref.dtype)

    col, vec, mat, cw = _lru_specs()
    out = pl.BlockSpec((S, HEAD), lambda h: (0, h))
    return pl.pallas_call(
        body, name=name, grid=(LRU_HEADS,),
        in_specs=[col(4), col(12), cw, vec, mat, vec, mat, vec, vec],
        out_specs=[out, out],
        out_shape=[jax.ShapeDtypeStruct((S, LRU_W), BF16), jax.ShapeDtypeStruct((S, LRU_W), F32)],
        scratch_shapes=[pltpu.VMEM((S, HEAD), F32), pltpu.VMEM((S, HEAD), F32)],
        compiler_params=_params(("parallel",)),
    )(proj, proj, conv_w, conv_b, w_a, b_a, w_x, b_x, lam)


def lru_bwd(name, proj, hstate, dycat, conv_w, conv_b, w_a, b_a, w_x, b_x, lam):
    def body(u_ref, ug_ref, h_ref, dy_ref, cw_ref, cb_ref, wa_ref, ba_ref, wx_ref, bx_ref, lam_ref,
             du_ref, dug_ref, dwa_ref, dwx_ref, dba_ref, dbx_ref, dlam_ref, dcw_ref, dcb_ref, a_s, b_s, g_s):
        u = u_ref[...]
        cw = cw_ref[...]
        cu = _conv(u, cw, cb_ref[...])
        lam_v = lam_ref[...]
        r, i, sp, a, mult = _lru_gates(cu, wa_ref[...], ba_ref[...], wx_ref[...], bx_ref[...], lam_v)
        ug = ug_ref[...]
        gl, t = _gelu(ug)
        dy = dy_ref[...]
        h = h_ref[...]
        dug_ref[...] = (dy * h * _gelu_grad(ug, t)).astype(dug_ref.dtype)
        a_s[...] = _shift_up(a, 1)
        b_s[...] = dy * gl
        _scan(a_s, b_s, g_s, reverse=True)
        dxin = g_s[...]
        da = dxin * _shift_down(h, 1)
        dmult = dxin * (i * cu)
        di = dxin * (mult * cu)
        dlog_a = da * a - dmult * (a * a) / mult
        dr_pre = dlog_a * ((-LRU_C) * sp) * (r * (1.0 - r))
        di_pre = di * (i * (1.0 - i))
        dsp = jnp.sum(dlog_a * ((-LRU_C) * r), axis=0, keepdims=True)
        dlam_ref[...] = dsp * (-jax.nn.sigmoid(-lam_v))
        dba_ref[...] = jnp.sum(dr_pre, axis=0, keepdims=True)
        dbx_ref[...] = jnp.sum(di_pre, axis=0, keepdims=True)
        dwa_ref[...] = _dot(cu, dr_pre, _TN)
        dwx_ref[...] = _dot(cu, di_pre, _TN)
        dcu = dxin * (mult * i) + _dot(dr_pre, wa_ref[...], _NT) + _dot(di_pre, wx_ref[...], _NT)
        dcb_ref[...] = jnp.sum(dcu, axis=0, keepdims=True)
        for k in range(4):
            dcw_ref[k:k + 1, :] = jnp.sum(dcu * (_shift_down(u, 3 - k) if k < 3 else u), axis=0, keepdims=True)
        du = cw[3:4] * dcu + cw[2:3] * _shift_up(dcu, 1) + cw[1:2] * _shift_up(dcu, 2) + cw[0:1] * _shift_up(dcu, 3)
        du_ref[...] = du.astype(du_ref.dtype)

    col, vec, mat, cw = _lru_specs()
    out = pl.BlockSpec((S, HEAD), lambda h: (0, h))
    big = jax.ShapeDtypeStruct((S, LRU_W), BF16)
    vec_shape = jax.ShapeDtypeStruct((1, LRU_W), F32)
    mat_shape = jax.ShapeDtypeStruct((LRU_HEADS, HEAD, HEAD), F32)
    return pl.pallas_call(
        body, name=name, grid=(LRU_HEADS,),
        in_specs=[col(4), col(12), out, col(4), cw, vec, mat, vec, mat, vec, vec],
        out_specs=[out, out, mat, mat, vec, vec, vec, cw, vec],
        out_shape=[big, big, mat_shape, mat_shape, vec_shape, vec_shape, vec_shape,
                   jax.ShapeDtypeStruct((4, LRU_W), F32), vec_shape],
        scratch_shapes=[pltpu.VMEM((S, HEAD), F32)] * 3,
        compiler_params=_params(("parallel",)),
    )(proj, proj, hstate, dycat, conv_w, conv_b, w_a, b_a, w_x, b_x, lam)


def rope_tables(pos_col, inv_freq):
    def body(pos_ref, f_ref, c_ref, s1_ref, s2_ref):
        ang = pos_ref[...].astype(F32) * f_ref[...]
        lane = lax.broadcasted_iota(jnp.int32, ang.shape, 1)
        cos, sin = jnp.cos(ang), jnp.sin(ang)
        c_ref[...] = jnp.where(lane < QK_ROPE, cos, 0.0)
        s1_ref[...] = jnp.where(lane < QK_ROPE // 2, -sin, 0.0)
        s2_ref[...] = jnp.where((lane >= QK_ROPE // 2) & (lane < QK_ROPE), sin, 0.0)

    tab = jax.ShapeDtypeStruct((S, HEAD), F32)
    return pl.pallas_call(
        body, name="rope_tables", grid=(1,),
        in_specs=[pl.BlockSpec((S, 1), lambda i: (0, 0)), pl.BlockSpec((1, HEAD), lambda i: (0, 0))],
        out_specs=[pl.BlockSpec((S, HEAD), lambda i: (0, 0))] * 3, out_shape=[tab, tab, tab],
        compiler_params=_params(("arbitrary",)),
    )(pos_col, inv_freq)


def _rope(v, c, s1, s2):
    return v * c + pltpu.roll(v, HEAD - QK_ROPE // 2, 1) * s1 + pltpu.roll(v, QK_ROPE // 2, 1) * s2


def _unrope(d, c, s1, s2):
    return d * c + pltpu.roll(d * s1, QK_ROPE // 2, 1) + pltpu.roll(d * s2, HEAD - QK_ROPE // 2, 1)


def _rms(x, g):
    rstd = lax.rsqrt(jnp.mean(x * x, axis=-1, keepdims=True) + RMS_EPS)
    return x * rstd, rstd


def mla_prep(name, down, gq, gkv, tabs, *, tm=256):
    def body(dn_ref, gq_ref, gkv_ref, c_ref, s1_ref, s2_ref, cq_ref, ckv_ref, kp_ref):
        xq, _ = _rms(dn_ref[:, :Q_RANK], None)
        cq_ref[...] = (xq * gq_ref[...]).astype(cq_ref.dtype)
        xkv, _ = _rms(dn_ref[:, Q_RANK:Q_RANK + KV_RANK], None)
        ckv_ref[...] = (xkv * gkv_ref[...]).astype(ckv_ref.dtype)
        kp = _rope(dn_ref[:, Q_RANK + KV_RANK:], c_ref[...], s1_ref[...], s2_ref[...])
        kp_ref[...] = kp.astype(kp_ref.dtype)

    tab = pl.BlockSpec((tm, HEAD), lambda i: (i, 0))
    return pl.pallas_call(
        body, name=name, grid=(S // tm,),
        in_specs=[pl.BlockSpec((tm, ODD_IN_PAD), lambda i: (i, 0)), pl.BlockSpec((1, Q_RANK), lambda i: (0, 0)),
                  pl.BlockSpec((1, KV_RANK), lambda i: (0, 0)), tab, tab, tab],
        out_specs=[pl.BlockSpec((tm, Q_RANK), lambda i: (i, 0)), pl.BlockSpec((tm, KV_RANK), lambda i: (i, 0)), tab],
        out_shape=[jax.ShapeDtypeStruct((S, Q_RANK), BF16), jax.ShapeDtypeStruct((S, KV_RANK), BF16),
                   jax.ShapeDtypeStruct((S, HEAD), BF16)],
        compiler_params=_params(("parallel",)),
    )(down, gq, gkv, *tabs)


def mla_prep_bwd(name, down, dcq, dckv, dkp, gq, gkv, tabs, *, tm=256):
    def body(dn_ref, dcq_ref, dckv_ref, dkp_ref, gq_ref, gkv_ref, c_ref, s1_ref, s2_ref, dd_ref, dgq_ref, dgkv_ref):
        i = pl.program_id(0)

        def rms_bwd(x, dy, g, dg_ref):
            xh, rstd = _rms(x, None)
            dxh = dy * g
            dx = rstd * (dxh - xh * jnp.mean(dxh * xh, axis=-1, keepdims=True))
            pg = jnp.sum(dy * xh, axis=0, keepdims=True)

            @pl.when(i == 0)
            def _():
                dg_ref[...] = pg

            @pl.when(i > 0)
            def _():
                dg_ref[...] += pg

            return dx

        dxq = rms_bwd(dn_ref[:, :Q_RANK], dcq_ref[...], gq_ref[...], dgq_ref)
        dd_ref[:, :Q_RANK] = dxq.astype(dd_ref.dtype)
        dxkv = rms_bwd(dn_ref[:, Q_RANK:Q_RANK + KV_RANK], dckv_ref[...], gkv_ref[...], dgkv_ref)
        dd_ref[:, Q_RANK:Q_RANK + KV_RANK] = dxkv.astype(dd_ref.dtype)
        dd_ref[:, Q_RANK + KV_RANK:] = _unrope(dkp_ref[...], c_ref[...], s1_ref[...], s2_ref[...]).astype(dd_ref.dtype)

    tab = pl.BlockSpec((tm, HEAD), lambda i: (i, 0))
    vq = pl.BlockSpec((1, Q_RANK), lambda i: (0, 0))
    vkv = pl.BlockSpec((1, KV_RANK), lambda i: (0, 0))
    return pl.pallas_call(
        body, name=name, grid=(S // tm,),
        in_specs=[pl.BlockSpec((tm, ODD_IN_PAD), lambda i: (i, 0)), pl.BlockSpec((tm, Q_RANK), lambda i: (i, 0)),
                  pl.BlockSpec((tm, KV_RANK), lambda i: (i, 0)), tab, vq, vkv, tab, tab, tab],
        out_specs=[pl.BlockSpec((tm, ODD_IN_PAD), lambda i: (i, 0)), vq, vkv],
        out_shape=[jax.ShapeDtypeStruct((S, ODD_IN_PAD), BF16), jax.ShapeDtypeStruct((1, Q_RANK), F32),
                   jax.ShapeDtypeStruct((1, KV_RANK), F32)],
        compiler_params=_params(("arbitrary",)),
    )(down, dcq, dckv, dkp, gq, gkv, *tabs)


ATT_TQ = 256


def _attn_scores(qn, qp, kv_ref, kp_ref, i, j):
    rows = pl.ds(pl.multiple_of(j * ATT_TQ, ATT_TQ), ATT_TQ)
    kn = kv_ref[rows, :HEAD]
    kp = kp_ref[rows, :]
    sc = (_dot(qn, kn, _NT) + _dot(qp, kp, _NT)) * ATT_SCALE
    q_chunk = (i * ATT_TQ + lax.broadcasted_iota(jnp.int32, sc.shape, 0)) // CHUNK
    k_chunk = (j * ATT_TQ + lax.broadcasted_iota(jnp.int32, sc.shape, 1)) // CHUNK
    return jnp.where(k_chunk <= q_chunk, sc, jnp.finfo(F32).min), rows, kn, kp


def _attn_specs():
    q = pl.BlockSpec((ATT_TQ, QHEAD_PAD), lambda h, i: (i, h))
    kv = pl.BlockSpec((S, QHEAD_PAD), lambda h, i: (0, h))
    kp = pl.BlockSpec((S, HEAD), lambda h, i: (0, 0))
    tab = pl.BlockSpec((ATT_TQ, HEAD), lambda h, i: (i, 0))
    o = pl.BlockSpec((ATT_TQ, HEAD), lambda h, i: (i, h))
    lse = pl.BlockSpec((None, ATT_TQ, 1), lambda h, i: (h, i, 0))
    return q, kv, kp, tab, o, lse


def attn_fwd(name, q, kv, kp, tabs):
    def body(q_ref, kv_ref, kp_ref, c_ref, s1_ref, s2_ref, o_ref, lse_ref):
        i = pl.program_id(1)
        qn = q_ref[:, :HEAD].astype(BF16)
        qp = _rope(q_ref[:, HEAD:], c_ref[...], s1_ref[...], s2_ref[...]).astype(BF16)

        def step(j, carry):
            m, l, acc = carry
            sc, rows, _, _ = _attn_scores(qn, qp, kv_ref, kp_ref, i, j)
            m_new = jnp.maximum(m, jnp.max(sc, axis=-1, keepdims=True))
            scale = jnp.exp(m - m_new)
            p = jnp.exp(sc - m_new)
            l = scale * l + jnp.sum(p, axis=-1, keepdims=True)
            acc = scale * acc + _dot(p, kv_ref[rows, HEAD:], _NN)
            return m_new, l, acc

        init = (jnp.full((ATT_TQ, 1), jnp.finfo(F32).min, F32), jnp.zeros((ATT_TQ, 1), F32),
                jnp.zeros((ATT_TQ, HEAD), F32))
        m, l, acc = lax.fori_loop(0, i + 1, step, init)
        o_ref[...] = acc * (1.0 / l)
        lse_ref[...] = m + jnp.log(l)

    qs, kvs, kps, tab, os, ls = _attn_specs()
    return pl.pallas_call(
        body, name=name, grid=(MLA_HEADS, S // ATT_TQ), in_specs=[qs, kvs, kps, tab, tab, tab], out_specs=[os, ls],
        out_shape=[jax.ShapeDtypeStruct((S, MLA_HEADS * HEAD), F32), jax.ShapeDtypeStruct((MLA_HEADS, S, 1), F32)],
        compiler_params=_params(("parallel", "parallel")),
    )(q, kv, kp, *tabs)


def attn_bwd(name, q, kv, kp, o, lse, do, tabs):
    def body(q_ref, kv_ref, kp_ref, o_ref, lse_ref, do_ref, c_ref, s1_ref, s2_ref, dq_ref, dkv_ref, dkp_ref):
        h, i = pl.program_id(0), pl.program_id(1)
        qn = q_ref[:, :HEAD].astype(BF16)
        qp = _rope(q_ref[:, HEAD:], c_ref[...], s1_ref[...], s2_ref[...]).astype(BF16)
        do_v = do_ref[...]
        delta = jnp.sum(do_v.astype(F32) * o_ref[...], axis=-1, keepdims=True)
        lse_v = lse_ref[...]

        @pl.when(i == 0)
        def _():
            dkv_ref[...] = jnp.zeros_like(dkv_ref)

        @pl.when((i == 0) & (h == 0))
        def _():
            dkp_ref[...] = jnp.zeros_like(dkp_ref)

        def step(j, carry):
            dqn, dqp = carry
            sc, rows, kn, kp_v = _attn_scores(qn, qp, kv_ref, kp_ref, i, j)
            p = jnp.exp(sc - lse_v)
            dp = _dot(do_v, kv_ref[rows, HEAD:], _NT)
            ds = (p * (dp - delta) * ATT_SCALE).astype(BF16)
            dkv_ref[rows, :HEAD] += _dot(ds, qn, _TN)
            dkv_ref[rows, HEAD:] += _dot(p, do_v, _TN)
            dkp_ref[rows, :] += _dot(ds, qp, _TN)
            return dqn + _dot(ds, kn, _NN), dqp + _dot(ds, kp_v, _NN)

        zero = jnp.zeros((ATT_TQ, HEAD), F32)
        dqn, dqp = lax.fori_loop(0, i + 1, step, (zero, zero))
        dq_ref[:, :HEAD] = dqn.astype(dq_ref.dtype)
        dq_ref[:, HEAD:] = _unrope(dqp, c_ref[...], s1_ref[...], s2_ref[...]).astype(dq_ref.dtype)

    qs, kvs, kps, tab, os, ls = _attn_specs()
    return pl.pallas_call(
        body, name=name, grid=(MLA_HEADS, S // ATT_TQ), in_specs=[qs, kvs, kps, os, ls, os, tab, tab, tab],
        out_specs=[qs, kvs, kps],
        out_shape=[jax.ShapeDtypeStruct((S, MLA_HEADS * QHEAD_PAD), BF16),
                   jax.ShapeDtypeStruct((S, MLA_HEADS * QHEAD_PAD), F32), jax.ShapeDtypeStruct((S, HEAD), F32)],
        compiler_params=_params(("arbitrary", "arbitrary")),
    )(q, kv, kp, o, lse, do, *tabs)


def adamw(name, w, g, m, v):
    rows, cols = w.shape
    tr = rows
    for cand in (512, 256, 128, 64, 32, 16, 8):
        if rows % cand == 0 and cand * cols * 4 <= 2 * 1024 * 1024:
            tr = cand
            break

    def body(w_ref, g_ref, m_ref, v_ref, d_ref, nm_ref, nv_ref):
        g_v = g_ref[...]
        nm = ADAM_B1 * m_ref[...] + (1.0 - ADAM_B1) * g_v
        nv = ADAM_B2 * v_ref[...] + (1.0 - ADAM_B2) * (g_v * g_v)
        m_hat = nm / (1.0 - ADAM_B1 ** ADAM_STEP)
        v_hat = nv / (1.0 - ADAM_B2 ** ADAM_STEP)
        d_ref[...] = (-ADAM_LR) * (m_hat / (jnp.sqrt(v_hat) + ADAM_EPS) + ADAM_WD * w_ref[...])
        nm_ref[...] = nm
        nv_ref[...] = nv

    blk = pl.BlockSpec((tr, cols), lambda i: (i, 0))
    shape = jax.ShapeDtypeStruct((rows, cols), F32)
    return pl.pallas_call(
        body, name=name, grid=(rows // tr,), in_specs=[blk] * 4, out_specs=[blk] * 3, out_shape=[shape] * 3,
        compiler_params=_params(("parallel",)),
    )(w, g, m, v)


def _local_step(x, pos_col, target, wg, sm):
    inv_freq = ROPE_THETA ** (-jnp.arange(0, QK_ROPE, 2, dtype=F32) / QK_ROPE)
    inv_freq = jnp.concatenate([inv_freq, inv_freq, jnp.zeros((HEAD - QK_ROPE,), F32)])[None, :]
    tabs = rope_tables(pos_col, inv_freq)
    saved = []
    for layer in range(DEPTH):
        j = layer // 2
        n = "l%d_" % layer
        sv = {"x": x}
        if layer % 2 == 0:
            proj = mm_cols(n + "proj", x, wg["w_in"], j, [F32])[0]
            y_pool = pool_fwd(n + "pool", proj, sm["pool_w"][j], sm["pool_scale"][j][None])
            y_lru, hstate = lru_fwd(n + "lru", proj, sm["conv_w"][j], sm["conv_b"][j][None], sm["w_a"][j],
                                    sm["b_a"][j][None], sm["w_x"][j], sm["b_x"][j][None], sm["lam"][j][None])
            ycat = jnp.concatenate([y_pool, y_lru], axis=1)
            mix = mm_rows(n + "mixout", ycat, wg["w_out"], j, F32)
            sv.update(proj=proj, hstate=hstate, ycat=ycat)
        else:
            down = mm_rows(n + "down", x, wg["w_down"], j, F32)
            cq, ckv, kp = mla_prep(n + "prep", down, sm["gq"][j][None], sm["gkv"][j][None], tabs)
            q = mm_cols(n + "q", cq, wg["w_qb"], j, [F32])[0]
            kv = mm_cols(n + "kv", ckv, wg["w_kvb"], j, [BF16])[0]
            o, lse = attn_fwd(n + "attn", q, kv, kp, tabs)
            mix = mm_rows(n + "attnout", o, wg["w_o"], j, F32)
            sv.update(down=down, cq=cq, ckv=ckv, kp=kp, q=q, kv=kv, o=o, lse=lse)
        x1, xhat1, rstd1 = ln_fwd(n + "ln_mix", x, mix, sm["ln_mix_g"][layer][None], sm["ln_mix_b"][layer][None])
        act, relu = mm_cols(n + "mlp1", x1, wg["w1"], layer, [BF16, BF16],
                            epilogue=lambda acc: (jnp.square(jnp.maximum(acc, 0.0)), jnp.maximum(acc, 0.0)))
        mlp = mm_rows(n + "mlp2", act, wg["w2"], layer, F32)
        x2, xhat2, rstd2 = ln_fwd(n + "ln_ffn", x1, mlp, sm["ln_ffn_g"][layer][None], sm["ln_ffn_b"][layer][None])
        sv.update(xhat1=xhat1, rstd1=rstd1, x1=x1, act=act, relu=relu, xhat2=xhat2, rstd2=rstd2)
        saved.append(sv)
        x = x2

    dx, loss_tile = loss_head(x, target)
    gw = {k: None for k in wg}
    gs = {k: [None] * (DEPTH if k.startswith("ln_") else DEPTH // 2) for k in sm}
    for layer in reversed(range(DEPTH)):
        j = layer // 2
        n = "l%d_" % layer
        sv = saved[layer]
        dz, gs["ln_ffn_g"][layer], gs["ln_ffn_b"][layer] = ln_bwd(
            n + "ln_ffn_b", dx, sv["xhat2"], sv["rstd2"], sm["ln_ffn_g"][layer][None])
        dh = mm_t_rows(n + "mlp2_dx", dz, wg["w2"], layer, BF16, gate=sv["relu"])
        gw["w2"] = mm_grad(n + "mlp2_dw", sv["act"], dz, layer, DEPTH, gw["w2"], a_sharded=True)
        dx = mm_t_cols(n + "mlp1_dx", dh, wg["w1"], layer, F32, resid=dz)
        gw["w1"] = mm_grad(n + "mlp1_dw", sv["x1"], dh, layer, DEPTH, gw["w1"], a_sharded=False)
        dz, gs["ln_mix_g"][layer], gs["ln_mix_b"][layer] = ln_bwd(
            n + "ln_mix_b", dx, sv["xhat1"], sv["rstd1"], sm["ln_mix_g"][layer][None])
        if layer % 2 == 0:
            dycat = mm_t_rows(n + "mixout_dx", dz, wg["w_out"], j, F32)
            gw["w_out"] = mm_grad(n + "mixout_dw", sv["ycat"], dz, j, DEPTH // 2, gw["w_out"], a_sharded=True)
            du_pool, gs["pool_w"][j], gs["pool_scale"][j] = pool_bwd(
                n + "pool_b", sv["proj"], dycat, sm["pool_w"][j], sm["pool_scale"][j][None])
            (du_lru, du_gate, gs["w_a"][j], gs["w_x"][j], gs["b_a"][j], gs["b_x"][j], gs["lam"][j], gs["conv_w"][j],
             gs["conv_b"][j]) = lru_bwd(n + "lru_b", sv["proj"], sv["hstate"], dycat, sm["conv_w"][j],
                                        sm["conv_b"][j][None], sm["w_a"][j], sm["b_a"][j][None], sm["w_x"][j],
                                        sm["b_x"][j][None], sm["lam"][j][None])
            dproj = jnp.concatenate([du_pool, du_lru, du_gate], axis=1)
            gw["w_in"] = mm_grad(n + "proj_dw", sv["x"], dproj, j, DEPTH // 2, gw["w_in"], a_sharded=False)
            dx = mm_t_cols(n + "proj_dx", dproj, wg["w_in"], j, F32, resid=dz)
        else:
            do = mm_t_rows(n + "attnout_dx", dz, wg["w_o"], j, BF16)
            gw["w_o"] = mm_grad(n + "attnout_dw", sv["o"], dz, j, DEPTH // 2, gw["w_o"], a_sharded=True)
            dq, dkv, dkp = attn_bwd(n + "attn_b", sv["q"], sv["kv"], sv["kp"], sv["o"], sv["lse"], do, tabs)
            gw["w_qb"] = mm_grad(n + "q_dw", sv["cq"], dq, j, DEPTH // 2, gw["w_qb"], a_sharded=False)
            dcq = mm_t_cols(n + "q_dx", dq, wg["w_qb"], j, F32)
            gw["w_kvb"] = mm_grad(n + "kv_dw", sv["ckv"], dkv, j, DEPTH // 2, gw["w_kvb"], a_sharded=False)
            dckv = mm_t_cols(n + "kv_dx", dkv, wg["w_kvb"], j, F32)
            ddown, gs["gq"][j], gs["gkv"][j] = mla_prep_bwd(
                n + "prep_b", sv["down"], dcq, dckv, dkp, sm["gq"][j][None], sm["gkv"][j][None], tabs)
            gw["w_down"] = mm_grad(n + "down_dw", sv["x"], ddown, j, DEPTH // 2, gw["w_down"], a_sharded=True)
            dx = mm_t_rows(n + "down_dx", ddown, wg["w_down"], j, F32, resid=dz)
    gs = {k: jnp.stack([a.reshape(sm[k].shape[1:]) for a in v]) for k, v in gs.items()}
    return loss_tile[0, 0], dx, gw, gs


def _place():
    x, y, c = lax.axis_index("x"), lax.axis_index("y"), lax.axis_index("c")
    chips = [(1 - x, y), (x, 1 - y), (1 - x, 1 - y)]
    return x, y, c, chips


def _hbm_call(body, name, args, out_shape, scratch, aliases=None):
    return pl.pallas_call(
        body, name=name, in_specs=[pl.BlockSpec(memory_space=pl.ANY)] * len(args),
        out_specs=[pl.BlockSpec(memory_space=pl.ANY)] * len(out_shape), out_shape=out_shape,
        scratch_shapes=scratch, input_output_aliases=aliases or {},
        compiler_params=pltpu.CompilerParams(has_side_effects=True),
    )(*args)


def all_gather_shards(arrs):
    n = len(arrs)

    def body(*refs):
        ins, outs = refs[:n], refs[n:2 * n]
        send_sems, recv_sems, local_sems = refs[2 * n:]
        x, y, c, chips = _place()
        me = 2 * x + y
        sibling = (x, y, 1 - c)

        def halves(k):
            hl = arrs[k].shape[0] // 2
            return pl.ds(c * hl, hl), pl.ds((1 - c) * hl, hl)

        def direct(k, j, to_chip, from_chip):
            mine, _ = halves(k)
            return pltpu.make_async_remote_copy(
                src_ref=ins[k].at[mine], dst_ref=outs[k].at[from_chip, mine], send_sem=send_sems.at[k, j],
                recv_sem=recv_sems.at[k, j], device_id=(*to_chip, c), device_id_type=MESH)

        def passed(k, j, chip_idx, half):
            return pltpu.make_async_remote_copy(
                src_ref=outs[k].at[chip_idx, half], dst_ref=outs[k].at[chip_idx, half], send_sem=send_sems.at[k, 3 + j],
                recv_sem=recv_sems.at[k, 3 + j], device_id=sibling, device_id_type=MESH)

        local = [pltpu.make_async_copy(ins[k], outs[k].at[me], local_sems.at[k]) for k in range(n)]
        for cp in local:
            cp.start()
        sends = [direct(k, j, chip, me) for k in range(n) for j, chip in enumerate(chips)]
        for cp in sends:
            cp.start()
        for k in range(n):
            mine, _ = halves(k)
            for j, (px, py) in enumerate(chips):
                direct(k, j, (px, py), 2 * px + py).wait_recv()
                fwd = passed(k, j, 2 * px + py, mine)
                fwd.start()
                sends.append(fwd)
        for k in range(n):
            _, other = halves(k)
            for j, (px, py) in enumerate(chips):
                passed(k, j, 2 * px + py, other).wait_recv()
        for cp in sends:
            cp.wait_send()
        for cp in local:
            cp.wait()

    out_shape = [jax.ShapeDtypeStruct((N_CHIPS,) + a.shape, a.dtype) for a in arrs]
    scratch = [pltpu.SemaphoreType.DMA((n, 6)), pltpu.SemaphoreType.DMA((n, 6)), pltpu.SemaphoreType.DMA((n,))]
    return _hbm_call(body, "all_gather_shards", arrs, out_shape, scratch)


def sibling_send_halves(grads):
    n = len(grads)

    def body(*refs):
        ins, outs = refs[:n], refs[n:2 * n]
        send_sems, recv_sems = refs[2 * n:]
        x, y, c, _ = _place()
        copies = []
        for k in range(n):
            nd, nl = grads[k].shape[:2]
            hl = nl // 2
            copies.append(pltpu.make_async_remote_copy(
                src_ref=ins[k].at[pl.ds(0, nd), pl.ds((1 - c) * hl, hl)], dst_ref=outs[k], send_sem=send_sems.at[k],
                recv_sem=recv_sems.at[k], device_id=(x, y, 1 - c), device_id_type=MESH))
        for cp in copies:
            cp.start()
        for cp in copies:
            cp.wait()

    out_shape = [jax.ShapeDtypeStruct((g.shape[0], g.shape[1] // 2) + g.shape[2:], g.dtype) for g in grads]
    scratch = [pltpu.SemaphoreType.DMA((n,)), pltpu.SemaphoreType.DMA((n,))]
    return _hbm_call(body, "sibling_send_halves", grads, out_shape, scratch)


def chip_exchange(parts):
    n = len(parts)

    def body(*refs):
        ins, outs = refs[:n], refs[n:2 * n]
        send_sems, recv_sems, local_sems = refs[2 * n:]
        x, y, c, chips = _place()
        me = 2 * x + y

        def slot(k, chip_idx):
            return ins[k].at[chip_idx] if parts[k].shape[0] == N_CHIPS else ins[k].at[0]

        def copy(k, j, to_chip, from_chip):
            return pltpu.make_async_remote_copy(
                src_ref=slot(k, 2 * to_chip[0] + to_chip[1]), dst_ref=outs[k].at[from_chip], send_sem=send_sems.at[k, j],
                recv_sem=recv_sems.at[k, j], device_id=(*to_chip, c), device_id_type=MESH)

        local = [pltpu.make_async_copy(slot(k, me), outs[k].at[me], local_sems.at[k]) for k in range(n)]
        for cp in local:
            cp.start()
        sends = [copy(k, j, chip, me) for k in range(n) for j, chip in enumerate(chips)]
        for cp in sends:
            cp.start()
        for k in range(n):
            for j, (px, py) in enumerate(chips):
                copy(k, j, (px, py), 2 * px + py).wait_recv()
        for cp in sends:
            cp.wait_send()
        for cp in local:
            cp.wait()

    out_shape = [jax.ShapeDtypeStruct((N_CHIPS,) + p.shape[1:], p.dtype) for p in parts]
    scratch = [pltpu.SemaphoreType.DMA((n, 3)), pltpu.SemaphoreType.DMA((n, 3)), pltpu.SemaphoreType.DMA((n,))]
    return _hbm_call(body, "chip_exchange", parts, out_shape, scratch)


def sibling_swap_halves(fulls):
    n = len(fulls)

    def body(*refs):
        outs = refs[n:2 * n]
        send_sems, recv_sems = refs[2 * n:]
        x, y, c, _ = _place()
        copies = []
        for k in range(n):
            hl = fulls[k].shape[0] // 2
            mine = outs[k].at[pl.ds(c * hl, hl)]
            copies.append((
                pltpu.make_async_remote_copy(src_ref=mine, dst_ref=mine, send_sem=send_sems.at[k], recv_sem=recv_sems.at[k],
                                             device_id=(x, y, 1 - c), device_id_type=MESH),
                pltpu.make_async_remote_copy(src_ref=mine, dst_ref=outs[k].at[pl.ds((1 - c) * hl, hl)], send_sem=send_sems.at[k],
                                             recv_sem=recv_sems.at[k], device_id=(x, y, 1 - c), device_id_type=MESH)))
        for send, _ in copies:
            send.start()
        for send, recv in copies:
            send.wait_send()
            recv.wait_recv()

    out_shape = [jax.ShapeDtypeStruct(f.shape, f.dtype) for f in fulls]
    scratch = [pltpu.SemaphoreType.DMA((n,)), pltpu.SemaphoreType.DMA((n,))]
    return _hbm_call(body, "sibling_swap_halves", fulls, out_shape, scratch, aliases={k: k for k in range(n)})


def _tile_rows(rows, cols, budget_bytes):
    best = None
    for t in range(16, rows + 1, 16):
        if rows % t == 0 and t * cols * 4 <= budget_bytes:
            best = t
    assert best is not None, (rows, cols)
    return best


def add_own_half(name, grad, recv, c_arr, out_dtype):
    nd, nl, rows, cols = grad.shape
    hl = nl // 2
    tr = _tile_rows(rows, cols, 2 * 1024 * 1024)

    def body(c_ref, g_ref, r_ref, o_ref):
        o_ref[...] = (g_ref[...] + r_ref[...]).astype(o_ref.dtype)

    blk = (None, None, tr, cols)
    return pl.pallas_call(
        body, name=name, out_shape=jax.ShapeDtypeStruct(recv.shape, out_dtype),
        grid_spec=pltpu.PrefetchScalarGridSpec(
            num_scalar_prefetch=1, grid=(nd, hl, rows // tr),
            in_specs=[pl.BlockSpec(blk, lambda d, l, r, c_ref: (d, c_ref[0] * hl + l, r, 0)),
                      pl.BlockSpec(blk, lambda d, l, r, c_ref: (d, l, r, 0))],
            out_specs=pl.BlockSpec(blk, lambda d, l, r, c_ref: (d, l, r, 0))),
        compiler_params=_params(("parallel", "parallel", "parallel")),
    )(c_arr, grad, recv)


def sum_chips(name, recv, c_arr):
    _, hl, rows, cols = recv.shape
    tr = _tile_rows(rows, cols, 1024 * 1024)

    def body(c_ref, r_ref, o_ref):
        r = [r_ref[s].astype(F32) for s in range(N_CHIPS)]
        o_ref[...] = ((r[0] + r[1]) + r[2]) + r[3]

    return pl.pallas_call(
        body, name=name, out_shape=jax.ShapeDtypeStruct((2 * hl, rows, cols), F32),
        grid_spec=pltpu.PrefetchScalarGridSpec(
            num_scalar_prefetch=1, grid=(hl, rows // tr),
            in_specs=[pl.BlockSpec((N_CHIPS, None, tr, cols), lambda l, r, c_ref: (0, l, r, 0))],
            out_specs=pl.BlockSpec((None, tr, cols), lambda l, r, c_ref: (c_ref[0] * hl + l, r, 0))),
        compiler_params=_params(("parallel", "parallel")),
    )(c_arr, recv)


def _pack(arrs, rows_multiple):
    flat = []
    for a in arrs:
        v = a.reshape(-1).astype(F32)
        flat.append(jnp.pad(v, (0, (-v.shape[0]) % HEAD)))
    v = jnp.concatenate(flat)
    v = jnp.pad(v, (0, (-v.shape[0]) % (HEAD * rows_multiple)))
    return v.reshape(-1, HEAD)


def _unpack(packed, shapes):
    flat = packed.reshape(-1)
    out, off = [], 0
    for shp in shapes:
        size = int(np.prod(shp))
        out.append(flat[off:off + size].reshape(shp))
        off += size + (-size) % HEAD
    return out


BIG = ["even_w_in", "even_w_out", "mla_w_down", "mla_w_qb", "mla_w_kvb", "mla_w_o", "mlp_w1", "mlp_w2"]
BIG_KEY = {"even_w_in": "w_in", "even_w_out": "w_out", "mla_w_down": "w_down", "mla_w_qb": "w_qb",
           "mla_w_kvb": "w_kvb", "mla_w_o": "w_o", "mlp_w1": "w1", "mlp_w2": "w2"}
SMALL_KEY = {"ln_mix_g": "ln_mix_g", "ln_mix_b": "ln_mix_b", "ln_ffn_g": "ln_ffn_g", "ln_ffn_b": "ln_ffn_b",
             "pool_w": "pool_w", "pool_scale": "pool_scale", "lru_conv_w": "conv_w", "lru_conv_b": "conv_b",
             "lru_w_a": "w_a", "lru_b_a": "b_a", "lru_w_x": "w_x", "lru_b_x": "b_x", "lru_lambda": "lam",
             "mla_q_norm_g": "gq", "mla_kv_norm_g": "gkv"}
SMALL = list(SMALL_KEY)
SMALL_SHARDED = ["lru_conv_w", "mla_q_norm_g", "mla_kv_norm_g"]
WEIGHTS = ["ln_mix_g", "ln_mix_b", "ln_ffn_g", "ln_ffn_b", "even_w_in", "pool_w", "pool_scale", "lru_conv_w",
           "lru_conv_b", "lru_w_a", "lru_b_a", "lru_w_x", "lru_b_x", "lru_lambda", "even_w_out", "mla_w_down",
           "mla_q_norm_g", "mla_kv_norm_g", "mla_w_qb", "mla_w_kvb", "mla_w_o", "mlp_w1", "mlp_w2"]


def _pad_q_heads(w):
    lead = w.shape[:-1]
    w = w.reshape(lead + (2, QK_NOPE + QK_ROPE))
    w = jnp.pad(w, ((0, 0),) * len(lead) + ((0, 0), (0, QHEAD_PAD - QK_NOPE - QK_ROPE)))
    return w.reshape(lead + (2 * QHEAD_PAD,))


def _unpad_q_heads(g):
    lead = g.shape[:-1]
    return g.reshape(lead + (2, QHEAD_PAD))[..., :QK_NOPE + QK_ROPE].reshape(lead + (2 * (QK_NOPE + QK_ROPE),))


def _step(x, positions, loss_target, w, m, v):
    cx, cy, cc = lax.axis_index("x"), lax.axis_index("y"), lax.axis_index("c")
    chip = 2 * cx + cy
    c_arr = jnp.reshape(cc, (1,)).astype(jnp.int32)

    small_shard_shapes = [w[k].shape for k in SMALL_SHARDED]
    small_pack = _pack([w[k] for k in SMALL_SHARDED], 16).reshape(2, -1, HEAD)
    gathered = all_gather_shards([w[k].astype(BF16) for k in BIG] + [small_pack])
    wg = {BIG_KEY[k]: g for k, g in zip(BIG, gathered)}
    wg["w_down"] = jnp.pad(wg["w_down"], ((0, 0),) * 3 + ((0, ODD_IN_PAD - ODD_IN),))
    wg["w_qb"] = _pad_q_heads(wg["w_qb"])
    per_chip = [_unpack(gathered[-1][s], small_shard_shapes) for s in range(N_CHIPS)]
    sm = {SMALL_KEY[k]: w[k] for k in SMALL if k not in SMALL_SHARDED}
    sm["conv_w"] = jnp.concatenate([p[0] for p in per_chip], axis=-1)
    sm["gq"] = jnp.concatenate([p[1] for p in per_chip], axis=-1)
    sm["gkv"] = jnp.concatenate([p[2] for p in per_chip], axis=-1)

    loss, grad_x, gw, gs = _local_step(x[0], positions.reshape(S, 1), loss_target[0], wg, sm)
    loss = lax.psum(loss, ("x", "y", "c"))

    small_shapes = [gs[SMALL_KEY[k]].shape for k in SMALL]
    gs_pack = _pack([gs[SMALL_KEY[k]] for k in SMALL], 32)
    grads = [gw[BIG_KEY[k]] for k in BIG] + [gs_pack.reshape(1, 2, -1, HEAD)]
    names = [BIG_KEY[k] for k in BIG] + ["small"]
    recv = sibling_send_halves(grads)
    parts = [add_own_half("add_" + n, g, r, c_arr, F32 if n == "small" else BF16)
             for n, g, r in zip(names, grads, recv)]
    landed = chip_exchange(parts)
    halves = [sum_chips("sum_" + n, r, c_arr) for n, r in zip(names, landed)]
    reduced = sibling_swap_halves(halves)
    g_big = dict(zip(BIG, reduced[:-1]))
    g_big["mla_w_down"] = g_big["mla_w_down"][..., :ODD_IN]
    g_big["mla_w_qb"] = _unpad_q_heads(g_big["mla_w_qb"])
    g_small = dict(zip(SMALL, _unpack(reduced[-1], small_shapes)))
    for k in SMALL_SHARDED:
        width = w[k].shape[-1]
        g_small[k] = lax.dynamic_slice_in_dim(g_small[k], chip * width, width, axis=-1)
    grad = {**g_big, **g_small}

    delta, new_m, new_v = {}, {}, {}
    for k in BIG:
        shp = w[k].shape
        view = lambda a: a.reshape(-1, shp[-1])
        d, nm, nv = adamw("adamw_" + BIG_KEY[k], view(w[k]), view(grad[k]), view(m[k]), view(v[k]))
        delta[k], new_m[k], new_v[k] = d.reshape(shp), nm.reshape(shp), nv.reshape(shp)
    shapes = [w[k].shape for k in SMALL]
    d, nm, nv = adamw("adamw_small", *[_pack([t[k] for k in SMALL], 512) for t in (w, grad, m, v)])
    for k, dk, mk, vk in zip(SMALL, _unpack(d, shapes), _unpack(nm, shapes), _unpack(nv, shapes)):
        delta[k], new_m[k], new_v[k] = dk, mk, vk
    return (loss, grad_x[None], *[grad[k] for k in WEIGHTS], *[delta[k] for k in WEIGHTS],
            *[new_m[k] for k in WEIGHTS], *[new_v[k] for k in WEIGHTS])


def kernel(x, positions, ln_mix_g, ln_mix_b, ln_ffn_g, ln_ffn_b, even_w_in, pool_w, pool_scale, lru_conv_w, lru_conv_b, lru_w_a, lru_b_a, lru_w_x, lru_b_x, lru_lambda, even_w_out, mla_w_down, mla_q_norm_g, mla_kv_norm_g, mla_w_qb, mla_w_kvb, mla_w_o, mlp_w1, mlp_w2, loss_target, m_ln_mix_g, m_ln_mix_b, m_ln_ffn_g, m_ln_ffn_b, m_even_w_in, m_pool_w, m_pool_scale, m_lru_conv_w, m_lru_conv_b, m_lru_w_a, m_lru_b_a, m_lru_w_x, m_lru_b_x, m_lru_lambda, m_even_w_out, m_mla_w_down, m_mla_q_norm_g, m_mla_kv_norm_g, m_mla_w_qb, m_mla_w_kvb, m_mla_w_o, m_mlp_w1, m_mlp_w2, v_ln_mix_g, v_ln_mix_b, v_ln_ffn_g, v_ln_ffn_b, v_even_w_in, v_pool_w, v_pool_scale, v_lru_conv_w, v_lru_conv_b, v_lru_w_a, v_lru_b_a, v_lru_w_x, v_lru_b_x, v_lru_lambda, v_even_w_out, v_mla_w_down, v_mla_q_norm_g, v_mla_kv_norm_g, v_mla_w_qb, v_mla_w_kvb, v_mla_w_o, v_mlp_w1, v_mlp_w2):
    args = locals()
    w = {k: args[k] for k in WEIGHTS}
    m = {k: args["m_" + k] for k in WEIGHTS}
    v = {k: args["v_" + k] for k in WEIGHTS}
    return _step(x, positions, loss_target, w, m, v)
```

```python
import functools
import math

import jax
import jax.numpy as jnp
import numpy as np
from jax import lax
from jax.experimental import pallas as pl
from jax.experimental.pallas import tpu as pltpu

F32 = jnp.float32
BF16 = jnp.bfloat16

S = 2048
D = 1024
DEPTH = 4
N_CHIPS = 4
POOL_WINDOWS = (2, 4, 8, 16)
POOL_W = 512
LRU_W = 1024
LRU_HEADS = 8
HEAD = 128
EVEN_IN = 2560
EVEN_MIX = 1536
MLA_HEADS = 8
QK_NOPE = 128
QK_ROPE = 64
Q_RANK = 384
KV_RANK = 256
ODD_IN = 704
ODD_IN_PAD = 768
QHEAD_PAD = 256
D_FF = 4096
CHUNK = 64
ALPHA = (2 * DEPTH) ** 0.25
LN_EPS = 1e-5
RMS_EPS = 1e-6
ATT_SCALE = (QK_NOPE + QK_ROPE) ** -0.5
ROPE_THETA = 10000.0
LRU_C = 8.0
ADAM_LR = 0.001
ADAM_B1 = 0.9
ADAM_B2 = 0.999
ADAM_EPS = 1e-08
ADAM_WD = 0.01
ADAM_STEP = 10

VMEM_LIMIT = 56 * 1024 * 1024
MESH = pl.DeviceIdType.MESH

_NN = (((1,), (0,)), ((), ()))
_NT = (((1,), (1,)), ((), ()))
_TN = (((0,), (0,)), ((), ()))


def _params(sem=None, **kw):
    return pltpu.CompilerParams(dimension_semantics=sem, vmem_limit_bytes=VMEM_LIMIT, **kw)


def _dot(a, b, dims):
    return lax.dot_general(a.astype(BF16), b.astype(BF16), dims, preferred_element_type=F32)


def _mm(name, a, w4, layer, *, dims, a_block, a_map, w_map, grid, sem, outs, out_blocks, out_maps,
        red_axis=None, n_red=1, extras=(), extra_blocks=(), extra_maps=(), epilogue=None):
    n_extra = len(extras)
    n_out = len(outs)
    w_block = (None, None) + tuple(w4.shape[2:])

    def body(a_ref, w_ref, *rest):
        extra_refs = rest[:n_extra]
        out_refs = rest[n_extra:n_extra + n_out]
        p = _dot(a_ref[...], w_ref[...], dims)

        def finish(acc):
            vals = epilogue(acc, *[r[...] for r in extra_refs]) if epilogue else (acc,)
            for r, v in zip(out_refs, vals):
                r[...] = v.astype(r.dtype)

        if red_axis is None:
            finish(p)
        else:
            acc_ref = rest[-1]
            k = pl.program_id(red_axis)

            @pl.when(k == 0)
            def _():
                acc_ref[...] = p

            @pl.when(k > 0)
            def _():
                acc_ref[...] += p

            @pl.when(k == n_red - 1)
            def _():
                finish(acc_ref[...])

    scratch = [] if red_axis is None else [pltpu.VMEM(out_blocks[0], F32)]
    return pl.pallas_call(
        body, name=name, grid=grid,
        in_specs=[pl.BlockSpec(a_block, a_map), pl.BlockSpec(w_block, functools.partial(w_map, layer))]
        + [pl.BlockSpec(b, m) for b, m in zip(extra_blocks, extra_maps)],
        out_specs=[pl.BlockSpec(b, m) for b, m in zip(out_blocks, out_maps)],
        out_shape=outs, scratch_shapes=scratch, compiler_params=_params(sem),
    )(a, w4, *extras)


def _w_map_outer(layer, s, i):
    return (s, layer, 0, 0)


def _w_map_inner(layer, i, s):
    return (s, layer, 0, 0)


def mm_cols(name, a, w4, layer, out_dtypes, *, tm=512, epilogue=None):
    m, k = a.shape
    nb = w4.shape[3]
    outs = [jax.ShapeDtypeStruct((m, N_CHIPS * nb), dt) for dt in out_dtypes]
    return _mm(name, a, w4, layer, dims=_NN, a_block=(tm, k), a_map=lambda s, i: (i, 0), w_map=_w_map_outer,
               grid=(N_CHIPS, m // tm), sem=("parallel", "parallel"), outs=outs,
               out_blocks=[(tm, nb)] * len(outs), out_maps=[lambda s, i: (i, s)] * len(outs), epilogue=epilogue)


def mm_rows(name, a, w4, layer, out_dtype, *, tm=512):
    m = a.shape[0]
    kb, n = w4.shape[2:]
    outs = [jax.ShapeDtypeStruct((m, n), out_dtype)]
    return _mm(name, a, w4, layer, dims=_NN, a_block=(tm, kb), a_map=lambda i, s: (i, s), w_map=_w_map_inner,
               grid=(m // tm, N_CHIPS), sem=("parallel", "arbitrary"), outs=outs, out_blocks=[(tm, n)],
               out_maps=[lambda i, s: (i, 0)], red_axis=1, n_red=N_CHIPS)[0]


def mm_t_cols(name, g, w4, layer, out_dtype, *, tm=512, resid=None):
    m = g.shape[0]
    k, nb = w4.shape[2:]
    outs = [jax.ShapeDtypeStruct((m, k), out_dtype)]
    extras = () if resid is None else (resid,)
    epi = None if resid is None else (lambda acc, r: (acc + ALPHA * r,))
    return _mm(name, g, w4, layer, dims=_NT, a_block=(tm, nb), a_map=lambda i, s: (i, s), w_map=_w_map_inner,
               grid=(m // tm, N_CHIPS), sem=("parallel", "arbitrary"), outs=outs, out_blocks=[(tm, k)],
               out_maps=[lambda i, s: (i, 0)], red_axis=1, n_red=N_CHIPS, extras=extras,
               extra_blocks=[(tm, k)] * len(extras), extra_maps=[lambda i, s: (i, 0)] * len(extras), epilogue=epi)[0]


def mm_t_rows(name, g, w4, layer, out_dtype, *, tm=512, resid=None, gate=None):
    m, n = g.shape
    kb = w4.shape[2]
    outs = [jax.ShapeDtypeStruct((m, N_CHIPS * kb), out_dtype)]
    extras, epi = (), None
    if resid is not None:
        extras, epi = (resid,), (lambda acc, r: (acc + ALPHA * r,))
    if gate is not None:
        extras, epi = (gate,), (lambda acc, r: (acc * (2.0 * r.astype(F32)),))
    return _mm(name, g, w4, layer, dims=_NT, a_block=(tm, n), a_map=lambda s, i: (i, 0), w_map=_w_map_outer,
               grid=(N_CHIPS, m // tm), sem=("parallel", "parallel"), outs=outs, out_blocks=[(tm, kb)],
               out_maps=[lambda s, i: (i, s)], extras=extras, extra_blocks=[(tm, kb)] * len(extras),
               extra_maps=[lambda s, i: (i, s)] * len(extras), epilogue=epi)[0]


def mm_grad(name, a, g, layer, n_layers, prev, *, a_sharded, tm=512):
    m = a.shape[0]
    ka = a.shape[1] // N_CHIPS if a_sharded else a.shape[1]
    ng = g.shape[1] if a_sharded else g.shape[1] // N_CHIPS
    n_red = m // tm
    a_map = (lambda s, i: (i, s)) if a_sharded else (lambda s, i: (i, 0))
    g_map = (lambda s, i: (i, 0)) if a_sharded else (lambda s, i: (i, s))

    def body(a_ref, g_ref, *rest):
        out_ref = rest[-1]
        p = _dot(a_ref[...], g_ref[...], _TN)
        k = pl.program_id(1)

        @pl.when(k == 0)
        def _():
            out_ref[...] = p

        @pl.when(k > 0)
        def _():
            out_ref[...] += p

    in_specs = [pl.BlockSpec((tm, ka), a_map), pl.BlockSpec((tm, ng), g_map)]
    args = [a, g]
    aliases = {}
    if prev is not None:
        in_specs.append(pl.BlockSpec(memory_space=pl.ANY))
        args.append(prev)
        aliases = {2: 0}
    return pl.pallas_call(
        body, name=name, grid=(N_CHIPS, n_red), in_specs=in_specs,
        out_specs=pl.BlockSpec((None, None, ka, ng), lambda s, i: (s, layer, 0, 0)),
        out_shape=jax.ShapeDtypeStruct((N_CHIPS, n_layers, ka, ng), F32),
        input_output_aliases=aliases, compiler_params=_params(("parallel", "arbitrary")),
    )(*args)


def ln_fwd(name, x, s, g, b, *, tm=256):
    def body(x_ref, s_ref, g_ref, b_ref, y_ref, xhat_ref, rstd_ref):
        z = ALPHA * x_ref[...] + s_ref[...]
        mu = jnp.mean(z, axis=-1, keepdims=True)
        zc = z - mu
        var = jnp.mean(zc * zc, axis=-1, keepdims=True)
        rstd = lax.rsqrt(var + LN_EPS)
        xhat = zc * rstd
        y_ref[...] = xhat * g_ref[...] + b_ref[...]
        xhat_ref[...] = xhat
        rstd_ref[...] = rstd

    row = pl.BlockSpec((tm, D), lambda i: (i, 0))
    vec = pl.BlockSpec((1, D), lambda i: (0, 0))
    return pl.pallas_call(
        body, name=name, grid=(S // tm,), in_specs=[row, row, vec, vec],
        out_specs=[row, row, pl.BlockSpec((tm, 1), lambda i: (i, 0))],
        out_shape=[jax.ShapeDtypeStruct((S, D), F32), jax.ShapeDtypeStruct((S, D), F32),
                   jax.ShapeDtypeStruct((S, 1), F32)],
        compiler_params=_params(("parallel",)),
    )(x, s, g, b)


def ln_bwd(name, dy, xhat, rstd, g, *, tm=256):
    def body(dy_ref, xhat_ref, rstd_ref, g_ref, dz_ref, dg_ref, db_ref):
        dy_v = dy_ref[...]
        xh = xhat_ref[...]
        dxh = dy_v * g_ref[...]
        m1 = jnp.mean(dxh, axis=-1, keepdims=True)
        m2 = jnp.mean(dxh * xh, axis=-1, keepdims=True)
        dz_ref[...] = rstd_ref[...] * (dxh - m1 - xh * m2)
        pg = jnp.sum(dy_v * xh, axis=0, keepdims=True)
        pb = jnp.sum(dy_v, axis=0, keepdims=True)
        i = pl.program_id(0)

        @pl.when(i == 0)
        def _():
            dg_ref[...] = pg
            db_ref[...] = pb

        @pl.when(i > 0)
        def _():
            dg_ref[...] += pg
            db_ref[...] += pb

    row = pl.BlockSpec((tm, D), lambda i: (i, 0))
    vec = pl.BlockSpec((1, D), lambda i: (0, 0))
    return pl.pallas_call(
        body, name=name, grid=(S // tm,), in_specs=[row, row, pl.BlockSpec((tm, 1), lambda i: (i, 0)), vec],
        out_specs=[row, vec, vec],
        out_shape=[jax.ShapeDtypeStruct((S, D), F32), jax.ShapeDtypeStruct((1, D), F32),
                   jax.ShapeDtypeStruct((1, D), F32)],
        compiler_params=_params(("arbitrary",)),
    )(dy, xhat, rstd, g)


def loss_head(y, target, *, tm=256):
    def body(y_ref, t_ref, dy_ref, loss_ref):
        e = y_ref[...] - t_ref[...]
        dy_ref[...] = e * (1.0 / D)
        part = jnp.sum(jnp.sum(e * e, axis=-1, keepdims=True), axis=0, keepdims=True) * (0.5 / D)
        i = pl.program_id(0)

        @pl.when(i == 0)
        def _():
            loss_ref[...] = jnp.zeros_like(loss_ref)

        loss_ref[...] += jnp.broadcast_to(part, loss_ref.shape)

    row = pl.BlockSpec((tm, D), lambda i: (i, 0))
    return pl.pallas_call(
        body, name="loss_head", grid=(S // tm,), in_specs=[row, row],
        out_specs=[row, pl.BlockSpec((8, 128), lambda i: (0, 0))],
        out_shape=[jax.ShapeDtypeStruct((S, D), F32), jax.ShapeDtypeStruct((8, 128), F32)],
        compiler_params=_params(("arbitrary",)),
    )(y, target)


def _rows(shape):
    return lax.broadcasted_iota(jnp.int32, shape, 0)


def _shift_down(x, k):
    return jnp.where(_rows(x.shape) >= k, pltpu.roll(x, k, 0), 0.0)


def _shift_up(x, k):
    n = x.shape[0]
    return jnp.where(_rows(x.shape) < n - k, pltpu.roll(x, n - k, 0), 0.0)


def _pool_diff(u, w):
    acc, k = u, 1
    while k < w:
        acc = acc + _shift_down(acc, k)
        k *= 2
    cnt = jnp.minimum(_rows(u.shape) + 1, w).astype(F32)
    return acc / cnt - u, cnt


def pool_fwd(name, proj, pool_w, pool_scale):
    def body(u_ref, w_ref, sc_ref, y_ref):
        for g, w in enumerate(POOL_WINDOWS):
            cols = slice(g * HEAD, (g + 1) * HEAD)
            d, _ = _pool_diff(u_ref[:, cols], w)
            z = _dot(d, w_ref[g], _NN)
            y_ref[:, cols] = (z * sc_ref[:, cols]).astype(y_ref.dtype)

    return pl.pallas_call(
        body, name=name, grid=(1,),
        in_specs=[pl.BlockSpec((S, POOL_W), lambda i: (0, 0)),
                  pl.BlockSpec((4, HEAD, HEAD), lambda i: (0, 0, 0)),
                  pl.BlockSpec((1, POOL_W), lambda i: (0, 0))],
        out_specs=pl.BlockSpec((S, POOL_W), lambda i: (0, 0)),
        out_shape=jax.ShapeDtypeStruct((S, POOL_W), BF16),
        compiler_params=_params(("arbitrary",)),
    )(proj, pool_w, pool_scale)


def pool_bwd(name, proj, dycat, pool_w, pool_scale):
    def body(u_ref, dy_ref, w_ref, sc_ref, du_ref, dw_ref, dsc_ref):
        for g, w in enumerate(POOL_WINDOWS):
            cols = slice(g * HEAD, (g + 1) * HEAD)
            d, cnt = _pool_diff(u_ref[:, cols], w)
            dy = dy_ref[:, cols]
            z = _dot(d, w_ref[g], _NN)
            dsc_ref[:, cols] = jnp.sum(dy * z, axis=0, keepdims=True)
            dz = dy * sc_ref[:, cols]
            dw_ref[g] = _dot(d, dz, _TN)
            dd = _dot(dz, w_ref[g], _NT)
            acc, k = dd / cnt, 1
            while k < w:
                acc = acc + _shift_up(acc, k)
                k *= 2
            du_ref[:, cols] = (acc - dd).astype(du_ref.dtype)

    return pl.pallas_call(
        body, name=name, grid=(1,),
        in_specs=[pl.BlockSpec((S, POOL_W), lambda i: (0, 0)),
                  pl.BlockSpec((S, POOL_W), lambda i: (0, 0)),
                  pl.BlockSpec((4, HEAD, HEAD), lambda i: (0, 0, 0)),
                  pl.BlockSpec((1, POOL_W), lambda i: (0, 0))],
        out_specs=[pl.BlockSpec((S, POOL_W), lambda i: (0, 0)),
                   pl.BlockSpec((4, HEAD, HEAD), lambda i: (0, 0, 0)),
                   pl.BlockSpec((1, POOL_W), lambda i: (0, 0))],
        out_shape=[jax.ShapeDtypeStruct((S, POOL_W), BF16), jax.ShapeDtypeStruct((4, HEAD, HEAD), F32),
                   jax.ShapeDtypeStruct((1, POOL_W), F32)],
        compiler_params=_params(("arbitrary",)),
    )(proj, dycat, pool_w, pool_scale)


def _expm1(x):
    series = x * (1.0 + x * (0.5 + x * (1.0 / 6.0 + x * (1.0 / 24.0 + x * (1.0 / 120.0)))))
    return jnp.where(jnp.abs(x) < 0.05, series, jnp.exp(x) - 1.0)


def _softplus_neg(lam):
    e = jnp.exp(-jnp.abs(lam))
    log1p = jnp.where(e < 0.01, e * (1.0 - e * (0.5 - e * (1.0 / 3.0))), jnp.log(1.0 + e))
    return jnp.maximum(-lam, 0.0) + log1p


_GELU_C = math.sqrt(2.0 / math.pi)


def _gelu(x):
    t = jnp.tanh(_GELU_C * (x + 0.044715 * x * x * x))
    return 0.5 * x * (1.0 + t), t


def _gelu_grad(x, t):
    return 0.5 * (1.0 + t) + 0.5 * x * (1.0 - t * t) * _GELU_C * (1.0 + 3.0 * 0.044715 * x * x)


def _conv(u, cw, cb):
    return cw[3:4] * u + cw[2:3] * _shift_down(u, 1) + cw[1:2] * _shift_down(u, 2) + cw[0:1] * _shift_down(u, 3) + cb


def _lru_gates(cu, wa, ba, wx, bx, lam):
    r = jax.nn.sigmoid(_dot(cu, wa, _NN) + ba)
    i = jax.nn.sigmoid(_dot(cu, wx, _NN) + bx)
    sp = _softplus_neg(lam)
    log_a = (-LRU_C) * r * sp
    a = jnp.exp(log_a)
    mult = jnp.sqrt(-_expm1(2.0 * log_a))
    return r, i, sp, a, mult


def _scan(a_ref, b_ref, h_ref, *, reverse):
    n_blk = S // 8
    row8 = lax.broadcasted_iota(jnp.int32, (8, HEAD), 0)

    def step(j, carry):
        blk = (n_blk - 1 - j) if reverse else j
        r0 = pl.multiple_of(blk * 8, 8)
        a = a_ref[pl.ds(r0, 8), :]
        b = b_ref[pl.ds(r0, 8), :]
        for k in (1, 2, 4):
            if reverse:
                keep = row8 < 8 - k
                a_s, b_s = pltpu.roll(a, 8 - k, 0), pltpu.roll(b, 8 - k, 0)
            else:
                keep = row8 >= k
                a_s, b_s = pltpu.roll(a, k, 0), pltpu.roll(b, k, 0)
            b = jnp.where(keep, a * b_s + b, b)
            a = jnp.where(keep, a * a_s, a)
        h = b + a * carry
        h_ref[pl.ds(r0, 8), :] = h
        edge = h[0:1, :] if reverse else h[7:8, :]
        return jnp.broadcast_to(edge, (8, HEAD))

    lax.fori_loop(0, n_blk, step, jnp.zeros((8, HEAD), F32), unroll=4)


def _lru_specs():
    def col(off):
        return pl.BlockSpec((S, HEAD), lambda h: (0, off + h))
    vec = pl.BlockSpec((1, HEAD), lambda h: (0, h))
    mat = pl.BlockSpec((None, HEAD, HEAD), lambda h: (h, 0, 0))
    cw = pl.BlockSpec((4, HEAD), lambda h: (0, h))
    return col, vec, mat, cw


def lru_fwd(name, proj, conv_w, conv_b, w_a, b_a, w_x, b_x, lam):
    def body(u_ref, ug_ref, cw_ref, cb_ref, wa_ref, ba_ref, wx_ref, bx_ref, lam_ref, y_ref, h_ref, a_s, b_s):
        cu = _conv(u_ref[...], cw_ref[...], cb_ref[...])
        _, i, _, a, mult = _lru_gates(cu, wa_ref[...], ba_ref[...], wx_ref[...], bx_ref[...], lam_ref[...])
        a_s[...] = a
        b_s[...] = mult * (i * cu)
        _scan(a_s, b_s, h_ref, reverse=False)
        gl, _ = _gelu(ug_ref[...])
        y_ref[...] = (h_ref[...] * gl).astype(y_ref.dtype)

    col, vec, mat, cw = _lru_specs()
    out = pl.BlockSpec((S, HEAD), lambda h: (0, h))
    return pl.pallas_call(
        body, name=name, grid=(LRU_HEADS,),
        in_specs=[col(4), col(12), cw, vec, mat, vec, mat, vec, vec],
        out_specs=[out, out],
        out_shape=[jax.ShapeDtypeStruct((S, LRU_W), BF16), jax.ShapeDtypeStruct((S, LRU_W), F32)],
        scratch_shapes=[pltpu.VMEM((S, HEAD), F32), pltpu.VMEM((S, HEAD), F32)],
        compiler_params=_params(("parallel",)),
    )(proj, proj, conv_w, conv_b, w_a, b_a, w_x, b_x, lam)


def lru_bwd(name, proj, hstate, dycat, conv_w, conv_b, w_a, b_a, w_x, b_x, lam):
    def body(u_ref, ug_ref, h_ref, dy_ref, cw_ref, cb_ref, wa_ref, ba_ref, wx_ref, bx_ref, lam_ref,
             du_ref, dug_ref, dwa_ref, dwx_ref, dba_ref, dbx_ref, dlam_ref, dcw_ref, dcb_ref, a_s, b_s, g_s):
        u = u_ref[...]
        cw = cw_ref[...]
        cu = _conv(u, cw, cb_ref[...])
        lam_v = lam_ref[...]
        r, i, sp, a, mult = _lru_gates(cu, wa_ref[...], ba_ref[...], wx_ref[...], bx_ref[...], lam_v)
        ug = ug_ref[...]
        gl, t = _gelu(ug)
        dy = dy_ref[...]
        h = h_ref[...]
        dug_ref[...] = (dy * h * _gelu_grad(ug, t)).astype(dug_ref.dtype)
        a_s[...] = _shift_up(a, 1)
        b_s[...] = dy * gl
        _scan(a_s, b_s, g_s, reverse=True)
        dxin = g_s[...]
        da = dxin * _shift_down(h, 1)
        dmult = dxin * (i * cu)
        di = dxin * (mult * cu)
        dlog_a = da * a - dmult * (a * a) / mult
        dr_pre = dlog_a * ((-LRU_C) * sp) * (r * (1.0 - r))
        di_pre = di * (i * (1.0 - i))
        dsp = jnp.sum(dlog_a * ((-LRU_C) * r), axis=0, keepdims=True)
        dlam_ref[...] = dsp * (-jax.nn.sigmoid(-lam_v))
        dba_ref[...] = jnp.sum(dr_pre, axis=0, keepdims=True)
        dbx_ref[...] = jnp.sum(di_pre, axis=0, keepdims=True)
        dwa_ref[...] = _dot(cu, dr_pre, _TN)
        dwx_ref[...] = _dot(cu, di_pre, _TN)
        dcu = dxin * (mult * i) + _dot(dr_pre, wa_ref[...], _NT) + _dot(di_pre, wx_ref[...], _NT)
        dcb_ref[...] = jnp.sum(dcu, axis=0, keepdims=True)
        for k in range(4):
            dcw_ref[k:k + 1, :] = jnp.sum(dcu * (_shift_down(u, 3 - k) if k < 3 else u), axis=0, keepdims=True)
        du = cw[3:4] * dcu + cw[2:3] * _shift_up(dcu, 1) + cw[1:2] * _shift_up(dcu, 2) + cw[0:1] * _shift_up(dcu, 3)
        du_ref[...] = du.astype(du_ref.dtype)

    col, vec, mat, cw = _lru_specs()
    out = pl.BlockSpec((S, HEAD), lambda h: (0, h))
    big = jax.ShapeDtypeStruct((S, LRU_W), BF16)
    vec_shape = jax.ShapeDtypeStruct((1, LRU_W), F32)
    mat_shape = jax.ShapeDtypeStruct((LRU_HEADS, HEAD, HEAD), F32)
    return pl.pallas_call(
        body, name=name, grid=(LRU_HEADS,),
        in_specs=[col(4), col(12), out, col(4), cw, vec, mat, vec, mat, vec, vec],
        out_specs=[out, out, mat, mat, vec, vec, vec, cw, vec],
        out_shape=[big, big, mat_shape, mat_shape, vec_shape, vec_shape, vec_shape,
                   jax.ShapeDtypeStruct((4, LRU_W), F32), vec_shape],
        scratch_shapes=[pltpu.VMEM((S, HEAD), F32)] * 3,
        compiler_params=_params(("parallel",)),
    )(proj, proj, hstate, dycat, conv_w, conv_b, w_a, b_a, w_x, b_x, lam)


def rope_tables(pos_col, inv_freq):
    def body(pos_ref, f_ref, c_ref, s1_ref, s2_ref):
        ang = pos_ref[...].astype(F32) * f_ref[...]
        lane = lax.broadcasted_iota(jnp.int32, ang.shape, 1)
        cos, sin = jnp.cos(ang), jnp.sin(ang)
        c_ref[...] = jnp.where(lane < QK_ROPE, cos, 0.0)
        s1_ref[...] = jnp.where(lane < QK_ROPE // 2, -sin, 0.0)
        s2_ref[...] = jnp.where((lane >= QK_ROPE // 2) & (lane < QK_ROPE), sin, 0.0)

    tab = jax.ShapeDtypeStruct((S, HEAD), F32)
    return pl.pallas_call(
        body, name="rope_tables", grid=(1,),
        in_specs=[pl.BlockSpec((S, 1), lambda i: (0, 0)), pl.BlockSpec((1, HEAD), lambda i: (0, 0))],
        out_specs=[pl.BlockSpec((S, HEAD), lambda i: (0, 0))] * 3, out_shape=[tab, tab, tab],
        compiler_params=_params(("arbitrary",)),
    )(pos_col, inv_freq)


def _rope(v, c, s1, s2):
    return v * c + pltpu.roll(v, HEAD - QK_ROPE // 2, 1) * s1 + pltpu.roll(v, QK_ROPE // 2, 1) * s2


def _unrope(d, c, s1, s2):
    return d * c + pltpu.roll(d * s1, QK_ROPE // 2, 1) + pltpu.roll(d * s2, HEAD - QK_ROPE // 2, 1)


def _rms(x, g):
    rstd = lax.rsqrt(jnp.mean(x * x, axis=-1, keepdims=True) + RMS_EPS)
    return x * rstd, rstd


def mla_prep(name, down, gq, gkv, tabs, *, tm=256):
    def body(dn_ref, gq_ref, gkv_ref, c_ref, s1_ref, s2_ref, cq_ref, ckv_ref, kp_ref):
        xq, _ = _rms(dn_ref[:, :Q_RANK], None)
        cq_ref[...] = (xq * gq_ref[...]).astype(cq_ref.dtype)
        xkv, _ = _rms(dn_ref[:, Q_RANK:Q_RANK + KV_RANK], None)
        ckv_ref[...] = (xkv * gkv_ref[...]).astype(ckv_ref.dtype)
        kp = _rope(dn_ref[:, Q_RANK + KV_RANK:], c_ref[...], s1_ref[...], s2_ref[...])
        kp_ref[...] = kp.astype(kp_ref.dtype)

    tab = pl.BlockSpec((tm, HEAD), lambda i: (i, 0))
    return pl.pallas_call(
        body, name=name, grid=(S // tm,),
        in_specs=[pl.BlockSpec((tm, ODD_IN_PAD), lambda i: (i, 0)), pl.BlockSpec((1, Q_RANK), lambda i: (0, 0)),
                  pl.BlockSpec((1, KV_RANK), lambda i: (0, 0)), tab, tab, tab],
        out_specs=[pl.BlockSpec((tm, Q_RANK), lambda i: (i, 0)), pl.BlockSpec((tm, KV_RANK), lambda i: (i, 0)), tab],
        out_shape=[jax.ShapeDtypeStruct((S, Q_RANK), BF16), jax.ShapeDtypeStruct((S, KV_RANK), BF16),
                   jax.ShapeDtypeStruct((S, HEAD), BF16)],
        compiler_params=_params(("parallel",)),
    )(down, gq, gkv, *tabs)


def mla_prep_bwd(name, down, dcq, dckv, dkp, gq, gkv, tabs, *, tm=256):
    def body(dn_ref, dcq_ref, dckv_ref, dkp_ref, gq_ref, gkv_ref, c_ref, s1_ref, s2_ref, dd_ref, dgq_ref, dgkv_ref):
        i = pl.program_id(0)

        def rms_bwd(x, dy, g, dg_ref):
            xh, rstd = _rms(x, None)
            dxh = dy * g
            dx = rstd * (dxh - xh * jnp.mean(dxh * xh, axis=-1, keepdims=True))
            pg = jnp.sum(dy * xh, axis=0, keepdims=True)

            @pl.when(i == 0)
            def _():
                dg_ref[...] = pg

            @pl.when(i > 0)
            def _():
                dg_ref[...] += pg

            return dx

        dxq = rms_bwd(dn_ref[:, :Q_RANK], dcq_ref[...], gq_ref[...], dgq_ref)
        dd_ref[:, :Q_RANK] = dxq.astype(dd_ref.dtype)
        dxkv = rms_bwd(dn_ref[:, Q_RANK:Q_RANK + KV_RANK], dckv_ref[...], gkv_ref[...], dgkv_ref)
        dd_ref[:, Q_RANK:Q_RANK + KV_RANK] = dxkv.astype(dd_ref.dtype)
        dd_ref[:, Q_RANK + KV_RANK:] = _unrope(dkp_ref[...], c_ref[...], s1_ref[...], s2_ref[...]).astype(dd_ref.dtype)

    tab = pl.BlockSpec((tm, HEAD), lambda i: (i, 0))
    vq = pl.BlockSpec((1, Q_RANK), lambda i: (0, 0))
    vkv = pl.BlockSpec((1, KV_RANK), lambda i: (0, 0))
    return pl.pallas_call(
        body, name=name, grid=(S // tm,),
        in_specs=[pl.BlockSpec((tm, ODD_IN_PAD), lambda i: (i, 0)), pl.BlockSpec((tm, Q_RANK), lambda i: (i, 0)),
                  pl.BlockSpec((tm, KV_RANK), lambda i: (i, 0)), tab, vq, vkv, tab, tab, tab],
        out_specs=[pl.BlockSpec((tm, ODD_IN_PAD), lambda i: (i, 0)), vq, vkv],
        out_shape=[jax.ShapeDtypeStruct((S, ODD_IN_PAD), BF16), jax.ShapeDtypeStruct((1, Q_RANK), F32),
                   jax.ShapeDtypeStruct((1, KV_RANK), F32)],
        compiler_params=_params(("arbitrary",)),
    )(down, dcq, dckv, dkp, gq, gkv, *tabs)


ATT_TQ = 256


def _attn_probs(q_ref, kv_ref, kp_ref, c_ref, s1_ref, s2_ref, i, nk):
    qn = q_ref[:, :HEAD].astype(BF16)
    qp = _rope(q_ref[:, HEAD:], c_ref[...], s1_ref[...], s2_ref[...]).astype(BF16)
    kn = kv_ref[:nk, :HEAD]
    sc = (_dot(qn, kn, _NT) + _dot(qp, kp_ref[:nk, :], _NT)) * ATT_SCALE
    q_chunk = (i * ATT_TQ + lax.broadcasted_iota(jnp.int32, sc.shape, 0)) // CHUNK
    k_chunk = lax.broadcasted_iota(jnp.int32, sc.shape, 1) // CHUNK
    sc = jnp.where(k_chunk <= q_chunk, sc, jnp.finfo(F32).min)
    e = jnp.exp(sc - jnp.max(sc, axis=-1, keepdims=True))
    p = e * (1.0 / jnp.sum(e, axis=-1, keepdims=True))
    return p, qn, qp, kn


def _for_each_prefix(i, fn):
    for k in range(S // ATT_TQ):
        pl.when(i == k)(functools.partial(fn, (k + 1) * ATT_TQ))


def _attn_specs():
    q = pl.BlockSpec((ATT_TQ, QHEAD_PAD), lambda h, i: (i, h))
    kv = pl.BlockSpec((S, QHEAD_PAD), lambda h, i: (0, h))
    kp = pl.BlockSpec((S, HEAD), lambda h, i: (0, 0))
    tab = pl.BlockSpec((ATT_TQ, HEAD), lambda h, i: (i, 0))
    o = pl.BlockSpec((ATT_TQ, HEAD), lambda h, i: (i, h))
    return q, kv, kp, tab, o


def attn_fwd(name, q, kv, kp, tabs):
    def body(q_ref, kv_ref, kp_ref, c_ref, s1_ref, s2_ref, o_ref):
        i = pl.program_id(1)

        def run(nk):
            p, _, _, _ = _attn_probs(q_ref, kv_ref, kp_ref, c_ref, s1_ref, s2_ref, i, nk)
            o_ref[...] = _dot(p, kv_ref[:nk, HEAD:], _NN).astype(o_ref.dtype)

        _for_each_prefix(i, run)

    qs, kvs, kps, tab, os = _attn_specs()
    return pl.pallas_call(
        body, name=name, grid=(MLA_HEADS, S // ATT_TQ), in_specs=[qs, kvs, kps, tab, tab, tab], out_specs=os,
        out_shape=jax.ShapeDtypeStruct((S, MLA_HEADS * HEAD), BF16),
        compiler_params=_params(("parallel", "parallel")),
    )(q, kv, kp, *tabs)


def attn_bwd(name, q, kv, kp, do, tabs):
    def body(q_ref, kv_ref, kp_ref, do_ref, c_ref, s1_ref, s2_ref, dq_ref, dkv_ref, dkp_ref):
        h, i = pl.program_id(0), pl.program_id(1)

        @pl.when(i == 0)
        def _():
            dkv_ref[...] = jnp.zeros_like(dkv_ref)

        @pl.when((i == 0) & (h == 0))
        def _():
            dkp_ref[...] = jnp.zeros_like(dkp_ref)

        def run(nk):
            p, qn, qp, kn = _attn_probs(q_ref, kv_ref, kp_ref, c_ref, s1_ref, s2_ref, i, nk)
            do_v = do_ref[...]
            dp = _dot(do_v, kv_ref[:nk, HEAD:], _NT)
            ds = (p * (dp - jnp.sum(p * dp, axis=-1, keepdims=True)) * ATT_SCALE).astype(BF16)
            dq_ref[:, :HEAD] = _dot(ds, kn, _NN).astype(dq_ref.dtype)
            dqp = _unrope(_dot(ds, kp_ref[:nk, :], _NN), c_ref[...], s1_ref[...], s2_ref[...])
            dq_ref[:, HEAD:] = dqp.astype(dq_ref.dtype)
            dkv_ref[:nk, :HEAD] += _dot(ds, qn, _TN)
            dkv_ref[:nk, HEAD:] += _dot(p, do_v, _TN)
            dkp_ref[:nk, :] += _dot(ds, qp, _TN)

        _for_each_prefix(i, run)

    qs, kvs, kps, tab, os = _attn_specs()
    return pl.pallas_call(
        body, name=name, grid=(MLA_HEADS, S // ATT_TQ), in_specs=[qs, kvs, kps, os, tab, tab, tab],
        out_specs=[qs, kvs, kps],
        out_shape=[jax.ShapeDtypeStruct((S, MLA_HEADS * QHEAD_PAD), BF16),
                   jax.ShapeDtypeStruct((S, MLA_HEADS * QHEAD_PAD), F32), jax.ShapeDtypeStruct((S, HEAD), F32)],
        compiler_params=_params(("arbitrary", "arbitrary")),
    )(q, kv, kp, do, *tabs)


def adamw(name, w, g, m, v):
    rows, cols = w.shape
    tr = rows
    for cand in (512, 256, 128, 64, 32, 16, 8):
        if rows % cand == 0 and cand * cols * 4 <= 2 * 1024 * 1024:
            tr = cand
            break

    def body(w_ref, g_ref, m_ref, v_ref, d_ref, nm_ref, nv_ref):
        g_v = g_ref[...]
        nm = ADAM_B1 * m_ref[...] + (1.0 - ADAM_B1) * g_v
        nv = ADAM_B2 * v_ref[...] + (1.0 - ADAM_B2) * (g_v * g_v)
        m_hat = nm / (1.0 - ADAM_B1 ** ADAM_STEP)
        v_hat = nv / (1.0 - ADAM_B2 ** ADAM_STEP)
        d_ref[...] = (-ADAM_LR) * (m_hat / (jnp.sqrt(v_hat) + ADAM_EPS) + ADAM_WD * w_ref[...])
        nm_ref[...] = nm
        nv_ref[...] = nv

    blk = pl.BlockSpec((tr, cols), lambda i: (i, 0))
    shape = jax.ShapeDtypeStruct((rows, cols), F32)
    return pl.pallas_call(
        body, name=name, grid=(rows // tr,), in_specs=[blk] * 4, out_specs=[blk] * 3, out_shape=[shape] * 3,
        compiler_params=_params(("parallel",)),
    )(w, g, m, v)


def _local_step(x, pos_col, target, wg, sm):
    inv_freq = ROPE_THETA ** (-jnp.arange(0, QK_ROPE, 2, dtype=F32) / QK_ROPE)
    inv_freq = jnp.concatenate([inv_freq, inv_freq, jnp.zeros((HEAD - QK_ROPE,), F32)])[None, :]
    tabs = rope_tables(pos_col, inv_freq)
    saved = []
    for layer in range(DEPTH):
        j = layer // 2
        n = "l%d_" % layer
        sv = {"x": x}
        if layer % 2 == 0:
            proj = mm_cols(n + "proj", x, wg["w_in"], j, [F32])[0]
            y_pool = pool_fwd(n + "pool", proj, sm["pool_w"][j], sm["pool_scale"][j][None])
            y_lru, hstate = lru_fwd(n + "lru", proj, sm["conv_w"][j], sm["conv_b"][j][None], sm["w_a"][j],
                                    sm["b_a"][j][None], sm["w_x"][j], sm["b_x"][j][None], sm["lam"][j][None])
            ycat = jnp.concatenate([y_pool, y_lru], axis=1)
            mix = mm_rows(n + "mixout", ycat, wg["w_out"], j, F32)
            sv.update(proj=proj, hstate=hstate, ycat=ycat)
        else:
            down = mm_rows(n + "down", x, wg["w_down"], j, F32)
            cq, ckv, kp = mla_prep(n + "prep", down, sm["gq"][j][None], sm["gkv"][j][None], tabs)
            q = mm_cols(n + "q", cq, wg["w_qb"], j, [F32])[0]
            kv = mm_cols(n + "kv", ckv, wg["w_kvb"], j, [BF16])[0]
            o = attn_fwd(n + "attn", q, kv, kp, tabs)
            mix = mm_rows(n + "attnout", o, wg["w_o"], j, F32)
            sv.update(down=down, cq=cq, ckv=ckv, kp=kp, q=q, kv=kv, o=o)
        x1, xhat1, rstd1 = ln_fwd(n + "ln_mix", x, mix, sm["ln_mix_g"][layer][None], sm["ln_mix_b"][layer][None])
        act, relu = mm_cols(n + "mlp1", x1, wg["w1"], layer, [BF16, BF16],
                            epilogue=lambda acc: (jnp.square(jnp.maximum(acc, 0.0)), jnp.maximum(acc, 0.0)))
        mlp = mm_rows(n + "mlp2", act, wg["w2"], layer, F32)
        x2, xhat2, rstd2 = ln_fwd(n + "ln_ffn", x1, mlp, sm["ln_ffn_g"][layer][None], sm["ln_ffn_b"][layer][None])
        sv.update(xhat1=xhat1, rstd1=rstd1, x1=x1, act=act, relu=relu, xhat2=xhat2, rstd2=rstd2)
        saved.append(sv)
        x = x2

    dx, loss_tile = loss_head(x, target)
    gw = {k: None for k in wg}
    gs = {k: [None] * (DEPTH if k.startswith("ln_") else DEPTH // 2) for k in sm}
    for layer in reversed(range(DEPTH)):
        j = layer // 2
        n = "l%d_" % layer
        sv = saved[layer]
        dz, gs["ln_ffn_g"][layer], gs["ln_ffn_b"][layer] = ln_bwd(
            n + "ln_ffn_b", dx, sv["xhat2"], sv["rstd2"], sm["ln_ffn_g"][layer][None])
        dh = mm_t_rows(n + "mlp2_dx", dz, wg["w2"], layer, BF16, gate=sv["relu"])
        gw["w2"] = mm_grad(n + "mlp2_dw", sv["act"], dz, layer, DEPTH, gw["w2"], a_sharded=True)
        dx = mm_t_cols(n + "mlp1_dx", dh, wg["w1"], layer, F32, resid=dz)
        gw["w1"] = mm_grad(n + "mlp1_dw", sv["x1"], dh, layer, DEPTH, gw["w1"], a_sharded=False)
        dz, gs["ln_mix_g"][layer], gs["ln_mix_b"][layer] = ln_bwd(
            n + "ln_mix_b", dx, sv["xhat1"], sv["rstd1"], sm["ln_mix_g"][layer][None])
        if layer % 2 == 0:
            dycat = mm_t_rows(n + "mixout_dx", dz, wg["w_out"], j, F32)
            gw["w_out"] = mm_grad(n + "mixout_dw", sv["ycat"], dz, j, DEPTH // 2, gw["w_out"], a_sharded=True)
            du_pool, gs["pool_w"][j], gs["pool_scale"][j] = pool_bwd(
                n + "pool_b", sv["proj"], dycat, sm["pool_w"][j], sm["pool_scale"][j][None])
            (du_lru, du_gate, gs["w_a"][j], gs["w_x"][j], gs["b_a"][j], gs["b_x"][j], gs["lam"][j], gs["conv_w"][j],
             gs["conv_b"][j]) = lru_bwd(n + "lru_b", sv["proj"], sv["hstate"], dycat, sm["conv_w"][j],
                                        sm["conv_b"][j][None], sm["w_a"][j], sm["b_a"][j][None], sm["w_x"][j],
                                        sm["b_x"][j][None], sm["lam"][j][None])
            dproj = jnp.concatenate([du_pool, du_lru, du_gate], axis=1)
            gw["w_in"] = mm_grad(n + "proj_dw", sv["x"], dproj, j, DEPTH // 2, gw["w_in"], a_sharded=False)
            dx = mm_t_cols(n + "proj_dx", dproj, wg["w_in"], j, F32, resid=dz)
        else:
            do = mm_t_rows(n + "attnout_dx", dz, wg["w_o"], j, BF16)
            gw["w_o"] = mm_grad(n + "attnout_dw", sv["o"], dz, j, DEPTH // 2, gw["w_o"], a_sharded=True)
            dq, dkv, dkp = attn_bwd(n + "attn_b", sv["q"], sv["kv"], sv["kp"], do, tabs)
            gw["w_qb"] = mm_grad(n + "q_dw", sv["cq"], dq, j, DEPTH // 2, gw["w_qb"], a_sharded=False)
            dcq = mm_t_cols(n + "q_dx", dq, wg["w_qb"], j, F32)
            gw["w_kvb"] = mm_grad(n + "kv_dw", sv["ckv"], dkv, j, DEPTH // 2, gw["w_kvb"], a_sharded=False)
            dckv = mm_t_cols(n + "kv_dx", dkv, wg["w_kvb"], j, F32)
            ddown, gs["gq"][j], gs["gkv"][j] = mla_prep_bwd(
                n + "prep_b", sv["down"], dcq, dckv, dkp, sm["gq"][j][None], sm["gkv"][j][None], tabs)
            gw["w_down"] = mm_grad(n + "down_dw", sv["x"], ddown, j, DEPTH // 2, gw["w_down"], a_sharded=True)
            dx = mm_t_rows(n + "down_dx", ddown, wg["w_down"], j, F32, resid=dz)
    gs = {k: jnp.stack([a.reshape(sm[k].shape[1:]) for a in v]) for k, v in gs.items()}
    return loss_tile[0, 0], dx, gw, gs


def _place():
    x, y, c = lax.axis_index("x"), lax.axis_index("y"), lax.axis_index("c")
    chips = [(1 - x, y), (x, 1 - y), (1 - x, 1 - y)]
    return x, y, c, chips


def _hbm_call(body, name, args, out_shape, scratch, aliases=None):
    return pl.pallas_call(
        body, name=name, in_specs=[pl.BlockSpec(memory_space=pl.ANY)] * len(args),
        out_specs=[pl.BlockSpec(memory_space=pl.ANY)] * len(out_shape), out_shape=out_shape,
        scratch_shapes=scratch, input_output_aliases=aliases or {},
        compiler_params=pltpu.CompilerParams(has_side_effects=True),
    )(*args)


def all_gather_shards(arrs):
    n = len(arrs)

    def body(*refs):
        ins, outs = refs[:n], refs[n:2 * n]
        send_sems, recv_sems, local_sems = refs[2 * n:]
        x, y, c, chips = _place()
        me = 2 * x + y
        sibling = (x, y, 1 - c)

        def halves(k):
            hl = arrs[k].shape[0] // 2
            return pl.ds(c * hl, hl), pl.ds((1 - c) * hl, hl)

        def direct(k, j, to_chip, from_chip):
            mine, _ = halves(k)
            return pltpu.make_async_remote_copy(
                src_ref=ins[k].at[mine], dst_ref=outs[k].at[from_chip, mine], send_sem=send_sems.at[k, j],
                recv_sem=recv_sems.at[k, j], device_id=(*to_chip, c), device_id_type=MESH)

        def passed(k, j, chip_idx, half):
            return pltpu.make_async_remote_copy(
                src_ref=outs[k].at[chip_idx, half], dst_ref=outs[k].at[chip_idx, half], send_sem=send_sems.at[k, 3 + j],
                recv_sem=recv_sems.at[k, 3 + j], device_id=sibling, device_id_type=MESH)

        local = [pltpu.make_async_copy(ins[k], outs[k].at[me], local_sems.at[k]) for k in range(n)]
        for cp in local:
            cp.start()
        sends = [direct(k, j, chip, me) for k in range(n) for j, chip in enumerate(chips)]
        for cp in sends:
            cp.start()
        for k in range(n):
            mine, _ = halves(k)
            for j, (px, py) in enumerate(chips):
                direct(k, j, (px, py), 2 * px + py).wait_recv()
                fwd = passed(k, j, 2 * px + py, mine)
                fwd.start()
                sends.append(fwd)
        for k in range(n):
            _, other = halves(k)
            for j, (px, py) in enumerate(chips):
                passed(k, j, 2 * px + py, other).wait_recv()
        for cp in sends:
            cp.wait_send()
        for cp in local:
            cp.wait()

    out_shape = [jax.ShapeDtypeStruct((N_CHIPS,) + a.shape, a.dtype) for a in arrs]
    scratch = [pltpu.SemaphoreType.DMA((n, 6)), pltpu.SemaphoreType.DMA((n, 6)), pltpu.SemaphoreType.DMA((n,))]
    return _hbm_call(body, "all_gather_shards", arrs, out_shape, scratch)


def sibling_send_halves(grads):
    n = len(grads)

    def body(*refs):
        ins, outs = refs[:n], refs[n:2 * n]
        send_sems, recv_sems = refs[2 * n:]
        x, y, c, _ = _place()
        copies = []
        for k in range(n):
            nd, nl = grads[k].shape[:2]
            hl = nl // 2
            copies.append(pltpu.make_async_remote_copy(
                src_ref=ins[k].at[pl.ds(0, nd), pl.ds((1 - c) * hl, hl)], dst_ref=outs[k], send_sem=send_sems.at[k],
                recv_sem=recv_sems.at[k], device_id=(x, y, 1 - c), device_id_type=MESH))
        for cp in copies:
            cp.start()
        for cp in copies:
            cp.wait()

    out_shape = [jax.ShapeDtypeStruct((g.shape[0], g.shape[1] // 2) + g.shape[2:], g.dtype) for g in grads]
    scratch = [pltpu.SemaphoreType.DMA((n,)), pltpu.SemaphoreType.DMA((n,))]
    return _hbm_call(body, "sibling_send_halves", grads, out_shape, scratch)


def chip_exchange(parts):
    n = len(parts)

    def body(*refs):
        ins, outs = refs[:n], refs[n:2 * n]
        send_sems, recv_sems, local_sems = refs[2 * n:]
        x, y, c, chips = _place()
        me = 2 * x + y

        def slot(k, chip_idx):
            return ins[k].at[chip_idx] if parts[k].shape[0] == N_CHIPS else ins[k].at[0]

        def copy(k, j, to_chip, from_chip):
            return pltpu.make_async_remote_copy(
                src_ref=slot(k, 2 * to_chip[0] + to_chip[1]), dst_ref=outs[k].at[from_chip], send_sem=send_sems.at[k, j],
                recv_sem=recv_sems.at[k, j], device_id=(*to_chip, c), device_id_type=MESH)

        local = [pltpu.make_async_copy(slot(k, me), outs[k].at[me], local_sems.at[k]) for k in range(n)]
        for cp in local:
            cp.start()
        sends = [copy(k, j, chip, me) for k in range(n) for j, chip in enumerate(chips)]
        for cp in sends:
            cp.start()
        for k in range(n):
            for j, (px, py) in enumerate(chips):
                copy(k, j, (px, py), 2 * px + py).wait_recv()
        for cp in sends:
            cp.wait_send()
        for cp in local:
            cp.wait()

    out_shape = [jax.ShapeDtypeStruct((N_CHIPS,) + p.shape[1:], p.dtype) for p in parts]
    scratch = [pltpu.SemaphoreType.DMA((n, 3)), pltpu.SemaphoreType.DMA((n, 3)), pltpu.SemaphoreType.DMA((n,))]
    return _hbm_call(body, "chip_exchange", parts, out_shape, scratch)


def sibling_swap_halves(fulls):
    n = len(fulls)

    def body(*refs):
        outs = refs[n:2 * n]
        send_sems, recv_sems = refs[2 * n:]
        x, y, c, _ = _place()
        copies = []
        for k in range(n):
            hl = fulls[k].shape[0] // 2
            mine = outs[k].at[pl.ds(c * hl, hl)]
            copies.append((
                pltpu.make_async_remote_copy(src_ref=mine, dst_ref=mine, send_sem=send_sems.at[k], recv_sem=recv_sems.at[k],
                                             device_id=(x, y, 1 - c), device_id_type=MESH),
                pltpu.make_async_remote_copy(src_ref=mine, dst_ref=outs[k].at[pl.ds((1 - c) * hl, hl)], send_sem=send_sems.at[k],
                                             recv_sem=recv_sems.at[k], device_id=(x, y, 1 - c), device_id_type=MESH)))
        for send, _ in copies:
            send.start()
        for send, recv in copies:
            send.wait_send()
            recv.wait_recv()

    out_shape = [jax.ShapeDtypeStruct(f.shape, f.dtype) for f in fulls]
    scratch = [pltpu.SemaphoreType.DMA((n,)), pltpu.SemaphoreType.DMA((n,))]
    return _hbm_call(body, "sibling_swap_halves", fulls, out_shape, scratch, aliases={k: k for k in range(n)})


def _tile_rows(rows, cols, budget_bytes):
    best = None
    for t in range(16, rows + 1, 16):
        if rows % t == 0 and t * cols * 4 <= budget_bytes:
            best = t
    assert best is not None, (rows, cols)
    return best


def add_own_half(name, grad, recv, c_arr, out_dtype):
    nd, nl, rows, cols = grad.shape
    hl = nl // 2
    tr = _tile_rows(rows, cols, 2 * 1024 * 1024)

    def body(c_ref, g_ref, r_ref, o_ref):
        o_ref[...] = (g_ref[...] + r_ref[...]).astype(o_ref.dtype)

    blk = (None, None, tr, cols)
    return pl.pallas_call(
        body, name=name, out_shape=jax.ShapeDtypeStruct(recv.shape, out_dtype),
        grid_spec=pltpu.PrefetchScalarGridSpec(
            num_scalar_prefetch=1, grid=(nd, hl, rows // tr),
            in_specs=[pl.BlockSpec(blk, lambda d, l, r, c_ref: (d, c_ref[0] * hl + l, r, 0)),
                      pl.BlockSpec(blk, lambda d, l, r, c_ref: (d, l, r, 0))],
            out_specs=pl.BlockSpec(blk, lambda d, l, r, c_ref: (d, l, r, 0))),
        compiler_params=_params(("parallel", "parallel", "parallel")),
    )(c_arr, grad, recv)


def sum_chips(name, recv, c_arr):
    _, hl, rows, cols = recv.shape
    tr = _tile_rows(rows, cols, 1024 * 1024)

    def body(c_ref, r_ref, o_ref):
        r = [r_ref[s].astype(F32) for s in range(N_CHIPS)]
        o_ref[...] = ((r[0] + r[1]) + r[2]) + r[3]

    return pl.pallas_call(
        body, name=name, out_shape=jax.ShapeDtypeStruct((2 * hl, rows, cols), F32),
        grid_spec=pltpu.PrefetchScalarGridSpec(
            num_scalar_prefetch=1, grid=(hl, rows // tr),
            in_specs=[pl.BlockSpec((N_CHIPS, None, tr, cols), lambda l, r, c_ref: (0, l, r, 0))],
            out_specs=pl.BlockSpec((None, tr, cols), lambda l, r, c_ref: (c_ref[0] * hl + l, r, 0))),
        compiler_params=_params(("parallel", "parallel")),
    )(c_arr, recv)


def _pack(arrs, rows_multiple):
    flat = []
    for a in arrs:
        v = a.reshape(-1).astype(F32)
        flat.append(jnp.pad(v, (0, (-v.shape[0]) % HEAD)))
    v = jnp.concatenate(flat)
    v = jnp.pad(v, (0, (-v.shape[0]) % (HEAD * rows_multiple)))
    return v.reshape(-1, HEAD)


def _unpack(packed, shapes):
    flat = packed.reshape(-1)
    out, off = [], 0
    for shp in shapes:
        size = int(np.prod(shp))
        out.append(flat[off:off + size].reshape(shp))
        off += size + (-size) % HEAD
    return out


BIG = ["even_w_in", "even_w_out", "mla_w_down", "mla_w_qb", "mla_w_kvb", "mla_w_o", "mlp_w1", "mlp_w2"]
BIG_KEY = {"even_w_in": "w_in", "even_w_out": "w_out", "mla_w_down": "w_down", "mla_w_qb": "w_qb",
           "mla_w_kvb": "w_kvb", "mla_w_o": "w_o", "mlp_w1": "w1", "mlp_w2": "w2"}
SMALL_KEY = {"ln_mix_g": "ln_mix_g", "ln_mix_b": "ln_mix_b", "ln_ffn_g": "ln_ffn_g", "ln_ffn_b": "ln_ffn_b",
             "pool_w": "pool_w", "pool_scale": "pool_scale", "lru_conv_w": "conv_w", "lru_conv_b": "conv_b",
             "lru_w_a": "w_a", "lru_b_a": "b_a", "lru_w_x": "w_x", "lru_b_x": "b_x", "lru_lambda": "lam",
             "mla_q_norm_g": "gq", "mla_kv_norm_g": "gkv"}
SMALL = list(SMALL_KEY)
SMALL_SHARDED = ["lru_conv_w", "mla_q_norm_g", "mla_kv_norm_g"]
WEIGHTS = ["ln_mix_g", "ln_mix_b", "ln_ffn_g", "ln_ffn_b", "even_w_in", "pool_w", "pool_scale", "lru_conv_w",
           "lru_conv_b", "lru_w_a", "lru_b_a", "lru_w_x", "lru_b_x", "lru_lambda", "even_w_out", "mla_w_down",
           "mla_q_norm_g", "mla_kv_norm_g", "mla_w_qb", "mla_w_kvb", "mla_w_o", "mlp_w1", "mlp_w2"]


def _pad_q_heads(w):
    lead = w.shape[:-1]
    w = w.reshape(lead + (2, QK_NOPE + QK_ROPE))
    w = jnp.pad(w, ((0, 0),) * len(lead) + ((0, 0), (0, QHEAD_PAD - QK_NOPE - QK_ROPE)))
    return w.reshape(lead + (2 * QHEAD_PAD,))


def _unpad_q_heads(g):
    lead = g.shape[:-1]
    return g.reshape(lead + (2, QHEAD_PAD))[..., :QK_NOPE + QK_ROPE].reshape(lead + (2 * (QK_NOPE + QK_ROPE),))


def _step(x, positions, loss_target, w, m, v):
    cx, cy, cc = lax.axis_index("x"), lax.axis_index("y"), lax.axis_index("c")
    chip = 2 * cx + cy
    c_arr = jnp.reshape(cc, (1,)).astype(jnp.int32)

    small_shard_shapes = [w[k].shape for k in SMALL_SHARDED]
    small_pack = _pack([w[k] for k in SMALL_SHARDED], 16).reshape(2, -1, HEAD)
    gathered = all_gather_shards([w[k].astype(BF16) for k in BIG] + [small_pack])
    wg = {BIG_KEY[k]: g for k, g in zip(BIG, gathered)}
    wg["w_down"] = jnp.pad(wg["w_down"], ((0, 0),) * 3 + ((0, ODD_IN_PAD - ODD_IN),))
    wg["w_qb"] = _pad_q_heads(wg["w_qb"])
    per_chip = [_unpack(gathered[-1][s], small_shard_shapes) for s in range(N_CHIPS)]
    sm = {SMALL_KEY[k]: w[k] for k in SMALL if k not in SMALL_SHARDED}
    sm["conv_w"] = jnp.concatenate([p[0] for p in per_chip], axis=-1)
    sm["gq"] = jnp.concatenate([p[1] for p in per_chip], axis=-1)
    sm["gkv"] = jnp.concatenate([p[2] for p in per_chip], axis=-1)

    loss, grad_x, gw, gs = _local_step(x[0], positions.reshape(S, 1), loss_target[0], wg, sm)
    loss = lax.psum(loss, ("x", "y", "c"))

    small_shapes = [gs[SMALL_KEY[k]].shape for k in SMALL]
    gs_pack = _pack([gs[SMALL_KEY[k]] for k in SMALL], 32)
    grads = [gw[BIG_KEY[k]] for k in BIG] + [gs_pack.reshape(1, 2, -1, HEAD)]
    names = [BIG_KEY[k] for k in BIG] + ["small"]
    recv = sibling_send_halves(grads)
    parts = [add_own_half("add_" + n, g, r, c_arr, F32 if n == "small" else BF16)
             for n, g, r in zip(names, grads, recv)]
    landed = chip_exchange(parts)
    halves = [sum_chips("sum_" + n, r, c_arr) for n, r in zip(names, landed)]
    reduced = sibling_swap_halves(halves)
    g_big = dict(zip(BIG, reduced[:-1]))
    g_big["mla_w_down"] = g_big["mla_w_down"][..., :ODD_IN]
    g_big["mla_w_qb"] = _unpad_q_heads(g_big["mla_w_qb"])
    g_small = dict(zip(SMALL, _unpack(reduced[-1], small_shapes)))
    for k in SMALL_SHARDED:
        width = w[k].shape[-1]
        g_small[k] = lax.dynamic_slice_in_dim(g_small[k], chip * width, width, axis=-1)
    grad = {**g_big, **g_small}

    delta, new_m, new_v = {}, {}, {}
    for k in BIG:
        shp = w[k].shape
        view = lambda a: a.reshape(-1, shp[-1])
        d, nm, nv = adamw("adamw_" + BIG_KEY[k], view(w[k]), view(grad[k]), view(m[k]), view(v[k]))
        delta[k], new_m[k], new_v[k] = d.reshape(shp), nm.reshape(shp), nv.reshape(shp)
    shapes = [w[k].shape for k in SMALL]
    d, nm, nv = adamw("adamw_small", *[_pack([t[k] for k in SMALL], 512) for t in (w, grad, m, v)])
    for k, dk, mk, vk in zip(SMALL, _unpack(d, shapes), _unpack(nm, shapes), _unpack(nv, shapes)):
        delta[k], new_m[k], new_v[k] = dk, mk, vk
    return (loss, grad_x[None], *[grad[k] for k in WEIGHTS], *[delta[k] for k in WEIGHTS],
            *[new_m[k] for k in WEIGHTS], *[new_v[k] for k in WEIGHTS])


def kernel(x, positions, ln_mix_g, ln_mix_b, ln_ffn_g, ln_ffn_b, even_w_in, pool_w, pool_scale, lru_conv_w, lru_conv_b, lru_w_a, lru_b_a, lru_w_x, lru_b_x, lru_lambda, even_w_out, mla_w_down, mla_q_norm_g, mla_kv_norm_g, mla_w_qb, mla_w_kvb, mla_w_o, mlp_w1, mlp_w2, loss_target, m_ln_mix_g, m_ln_mix_b, m_ln_ffn_g, m_ln_ffn_b, m_even_w_in, m_pool_w, m_pool_scale, m_lru_conv_w, m_lru_conv_b, m_lru_w_a, m_lru_b_a, m_lru_w_x, m_lru_b_x, m_lru_lambda, m_even_w_out, m_mla_w_down, m_mla_q_norm_g, m_mla_kv_norm_g, m_mla_w_qb, m_mla_w_kvb, m_mla_w_o, m_mlp_w1, m_mlp_w2, v_ln_mix_g, v_ln_mix_b, v_ln_ffn_g, v_ln_ffn_b, v_even_w_in, v_pool_w, v_pool_scale, v_lru_conv_w, v_lru_conv_b, v_lru_w_a, v_lru_b_a, v_lru_w_x, v_lru_b_x, v_lru_lambda, v_even_w_out, v_mla_w_down, v_mla_q_norm_g, v_mla_kv_norm_g, v_mla_w_qb, v_mla_w_kvb, v_mla_w_o, v_mlp_w1, v_mlp_w2):
    args = locals()
    w = {k: args[k] for k in WEIGHTS}
    m = {k: args["m_" + k] for k in WEIGHTS}
    v = {k: args["v_" + k] for k in WEIGHTS}
    return _step(x, positions, loss_target, w, m, v)
```

```python
import functools
import math

import jax
import jax.numpy as jnp
import numpy as np
from jax import lax
from jax.experimental import pallas as pl
from jax.experimental.pallas import tpu as pltpu

F32 = jnp.float32
BF16 = jnp.bfloat16

S = 2048
D = 1024
DEPTH = 4
N_CHIPS = 4
POOL_WINDOWS = (2, 4, 8, 16)
POOL_W = 512
LRU_W = 1024
LRU_HEADS = 8
HEAD = 128
EVEN_IN = 2560
EVEN_MIX = 1536
MLA_HEADS = 8
QK_NOPE = 128
QK_ROPE = 64
Q_RANK = 384
KV_RANK = 256
ODD_IN = 704
ODD_IN_PAD = 768
QHEAD_PAD = 256
D_FF = 4096
CHUNK = 64
ALPHA = (2 * DEPTH) ** 0.25
LN_EPS = 1e-5
RMS_EPS = 1e-6
ATT_SCALE = (QK_NOPE + QK_ROPE) ** -0.5
ROPE_THETA = 10000.0
LRU_C = 8.0
ADAM_LR = 0.001
ADAM_B1 = 0.9
ADAM_B2 = 0.999
ADAM_EPS = 1e-08
ADAM_WD = 0.01
ADAM_STEP = 10

VMEM_LIMIT = 56 * 1024 * 1024
MESH = pl.DeviceIdType.MESH

_NN = (((1,), (0,)), ((), ()))
_NT = (((1,), (1,)), ((), ()))
_TN = (((0,), (0,)), ((), ()))


def _params(sem=None, **kw):
    return pltpu.CompilerParams(dimension_semantics=sem, vmem_limit_bytes=VMEM_LIMIT, **kw)


def _dot(a, b, dims):
    return lax.dot_general(a.astype(BF16), b.astype(BF16), dims, preferred_element_type=F32)


def _mm(name, a, w4, *, dims, a_block, a_map, w_map, grid, sem, outs, out_blocks, out_maps,
        red_axis=None, n_red=1, extras=(), extra_blocks=(), extra_maps=(), epilogue=None):
    n_extra = len(extras)
    n_out = len(outs)
    w_block = (None,) + tuple(w4.shape[1:])

    def body(a_ref, w_ref, *rest):
        extra_refs = rest[:n_extra]
        out_refs = rest[n_extra:n_extra + n_out]
        p = _dot(a_ref[...], w_ref[...], dims)

        def finish(acc):
            vals = epilogue(acc, *[r[...] for r in extra_refs]) if epilogue else (acc,)
            for r, v in zip(out_refs, vals):
                r[...] = v.astype(r.dtype)

        if red_axis is None:
            finish(p)
        else:
            acc_ref = rest[-1]
            k = pl.program_id(red_axis)

            @pl.when(k == 0)
            def _():
                acc_ref[...] = p

            @pl.when(k > 0)
            def _():
                acc_ref[...] += p

            @pl.when(k == n_red - 1)
            def _():
                finish(acc_ref[...])

    scratch = [] if red_axis is None else [pltpu.VMEM(out_blocks[0], F32)]
    return pl.pallas_call(
        body, name=name, grid=grid,
        in_specs=[pl.BlockSpec(a_block, a_map), pl.BlockSpec(w_block, w_map)]
        + [pl.BlockSpec(b, m) for b, m in zip(extra_blocks, extra_maps)],
        out_specs=[pl.BlockSpec(b, m) for b, m in zip(out_blocks, out_maps)],
        out_shape=outs, scratch_shapes=scratch, compiler_params=_params(sem),
    )(a, w4, *extras)


def _w_map_outer(s, i):
    return (s, 0, 0)


def _w_map_inner(i, s):
    return (s, 0, 0)


def mm_cols(name, a, w4, out_dtypes, *, tm=512, epilogue=None):
    m, k = a.shape
    nb = w4.shape[2]
    outs = [jax.ShapeDtypeStruct((m, N_CHIPS * nb), dt) for dt in out_dtypes]
    return _mm(name, a, w4, dims=_NN, a_block=(tm, k), a_map=lambda s, i: (i, 0), w_map=_w_map_outer,
               grid=(N_CHIPS, m // tm), sem=("parallel", "parallel"), outs=outs,
               out_blocks=[(tm, nb)] * len(outs), out_maps=[lambda s, i: (i, s)] * len(outs), epilogue=epilogue)


def mm_rows(name, a, w4, out_dtype, *, tm=512):
    m = a.shape[0]
    kb, n = w4.shape[1:]
    outs = [jax.ShapeDtypeStruct((m, n), out_dtype)]
    return _mm(name, a, w4, dims=_NN, a_block=(tm, kb), a_map=lambda i, s: (i, s), w_map=_w_map_inner,
               grid=(m // tm, N_CHIPS), sem=("parallel", "arbitrary"), outs=outs, out_blocks=[(tm, n)],
               out_maps=[lambda i, s: (i, 0)], red_axis=1, n_red=N_CHIPS)[0]


def mm_t_cols(name, g, w4, out_dtype, *, tm=512, resid=None):
    m = g.shape[0]
    k, nb = w4.shape[1:]
    outs = [jax.ShapeDtypeStruct((m, k), out_dtype)]
    extras = () if resid is None else (resid,)
    epi = None if resid is None else (lambda acc, r: (acc + ALPHA * r,))
    return _mm(name, g, w4, dims=_NT, a_block=(tm, nb), a_map=lambda i, s: (i, s), w_map=_w_map_inner,
               grid=(m // tm, N_CHIPS), sem=("parallel", "arbitrary"), outs=outs, out_blocks=[(tm, k)],
               out_maps=[lambda i, s: (i, 0)], red_axis=1, n_red=N_CHIPS, extras=extras,
               extra_blocks=[(tm, k)] * len(extras), extra_maps=[lambda i, s: (i, 0)] * len(extras), epilogue=epi)[0]


def mm_t_rows(name, g, w4, out_dtype, *, tm=512, resid=None, gate=None):
    m, n = g.shape
    kb = w4.shape[1]
    outs = [jax.ShapeDtypeStruct((m, N_CHIPS * kb), out_dtype)]
    extras, epi = (), None
    if resid is not None:
        extras, epi = (resid,), (lambda acc, r: (acc + ALPHA * r,))
    if gate is not None:
        extras, epi = (gate,), (lambda acc, r: (acc * (2.0 * r.astype(F32)),))
    return _mm(name, g, w4, dims=_NT, a_block=(tm, n), a_map=lambda s, i: (i, 0), w_map=_w_map_outer,
               grid=(N_CHIPS, m // tm), sem=("parallel", "parallel"), outs=outs, out_blocks=[(tm, kb)],
               out_maps=[lambda s, i: (i, s)], extras=extras, extra_blocks=[(tm, kb)] * len(extras),
               extra_maps=[lambda s, i: (i, s)] * len(extras), epilogue=epi)[0]


def mm_grad(name, a, g, *, a_sharded, tm=512):
    m = a.shape[0]
    ka = a.shape[1] // N_CHIPS if a_sharded else a.shape[1]
    ng = g.shape[1] if a_sharded else g.shape[1] // N_CHIPS
    n_red = m // tm
    a_map = (lambda s, i: (i, s)) if a_sharded else (lambda s, i: (i, 0))
    g_map = (lambda s, i: (i, 0)) if a_sharded else (lambda s, i: (i, s))

    def body(a_ref, g_ref, out_ref, acc_ref):
        p = _dot(a_ref[...], g_ref[...], _TN)
        k = pl.program_id(1)

        @pl.when(k == 0)
        def _():
            acc_ref[...] = p

        @pl.when(k > 0)
        def _():
            acc_ref[...] += p

        @pl.when(k == n_red - 1)
        def _():
            out_ref[...] = acc_ref[...].astype(out_ref.dtype)

    return pl.pallas_call(
        body, name=name, grid=(N_CHIPS, n_red),
        in_specs=[pl.BlockSpec((tm, ka), a_map), pl.BlockSpec((tm, ng), g_map)],
        out_specs=pl.BlockSpec((None, ka, ng), lambda s, i: (s, 0, 0)),
        out_shape=jax.ShapeDtypeStruct((N_CHIPS, ka, ng), BF16),
        scratch_shapes=[pltpu.VMEM((ka, ng), F32)], compiler_params=_params(("parallel", "arbitrary")),
    )(a, g)


def ln_fwd(name, x, s, g, b, *, tm=256):
    def body(x_ref, s_ref, g_ref, b_ref, y_ref, xhat_ref, rstd_ref):
        z = ALPHA * x_ref[...] + s_ref[...]
        mu = jnp.mean(z, axis=-1, keepdims=True)
        zc = z - mu
        var = jnp.mean(zc * zc, axis=-1, keepdims=True)
        rstd = lax.rsqrt(var + LN_EPS)
        xhat = zc * rstd
        y_ref[...] = xhat * g_ref[...] + b_ref[...]
        xhat_ref[...] = xhat
        rstd_ref[...] = rstd

    row = pl.BlockSpec((tm, D), lambda i: (i, 0))
    vec = pl.BlockSpec((1, D), lambda i: (0, 0))
    return pl.pallas_call(
        body, name=name, grid=(S // tm,), in_specs=[row, row, vec, vec],
        out_specs=[row, row, pl.BlockSpec((tm, 1), lambda i: (i, 0))],
        out_shape=[jax.ShapeDtypeStruct((S, D), F32), jax.ShapeDtypeStruct((S, D), F32),
                   jax.ShapeDtypeStruct((S, 1), F32)],
        compiler_params=_params(("parallel",)),
    )(x, s, g, b)


def ln_bwd(name, dy, xhat, rstd, g, *, tm=256):
    def body(dy_ref, xhat_ref, rstd_ref, g_ref, dz_ref, dg_ref, db_ref):
        dy_v = dy_ref[...]
        xh = xhat_ref[...]
        dxh = dy_v * g_ref[...]
        m1 = jnp.mean(dxh, axis=-1, keepdims=True)
        m2 = jnp.mean(dxh * xh, axis=-1, keepdims=True)
        dz_ref[...] = rstd_ref[...] * (dxh - m1 - xh * m2)
        pg = jnp.sum(dy_v * xh, axis=0, keepdims=True)
        pb = jnp.sum(dy_v, axis=0, keepdims=True)
        i = pl.program_id(0)

        @pl.when(i == 0)
        def _():
            dg_ref[...] = pg
            db_ref[...] = pb

        @pl.when(i > 0)
        def _():
            dg_ref[...] += pg
            db_ref[...] += pb

    row = pl.BlockSpec((tm, D), lambda i: (i, 0))
    vec = pl.BlockSpec((1, D), lambda i: (0, 0))
    return pl.pallas_call(
        body, name=name, grid=(S // tm,), in_specs=[row, row, pl.BlockSpec((tm, 1), lambda i: (i, 0)), vec],
        out_specs=[row, vec, vec],
        out_shape=[jax.ShapeDtypeStruct((S, D), F32), jax.ShapeDtypeStruct((1, D), F32),
                   jax.ShapeDtypeStruct((1, D), F32)],
        compiler_params=_params(("arbitrary",)),
    )(dy, xhat, rstd, g)


def loss_head(y, target, *, tm=256):
    def body(y_ref, t_ref, dy_ref, loss_ref):
        e = y_ref[...] - t_ref[...]
        dy_ref[...] = e * (1.0 / D)
        part = jnp.sum(jnp.sum(e * e, axis=-1, keepdims=True), axis=0, keepdims=True) * (0.5 / D)
        i = pl.program_id(0)

        @pl.when(i == 0)
        def _():
            loss_ref[...] = jnp.zeros_like(loss_ref)

        loss_ref[...] += jnp.broadcast_to(part, loss_ref.shape)

    row = pl.BlockSpec((tm, D), lambda i: (i, 0))
    return pl.pallas_call(
        body, name="loss_head", grid=(S // tm,), in_specs=[row, row],
        out_specs=[row, pl.BlockSpec((8, 128), lambda i: (0, 0))],
        out_shape=[jax.ShapeDtypeStruct((S, D), F32), jax.ShapeDtypeStruct((8, 128), F32)],
        compiler_params=_params(("arbitrary",)),
    )(y, target)


def _rows(shape):
    return lax.broadcasted_iota(jnp.int32, shape, 0)


def _shift_down(x, k):
    return jnp.where(_rows(x.shape) >= k, pltpu.roll(x, k, 0), 0.0)


def _shift_up(x, k):
    n = x.shape[0]
    return jnp.where(_rows(x.shape) < n - k, pltpu.roll(x, n - k, 0), 0.0)


def _pool_diff(u, w):
    acc, k = u, 1
    while k < w:
        acc = acc + _shift_down(acc, k)
        k *= 2
    cnt = jnp.minimum(_rows(u.shape) + 1, w).astype(F32)
    return acc / cnt - u, cnt


def pool_fwd(name, proj, pool_w, pool_scale):
    def body(u_ref, w_ref, sc_ref, y_ref):
        for g, w in enumerate(POOL_WINDOWS):
            cols = slice(g * HEAD, (g + 1) * HEAD)
            d, _ = _pool_diff(u_ref[:, cols], w)
            z = _dot(d, w_ref[g], _NN)
            y_ref[:, cols] = (z * sc_ref[:, cols]).astype(y_ref.dtype)

    return pl.pallas_call(
        body, name=name, grid=(1,),
        in_specs=[pl.BlockSpec((S, POOL_W), lambda i: (0, 0)),
                  pl.BlockSpec((4, HEAD, HEAD), lambda i: (0, 0, 0)),
                  pl.BlockSpec((1, POOL_W), lambda i: (0, 0))],
        out_specs=pl.BlockSpec((S, POOL_W), lambda i: (0, 0)),
        out_shape=jax.ShapeDtypeStruct((S, POOL_W), BF16),
        compiler_params=_params(("arbitrary",)),
    )(proj, pool_w, pool_scale)


def pool_bwd(name, proj, dycat, pool_w, pool_scale):
    def body(u_ref, dy_ref, w_ref, sc_ref, du_ref, dw_ref, dsc_ref):
        for g, w in enumerate(POOL_WINDOWS):
            cols = slice(g * HEAD, (g + 1) * HEAD)
            d, cnt = _pool_diff(u_ref[:, cols], w)
            dy = dy_ref[:, cols]
            z = _dot(d, w_ref[g], _NN)
            dsc_ref[:, cols] = jnp.sum(dy * z, axis=0, keepdims=True)
            dz = dy * sc_ref[:, cols]
            dw_ref[g] = _dot(d, dz, _TN)
            dd = _dot(dz, w_ref[g], _NT)
            acc, k = dd / cnt, 1
            while k < w:
                acc = acc + _shift_up(acc, k)
                k *= 2
            du_ref[:, cols] = (acc - dd).astype(du_ref.dtype)

    return pl.pallas_call(
        body, name=name, grid=(1,),
        in_specs=[pl.BlockSpec((S, POOL_W), lambda i: (0, 0)),
                  pl.BlockSpec((S, POOL_W), lambda i: (0, 0)),
                  pl.BlockSpec((4, HEAD, HEAD), lambda i: (0, 0, 0)),
                  pl.BlockSpec((1, POOL_W), lambda i: (0, 0))],
        out_specs=[pl.BlockSpec((S, POOL_W), lambda i: (0, 0)),
                   pl.BlockSpec((4, HEAD, HEAD), lambda i: (0, 0, 0)),
                   pl.BlockSpec((1, POOL_W), lambda i: (0, 0))],
        out_shape=[jax.ShapeDtypeStruct((S, POOL_W), BF16), jax.ShapeDtypeStruct((4, HEAD, HEAD), F32),
                   jax.ShapeDtypeStruct((1, POOL_W), F32)],
        compiler_params=_params(("arbitrary",)),
    )(proj, dycat, pool_w, pool_scale)


def _expm1(x):
    series = x * (1.0 + x * (0.5 + x * (1.0 / 6.0 + x * (1.0 / 24.0 + x * (1.0 / 120.0)))))
    return jnp.where(jnp.abs(x) < 0.05, series, jnp.exp(x) - 1.0)


def _softplus_neg(lam):
    e = jnp.exp(-jnp.abs(lam))
    log1p = jnp.where(e < 0.01, e * (1.0 - e * (0.5 - e * (1.0 / 3.0))), jnp.log(1.0 + e))
    return jnp.maximum(-lam, 0.0) + log1p


_GELU_C = math.sqrt(2.0 / math.pi)


def _gelu(x):
    t = jnp.tanh(_GELU_C * (x + 0.044715 * x * x * x))
    return 0.5 * x * (1.0 + t), t


def _gelu_grad(x, t):
    return 0.5 * (1.0 + t) + 0.5 * x * (1.0 - t * t) * _GELU_C * (1.0 + 3.0 * 0.044715 * x * x)


def _conv(u, cw, cb):
    return cw[3:4] * u + cw[2:3] * _shift_down(u, 1) + cw[1:2] * _shift_down(u, 2) + cw[0:1] * _shift_down(u, 3) + cb


def _lru_gates(cu, wa, ba, wx, bx, lam):
    r = jax.nn.sigmoid(_dot(cu, wa, _NN) + ba)
    i = jax.nn.sigmoid(_dot(cu, wx, _NN) + bx)
    sp = _softplus_neg(lam)
    log_a = (-LRU_C) * r * sp
    a = jnp.exp(log_a)
    mult = jnp.sqrt(-_expm1(2.0 * log_a))
    return r, i, sp, a, mult


def _scan(a_ref, b_ref, h_ref, *, reverse):
    n_blk = S // 8
    row8 = lax.broadcasted_iota(jnp.int32, (8, HEAD), 0)

    def step(j, carry):
        blk = (n_blk - 1 - j) if reverse else j
        r0 = pl.multiple_of(blk * 8, 8)
        a = a_ref[pl.ds(r0, 8), :]
        b = b_ref[pl.ds(r0, 8), :]
        for k in (1, 2, 4):
            if reverse:
                keep = row8 < 8 - k
                a_s, b_s = pltpu.roll(a, 8 - k, 0), pltpu.roll(b, 8 - k, 0)
            else:
                keep = row8 >= k
                a_s, b_s = pltpu.roll(a, k, 0), pltpu.roll(b, k, 0)
            b = jnp.where(keep, a * b_s + b, b)
            a = jnp.where(keep, a * a_s, a)
        h = b + a * carry
        h_ref[pl.ds(r0, 8), :] = h
        edge = h[0:1, :] if reverse else h[7:8, :]
        return jnp.broadcast_to(edge, (8, HEAD))

    lax.fori_loop(0, n_blk, step, jnp.zeros((8, HEAD), F32), unroll=4)


def _lru_specs():
    def col(off):
        return pl.BlockSpec((S, HEAD), lambda h: (0, off + h))
    vec = pl.BlockSpec((1, HEAD), lambda h: (0, h))
    mat = pl.BlockSpec((None, HEAD, HEAD), lambda h: (h, 0, 0))
    cw = pl.BlockSpec((4, HEAD), lambda h: (0, h))
    return col, vec, mat, cw


def lru_fwd(name, proj, conv_w, conv_b, w_a, b_a, w_x, b_x, lam):
    def body(u_ref, ug_ref, cw_ref, cb_ref, wa_ref, ba_ref, wx_ref, bx_ref, lam_ref, y_ref, h_ref, a_s, b_s):
        cu = _conv(u_ref[...], cw_ref[...], cb_ref[...])
        _, i, _, a, mult = _lru_gates(cu, wa_ref[...], ba_ref[...], wx_ref[...], bx_ref[...], lam_ref[...])
        a_s[...] = a
        b_s[...] = mult * (i * cu)
        _scan(a_s, b_s, h_ref, reverse=False)
        gl, _ = _gelu(ug_ref[...])
        y_ref[...] = (h_ref[...] * gl).astype(y_ref.dtype)

    col, vec, mat, cw = _lru_specs()
    out = pl.BlockSpec((S, HEAD), lambda h: (0, h))
    return pl.pallas_call(
        body, name=name, grid=(LRU_HEADS,),
        in_specs=[col(4), col(12), cw, vec, mat, vec, mat, vec, vec],
        out_specs=[out, out],
        out_shape=[jax.ShapeDtypeStruct((S, LRU_W), BF16), jax.ShapeDtypeStruct((S, LRU_W), F32)],
        scratch_shapes=[pltpu.VMEM((S, HEAD), F32), pltpu.VMEM((S, HEAD), F32)],
        compiler_params=_params(("parallel",)),
    )(proj, proj, conv_w, conv_b, w_a, b_a, w_x, b_x, lam)


def lru_bwd(name, proj, hstate, dycat, conv_w, conv_b, w_a, b_a, w_x, b_x, lam):
    def body(u_ref, ug_ref, h_ref, dy_ref, cw_ref, cb_ref, wa_ref, ba_ref, wx_ref, bx_ref, lam_ref,
             du_ref, dug_ref, dwa_ref, dwx_ref, dba_ref, dbx_ref, dlam_ref, dcw_ref, dcb_ref, a_s, b_s, g_s):
        u = u_ref[...]
        cw = cw_ref[...]
        cu = _conv(u, cw, cb_ref[...])
        lam_v = lam_ref[...]
        r, i, sp, a, mult = _lru_gates(cu, wa_ref[...], ba_ref[...], wx_ref[...], bx_ref[...], lam_v)
        ug = ug_ref[...]
        gl, t = _gelu(ug)
        dy = dy_ref[...]
        h = h_ref[...]
        dug_ref[...] = (dy * h * _gelu_grad(ug, t)).astype(dug_ref.dtype)
        a_s[...] = _shift_up(a, 1)
        b_s[...] = dy * gl
        _scan(a_s, b_s, g_s, reverse=True)
        dxin = g_s[...]
        da = dxin * _shift_down(h, 1)
        dmult = dxin * (i * cu)
        di = dxin * (mult * cu)
        dlog_a = da * a - dmult * (a * a) / mult
        dr_pre = dlog_a * ((-LRU_C) * sp) * (r * (1.0 - r))
        di_pre = di * (i * (1.0 - i))
        dsp = jnp.sum(dlog_a * ((-LRU_C) * r), axis=0, keepdims=True)
        dlam_ref[...] = dsp * (-jax.nn.sigmoid(-lam_v))
        dba_ref[...] = jnp.sum(dr_pre, axis=0, keepdims=True)
        dbx_ref[...] = jnp.sum(di_pre, axis=0, keepdims=True)
        dwa_ref[...] = _dot(cu, dr_pre, _TN)
        dwx_ref[...] = _dot(cu, di_pre, _TN)
        dcu = dxin * (mult * i) + _dot(dr_pre, wa_ref[...], _NT) + _dot(di_pre, wx_ref[...], _NT)
        dcb_ref[...] = jnp.sum(dcu, axis=0, keepdims=True)
        for k in range(4):
            dcw_ref[k:k + 1, :] = jnp.sum(dcu * (_shift_down(u, 3 - k) if k < 3 else u), axis=0, keepdims=True)
        du = cw[3:4] * dcu + cw[2:3] * _shift_up(dcu, 1) + cw[1:2] * _shift_up(dcu, 2) + cw[0:1] * _shift_up(dcu, 3)
        du_ref[...] = du.astype(du_ref.dtype)

    col, vec, mat, cw = _lru_specs()
    out = pl.BlockSpec((S, HEAD), lambda h: (0, h))
    big = jax.ShapeDtypeStruct((S, LRU_W), BF16)
    vec_shape = jax.ShapeDtypeStruct((1, LRU_W), F32)
    mat_shape = jax.ShapeDtypeStruct((LRU_HEADS, HEAD, HEAD), F32)
    return pl.pallas_call(
        body, name=name, grid=(LRU_HEADS,),
        in_specs=[col(4), col(12), out, col(4), cw, vec, mat, vec, mat, vec, vec],
        out_specs=[out, out, mat, mat, vec, vec, vec, cw, vec],
        out_shape=[big, big, mat_shape, mat_shape, vec_shape, vec_shape, vec_shape,
                   jax.ShapeDtypeStruct((4, LRU_W), F32), vec_shape],
        scratch_shapes=[pltpu.VMEM((S, HEAD), F32)] * 3,
        compiler_params=_params(("parallel",)),
    )(proj, proj, hstate, dycat, conv_w, conv_b, w_a, b_a, w_x, b_x, lam)


def rope_tables(pos_col, inv_freq):
    def body(pos_ref, f_ref, c_ref, s1_ref, s2_ref):
        ang = pos_ref[...].astype(F32) * f_ref[...]
        lane = lax.broadcasted_iota(jnp.int32, ang.shape, 1)
        cos, sin = jnp.cos(ang), jnp.sin(ang)
        c_ref[...] = jnp.where(lane < QK_ROPE, cos, 0.0)
        s1_ref[...] = jnp.where(lane < QK_ROPE // 2, -sin, 0.0)
        s2_ref[...] = jnp.where((lane >= QK_ROPE // 2) & (lane < QK_ROPE), sin, 0.0)

    tab = jax.ShapeDtypeStruct((S, HEAD), F32)
    return pl.pallas_call(
        body, name="rope_tables", grid=(1,),
        in_specs=[pl.BlockSpec((S, 1), lambda i: (0, 0)), pl.BlockSpec((1, HEAD), lambda i: (0, 0))],
        out_specs=[pl.BlockSpec((S, HEAD), lambda i: (0, 0))] * 3, out_shape=[tab, tab, tab],
        compiler_params=_params(("arbitrary",)),
    )(pos_col, inv_freq)


def _rope(v, c, s1, s2):
    return v * c + pltpu.roll(v, HEAD - QK_ROPE // 2, 1) * s1 + pltpu.roll(v, QK_ROPE // 2, 1) * s2


def _unrope(d, c, s1, s2):
    return d * c + pltpu.roll(d * s1, QK_ROPE // 2, 1) + pltpu.roll(d * s2, HEAD - QK_ROPE // 2, 1)


def _rms(x, g):
    rstd = lax.rsqrt(jnp.mean(x * x, axis=-1, keepdims=True) + RMS_EPS)
    return x * rstd, rstd


def mla_prep(name, down, gq, gkv, tabs, *, tm=256):
    def body(dn_ref, gq_ref, gkv_ref, c_ref, s1_ref, s2_ref, cq_ref, ckv_ref, kp_ref):
        xq, _ = _rms(dn_ref[:, :Q_RANK], None)
        cq_ref[...] = (xq * gq_ref[...]).astype(cq_ref.dtype)
        xkv, _ = _rms(dn_ref[:, Q_RANK:Q_RANK + KV_RANK], None)
        ckv_ref[...] = (xkv * gkv_ref[...]).astype(ckv_ref.dtype)
        kp = _rope(dn_ref[:, Q_RANK + KV_RANK:], c_ref[...], s1_ref[...], s2_ref[...])
        kp_ref[...] = kp.astype(kp_ref.dtype)

    tab = pl.BlockSpec((tm, HEAD), lambda i: (i, 0))
    return pl.pallas_call(
        body, name=name, grid=(S // tm,),
        in_specs=[pl.BlockSpec((tm, ODD_IN_PAD), lambda i: (i, 0)), pl.BlockSpec((1, Q_RANK), lambda i: (0, 0)),
                  pl.BlockSpec((1, KV_RANK), lambda i: (0, 0)), tab, tab, tab],
        out_specs=[pl.BlockSpec((tm, Q_RANK), lambda i: (i, 0)), pl.BlockSpec((tm, KV_RANK), lambda i: (i, 0)), tab],
        out_shape=[jax.ShapeDtypeStruct((S, Q_RANK), BF16), jax.ShapeDtypeStruct((S, KV_RANK), BF16),
                   jax.ShapeDtypeStruct((S, HEAD), BF16)],
        compiler_params=_params(("parallel",)),
    )(down, gq, gkv, *tabs)


def mla_prep_bwd(name, down, dcq, dckv, dkp, gq, gkv, tabs, *, tm=256):
    def body(dn_ref, dcq_ref, dckv_ref, dkp_ref, gq_ref, gkv_ref, c_ref, s1_ref, s2_ref, dd_ref, dgq_ref, dgkv_ref):
        i = pl.program_id(0)

        def rms_bwd(x, dy, g, dg_ref):
            xh, rstd = _rms(x, None)
            dxh = dy * g
            dx = rstd * (dxh - xh * jnp.mean(dxh * xh, axis=-1, keepdims=True))
            pg = jnp.sum(dy * xh, axis=0, keepdims=True)

            @pl.when(i == 0)
            def _():
                dg_ref[...] = pg

            @pl.when(i > 0)
            def _():
                dg_ref[...] += pg

            return dx

        dxq = rms_bwd(dn_ref[:, :Q_RANK], dcq_ref[...], gq_ref[...], dgq_ref)
        dd_ref[:, :Q_RANK] = dxq.astype(dd_ref.dtype)
        dxkv = rms_bwd(dn_ref[:, Q_RANK:Q_RANK + KV_RANK], dckv_ref[...], gkv_ref[...], dgkv_ref)
        dd_ref[:, Q_RANK:Q_RANK + KV_RANK] = dxkv.astype(dd_ref.dtype)
        dd_ref[:, Q_RANK + KV_RANK:] = _unrope(dkp_ref[...], c_ref[...], s1_ref[...], s2_ref[...]).astype(dd_ref.dtype)

    tab = pl.BlockSpec((tm, HEAD), lambda i: (i, 0))
    vq = pl.BlockSpec((1, Q_RANK), lambda i: (0, 0))
    vkv = pl.BlockSpec((1, KV_RANK), lambda i: (0, 0))
    return pl.pallas_call(
        body, name=name, grid=(S // tm,),
        in_specs=[pl.BlockSpec((tm, ODD_IN_PAD), lambda i: (i, 0)), pl.BlockSpec((tm, Q_RANK), lambda i: (i, 0)),
                  pl.BlockSpec((tm, KV_RANK), lambda i: (i, 0)), tab, vq, vkv, tab, tab, tab],
        out_specs=[pl.BlockSpec((tm, ODD_IN_PAD), lambda i: (i, 0)), vq, vkv],
        out_shape=[jax.ShapeDtypeStruct((S, ODD_IN_PAD), BF16), jax.ShapeDtypeStruct((1, Q_RANK), F32),
                   jax.ShapeDtypeStruct((1, KV_RANK), F32)],
        compiler_params=_params(("arbitrary",)),
    )(down, dcq, dckv, dkp, gq, gkv, *tabs)


ATT_TQ = 256


def _attn_probs(q_ref, kv_ref, kp_ref, c_ref, s1_ref, s2_ref, i, nk):
    qn = q_ref[:, :HEAD].astype(BF16)
    qp = _rope(q_ref[:, HEAD:], c_ref[...], s1_ref[...], s2_ref[...]).astype(BF16)
    kn = kv_ref[:nk, :HEAD]
    sc = (_dot(qn, kn, _NT) + _dot(qp, kp_ref[:nk, :], _NT)) * ATT_SCALE
    q_chunk = (i * ATT_TQ + lax.broadcasted_iota(jnp.int32, sc.shape, 0)) // CHUNK
    k_chunk = lax.broadcasted_iota(jnp.int32, sc.shape, 1) // CHUNK
    sc = jnp.where(k_chunk <= q_chunk, sc, jnp.finfo(F32).min)
    e = jnp.exp(sc - jnp.max(sc, axis=-1, keepdims=True))
    p = e * (1.0 / jnp.sum(e, axis=-1, keepdims=True))
    return p, qn, qp, kn


def _for_each_prefix(i, fn):
    for k in range(S // ATT_TQ):
        pl.when(i == k)(functools.partial(fn, (k + 1) * ATT_TQ))


def _attn_specs():
    q = pl.BlockSpec((ATT_TQ, QHEAD_PAD), lambda h, i: (i, h))
    kv = pl.BlockSpec((S, QHEAD_PAD), lambda h, i: (0, h))
    kp = pl.BlockSpec((S, HEAD), lambda h, i: (0, 0))
    tab = pl.BlockSpec((ATT_TQ, HEAD), lambda h, i: (i, 0))
    o = pl.BlockSpec((ATT_TQ, HEAD), lambda h, i: (i, h))
    return q, kv, kp, tab, o


def attn_fwd(name, q, kv, kp, tabs):
    def body(q_ref, kv_ref, kp_ref, c_ref, s1_ref, s2_ref, o_ref):
        i = pl.program_id(1)

        def run(nk):
            p, _, _, _ = _attn_probs(q_ref, kv_ref, kp_ref, c_ref, s1_ref, s2_ref, i, nk)
            o_ref[...] = _dot(p, kv_ref[:nk, HEAD:], _NN).astype(o_ref.dtype)

        _for_each_prefix(i, run)

    qs, kvs, kps, tab, os = _attn_specs()
    return pl.pallas_call(
        body, name=name, grid=(MLA_HEADS, S // ATT_TQ), in_specs=[qs, kvs, kps, tab, tab, tab], out_specs=os,
        out_shape=jax.ShapeDtypeStruct((S, MLA_HEADS * HEAD), BF16),
        compiler_params=_params(("parallel", "parallel")),
    )(q, kv, kp, *tabs)


def attn_bwd(name, q, kv, kp, do, tabs):
    def body(q_ref, kv_ref, kp_ref, do_ref, c_ref, s1_ref, s2_ref, dq_ref, dkv_ref, dkp_ref):
        h, i = pl.program_id(0), pl.program_id(1)

        @pl.when(i == 0)
        def _():
            dkv_ref[...] = jnp.zeros_like(dkv_ref)

        @pl.when((i == 0) & (h == 0))
        def _():
            dkp_ref[...] = jnp.zeros_like(dkp_ref)

        def run(nk):
            p, qn, qp, kn = _attn_probs(q_ref, kv_ref, kp_ref, c_ref, s1_ref, s2_ref, i, nk)
            do_v = do_ref[...]
            dp = _dot(do_v, kv_ref[:nk, HEAD:], _NT)
            ds = (p * (dp - jnp.sum(p * dp, axis=-1, keepdims=True)) * ATT_SCALE).astype(BF16)
            dq_ref[:, :HEAD] = _dot(ds, kn, _NN).astype(dq_ref.dtype)
            dqp = _unrope(_dot(ds, kp_ref[:nk, :], _NN), c_ref[...], s1_ref[...], s2_ref[...])
            dq_ref[:, HEAD:] = dqp.astype(dq_ref.dtype)
            dkv_ref[:nk, :HEAD] += _dot(ds, qn, _TN)
            dkv_ref[:nk, HEAD:] += _dot(p, do_v, _TN)
            dkp_ref[:nk, :] += _dot(ds, qp, _TN)

        _for_each_prefix(i, run)

    qs, kvs, kps, tab, os = _attn_specs()
    return pl.pallas_call(
        body, name=name, grid=(MLA_HEADS, S // ATT_TQ), in_specs=[qs, kvs, kps, os, tab, tab, tab],
        out_specs=[qs, kvs, kps],
        out_shape=[jax.ShapeDtypeStruct((S, MLA_HEADS * QHEAD_PAD), BF16),
                   jax.ShapeDtypeStruct((S, MLA_HEADS * QHEAD_PAD), F32), jax.ShapeDtypeStruct((S, HEAD), F32)],
        compiler_params=_params(("arbitrary", "arbitrary")),
    )(q, kv, kp, do, *tabs)


def adamw(name, w, g, m, v):
    rows, cols = w.shape
    tr = rows
    for cand in (512, 256, 128, 64, 32, 16, 8):
        if rows % cand == 0 and cand * cols * 4 <= 2 * 1024 * 1024:
            tr = cand
            break

    def body(w_ref, g_ref, m_ref, v_ref, d_ref, nm_ref, nv_ref):
        g_v = g_ref[...]
        nm = ADAM_B1 * m_ref[...] + (1.0 - ADAM_B1) * g_v
        nv = ADAM_B2 * v_ref[...] + (1.0 - ADAM_B2) * (g_v * g_v)
        m_hat = nm / (1.0 - ADAM_B1 ** ADAM_STEP)
        v_hat = nv / (1.0 - ADAM_B2 ** ADAM_STEP)
        d_ref[...] = (-ADAM_LR) * (m_hat / (jnp.sqrt(v_hat) + ADAM_EPS) + ADAM_WD * w_ref[...])
        nm_ref[...] = nm
        nv_ref[...] = nv

    blk = pl.BlockSpec((tr, cols), lambda i: (i, 0))
    shape = jax.ShapeDtypeStruct((rows, cols), F32)
    return pl.pallas_call(
        body, name=name, grid=(rows // tr,), in_specs=[blk] * 4, out_specs=[blk] * 3, out_shape=[shape] * 3,
        compiler_params=_params(("parallel",)),
    )(w, g, m, v)


def _local_step(x, pos_col, target, sm, weights_of, emit_grads):
    inv_freq = ROPE_THETA ** (-jnp.arange(0, QK_ROPE, 2, dtype=F32) / QK_ROPE)
    inv_freq = jnp.concatenate([inv_freq, inv_freq, jnp.zeros((HEAD - QK_ROPE,), F32)])[None, :]
    tabs = rope_tables(pos_col, inv_freq)
    saved, wts = [], {}
    for layer in range(DEPTH):
        j = layer // 2
        n = "l%d_" % layer
        sv = {"x": x}
        wm = wts["mix%d" % layer] = weights_of("mix%d" % layer, x)
        if layer == 0:
            sm = dict(sm, conv_w=wm["conv_w"], gq=wm["gq"], gkv=wm["gkv"])
        if layer % 2 == 0:
            proj = mm_cols(n + "proj", x, wm["w_in"], [F32])[0]
            y_pool = pool_fwd(n + "pool", proj, sm["pool_w"][j], sm["pool_scale"][j][None])
            y_lru, hstate = lru_fwd(n + "lru", proj, sm["conv_w"][j], sm["conv_b"][j][None], sm["w_a"][j],
                                    sm["b_a"][j][None], sm["w_x"][j], sm["b_x"][j][None], sm["lam"][j][None])
            ycat = jnp.concatenate([y_pool, y_lru], axis=1)
            mix = mm_rows(n + "mixout", ycat, wm["w_out"], F32)
            sv.update(proj=proj, hstate=hstate, ycat=ycat)
        else:
            down = mm_rows(n + "down", x, wm["w_down"], F32)
            cq, ckv, kp = mla_prep(n + "prep", down, sm["gq"][j][None], sm["gkv"][j][None], tabs)
            q = mm_cols(n + "q", cq, wm["w_qb"], [F32])[0]
            kv = mm_cols(n + "kv", ckv, wm["w_kvb"], [BF16])[0]
            o = attn_fwd(n + "attn", q, kv, kp, tabs)
            mix = mm_rows(n + "attnout", o, wm["w_o"], F32)
            sv.update(down=down, cq=cq, ckv=ckv, kp=kp, q=q, kv=kv, o=o)
        x1, xhat1, rstd1 = ln_fwd(n + "ln_mix", x, mix, sm["ln_mix_g"][layer][None], sm["ln_mix_b"][layer][None])
        wf = wts["mlp%d" % layer] = weights_of("mlp%d" % layer, x1)
        act, relu = mm_cols(n + "mlp1", x1, wf["w1"], [BF16, BF16],
                            epilogue=lambda acc: (jnp.square(jnp.maximum(acc, 0.0)), jnp.maximum(acc, 0.0)))
        mlp = mm_rows(n + "mlp2", act, wf["w2"], F32)
        x2, xhat2, rstd2 = ln_fwd(n + "ln_ffn", x1, mlp, sm["ln_ffn_g"][layer][None], sm["ln_ffn_b"][layer][None])
        sv.update(xhat1=xhat1, rstd1=rstd1, x1=x1, act=act, relu=relu, xhat2=xhat2, rstd2=rstd2)
        saved.append(sv)
        x = x2

    dx, loss_tile = loss_head(x, target)
    gs = {k: [None] * (DEPTH if k.startswith("ln_") else DEPTH // 2) for k in sm}
    for layer in reversed(range(DEPTH)):
        j = layer // 2
        n = "l%d_" % layer
        sv = saved[layer]
        wm, wf = wts["mix%d" % layer], wts["mlp%d" % layer]
        dz, gs["ln_ffn_g"][layer], gs["ln_ffn_b"][layer] = ln_bwd(
            n + "ln_ffn_b", dx, sv["xhat2"], sv["rstd2"], sm["ln_ffn_g"][layer][None])
        dh = mm_t_rows(n + "mlp2_dx", dz, wf["w2"], BF16, gate=sv["relu"])
        g_w2 = mm_grad(n + "mlp2_dw", sv["act"], dz, a_sharded=True)
        g_w1 = mm_grad(n + "mlp1_dw", sv["x1"], dh, a_sharded=False)
        emit_grads("mlp%d" % layer, {"w1": g_w1, "w2": g_w2})
        dx = mm_t_cols(n + "mlp1_dx", dh, wf["w1"], F32, resid=dz)
        dz, gs["ln_mix_g"][layer], gs["ln_mix_b"][layer] = ln_bwd(
            n + "ln_mix_b", dx, sv["xhat1"], sv["rstd1"], sm["ln_mix_g"][layer][None])
        if layer % 2 == 0:
            dycat = mm_t_rows(n + "mixout_dx", dz, wm["w_out"], F32)
            g_out = mm_grad(n + "mixout_dw", sv["ycat"], dz, a_sharded=True)
            du_pool, gs["pool_w"][j], gs["pool_scale"][j] = pool_bwd(
                n + "pool_b", sv["proj"], dycat, sm["pool_w"][j], sm["pool_scale"][j][None])
            (du_lru, du_gate, gs["w_a"][j], gs["w_x"][j], gs["b_a"][j], gs["b_x"][j], gs["lam"][j], gs["conv_w"][j],
             gs["conv_b"][j]) = lru_bwd(n + "lru_b", sv["proj"], sv["hstate"], dycat, sm["conv_w"][j],
                                        sm["conv_b"][j][None], sm["w_a"][j], sm["b_a"][j][None], sm["w_x"][j],
                                        sm["b_x"][j][None], sm["lam"][j][None])
            dproj = jnp.concatenate([du_pool, du_lru, du_gate], axis=1)
            g_in = mm_grad(n + "proj_dw", sv["x"], dproj, a_sharded=False)
            emit_grads("mix%d" % layer, {"w_in": g_in, "w_out": g_out})
            dx = mm_t_cols(n + "proj_dx", dproj, wm["w_in"], F32, resid=dz)
        else:
            do = mm_t_rows(n + "attnout_dx", dz, wm["w_o"], BF16)
            g_o = mm_grad(n + "attnout_dw", sv["o"], dz, a_sharded=True)
            dq, dkv, dkp = attn_bwd(n + "attn_b", sv["q"], sv["kv"], sv["kp"], do, tabs)
            g_qb = mm_grad(n + "q_dw", sv["cq"], dq, a_sharded=False)
            dcq = mm_t_cols(n + "q_dx", dq, wm["w_qb"], F32)
            g_kvb = mm_grad(n + "kv_dw", sv["ckv"], dkv, a_sharded=False)
            dckv = mm_t_cols(n + "kv_dx", dkv, wm["w_kvb"], F32)
            ddown, gs["gq"][j], gs["gkv"][j] = mla_prep_bwd(
                n + "prep_b", sv["down"], dcq, dckv, dkp, sm["gq"][j][None], sm["gkv"][j][None], tabs)
            g_down = mm_grad(n + "down_dw", sv["x"], ddown, a_sharded=True)
            emit_grads("mix%d" % layer, {"w_down": g_down, "w_qb": g_qb, "w_kvb": g_kvb, "w_o": g_o})
            dx = mm_t_rows(n + "down_dx", ddown, wm["w_down"], F32, resid=dz)
    gs = {k: jnp.stack([a.reshape(sm[k].shape[1:]) for a in v]) for k, v in gs.items()}
    return loss_tile[0, 0], dx, gs


def _place():
    x, y, c = lax.axis_index("x"), lax.axis_index("y"), lax.axis_index("c")
    chips = [(1 - x, y), (x, 1 - y), (1 - x, 1 - y)]
    return x, y, c, chips


def _hbm_call(body, name, args, out_shape, scratch, aliases=None):
    return pl.pallas_call(
        body, name=name, in_specs=[pl.BlockSpec(memory_space=pl.ANY)] * len(args),
        out_specs=[pl.BlockSpec(memory_space=pl.ANY)] * len(out_shape), out_shape=out_shape,
        scratch_shapes=scratch, input_output_aliases=aliases or {},
        compiler_params=pltpu.CompilerParams(has_side_effects=True),
    )(*args)


HBM_SPEC = pl.BlockSpec(memory_space=pltpu.HBM)
SEM_SPEC = pl.BlockSpec(memory_space=pltpu.SEMAPHORE)
EFFECT = pltpu.SideEffectType.DATAFLOW_SIDE_EFFECTING


def _remote(src, dst, send_sem, recv_sem, device):
    return pltpu.make_async_remote_copy(src_ref=src, dst_ref=dst, send_sem=send_sem, recv_sem=recv_sem,
                                        device_id=device, device_id_type=MESH)


def split_start(name, srcs, land_shapes, n_sems, plan, token_in=None):
    n = len(srcs)
    srcs = [pltpu.with_memory_space_constraint(s, pltpu.HBM) for s in srcs]
    lands = [pltpu.with_memory_space_constraint(lax.empty(shp, dt), pltpu.HBM) for shp, dt in land_shapes]
    extra = [] if token_in is None else [token_in]

    def body(*refs):
        src_refs, land_refs = refs[:n], refs[n:2 * n]
        outs = refs[2 * n + len(extra):]
        send_sems, recv_sems, token, local_sems = outs[0], outs[1], outs[2 + 2 * n], outs[3 + 2 * n]
        local, sends, _ = plan(src_refs, land_refs, send_sems, recv_sems)
        local = [pltpu.make_async_copy(s, d, local_sems.at[k]) for k, (s, d) in enumerate(local)]
        for cp in local + sends:
            cp.start()
        for cp in local:
            cp.wait()
        token[...] = jnp.zeros_like(token)

    out_shape = (pltpu.SemaphoreType.DMA((n * n_sems,)), pltpu.SemaphoreType.DMA((n * n_sems,)),
                 *[pltpu.HBM(a.shape, a.dtype) for a in srcs + lands], jax.ShapeDtypeStruct((8, 128), F32))
    res = pl.pallas_call(
        body, name=name, out_shape=out_shape,
        in_specs=[HBM_SPEC] * (2 * n) + [pl.BlockSpec(memory_space=pl.ANY)] * len(extra),
        out_specs=(SEM_SPEC, SEM_SPEC, *[HBM_SPEC] * (2 * n), pl.BlockSpec(memory_space=pltpu.VMEM)),
        input_output_aliases={i: 2 + i for i in range(2 * n)},
        scratch_shapes=[pltpu.SemaphoreType.DMA((n,))],
        compiler_params=pltpu.CompilerParams(has_side_effects=EFFECT),
    )(*srcs, *lands, *extra)
    return dict(send=res[0], recv=res[1], srcs=list(res[2:2 + n]), lands=list(res[2 + n:2 + 2 * n]), plan=plan), res[-1]


def split_wait(name, started, after):
    sizes = [len(st["srcs"]) for st in started]
    n_buf = 2 * sum(sizes)

    def body(*refs):
        bufs, sems = refs[:n_buf], refs[n_buf:n_buf + 2 * len(started)]
        off = 0
        for g, (st, n) in enumerate(zip(started, sizes)):
            src_refs, land_refs = bufs[off:off + n], bufs[off + n:off + 2 * n]
            off += 2 * n
            _, sends, expects = st["plan"](src_refs, land_refs, sems[2 * g], sems[2 * g + 1])
            for cp in sends:
                cp.wait_send()
            for cp in expects:
                cp.wait_recv()

    bufs = [a for st in started for a in st["srcs"] + st["lands"]]
    sems = [s for st in started for s in (st["send"], st["recv"])]
    res = pl.pallas_call(
        body, name=name, out_shape=tuple(pltpu.HBM(a.shape, a.dtype) for a in bufs),
        in_specs=[HBM_SPEC] * n_buf + [SEM_SPEC] * len(sems) + [pl.BlockSpec(memory_space=pl.ANY)],
        out_specs=tuple([HBM_SPEC] * n_buf), input_output_aliases={i: i for i in range(n_buf)},
        compiler_params=pltpu.CompilerParams(has_side_effects=EFFECT),
    )(*bufs, *sems, after)
    out, off = [], 0
    for n in sizes:
        out.append(list(res[off + n:off + 2 * n]))
        off += 2 * n
    return out


def gather_plan(src_refs, land_refs, send_sems, recv_sems):
    x, y, c, chips = _place()
    me = 2 * x + y
    local = [(s, d.at[me]) for s, d in zip(src_refs, land_refs)]
    sends, expects = [], []
    for k, (s, d) in enumerate(zip(src_refs, land_refs)):
        for j, (px, py) in enumerate(chips):
            sem = 3 * k + j
            sends.append(_remote(s, d.at[me], send_sems.at[sem], recv_sems.at[sem], (px, py, c)))
            expects.append(_remote(s, d.at[2 * px + py], send_sems.at[sem], recv_sems.at[sem], (px, py, c)))
    return local, sends, expects


def reduce_plan(src_refs, land_refs, send_sems, recv_sems):
    x, y, c, chips = _place()
    me_chip, me_dev = 2 * x + y, 4 * x + 2 * y + c
    local, sends, expects = [], [], []
    for k, (s, d) in enumerate(zip(src_refs, land_refs)):
        hr = s.shape[1] // 2

        def part(chip_idx, h, s=s, hr=hr):
            return s.at[chip_idx if s.shape[0] == N_CHIPS else 0, pl.ds(h * hr, hr)]

        local.append((part(me_chip, c), d.at[me_dev]))
        for j, (px, py) in enumerate(chips):
            for h in range(2):
                sends.append(_remote(part(2 * px + py, h), d.at[me_dev], send_sems.at[7 * k + 2 * j + h],
                                     recv_sems.at[7 * k + 2 * j + c], (px, py, h)))
                expects.append(_remote(part(me_chip, c), d.at[4 * px + 2 * py + h], send_sems.at[7 * k + 2 * j + h],
                                       recv_sems.at[7 * k + 2 * j + h], (px, py, h)))
        sends.append(_remote(part(me_chip, 1 - c), d.at[me_dev], send_sems.at[7 * k + 6], recv_sems.at[7 * k + 6],
                             (x, y, 1 - c)))
        expects.append(_remote(part(me_chip, c), d.at[me_dev + 1 - 2 * c], send_sems.at[7 * k + 6],
                               recv_sems.at[7 * k + 6], (x, y, 1 - c)))
    return local, sends, expects


def sibling_swap_halves(fulls):
    n = len(fulls)

    def body(*refs):
        outs = refs[n:2 * n]
        send_sems, recv_sems = refs[2 * n:]
        x, y, c, _ = _place()
        copies = []
        for k in range(n):
            nl, rows = fulls[k].shape[:2]
            hr = rows // 2
            mine = outs[k].at[pl.ds(0, nl), pl.ds(c * hr, hr)]
            theirs = outs[k].at[pl.ds(0, nl), pl.ds((1 - c) * hr, hr)]
            copies.append((_remote(mine, mine, send_sems.at[k], recv_sems.at[k], (x, y, 1 - c)),
                           _remote(mine, theirs, send_sems.at[k], recv_sems.at[k], (x, y, 1 - c))))
        for send, _ in copies:
            send.start()
        for send, recv in copies:
            send.wait_send()
            recv.wait_recv()

    out_shape = [jax.ShapeDtypeStruct(f.shape, f.dtype) for f in fulls]
    scratch = [pltpu.SemaphoreType.DMA((n,)), pltpu.SemaphoreType.DMA((n,))]
    return _hbm_call(body, "sibling_swap_halves", fulls, out_shape, scratch, aliases={k: k for k in range(n)})


def _tile_rows(rows, cols, budget_bytes):
    best = None
    for t in range(16, rows + 1, 16):
        if rows % t == 0 and t * cols * 4 <= budget_bytes:
            best = t
    assert best is not None, (rows, cols)
    return best


N_DEV = 8


def sum_devices(name, landed, layer, n_layers, prev, c_arr):
    _, hr, cols = landed.shape
    tr = _tile_rows(hr, cols, 512 * 1024)
    n_blk = hr // tr

    def body(c_ref, r_ref, *rest):
        o_ref = rest[-1]
        acc = r_ref[0].astype(F32)
        for d in range(1, N_DEV):
            acc = acc + r_ref[d].astype(F32)
        o_ref[...] = acc

    in_specs = [pl.BlockSpec((N_DEV, tr, cols), lambda r, c_ref: (0, r, 0))]
    args = [c_arr, landed]
    aliases = {}
    if prev is not None:
        in_specs.append(pl.BlockSpec(memory_space=pl.ANY))
        args.append(prev)
        aliases = {2: 0}
    return pl.pallas_call(
        body, name=name, out_shape=jax.ShapeDtypeStruct((n_layers, 2 * hr, cols), F32),
        grid_spec=pltpu.PrefetchScalarGridSpec(
            num_scalar_prefetch=1, grid=(n_blk,), in_specs=in_specs,
            out_specs=pl.BlockSpec((None, tr, cols), lambda r, c_ref: (layer, c_ref[0] * n_blk + r, 0))),
        input_output_aliases=aliases, compiler_params=_params(("parallel",)),
    )(*args)


def _pack(arrs, rows_multiple):
    flat = []
    for a in arrs:
        v = a.reshape(-1).astype(F32)
        flat.append(jnp.pad(v, (0, (-v.shape[0]) % HEAD)))
    v = jnp.concatenate(flat)
    v = jnp.pad(v, (0, (-v.shape[0]) % (HEAD * rows_multiple)))
    return v.reshape(-1, HEAD)


def _unpack(packed, shapes):
    flat = packed.reshape(-1)
    out, off = [], 0
    for shp in shapes:
        size = int(np.prod(shp))
        out.append(flat[off:off + size].reshape(shp))
        off += size + (-size) % HEAD
    return out


BIG = ["even_w_in", "even_w_out", "mla_w_down", "mla_w_qb", "mla_w_kvb", "mla_w_o", "mlp_w1", "mlp_w2"]
BIG_KEY = {"even_w_in": "w_in", "even_w_out": "w_out", "mla_w_down": "w_down", "mla_w_qb": "w_qb",
           "mla_w_kvb": "w_kvb", "mla_w_o": "w_o", "mlp_w1": "w1", "mlp_w2": "w2"}
SMALL_KEY = {"ln_mix_g": "ln_mix_g", "ln_mix_b": "ln_mix_b", "ln_ffn_g": "ln_ffn_g", "ln_ffn_b": "ln_ffn_b",
             "pool_w": "pool_w", "pool_scale": "pool_scale", "lru_conv_w": "conv_w", "lru_conv_b": "conv_b",
             "lru_w_a": "w_a", "lru_b_a": "b_a", "lru_w_x": "w_x", "lru_b_x": "b_x", "lru_lambda": "lam",
             "mla_q_norm_g": "gq", "mla_kv_norm_g": "gkv"}
SMALL = list(SMALL_KEY)
SMALL_SHARDED = ["lru_conv_w", "mla_q_norm_g", "mla_kv_norm_g"]
WEIGHTS = ["ln_mix_g", "ln_mix_b", "ln_ffn_g", "ln_ffn_b", "even_w_in", "pool_w", "pool_scale", "lru_conv_w",
           "lru_conv_b", "lru_w_a", "lru_b_a", "lru_w_x", "lru_b_x", "lru_lambda", "even_w_out", "mla_w_down",
           "mla_q_norm_g", "mla_kv_norm_g", "mla_w_qb", "mla_w_kvb", "mla_w_o", "mlp_w1", "mlp_w2"]


GROUPS = ["mix0", "mlp0", "mix1", "mlp1", "mix2", "mlp2", "mix3", "mlp3"]


def _group_keys(group):
    layer = int(group[3:])
    if group.startswith("mlp"):
        return [("mlp_w1", "w1", layer), ("mlp_w2", "w2", layer)]
    if layer % 2 == 0:
        return [("even_w_in", "w_in", layer // 2), ("even_w_out", "w_out", layer // 2)]
    return [("mla_w_down", "w_down", layer // 2), ("mla_w_qb", "w_qb", layer // 2),
            ("mla_w_kvb", "w_kvb", layer // 2), ("mla_w_o", "w_o", layer // 2)]


def _pad_q_heads(w):
    lead = w.shape[:-1]
    w = w.reshape(lead + (2, QK_NOPE + QK_ROPE))
    w = jnp.pad(w, ((0, 0),) * len(lead) + ((0, 0), (0, QHEAD_PAD - QK_NOPE - QK_ROPE)))
    return w.reshape(lead + (2 * QHEAD_PAD,))


def _unpad_q_heads(g):
    lead = g.shape[:-1]
    return g.reshape(lead + (2, QHEAD_PAD))[..., :QK_NOPE + QK_ROPE].reshape(lead + (2 * (QK_NOPE + QK_ROPE),))


def _step(x, positions, loss_target, w, m, v):
    cx, cy, cc = lax.axis_index("x"), lax.axis_index("y"), lax.axis_index("c")
    chip = 2 * cx + cy
    c_arr = jnp.reshape(cc, (1,)).astype(jnp.int32)

    prepared = dict(w)
    prepared["mla_w_down"] = jnp.pad(w["mla_w_down"], ((0, 0), (0, 0), (0, ODD_IN_PAD - ODD_IN)))
    prepared["mla_w_qb"] = _pad_q_heads(w["mla_w_qb"])
    small_shard_shapes = [w[k].shape for k in SMALL_SHARDED]
    gathering, token = {}, None
    for g in GROUPS:
        srcs = [prepared[name][idx].astype(BF16) for name, _, idx in _group_keys(g)]
        if g == GROUPS[0]:
            srcs.append(_pack([w[k] for k in SMALL_SHARDED], 32))
        shapes = [((N_CHIPS,) + s.shape, s.dtype) for s in srcs]
        gathering[g], token = split_start("gather_" + g, srcs, shapes, 3, gather_plan, token)
    all_started = token

    def weights_of(g, after):
        lands = split_wait("gathered_" + g, [gathering[g]], all_started if g == GROUPS[0] else after)[0]
        out = {key: land for (_, key, _), land in zip(_group_keys(g), lands)}
        if g == GROUPS[0]:
            per_chip = [_unpack(lands[-1][s], small_shard_shapes) for s in range(N_CHIPS)]
            for i, key in enumerate(("conv_w", "gq", "gkv")):
                out[key] = jnp.concatenate([p[i] for p in per_chip], axis=-1)
        return out

    reducing = []

    def emit_grads(g, grads):
        srcs = [grads[key] for _, key, _ in _group_keys(g)]
        shapes = [((N_DEV, s.shape[1] // 2, s.shape[2]), s.dtype) for s in srcs]
        reducing.append((g, split_start("reduce_" + g, srcs, shapes, 7, reduce_plan)[0]))

    sm = {SMALL_KEY[k]: w[k] for k in SMALL if k not in SMALL_SHARDED}
    loss, grad_x, gs = _local_step(x[0], positions.reshape(S, 1), loss_target[0], sm, weights_of, emit_grads)
    loss = lax.psum(loss, ("x", "y", "c"))

    small_shapes = [gs[SMALL_KEY[k]].shape for k in SMALL]
    gs_pack = _pack([gs[SMALL_KEY[k]] for k in SMALL], 32)[None]
    shapes = [((N_DEV, gs_pack.shape[1] // 2, HEAD), F32)]
    reducing.append(("small", split_start("reduce_small", [gs_pack], shapes, 7, reduce_plan)[0]))

    landed = split_wait("reduced", [st for _, st in reducing], grad_x)
    stacks = {}
    for (g, _), lands in zip(reducing, landed):
        if g == "small":
            stacks["small"] = sum_devices("sum_small", lands[0], 0, 1, None, c_arr)
            continue
        for (name, key, idx), land in zip(_group_keys(g), lands):
            stacks[name] = sum_devices("sum_%s%d" % (key, idx), land, idx, w[name].shape[0], stacks.get(name), c_arr)
    reduced = sibling_swap_halves([stacks[k] for k in BIG] + [stacks["small"]])
    g_big = dict(zip(BIG, reduced[:-1]))
    g_big["mla_w_down"] = g_big["mla_w_down"][..., :ODD_IN]
    g_big["mla_w_qb"] = _unpad_q_heads(g_big["mla_w_qb"])
    g_small = dict(zip(SMALL, _unpack(reduced[-1], small_shapes)))
    for k in SMALL_SHARDED:
        width = w[k].shape[-1]
        g_small[k] = lax.dynamic_slice_in_dim(g_small[k], chip * width, width, axis=-1)
    grad = {**g_big, **g_small}

    delta, new_m, new_v = {}, {}, {}
    for k in BIG:
        shp = w[k].shape
        view = lambda a: a.reshape(-1, shp[-1])
        d, nm, nv = adamw("adamw_" + BIG_KEY[k], view(w[k]), view(grad[k]), view(m[k]), view(v[k]))
        delta[k], new_m[k], new_v[k] = d.reshape(shp), nm.reshape(shp), nv.reshape(shp)
    shapes = [w[k].shape for k in SMALL]
    d, nm, nv = adamw("adamw_small", *[_pack([t[k] for k in SMALL], 512) for t in (w, grad, m, v)])
    for k, dk, mk, vk in zip(SMALL, _unpack(d, shapes), _unpack(nm, shapes), _unpack(nv, shapes)):
        delta[k], new_m[k], new_v[k] = dk, mk, vk
    return (loss, grad_x[None], *[grad[k] for k in WEIGHTS], *[delta[k] for k in WEIGHTS],
            *[new_m[k] for k in WEIGHTS], *[new_v[k] for k in WEIGHTS])


def kernel(x, positions, ln_mix_g, ln_mix_b, ln_ffn_g, ln_ffn_b, even_w_in, pool_w, pool_scale, lru_conv_w, lru_conv_b, lru_w_a, lru_b_a, lru_w_x, lru_b_x, lru_lambda, even_w_out, mla_w_down, mla_q_norm_g, mla_kv_norm_g, mla_w_qb, mla_w_kvb, mla_w_o, mlp_w1, mlp_w2, loss_target, m_ln_mix_g, m_ln_mix_b, m_ln_ffn_g, m_ln_ffn_b, m_even_w_in, m_pool_w, m_pool_scale, m_lru_conv_w, m_lru_conv_b, m_lru_w_a, m_lru_b_a, m_lru_w_x, m_lru_b_x, m_lru_lambda, m_even_w_out, m_mla_w_down, m_mla_q_norm_g, m_mla_kv_norm_g, m_mla_w_qb, m_mla_w_kvb, m_mla_w_o, m_mlp_w1, m_mlp_w2, v_ln_mix_g, v_ln_mix_b, v_ln_ffn_g, v_ln_ffn_b, v_even_w_in, v_pool_w, v_pool_scale, v_lru_conv_w, v_lru_conv_b, v_lru_w_a, v_lru_b_a, v_lru_w_x, v_lru_b_x, v_lru_lambda, v_even_w_out, v_mla_w_down, v_mla_q_norm_g, v_mla_kv_norm_g, v_mla_w_qb, v_mla_w_kvb, v_mla_w_o, v_mlp_w1, v_mlp_w2):
    args = locals()
    w = {k: args[k] for k in WEIGHTS}
    m = {k: args["m_" + k] for k in WEIGHTS}
    v = {k: args["v_" + k] for k in WEIGHTS}
    return _step(x, positions, loss_target, w, m, v)
```

```python
import functools
import math

import jax
import jax.numpy as jnp
import numpy as np
from jax import lax
from jax.experimental import pallas as pl
from jax.experimental.pallas import tpu as pltpu

F32 = jnp.float32
BF16 = jnp.bfloat16

S = 2048
D = 1024
DEPTH = 4
N_CHIPS = 4
POOL_WINDOWS = (2, 4, 8, 16)
POOL_W = 512
LRU_W = 1024
LRU_HEADS = 8
HEAD = 128
EVEN_IN = 2560
EVEN_MIX = 1536
MLA_HEADS = 8
QK_NOPE = 128
QK_ROPE = 64
Q_RANK = 384
KV_RANK = 256
ODD_IN = 704
ODD_IN_PAD = 768
QHEAD_PAD = 256
D_FF = 4096
CHUNK = 64
ALPHA = (2 * DEPTH) ** 0.25
LN_EPS = 1e-5
RMS_EPS = 1e-6
ATT_SCALE = (QK_NOPE + QK_ROPE) ** -0.5
ROPE_THETA = 10000.0
LRU_C = 8.0
ADAM_LR = 0.001
ADAM_B1 = 0.9
ADAM_B2 = 0.999
ADAM_EPS = 1e-08
ADAM_WD = 0.01
ADAM_STEP = 10

VMEM_LIMIT = 56 * 1024 * 1024
MESH = pl.DeviceIdType.MESH

_NN = (((1,), (0,)), ((), ()))
_NT = (((1,), (1,)), ((), ()))
_TN = (((0,), (0,)), ((), ()))


def _params(sem=None, **kw):
    return pltpu.CompilerParams(dimension_semantics=sem, vmem_limit_bytes=VMEM_LIMIT, **kw)


def _dot(a, b, dims):
    return lax.dot_general(a.astype(BF16), b.astype(BF16), dims, preferred_element_type=F32)


def _mm(name, a, w4, *, dims, a_block, a_map, w_map, grid, sem, outs, out_blocks, out_maps,
        red_axis=None, n_red=1, extras=(), extra_blocks=(), extra_maps=(), epilogue=None, after=None):
    n_extra = len(extras)
    n_out = len(outs)
    n_after = 0 if after is None else 1
    w_block = (None,) + tuple(w4.shape[1:])

    def body(a_ref, w_ref, *rest):
        rest = rest[n_after:]
        extra_refs = rest[:n_extra]
        out_refs = rest[n_extra:n_extra + n_out]
        p = _dot(a_ref[...], w_ref[...], dims)

        def finish(acc):
            vals = epilogue(acc, *[r[...] for r in extra_refs]) if epilogue else (acc,)
            for r, v in zip(out_refs, vals):
                r[...] = v.astype(r.dtype)

        if red_axis is None:
            finish(p)
        else:
            acc_ref = rest[-1]
            k = pl.program_id(red_axis)

            @pl.when(k == 0)
            def _():
                acc_ref[...] = p

            @pl.when(k > 0)
            def _():
                acc_ref[...] += p

            @pl.when(k == n_red - 1)
            def _():
                finish(acc_ref[...])

    scratch = [] if red_axis is None else [pltpu.VMEM(out_blocks[0], F32)]
    return pl.pallas_call(
        body, name=name, grid=grid,
        in_specs=[pl.BlockSpec(a_block, a_map), pl.BlockSpec(w_block, w_map)]
        + [pl.BlockSpec(memory_space=pl.ANY)] * n_after + [pl.BlockSpec(b, m) for b, m in zip(extra_blocks, extra_maps)],
        out_specs=[pl.BlockSpec(b, m) for b, m in zip(out_blocks, out_maps)],
        out_shape=outs, scratch_shapes=scratch, compiler_params=_params(sem),
    )(a, w4, *([after] * n_after), *extras)


def _w_map_outer(s, i):
    return (s, 0, 0)


def _w_map_inner(i, s):
    return (s, 0, 0)


def mm_cols(name, a, w4, out_dtypes, *, tm=512, epilogue=None):
    m, k = a.shape
    nb = w4.shape[2]
    outs = [jax.ShapeDtypeStruct((m, N_CHIPS * nb), dt) for dt in out_dtypes]
    return _mm(name, a, w4, dims=_NN, a_block=(tm, k), a_map=lambda s, i: (i, 0), w_map=_w_map_outer,
               grid=(N_CHIPS, m // tm), sem=("parallel", "parallel"), outs=outs,
               out_blocks=[(tm, nb)] * len(outs), out_maps=[lambda s, i: (i, s)] * len(outs), epilogue=epilogue)


def mm_rows(name, a, w4, out_dtype, *, tm=512):
    m = a.shape[0]
    kb, n = w4.shape[1:]
    outs = [jax.ShapeDtypeStruct((m, n), out_dtype)]
    return _mm(name, a, w4, dims=_NN, a_block=(tm, kb), a_map=lambda i, s: (i, s), w_map=_w_map_inner,
               grid=(m // tm, N_CHIPS), sem=("parallel", "arbitrary"), outs=outs, out_blocks=[(tm, n)],
               out_maps=[lambda i, s: (i, 0)], red_axis=1, n_red=N_CHIPS)[0]


def mm_t_cols(name, g, w4, out_dtype, *, tm=512, resid=None, after=None):
    m = g.shape[0]
    k, nb = w4.shape[1:]
    outs = [jax.ShapeDtypeStruct((m, k), out_dtype)]
    extras = () if resid is None else (resid,)
    epi = None if resid is None else (lambda acc, r: (acc + ALPHA * r,))
    return _mm(name, g, w4, dims=_NT, a_block=(tm, nb), a_map=lambda i, s: (i, s), w_map=_w_map_inner,
               grid=(m // tm, N_CHIPS), sem=("parallel", "arbitrary"), outs=outs, out_blocks=[(tm, k)],
               out_maps=[lambda i, s: (i, 0)], red_axis=1, n_red=N_CHIPS, extras=extras,
               extra_blocks=[(tm, k)] * len(extras), extra_maps=[lambda i, s: (i, 0)] * len(extras), epilogue=epi,
               after=after)[0]


def mm_t_rows(name, g, w4, out_dtype, *, tm=512, resid=None, gate=None, after=None):
    m, n = g.shape
    kb = w4.shape[1]
    outs = [jax.ShapeDtypeStruct((m, N_CHIPS * kb), out_dtype)]
    extras, epi = (), None
    if resid is not None:
        extras, epi = (resid,), (lambda acc, r: (acc + ALPHA * r,))
    if gate is not None:
        extras, epi = (gate,), (lambda acc, r: (acc * (2.0 * r.astype(F32)),))
    return _mm(name, g, w4, dims=_NT, a_block=(tm, n), a_map=lambda s, i: (i, 0), w_map=_w_map_outer,
               grid=(N_CHIPS, m // tm), sem=("parallel", "parallel"), outs=outs, out_blocks=[(tm, kb)],
               out_maps=[lambda s, i: (i, s)], extras=extras, extra_blocks=[(tm, kb)] * len(extras),
               extra_maps=[lambda s, i: (i, s)] * len(extras), epilogue=epi, after=after)[0]


def mm_grad(name, a, g, *, a_sharded, tm=512):
    m = a.shape[0]
    ka = a.shape[1] // N_CHIPS if a_sharded else a.shape[1]
    ng = g.shape[1] if a_sharded else g.shape[1] // N_CHIPS
    n_red = m // tm
    a_map = (lambda s, i: (i, s)) if a_sharded else (lambda s, i: (i, 0))
    g_map = (lambda s, i: (i, 0)) if a_sharded else (lambda s, i: (i, s))

    def body(a_ref, g_ref, out_ref, acc_ref):
        p = _dot(a_ref[...], g_ref[...], _TN)
        k = pl.program_id(1)

        @pl.when(k == 0)
        def _():
            acc_ref[...] = p

        @pl.when(k > 0)
        def _():
            acc_ref[...] += p

        @pl.when(k == n_red - 1)
        def _():
            out_ref[...] = acc_ref[...].astype(out_ref.dtype)

    return pl.pallas_call(
        body, name=name, grid=(N_CHIPS, n_red),
        in_specs=[pl.BlockSpec((tm, ka), a_map), pl.BlockSpec((tm, ng), g_map)],
        out_specs=pl.BlockSpec((None, ka, ng), lambda s, i: (s, 0, 0)),
        out_shape=jax.ShapeDtypeStruct((N_CHIPS, ka, ng), BF16),
        scratch_shapes=[pltpu.VMEM((ka, ng), F32)], compiler_params=_params(("parallel", "arbitrary")),
    )(a, g)


def ln_fwd(name, x, s, g, b, *, tm=256):
    def body(x_ref, s_ref, g_ref, b_ref, y_ref, xhat_ref, rstd_ref):
        z = ALPHA * x_ref[...] + s_ref[...]
        mu = jnp.mean(z, axis=-1, keepdims=True)
        zc = z - mu
        var = jnp.mean(zc * zc, axis=-1, keepdims=True)
        rstd = lax.rsqrt(var + LN_EPS)
        xhat = zc * rstd
        y_ref[...] = xhat * g_ref[...] + b_ref[...]
        xhat_ref[...] = xhat
        rstd_ref[...] = rstd

    row = pl.BlockSpec((tm, D), lambda i: (i, 0))
    vec = pl.BlockSpec((1, D), lambda i: (0, 0))
    return pl.pallas_call(
        body, name=name, grid=(S // tm,), in_specs=[row, row, vec, vec],
        out_specs=[row, row, pl.BlockSpec((tm, 1), lambda i: (i, 0))],
        out_shape=[jax.ShapeDtypeStruct((S, D), F32), jax.ShapeDtypeStruct((S, D), F32),
                   jax.ShapeDtypeStruct((S, 1), F32)],
        compiler_params=_params(("parallel",)),
    )(x, s, g, b)


def ln_bwd(name, dy, xhat, rstd, g, *, tm=256):
    def body(dy_ref, xhat_ref, rstd_ref, g_ref, dz_ref, dg_ref, db_ref):
        dy_v = dy_ref[...]
        xh = xhat_ref[...]
        dxh = dy_v * g_ref[...]
        m1 = jnp.mean(dxh, axis=-1, keepdims=True)
        m2 = jnp.mean(dxh * xh, axis=-1, keepdims=True)
        dz_ref[...] = rstd_ref[...] * (dxh - m1 - xh * m2)
        pg = jnp.sum(dy_v * xh, axis=0, keepdims=True)
        pb = jnp.sum(dy_v, axis=0, keepdims=True)
        i = pl.program_id(0)

        @pl.when(i == 0)
        def _():
            dg_ref[...] = pg
            db_ref[...] = pb

        @pl.when(i > 0)
        def _():
            dg_ref[...] += pg
            db_ref[...] += pb

    row = pl.BlockSpec((tm, D), lambda i: (i, 0))
    vec = pl.BlockSpec((1, D), lambda i: (0, 0))
    return pl.pallas_call(
        body, name=name, grid=(S // tm,), in_specs=[row, row, pl.BlockSpec((tm, 1), lambda i: (i, 0)), vec],
        out_specs=[row, vec, vec],
        out_shape=[jax.ShapeDtypeStruct((S, D), F32), jax.ShapeDtypeStruct((1, D), F32),
                   jax.ShapeDtypeStruct((1, D), F32)],
        compiler_params=_params(("arbitrary",)),
    )(dy, xhat, rstd, g)


def loss_head(y, target, *, tm=256):
    def body(y_ref, t_ref, dy_ref, loss_ref):
        e = y_ref[...] - t_ref[...]
        dy_ref[...] = e * (1.0 / D)
        part = jnp.sum(jnp.sum(e * e, axis=-1, keepdims=True), axis=0, keepdims=True) * (0.5 / D)
        i = pl.program_id(0)

        @pl.when(i == 0)
        def _():
            loss_ref[...] = jnp.zeros_like(loss_ref)

        loss_ref[...] += jnp.broadcast_to(part, loss_ref.shape)

    row = pl.BlockSpec((tm, D), lambda i: (i, 0))
    return pl.pallas_call(
        body, name="loss_head", grid=(S // tm,), in_specs=[row, row],
        out_specs=[row, pl.BlockSpec((8, 128), lambda i: (0, 0))],
        out_shape=[jax.ShapeDtypeStruct((S, D), F32), jax.ShapeDtypeStruct((8, 128), F32)],
        compiler_params=_params(("arbitrary",)),
    )(y, target)


def _rows(shape):
    return lax.broadcasted_iota(jnp.int32, shape, 0)


def _shift_down(x, k):
    return jnp.where(_rows(x.shape) >= k, pltpu.roll(x, k, 0), 0.0)


def _shift_up(x, k):
    n = x.shape[0]
    return jnp.where(_rows(x.shape) < n - k, pltpu.roll(x, n - k, 0), 0.0)


def _pool_diff(u, w):
    acc, k = u, 1
    while k < w:
        acc = acc + _shift_down(acc, k)
        k *= 2
    cnt = jnp.minimum(_rows(u.shape) + 1, w).astype(F32)
    return acc / cnt - u, cnt


def pool_fwd(name, proj, pool_w, pool_scale):
    def body(u_ref, w_ref, sc_ref, y_ref):
        for g, w in enumerate(POOL_WINDOWS):
            cols = slice(g * HEAD, (g + 1) * HEAD)
            d, _ = _pool_diff(u_ref[:, cols], w)
            z = _dot(d, w_ref[g], _NN)
            y_ref[:, cols] = (z * sc_ref[:, cols]).astype(y_ref.dtype)

    return pl.pallas_call(
        body, name=name, grid=(1,),
        in_specs=[pl.BlockSpec((S, POOL_W), lambda i: (0, 0)),
                  pl.BlockSpec((4, HEAD, HEAD), lambda i: (0, 0, 0)),
                  pl.BlockSpec((1, POOL_W), lambda i: (0, 0))],
        out_specs=pl.BlockSpec((S, POOL_W), lambda i: (0, 0)),
        out_shape=jax.ShapeDtypeStruct((S, POOL_W), BF16),
        compiler_params=_params(("arbitrary",)),
    )(proj, pool_w, pool_scale)


def pool_bwd(name, proj, dycat, pool_w, pool_scale):
    def body(u_ref, dy_ref, w_ref, sc_ref, du_ref, dw_ref, dsc_ref):
        for g, w in enumerate(POOL_WINDOWS):
            cols = slice(g * HEAD, (g + 1) * HEAD)
            d, cnt = _pool_diff(u_ref[:, cols], w)
            dy = dy_ref[:, cols]
            z = _dot(d, w_ref[g], _NN)
            dsc_ref[:, cols] = jnp.sum(dy * z, axis=0, keepdims=True)
            dz = dy * sc_ref[:, cols]
            dw_ref[g] = _dot(d, dz, _TN)
            dd = _dot(dz, w_ref[g], _NT)
            acc, k = dd / cnt, 1
            while k < w:
                acc = acc + _shift_up(acc, k)
                k *= 2
            du_ref[:, cols] = (acc - dd).astype(du_ref.dtype)

    return pl.pallas_call(
        body, name=name, grid=(1,),
        in_specs=[pl.BlockSpec((S, POOL_W), lambda i: (0, 0)),
                  pl.BlockSpec((S, POOL_W), lambda i: (0, 0)),
                  pl.BlockSpec((4, HEAD, HEAD), lambda i: (0, 0, 0)),
                  pl.BlockSpec((1, POOL_W), lambda i: (0, 0))],
        out_specs=[pl.BlockSpec((S, POOL_W), lambda i: (0, 0)),
                   pl.BlockSpec((4, HEAD, HEAD), lambda i: (0, 0, 0)),
                   pl.BlockSpec((1, POOL_W), lambda i: (0, 0))],
        out_shape=[jax.ShapeDtypeStruct((S, POOL_W), BF16), jax.ShapeDtypeStruct((4, HEAD, HEAD), F32),
                   jax.ShapeDtypeStruct((1, POOL_W), F32)],
        compiler_params=_params(("arbitrary",)),
    )(proj, dycat, pool_w, pool_scale)


def _expm1(x):
    series = x * (1.0 + x * (0.5 + x * (1.0 / 6.0 + x * (1.0 / 24.0 + x * (1.0 / 120.0)))))
    return jnp.where(jnp.abs(x) < 0.05, series, jnp.exp(x) - 1.0)


def _softplus_neg(lam):
    e = jnp.exp(-jnp.abs(lam))
    log1p = jnp.where(e < 0.01, e * (1.0 - e * (0.5 - e * (1.0 / 3.0))), jnp.log(1.0 + e))
    return jnp.maximum(-lam, 0.0) + log1p


_GELU_C = math.sqrt(2.0 / math.pi)


def _gelu(x):
    t = jnp.tanh(_GELU_C * (x + 0.044715 * x * x * x))
    return 0.5 * x * (1.0 + t), t


def _gelu_grad(x, t):
    return 0.5 * (1.0 + t) + 0.5 * x * (1.0 - t * t) * _GELU_C * (1.0 + 3.0 * 0.044715 * x * x)


def _conv(u, cw, cb):
    return cw[3:4] * u + cw[2:3] * _shift_down(u, 1) + cw[1:2] * _shift_down(u, 2) + cw[0:1] * _shift_down(u, 3) + cb


def _lru_gates(cu, wa, ba, wx, bx, lam):
    r = jax.nn.sigmoid(_dot(cu, wa, _NN) + ba)
    i = jax.nn.sigmoid(_dot(cu, wx, _NN) + bx)
    sp = _softplus_neg(lam)
    log_a = (-LRU_C) * r * sp
    a = jnp.exp(log_a)
    mult = jnp.sqrt(-_expm1(2.0 * log_a))
    return r, i, sp, a, mult


def _scan(a_ref, b_ref, h_ref, *, reverse):
    n_blk = S // 8
    row8 = lax.broadcasted_iota(jnp.int32, (8, HEAD), 0)

    def step(j, carry):
        blk = (n_blk - 1 - j) if reverse else j
        r0 = pl.multiple_of(blk * 8, 8)
        a = a_ref[pl.ds(r0, 8), :]
        b = b_ref[pl.ds(r0, 8), :]
        for k in (1, 2, 4):
            if reverse:
                keep = row8 < 8 - k
                a_s, b_s = pltpu.roll(a, 8 - k, 0), pltpu.roll(b, 8 - k, 0)
            else:
                keep = row8 >= k
                a_s, b_s = pltpu.roll(a, k, 0), pltpu.roll(b, k, 0)
            b = jnp.where(keep, a * b_s + b, b)
            a = jnp.where(keep, a * a_s, a)
        h = b + a * carry
        h_ref[pl.ds(r0, 8), :] = h
        edge = h[0:1, :] if reverse else h[7:8, :]
        return jnp.broadcast_to(edge, (8, HEAD))

    lax.fori_loop(0, n_blk, step, jnp.zeros((8, HEAD), F32), unroll=4)


def _lru_specs():
    def col(off):
        return pl.BlockSpec((S, HEAD), lambda h: (0, off + h))
    vec = pl.BlockSpec((1, HEAD), lambda h: (0, h))
    mat = pl.BlockSpec((None, HEAD, HEAD), lambda h: (h, 0, 0))
    cw = pl.BlockSpec((4, HEAD), lambda h: (0, h))
    return col, vec, mat, cw


def lru_fwd(name, proj, conv_w, conv_b, w_a, b_a, w_x, b_x, lam):
    def body(u_ref, ug_ref, cw_ref, cb_ref, wa_ref, ba_ref, wx_ref, bx_ref, lam_ref, y_ref, h_ref, a_s, b_s):
        cu = _conv(u_ref[...], cw_ref[...], cb_ref[...])
        _, i, _, a, mult = _lru_gates(cu, wa_ref[...], ba_ref[...], wx_ref[...], bx_ref[...], lam_ref[...])
        a_s[...] = a
        b_s[...] = mult * (i * cu)
        _scan(a_s, b_s, h_ref, reverse=False)
        gl, _ = _gelu(ug_ref[...])
        y_ref[...] = (h_ref[...] * gl).astype(y_ref.dtype)

    col, vec, mat, cw = _lru_specs()
    out = pl.BlockSpec((S, HEAD), lambda h: (0, h))
    return pl.pallas_call(
        body, name=name, grid=(LRU_HEADS,),
        in_specs=[col(4), col(12), cw, vec, mat, vec, mat, vec, vec],
        out_specs=[out, out],
        out_shape=[jax.ShapeDtypeStruct((S, LRU_W), BF16), jax.ShapeDtypeStruct((S, LRU_W), F32)],
        scratch_shapes=[pltpu.VMEM((S, HEAD), F32), pltpu.VMEM((S, HEAD), F32)],
        compiler_params=_params(("parallel",)),
    )(proj, proj, conv_w, conv_b, w_a, b_a, w_x, b_x, lam)


def lru_bwd(name, proj, hstate, dycat, conv_w, conv_b, w_a, b_a, w_x, b_x, lam):
    def body(u_ref, ug_ref, h_ref, dy_ref, cw_ref, cb_ref, wa_ref, ba_ref, wx_ref, bx_ref, lam_ref,
             du_ref, dug_ref, dwa_ref, dwx_ref, dba_ref, dbx_ref, dlam_ref, dcw_ref, dcb_ref, a_s, b_s, g_s):
        u = u_ref[...]
        cw = cw_ref[...]
        cu = _conv(u, cw, cb_ref[...])
        lam_v = lam_ref[...]
        r, i, sp, a, mult = _lru_gates(cu, wa_ref[...], ba_ref[...], wx_ref[...], bx_ref[...], lam_v)
        ug = ug_ref[...]
        gl, t = _gelu(ug)
        dy = dy_ref[...]
        h = h_ref[...]
        dug_ref[...] = (dy * h * _gelu_grad(ug, t)).astype(dug_ref.dtype)
        a_s[...] = _shift_up(a, 1)
        b_s[...] = dy * gl
        _scan(a_s, b_s, g_s, reverse=True)
        dxin = g_s[...]
        da = dxin * _shift_down(h, 1)
        dmult = dxin * (i * cu)
        di = dxin * (mult * cu)
        dlog_a = da * a - dmult * (a * a) / mult
        dr_pre = dlog_a * ((-LRU_C) * sp) * (r * (1.0 - r))
        di_pre = di * (i * (1.0 - i))
        dsp = jnp.sum(dlog_a * ((-LRU_C) * r), axis=0, keepdims=True)
        dlam_ref[...] = dsp * (-jax.nn.sigmoid(-lam_v))
        dba_ref[...] = jnp.sum(dr_pre, axis=0, keepdims=True)
        dbx_ref[...] = jnp.sum(di_pre, axis=0, keepdims=True)
        dwa_ref[...] = _dot(cu, dr_pre, _TN)
        dwx_ref[...] = _dot(cu, di_pre, _TN)
        dcu = dxin * (mult * i) + _dot(dr_pre, wa_ref[...], _NT) + _dot(di_pre, wx_ref[...], _NT)
        dcb_ref[...] = jnp.sum(dcu, axis=0, keepdims=True)
        for k in range(4):
            dcw_ref[k:k + 1, :] = jnp.sum(dcu * (_shift_down(u, 3 - k) if k < 3 else u), axis=0, keepdims=True)
        du = cw[3:4] * dcu + cw[2:3] * _shift_up(dcu, 1) + cw[1:2] * _shift_up(dcu, 2) + cw[0:1] * _shift_up(dcu, 3)
        du_ref[...] = du.astype(du_ref.dtype)

    col, vec, mat, cw = _lru_specs()
    out = pl.BlockSpec((S, HEAD), lambda h: (0, h))
    big = jax.ShapeDtypeStruct((S, LRU_W), BF16)
    vec_shape = jax.ShapeDtypeStruct((1, LRU_W), F32)
    mat_shape = jax.ShapeDtypeStruct((LRU_HEADS, HEAD, HEAD), F32)
    return pl.pallas_call(
        body, name=name, grid=(LRU_HEADS,),
        in_specs=[col(4), col(12), out, col(4), cw, vec, mat, vec, mat, vec, vec],
        out_specs=[out, out, mat, mat, vec, vec, vec, cw, vec],
        out_shape=[big, big, mat_shape, mat_shape, vec_shape, vec_shape, vec_shape,
                   jax.ShapeDtypeStruct((4, LRU_W), F32), vec_shape],
        scratch_shapes=[pltpu.VMEM((S, HEAD), F32)] * 3,
        compiler_params=_params(("parallel",)),
    )(proj, proj, hstate, dycat, conv_w, conv_b, w_a, b_a, w_x, b_x, lam)


def rope_tables(pos_col, inv_freq):
    def body(pos_ref, f_ref, c_ref, s1_ref, s2_ref):
        ang = pos_ref[...].astype(F32) * f_ref[...]
        lane = lax.broadcasted_iota(jnp.int32, ang.shape, 1)
        cos, sin = jnp.cos(ang), jnp.sin(ang)
        c_ref[...] = jnp.where(lane < QK_ROPE, cos, 0.0)
        s1_ref[...] = jnp.where(lane < QK_ROPE // 2, -sin, 0.0)
        s2_ref[...] = jnp.where((lane >= QK_ROPE // 2) & (lane < QK_ROPE), sin, 0.0)

    tab = jax.ShapeDtypeStruct((S, HEAD), F32)
    return pl.pallas_call(
        body, name="rope_tables", grid=(1,),
        in_specs=[pl.BlockSpec((S, 1), lambda i: (0, 0)), pl.BlockSpec((1, HEAD), lambda i: (0, 0))],
        out_specs=[pl.BlockSpec((S, HEAD), lambda i: (0, 0))] * 3, out_shape=[tab, tab, tab],
        compiler_params=_params(("arbitrary",)),
    )(pos_col, inv_freq)


def _rope(v, c, s1, s2):
    return v * c + pltpu.roll(v, HEAD - QK_ROPE // 2, 1) * s1 + pltpu.roll(v, QK_ROPE // 2, 1) * s2


def _unrope(d, c, s1, s2):
    return d * c + pltpu.roll(d * s1, QK_ROPE // 2, 1) + pltpu.roll(d * s2, HEAD - QK_ROPE // 2, 1)


def _rms(x, g):
    rstd = lax.rsqrt(jnp.mean(x * x, axis=-1, keepdims=True) + RMS_EPS)
    return x * rstd, rstd


def mla_prep(name, down, gq, gkv, tabs, *, tm=256):
    def body(dn_ref, gq_ref, gkv_ref, c_ref, s1_ref, s2_ref, cq_ref, ckv_ref, kp_ref):
        xq, _ = _rms(dn_ref[:, :Q_RANK], None)
        cq_ref[...] = (xq * gq_ref[...]).astype(cq_ref.dtype)
        xkv, _ = _rms(dn_ref[:, Q_RANK:Q_RANK + KV_RANK], None)
        ckv_ref[...] = (xkv * gkv_ref[...]).astype(ckv_ref.dtype)
        kp = _rope(dn_ref[:, Q_RANK + KV_RANK:], c_ref[...], s1_ref[...], s2_ref[...])
        kp_ref[...] = kp.astype(kp_ref.dtype)

    tab = pl.BlockSpec((tm, HEAD), lambda i: (i, 0))
    return pl.pallas_call(
        body, name=name, grid=(S // tm,),
        in_specs=[pl.BlockSpec((tm, ODD_IN_PAD), lambda i: (i, 0)), pl.BlockSpec((1, Q_RANK), lambda i: (0, 0)),
                  pl.BlockSpec((1, KV_RANK), lambda i: (0, 0)), tab, tab, tab],
        out_specs=[pl.BlockSpec((tm, Q_RANK), lambda i: (i, 0)), pl.BlockSpec((tm, KV_RANK), lambda i: (i, 0)), tab],
        out_shape=[jax.ShapeDtypeStruct((S, Q_RANK), BF16), jax.ShapeDtypeStruct((S, KV_RANK), BF16),
                   jax.ShapeDtypeStruct((S, HEAD), BF16)],
        compiler_params=_params(("parallel",)),
    )(down, gq, gkv, *tabs)


def mla_prep_bwd(name, down, dcq, dckv, dkp, gq, gkv, tabs, *, tm=256):
    def body(dn_ref, dcq_ref, dckv_ref, dkp_ref, gq_ref, gkv_ref, c_ref, s1_ref, s2_ref, dd_ref, dgq_ref, dgkv_ref):
        i = pl.program_id(0)

        def rms_bwd(x, dy, g, dg_ref):
            xh, rstd = _rms(x, None)
            dxh = dy * g
            dx = rstd * (dxh - xh * jnp.mean(dxh * xh, axis=-1, keepdims=True))
            pg = jnp.sum(dy * xh, axis=0, keepdims=True)

            @pl.when(i == 0)
            def _():
                dg_ref[...] = pg

            @pl.when(i > 0)
            def _():
                dg_ref[...] += pg

            return dx

        dxq = rms_bwd(dn_ref[:, :Q_RANK], dcq_ref[...], gq_ref[...], dgq_ref)
        dd_ref[:, :Q_RANK] = dxq.astype(dd_ref.dtype)
        dxkv = rms_bwd(dn_ref[:, Q_RANK:Q_RANK + KV_RANK], dckv_ref[...], gkv_ref[...], dgkv_ref)
        dd_ref[:, Q_RANK:Q_RANK + KV_RANK] = dxkv.astype(dd_ref.dtype)
        dd_ref[:, Q_RANK + KV_RANK:] = _unrope(dkp_ref[...], c_ref[...], s1_ref[...], s2_ref[...]).astype(dd_ref.dtype)

    tab = pl.BlockSpec((tm, HEAD), lambda i: (i, 0))
    vq = pl.BlockSpec((1, Q_RANK), lambda i: (0, 0))
    vkv = pl.BlockSpec((1, KV_RANK), lambda i: (0, 0))
    return pl.pallas_call(
        body, name=name, grid=(S // tm,),
        in_specs=[pl.BlockSpec((tm, ODD_IN_PAD), lambda i: (i, 0)), pl.BlockSpec((tm, Q_RANK), lambda i: (i, 0)),
                  pl.BlockSpec((tm, KV_RANK), lambda i: (i, 0)), tab, vq, vkv, tab, tab, tab],
        out_specs=[pl.BlockSpec((tm, ODD_IN_PAD), lambda i: (i, 0)), vq, vkv],
        out_shape=[jax.ShapeDtypeStruct((S, ODD_IN_PAD), BF16), jax.ShapeDtypeStruct((1, Q_RANK), F32),
                   jax.ShapeDtypeStruct((1, KV_RANK), F32)],
        compiler_params=_params(("arbitrary",)),
    )(down, dcq, dckv, dkp, gq, gkv, *tabs)


ATT_TQ = 256


def _attn_probs(q_ref, kv_ref, kp_ref, c_ref, s1_ref, s2_ref, i, nk):
    qn = q_ref[:, :HEAD].astype(BF16)
    qp = _rope(q_ref[:, HEAD:], c_ref[...], s1_ref[...], s2_ref[...]).astype(BF16)
    kn = kv_ref[:nk, :HEAD]
    sc = (_dot(qn, kn, _NT) + _dot(qp, kp_ref[:nk, :], _NT)) * ATT_SCALE
    q_chunk = (i * ATT_TQ + lax.broadcasted_iota(jnp.int32, sc.shape, 0)) // CHUNK
    k_chunk = lax.broadcasted_iota(jnp.int32, sc.shape, 1) // CHUNK
    sc = jnp.where(k_chunk <= q_chunk, sc, jnp.finfo(F32).min)
    e = jnp.exp(sc - jnp.max(sc, axis=-1, keepdims=True))
    p = e * (1.0 / jnp.sum(e, axis=-1, keepdims=True))
    return p, qn, qp, kn


def _for_each_prefix(i, fn):
    for k in range(S // ATT_TQ):
        pl.when(i == k)(functools.partial(fn, (k + 1) * ATT_TQ))


def _attn_specs():
    q = pl.BlockSpec((ATT_TQ, QHEAD_PAD), lambda h, i: (i, h))
    kv = pl.BlockSpec((S, QHEAD_PAD), lambda h, i: (0, h))
    kp = pl.BlockSpec((S, HEAD), lambda h, i: (0, 0))
    tab = pl.BlockSpec((ATT_TQ, HEAD), lambda h, i: (i, 0))
    o = pl.BlockSpec((ATT_TQ, HEAD), lambda h, i: (i, h))
    return q, kv, kp, tab, o


def attn_fwd(name, q, kv, kp, tabs):
    def body(q_ref, kv_ref, kp_ref, c_ref, s1_ref, s2_ref, o_ref):
        i = pl.program_id(1)

        def run(nk):
            p, _, _, _ = _attn_probs(q_ref, kv_ref, kp_ref, c_ref, s1_ref, s2_ref, i, nk)
            o_ref[...] = _dot(p, kv_ref[:nk, HEAD:], _NN).astype(o_ref.dtype)

        _for_each_prefix(i, run)

    qs, kvs, kps, tab, os = _attn_specs()
    return pl.pallas_call(
        body, name=name, grid=(MLA_HEADS, S // ATT_TQ), in_specs=[qs, kvs, kps, tab, tab, tab], out_specs=os,
        out_shape=jax.ShapeDtypeStruct((S, MLA_HEADS * HEAD), BF16),
        compiler_params=_params(("parallel", "parallel")),
    )(q, kv, kp, *tabs)


def attn_bwd(name, q, kv, kp, do, tabs):
    def body(q_ref, kv_ref, kp_ref, do_ref, c_ref, s1_ref, s2_ref, dq_ref, dkv_ref, dkp_ref):
        h, i = pl.program_id(0), pl.program_id(1)

        @pl.when(i == 0)
        def _():
            dkv_ref[...] = jnp.zeros_like(dkv_ref)

        @pl.when((i == 0) & (h == 0))
        def _():
            dkp_ref[...] = jnp.zeros_like(dkp_ref)

        def run(nk):
            p, qn, qp, kn = _attn_probs(q_ref, kv_ref, kp_ref, c_ref, s1_ref, s2_ref, i, nk)
            do_v = do_ref[...]
            dp = _dot(do_v, kv_ref[:nk, HEAD:], _NT)
            ds = (p * (dp - jnp.sum(p * dp, axis=-1, keepdims=True)) * ATT_SCALE).astype(BF16)
            dq_ref[:, :HEAD] = _dot(ds, kn, _NN).astype(dq_ref.dtype)
            dqp = _unrope(_dot(ds, kp_ref[:nk, :], _NN), c_ref[...], s1_ref[...], s2_ref[...])
            dq_ref[:, HEAD:] = dqp.astype(dq_ref.dtype)
            dkv_ref[:nk, :HEAD] += _dot(ds, qn, _TN)
            dkv_ref[:nk, HEAD:] += _dot(p, do_v, _TN)
            dkp_ref[:nk, :] += _dot(ds, qp, _TN)

        _for_each_prefix(i, run)

    qs, kvs, kps, tab, os = _attn_specs()
    return pl.pallas_call(
        body, name=name, grid=(MLA_HEADS, S // ATT_TQ), in_specs=[qs, kvs, kps, os, tab, tab, tab],
        out_specs=[qs, kvs, kps],
        out_shape=[jax.ShapeDtypeStruct((S, MLA_HEADS * QHEAD_PAD), BF16),
                   jax.ShapeDtypeStruct((S, MLA_HEADS * QHEAD_PAD), F32), jax.ShapeDtypeStruct((S, HEAD), F32)],
        compiler_params=_params(("arbitrary", "arbitrary")),
    )(q, kv, kp, do, *tabs)


def adamw(name, w, g, m, v):
    rows, cols = w.shape
    tr = rows
    for cand in (512, 256, 128, 64, 32, 16, 8):
        if rows % cand == 0 and cand * cols * 4 <= 2 * 1024 * 1024:
            tr = cand
            break

    def body(w_ref, g_ref, m_ref, v_ref, d_ref, nm_ref, nv_ref):
        g_v = g_ref[...]
        nm = ADAM_B1 * m_ref[...] + (1.0 - ADAM_B1) * g_v
        nv = ADAM_B2 * v_ref[...] + (1.0 - ADAM_B2) * (g_v * g_v)
        m_hat = nm / (1.0 - ADAM_B1 ** ADAM_STEP)
        v_hat = nv / (1.0 - ADAM_B2 ** ADAM_STEP)
        d_ref[...] = (-ADAM_LR) * (m_hat / (jnp.sqrt(v_hat) + ADAM_EPS) + ADAM_WD * w_ref[...])
        nm_ref[...] = nm
        nv_ref[...] = nv

    blk = pl.BlockSpec((tr, cols), lambda i: (i, 0))
    shape = jax.ShapeDtypeStruct((rows, cols), F32)
    return pl.pallas_call(
        body, name=name, grid=(rows // tr,), in_specs=[blk] * 4, out_specs=[blk] * 3, out_shape=[shape] * 3,
        compiler_params=_params(("parallel",)),
    )(w, g, m, v)


def _local_step(x, pos_col, target, sm, weights_of, emit_grads):
    inv_freq = ROPE_THETA ** (-jnp.arange(0, QK_ROPE, 2, dtype=F32) / QK_ROPE)
    inv_freq = jnp.concatenate([inv_freq, inv_freq, jnp.zeros((HEAD - QK_ROPE,), F32)])[None, :]
    tabs = rope_tables(pos_col, inv_freq)
    saved, wts = [], {}
    for layer in range(DEPTH):
        j = layer // 2
        n = "l%d_" % layer
        sv = {"x": x}
        wm = wts["mix%d" % layer] = weights_of("mix%d" % layer, x)
        if layer == 0:
            sm = dict(sm, conv_w=wm["conv_w"], gq=wm["gq"], gkv=wm["gkv"])
        if layer % 2 == 0:
            proj = mm_cols(n + "proj", x, wm["w_in"], [F32])[0]
            y_pool = pool_fwd(n + "pool", proj, sm["pool_w"][j], sm["pool_scale"][j][None])
            y_lru, hstate = lru_fwd(n + "lru", proj, sm["conv_w"][j], sm["conv_b"][j][None], sm["w_a"][j],
                                    sm["b_a"][j][None], sm["w_x"][j], sm["b_x"][j][None], sm["lam"][j][None])
            ycat = jnp.concatenate([y_pool, y_lru], axis=1)
            mix = mm_rows(n + "mixout", ycat, wm["w_out"], F32)
            sv.update(proj=proj, hstate=hstate, ycat=ycat)
        else:
            down = mm_rows(n + "down", x, wm["w_down"], F32)
            cq, ckv, kp = mla_prep(n + "prep", down, sm["gq"][j][None], sm["gkv"][j][None], tabs)
            q = mm_cols(n + "q", cq, wm["w_qb"], [F32])[0]
            kv = mm_cols(n + "kv", ckv, wm["w_kvb"], [BF16])[0]
            o = attn_fwd(n + "attn", q, kv, kp, tabs)
            mix = mm_rows(n + "attnout", o, wm["w_o"], F32)
            sv.update(down=down, cq=cq, ckv=ckv, kp=kp, q=q, kv=kv, o=o)
        x1, xhat1, rstd1 = ln_fwd(n + "ln_mix", x, mix, sm["ln_mix_g"][layer][None], sm["ln_mix_b"][layer][None])
        wf = wts["mlp%d" % layer] = weights_of("mlp%d" % layer, x1)
        act, relu = mm_cols(n + "mlp1", x1, wf["w1"], [BF16, BF16],
                            epilogue=lambda acc: (jnp.square(jnp.maximum(acc, 0.0)), jnp.maximum(acc, 0.0)))
        mlp = mm_rows(n + "mlp2", act, wf["w2"], F32)
        x2, xhat2, rstd2 = ln_fwd(n + "ln_ffn", x1, mlp, sm["ln_ffn_g"][layer][None], sm["ln_ffn_b"][layer][None])
        sv.update(xhat1=xhat1, rstd1=rstd1, x1=x1, act=act, relu=relu, xhat2=xhat2, rstd2=rstd2)
        saved.append(sv)
        x = x2

    dx, loss_tile = loss_head(x, target)
    gs = {k: [None] * (DEPTH if k.startswith("ln_") else DEPTH // 2) for k in sm}
    for layer in reversed(range(DEPTH)):
        j = layer // 2
        n = "l%d_" % layer
        sv = saved[layer]
        wm, wf = wts["mix%d" % layer], wts["mlp%d" % layer]
        dz, gs["ln_ffn_g"][layer], gs["ln_ffn_b"][layer] = ln_bwd(
            n + "ln_ffn_b", dx, sv["xhat2"], sv["rstd2"], sm["ln_ffn_g"][layer][None])
        dh = mm_t_rows(n + "mlp2_dx", dz, wf["w2"], BF16, gate=sv["relu"])
        g_w2 = mm_grad(n + "mlp2_dw", sv["act"], dz, a_sharded=True)
        g_w1 = mm_grad(n + "mlp1_dw", sv["x1"], dh, a_sharded=False)
        sent = emit_grads("mlp%d" % layer, {"w1": g_w1, "w2": g_w2})
        dx = mm_t_cols(n + "mlp1_dx", dh, wf["w1"], F32, resid=dz, after=sent)
        dz, gs["ln_mix_g"][layer], gs["ln_mix_b"][layer] = ln_bwd(
            n + "ln_mix_b", dx, sv["xhat1"], sv["rstd1"], sm["ln_mix_g"][layer][None])
        if layer % 2 == 0:
            dycat = mm_t_rows(n + "mixout_dx", dz, wm["w_out"], F32)
            g_out = mm_grad(n + "mixout_dw", sv["ycat"], dz, a_sharded=True)
            du_pool, gs["pool_w"][j], gs["pool_scale"][j] = pool_bwd(
                n + "pool_b", sv["proj"], dycat, sm["pool_w"][j], sm["pool_scale"][j][None])
            (du_lru, du_gate, gs["w_a"][j], gs["w_x"][j], gs["b_a"][j], gs["b_x"][j], gs["lam"][j], gs["conv_w"][j],
             gs["conv_b"][j]) = lru_bwd(n + "lru_b", sv["proj"], sv["hstate"], dycat, sm["conv_w"][j],
                                        sm["conv_b"][j][None], sm["w_a"][j], sm["b_a"][j][None], sm["w_x"][j],
                                        sm["b_x"][j][None], sm["lam"][j][None])
            dproj = jnp.concatenate([du_pool, du_lru, du_gate], axis=1)
            g_in = mm_grad(n + "proj_dw", sv["x"], dproj, a_sharded=False)
            sent = emit_grads("mix%d" % layer, {"w_in": g_in, "w_out": g_out})
            dx = mm_t_cols(n + "proj_dx", dproj, wm["w_in"], F32, resid=dz, after=sent)
        else:
            do = mm_t_rows(n + "attnout_dx", dz, wm["w_o"], BF16)
            g_o = mm_grad(n + "attnout_dw", sv["o"], dz, a_sharded=True)
            dq, dkv, dkp = attn_bwd(n + "attn_b", sv["q"], sv["kv"], sv["kp"], do, tabs)
            g_qb = mm_grad(n + "q_dw", sv["cq"], dq, a_sharded=False)
            dcq = mm_t_cols(n + "q_dx", dq, wm["w_qb"], F32)
            g_kvb = mm_grad(n + "kv_dw", sv["ckv"], dkv, a_sharded=False)
            dckv = mm_t_cols(n + "kv_dx", dkv, wm["w_kvb"], F32)
            ddown, gs["gq"][j], gs["gkv"][j] = mla_prep_bwd(
                n + "prep_b", sv["down"], dcq, dckv, dkp, sm["gq"][j][None], sm["gkv"][j][None], tabs)
            g_down = mm_grad(n + "down_dw", sv["x"], ddown, a_sharded=True)
            sent = emit_grads("mix%d" % layer, {"w_down": g_down, "w_qb": g_qb, "w_kvb": g_kvb, "w_o": g_o})
            dx = mm_t_rows(n + "down_dx", ddown, wm["w_down"], F32, resid=dz, after=sent)
    gs = {k: jnp.stack([a.reshape(sm[k].shape[1:]) for a in v]) for k, v in gs.items()}
    return loss_tile[0, 0], dx, gs


def _place():
    x, y, c = lax.axis_index("x"), lax.axis_index("y"), lax.axis_index("c")
    chips = [(1 - x, y), (x, 1 - y), (1 - x, 1 - y)]
    return x, y, c, chips


def _hbm_call(body, name, args, out_shape, scratch, aliases=None):
    return pl.pallas_call(
        body, name=name, in_specs=[pl.BlockSpec(memory_space=pl.ANY)] * len(args),
        out_specs=[pl.BlockSpec(memory_space=pl.ANY)] * len(out_shape), out_shape=out_shape,
        scratch_shapes=scratch, input_output_aliases=aliases or {},
        compiler_params=pltpu.CompilerParams(has_side_effects=True),
    )(*args)


HBM_SPEC = pl.BlockSpec(memory_space=pltpu.HBM)
SEM_SPEC = pl.BlockSpec(memory_space=pltpu.SEMAPHORE)
EFFECT = pltpu.SideEffectType.DATAFLOW_SIDE_EFFECTING


def _remote(src, dst, send_sem, recv_sem, device):
    return pltpu.make_async_remote_copy(src_ref=src, dst_ref=dst, send_sem=send_sem, recv_sem=recv_sem,
                                        device_id=device, device_id_type=MESH)


def place_own(name, srcs, place, steps, reducing):
    n = len(srcs)
    in_specs, out_specs, out_shape = [], [], []
    for s in srcs:
        if reducing:
            nd, rows, cols = s.shape
            tr = rows // 2 // steps
            pick = (lambda i, p: (p[0], p[1] * steps + i, 0)) if nd == N_CHIPS else (lambda i, p: (0, p[1] * steps + i, 0))
            in_specs.append(pl.BlockSpec((None, tr, cols), pick))
            out_specs.append(pl.BlockSpec((None, tr, cols), lambda i, p: (p[2], i, 0)))
            out_shape.append(jax.ShapeDtypeStruct((N_DEV, rows // 2, cols), s.dtype))
        else:
            rows, cols = s.shape
            tr = rows // steps
            in_specs.append(pl.BlockSpec((tr, cols), lambda i, p: (i, 0)))
            out_specs.append(pl.BlockSpec((None, tr, cols), lambda i, p: (p[0], i, 0)))
            out_shape.append(jax.ShapeDtypeStruct((N_CHIPS, rows, cols), s.dtype))

    def body(p_ref, *refs):
        for i_ref, o_ref in zip(refs[:n], refs[n:]):
            o_ref[...] = i_ref[...]

    return pl.pallas_call(
        body, name=name, out_shape=out_shape,
        grid_spec=pltpu.PrefetchScalarGridSpec(num_scalar_prefetch=1, grid=(steps,), in_specs=in_specs,
                                               out_specs=out_specs),
        compiler_params=_params(("parallel",)),
    )(place, *srcs)


def split_start(name, srcs, lands, n_sems, plan, token_in=None):
    n = len(srcs)
    srcs = [pltpu.with_memory_space_constraint(s, pltpu.HBM) for s in srcs]
    lands = [pltpu.with_memory_space_constraint(a, pltpu.HBM) for a in lands]
    extra = [] if token_in is None else [token_in]

    def body(*refs):
        src_refs, land_refs = refs[:n], refs[n:2 * n]
        outs = refs[2 * n + len(extra):]
        send_sems, recv_sems, token = outs[0], outs[1], outs[2 + 2 * n]
        sends, _ = plan(src_refs, land_refs, send_sems, recv_sems)
        for cp in sends:
            cp.start()
        token[...] = jnp.zeros_like(token)

    out_shape = (pltpu.SemaphoreType.DMA((n * n_sems,)), pltpu.SemaphoreType.DMA((n * n_sems,)),
                 *[pltpu.HBM(a.shape, a.dtype) for a in srcs + lands], jax.ShapeDtypeStruct((8, 128), F32))
    res = pl.pallas_call(
        body, name=name, out_shape=out_shape,
        in_specs=[HBM_SPEC] * (2 * n) + [pl.BlockSpec(memory_space=pl.ANY)] * len(extra),
        out_specs=(SEM_SPEC, SEM_SPEC, *[HBM_SPEC] * (2 * n), pl.BlockSpec(memory_space=pltpu.VMEM)),
        input_output_aliases={i: 2 + i for i in range(2 * n)},
        compiler_params=pltpu.CompilerParams(has_side_effects=EFFECT),
    )(*srcs, *lands, *extra)
    return dict(send=res[0], recv=res[1], srcs=list(res[2:2 + n]), lands=list(res[2 + n:2 + 2 * n]), plan=plan), res[-1]


def split_wait(name, started, after):
    sizes = [len(st["srcs"]) for st in started]
    n_buf = 2 * sum(sizes)

    def body(*refs):
        bufs, sems = refs[:n_buf], refs[n_buf:n_buf + 2 * len(started)]
        off = 0
        for g, (st, n) in enumerate(zip(started, sizes)):
            src_refs, land_refs = bufs[off:off + n], bufs[off + n:off + 2 * n]
            off += 2 * n
            sends, expects = st["plan"](src_refs, land_refs, sems[2 * g], sems[2 * g + 1])
            for cp in sends:
                cp.wait_send()
            for cp in expects:
                cp.wait_recv()

    bufs = [a for st in started for a in st["srcs"] + st["lands"]]
    sems = [s for st in started for s in (st["send"], st["recv"])]
    res = pl.pallas_call(
        body, name=name, out_shape=tuple(pltpu.HBM(a.shape, a.dtype) for a in bufs),
        in_specs=[HBM_SPEC] * n_buf + [SEM_SPEC] * len(sems) + [pl.BlockSpec(memory_space=pl.ANY)],
        out_specs=tuple([HBM_SPEC] * n_buf), input_output_aliases={i: i for i in range(n_buf)},
        compiler_params=pltpu.CompilerParams(has_side_effects=EFFECT),
    )(*bufs, *sems, after)
    out, off = [], 0
    for n in sizes:
        out.append(list(res[off + n:off + 2 * n]))
        off += 2 * n
    return out


def gather_plan(src_refs, land_refs, send_sems, recv_sems):
    x, y, c, chips = _place()
    me = 2 * x + y
    sends, expects = [], []
    for k, (s, d) in enumerate(zip(src_refs, land_refs)):
        for j, (px, py) in enumerate(chips):
            sem = 3 * k + j
            sends.append(_remote(s, d.at[me], send_sems.at[sem], recv_sems.at[sem], (px, py, c)))
            expects.append(_remote(s, d.at[2 * px + py], send_sems.at[sem], recv_sems.at[sem], (px, py, c)))
    return sends, expects


def _reduce_part(s, chip_idx, h):
    hr = s.shape[1] // 2
    return s.at[chip_idx if s.shape[0] == N_CHIPS else 0, pl.ds(h * hr, hr)]


def reduce_plan(src_refs, land_refs, send_sems, recv_sems):
    x, y, c, chips = _place()
    me_chip, me_dev = 2 * x + y, 4 * x + 2 * y + c
    sends, expects = [], []
    for k, (s, d) in enumerate(zip(src_refs, land_refs)):
        part = functools.partial(_reduce_part, s)
        for j, (px, py) in enumerate(chips):
            for h in range(2):
                sends.append(_remote(part(2 * px + py, h), d.at[me_dev], send_sems.at[7 * k + 2 * j + h],
                                     recv_sems.at[7 * k + 2 * j + c], (px, py, h)))
                expects.append(_remote(part(me_chip, c), d.at[4 * px + 2 * py + h], send_sems.at[7 * k + 2 * j + h],
                                       recv_sems.at[7 * k + 2 * j + h], (px, py, h)))
        sends.append(_remote(part(me_chip, 1 - c), d.at[me_dev], send_sems.at[7 * k + 6], recv_sems.at[7 * k + 6],
                             (x, y, 1 - c)))
        expects.append(_remote(part(me_chip, c), d.at[me_dev + 1 - 2 * c], send_sems.at[7 * k + 6],
                               recv_sems.at[7 * k + 6], (x, y, 1 - c)))
    return sends, expects


def sibling_swap_halves(fulls):
    n = len(fulls)

    def body(*refs):
        outs = refs[n:2 * n]
        send_sems, recv_sems = refs[2 * n:]
        x, y, c, _ = _place()
        copies = []
        for k in range(n):
            nl, rows = fulls[k].shape[:2]
            hr = rows // 2
            mine = outs[k].at[pl.ds(0, nl), pl.ds(c * hr, hr)]
            theirs = outs[k].at[pl.ds(0, nl), pl.ds((1 - c) * hr, hr)]
            copies.append((_remote(mine, mine, send_sems.at[k], recv_sems.at[k], (x, y, 1 - c)),
                           _remote(mine, theirs, send_sems.at[k], recv_sems.at[k], (x, y, 1 - c))))
        for send, _ in copies:
            send.start()
        for send, recv in copies:
            send.wait_send()
            recv.wait_recv()

    out_shape = [jax.ShapeDtypeStruct(f.shape, f.dtype) for f in fulls]
    scratch = [pltpu.SemaphoreType.DMA((n,)), pltpu.SemaphoreType.DMA((n,))]
    return _hbm_call(body, "sibling_swap_halves", fulls, out_shape, scratch, aliases={k: k for k in range(n)})


def _tile_rows(rows, cols, budget_bytes):
    best = None
    for t in range(16, rows + 1, 16):
        if rows % t == 0 and t * cols * 4 <= budget_bytes:
            best = t
    assert best is not None, (rows, cols)
    return best


N_DEV = 8


def sum_devices(name, landed, layer, n_layers, prev, c_arr):
    _, hr, cols = landed.shape
    tr = _tile_rows(hr, cols, 512 * 1024)
    n_blk = hr // tr

    def body(c_ref, r_ref, *rest):
        o_ref = rest[-1]
        acc = r_ref[0].astype(F32)
        for d in range(1, N_DEV):
            acc = acc + r_ref[d].astype(F32)
        o_ref[...] = acc

    in_specs = [pl.BlockSpec((N_DEV, tr, cols), lambda r, c_ref: (0, r, 0))]
    args = [c_arr, landed]
    aliases = {}
    if prev is not None:
        in_specs.append(pl.BlockSpec(memory_space=pl.ANY))
        args.append(prev)
        aliases = {2: 0}
    return pl.pallas_call(
        body, name=name, out_shape=jax.ShapeDtypeStruct((n_layers, 2 * hr, cols), F32),
        grid_spec=pltpu.PrefetchScalarGridSpec(
            num_scalar_prefetch=1, grid=(n_blk,), in_specs=in_specs,
            out_specs=pl.BlockSpec((None, tr, cols), lambda r, c_ref: (layer, c_ref[0] * n_blk + r, 0))),
        input_output_aliases=aliases, compiler_params=_params(("parallel",)),
    )(*args)


def _pack(arrs, rows_multiple):
    flat = []
    for a in arrs:
        v = a.reshape(-1).astype(F32)
        flat.append(jnp.pad(v, (0, (-v.shape[0]) % HEAD)))
    v = jnp.concatenate(flat)
    v = jnp.pad(v, (0, (-v.shape[0]) % (HEAD * rows_multiple)))
    return v.reshape(-1, HEAD)


def _unpack(packed, shapes):
    flat = packed.reshape(-1)
    out, off = [], 0
    for shp in shapes:
        size = int(np.prod(shp))
        out.append(flat[off:off + size].reshape(shp))
        off += size + (-size) % HEAD
    return out


BIG = ["even_w_in", "even_w_out", "mla_w_down", "mla_w_qb", "mla_w_kvb", "mla_w_o", "mlp_w1", "mlp_w2"]
BIG_KEY = {"even_w_in": "w_in", "even_w_out": "w_out", "mla_w_down": "w_down", "mla_w_qb": "w_qb",
           "mla_w_kvb": "w_kvb", "mla_w_o": "w_o", "mlp_w1": "w1", "mlp_w2": "w2"}
SMALL_KEY = {"ln_mix_g": "ln_mix_g", "ln_mix_b": "ln_mix_b", "ln_ffn_g": "ln_ffn_g", "ln_ffn_b": "ln_ffn_b",
             "pool_w": "pool_w", "pool_scale": "pool_scale", "lru_conv_w": "conv_w", "lru_conv_b": "conv_b",
             "lru_w_a": "w_a", "lru_b_a": "b_a", "lru_w_x": "w_x", "lru_b_x": "b_x", "lru_lambda": "lam",
             "mla_q_norm_g": "gq", "mla_kv_norm_g": "gkv"}
SMALL = list(SMALL_KEY)
SMALL_SHARDED = ["lru_conv_w", "mla_q_norm_g", "mla_kv_norm_g"]
WEIGHTS = ["ln_mix_g", "ln_mix_b", "ln_ffn_g", "ln_ffn_b", "even_w_in", "pool_w", "pool_scale", "lru_conv_w",
           "lru_conv_b", "lru_w_a", "lru_b_a", "lru_w_x", "lru_b_x", "lru_lambda", "even_w_out", "mla_w_down",
           "mla_q_norm_g", "mla_kv_norm_g", "mla_w_qb", "mla_w_kvb", "mla_w_o", "mlp_w1", "mlp_w2"]


GROUPS = ["mix0", "mlp0", "mix1", "mlp1", "mix2", "mlp2", "mix3", "mlp3"]


def _group_keys(group):
    layer = int(group[3:])
    if group.startswith("mlp"):
        return [("mlp_w1", "w1", layer), ("mlp_w2", "w2", layer)]
    if layer % 2 == 0:
        return [("even_w_in", "w_in", layer // 2), ("even_w_out", "w_out", layer // 2)]
    return [("mla_w_down", "w_down", layer // 2), ("mla_w_qb", "w_qb", layer // 2),
            ("mla_w_kvb", "w_kvb", layer // 2), ("mla_w_o", "w_o", layer // 2)]


def _pad_q_heads(w):
    lead = w.shape[:-1]
    w = w.reshape(lead + (2, QK_NOPE + QK_ROPE))
    w = jnp.pad(w, ((0, 0),) * len(lead) + ((0, 0), (0, QHEAD_PAD - QK_NOPE - QK_ROPE)))
    return w.reshape(lead + (2 * QHEAD_PAD,))


def _unpad_q_heads(g):
    lead = g.shape[:-1]
    return g.reshape(lead + (2, QHEAD_PAD))[..., :QK_NOPE + QK_ROPE].reshape(lead + (2 * (QK_NOPE + QK_ROPE),))


def _step(x, positions, loss_target, w, m, v):
    cx, cy, cc = lax.axis_index("x"), lax.axis_index("y"), lax.axis_index("c")
    chip = 2 * cx + cy
    c_arr = jnp.reshape(cc, (1,)).astype(jnp.int32)
    place = jnp.stack([chip, cc, 2 * chip + cc]).astype(jnp.int32)

    prepared = dict(w)
    prepared["mla_w_down"] = jnp.pad(w["mla_w_down"], ((0, 0), (0, 0), (0, ODD_IN_PAD - ODD_IN)))
    prepared["mla_w_qb"] = _pad_q_heads(w["mla_w_qb"])
    small_shard_shapes = [w[k].shape for k in SMALL_SHARDED]
    sources, zones = {}, {}
    for g in GROUPS:
        srcs = [prepared[name][idx].astype(BF16) for name, _, idx in _group_keys(g)]
        if g == GROUPS[0]:
            srcs.append(_pack([w[k] for k in SMALL_SHARDED], 32))
        sources[g] = srcs
        zones[g] = place_own("own_" + g, srcs, place, 4, False)
    gathering, token = {}, None
    for g in GROUPS:
        gathering[g], token = split_start("gather_" + g, sources[g], zones[g], 3, gather_plan, token)
    all_started = token

    def weights_of(g, after):
        lands = split_wait("gathered_" + g, [gathering[g]], all_started if g == GROUPS[0] else after)[0]
        out = {key: land for (_, key, _), land in zip(_group_keys(g), lands)}
        if g == GROUPS[0]:
            per_chip = [_unpack(lands[-1][s], small_shard_shapes) for s in range(N_CHIPS)]
            for i, key in enumerate(("conv_w", "gq", "gkv")):
                out[key] = jnp.concatenate([p[i] for p in per_chip], axis=-1)
        return out

    reducing = []

    def emit_grads(g, grads):
        srcs = [grads[key] for _, key, _ in _group_keys(g)]
        lands = place_own("mine_" + g, srcs, place, 4, True)
        started, token = split_start("reduce_" + g, srcs, lands, 7, reduce_plan)
        reducing.append((g, started))
        return token

    sm = {SMALL_KEY[k]: w[k] for k in SMALL if k not in SMALL_SHARDED}
    loss, grad_x, gs = _local_step(x[0], positions.reshape(S, 1), loss_target[0], sm, weights_of, emit_grads)
    loss = lax.psum(loss, ("x", "y", "c"))

    small_shapes = [gs[SMALL_KEY[k]].shape for k in SMALL]
    gs_pack = _pack([gs[SMALL_KEY[k]] for k in SMALL], 32)[None]
    lands = place_own("mine_small", [gs_pack], place, gs_pack.shape[1] // 2 // _tile_rows(gs_pack.shape[1] // 2, HEAD, 512 * 1024), True)
    reducing.append(("small", split_start("reduce_small", [gs_pack], lands, 7, reduce_plan)[0]))

    landed = split_wait("reduced", [st for _, st in reducing], grad_x)
    stacks = {}
    for (g, _), lands in zip(reducing, landed):
        if g == "small":
            stacks["small"] = sum_devices("sum_small", lands[0], 0, 1, None, c_arr)
            continue
        for (name, key, idx), land in zip(_group_keys(g), lands):
            stacks[name] = sum_devices("sum_%s%d" % (key, idx), land, idx, w[name].shape[0], stacks.get(name), c_arr)
    reduced = sibling_swap_halves([stacks[k] for k in BIG] + [stacks["small"]])
    g_big = dict(zip(BIG, reduced[:-1]))
    g_big["mla_w_down"] = g_big["mla_w_down"][..., :ODD_IN]
    g_big["mla_w_qb"] = _unpad_q_heads(g_big["mla_w_qb"])
    g_small = dict(zip(SMALL, _unpack(reduced[-1], small_shapes)))
    for k in SMALL_SHARDED:
        width = w[k].shape[-1]
        g_small[k] = lax.dynamic_slice_in_dim(g_small[k], chip * width, width, axis=-1)
    grad = {**g_big, **g_small}

    delta, new_m, new_v = {}, {}, {}
    for k in BIG:
        shp = w[k].shape
        view = lambda a: a.reshape(-1, shp[-1])
        d, nm, nv = adamw("adamw_" + BIG_KEY[k], view(w[k]), view(grad[k]), view(m[k]), view(v[k]))
        delta[k], new_m[k], new_v[k] = d.reshape(shp), nm.reshape(shp), nv.reshape(shp)
    shapes = [w[k].shape for k in SMALL]
    d, nm, nv = adamw("adamw_small", *[_pack([t[k] for k in SMALL], 512) for t in (w, grad, m, v)])
    for k, dk, mk, vk in zip(SMALL, _unpack(d, shapes), _unpack(nm, shapes), _unpack(nv, shapes)):
        delta[k], new_m[k], new_v[k] = dk, mk, vk
    return (loss, grad_x[None], *[grad[k] for k in WEIGHTS], *[delta[k] for k in WEIGHTS],
            *[new_m[k] for k in WEIGHTS], *[new_v[k] for k in WEIGHTS])


def kernel(x, positions, ln_mix_g, ln_mix_b, ln_ffn_g, ln_ffn_b, even_w_in, pool_w, pool_scale, lru_conv_w, lru_conv_b, lru_w_a, lru_b_a, lru_w_x, lru_b_x, lru_lambda, even_w_out, mla_w_down, mla_q_norm_g, mla_kv_norm_g, mla_w_qb, mla_w_kvb, mla_w_o, mlp_w1, mlp_w2, loss_target, m_ln_mix_g, m_ln_mix_b, m_ln_ffn_g, m_ln_ffn_b, m_even_w_in, m_pool_w, m_pool_scale, m_lru_conv_w, m_lru_conv_b, m_lru_w_a, m_lru_b_a, m_lru_w_x, m_lru_b_x, m_lru_lambda, m_even_w_out, m_mla_w_down, m_mla_q_norm_g, m_mla_kv_norm_g, m_mla_w_qb, m_mla_w_kvb, m_mla_w_o, m_mlp_w1, m_mlp_w2, v_ln_mix_g, v_ln_mix_b, v_ln_ffn_g, v_ln_ffn_b, v_even_w_in, v_pool_w, v_pool_scale, v_lru_conv_w, v_lru_conv_b, v_lru_w_a, v_lru_b_a, v_lru_w_x, v_lru_b_x, v_lru_lambda, v_even_w_out, v_mla_w_down, v_mla_q_norm_g, v_mla_kv_norm_g, v_mla_w_qb, v_mla_w_kvb, v_mla_w_o, v_mlp_w1, v_mlp_w2):
    args = locals()
    w = {k: args[k] for k in WEIGHTS}
    m = {k: args["m_" + k] for k in WEIGHTS}
    v = {k: args["v_" + k] for k in WEIGHTS}
    return _step(x, positions, loss_target, w, m, v)
```

```python
import functools
import math

import jax
import jax.numpy as jnp
import numpy as np
from jax import lax
from jax.experimental import pallas as pl
from jax.experimental.pallas import tpu as pltpu

F32 = jnp.float32
BF16 = jnp.bfloat16

S = 2048
D = 1024
DEPTH = 4
N_CHIPS = 4
POOL_WINDOWS = (2, 4, 8, 16)
POOL_W = 512
LRU_W = 1024
LRU_HEADS = 8
HEAD = 128
EVEN_IN = 2560
EVEN_MIX = 1536
MLA_HEADS = 8
QK_NOPE = 128
QK_ROPE = 64
Q_RANK = 384
KV_RANK = 256
ODD_IN = 704
ODD_IN_PAD = 768
QHEAD_PAD = 256
D_FF = 4096
CHUNK = 64
ALPHA = (2 * DEPTH) ** 0.25
LN_EPS = 1e-5
RMS_EPS = 1e-6
ATT_SCALE = (QK_NOPE + QK_ROPE) ** -0.5
ROPE_THETA = 10000.0
LRU_C = 8.0
ADAM_LR = 0.001
ADAM_B1 = 0.9
ADAM_B2 = 0.999
ADAM_EPS = 1e-08
ADAM_WD = 0.01
ADAM_STEP = 10

VMEM_LIMIT = 56 * 1024 * 1024
MESH = pl.DeviceIdType.MESH

_NN = (((1,), (0,)), ((), ()))
_NT = (((1,), (1,)), ((), ()))
_TN = (((0,), (0,)), ((), ()))


def _params(sem=None, **kw):
    return pltpu.CompilerParams(dimension_semantics=sem, vmem_limit_bytes=VMEM_LIMIT, **kw)


def _dot(a, b, dims):
    return lax.dot_general(a.astype(BF16), b.astype(BF16), dims, preferred_element_type=F32)


def _mm(name, a, w4, *, dims, a_block, a_map, w_map, grid, sem, outs, out_blocks, out_maps,
        red_axis=None, n_red=1, extras=(), extra_blocks=(), extra_maps=(), epilogue=None, after=None):
    n_extra = len(extras)
    n_out = len(outs)
    n_after = 0 if after is None else 1
    w_block = (None,) + tuple(w4.shape[1:])

    def body(a_ref, w_ref, *rest):
        rest = rest[n_after:]
        extra_refs = rest[:n_extra]
        out_refs = rest[n_extra:n_extra + n_out]
        p = _dot(a_ref[...], w_ref[...], dims)

        def finish(acc):
            vals = epilogue(acc, *[r[...] for r in extra_refs]) if epilogue else (acc,)
            for r, v in zip(out_refs, vals):
                r[...] = v.astype(r.dtype)

        if red_axis is None:
            finish(p)
        else:
            acc_ref = rest[-1]
            k = pl.program_id(red_axis)

            @pl.when(k == 0)
            def _():
                acc_ref[...] = p

            @pl.when(k > 0)
            def _():
                acc_ref[...] += p

            @pl.when(k == n_red - 1)
            def _():
                finish(acc_ref[...])

    scratch = [] if red_axis is None else [pltpu.VMEM(out_blocks[0], F32)]
    return pl.pallas_call(
        body, name=name, grid=grid,
        in_specs=[pl.BlockSpec(a_block, a_map), pl.BlockSpec(w_block, w_map)]
        + [pl.BlockSpec(memory_space=pl.ANY)] * n_after + [pl.BlockSpec(b, m) for b, m in zip(extra_blocks, extra_maps)],
        out_specs=[pl.BlockSpec(b, m) for b, m in zip(out_blocks, out_maps)],
        out_shape=outs, scratch_shapes=scratch, compiler_params=_params(sem),
    )(a, w4, *([after] * n_after), *extras)


def _w_map_outer(s, i):
    return (s, 0, 0)


def _w_map_inner(i, s):
    return (s, 0, 0)


def mm_cols(name, a, w4, out_dtypes, *, tm=512, epilogue=None):
    m, k = a.shape
    nb = w4.shape[2]
    outs = [jax.ShapeDtypeStruct((m, N_CHIPS * nb), dt) for dt in out_dtypes]
    return _mm(name, a, w4, dims=_NN, a_block=(tm, k), a_map=lambda s, i: (i, 0), w_map=_w_map_outer,
               grid=(N_CHIPS, m // tm), sem=("parallel", "parallel"), outs=outs,
               out_blocks=[(tm, nb)] * len(outs), out_maps=[lambda s, i: (i, s)] * len(outs), epilogue=epilogue)


def mm_rows(name, a, w4, out_dtype, *, tm=512):
    m = a.shape[0]
    kb, n = w4.shape[1:]
    outs = [jax.ShapeDtypeStruct((m, n), out_dtype)]
    return _mm(name, a, w4, dims=_NN, a_block=(tm, kb), a_map=lambda i, s: (i, s), w_map=_w_map_inner,
               grid=(m // tm, N_CHIPS), sem=("parallel", "arbitrary"), outs=outs, out_blocks=[(tm, n)],
               out_maps=[lambda i, s: (i, 0)], red_axis=1, n_red=N_CHIPS)[0]


def mm_rows_ln(name, a, w4, x, g, b, *, tm=512):
    m = a.shape[0]
    kb, n = w4.shape[1:]
    outs = [jax.ShapeDtypeStruct((m, n), F32), jax.ShapeDtypeStruct((m, n), F32), jax.ShapeDtypeStruct((m, 1), F32)]
    return _mm(name, a, w4, dims=_NN, a_block=(tm, kb), a_map=lambda i, s: (i, s), w_map=_w_map_inner,
               grid=(m // tm, N_CHIPS), sem=("parallel", "arbitrary"), outs=outs, out_blocks=[(tm, n), (tm, n), (tm, 1)],
               out_maps=[lambda i, s: (i, 0)] * 3, red_axis=1, n_red=N_CHIPS, extras=(x, g, b),
               extra_blocks=[(tm, n), (1, n), (1, n)],
               extra_maps=[lambda i, s: (i, 0), lambda i, s: (0, 0), lambda i, s: (0, 0)],
               epilogue=lambda acc, xv, gv, bv: _layer_norm(ALPHA * xv + acc, gv, bv))


def mm_t_cols(name, g, w4, out_dtype, *, tm=512, resid=None, after=None):
    m = g.shape[0]
    k, nb = w4.shape[1:]
    outs = [jax.ShapeDtypeStruct((m, k), out_dtype)]
    extras = () if resid is None else (resid,)
    epi = None if resid is None else (lambda acc, r: (acc + ALPHA * r,))
    return _mm(name, g, w4, dims=_NT, a_block=(tm, nb), a_map=lambda i, s: (i, s), w_map=_w_map_inner,
               grid=(m // tm, N_CHIPS), sem=("parallel", "arbitrary"), outs=outs, out_blocks=[(tm, k)],
               out_maps=[lambda i, s: (i, 0)], red_axis=1, n_red=N_CHIPS, extras=extras,
               extra_blocks=[(tm, k)] * len(extras), extra_maps=[lambda i, s: (i, 0)] * len(extras), epilogue=epi,
               after=after)[0]


def mm_t_rows(name, g, w4, out_dtype, *, tm=512, resid=None, gate=None, after=None):
    m, n = g.shape
    kb = w4.shape[1]
    outs = [jax.ShapeDtypeStruct((m, N_CHIPS * kb), out_dtype)]
    extras, epi = (), None
    if resid is not None:
        extras, epi = (resid,), (lambda acc, r: (acc + ALPHA * r,))
    if gate is not None:
        extras, epi = (gate,), (lambda acc, r: (acc * (2.0 * r.astype(F32)),))
    return _mm(name, g, w4, dims=_NT, a_block=(tm, n), a_map=lambda s, i: (i, 0), w_map=_w_map_outer,
               grid=(N_CHIPS, m // tm), sem=("parallel", "parallel"), outs=outs, out_blocks=[(tm, kb)],
               out_maps=[lambda s, i: (i, s)], extras=extras, extra_blocks=[(tm, kb)] * len(extras),
               extra_maps=[lambda s, i: (i, s)] * len(extras), epilogue=epi, after=after)[0]


def mm_grad(name, a, g, *, a_sharded, tm=512):
    m = a.shape[0]
    ka = a.shape[1] // N_CHIPS if a_sharded else a.shape[1]
    ng = g.shape[1] if a_sharded else g.shape[1] // N_CHIPS
    n_red = m // tm
    a_map = (lambda s, i: (i, s)) if a_sharded else (lambda s, i: (i, 0))
    g_map = (lambda s, i: (i, 0)) if a_sharded else (lambda s, i: (i, s))

    def body(a_ref, g_ref, out_ref, acc_ref):
        p = _dot(a_ref[...], g_ref[...], _TN)
        k = pl.program_id(1)

        @pl.when(k == 0)
        def _():
            acc_ref[...] = p

        @pl.when(k > 0)
        def _():
            acc_ref[...] += p

        @pl.when(k == n_red - 1)
        def _():
            out_ref[...] = acc_ref[...].astype(out_ref.dtype)

    return pl.pallas_call(
        body, name=name, grid=(N_CHIPS, n_red),
        in_specs=[pl.BlockSpec((tm, ka), a_map), pl.BlockSpec((tm, ng), g_map)],
        out_specs=pl.BlockSpec((None, ka, ng), lambda s, i: (s, 0, 0)),
        out_shape=jax.ShapeDtypeStruct((N_CHIPS, ka, ng), BF16),
        scratch_shapes=[pltpu.VMEM((ka, ng), F32)], compiler_params=_params(("parallel", "arbitrary")),
    )(a, g)


def _layer_norm(z, g, b):
    mu = jnp.mean(z, axis=-1, keepdims=True)
    zc = z - mu
    rstd = lax.rsqrt(jnp.mean(zc * zc, axis=-1, keepdims=True) + LN_EPS)
    xhat = zc * rstd
    return xhat * g + b, xhat, rstd


def mlp_fwd(name, x, w1, w2, g, b, *, tm=512):
    fb = w1.shape[2]

    def body(x_ref, w1_ref, w2_ref, g_ref, b_ref, y_ref, xhat_ref, rstd_ref, relu_ref, acc_ref):
        s = pl.program_id(1)
        r = jnp.maximum(_dot(x_ref[...], w1_ref[...], _NN), 0.0)
        relu_ref[...] = r.astype(relu_ref.dtype)
        p = _dot(r * r, w2_ref[...], _NN)

        @pl.when(s == 0)
        def _():
            acc_ref[...] = p

        @pl.when(s > 0)
        def _():
            acc_ref[...] += p

        @pl.when(s == N_CHIPS - 1)
        def _():
            y_ref[...], xhat_ref[...], rstd_ref[...] = _layer_norm(ALPHA * x_ref[...] + acc_ref[...], g_ref[...], b_ref[...])

    row = pl.BlockSpec((tm, D), lambda i, s: (i, 0))
    vec = pl.BlockSpec((1, D), lambda i, s: (0, 0))
    return pl.pallas_call(
        body, name=name, grid=(S // tm, N_CHIPS),
        in_specs=[row, pl.BlockSpec((None, D, fb), lambda i, s: (s, 0, 0)),
                  pl.BlockSpec((None, fb, D), lambda i, s: (s, 0, 0)), vec, vec],
        out_specs=[row, row, pl.BlockSpec((tm, 1), lambda i, s: (i, 0)), pl.BlockSpec((tm, fb), lambda i, s: (i, s))],
        out_shape=[jax.ShapeDtypeStruct((S, D), F32), jax.ShapeDtypeStruct((S, D), F32),
                   jax.ShapeDtypeStruct((S, 1), F32), jax.ShapeDtypeStruct((S, N_CHIPS * fb), BF16)],
        scratch_shapes=[pltpu.VMEM((tm, D), F32)], compiler_params=_params(("parallel", "arbitrary")),
    )(x, w1, w2, g, b)


def mlp_bwd(name, dz, relu, x, w1, w2, *, tm=256):
    fb = w1.shape[2]
    n_i = S // tm

    def body(dz_ref, relu_ref, x_ref, w1_ref, w2_ref, dx_ref, g1_ref, g2_ref, acc1_ref, acc2_ref):
        s, i = pl.program_id(0), pl.program_id(1)
        dz_v = dz_ref[...]
        dz_b = dz_v.astype(BF16)
        r = relu_ref[...]
        dh = (_dot(dz_b, w2_ref[...], _NT) * (2.0 * r.astype(F32))).astype(BF16)
        p2 = _dot(r * r, dz_b, _TN)
        p1 = _dot(x_ref[...], dh, _TN)
        rows = pl.ds(pl.multiple_of(i * tm, tm), tm)
        pdx = _dot(dh, w1_ref[...], _NT)

        @pl.when(s == 0)
        def _():
            dx_ref[rows, :] = pdx + ALPHA * dz_v

        @pl.when(s > 0)
        def _():
            dx_ref[rows, :] += pdx

        @pl.when(i == 0)
        def _():
            acc1_ref[...] = p1
            acc2_ref[...] = p2

        @pl.when(i > 0)
        def _():
            acc1_ref[...] += p1
            acc2_ref[...] += p2

        @pl.when(i == n_i - 1)
        def _():
            g1_ref[...] = acc1_ref[...].astype(g1_ref.dtype)
            g2_ref[...] = acc2_ref[...].astype(g2_ref.dtype)

    row = pl.BlockSpec((tm, D), lambda s, i: (i, 0))
    return pl.pallas_call(
        body, name=name, grid=(N_CHIPS, n_i),
        in_specs=[row, pl.BlockSpec((tm, fb), lambda s, i: (i, s)), row,
                  pl.BlockSpec((None, D, fb), lambda s, i: (s, 0, 0)), pl.BlockSpec((None, fb, D), lambda s, i: (s, 0, 0))],
        out_specs=[pl.BlockSpec((S, D), lambda s, i: (0, 0)), pl.BlockSpec((None, D, fb), lambda s, i: (s, 0, 0)),
                   pl.BlockSpec((None, fb, D), lambda s, i: (s, 0, 0))],
        out_shape=[jax.ShapeDtypeStruct((S, D), F32), jax.ShapeDtypeStruct((N_CHIPS, D, fb), BF16),
                   jax.ShapeDtypeStruct((N_CHIPS, fb, D), BF16)],
        scratch_shapes=[pltpu.VMEM((D, fb), F32), pltpu.VMEM((fb, D), F32)],
        compiler_params=_params(("arbitrary", "arbitrary")),
    )(dz, relu, x, w1, w2)


def ln_bwd(name, dy, xhat, rstd, g, *, tm=256, after=None):
    n_after = 0 if after is None else 1

    def body(dy_ref, xhat_ref, rstd_ref, g_ref, *rest):
        dz_ref, dg_ref, db_ref = rest[n_after:]
        dy_v = dy_ref[...]
        xh = xhat_ref[...]
        dxh = dy_v * g_ref[...]
        m1 = jnp.mean(dxh, axis=-1, keepdims=True)
        m2 = jnp.mean(dxh * xh, axis=-1, keepdims=True)
        dz_ref[...] = rstd_ref[...] * (dxh - m1 - xh * m2)
        pg = jnp.sum(dy_v * xh, axis=0, keepdims=True)
        pb = jnp.sum(dy_v, axis=0, keepdims=True)
        i = pl.program_id(0)

        @pl.when(i == 0)
        def _():
            dg_ref[...] = pg
            db_ref[...] = pb

        @pl.when(i > 0)
        def _():
            dg_ref[...] += pg
            db_ref[...] += pb

    row = pl.BlockSpec((tm, D), lambda i: (i, 0))
    vec = pl.BlockSpec((1, D), lambda i: (0, 0))
    return pl.pallas_call(
        body, name=name, grid=(S // tm,),
        in_specs=[row, row, pl.BlockSpec((tm, 1), lambda i: (i, 0)), vec] + [pl.BlockSpec(memory_space=pl.ANY)] * n_after,
        out_specs=[row, vec, vec],
        out_shape=[jax.ShapeDtypeStruct((S, D), F32), jax.ShapeDtypeStruct((1, D), F32),
                   jax.ShapeDtypeStruct((1, D), F32)],
        compiler_params=_params(("arbitrary",)),
    )(dy, xhat, rstd, g, *([after] * n_after))


def loss_head(y, target, *, tm=256):
    def body(y_ref, t_ref, dy_ref, loss_ref):
        e = y_ref[...] - t_ref[...]
        dy_ref[...] = e * (1.0 / D)
        part = jnp.sum(jnp.sum(e * e, axis=-1, keepdims=True), axis=0, keepdims=True) * (0.5 / D)
        i = pl.program_id(0)

        @pl.when(i == 0)
        def _():
            loss_ref[...] = jnp.zeros_like(loss_ref)

        loss_ref[...] += jnp.broadcast_to(part, loss_ref.shape)

    row = pl.BlockSpec((tm, D), lambda i: (i, 0))
    return pl.pallas_call(
        body, name="loss_head", grid=(S // tm,), in_specs=[row, row],
        out_specs=[row, pl.BlockSpec((8, 128), lambda i: (0, 0))],
        out_shape=[jax.ShapeDtypeStruct((S, D), F32), jax.ShapeDtypeStruct((8, 128), F32)],
        compiler_params=_params(("arbitrary",)),
    )(y, target)


def _rows(shape):
    return lax.broadcasted_iota(jnp.int32, shape, 0)


def _shift_down(x, k):
    return jnp.where(_rows(x.shape) >= k, pltpu.roll(x, k, 0), 0.0)


def _shift_up(x, k):
    n = x.shape[0]
    return jnp.where(_rows(x.shape) < n - k, pltpu.roll(x, n - k, 0), 0.0)


def _pool_diff(u, w):
    acc, k = u, 1
    while k < w:
        acc = acc + _shift_down(acc, k)
        k *= 2
    cnt = jnp.minimum(_rows(u.shape) + 1, w).astype(F32)
    return acc / cnt - u, cnt


def pool_fwd(name, proj, pool_w, pool_scale):
    def body(u_ref, w_ref, sc_ref, y_ref):
        for g, w in enumerate(POOL_WINDOWS):
            cols = slice(g * HEAD, (g + 1) * HEAD)
            d, _ = _pool_diff(u_ref[:, cols], w)
            z = _dot(d, w_ref[g], _NN)
            y_ref[:, cols] = (z * sc_ref[:, cols]).astype(y_ref.dtype)

    return pl.pallas_call(
        body, name=name, grid=(1,),
        in_specs=[pl.BlockSpec((S, POOL_W), lambda i: (0, 0)),
                  pl.BlockSpec((4, HEAD, HEAD), lambda i: (0, 0, 0)),
                  pl.BlockSpec((1, POOL_W), lambda i: (0, 0))],
        out_specs=pl.BlockSpec((S, POOL_W), lambda i: (0, 0)),
        out_shape=jax.ShapeDtypeStruct((S, POOL_W), BF16),
        compiler_params=_params(("arbitrary",)),
    )(proj, pool_w, pool_scale)


def pool_bwd(name, proj, dycat, pool_w, pool_scale):
    def body(u_ref, dy_ref, w_ref, sc_ref, du_ref, dw_ref, dsc_ref):
        for g, w in enumerate(POOL_WINDOWS):
            cols = slice(g * HEAD, (g + 1) * HEAD)
            d, cnt = _pool_diff(u_ref[:, cols], w)
            dy = dy_ref[:, cols]
            z = _dot(d, w_ref[g], _NN)
            dsc_ref[:, cols] = jnp.sum(dy * z, axis=0, keepdims=True)
            dz = dy * sc_ref[:, cols]
            dw_ref[g] = _dot(d, dz, _TN)
            dd = _dot(dz, w_ref[g], _NT)
            acc, k = dd / cnt, 1
            while k < w:
                acc = acc + _shift_up(acc, k)
                k *= 2
            du_ref[:, cols] = (acc - dd).astype(du_ref.dtype)

    return pl.pallas_call(
        body, name=name, grid=(1,),
        in_specs=[pl.BlockSpec((S, POOL_W), lambda i: (0, 0)),
                  pl.BlockSpec((S, POOL_W), lambda i: (0, 0)),
                  pl.BlockSpec((4, HEAD, HEAD), lambda i: (0, 0, 0)),
                  pl.BlockSpec((1, POOL_W), lambda i: (0, 0))],
        out_specs=[pl.BlockSpec((S, POOL_W), lambda i: (0, 0)),
                   pl.BlockSpec((4, HEAD, HEAD), lambda i: (0, 0, 0)),
                   pl.BlockSpec((1, POOL_W), lambda i: (0, 0))],
        out_shape=[jax.ShapeDtypeStruct((S, POOL_W), BF16), jax.ShapeDtypeStruct((4, HEAD, HEAD), F32),
                   jax.ShapeDtypeStruct((1, POOL_W), F32)],
        compiler_params=_params(("arbitrary",)),
    )(proj, dycat, pool_w, pool_scale)


def _expm1(x):
    series = x * (1.0 + x * (0.5 + x * (1.0 / 6.0 + x * (1.0 / 24.0 + x * (1.0 / 120.0)))))
    return jnp.where(jnp.abs(x) < 0.05, series, jnp.exp(x) - 1.0)


def _softplus_neg(lam):
    e = jnp.exp(-jnp.abs(lam))
    log1p = jnp.where(e < 0.01, e * (1.0 - e * (0.5 - e * (1.0 / 3.0))), jnp.log(1.0 + e))
    return jnp.maximum(-lam, 0.0) + log1p


_GELU_C = math.sqrt(2.0 / math.pi)


def _gelu(x):
    t = jnp.tanh(_GELU_C * (x + 0.044715 * x * x * x))
    return 0.5 * x * (1.0 + t), t


def _gelu_grad(x, t):
    return 0.5 * (1.0 + t) + 0.5 * x * (1.0 - t * t) * _GELU_C * (1.0 + 3.0 * 0.044715 * x * x)


def _conv(u, cw, cb):
    return cw[3:4] * u + cw[2:3] * _shift_down(u, 1) + cw[1:2] * _shift_down(u, 2) + cw[0:1] * _shift_down(u, 3) + cb


def _lru_gates(cu, wa, ba, wx, bx, lam):
    r = jax.nn.sigmoid(_dot(cu, wa, _NN) + ba)
    i = jax.nn.sigmoid(_dot(cu, wx, _NN) + bx)
    sp = _softplus_neg(lam)
    log_a = (-LRU_C) * r * sp
    a = jnp.exp(log_a)
    mult = jnp.sqrt(-_expm1(2.0 * log_a))
    return r, i, sp, a, mult


def _scan(a_ref, b_ref, h_ref, *, reverse):
    n_blk = S // 8
    row8 = lax.broadcasted_iota(jnp.int32, (8, HEAD), 0)

    def step(j, carry):
        blk = (n_blk - 1 - j) if reverse else j
        r0 = pl.multiple_of(blk * 8, 8)
        a = a_ref[pl.ds(r0, 8), :]
        b = b_ref[pl.ds(r0, 8), :]
        for k in (1, 2, 4):
            if reverse:
                keep = row8 < 8 - k
                a_s, b_s = pltpu.roll(a, 8 - k, 0), pltpu.roll(b, 8 - k, 0)
            else:
                keep = row8 >= k
                a_s, b_s = pltpu.roll(a, k, 0), pltpu.roll(b, k, 0)
            b = jnp.where(keep, a * b_s + b, b)
            a = jnp.where(keep, a * a_s, a)
        h = b + a * carry
        h_ref[pl.ds(r0, 8), :] = h
        edge = h[0:1, :] if reverse else h[7:8, :]
        return jnp.broadcast_to(edge, (8, HEAD))

    lax.fori_loop(0, n_blk, step, jnp.zeros((8, HEAD), F32), unroll=4)


def _lru_specs():
    def col(off):
        return pl.BlockSpec((S, HEAD), lambda h: (0, off + h))
    vec = pl.BlockSpec((1, HEAD), lambda h: (0, h))
    mat = pl.BlockSpec((None, HEAD, HEAD), lambda h: (h, 0, 0))
    cw = pl.BlockSpec((4, HEAD), lambda h: (0, h))
    return col, vec, mat, cw


def lru_fwd(name, proj, conv_w, conv_b, w_a, b_a, w_x, b_x, lam):
    def body(u_ref, ug_ref, cw_ref, cb_ref, wa_ref, ba_ref, wx_ref, bx_ref, lam_ref, y_ref, h_ref, a_s, b_s):
        cu = _conv(u_ref[...], cw_ref[...], cb_ref[...])
        _, i, _, a, mult = _lru_gates(cu, wa_ref[...], ba_ref[...], wx_ref[...], bx_ref[...], lam_ref[...])
        a_s[...] = a
        b_s[...] = mult * (i * cu)
        _scan(a_s, b_s, h_ref, reverse=False)
        gl, _ = _gelu(ug_ref[...])
        y_ref[...] = (h_ref[...] * gl).astype(y_ref.dtype)

    col, vec, mat, cw = _lru_specs()
    out = pl.BlockSpec((S, HEAD), lambda h: (0, h))
    return pl.pallas_call(
        body, name=name, grid=(LRU_HEADS,),
        in_specs=[col(4), col(12), cw, vec, mat, vec, mat, vec, vec],
        out_specs=[out, out],
        out_shape=[jax.ShapeDtypeStruct((S, LRU_W), BF16), jax.ShapeDtypeStruct((S, LRU_W), F32)],
        scratch_shapes=[pltpu.VMEM((S, HEAD), F32), pltpu.VMEM((S, HEAD), F32)],
        compiler_params=_params(("parallel",)),
    )(proj, proj, conv_w, conv_b, w_a, b_a, w_x, b_x, lam)


def lru_bwd(name, proj, hstate, dycat, conv_w, conv_b, w_a, b_a, w_x, b_x, lam):
    def body(u_ref, ug_ref, h_ref, dy_ref, cw_ref, cb_ref, wa_ref, ba_ref, wx_ref, bx_ref, lam_ref,
             du_ref, dug_ref, dwa_ref, dwx_ref, dba_ref, dbx_ref, dlam_ref, dcw_ref, dcb_ref, a_s, b_s, g_s):
        u = u_ref[...]
        cw = cw_ref[...]
        cu = _conv(u, cw, cb_ref[...])
        lam_v = lam_ref[...]
        r, i, sp, a, mult = _lru_gates(cu, wa_ref[...], ba_ref[...], wx_ref[...], bx_ref[...], lam_v)
        ug = ug_ref[...]
        gl, t = _gelu(ug)
        dy = dy_ref[...]
        h = h_ref[...]
        dug_ref[...] = (dy * h * _gelu_grad(ug, t)).astype(dug_ref.dtype)
        a_s[...] = _shift_up(a, 1)
        b_s[...] = dy * gl
        _scan(a_s, b_s, g_s, reverse=True)
        dxin = g_s[...]
        da = dxin * _shift_down(h, 1)
        dmult = dxin * (i * cu)
        di = dxin * (mult * cu)
        dlog_a = da * a - dmult * (a * a) / mult
        dr_pre = dlog_a * ((-LRU_C) * sp) * (r * (1.0 - r))
        di_pre = di * (i * (1.0 - i))
        dsp = jnp.sum(dlog_a * ((-LRU_C) * r), axis=0, keepdims=True)
        dlam_ref[...] = dsp * (-jax.nn.sigmoid(-lam_v))
        dba_ref[...] = jnp.sum(dr_pre, axis=0, keepdims=True)
        dbx_ref[...] = jnp.sum(di_pre, axis=0, keepdims=True)
        dwa_ref[...] = _dot(cu, dr_pre, _TN)
        dwx_ref[...] = _dot(cu, di_pre, _TN)
        dcu = dxin * (mult * i) + _dot(dr_pre, wa_ref[...], _NT) + _dot(di_pre, wx_ref[...], _NT)
        dcb_ref[...] = jnp.sum(dcu, axis=0, keepdims=True)
        for k in range(4):
            dcw_ref[k:k + 1, :] = jnp.sum(dcu * (_shift_down(u, 3 - k) if k < 3 else u), axis=0, keepdims=True)
        du = cw[3:4] * dcu + cw[2:3] * _shift_up(dcu, 1) + cw[1:2] * _shift_up(dcu, 2) + cw[0:1] * _shift_up(dcu, 3)
        du_ref[...] = du.astype(du_ref.dtype)

    col, vec, mat, cw = _lru_specs()
    out = pl.BlockSpec((S, HEAD), lambda h: (0, h))
    big = jax.ShapeDtypeStruct((S, LRU_W), BF16)
    vec_shape = jax.ShapeDtypeStruct((1, LRU_W), F32)
    mat_shape = jax.ShapeDtypeStruct((LRU_HEADS, HEAD, HEAD), F32)
    return pl.pallas_call(
        body, name=name, grid=(LRU_HEADS,),
        in_specs=[col(4), col(12), out, col(4), cw, vec, mat, vec, mat, vec, vec],
        out_specs=[out, out, mat, mat, vec, vec, vec, cw, vec],
        out_shape=[big, big, mat_shape, mat_shape, vec_shape, vec_shape, vec_shape,
                   jax.ShapeDtypeStruct((4, LRU_W), F32), vec_shape],
        scratch_shapes=[pltpu.VMEM((S, HEAD), F32)] * 3,
        compiler_params=_params(("parallel",)),
    )(proj, proj, hstate, dycat, conv_w, conv_b, w_a, b_a, w_x, b_x, lam)


def rope_tables(pos_col, inv_freq):
    def body(pos_ref, f_ref, c_ref, s1_ref, s2_ref):
        ang = pos_ref[...].astype(F32) * f_ref[...]
        lane = lax.broadcasted_iota(jnp.int32, ang.shape, 1)
        cos, sin = jnp.cos(ang), jnp.sin(ang)
        c_ref[...] = jnp.where(lane < QK_ROPE, cos, 0.0)
        s1_ref[...] = jnp.where(lane < QK_ROPE // 2, -sin, 0.0)
        s2_ref[...] = jnp.where((lane >= QK_ROPE // 2) & (lane < QK_ROPE), sin, 0.0)

    tab = jax.ShapeDtypeStruct((S, HEAD), F32)
    return pl.pallas_call(
        body, name="rope_tables", grid=(1,),
        in_specs=[pl.BlockSpec((S, 1), lambda i: (0, 0)), pl.BlockSpec((1, HEAD), lambda i: (0, 0))],
        out_specs=[pl.BlockSpec((S, HEAD), lambda i: (0, 0))] * 3, out_shape=[tab, tab, tab],
        compiler_params=_params(("arbitrary",)),
    )(pos_col, inv_freq)


def _rope(v, c, s1, s2):
    return v * c + pltpu.roll(v, HEAD - QK_ROPE // 2, 1) * s1 + pltpu.roll(v, QK_ROPE // 2, 1) * s2


def _unrope(d, c, s1, s2):
    return d * c + pltpu.roll(d * s1, QK_ROPE // 2, 1) + pltpu.roll(d * s2, HEAD - QK_ROPE // 2, 1)


def _rms(x, g):
    rstd = lax.rsqrt(jnp.mean(x * x, axis=-1, keepdims=True) + RMS_EPS)
    return x * rstd, rstd


def mla_prep(name, down, gq, gkv, tabs, *, tm=256):
    def body(dn_ref, gq_ref, gkv_ref, c_ref, s1_ref, s2_ref, cq_ref, ckv_ref, kp_ref):
        xq, _ = _rms(dn_ref[:, :Q_RANK], None)
        cq_ref[...] = (xq * gq_ref[...]).astype(cq_ref.dtype)
        xkv, _ = _rms(dn_ref[:, Q_RANK:Q_RANK + KV_RANK], None)
        ckv_ref[...] = (xkv * gkv_ref[...]).astype(ckv_ref.dtype)
        kp = _rope(dn_ref[:, Q_RANK + KV_RANK:], c_ref[...], s1_ref[...], s2_ref[...])
        kp_ref[...] = kp.astype(kp_ref.dtype)

    tab = pl.BlockSpec((tm, HEAD), lambda i: (i, 0))
    return pl.pallas_call(
        body, name=name, grid=(S // tm,),
        in_specs=[pl.BlockSpec((tm, ODD_IN_PAD), lambda i: (i, 0)), pl.BlockSpec((1, Q_RANK), lambda i: (0, 0)),
                  pl.BlockSpec((1, KV_RANK), lambda i: (0, 0)), tab, tab, tab],
        out_specs=[pl.BlockSpec((tm, Q_RANK), lambda i: (i, 0)), pl.BlockSpec((tm, KV_RANK), lambda i: (i, 0)), tab],
        out_shape=[jax.ShapeDtypeStruct((S, Q_RANK), BF16), jax.ShapeDtypeStruct((S, KV_RANK), BF16),
                   jax.ShapeDtypeStruct((S, HEAD), BF16)],
        compiler_params=_params(("parallel",)),
    )(down, gq, gkv, *tabs)


def mla_prep_bwd(name, down, dcq, dckv, dkp, gq, gkv, tabs, *, tm=256):
    def body(dn_ref, dcq_ref, dckv_ref, dkp_ref, gq_ref, gkv_ref, c_ref, s1_ref, s2_ref, dd_ref, dgq_ref, dgkv_ref):
        i = pl.program_id(0)

        def rms_bwd(x, dy, g, dg_ref):
            xh, rstd = _rms(x, None)
            dxh = dy * g
            dx = rstd * (dxh - xh * jnp.mean(dxh * xh, axis=-1, keepdims=True))
            pg = jnp.sum(dy * xh, axis=0, keepdims=True)

            @pl.when(i == 0)
            def _():
                dg_ref[...] = pg

            @pl.when(i > 0)
            def _():
                dg_ref[...] += pg

            return dx

        dxq = rms_bwd(dn_ref[:, :Q_RANK], dcq_ref[...], gq_ref[...], dgq_ref)
        dd_ref[:, :Q_RANK] = dxq.astype(dd_ref.dtype)
        dxkv = rms_bwd(dn_ref[:, Q_RANK:Q_RANK + KV_RANK], dckv_ref[...], gkv_ref[...], dgkv_ref)
        dd_ref[:, Q_RANK:Q_RANK + KV_RANK] = dxkv.astype(dd_ref.dtype)
        dd_ref[:, Q_RANK + KV_RANK:] = _unrope(dkp_ref[...], c_ref[...], s1_ref[...], s2_ref[...]).astype(dd_ref.dtype)

    tab = pl.BlockSpec((tm, HEAD), lambda i: (i, 0))
    vq = pl.BlockSpec((1, Q_RANK), lambda i: (0, 0))
    vkv = pl.BlockSpec((1, KV_RANK), lambda i: (0, 0))
    return pl.pallas_call(
        body, name=name, grid=(S // tm,),
        in_specs=[pl.BlockSpec((tm, ODD_IN_PAD), lambda i: (i, 0)), pl.BlockSpec((tm, Q_RANK), lambda i: (i, 0)),
                  pl.BlockSpec((tm, KV_RANK), lambda i: (i, 0)), tab, vq, vkv, tab, tab, tab],
        out_specs=[pl.BlockSpec((tm, ODD_IN_PAD), lambda i: (i, 0)), vq, vkv],
        out_shape=[jax.ShapeDtypeStruct((S, ODD_IN_PAD), BF16), jax.ShapeDtypeStruct((1, Q_RANK), F32),
                   jax.ShapeDtypeStruct((1, KV_RANK), F32)],
        compiler_params=_params(("arbitrary",)),
    )(down, dcq, dckv, dkp, gq, gkv, *tabs)


ATT_TQ = 256


def _attn_probs(q_ref, kv_ref, kp_ref, c_ref, s1_ref, s2_ref, i, nk):
    qn = q_ref[:, :HEAD].astype(BF16)
    qp = _rope(q_ref[:, HEAD:], c_ref[...], s1_ref[...], s2_ref[...]).astype(BF16)
    kn = kv_ref[:nk, :HEAD]
    sc = (_dot(qn, kn, _NT) + _dot(qp, kp_ref[:nk, :], _NT)) * ATT_SCALE
    q_chunk = (i * ATT_TQ + lax.broadcasted_iota(jnp.int32, sc.shape, 0)) // CHUNK
    k_chunk = lax.broadcasted_iota(jnp.int32, sc.shape, 1) // CHUNK
    sc = jnp.where(k_chunk <= q_chunk, sc, jnp.finfo(F32).min)
    e = jnp.exp(sc - jnp.max(sc, axis=-1, keepdims=True))
    p = e * (1.0 / jnp.sum(e, axis=-1, keepdims=True))
    return p, qn, qp, kn


def _for_each_prefix(i, fn):
    for k in range(S // ATT_TQ):
        pl.when(i == k)(functools.partial(fn, (k + 1) * ATT_TQ))


def _attn_specs():
    q = pl.BlockSpec((ATT_TQ, QHEAD_PAD), lambda h, i: (i, h))
    kv = pl.BlockSpec((S, QHEAD_PAD), lambda h, i: (0, h))
    kp = pl.BlockSpec((S, HEAD), lambda h, i: (0, 0))
    tab = pl.BlockSpec((ATT_TQ, HEAD), lambda h, i: (i, 0))
    o = pl.BlockSpec((ATT_TQ, HEAD), lambda h, i: (i, h))
    return q, kv, kp, tab, o


def attn_fwd(name, q, kv, kp, tabs):
    def body(q_ref, kv_ref, kp_ref, c_ref, s1_ref, s2_ref, o_ref):
        i = pl.program_id(1)

        def run(nk):
            p, _, _, _ = _attn_probs(q_ref, kv_ref, kp_ref, c_ref, s1_ref, s2_ref, i, nk)
            o_ref[...] = _dot(p, kv_ref[:nk, HEAD:], _NN).astype(o_ref.dtype)

        _for_each_prefix(i, run)

    qs, kvs, kps, tab, os = _attn_specs()
    return pl.pallas_call(
        body, name=name, grid=(MLA_HEADS, S // ATT_TQ), in_specs=[qs, kvs, kps, tab, tab, tab], out_specs=os,
        out_shape=jax.ShapeDtypeStruct((S, MLA_HEADS * HEAD), BF16),
        compiler_params=_params(("parallel", "parallel")),
    )(q, kv, kp, *tabs)


def attn_bwd(name, q, kv, kp, do, tabs):
    def body(q_ref, kv_ref, kp_ref, do_ref, c_ref, s1_ref, s2_ref, dq_ref, dkv_ref, dkp_ref):
        h, i = pl.program_id(0), pl.program_id(1)

        @pl.when(i == 0)
        def _():
            dkv_ref[...] = jnp.zeros_like(dkv_ref)

        @pl.when((i == 0) & (h == 0))
        def _():
            dkp_ref[...] = jnp.zeros_like(dkp_ref)

        def run(nk):
            p, qn, qp, kn = _attn_probs(q_ref, kv_ref, kp_ref, c_ref, s1_ref, s2_ref, i, nk)
            do_v = do_ref[...]
            dp = _dot(do_v, kv_ref[:nk, HEAD:], _NT)
            ds = (p * (dp - jnp.sum(p * dp, axis=-1, keepdims=True)) * ATT_SCALE).astype(BF16)
            dq_ref[:, :HEAD] = _dot(ds, kn, _NN).astype(dq_ref.dtype)
            dqp = _unrope(_dot(ds, kp_ref[:nk, :], _NN), c_ref[...], s1_ref[...], s2_ref[...])
            dq_ref[:, HEAD:] = dqp.astype(dq_ref.dtype)
            dkv_ref[:nk, :HEAD] += _dot(ds, qn, _TN)
            dkv_ref[:nk, HEAD:] += _dot(p, do_v, _TN)
            dkp_ref[:nk, :] += _dot(ds, qp, _TN)

        _for_each_prefix(i, run)

    qs, kvs, kps, tab, os = _attn_specs()
    return pl.pallas_call(
        body, name=name, grid=(MLA_HEADS, S // ATT_TQ), in_specs=[qs, kvs, kps, os, tab, tab, tab],
        out_specs=[qs, kvs, kps],
        out_shape=[jax.ShapeDtypeStruct((S, MLA_HEADS * QHEAD_PAD), BF16),
                   jax.ShapeDtypeStruct((S, MLA_HEADS * QHEAD_PAD), F32), jax.ShapeDtypeStruct((S, HEAD), F32)],
        compiler_params=_params(("arbitrary", "arbitrary")),
    )(q, kv, kp, do, *tabs)


def adamw(name, w, g, m, v):
    rows, cols = w.shape
    tr = rows
    for cand in (512, 256, 128, 64, 32, 16, 8):
        if rows % cand == 0 and cand * cols * 4 <= 2 * 1024 * 1024:
            tr = cand
            break

    def body(w_ref, g_ref, m_ref, v_ref, d_ref, nm_ref, nv_ref):
        g_v = g_ref[...]
        nm = ADAM_B1 * m_ref[...] + (1.0 - ADAM_B1) * g_v
        nv = ADAM_B2 * v_ref[...] + (1.0 - ADAM_B2) * (g_v * g_v)
        m_hat = nm / (1.0 - ADAM_B1 ** ADAM_STEP)
        v_hat = nv / (1.0 - ADAM_B2 ** ADAM_STEP)
        d_ref[...] = (-ADAM_LR) * (m_hat / (jnp.sqrt(v_hat) + ADAM_EPS) + ADAM_WD * w_ref[...])
        nm_ref[...] = nm
        nv_ref[...] = nv

    blk = pl.BlockSpec((tr, cols), lambda i: (i, 0))
    shape = jax.ShapeDtypeStruct((rows, cols), F32)
    return pl.pallas_call(
        body, name=name, grid=(rows // tr,), in_specs=[blk] * 4, out_specs=[blk] * 3, out_shape=[shape] * 3,
        compiler_params=_params(("parallel",)),
    )(w, g, m, v)


def _local_step(x, pos_col, target, sm, weights_of, emit_grads):
    inv_freq = ROPE_THETA ** (-jnp.arange(0, QK_ROPE, 2, dtype=F32) / QK_ROPE)
    inv_freq = jnp.concatenate([inv_freq, inv_freq, jnp.zeros((HEAD - QK_ROPE,), F32)])[None, :]
    tabs = rope_tables(pos_col, inv_freq)
    saved, wts = [], {}
    for layer in range(DEPTH):
        j = layer // 2
        n = "l%d_" % layer
        sv = {"x": x}
        wm = wts["mix%d" % layer] = weights_of("mix%d" % layer, x)
        if layer == 0:
            sm = dict(sm, conv_w=wm["conv_w"], gq=wm["gq"], gkv=wm["gkv"])
        if layer % 2 == 0:
            proj = mm_cols(n + "proj", x, wm["w_in"], [F32])[0]
            y_pool = pool_fwd(n + "pool", proj, sm["pool_w"][j], sm["pool_scale"][j][None])
            y_lru, hstate = lru_fwd(n + "lru", proj, sm["conv_w"][j], sm["conv_b"][j][None], sm["w_a"][j],
                                    sm["b_a"][j][None], sm["w_x"][j], sm["b_x"][j][None], sm["lam"][j][None])
            ycat = jnp.concatenate([y_pool, y_lru], axis=1)
            mix_in, w_mix = ycat, wm["w_out"]
            sv.update(proj=proj, hstate=hstate, ycat=ycat)
        else:
            down = mm_rows(n + "down", x, wm["w_down"], F32)
            cq, ckv, kp = mla_prep(n + "prep", down, sm["gq"][j][None], sm["gkv"][j][None], tabs)
            q = mm_cols(n + "q", cq, wm["w_qb"], [F32])[0]
            kv = mm_cols(n + "kv", ckv, wm["w_kvb"], [BF16])[0]
            o = attn_fwd(n + "attn", q, kv, kp, tabs)
            mix_in, w_mix = o, wm["w_o"]
            sv.update(down=down, cq=cq, ckv=ckv, kp=kp, q=q, kv=kv, o=o)
        x1, xhat1, rstd1 = mm_rows_ln(n + "mixout", mix_in, w_mix, x, sm["ln_mix_g"][layer][None],
                                      sm["ln_mix_b"][layer][None])
        wf = wts["mlp%d" % layer] = weights_of("mlp%d" % layer, x1)
        x2, xhat2, rstd2, relu = mlp_fwd(n + "mlp", x1, wf["w1"], wf["w2"], sm["ln_ffn_g"][layer][None],
                                         sm["ln_ffn_b"][layer][None])
        sv.update(xhat1=xhat1, rstd1=rstd1, x1=x1, relu=relu, xhat2=xhat2, rstd2=rstd2)
        saved.append(sv)
        x = x2

    dx, loss_tile = loss_head(x, target)
    gs = {k: [None] * (DEPTH if k.startswith("ln_") else DEPTH // 2) for k in sm}
    for layer in reversed(range(DEPTH)):
        j = layer // 2
        n = "l%d_" % layer
        sv = saved[layer]
        wm, wf = wts["mix%d" % layer], wts["mlp%d" % layer]
        dz, gs["ln_ffn_g"][layer], gs["ln_ffn_b"][layer] = ln_bwd(
            n + "ln_ffn_b", dx, sv["xhat2"], sv["rstd2"], sm["ln_ffn_g"][layer][None])
        dx, g_w1, g_w2 = mlp_bwd(n + "mlp_b", dz, sv["relu"], sv["x1"], wf["w1"], wf["w2"])
        sent = emit_grads("mlp%d" % layer, {"w1": g_w1, "w2": g_w2})
        dz, gs["ln_mix_g"][layer], gs["ln_mix_b"][layer] = ln_bwd(
            n + "ln_mix_b", dx, sv["xhat1"], sv["rstd1"], sm["ln_mix_g"][layer][None], after=sent)
        if layer % 2 == 0:
            dycat = mm_t_rows(n + "mixout_dx", dz, wm["w_out"], F32)
            g_out = mm_grad(n + "mixout_dw", sv["ycat"], dz, a_sharded=True)
            du_pool, gs["pool_w"][j], gs["pool_scale"][j] = pool_bwd(
                n + "pool_b", sv["proj"], dycat, sm["pool_w"][j], sm["pool_scale"][j][None])
            (du_lru, du_gate, gs["w_a"][j], gs["w_x"][j], gs["b_a"][j], gs["b_x"][j], gs["lam"][j], gs["conv_w"][j],
             gs["conv_b"][j]) = lru_bwd(n + "lru_b", sv["proj"], sv["hstate"], dycat, sm["conv_w"][j],
                                        sm["conv_b"][j][None], sm["w_a"][j], sm["b_a"][j][None], sm["w_x"][j],
                                        sm["b_x"][j][None], sm["lam"][j][None])
            dproj = jnp.concatenate([du_pool, du_lru, du_gate], axis=1)
            g_in = mm_grad(n + "proj_dw", sv["x"], dproj, a_sharded=False)
            sent = emit_grads("mix%d" % layer, {"w_in": g_in, "w_out": g_out})
            dx = mm_t_cols(n + "proj_dx", dproj, wm["w_in"], F32, resid=dz, after=sent)
        else:
            do = mm_t_rows(n + "attnout_dx", dz, wm["w_o"], BF16)
            g_o = mm_grad(n + "attnout_dw", sv["o"], dz, a_sharded=True)
            dq, dkv, dkp = attn_bwd(n + "attn_b", sv["q"], sv["kv"], sv["kp"], do, tabs)
            g_qb = mm_grad(n + "q_dw", sv["cq"], dq, a_sharded=False)
            dcq = mm_t_cols(n + "q_dx", dq, wm["w_qb"], F32)
            g_kvb = mm_grad(n + "kv_dw", sv["ckv"], dkv, a_sharded=False)
            dckv = mm_t_cols(n + "kv_dx", dkv, wm["w_kvb"], F32)
            ddown, gs["gq"][j], gs["gkv"][j] = mla_prep_bwd(
                n + "prep_b", sv["down"], dcq, dckv, dkp, sm["gq"][j][None], sm["gkv"][j][None], tabs)
            g_down = mm_grad(n + "down_dw", sv["x"], ddown, a_sharded=True)
            sent = emit_grads("mix%d" % layer, {"w_down": g_down, "w_qb": g_qb, "w_kvb": g_kvb, "w_o": g_o})
            dx = mm_t_rows(n + "down_dx", ddown, wm["w_down"], F32, resid=dz, after=sent)
    gs = {k: jnp.stack([a.reshape(sm[k].shape[1:]) for a in v]) for k, v in gs.items()}
    return loss_tile[0, 0], dx, gs


def _place():
    x, y, c = lax.axis_index("x"), lax.axis_index("y"), lax.axis_index("c")
    chips = [(1 - x, y), (x, 1 - y), (1 - x, 1 - y)]
    return x, y, c, chips


def _hbm_call(body, name, args, out_shape, scratch, aliases=None):
    return pl.pallas_call(
        body, name=name, in_specs=[pl.BlockSpec(memory_space=pl.ANY)] * len(args),
        out_specs=[pl.BlockSpec(memory_space=pl.ANY)] * len(out_shape), out_shape=out_shape,
        scratch_shapes=scratch, input_output_aliases=aliases or {},
        compiler_params=pltpu.CompilerParams(has_side_effects=True),
    )(*args)


HBM_SPEC = pl.BlockSpec(memory_space=pltpu.HBM)
SEM_SPEC = pl.BlockSpec(memory_space=pltpu.SEMAPHORE)
EFFECT = pltpu.SideEffectType.DATAFLOW_SIDE_EFFECTING


def _remote(src, dst, send_sem, recv_sem, device):
    return pltpu.make_async_remote_copy(src_ref=src, dst_ref=dst, send_sem=send_sem, recv_sem=recv_sem,
                                        device_id=device, device_id_type=MESH)


def place_own(name, srcs, place, steps, reducing):
    n = len(srcs)
    in_specs, out_specs, out_shape = [], [], []
    for s in srcs:
        if reducing:
            nd, rows, cols = s.shape
            tr = rows // 2 // steps
            pick = (lambda i, p: (p[0], p[1] * steps + i, 0)) if nd == N_CHIPS else (lambda i, p: (0, p[1] * steps + i, 0))
            in_specs.append(pl.BlockSpec((None, tr, cols), pick))
            out_specs.append(pl.BlockSpec((None, tr, cols), lambda i, p: (p[2], i, 0)))
            out_shape.append(jax.ShapeDtypeStruct((N_DEV, rows // 2, cols), s.dtype))
        else:
            rows, cols = s.shape
            tr = rows // steps
            in_specs.append(pl.BlockSpec((tr, cols), lambda i, p: (i, 0)))
            out_specs.append(pl.BlockSpec((None, tr, cols), lambda i, p: (p[0], i, 0)))
            out_shape.append(jax.ShapeDtypeStruct((N_CHIPS, rows, cols), s.dtype))

    def body(p_ref, *refs):
        for i_ref, o_ref in zip(refs[:n], refs[n:]):
            o_ref[...] = i_ref[...]

    return pl.pallas_call(
        body, name=name, out_shape=out_shape,
        grid_spec=pltpu.PrefetchScalarGridSpec(num_scalar_prefetch=1, grid=(steps,), in_specs=in_specs,
                                               out_specs=out_specs),
        compiler_params=_params(("parallel",)),
    )(place, *srcs)


def split_start(name, srcs, lands, n_sems, plan, token_in=None):
    n = len(srcs)
    srcs = [pltpu.with_memory_space_constraint(s, pltpu.HBM) for s in srcs]
    lands = [pltpu.with_memory_space_constraint(a, pltpu.HBM) for a in lands]
    extra = [] if token_in is None else [token_in]

    def body(*refs):
        src_refs, land_refs = refs[:n], refs[n:2 * n]
        outs = refs[2 * n + len(extra):]
        send_sems, recv_sems, token = outs[0], outs[1], outs[2 + 2 * n]
        sends, _ = plan(src_refs, land_refs, send_sems, recv_sems)
        for cp in sends:
            cp.start()
        token[...] = jnp.zeros_like(token)

    out_shape = (pltpu.SemaphoreType.DMA((n * n_sems,)), pltpu.SemaphoreType.DMA((n * n_sems,)),
                 *[pltpu.HBM(a.shape, a.dtype) for a in srcs + lands], jax.ShapeDtypeStruct((8, 128), F32))
    res = pl.pallas_call(
        body, name=name, out_shape=out_shape,
        in_specs=[HBM_SPEC] * (2 * n) + [pl.BlockSpec(memory_space=pl.ANY)] * len(extra),
        out_specs=(SEM_SPEC, SEM_SPEC, *[HBM_SPEC] * (2 * n), pl.BlockSpec(memory_space=pltpu.VMEM)),
        input_output_aliases={i: 2 + i for i in range(2 * n)},
        compiler_params=pltpu.CompilerParams(has_side_effects=EFFECT),
    )(*srcs, *lands, *extra)
    return dict(send=res[0], recv=res[1], srcs=list(res[2:2 + n]), lands=list(res[2 + n:2 + 2 * n]), plan=plan), res[-1]


def split_wait(name, started, after):
    sizes = [len(st["srcs"]) for st in started]
    n_buf = 2 * sum(sizes)

    def body(*refs):
        bufs, sems = refs[:n_buf], refs[n_buf:n_buf + 2 * len(started)]
        off = 0
        for g, (st, n) in enumerate(zip(started, sizes)):
            src_refs, land_refs = bufs[off:off + n], bufs[off + n:off + 2 * n]
            off += 2 * n
            sends, expects = st["plan"](src_refs, land_refs, sems[2 * g], sems[2 * g + 1])
            for cp in sends:
                cp.wait_send()
            for cp in expects:
                cp.wait_recv()

    bufs = [a for st in started for a in st["srcs"] + st["lands"]]
    sems = [s for st in started for s in (st["send"], st["recv"])]
    res = pl.pallas_call(
        body, name=name, out_shape=tuple(pltpu.HBM(a.shape, a.dtype) for a in bufs),
        in_specs=[HBM_SPEC] * n_buf + [SEM_SPEC] * len(sems) + [pl.BlockSpec(memory_space=pl.ANY)],
        out_specs=tuple([HBM_SPEC] * n_buf), input_output_aliases={i: i for i in range(n_buf)},
        compiler_params=pltpu.CompilerParams(has_side_effects=EFFECT),
    )(*bufs, *sems, after)
    out, off = [], 0
    for n in sizes:
        out.append(list(res[off + n:off + 2 * n]))
        off += 2 * n
    return out


def gather_plan(src_refs, land_refs, send_sems, recv_sems):
    x, y, c, chips = _place()
    me = 2 * x + y
    sends, expects = [], []
    for k, (s, d) in enumerate(zip(src_refs, land_refs)):
        for j, (px, py) in enumerate(chips):
            sem = 3 * k + j
            sends.append(_remote(s, d.at[me], send_sems.at[sem], recv_sems.at[sem], (px, py, c)))
            expects.append(_remote(s, d.at[2 * px + py], send_sems.at[sem], recv_sems.at[sem], (px, py, c)))
    return sends, expects


def _reduce_part(s, chip_idx, h):
    hr = s.shape[1] // 2
    return s.at[chip_idx if s.shape[0] == N_CHIPS else 0, pl.ds(h * hr, hr)]


def reduce_plan(src_refs, land_refs, send_sems, recv_sems):
    x, y, c, chips = _place()
    me_chip, me_dev = 2 * x + y, 4 * x + 2 * y + c
    sends, expects = [], []
    for k, (s, d) in enumerate(zip(src_refs, land_refs)):
        part = functools.partial(_reduce_part, s)
        for j, (px, py) in enumerate(chips):
            for h in range(2):
                sends.append(_remote(part(2 * px + py, h), d.at[me_dev], send_sems.at[7 * k + 2 * j + h],
                                     recv_sems.at[7 * k + 2 * j + c], (px, py, h)))
                expects.append(_remote(part(me_chip, c), d.at[4 * px + 2 * py + h], send_sems.at[7 * k + 2 * j + h],
                                       recv_sems.at[7 * k + 2 * j + h], (px, py, h)))
        sends.append(_remote(part(me_chip, 1 - c), d.at[me_dev], send_sems.at[7 * k + 6], recv_sems.at[7 * k + 6],
                             (x, y, 1 - c)))
        expects.append(_remote(part(me_chip, c), d.at[me_dev + 1 - 2 * c], send_sems.at[7 * k + 6],
                               recv_sems.at[7 * k + 6], (x, y, 1 - c)))
    return sends, expects


def sibling_swap_halves(fulls):
    n = len(fulls)

    def body(*refs):
        outs = refs[n:2 * n]
        send_sems, recv_sems = refs[2 * n:]
        x, y, c, _ = _place()
        copies = []
        for k in range(n):
            nl, rows = fulls[k].shape[:2]
            hr = rows // 2
            mine = outs[k].at[pl.ds(0, nl), pl.ds(c * hr, hr)]
            theirs = outs[k].at[pl.ds(0, nl), pl.ds((1 - c) * hr, hr)]
            copies.append((_remote(mine, mine, send_sems.at[k], recv_sems.at[k], (x, y, 1 - c)),
                           _remote(mine, theirs, send_sems.at[k], recv_sems.at[k], (x, y, 1 - c))))
        for send, _ in copies:
            send.start()
        for send, recv in copies:
            send.wait_send()
            recv.wait_recv()

    out_shape = [jax.ShapeDtypeStruct(f.shape, f.dtype) for f in fulls]
    scratch = [pltpu.SemaphoreType.DMA((n,)), pltpu.SemaphoreType.DMA((n,))]
    return _hbm_call(body, "sibling_swap_halves", fulls, out_shape, scratch, aliases={k: k for k in range(n)})


def _tile_rows(rows, cols, budget_bytes):
    best = None
    for t in range(16, rows + 1, 16):
        if rows % t == 0 and t * cols * 4 <= budget_bytes:
            best = t
    assert best is not None, (rows, cols)
    return best


N_DEV = 8


def sum_devices(name, landed, layer, n_layers, prev, c_arr):
    _, hr, cols = landed.shape
    tr = _tile_rows(hr, cols, 512 * 1024)
    n_blk = hr // tr

    def body(c_ref, r_ref, *rest):
        o_ref = rest[-1]
        acc = r_ref[0].astype(F32)
        for d in range(1, N_DEV):
            acc = acc + r_ref[d].astype(F32)
        o_ref[...] = acc

    in_specs = [pl.BlockSpec((N_DEV, tr, cols), lambda r, c_ref: (0, r, 0))]
    args = [c_arr, landed]
    aliases = {}
    if prev is not None:
        in_specs.append(pl.BlockSpec(memory_space=pl.ANY))
        args.append(prev)
        aliases = {2: 0}
    return pl.pallas_call(
        body, name=name, out_shape=jax.ShapeDtypeStruct((n_layers, 2 * hr, cols), F32),
        grid_spec=pltpu.PrefetchScalarGridSpec(
            num_scalar_prefetch=1, grid=(n_blk,), in_specs=in_specs,
            out_specs=pl.BlockSpec((None, tr, cols), lambda r, c_ref: (layer, c_ref[0] * n_blk + r, 0))),
        input_output_aliases=aliases, compiler_params=_params(("parallel",)),
    )(*args)


def _pack(arrs, rows_multiple):
    flat = []
    for a in arrs:
        v = a.reshape(-1).astype(F32)
        flat.append(jnp.pad(v, (0, (-v.shape[0]) % HEAD)))
    v = jnp.concatenate(flat)
    v = jnp.pad(v, (0, (-v.shape[0]) % (HEAD * rows_multiple)))
    return v.reshape(-1, HEAD)


def _unpack(packed, shapes):
    flat = packed.reshape(-1)
    out, off = [], 0
    for shp in shapes:
        size = int(np.prod(shp))
        out.append(flat[off:off + size].reshape(shp))
        off += size + (-size) % HEAD
    return out


BIG = ["even_w_in", "even_w_out", "mla_w_down", "mla_w_qb", "mla_w_kvb", "mla_w_o", "mlp_w1", "mlp_w2"]
BIG_KEY = {"even_w_in": "w_in", "even_w_out": "w_out", "mla_w_down": "w_down", "mla_w_qb": "w_qb",
           "mla_w_kvb": "w_kvb", "mla_w_o": "w_o", "mlp_w1": "w1", "mlp_w2": "w2"}
SMALL_KEY = {"ln_mix_g": "ln_mix_g", "ln_mix_b": "ln_mix_b", "ln_ffn_g": "ln_ffn_g", "ln_ffn_b": "ln_ffn_b",
             "pool_w": "pool_w", "pool_scale": "pool_scale", "lru_conv_w": "conv_w", "lru_conv_b": "conv_b",
             "lru_w_a": "w_a", "lru_b_a": "b_a", "lru_w_x": "w_x", "lru_b_x": "b_x", "lru_lambda": "lam",
             "mla_q_norm_g": "gq", "mla_kv_norm_g": "gkv"}
SMALL = list(SMALL_KEY)
SMALL_SHARDED = ["lru_conv_w", "mla_q_norm_g", "mla_kv_norm_g"]
WEIGHTS = ["ln_mix_g", "ln_mix_b", "ln_ffn_g", "ln_ffn_b", "even_w_in", "pool_w", "pool_scale", "lru_conv_w",
           "lru_conv_b", "lru_w_a", "lru_b_a", "lru_w_x", "lru_b_x", "lru_lambda", "even_w_out", "mla_w_down",
           "mla_q_norm_g", "mla_kv_norm_g", "mla_w_qb", "mla_w_kvb", "mla_w_o", "mlp_w1", "mlp_w2"]


GROUPS = ["mix0", "mlp0", "mix1", "mlp1", "mix2", "mlp2", "mix3", "mlp3"]


def _group_keys(group):
    layer = int(group[3:])
    if group.startswith("mlp"):
        return [("mlp_w1", "w1", layer), ("mlp_w2", "w2", layer)]
    if layer % 2 == 0:
        return [("even_w_in", "w_in", layer // 2), ("even_w_out", "w_out", layer // 2)]
    return [("mla_w_down", "w_down", layer // 2), ("mla_w_qb", "w_qb", layer // 2),
            ("mla_w_kvb", "w_kvb", layer // 2), ("mla_w_o", "w_o", layer // 2)]


def _pad_q_heads(w):
    lead = w.shape[:-1]
    w = w.reshape(lead + (2, QK_NOPE + QK_ROPE))
    w = jnp.pad(w, ((0, 0),) * len(lead) + ((0, 0), (0, QHEAD_PAD - QK_NOPE - QK_ROPE)))
    return w.reshape(lead + (2 * QHEAD_PAD,))


def _unpad_q_heads(g):
    lead = g.shape[:-1]
    return g.reshape(lead + (2, QHEAD_PAD))[..., :QK_NOPE + QK_ROPE].reshape(lead + (2 * (QK_NOPE + QK_ROPE),))


def _step(x, positions, loss_target, w, m, v):
    cx, cy, cc = lax.axis_index("x"), lax.axis_index("y"), lax.axis_index("c")
    chip = 2 * cx + cy
    c_arr = jnp.reshape(cc, (1,)).astype(jnp.int32)
    place = jnp.stack([chip, cc, 2 * chip + cc]).astype(jnp.int32)

    prepared = dict(w)
    prepared["mla_w_down"] = jnp.pad(w["mla_w_down"], ((0, 0), (0, 0), (0, ODD_IN_PAD - ODD_IN)))
    prepared["mla_w_qb"] = _pad_q_heads(w["mla_w_qb"])
    small_shard_shapes = [w[k].shape for k in SMALL_SHARDED]
    sources, zones = {}, {}
    for g in GROUPS:
        srcs = [prepared[name][idx].astype(BF16) for name, _, idx in _group_keys(g)]
        if g == GROUPS[0]:
            srcs.append(_pack([w[k] for k in SMALL_SHARDED], 32))
        sources[g] = srcs
        zones[g] = place_own("own_" + g, srcs, place, 4, False)
    gathering, token = {}, None
    for g in GROUPS:
        gathering[g], token = split_start("gather_" + g, sources[g], zones[g], 3, gather_plan, token)
    all_started = token

    def weights_of(g, after):
        lands = split_wait("gathered_" + g, [gathering[g]], all_started if g == GROUPS[0] else after)[0]
        out = {key: land for (_, key, _), land in zip(_group_keys(g), lands)}
        if g == GROUPS[0]:
            per_chip = [_unpack(lands[-1][s], small_shard_shapes) for s in range(N_CHIPS)]
            for i, key in enumerate(("conv_w", "gq", "gkv")):
                out[key] = jnp.concatenate([p[i] for p in per_chip], axis=-1)
        return out

    reducing = []

    def emit_grads(g, grads):
        srcs = [grads[key] for _, key, _ in _group_keys(g)]
        lands = place_own("mine_" + g, srcs, place, 4, True)
        started, token = split_start("reduce_" + g, srcs, lands, 7, reduce_plan)
        reducing.append((g, started))
        return token

    sm = {SMALL_KEY[k]: w[k] for k in SMALL if k not in SMALL_SHARDED}
    loss, grad_x, gs = _local_step(x[0], positions.reshape(S, 1), loss_target[0], sm, weights_of, emit_grads)
    loss = lax.psum(loss, ("x", "y", "c"))

    small_shapes = [gs[SMALL_KEY[k]].shape for k in SMALL]
    gs_pack = _pack([gs[SMALL_KEY[k]] for k in SMALL], 32)[None]
    lands = place_own("mine_small", [gs_pack], place, gs_pack.shape[1] // 2 // _tile_rows(gs_pack.shape[1] // 2, HEAD, 512 * 1024), True)
    reducing.append(("small", split_start("reduce_small", [gs_pack], lands, 7, reduce_plan)[0]))

    landed = split_wait("reduced", [st for _, st in reducing], grad_x)
    stacks = {}
    for (g, _), lands in zip(reducing, landed):
        if g == "small":
            stacks["small"] = sum_devices("sum_small", lands[0], 0, 1, None, c_arr)
            continue
        for (name, key, idx), land in zip(_group_keys(g), lands):
            stacks[name] = sum_devices("sum_%s%d" % (key, idx), land, idx, w[name].shape[0], stacks.get(name), c_arr)
    reduced = sibling_swap_halves([stacks[k] for k in BIG] + [stacks["small"]])
    g_big = dict(zip(BIG, reduced[:-1]))
    g_big["mla_w_down"] = g_big["mla_w_down"][..., :ODD_IN]
    g_big["mla_w_qb"] = _unpad_q_heads(g_big["mla_w_qb"])
    g_small = dict(zip(SMALL, _unpack(reduced[-1], small_shapes)))
    for k in SMALL_SHARDED:
        width = w[k].shape[-1]
        g_small[k] = lax.dynamic_slice_in_dim(g_small[k], chip * width, width, axis=-1)
    grad = {**g_big, **g_small}

    delta, new_m, new_v = {}, {}, {}
    for k in BIG:
        shp = w[k].shape
        view = lambda a: a.reshape(-1, shp[-1])
        d, nm, nv = adamw("adamw_" + BIG_KEY[k], view(w[k]), view(grad[k]), view(m[k]), view(v[k]))
        delta[k], new_m[k], new_v[k] = d.reshape(shp), nm.reshape(shp), nv.reshape(shp)
    shapes = [w[k].shape for k in SMALL]
    d, nm, nv = adamw("adamw_small", *[_pack([t[k] for k in SMALL], 512) for t in (w, grad, m, v)])
    for k, dk, mk, vk in zip(SMALL, _unpack(d, shapes), _unpack(nm, shapes), _unpack(nv, shapes)):
        delta[k], new_m[k], new_v[k] = dk, mk, vk
    return (loss, grad_x[None], *[grad[k] for k in WEIGHTS], *[delta[k] for k in WEIGHTS],
            *[new_m[k] for k in WEIGHTS], *[new_v[k] for k in WEIGHTS])


def kernel(x, positions, ln_mix_g, ln_mix_b, ln_ffn_g, ln_ffn_b, even_w_in, pool_w, pool_scale, lru_conv_w, lru_conv_b, lru_w_a, lru_b_a, lru_w_x, lru_b_x, lru_lambda, even_w_out, mla_w_down, mla_q_norm_g, mla_kv_norm_g, mla_w_qb, mla_w_kvb, mla_w_o, mlp_w1, mlp_w2, loss_target, m_ln_mix_g, m_ln_mix_b, m_ln_ffn_g, m_ln_ffn_b, m_even_w_in, m_pool_w, m_pool_scale, m_lru_conv_w, m_lru_conv_b, m_lru_w_a, m_lru_b_a, m_lru_w_x, m_lru_b_x, m_lru_lambda, m_even_w_out, m_mla_w_down, m_mla_q_norm_g, m_mla_kv_norm_g, m_mla_w_qb, m_mla_w_kvb, m_mla_w_o, m_mlp_w1, m_mlp_w2, v_ln_mix_g, v_ln_mix_b, v_ln_ffn_g, v_ln_ffn_b, v_even_w_in, v_pool_w, v_pool_scale, v_lru_conv_w, v_lru_conv_b, v_lru_w_a, v_lru_b_a, v_lru_w_x, v_lru_b_x, v_lru_lambda, v_even_w_out, v_mla_w_down, v_mla_q_norm_g, v_mla_kv_norm_g, v_mla_w_qb, v_mla_w_kvb, v_mla_w_o, v_mlp_w1, v_mlp_w2):
    args = locals()
    w = {k: args[k] for k in WEIGHTS}
    m = {k: args["m_" + k] for k in WEIGHTS}
    v = {k: args["v_" + k] for k in WEIGHTS}
    return _step(x, positions, loss_target, w, m, v)
```

```python
import functools
import math

import jax
import jax.numpy as jnp
import numpy as np
from jax import lax
from jax.experimental import pallas as pl
from jax.experimental.pallas import tpu as pltpu

F32 = jnp.float32
BF16 = jnp.bfloat16

S = 2048
D = 1024
DEPTH = 4
N_CHIPS = 4
POOL_WINDOWS = (2, 4, 8, 16)
POOL_W = 512
LRU_W = 1024
LRU_HEADS = 8
HEAD = 128
EVEN_IN = 2560
EVEN_MIX = 1536
MLA_HEADS = 8
QK_NOPE = 128
QK_ROPE = 64
Q_RANK = 384
KV_RANK = 256
ODD_IN = 704
ODD_IN_PAD = 768
QHEAD_PAD = 256
D_FF = 4096
CHUNK = 64
ALPHA = (2 * DEPTH) ** 0.25
LN_EPS = 1e-5
RMS_EPS = 1e-6
ATT_SCALE = (QK_NOPE + QK_ROPE) ** -0.5
ROPE_THETA = 10000.0
LRU_C = 8.0
ADAM_LR = 0.001
ADAM_B1 = 0.9
ADAM_B2 = 0.999
ADAM_EPS = 1e-08
ADAM_WD = 0.01
ADAM_STEP = 10

VMEM_LIMIT = 56 * 1024 * 1024
MESH = pl.DeviceIdType.MESH

_NN = (((1,), (0,)), ((), ()))
_NT = (((1,), (1,)), ((), ()))
_TN = (((0,), (0,)), ((), ()))


def _params(sem=None, **kw):
    return pltpu.CompilerParams(dimension_semantics=sem, vmem_limit_bytes=VMEM_LIMIT, **kw)


def _dot(a, b, dims):
    return lax.dot_general(a.astype(BF16), b.astype(BF16), dims, preferred_element_type=F32)


def _mm(name, a, w4, *, dims, a_block, a_map, w_map, grid, sem, outs, out_blocks, out_maps,
        red_axis=None, n_red=1, extras=(), extra_blocks=(), extra_maps=(), epilogue=None, after=None):
    n_extra = len(extras)
    n_out = len(outs)
    n_after = 0 if after is None else 1
    w_block = (None,) + tuple(w4.shape[1:])

    def body(a_ref, w_ref, *rest):
        rest = rest[n_after:]
        extra_refs = rest[:n_extra]
        out_refs = rest[n_extra:n_extra + n_out]

        def finish(acc):
            vals = epilogue(acc, *[r[...] for r in extra_refs]) if epilogue else (acc,)
            for r, v in zip(out_refs, vals):
                r[...] = v.astype(r.dtype)

        if red_axis is None:
            finish(_dot(a_ref[...], w_ref[...], dims))
        else:
            acc_ref = rest[-1]
            k = pl.program_id(red_axis)

            @pl.when(k == 0)
            def _():
                acc_ref[...] = jnp.zeros_like(acc_ref)

            acc_ref[...] += _dot(a_ref[...], w_ref[...], dims)

            @pl.when(k == n_red - 1)
            def _():
                finish(acc_ref[...])

    scratch = [] if red_axis is None else [pltpu.VMEM(out_blocks[0], F32)]
    return pl.pallas_call(
        body, name=name, grid=grid,
        in_specs=[pl.BlockSpec(a_block, a_map), pl.BlockSpec(w_block, w_map)]
        + [pl.BlockSpec(memory_space=pl.ANY)] * n_after + [pl.BlockSpec(b, m) for b, m in zip(extra_blocks, extra_maps)],
        out_specs=[pl.BlockSpec(b, m) for b, m in zip(out_blocks, out_maps)],
        out_shape=outs, scratch_shapes=scratch, compiler_params=_params(sem),
    )(a, w4, *([after] * n_after), *extras)


def _w_map_outer(s, i):
    return (s, 0, 0)


def _w_map_inner(i, s):
    return (s, 0, 0)


def mm_cols(name, a, w4, out_dtypes, *, tm=512, epilogue=None):
    m, k = a.shape
    nb = w4.shape[2]
    outs = [jax.ShapeDtypeStruct((m, N_CHIPS * nb), dt) for dt in out_dtypes]
    return _mm(name, a, w4, dims=_NN, a_block=(tm, k), a_map=lambda s, i: (i, 0), w_map=_w_map_outer,
               grid=(N_CHIPS, m // tm), sem=("parallel", "parallel"), outs=outs,
               out_blocks=[(tm, nb)] * len(outs), out_maps=[lambda s, i: (i, s)] * len(outs), epilogue=epilogue)


def mm_rows(name, a, w4, out_dtype, *, tm=512):
    m = a.shape[0]
    kb, n = w4.shape[1:]
    outs = [jax.ShapeDtypeStruct((m, n), out_dtype)]
    return _mm(name, a, w4, dims=_NN, a_block=(tm, kb), a_map=lambda i, s: (i, s), w_map=_w_map_inner,
               grid=(m // tm, N_CHIPS), sem=("parallel", "arbitrary"), outs=outs, out_blocks=[(tm, n)],
               out_maps=[lambda i, s: (i, 0)], red_axis=1, n_red=N_CHIPS)[0]


def mm_rows_ln(name, a, w4, x, g, b, *, tm=512):
    m = a.shape[0]
    kb, n = w4.shape[1:]
    outs = [jax.ShapeDtypeStruct((m, n), F32), jax.ShapeDtypeStruct((m, n), F32), jax.ShapeDtypeStruct((m, 1), F32)]
    return _mm(name, a, w4, dims=_NN, a_block=(tm, kb), a_map=lambda i, s: (i, s), w_map=_w_map_inner,
               grid=(m // tm, N_CHIPS), sem=("parallel", "arbitrary"), outs=outs, out_blocks=[(tm, n), (tm, n), (tm, 1)],
               out_maps=[lambda i, s: (i, 0)] * 3, red_axis=1, n_red=N_CHIPS, extras=(x, g, b),
               extra_blocks=[(tm, n), (1, n), (1, n)],
               extra_maps=[lambda i, s: (i, 0), lambda i, s: (0, 0), lambda i, s: (0, 0)],
               epilogue=lambda acc, xv, gv, bv: _layer_norm(ALPHA * xv + acc, gv, bv))


def mm_t_cols(name, g, w4, out_dtype, *, tm=512, resid=None, after=None):
    m = g.shape[0]
    k, nb = w4.shape[1:]
    outs = [jax.ShapeDtypeStruct((m, k), out_dtype)]
    extras = () if resid is None else (resid,)
    epi = None if resid is None else (lambda acc, r: (acc + ALPHA * r,))
    return _mm(name, g, w4, dims=_NT, a_block=(tm, nb), a_map=lambda i, s: (i, s), w_map=_w_map_inner,
               grid=(m // tm, N_CHIPS), sem=("parallel", "arbitrary"), outs=outs, out_blocks=[(tm, k)],
               out_maps=[lambda i, s: (i, 0)], red_axis=1, n_red=N_CHIPS, extras=extras,
               extra_blocks=[(tm, k)] * len(extras), extra_maps=[lambda i, s: (i, 0)] * len(extras), epilogue=epi,
               after=after)[0]


def mm_t_rows(name, g, w4, out_dtype, *, tm=512, resid=None, gate=None, after=None):
    m, n = g.shape
    kb = w4.shape[1]
    outs = [jax.ShapeDtypeStruct((m, N_CHIPS * kb), out_dtype)]
    extras, epi = (), None
    if resid is not None:
        extras, epi = (resid,), (lambda acc, r: (acc + ALPHA * r,))
    if gate is not None:
        extras, epi = (gate,), (lambda acc, r: (acc * (2.0 * r.astype(F32)),))
    return _mm(name, g, w4, dims=_NT, a_block=(tm, n), a_map=lambda s, i: (i, 0), w_map=_w_map_outer,
               grid=(N_CHIPS, m // tm), sem=("parallel", "parallel"), outs=outs, out_blocks=[(tm, kb)],
               out_maps=[lambda s, i: (i, s)], extras=extras, extra_blocks=[(tm, kb)] * len(extras),
               extra_maps=[lambda s, i: (i, s)] * len(extras), epilogue=epi, after=after)[0]


def mm_grad(name, a, g, *, a_sharded, tm=512):
    m = a.shape[0]
    ka = a.shape[1] // N_CHIPS if a_sharded else a.shape[1]
    ng = g.shape[1] if a_sharded else g.shape[1] // N_CHIPS
    n_red = m // tm
    a_map = (lambda s, i: (i, s)) if a_sharded else (lambda s, i: (i, 0))
    g_map = (lambda s, i: (i, 0)) if a_sharded else (lambda s, i: (i, s))

    def body(a_ref, g_ref, out_ref, acc_ref):
        k = pl.program_id(1)

        @pl.when(k == 0)
        def _():
            acc_ref[...] = jnp.zeros_like(acc_ref)

        acc_ref[...] += _dot(a_ref[...], g_ref[...], _TN)

        @pl.when(k == n_red - 1)
        def _():
            out_ref[...] = acc_ref[...].astype(out_ref.dtype)

    return pl.pallas_call(
        body, name=name, grid=(N_CHIPS, n_red),
        in_specs=[pl.BlockSpec((tm, ka), a_map), pl.BlockSpec((tm, ng), g_map)],
        out_specs=pl.BlockSpec((None, ka, ng), lambda s, i: (s, 0, 0)),
        out_shape=jax.ShapeDtypeStruct((N_CHIPS, ka, ng), BF16),
        scratch_shapes=[pltpu.VMEM((ka, ng), F32)], compiler_params=_params(("parallel", "arbitrary")),
    )(a, g)


def _layer_norm(z, g, b):
    mu = jnp.mean(z, axis=-1, keepdims=True)
    zc = z - mu
    rstd = lax.rsqrt(jnp.mean(zc * zc, axis=-1, keepdims=True) + LN_EPS)
    xhat = zc * rstd
    return xhat * g + b, xhat, rstd


def mlp_fwd(name, x, w1, w2, g, b, *, tm=512):
    fb = w1.shape[2]

    def body(x_ref, w1_ref, w2_ref, g_ref, b_ref, y_ref, xhat_ref, rstd_ref, relu_ref, acc_ref):
        s = pl.program_id(1)

        @pl.when(s == 0)
        def _():
            acc_ref[...] = jnp.zeros_like(acc_ref)

        r = jnp.maximum(_dot(x_ref[...], w1_ref[...], _NN), 0.0)
        relu_ref[...] = r.astype(relu_ref.dtype)
        acc_ref[...] += _dot(r * r, w2_ref[...], _NN)

        @pl.when(s == N_CHIPS - 1)
        def _():
            y_ref[...], xhat_ref[...], rstd_ref[...] = _layer_norm(ALPHA * x_ref[...] + acc_ref[...], g_ref[...], b_ref[...])

    row = pl.BlockSpec((tm, D), lambda i, s: (i, 0))
    vec = pl.BlockSpec((1, D), lambda i, s: (0, 0))
    return pl.pallas_call(
        body, name=name, grid=(S // tm, N_CHIPS),
        in_specs=[row, pl.BlockSpec((None, D, fb), lambda i, s: (s, 0, 0)),
                  pl.BlockSpec((None, fb, D), lambda i, s: (s, 0, 0)), vec, vec],
        out_specs=[row, row, pl.BlockSpec((tm, 1), lambda i, s: (i, 0)), pl.BlockSpec((tm, fb), lambda i, s: (i, s))],
        out_shape=[jax.ShapeDtypeStruct((S, D), F32), jax.ShapeDtypeStruct((S, D), F32),
                   jax.ShapeDtypeStruct((S, 1), F32), jax.ShapeDtypeStruct((S, N_CHIPS * fb), BF16)],
        scratch_shapes=[pltpu.VMEM((tm, D), F32)], compiler_params=_params(("parallel", "arbitrary")),
    )(x, w1, w2, g, b)


def mlp_bwd(name, dz, relu, x, w1, w2, *, tm=256):
    fb = w1.shape[2]
    n_i = S // tm

    def body(dz_ref, relu_ref, x_ref, w1_ref, w2_ref, dx_ref, g1_ref, g2_ref, acc1_ref, acc2_ref):
        s, i = pl.program_id(0), pl.program_id(1)
        rows = pl.ds(pl.multiple_of(i * tm, tm), tm)
        dz_v = dz_ref[...]

        @pl.when(i == 0)
        def _():
            acc1_ref[...] = jnp.zeros_like(acc1_ref)
            acc2_ref[...] = jnp.zeros_like(acc2_ref)

        @pl.when(s == 0)
        def _():
            dx_ref[rows, :] = ALPHA * dz_v

        dz_b = dz_v.astype(BF16)
        r = relu_ref[...]
        dh = (_dot(dz_b, w2_ref[...], _NT) * (2.0 * r.astype(F32))).astype(BF16)
        p2 = _dot(r * r, dz_b, _TN)
        p1 = _dot(x_ref[...], dh, _TN)
        dx_ref[rows, :] += _dot(dh, w1_ref[...], _NT)
        acc1_ref[...] += p1
        acc2_ref[...] += p2

        @pl.when(i == n_i - 1)
        def _():
            g1_ref[...] = acc1_ref[...].astype(g1_ref.dtype)
            g2_ref[...] = acc2_ref[...].astype(g2_ref.dtype)

    row = pl.BlockSpec((tm, D), lambda s, i: (i, 0))
    return pl.pallas_call(
        body, name=name, grid=(N_CHIPS, n_i),
        in_specs=[row, pl.BlockSpec((tm, fb), lambda s, i: (i, s)), row,
                  pl.BlockSpec((None, D, fb), lambda s, i: (s, 0, 0)), pl.BlockSpec((None, fb, D), lambda s, i: (s, 0, 0))],
        out_specs=[pl.BlockSpec((S, D), lambda s, i: (0, 0)), pl.BlockSpec((None, D, fb), lambda s, i: (s, 0, 0)),
                   pl.BlockSpec((None, fb, D), lambda s, i: (s, 0, 0))],
        out_shape=[jax.ShapeDtypeStruct((S, D), F32), jax.ShapeDtypeStruct((N_CHIPS, D, fb), BF16),
                   jax.ShapeDtypeStruct((N_CHIPS, fb, D), BF16)],
        scratch_shapes=[pltpu.VMEM((D, fb), F32), pltpu.VMEM((fb, D), F32)],
        compiler_params=_params(("arbitrary", "arbitrary")),
    )(dz, relu, x, w1, w2)


def ln_bwd(name, dy, xhat, rstd, g, *, tm=256, after=None):
    n_after = 0 if after is None else 1

    def body(dy_ref, xhat_ref, rstd_ref, g_ref, *rest):
        dz_ref, dg_ref, db_ref = rest[n_after:]
        dy_v = dy_ref[...]
        xh = xhat_ref[...]
        dxh = dy_v * g_ref[...]
        m1 = jnp.mean(dxh, axis=-1, keepdims=True)
        m2 = jnp.mean(dxh * xh, axis=-1, keepdims=True)
        dz_ref[...] = rstd_ref[...] * (dxh - m1 - xh * m2)
        pg = jnp.sum(dy_v * xh, axis=0, keepdims=True)
        pb = jnp.sum(dy_v, axis=0, keepdims=True)
        i = pl.program_id(0)

        @pl.when(i == 0)
        def _():
            dg_ref[...] = pg
            db_ref[...] = pb

        @pl.when(i > 0)
        def _():
            dg_ref[...] += pg
            db_ref[...] += pb

    row = pl.BlockSpec((tm, D), lambda i: (i, 0))
    vec = pl.BlockSpec((1, D), lambda i: (0, 0))
    return pl.pallas_call(
        body, name=name, grid=(S // tm,),
        in_specs=[row, row, pl.BlockSpec((tm, 1), lambda i: (i, 0)), vec] + [pl.BlockSpec(memory_space=pl.ANY)] * n_after,
        out_specs=[row, vec, vec],
        out_shape=[jax.ShapeDtypeStruct((S, D), F32), jax.ShapeDtypeStruct((1, D), F32),
                   jax.ShapeDtypeStruct((1, D), F32)],
        compiler_params=_params(("arbitrary",)),
    )(dy, xhat, rstd, g, *([after] * n_after))


def loss_head(y, target, *, tm=256):
    def body(y_ref, t_ref, dy_ref, loss_ref):
        e = y_ref[...] - t_ref[...]
        dy_ref[...] = e * (1.0 / D)
        part = jnp.sum(jnp.sum(e * e, axis=-1, keepdims=True), axis=0, keepdims=True) * (0.5 / D)
        i = pl.program_id(0)

        @pl.when(i == 0)
        def _():
            loss_ref[...] = jnp.zeros_like(loss_ref)

        loss_ref[...] += jnp.broadcast_to(part, loss_ref.shape)

    row = pl.BlockSpec((tm, D), lambda i: (i, 0))
    return pl.pallas_call(
        body, name="loss_head", grid=(S // tm,), in_specs=[row, row],
        out_specs=[row, pl.BlockSpec((8, 128), lambda i: (0, 0))],
        out_shape=[jax.ShapeDtypeStruct((S, D), F32), jax.ShapeDtypeStruct((8, 128), F32)],
        compiler_params=_params(("arbitrary",)),
    )(y, target)


def _rows(shape):
    return lax.broadcasted_iota(jnp.int32, shape, 0)


def _shift_down(x, k):
    return jnp.where(_rows(x.shape) >= k, pltpu.roll(x, k, 0), 0.0)


def _shift_up(x, k):
    n = x.shape[0]
    return jnp.where(_rows(x.shape) < n - k, pltpu.roll(x, n - k, 0), 0.0)


def _pool_diff(u, w):
    acc, k = u, 1
    while k < w:
        acc = acc + _shift_down(acc, k)
        k *= 2
    cnt = jnp.minimum(_rows(u.shape) + 1, w).astype(F32)
    return acc / cnt - u, cnt


def pool_fwd(name, proj, pool_w, pool_scale):
    def body(u_ref, w_ref, sc_ref, y_ref):
        for g, w in enumerate(POOL_WINDOWS):
            cols = slice(g * HEAD, (g + 1) * HEAD)
            d, _ = _pool_diff(u_ref[:, cols], w)
            z = _dot(d, w_ref[g], _NN)
            y_ref[:, cols] = (z * sc_ref[:, cols]).astype(y_ref.dtype)

    return pl.pallas_call(
        body, name=name, grid=(1,),
        in_specs=[pl.BlockSpec((S, POOL_W), lambda i: (0, 0)),
                  pl.BlockSpec((4, HEAD, HEAD), lambda i: (0, 0, 0)),
                  pl.BlockSpec((1, POOL_W), lambda i: (0, 0))],
        out_specs=pl.BlockSpec((S, POOL_W), lambda i: (0, 0)),
        out_shape=jax.ShapeDtypeStruct((S, POOL_W), BF16),
        compiler_params=_params(("arbitrary",)),
    )(proj, pool_w, pool_scale)


def pool_bwd(name, proj, dycat, pool_w, pool_scale):
    def body(u_ref, dy_ref, w_ref, sc_ref, du_ref, dw_ref, dsc_ref):
        for g, w in enumerate(POOL_WINDOWS):
            cols = slice(g * HEAD, (g + 1) * HEAD)
            d, cnt = _pool_diff(u_ref[:, cols], w)
            dy = dy_ref[:, cols]
            z = _dot(d, w_ref[g], _NN)
            dsc_ref[:, cols] = jnp.sum(dy * z, axis=0, keepdims=True)
            dz = dy * sc_ref[:, cols]
            dw_ref[g] = _dot(d, dz, _TN)
            dd = _dot(dz, w_ref[g], _NT)
            acc, k = dd / cnt, 1
            while k < w:
                acc = acc + _shift_up(acc, k)
                k *= 2
            du_ref[:, cols] = (acc - dd).astype(du_ref.dtype)

    return pl.pallas_call(
        body, name=name, grid=(1,),
        in_specs=[pl.BlockSpec((S, POOL_W), lambda i: (0, 0)),
                  pl.BlockSpec((S, POOL_W), lambda i: (0, 0)),
                  pl.BlockSpec((4, HEAD, HEAD), lambda i: (0, 0, 0)),
                  pl.BlockSpec((1, POOL_W), lambda i: (0, 0))],
        out_specs=[pl.BlockSpec((S, POOL_W), lambda i: (0, 0)),
                   pl.BlockSpec((4, HEAD, HEAD), lambda i: (0, 0, 0)),
                   pl.BlockSpec((1, POOL_W), lambda i: (0, 0))],
        out_shape=[jax.ShapeDtypeStruct((S, POOL_W), BF16), jax.ShapeDtypeStruct((4, HEAD, HEAD), F32),
                   jax.ShapeDtypeStruct((1, POOL_W), F32)],
        compiler_params=_params(("arbitrary",)),
    )(proj, dycat, pool_w, pool_scale)


def _expm1(x):
    series = x * (1.0 + x * (0.5 + x * (1.0 / 6.0 + x * (1.0 / 24.0 + x * (1.0 / 120.0)))))
    return jnp.where(jnp.abs(x) < 0.05, series, jnp.exp(x) - 1.0)


def _softplus_neg(lam):
    e = jnp.exp(-jnp.abs(lam))
    log1p = jnp.where(e < 0.01, e * (1.0 - e * (0.5 - e * (1.0 / 3.0))), jnp.log(1.0 + e))
    return jnp.maximum(-lam, 0.0) + log1p


_GELU_C = math.sqrt(2.0 / math.pi)


def _gelu(x):
    t = jnp.tanh(_GELU_C * (x + 0.044715 * x * x * x))
    return 0.5 * x * (1.0 + t), t


def _gelu_grad(x, t):
    return 0.5 * (1.0 + t) + 0.5 * x * (1.0 - t * t) * _GELU_C * (1.0 + 3.0 * 0.044715 * x * x)


def _conv(u, cw, cb):
    return cw[3:4] * u + cw[2:3] * _shift_down(u, 1) + cw[1:2] * _shift_down(u, 2) + cw[0:1] * _shift_down(u, 3) + cb


def _lru_gates(cu, wa, ba, wx, bx, lam):
    r = jax.nn.sigmoid(_dot(cu, wa, _NN) + ba)
    i = jax.nn.sigmoid(_dot(cu, wx, _NN) + bx)
    sp = _softplus_neg(lam)
    log_a = (-LRU_C) * r * sp
    a = jnp.exp(log_a)
    mult = jnp.sqrt(-_expm1(2.0 * log_a))
    return r, i, sp, a, mult


def _scan(a_ref, b_ref, h_ref, *, reverse):
    n_blk = S // 8
    row8 = lax.broadcasted_iota(jnp.int32, (8, HEAD), 0)

    def step(j, carry):
        blk = (n_blk - 1 - j) if reverse else j
        r0 = pl.multiple_of(blk * 8, 8)
        a = a_ref[pl.ds(r0, 8), :]
        b = b_ref[pl.ds(r0, 8), :]
        for k in (1, 2, 4):
            if reverse:
                keep = row8 < 8 - k
                a_s, b_s = pltpu.roll(a, 8 - k, 0), pltpu.roll(b, 8 - k, 0)
            else:
                keep = row8 >= k
                a_s, b_s = pltpu.roll(a, k, 0), pltpu.roll(b, k, 0)
            b = jnp.where(keep, a * b_s + b, b)
            a = jnp.where(keep, a * a_s, a)
        h = b + a * carry
        h_ref[pl.ds(r0, 8), :] = h
        edge = h[0:1, :] if reverse else h[7:8, :]
        return jnp.broadcast_to(edge, (8, HEAD))

    lax.fori_loop(0, n_blk, step, jnp.zeros((8, HEAD), F32), unroll=4)


def _lru_specs():
    def col(off):
        return pl.BlockSpec((S, HEAD), lambda h: (0, off + h))
    vec = pl.BlockSpec((1, HEAD), lambda h: (0, h))
    mat = pl.BlockSpec((None, HEAD, HEAD), lambda h: (h, 0, 0))
    cw = pl.BlockSpec((4, HEAD), lambda h: (0, h))
    return col, vec, mat, cw


def lru_fwd(name, proj, conv_w, conv_b, w_a, b_a, w_x, b_x, lam):
    def body(u_ref, ug_ref, cw_ref, cb_ref, wa_ref, ba_ref, wx_ref, bx_ref, lam_ref, y_ref, h_ref, a_s, b_s):
        cu = _conv(u_ref[...], cw_ref[...], cb_ref[...])
        _, i, _, a, mult = _lru_gates(cu, wa_ref[...], ba_ref[...], wx_ref[...], bx_ref[...], lam_ref[...])
        a_s[...] = a
        b_s[...] = mult * (i * cu)
        _scan(a_s, b_s, h_ref, reverse=False)
        gl, _ = _gelu(ug_ref[...])
        y_ref[...] = (h_ref[...] * gl).astype(y_ref.dtype)

    col, vec, mat, cw = _lru_specs()
    out = pl.BlockSpec((S, HEAD), lambda h: (0, h))
    return pl.pallas_call(
        body, name=name, grid=(LRU_HEADS,),
        in_specs=[col(4), col(12), cw, vec, mat, vec, mat, vec, vec],
        out_specs=[out, out],
        out_shape=[jax.ShapeDtypeStruct((S, LRU_W), BF16), jax.ShapeDtypeStruct((S, LRU_W), F32)],
        scratch_shapes=[pltpu.VMEM((S, HEAD), F32), pltpu.VMEM((S, HEAD), F32)],
        compiler_params=_params(("parallel",)),
    )(proj, proj, conv_w, conv_b, w_a, b_a, w_x, b_x, lam)


def lru_bwd(name, proj, hstate, dycat, conv_w, conv_b, w_a, b_a, w_x, b_x, lam):
    def body(u_ref, ug_ref, h_ref, dy_ref, cw_ref, cb_ref, wa_ref, ba_ref, wx_ref, bx_ref, lam_ref,
             du_ref, dug_ref, dwa_ref, dwx_ref, dba_ref, dbx_ref, dlam_ref, dcw_ref, dcb_ref, a_s, b_s, g_s):
        u = u_ref[...]
        cw = cw_ref[...]
        cu = _conv(u, cw, cb_ref[...])
        lam_v = lam_ref[...]
        r, i, sp, a, mult = _lru_gates(cu, wa_ref[...], ba_ref[...], wx_ref[...], bx_ref[...], lam_v)
        ug = ug_ref[...]
        gl, t = _gelu(ug)
        dy = dy_ref[...]
        h = h_ref[...]
        dug_ref[...] = (dy * h * _gelu_grad(ug, t)).astype(dug_ref.dtype)
        a_s[...] = _shift_up(a, 1)
        b_s[...] = dy * gl
        _scan(a_s, b_s, g_s, reverse=True)
        dxin = g_s[...]
        da = dxin * _shift_down(h, 1)
        dmult = dxin * (i * cu)
        di = dxin * (mult * cu)
        dlog_a = da * a - dmult * (a * a) / mult
        dr_pre = dlog_a * ((-LRU_C) * sp) * (r * (1.0 - r))
        di_pre = di * (i * (1.0 - i))
        dsp = jnp.sum(dlog_a * ((-LRU_C) * r), axis=0, keepdims=True)
        dlam_ref[...] = dsp * (-jax.nn.sigmoid(-lam_v))
        dba_ref[...] = jnp.sum(dr_pre, axis=0, keepdims=True)
        dbx_ref[...] = jnp.sum(di_pre, axis=0, keepdims=True)
        dwa_ref[...] = _dot(cu, dr_pre, _TN)
        dwx_ref[...] = _dot(cu, di_pre, _TN)
        dcu = dxin * (mult * i) + _dot(dr_pre, wa_ref[...], _NT) + _dot(di_pre, wx_ref[...], _NT)
        dcb_ref[...] = jnp.sum(dcu, axis=0, keepdims=True)
        for k in range(4):
            dcw_ref[k:k + 1, :] = jnp.sum(dcu * (_shift_down(u, 3 - k) if k < 3 else u), axis=0, keepdims=True)
        du = cw[3:4] * dcu + cw[2:3] * _shift_up(dcu, 1) + cw[1:2] * _shift_up(dcu, 2) + cw[0:1] * _shift_up(dcu, 3)
        du_ref[...] = du.astype(du_ref.dtype)

    col, vec, mat, cw = _lru_specs()
    out = pl.BlockSpec((S, HEAD), lambda h: (0, h))
    big = jax.ShapeDtypeStruct((S, LRU_W), BF16)
    vec_shape = jax.ShapeDtypeStruct((1, LRU_W), F32)
    mat_shape = jax.ShapeDtypeStruct((LRU_HEADS, HEAD, HEAD), F32)
    return pl.pallas_call(
        body, name=name, grid=(LRU_HEADS,),
        in_specs=[col(4), col(12), out, col(4), cw, vec, mat, vec, mat, vec, vec],
        out_specs=[out, out, mat, mat, vec, vec, vec, cw, vec],
        out_shape=[big, big, mat_shape, mat_shape, vec_shape, vec_shape, vec_shape,
                   jax.ShapeDtypeStruct((4, LRU_W), F32), vec_shape],
        scratch_shapes=[pltpu.VMEM((S, HEAD), F32)] * 3,
        compiler_params=_params(("parallel",)),
    )(proj, proj, hstate, dycat, conv_w, conv_b, w_a, b_a, w_x, b_x, lam)


def rope_tables(pos_col, inv_freq):
    def body(pos_ref, f_ref, c_ref, s1_ref, s2_ref):
        ang = pos_ref[...].astype(F32) * f_ref[...]
        lane = lax.broadcasted_iota(jnp.int32, ang.shape, 1)
        cos, sin = jnp.cos(ang), jnp.sin(ang)
        c_ref[...] = jnp.where(lane < QK_ROPE, cos, 0.0)
        s1_ref[...] = jnp.where(lane < QK_ROPE // 2, -sin, 0.0)
        s2_ref[...] = jnp.where((lane >= QK_ROPE // 2) & (lane < QK_ROPE), sin, 0.0)

    tab = jax.ShapeDtypeStruct((S, HEAD), F32)
    return pl.pallas_call(
        body, name="rope_tables", grid=(1,),
        in_specs=[pl.BlockSpec((S, 1), lambda i: (0, 0)), pl.BlockSpec((1, HEAD), lambda i: (0, 0))],
        out_specs=[pl.BlockSpec((S, HEAD), lambda i: (0, 0))] * 3, out_shape=[tab, tab, tab],
        compiler_params=_params(("arbitrary",)),
    )(pos_col, inv_freq)


def _rope(v, c, s1, s2):
    return v * c + pltpu.roll(v, HEAD - QK_ROPE // 2, 1) * s1 + pltpu.roll(v, QK_ROPE // 2, 1) * s2


def _unrope(d, c, s1, s2):
    return d * c + pltpu.roll(d * s1, QK_ROPE // 2, 1) + pltpu.roll(d * s2, HEAD - QK_ROPE // 2, 1)


def _rms(x, g):
    rstd = lax.rsqrt(jnp.mean(x * x, axis=-1, keepdims=True) + RMS_EPS)
    return x * rstd, rstd


def mla_prep(name, down, gq, gkv, tabs, *, tm=256):
    def body(dn_ref, gq_ref, gkv_ref, c_ref, s1_ref, s2_ref, cq_ref, ckv_ref, kp_ref):
        xq, _ = _rms(dn_ref[:, :Q_RANK], None)
        cq_ref[...] = (xq * gq_ref[...]).astype(cq_ref.dtype)
        xkv, _ = _rms(dn_ref[:, Q_RANK:Q_RANK + KV_RANK], None)
        ckv_ref[...] = (xkv * gkv_ref[...]).astype(ckv_ref.dtype)
        kp = _rope(dn_ref[:, Q_RANK + KV_RANK:], c_ref[...], s1_ref[...], s2_ref[...])
        kp_ref[...] = kp.astype(kp_ref.dtype)

    tab = pl.BlockSpec((tm, HEAD), lambda i: (i, 0))
    return pl.pallas_call(
        body, name=name, grid=(S // tm,),
        in_specs=[pl.BlockSpec((tm, ODD_IN_PAD), lambda i: (i, 0)), pl.BlockSpec((1, Q_RANK), lambda i: (0, 0)),
                  pl.BlockSpec((1, KV_RANK), lambda i: (0, 0)), tab, tab, tab],
        out_specs=[pl.BlockSpec((tm, Q_RANK), lambda i: (i, 0)), pl.BlockSpec((tm, KV_RANK), lambda i: (i, 0)), tab],
        out_shape=[jax.ShapeDtypeStruct((S, Q_RANK), BF16), jax.ShapeDtypeStruct((S, KV_RANK), BF16),
                   jax.ShapeDtypeStruct((S, HEAD), BF16)],
        compiler_params=_params(("parallel",)),
    )(down, gq, gkv, *tabs)


def mla_prep_bwd(name, down, dcq, dckv, dkp, gq, gkv, tabs, *, tm=256):
    def body(dn_ref, dcq_ref, dckv_ref, dkp_ref, gq_ref, gkv_ref, c_ref, s1_ref, s2_ref, dd_ref, dgq_ref, dgkv_ref):
        i = pl.program_id(0)

        def rms_bwd(x, dy, g, dg_ref):
            xh, rstd = _rms(x, None)
            dxh = dy * g
            dx = rstd * (dxh - xh * jnp.mean(dxh * xh, axis=-1, keepdims=True))
            pg = jnp.sum(dy * xh, axis=0, keepdims=True)

            @pl.when(i == 0)
            def _():
                dg_ref[...] = pg

            @pl.when(i > 0)
            def _():
                dg_ref[...] += pg

            return dx

        dxq = rms_bwd(dn_ref[:, :Q_RANK], dcq_ref[...], gq_ref[...], dgq_ref)
        dd_ref[:, :Q_RANK] = dxq.astype(dd_ref.dtype)
        dxkv = rms_bwd(dn_ref[:, Q_RANK:Q_RANK + KV_RANK], dckv_ref[...], gkv_ref[...], dgkv_ref)
        dd_ref[:, Q_RANK:Q_RANK + KV_RANK] = dxkv.astype(dd_ref.dtype)
        dd_ref[:, Q_RANK + KV_RANK:] = _unrope(dkp_ref[...], c_ref[...], s1_ref[...], s2_ref[...]).astype(dd_ref.dtype)

    tab = pl.BlockSpec((tm, HEAD), lambda i: (i, 0))
    vq = pl.BlockSpec((1, Q_RANK), lambda i: (0, 0))
    vkv = pl.BlockSpec((1, KV_RANK), lambda i: (0, 0))
    return pl.pallas_call(
        body, name=name, grid=(S // tm,),
        in_specs=[pl.BlockSpec((tm, ODD_IN_PAD), lambda i: (i, 0)), pl.BlockSpec((tm, Q_RANK), lambda i: (i, 0)),
                  pl.BlockSpec((tm, KV_RANK), lambda i: (i, 0)), tab, vq, vkv, tab, tab, tab],
        out_specs=[pl.BlockSpec((tm, ODD_IN_PAD), lambda i: (i, 0)), vq, vkv],
        out_shape=[jax.ShapeDtypeStruct((S, ODD_IN_PAD), BF16), jax.ShapeDtypeStruct((1, Q_RANK), F32),
                   jax.ShapeDtypeStruct((1, KV_RANK), F32)],
        compiler_params=_params(("arbitrary",)),
    )(down, dcq, dckv, dkp, gq, gkv, *tabs)


ATT_TQ = 256


def _attn_probs(q_ref, kv_ref, kp_ref, c_ref, s1_ref, s2_ref, i, nk):
    qn = q_ref[:, :HEAD].astype(BF16)
    qp = _rope(q_ref[:, HEAD:], c_ref[...], s1_ref[...], s2_ref[...]).astype(BF16)
    kn = kv_ref[:nk, :HEAD]
    sc = (_dot(qn, kn, _NT) + _dot(qp, kp_ref[:nk, :], _NT)) * ATT_SCALE
    q_chunk = (i * ATT_TQ + lax.broadcasted_iota(jnp.int32, sc.shape, 0)) // CHUNK
    k_chunk = lax.broadcasted_iota(jnp.int32, sc.shape, 1) // CHUNK
    sc = jnp.where(k_chunk <= q_chunk, sc, jnp.finfo(F32).min)
    e = jnp.exp(sc - jnp.max(sc, axis=-1, keepdims=True))
    p = e * (1.0 / jnp.sum(e, axis=-1, keepdims=True))
    return p, qn, qp, kn


def _for_each_prefix(i, fn):
    for k in range(S // ATT_TQ):
        pl.when(i == k)(functools.partial(fn, (k + 1) * ATT_TQ))


def _attn_specs():
    q = pl.BlockSpec((ATT_TQ, QHEAD_PAD), lambda h, i: (i, h))
    kv = pl.BlockSpec((S, QHEAD_PAD), lambda h, i: (0, h))
    kp = pl.BlockSpec((S, HEAD), lambda h, i: (0, 0))
    tab = pl.BlockSpec((ATT_TQ, HEAD), lambda h, i: (i, 0))
    o = pl.BlockSpec((ATT_TQ, HEAD), lambda h, i: (i, h))
    return q, kv, kp, tab, o


def attn_fwd(name, q, kv, kp, tabs):
    def body(q_ref, kv_ref, kp_ref, c_ref, s1_ref, s2_ref, o_ref):
        i = pl.program_id(1)

        def run(nk):
            p, _, _, _ = _attn_probs(q_ref, kv_ref, kp_ref, c_ref, s1_ref, s2_ref, i, nk)
            o_ref[...] = _dot(p, kv_ref[:nk, HEAD:], _NN).astype(o_ref.dtype)

        _for_each_prefix(i, run)

    qs, kvs, kps, tab, os = _attn_specs()
    return pl.pallas_call(
        body, name=name, grid=(MLA_HEADS, S // ATT_TQ), in_specs=[qs, kvs, kps, tab, tab, tab], out_specs=os,
        out_shape=jax.ShapeDtypeStruct((S, MLA_HEADS * HEAD), BF16),
        compiler_params=_params(("parallel", "parallel")),
    )(q, kv, kp, *tabs)


def attn_bwd(name, q, kv, kp, do, tabs):
    def body(q_ref, kv_ref, kp_ref, do_ref, c_ref, s1_ref, s2_ref, dq_ref, dkv_ref, dkp_ref):
        h, i = pl.program_id(0), pl.program_id(1)

        @pl.when(i == 0)
        def _():
            dkv_ref[...] = jnp.zeros_like(dkv_ref)

        @pl.when((i == 0) & (h == 0))
        def _():
            dkp_ref[...] = jnp.zeros_like(dkp_ref)

        def run(nk):
            p, qn, qp, kn = _attn_probs(q_ref, kv_ref, kp_ref, c_ref, s1_ref, s2_ref, i, nk)
            do_v = do_ref[...]
            dp = _dot(do_v, kv_ref[:nk, HEAD:], _NT)
            ds = (p * (dp - jnp.sum(p * dp, axis=-1, keepdims=True)) * ATT_SCALE).astype(BF16)
            dq_ref[:, :HEAD] = _dot(ds, kn, _NN).astype(dq_ref.dtype)
            dqp = _unrope(_dot(ds, kp_ref[:nk, :], _NN), c_ref[...], s1_ref[...], s2_ref[...])
            dq_ref[:, HEAD:] = dqp.astype(dq_ref.dtype)
            dkv_ref[:nk, :HEAD] += _dot(ds, qn, _TN)
            dkv_ref[:nk, HEAD:] += _dot(p, do_v, _TN)
            dkp_ref[:nk, :] += _dot(ds, qp, _TN)

        _for_each_prefix(i, run)

    qs, kvs, kps, tab, os = _attn_specs()
    return pl.pallas_call(
        body, name=name, grid=(MLA_HEADS, S // ATT_TQ), in_specs=[qs, kvs, kps, os, tab, tab, tab],
        out_specs=[qs, kvs, kps],
        out_shape=[jax.ShapeDtypeStruct((S, MLA_HEADS * QHEAD_PAD), BF16),
                   jax.ShapeDtypeStruct((S, MLA_HEADS * QHEAD_PAD), F32), jax.ShapeDtypeStruct((S, HEAD), F32)],
        compiler_params=_params(("arbitrary", "arbitrary")),
    )(q, kv, kp, do, *tabs)


def adamw(name, w, g, m, v):
    rows, cols = w.shape
    tr = rows
    for cand in (512, 256, 128, 64, 32, 16, 8):
        if rows % cand == 0 and cand * cols * 4 <= 2 * 1024 * 1024:
            tr = cand
            break

    def body(w_ref, g_ref, m_ref, v_ref, d_ref, nm_ref, nv_ref):
        g_v = g_ref[...]
        nm = ADAM_B1 * m_ref[...] + (1.0 - ADAM_B1) * g_v
        nv = ADAM_B2 * v_ref[...] + (1.0 - ADAM_B2) * (g_v * g_v)
        m_hat = nm / (1.0 - ADAM_B1 ** ADAM_STEP)
        v_hat = nv / (1.0 - ADAM_B2 ** ADAM_STEP)
        d_ref[...] = (-ADAM_LR) * (m_hat / (jnp.sqrt(v_hat) + ADAM_EPS) + ADAM_WD * w_ref[...])
        nm_ref[...] = nm
        nv_ref[...] = nv

    blk = pl.BlockSpec((tr, cols), lambda i: (i, 0))
    shape = jax.ShapeDtypeStruct((rows, cols), F32)
    return pl.pallas_call(
        body, name=name, grid=(rows // tr,), in_specs=[blk] * 4, out_specs=[blk] * 3, out_shape=[shape] * 3,
        compiler_params=_params(("parallel",)),
    )(w, g, m, v)


def _local_step(x, pos_col, target, sm, weights_of, emit_grads):
    inv_freq = ROPE_THETA ** (-jnp.arange(0, QK_ROPE, 2, dtype=F32) / QK_ROPE)
    inv_freq = jnp.concatenate([inv_freq, inv_freq, jnp.zeros((HEAD - QK_ROPE,), F32)])[None, :]
    tabs = rope_tables(pos_col, inv_freq)
    saved, wts = [], {}
    for layer in range(DEPTH):
        j = layer // 2
        n = "l%d_" % layer
        sv = {"x": x}
        wm = wts["mix%d" % layer] = weights_of("mix%d" % layer, x)
        if layer == 0:
            sm = dict(sm, conv_w=wm["conv_w"], gq=wm["gq"], gkv=wm["gkv"])
        if layer % 2 == 0:
            proj = mm_cols(n + "proj", x, wm["w_in"], [F32])[0]
            y_pool = pool_fwd(n + "pool", proj, sm["pool_w"][j], sm["pool_scale"][j][None])
            y_lru, hstate = lru_fwd(n + "lru", proj, sm["conv_w"][j], sm["conv_b"][j][None], sm["w_a"][j],
                                    sm["b_a"][j][None], sm["w_x"][j], sm["b_x"][j][None], sm["lam"][j][None])
            ycat = jnp.concatenate([y_pool, y_lru], axis=1)
            mix_in, w_mix = ycat, wm["w_out"]
            sv.update(proj=proj, hstate=hstate, ycat=ycat)
        else:
            down = mm_rows(n + "down", x, wm["w_down"], F32)
            cq, ckv, kp = mla_prep(n + "prep", down, sm["gq"][j][None], sm["gkv"][j][None], tabs)
            q = mm_cols(n + "q", cq, wm["w_qb"], [F32])[0]
            kv = mm_cols(n + "kv", ckv, wm["w_kvb"], [BF16])[0]
            o = attn_fwd(n + "attn", q, kv, kp, tabs)
            mix_in, w_mix = o, wm["w_o"]
            sv.update(down=down, cq=cq, ckv=ckv, kp=kp, q=q, kv=kv, o=o)
        x1, xhat1, rstd1 = mm_rows_ln(n + "mixout", mix_in, w_mix, x, sm["ln_mix_g"][layer][None],
                                      sm["ln_mix_b"][layer][None])
        wf = wts["mlp%d" % layer] = weights_of("mlp%d" % layer, x1)
        x2, xhat2, rstd2, relu = mlp_fwd(n + "mlp", x1, wf["w1"], wf["w2"], sm["ln_ffn_g"][layer][None],
                                         sm["ln_ffn_b"][layer][None])
        sv.update(xhat1=xhat1, rstd1=rstd1, x1=x1, relu=relu, xhat2=xhat2, rstd2=rstd2)
        saved.append(sv)
        x = x2

    dx, loss_tile = loss_head(x, target)
    gs = {k: [None] * (DEPTH if k.startswith("ln_") else DEPTH // 2) for k in sm}
    for layer in reversed(range(DEPTH)):
        j = layer // 2
        n = "l%d_" % layer
        sv = saved[layer]
        wm, wf = wts["mix%d" % layer], wts["mlp%d" % layer]
        dz, gs["ln_ffn_g"][layer], gs["ln_ffn_b"][layer] = ln_bwd(
            n + "ln_ffn_b", dx, sv["xhat2"], sv["rstd2"], sm["ln_ffn_g"][layer][None])
        dx, g_w1, g_w2 = mlp_bwd(n + "mlp_b", dz, sv["relu"], sv["x1"], wf["w1"], wf["w2"])
        sent = emit_grads("mlp%d" % layer, {"w1": g_w1, "w2": g_w2})
        dz, gs["ln_mix_g"][layer], gs["ln_mix_b"][layer] = ln_bwd(
            n + "ln_mix_b", dx, sv["xhat1"], sv["rstd1"], sm["ln_mix_g"][layer][None], after=sent)
        if layer % 2 == 0:
            dycat = mm_t_rows(n + "mixout_dx", dz, wm["w_out"], F32)
            g_out = mm_grad(n + "mixout_dw", sv["ycat"], dz, a_sharded=True)
            du_pool, gs["pool_w"][j], gs["pool_scale"][j] = pool_bwd(
                n + "pool_b", sv["proj"], dycat, sm["pool_w"][j], sm["pool_scale"][j][None])
            (du_lru, du_gate, gs["w_a"][j], gs["w_x"][j], gs["b_a"][j], gs["b_x"][j], gs["lam"][j], gs["conv_w"][j],
             gs["conv_b"][j]) = lru_bwd(n + "lru_b", sv["proj"], sv["hstate"], dycat, sm["conv_w"][j],
                                        sm["conv_b"][j][None], sm["w_a"][j], sm["b_a"][j][None], sm["w_x"][j],
                                        sm["b_x"][j][None], sm["lam"][j][None])
            dproj = jnp.concatenate([du_pool, du_lru, du_gate], axis=1)
            g_in = mm_grad(n + "proj_dw", sv["x"], dproj, a_sharded=False)
            sent = emit_grads("mix%d" % layer, {"w_in": g_in, "w_out": g_out})
            dx = mm_t_cols(n + "proj_dx", dproj, wm["w_in"], F32, resid=dz, after=sent)
        else:
            do = mm_t_rows(n + "attnout_dx", dz, wm["w_o"], BF16)
            g_o = mm_grad(n + "attnout_dw", sv["o"], dz, a_sharded=True)
            dq, dkv, dkp = attn_bwd(n + "attn_b", sv["q"], sv["kv"], sv["kp"], do, tabs)
            g_qb = mm_grad(n + "q_dw", sv["cq"], dq, a_sharded=False)
            dcq = mm_t_cols(n + "q_dx", dq, wm["w_qb"], F32)
            g_kvb = mm_grad(n + "kv_dw", sv["ckv"], dkv, a_sharded=False)
            dckv = mm_t_cols(n + "kv_dx", dkv, wm["w_kvb"], F32)
            ddown, gs["gq"][j], gs["gkv"][j] = mla_prep_bwd(
                n + "prep_b", sv["down"], dcq, dckv, dkp, sm["gq"][j][None], sm["gkv"][j][None], tabs)
            g_down = mm_grad(n + "down_dw", sv["x"], ddown, a_sharded=True)
            sent = emit_grads("mix%d" % layer, {"w_down": g_down, "w_qb": g_qb, "w_kvb": g_kvb, "w_o": g_o})
            dx = mm_t_rows(n + "down_dx", ddown, wm["w_down"], F32, resid=dz, after=sent)
    gs = {k: jnp.stack([a.reshape(sm[k].shape[1:]) for a in v]) for k, v in gs.items()}
    return loss_tile[0, 0], dx, gs


def _place():
    x, y, c = lax.axis_index("x"), lax.axis_index("y"), lax.axis_index("c")
    chips = [(1 - x, y), (x, 1 - y), (1 - x, 1 - y)]
    return x, y, c, chips


def _hbm_call(body, name, args, out_shape, scratch, aliases=None):
    return pl.pallas_call(
        body, name=name, in_specs=[pl.BlockSpec(memory_space=pl.ANY)] * len(args),
        out_specs=[pl.BlockSpec(memory_space=pl.ANY)] * len(out_shape), out_shape=out_shape,
        scratch_shapes=scratch, input_output_aliases=aliases or {},
        compiler_params=pltpu.CompilerParams(has_side_effects=True),
    )(*args)


HBM_SPEC = pl.BlockSpec(memory_space=pltpu.HBM)
SEM_SPEC = pl.BlockSpec(memory_space=pltpu.SEMAPHORE)
EFFECT = pltpu.SideEffectType.DATAFLOW_SIDE_EFFECTING


def _remote(src, dst, send_sem, recv_sem, device):
    return pltpu.make_async_remote_copy(src_ref=src, dst_ref=dst, send_sem=send_sem, recv_sem=recv_sem,
                                        device_id=device, device_id_type=MESH)


def place_own(name, srcs, place, steps, reducing):
    n = len(srcs)
    in_specs, out_specs, out_shape = [], [], []
    for s in srcs:
        if reducing:
            nd, rows, cols = s.shape
            tr = rows // 2 // steps
            pick = (lambda i, p: (p[0], p[1] * steps + i, 0)) if nd == N_CHIPS else (lambda i, p: (0, p[1] * steps + i, 0))
            in_specs.append(pl.BlockSpec((None, tr, cols), pick))
            out_specs.append(pl.BlockSpec((None, tr, cols), lambda i, p: (p[2], i, 0)))
            out_shape.append(jax.ShapeDtypeStruct((N_DEV, rows // 2, cols), s.dtype))
        else:
            rows, cols = s.shape
            tr = rows // steps
            in_specs.append(pl.BlockSpec((tr, cols), lambda i, p: (i, 0)))
            out_specs.append(pl.BlockSpec((None, tr, cols), lambda i, p: (p[0], i, 0)))
            out_shape.append(jax.ShapeDtypeStruct((N_CHIPS, rows, cols), s.dtype))

    def body(p_ref, *refs):
        for i_ref, o_ref in zip(refs[:n], refs[n:]):
            o_ref[...] = i_ref[...]

    return pl.pallas_call(
        body, name=name, out_shape=out_shape,
        grid_spec=pltpu.PrefetchScalarGridSpec(num_scalar_prefetch=1, grid=(steps,), in_specs=in_specs,
                                               out_specs=out_specs),
        compiler_params=_params(("parallel",)),
    )(place, *srcs)


def split_start(name, srcs, lands, n_sems, plan, token_in=None):
    n = len(srcs)
    srcs = [pltpu.with_memory_space_constraint(s, pltpu.HBM) for s in srcs]
    lands = [pltpu.with_memory_space_constraint(a, pltpu.HBM) for a in lands]
    extra = [] if token_in is None else [token_in]

    def body(*refs):
        src_refs, land_refs = refs[:n], refs[n:2 * n]
        outs = refs[2 * n + len(extra):]
        send_sems, recv_sems, token = outs[0], outs[1], outs[2 + 2 * n]
        sends, _ = plan(src_refs, land_refs, send_sems, recv_sems)
        for cp in sends:
            cp.start()
        token[...] = jnp.zeros_like(token)

    out_shape = (pltpu.SemaphoreType.DMA((n * n_sems,)), pltpu.SemaphoreType.DMA((n * n_sems,)),
                 *[pltpu.HBM(a.shape, a.dtype) for a in srcs + lands], jax.ShapeDtypeStruct((8, 128), F32))
    res = pl.pallas_call(
        body, name=name, out_shape=out_shape,
        in_specs=[HBM_SPEC] * (2 * n) + [pl.BlockSpec(memory_space=pl.ANY)] * len(extra),
        out_specs=(SEM_SPEC, SEM_SPEC, *[HBM_SPEC] * (2 * n), pl.BlockSpec(memory_space=pltpu.VMEM)),
        input_output_aliases={i: 2 + i for i in range(2 * n)},
        compiler_params=pltpu.CompilerParams(has_side_effects=EFFECT),
    )(*srcs, *lands, *extra)
    return dict(send=res[0], recv=res[1], srcs=list(res[2:2 + n]), lands=list(res[2 + n:2 + 2 * n]), plan=plan), res[-1]


def split_wait(name, started, after):
    sizes = [len(st["srcs"]) for st in started]
    n_buf = 2 * sum(sizes)

    def body(*refs):
        bufs, sems = refs[:n_buf], refs[n_buf:n_buf + 2 * len(started)]
        off = 0
        for g, (st, n) in enumerate(zip(started, sizes)):
            src_refs, land_refs = bufs[off:off + n], bufs[off + n:off + 2 * n]
            off += 2 * n
            sends, expects = st["plan"](src_refs, land_refs, sems[2 * g], sems[2 * g + 1])
            for cp in sends:
                cp.wait_send()
            for cp in expects:
                cp.wait_recv()

    bufs = [a for st in started for a in st["srcs"] + st["lands"]]
    sems = [s for st in started for s in (st["send"], st["recv"])]
    res = pl.pallas_call(
        body, name=name, out_shape=tuple(pltpu.HBM(a.shape, a.dtype) for a in bufs),
        in_specs=[HBM_SPEC] * n_buf + [SEM_SPEC] * len(sems) + [pl.BlockSpec(memory_space=pl.ANY)],
        out_specs=tuple([HBM_SPEC] * n_buf), input_output_aliases={i: i for i in range(n_buf)},
        compiler_params=pltpu.CompilerParams(has_side_effects=EFFECT),
    )(*bufs, *sems, after)
    out, off = [], 0
    for n in sizes:
        out.append(list(res[off + n:off + 2 * n]))
        off += 2 * n
    return out


def gather_plan(src_refs, land_refs, send_sems, recv_sems):
    x, y, c, chips = _place()
    me = 2 * x + y
    sends, expects = [], []
    for k, (s, d) in enumerate(zip(src_refs, land_refs)):
        for j, (px, py) in enumerate(chips):
            sem = 3 * k + j
            sends.append(_remote(s, d.at[me], send_sems.at[sem], recv_sems.at[sem], (px, py, c)))
            expects.append(_remote(s, d.at[2 * px + py], send_sems.at[sem], recv_sems.at[sem], (px, py, c)))
    return sends, expects


def _reduce_part(s, chip_idx, h):
    hr = s.shape[1] // 2
    return s.at[chip_idx if s.shape[0] == N_CHIPS else 0, pl.ds(h * hr, hr)]


def reduce_plan(src_refs, land_refs, send_sems, recv_sems):
    x, y, c, chips = _place()
    me_chip, me_dev = 2 * x + y, 4 * x + 2 * y + c
    sends, expects = [], []
    for k, (s, d) in enumerate(zip(src_refs, land_refs)):
        part = functools.partial(_reduce_part, s)
        for j, (px, py) in enumerate(chips):
            for h in range(2):
                sends.append(_remote(part(2 * px + py, h), d.at[me_dev], send_sems.at[7 * k + 2 * j + h],
                                     recv_sems.at[7 * k + 2 * j + c], (px, py, h)))
                expects.append(_remote(part(me_chip, c), d.at[4 * px + 2 * py + h], send_sems.at[7 * k + 2 * j + h],
                                       recv_sems.at[7 * k + 2 * j + h], (px, py, h)))
        sends.append(_remote(part(me_chip, 1 - c), d.at[me_dev], send_sems.at[7 * k + 6], recv_sems.at[7 * k + 6],
                             (x, y, 1 - c)))
        expects.append(_remote(part(me_chip, c), d.at[me_dev + 1 - 2 * c], send_sems.at[7 * k + 6],
                               recv_sems.at[7 * k + 6], (x, y, 1 - c)))
    return sends, expects


def sibling_swap_halves(name, fulls):
    n = len(fulls)

    def body(*refs):
        outs = refs[n:2 * n]
        send_sems, recv_sems = refs[2 * n:]
        x, y, c, _ = _place()
        copies = []
        for k in range(n):
            nl, rows = fulls[k].shape[:2]
            hr = rows // 2
            mine = outs[k].at[pl.ds(0, nl), pl.ds(c * hr, hr)]
            theirs = outs[k].at[pl.ds(0, nl), pl.ds((1 - c) * hr, hr)]
            copies.append((_remote(mine, mine, send_sems.at[k], recv_sems.at[k], (x, y, 1 - c)),
                           _remote(mine, theirs, send_sems.at[k], recv_sems.at[k], (x, y, 1 - c))))
        for send, _ in copies:
            send.start()
        for send, recv in copies:
            send.wait_send()
            recv.wait_recv()

    out_shape = [jax.ShapeDtypeStruct(f.shape, f.dtype) for f in fulls]
    scratch = [pltpu.SemaphoreType.DMA((n,)), pltpu.SemaphoreType.DMA((n,))]
    return _hbm_call(body, name, fulls, out_shape, scratch, aliases={k: k for k in range(n)})


def _tile_rows(rows, cols, budget_bytes):
    best = None
    for t in range(16, rows + 1, 16):
        if rows % t == 0 and t * cols * 4 <= budget_bytes:
            best = t
    assert best is not None, (rows, cols)
    return best


N_DEV = 8


def sum_devices(name, landed, layer, n_layers, prev, c_arr):
    _, hr, cols = landed.shape
    tr = _tile_rows(hr, cols, 512 * 1024)
    n_blk = hr // tr

    def body(c_ref, r_ref, *rest):
        o_ref = rest[-1]
        acc = r_ref[0].astype(F32)
        for d in range(1, N_DEV):
            acc = acc + r_ref[d].astype(F32)
        o_ref[...] = acc

    in_specs = [pl.BlockSpec((N_DEV, tr, cols), lambda r, c_ref: (0, r, 0))]
    args = [c_arr, landed]
    aliases = {}
    if prev is not None:
        in_specs.append(pl.BlockSpec(memory_space=pl.ANY))
        args.append(prev)
        aliases = {2: 0}
    return pl.pallas_call(
        body, name=name, out_shape=jax.ShapeDtypeStruct((n_layers, 2 * hr, cols), F32),
        grid_spec=pltpu.PrefetchScalarGridSpec(
            num_scalar_prefetch=1, grid=(n_blk,), in_specs=in_specs,
            out_specs=pl.BlockSpec((None, tr, cols), lambda r, c_ref: (layer, c_ref[0] * n_blk + r, 0))),
        input_output_aliases=aliases, compiler_params=_params(("parallel",)),
    )(*args)


def _pack(arrs, rows_multiple):
    flat = []
    for a in arrs:
        v = a.reshape(-1).astype(F32)
        flat.append(jnp.pad(v, (0, (-v.shape[0]) % HEAD)))
    v = jnp.concatenate(flat)
    v = jnp.pad(v, (0, (-v.shape[0]) % (HEAD * rows_multiple)))
    return v.reshape(-1, HEAD)


def _unpack(packed, shapes):
    flat = packed.reshape(-1)
    out, off = [], 0
    for shp in shapes:
        size = int(np.prod(shp))
        out.append(flat[off:off + size].reshape(shp))
        off += size + (-size) % HEAD
    return out


BIG = ["even_w_in", "even_w_out", "mla_w_down", "mla_w_qb", "mla_w_kvb", "mla_w_o", "mlp_w1", "mlp_w2"]
BIG_KEY = {"even_w_in": "w_in", "even_w_out": "w_out", "mla_w_down": "w_down", "mla_w_qb": "w_qb",
           "mla_w_kvb": "w_kvb", "mla_w_o": "w_o", "mlp_w1": "w1", "mlp_w2": "w2"}
SMALL_KEY = {"ln_mix_g": "ln_mix_g", "ln_mix_b": "ln_mix_b", "ln_ffn_g": "ln_ffn_g", "ln_ffn_b": "ln_ffn_b",
             "pool_w": "pool_w", "pool_scale": "pool_scale", "lru_conv_w": "conv_w", "lru_conv_b": "conv_b",
             "lru_w_a": "w_a", "lru_b_a": "b_a", "lru_w_x": "w_x", "lru_b_x": "b_x", "lru_lambda": "lam",
             "mla_q_norm_g": "gq", "mla_kv_norm_g": "gkv"}
SMALL = list(SMALL_KEY)
SMALL_SHARDED = ["lru_conv_w", "mla_q_norm_g", "mla_kv_norm_g"]
WEIGHTS = ["ln_mix_g", "ln_mix_b", "ln_ffn_g", "ln_ffn_b", "even_w_in", "pool_w", "pool_scale", "lru_conv_w",
           "lru_conv_b", "lru_w_a", "lru_b_a", "lru_w_x", "lru_b_x", "lru_lambda", "even_w_out", "mla_w_down",
           "mla_q_norm_g", "mla_kv_norm_g", "mla_w_qb", "mla_w_kvb", "mla_w_o", "mlp_w1", "mlp_w2"]


GROUPS = ["mix0", "mlp0", "mix1", "mlp1", "mix2", "mlp2", "mix3", "mlp3"]


def _group_keys(group):
    layer = int(group[3:])
    if group.startswith("mlp"):
        return [("mlp_w1", "w1", layer), ("mlp_w2", "w2", layer)]
    if layer % 2 == 0:
        return [("even_w_in", "w_in", layer // 2), ("even_w_out", "w_out", layer // 2)]
    return [("mla_w_down", "w_down", layer // 2), ("mla_w_qb", "w_qb", layer // 2),
            ("mla_w_kvb", "w_kvb", layer // 2), ("mla_w_o", "w_o", layer // 2)]


def _pad_q_heads(w):
    lead = w.shape[:-1]
    w = w.reshape(lead + (2, QK_NOPE + QK_ROPE))
    w = jnp.pad(w, ((0, 0),) * len(lead) + ((0, 0), (0, QHEAD_PAD - QK_NOPE - QK_ROPE)))
    return w.reshape(lead + (2 * QHEAD_PAD,))


def _unpad_q_heads(g):
    lead = g.shape[:-1]
    return g.reshape(lead + (2, QHEAD_PAD))[..., :QK_NOPE + QK_ROPE].reshape(lead + (2 * (QK_NOPE + QK_ROPE),))


def _step(x, positions, loss_target, w, m, v):
    cx, cy, cc = lax.axis_index("x"), lax.axis_index("y"), lax.axis_index("c")
    chip = 2 * cx + cy
    c_arr = jnp.reshape(cc, (1,)).astype(jnp.int32)
    place = jnp.stack([chip, cc, 2 * chip + cc]).astype(jnp.int32)

    prepared = dict(w)
    prepared["mla_w_down"] = jnp.pad(w["mla_w_down"], ((0, 0), (0, 0), (0, ODD_IN_PAD - ODD_IN)))
    prepared["mla_w_qb"] = _pad_q_heads(w["mla_w_qb"])
    small_shard_shapes = [w[k].shape for k in SMALL_SHARDED]
    sources, zones = {}, {}
    for g in GROUPS:
        srcs = [prepared[name][idx].astype(BF16) for name, _, idx in _group_keys(g)]
        if g == GROUPS[0]:
            srcs.append(_pack([w[k] for k in SMALL_SHARDED], 32))
        sources[g] = srcs
        zones[g] = place_own("own_" + g, srcs, place, 4, False)
    gathering, token = {}, None
    for g in GROUPS:
        gathering[g], token = split_start("gather_" + g, sources[g], zones[g], 3, gather_plan, token)
    all_started = token

    def weights_of(g, after):
        lands = split_wait("gathered_" + g, [gathering[g]], all_started if g == GROUPS[0] else after)[0]
        out = {key: land for (_, key, _), land in zip(_group_keys(g), lands)}
        if g == GROUPS[0]:
            per_chip = [_unpack(lands[-1][s], small_shard_shapes) for s in range(N_CHIPS)]
            for i, key in enumerate(("conv_w", "gq", "gkv")):
                out[key] = jnp.concatenate([p[i] for p in per_chip], axis=-1)
        return out

    reducing = []

    def emit_grads(g, grads):
        srcs = [grads[key] for _, key, _ in _group_keys(g)]
        lands = place_own("mine_" + g, srcs, place, 4, True)
        started, token = split_start("reduce_" + g, srcs, lands, 7, reduce_plan)
        reducing.append((g, started))
        return token

    sm = {SMALL_KEY[k]: w[k] for k in SMALL if k not in SMALL_SHARDED}
    loss, grad_x, gs = _local_step(x[0], positions.reshape(S, 1), loss_target[0], sm, weights_of, emit_grads)
    loss = lax.psum(loss, ("x", "y", "c"))

    small_shapes = [gs[SMALL_KEY[k]].shape for k in SMALL]
    gs_pack = _pack([gs[SMALL_KEY[k]] for k in SMALL], 32)[None]
    steps = gs_pack.shape[1] // 2 // _tile_rows(gs_pack.shape[1] // 2, HEAD, 512 * 1024)
    lands = place_own("mine_small", [gs_pack], place, steps, True)
    started, small_sent = split_start("reduce_small", [gs_pack], lands, 7, reduce_plan, grad_x)
    reducing.append(("small", started))

    grad, delta, new_m, new_v = {}, {}, {}, {}
    late_groups = ("mix0", "small")
    stacks, after = {}, small_sent
    for late in (False, True):
        part = [(g, st) for g, st in reducing if (g in late_groups) == late]
        landed = split_wait("reduced_late" if late else "reduced_early", [st for _, st in part], after)
        for (g, _), lands in zip(part, landed):
            if g == "small":
                stacks["small"] = sum_devices("sum_small", lands[0], 0, 1, None, c_arr)
                continue
            for (name, key, idx), land in zip(_group_keys(g), lands):
                stacks[name] = sum_devices("sum_%s%d" % (key, idx), land, idx, w[name].shape[0], stacks.get(name), c_arr)
        names = [k for k in BIG if (k in ("even_w_in", "even_w_out")) == late] + (["small"] if late else [])
        reduced = dict(zip(names, sibling_swap_halves("swap_late" if late else "swap_early", [stacks[k] for k in names])))
        if not late:
            reduced["mla_w_down"] = reduced["mla_w_down"][..., :ODD_IN]
            reduced["mla_w_qb"] = _unpad_q_heads(reduced["mla_w_qb"])
        for k in names:
            if k == "small":
                continue
            grad[k] = reduced[k]
            shp = w[k].shape
            view = lambda a: a.reshape(-1, shp[-1])
            d, nm, nv = adamw("adamw_" + BIG_KEY[k], view(w[k]), view(grad[k]), view(m[k]), view(v[k]))
            delta[k], new_m[k], new_v[k] = d.reshape(shp), nm.reshape(shp), nv.reshape(shp)
            after = d
    g_small = dict(zip(SMALL, _unpack(reduced["small"], small_shapes)))
    for k in SMALL_SHARDED:
        width = w[k].shape[-1]
        g_small[k] = lax.dynamic_slice_in_dim(g_small[k], chip * width, width, axis=-1)
    grad.update(g_small)

    shapes = [w[k].shape for k in SMALL]
    d, nm, nv = adamw("adamw_small", *[_pack([t[k] for k in SMALL], 512) for t in (w, grad, m, v)])
    for k, dk, mk, vk in zip(SMALL, _unpack(d, shapes), _unpack(nm, shapes), _unpack(nv, shapes)):
        delta[k], new_m[k], new_v[k] = dk, mk, vk
    return (loss, grad_x[None], *[grad[k] for k in WEIGHTS], *[delta[k] for k in WEIGHTS],
            *[new_m[k] for k in WEIGHTS], *[new_v[k] for k in WEIGHTS])


def kernel(x, positions, ln_mix_g, ln_mix_b, ln_ffn_g, ln_ffn_b, even_w_in, pool_w, pool_scale, lru_conv_w, lru_conv_b, lru_w_a, lru_b_a, lru_w_x, lru_b_x, lru_lambda, even_w_out, mla_w_down, mla_q_norm_g, mla_kv_norm_g, mla_w_qb, mla_w_kvb, mla_w_o, mlp_w1, mlp_w2, loss_target, m_ln_mix_g, m_ln_mix_b, m_ln_ffn_g, m_ln_ffn_b, m_even_w_in, m_pool_w, m_pool_scale, m_lru_conv_w, m_lru_conv_b, m_lru_w_a, m_lru_b_a, m_lru_w_x, m_lru_b_x, m_lru_lambda, m_even_w_out, m_mla_w_down, m_mla_q_norm_g, m_mla_kv_norm_g, m_mla_w_qb, m_mla_w_kvb, m_mla_w_o, m_mlp_w1, m_mlp_w2, v_ln_mix_g, v_ln_mix_b, v_ln_ffn_g, v_ln_ffn_b, v_even_w_in, v_pool_w, v_pool_scale, v_lru_conv_w, v_lru_conv_b, v_lru_w_a, v_lru_b_a, v_lru_w_x, v_lru_b_x, v_lru_lambda, v_even_w_out, v_mla_w_down, v_mla_q_norm_g, v_mla_kv_norm_g, v_mla_w_qb, v_mla_w_kvb, v_mla_w_o, v_mlp_w1, v_mlp_w2):
    args = locals()
    w = {k: args[k] for k in WEIGHTS}
    m = {k: args["m_" + k] for k in WEIGHTS}
    v = {k: args["v_" + k] for k in WEIGHTS}
    return _step(x, positions, loss_target, w, m, v)
```

```python
import functools
import math

import jax
import jax.numpy as jnp
import numpy as np
from jax import lax
from jax.experimental import pallas as pl
from jax.experimental.pallas import tpu as pltpu

F32 = jnp.float32
BF16 = jnp.bfloat16

S = 2048
D = 1024
DEPTH = 4
N_CHIPS = 4
POOL_WINDOWS = (2, 4, 8, 16)
POOL_W = 512
LRU_W = 1024
LRU_HEADS = 8
HEAD = 128
EVEN_IN = 2560
EVEN_MIX = 1536
MLA_HEADS = 8
QK_NOPE = 128
QK_ROPE = 64
Q_RANK = 384
KV_RANK = 256
ODD_IN = 704
ODD_IN_PAD = 768
QHEAD_PAD = 256
D_FF = 4096
CHUNK = 64
ALPHA = (2 * DEPTH) ** 0.25
LN_EPS = 1e-5
RMS_EPS = 1e-6
ATT_SCALE = (QK_NOPE + QK_ROPE) ** -0.5
ROPE_THETA = 10000.0
LRU_C = 8.0
ADAM_LR = 0.001
ADAM_B1 = 0.9
ADAM_B2 = 0.999
ADAM_EPS = 1e-08
ADAM_WD = 0.01
ADAM_STEP = 10

VMEM_LIMIT = 56 * 1024 * 1024
MESH = pl.DeviceIdType.MESH

_NN = (((1,), (0,)), ((), ()))
_NT = (((1,), (1,)), ((), ()))
_TN = (((0,), (0,)), ((), ()))


def _params(sem=None, **kw):
    return pltpu.CompilerParams(dimension_semantics=sem, vmem_limit_bytes=VMEM_LIMIT, **kw)


def _dot(a, b, dims):
    return lax.dot_general(a.astype(BF16), b.astype(BF16), dims, preferred_element_type=F32)


def _mm(name, a, w4, *, dims, a_block, a_map, w_map, grid, sem, outs, out_blocks, out_maps,
        red_axis=None, n_red=1, extras=(), extra_blocks=(), extra_maps=(), epilogue=None, after=None):
    n_extra = len(extras)
    n_out = len(outs)
    n_after = 0 if after is None else 1
    w_block = (None,) + tuple(w4.shape[1:])

    def body(a_ref, w_ref, *rest):
        rest = rest[n_after:]
        extra_refs = rest[:n_extra]
        out_refs = rest[n_extra:n_extra + n_out]

        def finish(acc):
            vals = epilogue(acc, *[r[...] for r in extra_refs]) if epilogue else (acc,)
            for r, v in zip(out_refs, vals):
                r[...] = v.astype(r.dtype)

        if red_axis is None:
            finish(_dot(a_ref[...], w_ref[...], dims))
        else:
            acc_ref = rest[-1]
            k = pl.program_id(red_axis)

            @pl.when(k == 0)
            def _():
                acc_ref[...] = jnp.zeros_like(acc_ref)

            acc_ref[...] += _dot(a_ref[...], w_ref[...], dims)

            @pl.when(k == n_red - 1)
            def _():
                finish(acc_ref[...])

    scratch = [] if red_axis is None else [pltpu.VMEM(out_blocks[0], F32)]
    return pl.pallas_call(
        body, name=name, grid=grid,
        in_specs=[pl.BlockSpec(a_block, a_map), pl.BlockSpec(w_block, w_map)]
        + [pl.BlockSpec(memory_space=pl.ANY)] * n_after + [pl.BlockSpec(b, m) for b, m in zip(extra_blocks, extra_maps)],
        out_specs=[pl.BlockSpec(b, m) for b, m in zip(out_blocks, out_maps)],
        out_shape=outs, scratch_shapes=scratch, compiler_params=_params(sem),
    )(a, w4, *([after] * n_after), *extras)


def _w_map_outer(s, i):
    return (s, 0, 0)


def _w_map_inner(i, s):
    return (s, 0, 0)


def mm_cols(name, a, w4, out_dtypes, *, tm=512, epilogue=None):
    m, k = a.shape
    nb = w4.shape[2]
    outs = [jax.ShapeDtypeStruct((m, N_CHIPS * nb), dt) for dt in out_dtypes]
    return _mm(name, a, w4, dims=_NN, a_block=(tm, k), a_map=lambda s, i: (i, 0), w_map=_w_map_outer,
               grid=(N_CHIPS, m // tm), sem=("parallel", "parallel"), outs=outs,
               out_blocks=[(tm, nb)] * len(outs), out_maps=[lambda s, i: (i, s)] * len(outs), epilogue=epilogue)


def mm_rows(name, a, w4, out_dtype, *, tm=512):
    m = a.shape[0]
    kb, n = w4.shape[1:]
    outs = [jax.ShapeDtypeStruct((m, n), out_dtype)]
    return _mm(name, a, w4, dims=_NN, a_block=(tm, kb), a_map=lambda i, s: (i, s), w_map=_w_map_inner,
               grid=(m // tm, N_CHIPS), sem=("parallel", "arbitrary"), outs=outs, out_blocks=[(tm, n)],
               out_maps=[lambda i, s: (i, 0)], red_axis=1, n_red=N_CHIPS)[0]


def mm_rows_ln(name, a, w4, x, g, b, *, tm=512):
    m = a.shape[0]
    kb, n = w4.shape[1:]
    outs = [jax.ShapeDtypeStruct((m, n), F32), jax.ShapeDtypeStruct((m, n), F32), jax.ShapeDtypeStruct((m, 1), F32)]
    return _mm(name, a, w4, dims=_NN, a_block=(tm, kb), a_map=lambda i, s: (i, s), w_map=_w_map_inner,
               grid=(m // tm, N_CHIPS), sem=("parallel", "arbitrary"), outs=outs, out_blocks=[(tm, n), (tm, n), (tm, 1)],
               out_maps=[lambda i, s: (i, 0)] * 3, red_axis=1, n_red=N_CHIPS, extras=(x, g, b),
               extra_blocks=[(tm, n), (1, n), (1, n)],
               extra_maps=[lambda i, s: (i, 0), lambda i, s: (0, 0), lambda i, s: (0, 0)],
               epilogue=lambda acc, xv, gv, bv: _layer_norm(ALPHA * xv + acc, gv, bv))


def mm_t_cols(name, g, w4, out_dtype, *, tm=512, resid=None, after=None):
    m = g.shape[0]
    k, nb = w4.shape[1:]
    outs = [jax.ShapeDtypeStruct((m, k), out_dtype)]
    extras = () if resid is None else (resid,)
    epi = None if resid is None else (lambda acc, r: (acc + ALPHA * r,))
    return _mm(name, g, w4, dims=_NT, a_block=(tm, nb), a_map=lambda i, s: (i, s), w_map=_w_map_inner,
               grid=(m // tm, N_CHIPS), sem=("parallel", "arbitrary"), outs=outs, out_blocks=[(tm, k)],
               out_maps=[lambda i, s: (i, 0)], red_axis=1, n_red=N_CHIPS, extras=extras,
               extra_blocks=[(tm, k)] * len(extras), extra_maps=[lambda i, s: (i, 0)] * len(extras), epilogue=epi,
               after=after)[0]


def mm_t_rows(name, g, w4, out_dtype, *, tm=512, resid=None, gate=None, after=None):
    m, n = g.shape
    kb = w4.shape[1]
    outs = [jax.ShapeDtypeStruct((m, N_CHIPS * kb), out_dtype)]
    extras, epi = (), None
    if resid is not None:
        extras, epi = (resid,), (lambda acc, r: (acc + ALPHA * r,))
    if gate is not None:
        extras, epi = (gate,), (lambda acc, r: (acc * (2.0 * r.astype(F32)),))
    return _mm(name, g, w4, dims=_NT, a_block=(tm, n), a_map=lambda s, i: (i, 0), w_map=_w_map_outer,
               grid=(N_CHIPS, m // tm), sem=("parallel", "parallel"), outs=outs, out_blocks=[(tm, kb)],
               out_maps=[lambda s, i: (i, s)], extras=extras, extra_blocks=[(tm, kb)] * len(extras),
               extra_maps=[lambda s, i: (i, s)] * len(extras), epilogue=epi, after=after)[0]


def mm_grad(name, a, g, *, a_sharded, tm=512):
    m = a.shape[0]
    ka = a.shape[1] // N_CHIPS if a_sharded else a.shape[1]
    ng = g.shape[1] if a_sharded else g.shape[1] // N_CHIPS
    n_red = m // tm
    a_map = (lambda s, i: (i, s)) if a_sharded else (lambda s, i: (i, 0))
    g_map = (lambda s, i: (i, 0)) if a_sharded else (lambda s, i: (i, s))

    def body(a_ref, g_ref, out_ref, acc_ref):
        k = pl.program_id(1)

        @pl.when(k == 0)
        def _():
            acc_ref[...] = jnp.zeros_like(acc_ref)

        acc_ref[...] += _dot(a_ref[...], g_ref[...], _TN)

        @pl.when(k == n_red - 1)
        def _():
            out_ref[...] = acc_ref[...].astype(out_ref.dtype)

    return pl.pallas_call(
        body, name=name, grid=(N_CHIPS, n_red),
        in_specs=[pl.BlockSpec((tm, ka), a_map), pl.BlockSpec((tm, ng), g_map)],
        out_specs=pl.BlockSpec((None, ka, ng), lambda s, i: (s, 0, 0)),
        out_shape=jax.ShapeDtypeStruct((N_CHIPS, ka, ng), BF16),
        scratch_shapes=[pltpu.VMEM((ka, ng), F32)], compiler_params=_params(("parallel", "arbitrary")),
    )(a, g)


def _layer_norm(z, g, b):
    mu = jnp.mean(z, axis=-1, keepdims=True)
    zc = z - mu
    rstd = lax.rsqrt(jnp.mean(zc * zc, axis=-1, keepdims=True) + LN_EPS)
    xhat = zc * rstd
    return xhat * g + b, xhat, rstd


def mlp_fwd(name, x, w1, w2, g, b, *, tm=512):
    fb = w1.shape[2]

    def body(x_ref, w1_ref, w2_ref, g_ref, b_ref, y_ref, xhat_ref, rstd_ref, relu_ref, acc_ref):
        s = pl.program_id(1)

        @pl.when(s == 0)
        def _():
            acc_ref[...] = jnp.zeros_like(acc_ref)

        r = jnp.maximum(_dot(x_ref[...], w1_ref[...], _NN), 0.0)
        relu_ref[...] = r.astype(relu_ref.dtype)
        acc_ref[...] += _dot(r * r, w2_ref[...], _NN)

        @pl.when(s == N_CHIPS - 1)
        def _():
            y_ref[...], xhat_ref[...], rstd_ref[...] = _layer_norm(ALPHA * x_ref[...] + acc_ref[...], g_ref[...], b_ref[...])

    row = pl.BlockSpec((tm, D), lambda i, s: (i, 0))
    vec = pl.BlockSpec((1, D), lambda i, s: (0, 0))
    return pl.pallas_call(
        body, name=name, grid=(S // tm, N_CHIPS),
        in_specs=[row, pl.BlockSpec((None, D, fb), lambda i, s: (s, 0, 0)),
                  pl.BlockSpec((None, fb, D), lambda i, s: (s, 0, 0)), vec, vec],
        out_specs=[row, row, pl.BlockSpec((tm, 1), lambda i, s: (i, 0)), pl.BlockSpec((tm, fb), lambda i, s: (i, s))],
        out_shape=[jax.ShapeDtypeStruct((S, D), F32), jax.ShapeDtypeStruct((S, D), F32),
                   jax.ShapeDtypeStruct((S, 1), F32), jax.ShapeDtypeStruct((S, N_CHIPS * fb), BF16)],
        scratch_shapes=[pltpu.VMEM((tm, D), F32)], compiler_params=_params(("parallel", "arbitrary")),
    )(x, w1, w2, g, b)


def mlp_bwd(name, dz, relu, x, w1, w2, *, tm=256):
    fb = w1.shape[2]
    n_i = S // tm

    def body(dz_ref, relu_ref, x_ref, w1_ref, w2_ref, dx_ref, g1_ref, g2_ref, acc1_ref, acc2_ref):
        s, i = pl.program_id(0), pl.program_id(1)
        rows = pl.ds(pl.multiple_of(i * tm, tm), tm)
        dz_v = dz_ref[...]

        @pl.when(i == 0)
        def _():
            acc1_ref[...] = jnp.zeros_like(acc1_ref)
            acc2_ref[...] = jnp.zeros_like(acc2_ref)

        @pl.when(s == 0)
        def _():
            dx_ref[rows, :] = ALPHA * dz_v

        dz_b = dz_v.astype(BF16)
        r = relu_ref[...]
        dh = (_dot(dz_b, w2_ref[...], _NT) * (2.0 * r.astype(F32))).astype(BF16)
        p2 = _dot(r * r, dz_b, _TN)
        p1 = _dot(x_ref[...], dh, _TN)
        dx_ref[rows, :] += _dot(dh, w1_ref[...], _NT)
        acc1_ref[...] += p1
        acc2_ref[...] += p2

        @pl.when(i == n_i - 1)
        def _():
            g1_ref[...] = acc1_ref[...].astype(g1_ref.dtype)
            g2_ref[...] = acc2_ref[...].astype(g2_ref.dtype)

    row = pl.BlockSpec((tm, D), lambda s, i: (i, 0))
    return pl.pallas_call(
        body, name=name, grid=(N_CHIPS, n_i),
        in_specs=[row, pl.BlockSpec((tm, fb), lambda s, i: (i, s)), row,
                  pl.BlockSpec((None, D, fb), lambda s, i: (s, 0, 0)), pl.BlockSpec((None, fb, D), lambda s, i: (s, 0, 0))],
        out_specs=[pl.BlockSpec((S, D), lambda s, i: (0, 0)), pl.BlockSpec((None, D, fb), lambda s, i: (s, 0, 0)),
                   pl.BlockSpec((None, fb, D), lambda s, i: (s, 0, 0))],
        out_shape=[jax.ShapeDtypeStruct((S, D), F32), jax.ShapeDtypeStruct((N_CHIPS, D, fb), BF16),
                   jax.ShapeDtypeStruct((N_CHIPS, fb, D), BF16)],
        scratch_shapes=[pltpu.VMEM((D, fb), F32), pltpu.VMEM((fb, D), F32)],
        compiler_params=_params(("arbitrary", "arbitrary")),
    )(dz, relu, x, w1, w2)


def ln_bwd(name, dy, xhat, rstd, g, *, tm=256, after=None):
    n_after = 0 if after is None else 1

    def body(dy_ref, xhat_ref, rstd_ref, g_ref, *rest):
        dz_ref, dg_ref, db_ref = rest[n_after:]
        dy_v = dy_ref[...]
        xh = xhat_ref[...]
        dxh = dy_v * g_ref[...]
        m1 = jnp.mean(dxh, axis=-1, keepdims=True)
        m2 = jnp.mean(dxh * xh, axis=-1, keepdims=True)
        dz_ref[...] = rstd_ref[...] * (dxh - m1 - xh * m2)
        pg = jnp.sum(dy_v * xh, axis=0, keepdims=True)
        pb = jnp.sum(dy_v, axis=0, keepdims=True)
        i = pl.program_id(0)

        @pl.when(i == 0)
        def _():
            dg_ref[...] = pg
            db_ref[...] = pb

        @pl.when(i > 0)
        def _():
            dg_ref[...] += pg
            db_ref[...] += pb

    row = pl.BlockSpec((tm, D), lambda i: (i, 0))
    vec = pl.BlockSpec((1, D), lambda i: (0, 0))
    return pl.pallas_call(
        body, name=name, grid=(S // tm,),
        in_specs=[row, row, pl.BlockSpec((tm, 1), lambda i: (i, 0)), vec] + [pl.BlockSpec(memory_space=pl.ANY)] * n_after,
        out_specs=[row, vec, vec],
        out_shape=[jax.ShapeDtypeStruct((S, D), F32), jax.ShapeDtypeStruct((1, D), F32),
                   jax.ShapeDtypeStruct((1, D), F32)],
        compiler_params=_params(("arbitrary",)),
    )(dy, xhat, rstd, g, *([after] * n_after))


def loss_head(y, target, *, tm=256):
    def body(y_ref, t_ref, dy_ref, loss_ref):
        e = y_ref[...] - t_ref[...]
        dy_ref[...] = e * (1.0 / D)
        part = jnp.sum(jnp.sum(e * e, axis=-1, keepdims=True), axis=0, keepdims=True) * (0.5 / D)
        i = pl.program_id(0)

        @pl.when(i == 0)
        def _():
            loss_ref[...] = jnp.zeros_like(loss_ref)

        loss_ref[...] += jnp.broadcast_to(part, loss_ref.shape)

    row = pl.BlockSpec((tm, D), lambda i: (i, 0))
    return pl.pallas_call(
        body, name="loss_head", grid=(S // tm,), in_specs=[row, row],
        out_specs=[row, pl.BlockSpec((8, 128), lambda i: (0, 0))],
        out_shape=[jax.ShapeDtypeStruct((S, D), F32), jax.ShapeDtypeStruct((8, 128), F32)],
        compiler_params=_params(("arbitrary",)),
    )(y, target)


def _rows(shape):
    return lax.broadcasted_iota(jnp.int32, shape, 0)


def _shift_down(x, k):
    return jnp.where(_rows(x.shape) >= k, pltpu.roll(x, k, 0), 0.0)


def _shift_up(x, k):
    n = x.shape[0]
    return jnp.where(_rows(x.shape) < n - k, pltpu.roll(x, n - k, 0), 0.0)


def _pool_diff(u, w):
    acc, k = u, 1
    while k < w:
        acc = acc + _shift_down(acc, k)
        k *= 2
    cnt = jnp.minimum(_rows(u.shape) + 1, w).astype(F32)
    return acc / cnt - u, cnt


def pool_fwd(name, proj, pool_w, pool_scale):
    def body(u_ref, w_ref, sc_ref, y_ref):
        for g, w in enumerate(POOL_WINDOWS):
            cols = slice(g * HEAD, (g + 1) * HEAD)
            d, _ = _pool_diff(u_ref[:, cols], w)
            z = _dot(d, w_ref[g], _NN)
            y_ref[:, cols] = (z * sc_ref[:, cols]).astype(y_ref.dtype)

    return pl.pallas_call(
        body, name=name, grid=(1,),
        in_specs=[pl.BlockSpec((S, POOL_W), lambda i: (0, 0)),
                  pl.BlockSpec((4, HEAD, HEAD), lambda i: (0, 0, 0)),
                  pl.BlockSpec((1, POOL_W), lambda i: (0, 0))],
        out_specs=pl.BlockSpec((S, POOL_W), lambda i: (0, 0)),
        out_shape=jax.ShapeDtypeStruct((S, POOL_W), BF16),
        compiler_params=_params(("arbitrary",)),
    )(proj, pool_w, pool_scale)


def pool_bwd(name, proj, dycat, pool_w, pool_scale):
    def body(u_ref, dy_ref, w_ref, sc_ref, du_ref, dw_ref, dsc_ref):
        for g, w in enumerate(POOL_WINDOWS):
            cols = slice(g * HEAD, (g + 1) * HEAD)
            d, cnt = _pool_diff(u_ref[:, cols], w)
            dy = dy_ref[:, cols]
            z = _dot(d, w_ref[g], _NN)
            dsc_ref[:, cols] = jnp.sum(dy * z, axis=0, keepdims=True)
            dz = dy * sc_ref[:, cols]
            dw_ref[g] = _dot(d, dz, _TN)
            dd = _dot(dz, w_ref[g], _NT)
            acc, k = dd / cnt, 1
            while k < w:
                acc = acc + _shift_up(acc, k)
                k *= 2
            du_ref[:, cols] = (acc - dd).astype(du_ref.dtype)

    return pl.pallas_call(
        body, name=name, grid=(1,),
        in_specs=[pl.BlockSpec((S, POOL_W), lambda i: (0, 0)),
                  pl.BlockSpec((S, POOL_W), lambda i: (0, 0)),
                  pl.BlockSpec((4, HEAD, HEAD), lambda i: (0, 0, 0)),
                  pl.BlockSpec((1, POOL_W), lambda i: (0, 0))],
        out_specs=[pl.BlockSpec((S, POOL_W), lambda i: (0, 0)),
                   pl.BlockSpec((4, HEAD, HEAD), lambda i: (0, 0, 0)),
                   pl.BlockSpec((1, POOL_W), lambda i: (0, 0))],
        out_shape=[jax.ShapeDtypeStruct((S, POOL_W), BF16), jax.ShapeDtypeStruct((4, HEAD, HEAD), F32),
                   jax.ShapeDtypeStruct((1, POOL_W), F32)],
        compiler_params=_params(("arbitrary",)),
    )(proj, dycat, pool_w, pool_scale)


def _expm1(x):
    series = x * (1.0 + x * (0.5 + x * (1.0 / 6.0 + x * (1.0 / 24.0 + x * (1.0 / 120.0)))))
    return jnp.where(jnp.abs(x) < 0.05, series, jnp.exp(x) - 1.0)


def _softplus_neg(lam):
    e = jnp.exp(-jnp.abs(lam))
    log1p = jnp.where(e < 0.01, e * (1.0 - e * (0.5 - e * (1.0 / 3.0))), jnp.log(1.0 + e))
    return jnp.maximum(-lam, 0.0) + log1p


_GELU_C = math.sqrt(2.0 / math.pi)


def _gelu(x):
    t = jnp.tanh(_GELU_C * (x + 0.044715 * x * x * x))
    return 0.5 * x * (1.0 + t), t


def _gelu_grad(x, t):
    return 0.5 * (1.0 + t) + 0.5 * x * (1.0 - t * t) * _GELU_C * (1.0 + 3.0 * 0.044715 * x * x)


def _conv(u, cw, cb):
    return cw[3:4] * u + cw[2:3] * _shift_down(u, 1) + cw[1:2] * _shift_down(u, 2) + cw[0:1] * _shift_down(u, 3) + cb


def _lru_gates(cu, wa, ba, wx, bx, lam):
    r = jax.nn.sigmoid(_dot(cu, wa, _NN) + ba)
    i = jax.nn.sigmoid(_dot(cu, wx, _NN) + bx)
    sp = _softplus_neg(lam)
    log_a = (-LRU_C) * r * sp
    a = jnp.exp(log_a)
    mult = jnp.sqrt(-_expm1(2.0 * log_a))
    return r, i, sp, a, mult


def _scan(a_ref, b_ref, h_ref, *, reverse):
    n_blk = S // 8
    row8 = lax.broadcasted_iota(jnp.int32, (8, HEAD), 0)

    def step(j, carry):
        blk = (n_blk - 1 - j) if reverse else j
        r0 = pl.multiple_of(blk * 8, 8)
        a = a_ref[pl.ds(r0, 8), :]
        b = b_ref[pl.ds(r0, 8), :]
        for k in (1, 2, 4):
            if reverse:
                keep = row8 < 8 - k
                a_s, b_s = pltpu.roll(a, 8 - k, 0), pltpu.roll(b, 8 - k, 0)
            else:
                keep = row8 >= k
                a_s, b_s = pltpu.roll(a, k, 0), pltpu.roll(b, k, 0)
            b = jnp.where(keep, a * b_s + b, b)
            a = jnp.where(keep, a * a_s, a)
        h = b + a * carry
        h_ref[pl.ds(r0, 8), :] = h
        edge = h[0:1, :] if reverse else h[7:8, :]
        return jnp.broadcast_to(edge, (8, HEAD))

    lax.fori_loop(0, n_blk, step, jnp.zeros((8, HEAD), F32), unroll=4)


def _lru_specs():
    def col(off):
        return pl.BlockSpec((S, HEAD), lambda h: (0, off + h))
    vec = pl.BlockSpec((1, HEAD), lambda h: (0, h))
    mat = pl.BlockSpec((None, HEAD, HEAD), lambda h: (h, 0, 0))
    cw = pl.BlockSpec((4, HEAD), lambda h: (0, h))
    return col, vec, mat, cw


def lru_fwd(name, proj, conv_w, conv_b, w_a, b_a, w_x, b_x, lam):
    def body(u_ref, ug_ref, cw_ref, cb_ref, wa_ref, ba_ref, wx_ref, bx_ref, lam_ref, y_ref, h_ref, a_s, b_s):
        cu = _conv(u_ref[...], cw_ref[...], cb_ref[...])
        _, i, _, a, mult = _lru_gates(cu, wa_ref[...], ba_ref[...], wx_ref[...], bx_ref[...], lam_ref[...])
        a_s[...] = a
        b_s[...] = mult * (i * cu)
        _scan(a_s, b_s, h_ref, reverse=False)
        gl, _ = _gelu(ug_ref[...])
        y_ref[...] = (h_ref[...] * gl).astype(y_ref.dtype)

    col, vec, mat, cw = _lru_specs()
    out = pl.BlockSpec((S, HEAD), lambda h: (0, h))
    return pl.pallas_call(
        body, name=name, grid=(LRU_HEADS,),
        in_specs=[col(4), col(12), cw, vec, mat, vec, mat, vec, vec],
        out_specs=[out, out],
        out_shape=[jax.ShapeDtypeStruct((S, LRU_W), BF16), jax.ShapeDtypeStruct((S, LRU_W), F32)],
        scratch_shapes=[pltpu.VMEM((S, HEAD), F32), pltpu.VMEM((S, HEAD), F32)],
        compiler_params=_params(("parallel",)),
    )(proj, proj, conv_w, conv_b, w_a, b_a, w_x, b_x, lam)


def lru_bwd(name, proj, hstate, dycat, conv_w, conv_b, w_a, b_a, w_x, b_x, lam):
    def body(u_ref, ug_ref, h_ref, dy_ref, cw_ref, cb_ref, wa_ref, ba_ref, wx_ref, bx_ref, lam_ref,
             du_ref, dug_ref, dwa_ref, dwx_ref, dba_ref, dbx_ref, dlam_ref, dcw_ref, dcb_ref, a_s, b_s, g_s):
        u = u_ref[...]
        cw = cw_ref[...]
        cu = _conv(u, cw, cb_ref[...])
        lam_v = lam_ref[...]
        r, i, sp, a, mult = _lru_gates(cu, wa_ref[...], ba_ref[...], wx_ref[...], bx_ref[...], lam_v)
        ug = ug_ref[...]
        gl, t = _gelu(ug)
        dy = dy_ref[...]
        h = h_ref[...]
        dug_ref[...] = (dy * h * _gelu_grad(ug, t)).astype(dug_ref.dtype)
        a_s[...] = _shift_up(a, 1)
        b_s[...] = dy * gl
        _scan(a_s, b_s, g_s, reverse=True)
        dxin = g_s[...]
        da = dxin * _shift_down(h, 1)
        dmult = dxin * (i * cu)
        di = dxin * (mult * cu)
        dlog_a = da * a - dmult * (a * a) / mult
        dr_pre = dlog_a * ((-LRU_C) * sp) * (r * (1.0 - r))
        di_pre = di * (i * (1.0 - i))
        dsp = jnp.sum(dlog_a * ((-LRU_C) * r), axis=0, keepdims=True)
        dlam_ref[...] = dsp * (-jax.nn.sigmoid(-lam_v))
        dba_ref[...] = jnp.sum(dr_pre, axis=0, keepdims=True)
        dbx_ref[...] = jnp.sum(di_pre, axis=0, keepdims=True)
        dwa_ref[...] = _dot(cu, dr_pre, _TN)
        dwx_ref[...] = _dot(cu, di_pre, _TN)
        dcu = dxin * (mult * i) + _dot(dr_pre, wa_ref[...], _NT) + _dot(di_pre, wx_ref[...], _NT)
        dcb_ref[...] = jnp.sum(dcu, axis=0, keepdims=True)
        for k in range(4):
            dcw_ref[k:k + 1, :] = jnp.sum(dcu * (_shift_down(u, 3 - k) if k < 3 else u), axis=0, keepdims=True)
        du = cw[3:4] * dcu + cw[2:3] * _shift_up(dcu, 1) + cw[1:2] * _shift_up(dcu, 2) + cw[0:1] * _shift_up(dcu, 3)
        du_ref[...] = du.astype(du_ref.dtype)

    col, vec, mat, cw = _lru_specs()
    out = pl.BlockSpec((S, HEAD), lambda h: (0, h))
    big = jax.ShapeDtypeStruct((S, LRU_W), BF16)
    vec_shape = jax.ShapeDtypeStruct((1, LRU_W), F32)
    mat_shape = jax.ShapeDtypeStruct((LRU_HEADS, HEAD, HEAD), F32)
    return pl.pallas_call(
        body, name=name, grid=(LRU_HEADS,),
        in_specs=[col(4), col(12), out, col(4), cw, vec, mat, vec, mat, vec, vec],
        out_specs=[out, out, mat, mat, vec, vec, vec, cw, vec],
        out_shape=[big, big, mat_shape, mat_shape, vec_shape, vec_shape, vec_shape,
                   jax.ShapeDtypeStruct((4, LRU_W), F32), vec_shape],
        scratch_shapes=[pltpu.VMEM((S, HEAD), F32)] * 3,
        compiler_params=_params(("parallel",)),
    )(proj, proj, hstate, dycat, conv_w, conv_b, w_a, b_a, w_x, b_x, lam)


def rope_tables(pos_col, inv_freq):
    def body(pos_ref, f_ref, c_ref, s1_ref, s2_ref):
        ang = pos_ref[...].astype(F32) * f_ref[...]
        lane = lax.broadcasted_iota(jnp.int32, ang.shape, 1)
        cos, sin = jnp.cos(ang), jnp.sin(ang)
        c_ref[...] = jnp.where(lane < QK_ROPE, cos, 0.0)
        s1_ref[...] = jnp.where(lane < QK_ROPE // 2, -sin, 0.0)
        s2_ref[...] = jnp.where((lane >= QK_ROPE // 2) & (lane < QK_ROPE), sin, 0.0)

    tab = jax.ShapeDtypeStruct((S, HEAD), F32)
    return pl.pallas_call(
        body, name="rope_tables", grid=(1,),
        in_specs=[pl.BlockSpec((S, 1), lambda i: (0, 0)), pl.BlockSpec((1, HEAD), lambda i: (0, 0))],
        out_specs=[pl.BlockSpec((S, HEAD), lambda i: (0, 0))] * 3, out_shape=[tab, tab, tab],
        compiler_params=_params(("arbitrary",)),
    )(pos_col, inv_freq)


def _rope(v, c, s1, s2):
    return v * c + pltpu.roll(v, HEAD - QK_ROPE // 2, 1) * s1 + pltpu.roll(v, QK_ROPE // 2, 1) * s2


def _unrope(d, c, s1, s2):
    return d * c + pltpu.roll(d * s1, QK_ROPE // 2, 1) + pltpu.roll(d * s2, HEAD - QK_ROPE // 2, 1)


def _rms(x, g):
    rstd = lax.rsqrt(jnp.mean(x * x, axis=-1, keepdims=True) + RMS_EPS)
    return x * rstd, rstd


def mla_prep(name, down, gq, gkv, tabs, *, tm=256):
    def body(dn_ref, gq_ref, gkv_ref, c_ref, s1_ref, s2_ref, cq_ref, ckv_ref, kp_ref):
        xq, _ = _rms(dn_ref[:, :Q_RANK], None)
        cq_ref[...] = (xq * gq_ref[...]).astype(cq_ref.dtype)
        xkv, _ = _rms(dn_ref[:, Q_RANK:Q_RANK + KV_RANK], None)
        ckv_ref[...] = (xkv * gkv_ref[...]).astype(ckv_ref.dtype)
        kp = _rope(dn_ref[:, Q_RANK + KV_RANK:], c_ref[...], s1_ref[...], s2_ref[...])
        kp_ref[...] = kp.astype(kp_ref.dtype)

    tab = pl.BlockSpec((tm, HEAD), lambda i: (i, 0))
    return pl.pallas_call(
        body, name=name, grid=(S // tm,),
        in_specs=[pl.BlockSpec((tm, ODD_IN_PAD), lambda i: (i, 0)), pl.BlockSpec((1, Q_RANK), lambda i: (0, 0)),
                  pl.BlockSpec((1, KV_RANK), lambda i: (0, 0)), tab, tab, tab],
        out_specs=[pl.BlockSpec((tm, Q_RANK), lambda i: (i, 0)), pl.BlockSpec((tm, KV_RANK), lambda i: (i, 0)), tab],
        out_shape=[jax.ShapeDtypeStruct((S, Q_RANK), BF16), jax.ShapeDtypeStruct((S, KV_RANK), BF16),
                   jax.ShapeDtypeStruct((S, HEAD), BF16)],
        compiler_params=_params(("parallel",)),
    )(down, gq, gkv, *tabs)


def mla_prep_bwd(name, down, dcq, dckv, dkp, gq, gkv, tabs, *, tm=256):
    def body(dn_ref, dcq_ref, dckv_ref, dkp_ref, gq_ref, gkv_ref, c_ref, s1_ref, s2_ref, dd_ref, dgq_ref, dgkv_ref):
        i = pl.program_id(0)

        def rms_bwd(x, dy, g, dg_ref):
            xh, rstd = _rms(x, None)
            dxh = dy * g
            dx = rstd * (dxh - xh * jnp.mean(dxh * xh, axis=-1, keepdims=True))
            pg = jnp.sum(dy * xh, axis=0, keepdims=True)

            @pl.when(i == 0)
            def _():
                dg_ref[...] = pg

            @pl.when(i > 0)
            def _():
                dg_ref[...] += pg

            return dx

        dxq = rms_bwd(dn_ref[:, :Q_RANK], dcq_ref[...], gq_ref[...], dgq_ref)
        dd_ref[:, :Q_RANK] = dxq.astype(dd_ref.dtype)
        dxkv = rms_bwd(dn_ref[:, Q_RANK:Q_RANK + KV_RANK], dckv_ref[...], gkv_ref[...], dgkv_ref)
        dd_ref[:, Q_RANK:Q_RANK + KV_RANK] = dxkv.astype(dd_ref.dtype)
        dd_ref[:, Q_RANK + KV_RANK:] = _unrope(dkp_ref[...], c_ref[...], s1_ref[...], s2_ref[...]).astype(dd_ref.dtype)

    tab = pl.BlockSpec((tm, HEAD), lambda i: (i, 0))
    vq = pl.BlockSpec((1, Q_RANK), lambda i: (0, 0))
    vkv = pl.BlockSpec((1, KV_RANK), lambda i: (0, 0))
    return pl.pallas_call(
        body, name=name, grid=(S // tm,),
        in_specs=[pl.BlockSpec((tm, ODD_IN_PAD), lambda i: (i, 0)), pl.BlockSpec((tm, Q_RANK), lambda i: (i, 0)),
                  pl.BlockSpec((tm, KV_RANK), lambda i: (i, 0)), tab, vq, vkv, tab, tab, tab],
        out_specs=[pl.BlockSpec((tm, ODD_IN_PAD), lambda i: (i, 0)), vq, vkv],
        out_shape=[jax.ShapeDtypeStruct((S, ODD_IN_PAD), BF16), jax.ShapeDtypeStruct((1, Q_RANK), F32),
                   jax.ShapeDtypeStruct((1, KV_RANK), F32)],
        compiler_params=_params(("arbitrary",)),
    )(down, dcq, dckv, dkp, gq, gkv, *tabs)


ATT_TQ = 256


def _attn_probs(q_ref, kv_ref, kp_ref, c_ref, s1_ref, s2_ref, i, nk):
    qn = q_ref[:, :HEAD].astype(BF16)
    qp = _rope(q_ref[:, HEAD:], c_ref[...], s1_ref[...], s2_ref[...]).astype(BF16)
    kn = kv_ref[:nk, :HEAD]
    sc = (_dot(qn, kn, _NT) + _dot(qp, kp_ref[:nk, :], _NT)) * ATT_SCALE
    q_chunk = (i * ATT_TQ + lax.broadcasted_iota(jnp.int32, sc.shape, 0)) // CHUNK
    k_chunk = lax.broadcasted_iota(jnp.int32, sc.shape, 1) // CHUNK
    sc = jnp.where(k_chunk <= q_chunk, sc, jnp.finfo(F32).min)
    e = jnp.exp(sc - jnp.max(sc, axis=-1, keepdims=True))
    p = e * (1.0 / jnp.sum(e, axis=-1, keepdims=True))
    return p, qn, qp, kn


def _for_each_prefix(i, fn):
    for k in range(S // ATT_TQ):
        pl.when(i == k)(functools.partial(fn, (k + 1) * ATT_TQ))


def _attn_specs():
    q = pl.BlockSpec((ATT_TQ, QHEAD_PAD), lambda h, i: (i, h))
    kv = pl.BlockSpec((S, QHEAD_PAD), lambda h, i: (0, h))
    kp = pl.BlockSpec((S, HEAD), lambda h, i: (0, 0))
    tab = pl.BlockSpec((ATT_TQ, HEAD), lambda h, i: (i, 0))
    o = pl.BlockSpec((ATT_TQ, HEAD), lambda h, i: (i, h))
    return q, kv, kp, tab, o


def attn_fwd(name, q, kv, kp, tabs):
    def body(q_ref, kv_ref, kp_ref, c_ref, s1_ref, s2_ref, o_ref):
        i = pl.program_id(1)

        def run(nk):
            p, _, _, _ = _attn_probs(q_ref, kv_ref, kp_ref, c_ref, s1_ref, s2_ref, i, nk)
            o_ref[...] = _dot(p, kv_ref[:nk, HEAD:], _NN).astype(o_ref.dtype)

        _for_each_prefix(i, run)

    qs, kvs, kps, tab, os = _attn_specs()
    return pl.pallas_call(
        body, name=name, grid=(MLA_HEADS, S // ATT_TQ), in_specs=[qs, kvs, kps, tab, tab, tab], out_specs=os,
        out_shape=jax.ShapeDtypeStruct((S, MLA_HEADS * HEAD), BF16),
        compiler_params=_params(("parallel", "parallel")),
    )(q, kv, kp, *tabs)


def attn_bwd(name, q, kv, kp, do, tabs):
    def body(q_ref, kv_ref, kp_ref, do_ref, c_ref, s1_ref, s2_ref, dq_ref, dkv_ref, dkp_ref):
        h, i = pl.program_id(0), pl.program_id(1)

        @pl.when(i == 0)
        def _():
            dkv_ref[...] = jnp.zeros_like(dkv_ref)

        @pl.when((i == 0) & (h == 0))
        def _():
            dkp_ref[...] = jnp.zeros_like(dkp_ref)

        def run(nk):
            p, qn, qp, kn = _attn_probs(q_ref, kv_ref, kp_ref, c_ref, s1_ref, s2_ref, i, nk)
            do_v = do_ref[...]
            dp = _dot(do_v, kv_ref[:nk, HEAD:], _NT)
            ds = (p * (dp - jnp.sum(p * dp, axis=-1, keepdims=True)) * ATT_SCALE).astype(BF16)
            dq_ref[:, :HEAD] = _dot(ds, kn, _NN).astype(dq_ref.dtype)
            dqp = _unrope(_dot(ds, kp_ref[:nk, :], _NN), c_ref[...], s1_ref[...], s2_ref[...])
            dq_ref[:, HEAD:] = dqp.astype(dq_ref.dtype)
            dkv_ref[:nk, :HEAD] += _dot(ds, qn, _TN)
            dkv_ref[:nk, HEAD:] += _dot(p, do_v, _TN)
            dkp_ref[:nk, :] += _dot(ds, qp, _TN)

        _for_each_prefix(i, run)

    qs, kvs, kps, tab, os = _attn_specs()
    return pl.pallas_call(
        body, name=name, grid=(MLA_HEADS, S // ATT_TQ), in_specs=[qs, kvs, kps, os, tab, tab, tab],
        out_specs=[qs, kvs, kps],
        out_shape=[jax.ShapeDtypeStruct((S, MLA_HEADS * QHEAD_PAD), BF16),
                   jax.ShapeDtypeStruct((S, MLA_HEADS * QHEAD_PAD), F32), jax.ShapeDtypeStruct((S, HEAD), F32)],
        compiler_params=_params(("arbitrary", "arbitrary")),
    )(q, kv, kp, do, *tabs)


def adamw(name, w, g, m, v):
    rows, cols = w.shape
    tr = rows
    for cand in (512, 256, 128, 64, 32, 16, 8):
        if rows % cand == 0 and cand * cols * 4 <= 2 * 1024 * 1024:
            tr = cand
            break

    def body(w_ref, g_ref, m_ref, v_ref, d_ref, nm_ref, nv_ref):
        g_v = g_ref[...]
        nm = ADAM_B1 * m_ref[...] + (1.0 - ADAM_B1) * g_v
        nv = ADAM_B2 * v_ref[...] + (1.0 - ADAM_B2) * (g_v * g_v)
        m_hat = nm / (1.0 - ADAM_B1 ** ADAM_STEP)
        v_hat = nv / (1.0 - ADAM_B2 ** ADAM_STEP)
        d_ref[...] = (-ADAM_LR) * (m_hat / (jnp.sqrt(v_hat) + ADAM_EPS) + ADAM_WD * w_ref[...])
        nm_ref[...] = nm
        nv_ref[...] = nv

    blk = pl.BlockSpec((tr, cols), lambda i: (i, 0))
    shape = jax.ShapeDtypeStruct((rows, cols), F32)
    return pl.pallas_call(
        body, name=name, grid=(rows // tr,), in_specs=[blk] * 4, out_specs=[blk] * 3, out_shape=[shape] * 3,
        compiler_params=_params(("parallel",)),
    )(w, g, m, v)


def _local_step(x, pos_col, target, sm, weights_of, emit_grads, prefetch):
    inv_freq = ROPE_THETA ** (-jnp.arange(0, QK_ROPE, 2, dtype=F32) / QK_ROPE)
    inv_freq = jnp.concatenate([inv_freq, inv_freq, jnp.zeros((HEAD - QK_ROPE,), F32)])[None, :]
    tabs = rope_tables(pos_col, inv_freq)
    saved, wts = [], {}
    for layer in range(DEPTH):
        j = layer // 2
        n = "l%d_" % layer
        sv = {"x": x}
        wm = wts["mix%d" % layer] = weights_of("mix%d" % layer, x)
        if layer == 0:
            sm = dict(sm, conv_w=wm["conv_w"], gq=wm["gq"], gkv=wm["gkv"])
        if layer % 2 == 0:
            proj = mm_cols(n + "proj", x, wm["w_in"], [F32])[0]
            prefetch("mlp%d" % layer, proj)
            y_pool = pool_fwd(n + "pool", proj, sm["pool_w"][j], sm["pool_scale"][j][None])
            y_lru, hstate = lru_fwd(n + "lru", proj, sm["conv_w"][j], sm["conv_b"][j][None], sm["w_a"][j],
                                    sm["b_a"][j][None], sm["w_x"][j], sm["b_x"][j][None], sm["lam"][j][None])
            ycat = jnp.concatenate([y_pool, y_lru], axis=1)
            mix_in, w_mix = ycat, wm["w_out"]
            sv.update(proj=proj, hstate=hstate, ycat=ycat)
        else:
            down = mm_rows(n + "down", x, wm["w_down"], F32)
            prefetch("mlp%d" % layer, down)
            cq, ckv, kp = mla_prep(n + "prep", down, sm["gq"][j][None], sm["gkv"][j][None], tabs)
            q = mm_cols(n + "q", cq, wm["w_qb"], [F32])[0]
            kv = mm_cols(n + "kv", ckv, wm["w_kvb"], [BF16])[0]
            o = attn_fwd(n + "attn", q, kv, kp, tabs)
            mix_in, w_mix = o, wm["w_o"]
            sv.update(down=down, cq=cq, ckv=ckv, kp=kp, q=q, kv=kv, o=o)
        x1, xhat1, rstd1 = mm_rows_ln(n + "mixout", mix_in, w_mix, x, sm["ln_mix_g"][layer][None],
                                      sm["ln_mix_b"][layer][None])
        wf = wts["mlp%d" % layer] = weights_of("mlp%d" % layer, x1)
        x2, xhat2, rstd2, relu = mlp_fwd(n + "mlp", x1, wf["w1"], wf["w2"], sm["ln_ffn_g"][layer][None],
                                         sm["ln_ffn_b"][layer][None])
        sv.update(xhat1=xhat1, rstd1=rstd1, x1=x1, relu=relu, xhat2=xhat2, rstd2=rstd2)
        saved.append(sv)
        x = x2

    dx, loss_tile = loss_head(x, target)
    gs = {k: [None] * (DEPTH if k.startswith("ln_") else DEPTH // 2) for k in sm}
    for layer in reversed(range(DEPTH)):
        j = layer // 2
        n = "l%d_" % layer
        sv = saved[layer]
        wm, wf = wts["mix%d" % layer], wts["mlp%d" % layer]
        dz, gs["ln_ffn_g"][layer], gs["ln_ffn_b"][layer] = ln_bwd(
            n + "ln_ffn_b", dx, sv["xhat2"], sv["rstd2"], sm["ln_ffn_g"][layer][None])
        dx, g_w1, g_w2 = mlp_bwd(n + "mlp_b", dz, sv["relu"], sv["x1"], wf["w1"], wf["w2"])
        sent = emit_grads("mlp%d" % layer, {"w1": g_w1, "w2": g_w2})
        dz, gs["ln_mix_g"][layer], gs["ln_mix_b"][layer] = ln_bwd(
            n + "ln_mix_b", dx, sv["xhat1"], sv["rstd1"], sm["ln_mix_g"][layer][None], after=sent)
        if layer % 2 == 0:
            dycat = mm_t_rows(n + "mixout_dx", dz, wm["w_out"], F32)
            g_out = mm_grad(n + "mixout_dw", sv["ycat"], dz, a_sharded=True)
            du_pool, gs["pool_w"][j], gs["pool_scale"][j] = pool_bwd(
                n + "pool_b", sv["proj"], dycat, sm["pool_w"][j], sm["pool_scale"][j][None])
            (du_lru, du_gate, gs["w_a"][j], gs["w_x"][j], gs["b_a"][j], gs["b_x"][j], gs["lam"][j], gs["conv_w"][j],
             gs["conv_b"][j]) = lru_bwd(n + "lru_b", sv["proj"], sv["hstate"], dycat, sm["conv_w"][j],
                                        sm["conv_b"][j][None], sm["w_a"][j], sm["b_a"][j][None], sm["w_x"][j],
                                        sm["b_x"][j][None], sm["lam"][j][None])
            dproj = jnp.concatenate([du_pool, du_lru, du_gate], axis=1)
            g_in = mm_grad(n + "proj_dw", sv["x"], dproj, a_sharded=False)
            sent = emit_grads("mix%d" % layer, {"w_in": g_in, "w_out": g_out})
            dx = mm_t_cols(n + "proj_dx", dproj, wm["w_in"], F32, resid=dz, after=sent)
        else:
            do = mm_t_rows(n + "attnout_dx", dz, wm["w_o"], BF16)
            g_o = mm_grad(n + "attnout_dw", sv["o"], dz, a_sharded=True)
            dq, dkv, dkp = attn_bwd(n + "attn_b", sv["q"], sv["kv"], sv["kp"], do, tabs)
            g_qb = mm_grad(n + "q_dw", sv["cq"], dq, a_sharded=False)
            dcq = mm_t_cols(n + "q_dx", dq, wm["w_qb"], F32)
            g_kvb = mm_grad(n + "kv_dw", sv["ckv"], dkv, a_sharded=False)
            dckv = mm_t_cols(n + "kv_dx", dkv, wm["w_kvb"], F32)
            ddown, gs["gq"][j], gs["gkv"][j] = mla_prep_bwd(
                n + "prep_b", sv["down"], dcq, dckv, dkp, sm["gq"][j][None], sm["gkv"][j][None], tabs)
            g_down = mm_grad(n + "down_dw", sv["x"], ddown, a_sharded=True)
            sent = emit_grads("mix%d" % layer, {"w_down": g_down, "w_qb": g_qb, "w_kvb": g_kvb, "w_o": g_o})
            dx = mm_t_rows(n + "down_dx", ddown, wm["w_down"], F32, resid=dz, after=sent)
    gs = {k: jnp.stack([a.reshape(sm[k].shape[1:]) for a in v]) for k, v in gs.items()}
    return loss_tile[0, 0], dx, gs


def _place():
    x, y, c = lax.axis_index("x"), lax.axis_index("y"), lax.axis_index("c")
    chips = [(1 - x, y), (x, 1 - y), (1 - x, 1 - y)]
    return x, y, c, chips


def _hbm_call(body, name, args, out_shape, scratch, aliases=None):
    return pl.pallas_call(
        body, name=name, in_specs=[pl.BlockSpec(memory_space=pl.ANY)] * len(args),
        out_specs=[pl.BlockSpec(memory_space=pl.ANY)] * len(out_shape), out_shape=out_shape,
        scratch_shapes=scratch, input_output_aliases=aliases or {},
        compiler_params=pltpu.CompilerParams(has_side_effects=True),
    )(*args)


HBM_SPEC = pl.BlockSpec(memory_space=pltpu.HBM)
SEM_SPEC = pl.BlockSpec(memory_space=pltpu.SEMAPHORE)
EFFECT = pltpu.SideEffectType.DATAFLOW_SIDE_EFFECTING


def _remote(src, dst, send_sem, recv_sem, device):
    return pltpu.make_async_remote_copy(src_ref=src, dst_ref=dst, send_sem=send_sem, recv_sem=recv_sem,
                                        device_id=device, device_id_type=MESH)


def place_own(name, srcs, place, steps, reducing):
    n = len(srcs)
    in_specs, out_specs, out_shape = [], [], []
    for s in srcs:
        if reducing:
            nd, rows, cols = s.shape
            tr = rows // 2 // steps
            pick = (lambda i, p: (p[0], p[1] * steps + i, 0)) if nd == N_CHIPS else (lambda i, p: (0, p[1] * steps + i, 0))
            in_specs.append(pl.BlockSpec((None, tr, cols), pick))
            out_specs.append(pl.BlockSpec((None, tr, cols), lambda i, p: (p[2], i, 0)))
            out_shape.append(jax.ShapeDtypeStruct((N_DEV, rows // 2, cols), s.dtype))
        else:
            rows, cols = s.shape
            tr = rows // steps
            in_specs.append(pl.BlockSpec((tr, cols), lambda i, p: (i, 0)))
            out_specs.append(pl.BlockSpec((None, tr, cols), lambda i, p: (p[0], i, 0)))
            out_shape.append(jax.ShapeDtypeStruct((N_CHIPS, rows, cols), s.dtype))

    def body(p_ref, *refs):
        for i_ref, o_ref in zip(refs[:n], refs[n:]):
            o_ref[...] = i_ref[...]

    return pl.pallas_call(
        body, name=name, out_shape=out_shape,
        grid_spec=pltpu.PrefetchScalarGridSpec(num_scalar_prefetch=1, grid=(steps,), in_specs=in_specs,
                                               out_specs=out_specs),
        compiler_params=_params(("parallel",)),
    )(place, *srcs)


def split_start(name, srcs, lands, n_sems, plan, token_in=None):
    n = len(srcs)
    srcs = [pltpu.with_memory_space_constraint(s, pltpu.HBM) for s in srcs]
    lands = [pltpu.with_memory_space_constraint(a, pltpu.HBM) for a in lands]
    extra = [] if token_in is None else [token_in]

    def body(*refs):
        src_refs, land_refs = refs[:n], refs[n:2 * n]
        outs = refs[2 * n + len(extra):]
        send_sems, recv_sems, token = outs[0], outs[1], outs[2 + 2 * n]
        sends, _ = plan(src_refs, land_refs, send_sems, recv_sems)
        for cp in sends:
            cp.start()
        token[...] = jnp.zeros_like(token)

    out_shape = (pltpu.SemaphoreType.DMA((n * n_sems,)), pltpu.SemaphoreType.DMA((n * n_sems,)),
                 *[pltpu.HBM(a.shape, a.dtype) for a in srcs + lands], jax.ShapeDtypeStruct((8, 128), F32))
    res = pl.pallas_call(
        body, name=name, out_shape=out_shape,
        in_specs=[HBM_SPEC] * (2 * n) + [pl.BlockSpec(memory_space=pl.ANY)] * len(extra),
        out_specs=(SEM_SPEC, SEM_SPEC, *[HBM_SPEC] * (2 * n), pl.BlockSpec(memory_space=pltpu.VMEM)),
        input_output_aliases={i: 2 + i for i in range(2 * n)},
        compiler_params=pltpu.CompilerParams(has_side_effects=EFFECT),
    )(*srcs, *lands, *extra)
    return dict(send=res[0], recv=res[1], srcs=list(res[2:2 + n]), lands=list(res[2 + n:2 + 2 * n]), plan=plan), res[-1]


def _wait_started(st, bufs, send_sems, recv_sems):
    n = len(st["srcs"])
    sends, expects = st["plan"](bufs[:n], bufs[n:], send_sems, recv_sems)
    for cp in sends:
        cp.wait_send()
    for cp in expects:
        cp.wait_recv()


def split_wait(name, started, after):
    sizes = [len(st["srcs"]) + len(st["lands"]) for st in started]
    n_buf = sum(sizes)

    def body(*refs):
        bufs, sems = refs[:n_buf], refs[n_buf:n_buf + 2 * len(started)]
        off = 0
        for g, (st, n) in enumerate(zip(started, sizes)):
            _wait_started(st, bufs[off:off + n], sems[2 * g], sems[2 * g + 1])
            off += n

    bufs = [a for st in started for a in st["srcs"] + st["lands"]]
    sems = [s for st in started for s in (st["send"], st["recv"])]
    res = pl.pallas_call(
        body, name=name, out_shape=tuple(pltpu.HBM(a.shape, a.dtype) for a in bufs),
        in_specs=[HBM_SPEC] * n_buf + [SEM_SPEC] * len(sems) + [pl.BlockSpec(memory_space=pl.ANY)],
        out_specs=tuple([HBM_SPEC] * n_buf), input_output_aliases={i: i for i in range(n_buf)},
        compiler_params=pltpu.CompilerParams(has_side_effects=EFFECT),
    )(*bufs, *sems, after)
    out, off = [], 0
    for st, n in zip(started, sizes):
        out.append(list(res[off + len(st["srcs"]):off + n]))
        off += n
    return out


def split_relay(name, st, n_sems, plan, after):
    n_src, n = len(st["srcs"]), len(st["lands"])

    def body(*refs):
        bufs = refs[:n_src + n]
        outs = refs[n_src + n + 3:]
        _wait_started(st, bufs, refs[n_src + n], refs[n_src + n + 1])
        sends, _ = plan((), bufs[n_src:], outs[0], outs[1])
        for cp in sends:
            cp.start()
        outs[2 + n][...] = jnp.zeros((8, 128), F32)

    bufs = st["srcs"] + st["lands"]
    res = pl.pallas_call(
        body, name=name,
        out_shape=(pltpu.SemaphoreType.DMA((n * n_sems,)), pltpu.SemaphoreType.DMA((n * n_sems,)),
                   *[pltpu.HBM(a.shape, a.dtype) for a in st["lands"]], jax.ShapeDtypeStruct((8, 128), F32)),
        in_specs=[HBM_SPEC] * (n_src + n) + [SEM_SPEC] * 2 + [pl.BlockSpec(memory_space=pl.ANY)],
        out_specs=(SEM_SPEC, SEM_SPEC, *[HBM_SPEC] * n, pl.BlockSpec(memory_space=pltpu.VMEM)),
        input_output_aliases={n_src + i: 2 + i for i in range(n)},
        compiler_params=pltpu.CompilerParams(has_side_effects=EFFECT),
    )(*bufs, st["send"], st["recv"], after)
    return dict(send=res[0], recv=res[1], srcs=[], lands=list(res[2:2 + n]), plan=plan), res[-1]


def gather_plan(src_refs, land_refs, send_sems, recv_sems):
    x, y, c, chips = _place()
    me = 2 * x + y
    sends, expects = [], []
    for k, (s, d) in enumerate(zip(src_refs, land_refs)):
        mine = pl.ds(c * (s.shape[0] // 2), s.shape[0] // 2)
        for j, (px, py) in enumerate(chips):
            sem = 3 * k + j
            sends.append(_remote(s.at[mine], d.at[me, mine], send_sems.at[sem], recv_sems.at[sem], (px, py, c)))
            expects.append(_remote(s.at[mine], d.at[2 * px + py, mine], send_sems.at[sem], recv_sems.at[sem], (px, py, c)))
    return sends, expects


def relay_plan(src_refs, land_refs, send_sems, recv_sems):
    x, y, c, chips = _place()
    sends, expects = [], []
    for k, d in enumerate(land_refs):
        hr = d.shape[1] // 2
        mine, theirs = pl.ds(c * hr, hr), pl.ds((1 - c) * hr, hr)
        for j, (px, py) in enumerate(chips):
            sem, chip = 3 * k + j, 2 * px + py
            sends.append(_remote(d.at[chip, mine], d.at[chip, mine], send_sems.at[sem], recv_sems.at[sem], (x, y, 1 - c)))
            expects.append(_remote(d.at[chip, mine], d.at[chip, theirs], send_sems.at[sem], recv_sems.at[sem], (x, y, 1 - c)))
    return sends, expects


def _reduce_part(s, chip_idx, h):
    hr = s.shape[1] // 2
    return s.at[chip_idx if s.shape[0] == N_CHIPS else 0, pl.ds(h * hr, hr)]


def reduce_plan(src_refs, land_refs, send_sems, recv_sems):
    x, y, c, chips = _place()
    me_chip, me_dev = 2 * x + y, 4 * x + 2 * y + c
    sends, expects = [], []
    for k, (s, d) in enumerate(zip(src_refs, land_refs)):
        part = functools.partial(_reduce_part, s)
        for j, (px, py) in enumerate(chips):
            for h in range(2):
                sends.append(_remote(part(2 * px + py, h), d.at[me_dev], send_sems.at[7 * k + 2 * j + h],
                                     recv_sems.at[7 * k + 2 * j + c], (px, py, h)))
                expects.append(_remote(part(me_chip, c), d.at[4 * px + 2 * py + h], send_sems.at[7 * k + 2 * j + h],
                                       recv_sems.at[7 * k + 2 * j + h], (px, py, h)))
        sends.append(_remote(part(me_chip, 1 - c), d.at[me_dev], send_sems.at[7 * k + 6], recv_sems.at[7 * k + 6],
                             (x, y, 1 - c)))
        expects.append(_remote(part(me_chip, c), d.at[me_dev + 1 - 2 * c], send_sems.at[7 * k + 6],
                               recv_sems.at[7 * k + 6], (x, y, 1 - c)))
    return sends, expects


def sibling_swap_halves(name, fulls):
    n = len(fulls)

    def body(*refs):
        outs = refs[n:2 * n]
        send_sems, recv_sems = refs[2 * n:]
        x, y, c, _ = _place()
        copies = []
        for k in range(n):
            nl, rows = fulls[k].shape[:2]
            hr = rows // 2
            mine = outs[k].at[pl.ds(0, nl), pl.ds(c * hr, hr)]
            theirs = outs[k].at[pl.ds(0, nl), pl.ds((1 - c) * hr, hr)]
            copies.append((_remote(mine, mine, send_sems.at[k], recv_sems.at[k], (x, y, 1 - c)),
                           _remote(mine, theirs, send_sems.at[k], recv_sems.at[k], (x, y, 1 - c))))
        for send, _ in copies:
            send.start()
        for send, recv in copies:
            send.wait_send()
            recv.wait_recv()

    out_shape = [jax.ShapeDtypeStruct(f.shape, f.dtype) for f in fulls]
    scratch = [pltpu.SemaphoreType.DMA((n,)), pltpu.SemaphoreType.DMA((n,))]
    return _hbm_call(body, name, fulls, out_shape, scratch, aliases={k: k for k in range(n)})


def _tile_rows(rows, cols, budget_bytes):
    best = None
    for t in range(16, rows + 1, 16):
        if rows % t == 0 and t * cols * 4 <= budget_bytes:
            best = t
    assert best is not None, (rows, cols)
    return best


N_DEV = 8


def sum_devices(name, landed, layer, n_layers, prev, c_arr):
    _, hr, cols = landed.shape
    tr = _tile_rows(hr, cols, 512 * 1024)
    n_blk = hr // tr

    def body(c_ref, r_ref, *rest):
        o_ref = rest[-1]
        acc = r_ref[0].astype(F32)
        for d in range(1, N_DEV):
            acc = acc + r_ref[d].astype(F32)
        o_ref[...] = acc

    in_specs = [pl.BlockSpec((N_DEV, tr, cols), lambda r, c_ref: (0, r, 0))]
    args = [c_arr, landed]
    aliases = {}
    if prev is not None:
        in_specs.append(pl.BlockSpec(memory_space=pl.ANY))
        args.append(prev)
        aliases = {2: 0}
    return pl.pallas_call(
        body, name=name, out_shape=jax.ShapeDtypeStruct((n_layers, 2 * hr, cols), F32),
        grid_spec=pltpu.PrefetchScalarGridSpec(
            num_scalar_prefetch=1, grid=(n_blk,), in_specs=in_specs,
            out_specs=pl.BlockSpec((None, tr, cols), lambda r, c_ref: (layer, c_ref[0] * n_blk + r, 0))),
        input_output_aliases=aliases, compiler_params=_params(("parallel",)),
    )(*args)


def _pack(arrs, rows_multiple):
    flat = []
    for a in arrs:
        v = a.reshape(-1).astype(F32)
        flat.append(jnp.pad(v, (0, (-v.shape[0]) % HEAD)))
    v = jnp.concatenate(flat)
    v = jnp.pad(v, (0, (-v.shape[0]) % (HEAD * rows_multiple)))
    return v.reshape(-1, HEAD)


def _unpack(packed, shapes):
    flat = packed.reshape(-1)
    out, off = [], 0
    for shp in shapes:
        size = int(np.prod(shp))
        out.append(flat[off:off + size].reshape(shp))
        off += size + (-size) % HEAD
    return out


BIG = ["even_w_in", "even_w_out", "mla_w_down", "mla_w_qb", "mla_w_kvb", "mla_w_o", "mlp_w1", "mlp_w2"]
BIG_KEY = {"even_w_in": "w_in", "even_w_out": "w_out", "mla_w_down": "w_down", "mla_w_qb": "w_qb",
           "mla_w_kvb": "w_kvb", "mla_w_o": "w_o", "mlp_w1": "w1", "mlp_w2": "w2"}
SMALL_KEY = {"ln_mix_g": "ln_mix_g", "ln_mix_b": "ln_mix_b", "ln_ffn_g": "ln_ffn_g", "ln_ffn_b": "ln_ffn_b",
             "pool_w": "pool_w", "pool_scale": "pool_scale", "lru_conv_w": "conv_w", "lru_conv_b": "conv_b",
             "lru_w_a": "w_a", "lru_b_a": "b_a", "lru_w_x": "w_x", "lru_b_x": "b_x", "lru_lambda": "lam",
             "mla_q_norm_g": "gq", "mla_kv_norm_g": "gkv"}
SMALL = list(SMALL_KEY)
SMALL_SHARDED = ["lru_conv_w", "mla_q_norm_g", "mla_kv_norm_g"]
WEIGHTS = ["ln_mix_g", "ln_mix_b", "ln_ffn_g", "ln_ffn_b", "even_w_in", "pool_w", "pool_scale", "lru_conv_w",
           "lru_conv_b", "lru_w_a", "lru_b_a", "lru_w_x", "lru_b_x", "lru_lambda", "even_w_out", "mla_w_down",
           "mla_q_norm_g", "mla_kv_norm_g", "mla_w_qb", "mla_w_kvb", "mla_w_o", "mlp_w1", "mlp_w2"]


GROUPS = ["mix0", "mlp0", "mix1", "mlp1", "mix2", "mlp2", "mix3", "mlp3"]


def _group_keys(group):
    layer = int(group[3:])
    if group.startswith("mlp"):
        return [("mlp_w1", "w1", layer), ("mlp_w2", "w2", layer)]
    if layer % 2 == 0:
        return [("even_w_in", "w_in", layer // 2), ("even_w_out", "w_out", layer // 2)]
    return [("mla_w_down", "w_down", layer // 2), ("mla_w_qb", "w_qb", layer // 2),
            ("mla_w_kvb", "w_kvb", layer // 2), ("mla_w_o", "w_o", layer // 2)]


def _pad_q_heads(w):
    lead = w.shape[:-1]
    w = w.reshape(lead + (2, QK_NOPE + QK_ROPE))
    w = jnp.pad(w, ((0, 0),) * len(lead) + ((0, 0), (0, QHEAD_PAD - QK_NOPE - QK_ROPE)))
    return w.reshape(lead + (2 * QHEAD_PAD,))


def _unpad_q_heads(g):
    lead = g.shape[:-1]
    return g.reshape(lead + (2, QHEAD_PAD))[..., :QK_NOPE + QK_ROPE].reshape(lead + (2 * (QK_NOPE + QK_ROPE),))


def _step(x, positions, loss_target, w, m, v):
    cx, cy, cc = lax.axis_index("x"), lax.axis_index("y"), lax.axis_index("c")
    chip = 2 * cx + cy
    c_arr = jnp.reshape(cc, (1,)).astype(jnp.int32)
    place = jnp.stack([chip, cc, 2 * chip + cc]).astype(jnp.int32)

    prepared = dict(w)
    prepared["mla_w_down"] = jnp.pad(w["mla_w_down"], ((0, 0), (0, 0), (0, ODD_IN_PAD - ODD_IN)))
    prepared["mla_w_qb"] = _pad_q_heads(w["mla_w_qb"])
    small_shard_shapes = [w[k].shape for k in SMALL_SHARDED]
    sources, zones = {}, {}
    for g in GROUPS:
        srcs = [prepared[name][idx].astype(BF16) for name, _, idx in _group_keys(g)]
        if g == GROUPS[0]:
            srcs.append(_pack([w[k] for k in SMALL_SHARDED], 32))
        sources[g] = srcs
        zones[g] = place_own("own_" + g, srcs, place, 4, False)
    gathering, token = {}, None
    for g in GROUPS:
        gathering[g], token = split_start("gather_" + g, sources[g], zones[g], 3, gather_plan, token)
    relayed, relay_token = {}, {}

    def prefetch(g, after):
        relayed[g], relay_token[g] = split_relay("relay_" + g, gathering[g], 3, relay_plan, after)

    prefetch(GROUPS[0], token)

    def weights_of(g, after):
        lands = split_wait("gathered_" + g, [relayed[g]], relay_token[g] if g == GROUPS[0] else after)[0]
        if g.startswith("mlp") and g != GROUPS[-1]:
            prefetch(GROUPS[GROUPS.index(g) + 1], lands[0])
        out = {key: land for (_, key, _), land in zip(_group_keys(g), lands)}
        if g == GROUPS[0]:
            per_chip = [_unpack(lands[-1][s], small_shard_shapes) for s in range(N_CHIPS)]
            for i, key in enumerate(("conv_w", "gq", "gkv")):
                out[key] = jnp.concatenate([p[i] for p in per_chip], axis=-1)
        return out

    reducing = []

    def emit_grads(g, grads):
        srcs = [grads[key] for _, key, _ in _group_keys(g)]
        lands = place_own("mine_" + g, srcs, place, 4, True)
        started, token = split_start("reduce_" + g, srcs, lands, 7, reduce_plan)
        reducing.append((g, started))
        return token

    sm = {SMALL_KEY[k]: w[k] for k in SMALL if k not in SMALL_SHARDED}
    loss, grad_x, gs = _local_step(x[0], positions.reshape(S, 1), loss_target[0], sm, weights_of, emit_grads, prefetch)
    loss = lax.psum(loss, ("x", "y", "c"))

    small_shapes = [gs[SMALL_KEY[k]].shape for k in SMALL]
    gs_pack = _pack([gs[SMALL_KEY[k]] for k in SMALL], 32)[None]
    steps = gs_pack.shape[1] // 2 // _tile_rows(gs_pack.shape[1] // 2, HEAD, 512 * 1024)
    lands = place_own("mine_small", [gs_pack], place, steps, True)
    started, small_sent = split_start("reduce_small", [gs_pack], lands, 7, reduce_plan, grad_x)
    reducing.append(("small", started))

    grad, delta, new_m, new_v = {}, {}, {}, {}
    late_groups = ("mix0", "small")
    stacks, after = {}, small_sent
    for late in (False, True):
        part = [(g, st) for g, st in reducing if (g in late_groups) == late]
        landed = split_wait("reduced_late" if late else "reduced_early", [st for _, st in part], after)
        for (g, _), lands in zip(part, landed):
            if g == "small":
                stacks["small"] = sum_devices("sum_small", lands[0], 0, 1, None, c_arr)
                continue
            for (name, key, idx), land in zip(_group_keys(g), lands):
                stacks[name] = sum_devices("sum_%s%d" % (key, idx), land, idx, w[name].shape[0], stacks.get(name), c_arr)
        names = [k for k in BIG if (k in ("even_w_in", "even_w_out")) == late] + (["small"] if late else [])
        reduced = dict(zip(names, sibling_swap_halves("swap_late" if late else "swap_early", [stacks[k] for k in names])))
        if not late:
            reduced["mla_w_down"] = reduced["mla_w_down"][..., :ODD_IN]
            reduced["mla_w_qb"] = _unpad_q_heads(reduced["mla_w_qb"])
        for k in names:
            if k == "small":
                continue
            grad[k] = reduced[k]
            shp = w[k].shape
            view = lambda a: a.reshape(-1, shp[-1])
            d, nm, nv = adamw("adamw_" + BIG_KEY[k], view(w[k]), view(grad[k]), view(m[k]), view(v[k]))
            delta[k], new_m[k], new_v[k] = d.reshape(shp), nm.reshape(shp), nv.reshape(shp)
            after = d
    g_small = dict(zip(SMALL, _unpack(reduced["small"], small_shapes)))
    for k in SMALL_SHARDED:
        width = w[k].shape[-1]
        g_small[k] = lax.dynamic_slice_in_dim(g_small[k], chip * width, width, axis=-1)
    grad.update(g_small)

    shapes = [w[k].shape for k in SMALL]
    d, nm, nv = adamw("adamw_small", *[_pack([t[k] for k in SMALL], 512) for t in (w, grad, m, v)])
    for k, dk, mk, vk in zip(SMALL, _unpack(d, shapes), _unpack(nm, shapes), _unpack(nv, shapes)):
        delta[k], new_m[k], new_v[k] = dk, mk, vk
    return (loss, grad_x[None], *[grad[k] for k in WEIGHTS], *[delta[k] for k in WEIGHTS],
            *[new_m[k] for k in WEIGHTS], *[new_v[k] for k in WEIGHTS])


def kernel(x, positions, ln_mix_g, ln_mix_b, ln_ffn_g, ln_ffn_b, even_w_in, pool_w, pool_scale, lru_conv_w, lru_conv_b, lru_w_a, lru_b_a, lru_w_x, lru_b_x, lru_lambda, even_w_out, mla_w_down, mla_q_norm_g, mla_kv_norm_g, mla_w_qb, mla_w_kvb, mla_w_o, mlp_w1, mlp_w2, loss_target, m_ln_mix_g, m_ln_mix_b, m_ln_ffn_g, m_ln_ffn_b, m_even_w_in, m_pool_w, m_pool_scale, m_lru_conv_w, m_lru_conv_b, m_lru_w_a, m_lru_b_a, m_lru_w_x, m_lru_b_x, m_lru_lambda, m_even_w_out, m_mla_w_down, m_mla_q_norm_g, m_mla_kv_norm_g, m_mla_w_qb, m_mla_w_kvb, m_mla_w_o, m_mlp_w1, m_mlp_w2, v_ln_mix_g, v_ln_mix_b, v_ln_ffn_g, v_ln_ffn_b, v_even_w_in, v_pool_w, v_pool_scale, v_lru_conv_w, v_lru_conv_b, v_lru_w_a, v_lru_b_a, v_lru_w_x, v_lru_b_x, v_lru_lambda, v_even_w_out, v_mla_w_down, v_mla_q_norm_g, v_mla_kv_norm_g, v_mla_w_qb, v_mla_w_kvb, v_mla_w_o, v_mlp_w1, v_mlp_w2):
    args = locals()
    w = {k: args[k] for k in WEIGHTS}
    m = {k: args["m_" + k] for k in WEIGHTS}
    v = {k: args["v_" + k] for k in WEIGHTS}
    return _step(x, positions, loss_target, w, m, v)
```

```python
import functools
import math

import jax
import jax.numpy as jnp
import numpy as np
from jax import lax
from jax.experimental import pallas as pl
from jax.experimental.pallas import tpu as pltpu

F32 = jnp.float32
BF16 = jnp.bfloat16

S = 2048
D = 1024
DEPTH = 4
N_CHIPS = 4
POOL_WINDOWS = (2, 4, 8, 16)
POOL_W = 512
LRU_W = 1024
LRU_HEADS = 8
HEAD = 128
EVEN_IN = 2560
EVEN_MIX = 1536
MLA_HEADS = 8
QK_NOPE = 128
QK_ROPE = 64
Q_RANK = 384
KV_RANK = 256
ODD_IN = 704
ODD_IN_PAD = 768
QHEAD_PAD = 256
D_FF = 4096
CHUNK = 64
ALPHA = (2 * DEPTH) ** 0.25
LN_EPS = 1e-5
RMS_EPS = 1e-6
ATT_SCALE = (QK_NOPE + QK_ROPE) ** -0.5
ROPE_THETA = 10000.0
LRU_C = 8.0
ADAM_LR = 0.001
ADAM_B1 = 0.9
ADAM_B2 = 0.999
ADAM_EPS = 1e-08
ADAM_WD = 0.01
ADAM_STEP = 10

VMEM_LIMIT = 56 * 1024 * 1024
MESH = pl.DeviceIdType.MESH

_NN = (((1,), (0,)), ((), ()))
_NT = (((1,), (1,)), ((), ()))
_TN = (((0,), (0,)), ((), ()))


def _params(sem=None, **kw):
    return pltpu.CompilerParams(dimension_semantics=sem, vmem_limit_bytes=VMEM_LIMIT, **kw)


def _dot(a, b, dims):
    return lax.dot_general(a.astype(BF16), b.astype(BF16), dims, preferred_element_type=F32)


def _mm(name, a, w4, *, dims, a_block, a_map, w_map, grid, sem, outs, out_blocks, out_maps,
        red_axis=None, n_red=1, extras=(), extra_blocks=(), extra_maps=(), epilogue=None, after=None):
    n_extra = len(extras)
    n_out = len(outs)
    n_after = 0 if after is None else 1
    w_block = (None,) + tuple(w4.shape[1:])

    def body(a_ref, w_ref, *rest):
        rest = rest[n_after:]
        extra_refs = rest[:n_extra]
        out_refs = rest[n_extra:n_extra + n_out]

        def finish(acc):
            vals = epilogue(acc, *[r[...] for r in extra_refs]) if epilogue else (acc,)
            for r, v in zip(out_refs, vals):
                r[...] = v.astype(r.dtype)

        if red_axis is None:
            finish(_dot(a_ref[...], w_ref[...], dims))
        else:
            acc_ref = rest[-1]
            k = pl.program_id(red_axis)

            @pl.when(k == 0)
            def _():
                acc_ref[...] = jnp.zeros_like(acc_ref)

            acc_ref[...] += _dot(a_ref[...], w_ref[...], dims)

            @pl.when(k == n_red - 1)
            def _():
                finish(acc_ref[...])

    scratch = [] if red_axis is None else [pltpu.VMEM(out_blocks[0], F32)]
    return pl.pallas_call(
        body, name=name, grid=grid,
        in_specs=[pl.BlockSpec(a_block, a_map), pl.BlockSpec(w_block, w_map)]
        + [pl.BlockSpec(memory_space=pl.ANY)] * n_after + [pl.BlockSpec(b, m) for b, m in zip(extra_blocks, extra_maps)],
        out_specs=[pl.BlockSpec(b, m) for b, m in zip(out_blocks, out_maps)],
        out_shape=outs, scratch_shapes=scratch, compiler_params=_params(sem),
    )(a, w4, *([after] * n_after), *extras)


def _w_map_outer(s, i):
    return (s, 0, 0)


def _w_map_inner(i, s):
    return (s, 0, 0)


def mm_cols(name, a, w4, out_dtypes, *, tm=512, epilogue=None):
    m, k = a.shape
    nb = w4.shape[2]
    outs = [jax.ShapeDtypeStruct((m, N_CHIPS * nb), dt) for dt in out_dtypes]
    return _mm(name, a, w4, dims=_NN, a_block=(tm, k), a_map=lambda s, i: (i, 0), w_map=_w_map_outer,
               grid=(N_CHIPS, m // tm), sem=("parallel", "parallel"), outs=outs,
               out_blocks=[(tm, nb)] * len(outs), out_maps=[lambda s, i: (i, s)] * len(outs), epilogue=epilogue)


def mm_rows(name, a, w4, out_dtype, *, tm=512):
    m = a.shape[0]
    kb, n = w4.shape[1:]
    outs = [jax.ShapeDtypeStruct((m, n), out_dtype)]
    return _mm(name, a, w4, dims=_NN, a_block=(tm, kb), a_map=lambda i, s: (i, s), w_map=_w_map_inner,
               grid=(m // tm, N_CHIPS), sem=("parallel", "arbitrary"), outs=outs, out_blocks=[(tm, n)],
               out_maps=[lambda i, s: (i, 0)], red_axis=1, n_red=N_CHIPS)[0]


def mm_rows_ln(name, a, w4, x, g, b, *, tm=512, after=None):
    m = a.shape[0]
    kb, n = w4.shape[1:]
    outs = [jax.ShapeDtypeStruct((m, n), F32), jax.ShapeDtypeStruct((m, n), F32), jax.ShapeDtypeStruct((m, 1), F32)]
    return _mm(name, a, w4, dims=_NN, a_block=(tm, kb), a_map=lambda i, s: (i, s), w_map=_w_map_inner,
               grid=(m // tm, N_CHIPS), sem=("parallel", "arbitrary"), outs=outs, out_blocks=[(tm, n), (tm, n), (tm, 1)],
               out_maps=[lambda i, s: (i, 0)] * 3, red_axis=1, n_red=N_CHIPS, extras=(x, g, b),
               extra_blocks=[(tm, n), (1, n), (1, n)],
               extra_maps=[lambda i, s: (i, 0), lambda i, s: (0, 0), lambda i, s: (0, 0)],
               epilogue=lambda acc, xv, gv, bv: _layer_norm(ALPHA * xv + acc, gv, bv), after=after)


def mm_t_cols(name, g, w4, out_dtype, *, tm=512, resid=None, after=None):
    m = g.shape[0]
    k, nb = w4.shape[1:]
    outs = [jax.ShapeDtypeStruct((m, k), out_dtype)]
    extras = () if resid is None else (resid,)
    epi = None if resid is None else (lambda acc, r: (acc + ALPHA * r,))
    return _mm(name, g, w4, dims=_NT, a_block=(tm, nb), a_map=lambda i, s: (i, s), w_map=_w_map_inner,
               grid=(m // tm, N_CHIPS), sem=("parallel", "arbitrary"), outs=outs, out_blocks=[(tm, k)],
               out_maps=[lambda i, s: (i, 0)], red_axis=1, n_red=N_CHIPS, extras=extras,
               extra_blocks=[(tm, k)] * len(extras), extra_maps=[lambda i, s: (i, 0)] * len(extras), epilogue=epi,
               after=after)[0]


def mm_t_rows(name, g, w4, out_dtype, *, tm=512, resid=None, gate=None, after=None):
    m, n = g.shape
    kb = w4.shape[1]
    outs = [jax.ShapeDtypeStruct((m, N_CHIPS * kb), out_dtype)]
    extras, epi = (), None
    if resid is not None:
        extras, epi = (resid,), (lambda acc, r: (acc + ALPHA * r,))
    if gate is not None:
        extras, epi = (gate,), (lambda acc, r: (acc * (2.0 * r.astype(F32)),))
    return _mm(name, g, w4, dims=_NT, a_block=(tm, n), a_map=lambda s, i: (i, 0), w_map=_w_map_outer,
               grid=(N_CHIPS, m // tm), sem=("parallel", "parallel"), outs=outs, out_blocks=[(tm, kb)],
               out_maps=[lambda s, i: (i, s)], extras=extras, extra_blocks=[(tm, kb)] * len(extras),
               extra_maps=[lambda s, i: (i, s)] * len(extras), epilogue=epi, after=after)[0]


def mm_grad(name, a, g, *, a_sharded, tm=512):
    m = a.shape[0]
    ka = a.shape[1] // N_CHIPS if a_sharded else a.shape[1]
    ng = g.shape[1] if a_sharded else g.shape[1] // N_CHIPS
    n_red = m // tm
    a_map = (lambda s, i: (i, s)) if a_sharded else (lambda s, i: (i, 0))
    g_map = (lambda s, i: (i, 0)) if a_sharded else (lambda s, i: (i, s))

    def body(a_ref, g_ref, out_ref, acc_ref):
        k = pl.program_id(1)

        @pl.when(k == 0)
        def _():
            acc_ref[...] = jnp.zeros_like(acc_ref)

        acc_ref[...] += _dot(a_ref[...], g_ref[...], _TN)

        @pl.when(k == n_red - 1)
        def _():
            out_ref[...] = acc_ref[...].astype(out_ref.dtype)

    return pl.pallas_call(
        body, name=name, grid=(N_CHIPS, n_red),
        in_specs=[pl.BlockSpec((tm, ka), a_map), pl.BlockSpec((tm, ng), g_map)],
        out_specs=pl.BlockSpec((None, ka, ng), lambda s, i: (s, 0, 0)),
        out_shape=jax.ShapeDtypeStruct((N_CHIPS, ka, ng), BF16),
        scratch_shapes=[pltpu.VMEM((ka, ng), F32)], compiler_params=_params(("parallel", "arbitrary")),
    )(a, g)


def _layer_norm(z, g, b):
    mu = jnp.mean(z, axis=-1, keepdims=True)
    zc = z - mu
    rstd = lax.rsqrt(jnp.mean(zc * zc, axis=-1, keepdims=True) + LN_EPS)
    xhat = zc * rstd
    return xhat * g + b, xhat, rstd


def mlp_fwd(name, x, w1, w2, g, b, *, tm=512):
    fb = w1.shape[2]

    def body(x_ref, w1_ref, w2_ref, g_ref, b_ref, y_ref, xhat_ref, rstd_ref, relu_ref, acc_ref):
        s = pl.program_id(1)

        @pl.when(s == 0)
        def _():
            acc_ref[...] = jnp.zeros_like(acc_ref)

        r = jnp.maximum(_dot(x_ref[...], w1_ref[...], _NN), 0.0)
        relu_ref[...] = r.astype(relu_ref.dtype)
        acc_ref[...] += _dot(r * r, w2_ref[...], _NN)

        @pl.when(s == N_CHIPS - 1)
        def _():
            y_ref[...], xhat_ref[...], rstd_ref[...] = _layer_norm(ALPHA * x_ref[...] + acc_ref[...], g_ref[...], b_ref[...])

    row = pl.BlockSpec((tm, D), lambda i, s: (i, 0))
    vec = pl.BlockSpec((1, D), lambda i, s: (0, 0))
    return pl.pallas_call(
        body, name=name, grid=(S // tm, N_CHIPS),
        in_specs=[row, pl.BlockSpec((None, D, fb), lambda i, s: (s, 0, 0)),
                  pl.BlockSpec((None, fb, D), lambda i, s: (s, 0, 0)), vec, vec],
        out_specs=[row, row, pl.BlockSpec((tm, 1), lambda i, s: (i, 0)), pl.BlockSpec((tm, fb), lambda i, s: (i, s))],
        out_shape=[jax.ShapeDtypeStruct((S, D), F32), jax.ShapeDtypeStruct((S, D), F32),
                   jax.ShapeDtypeStruct((S, 1), F32), jax.ShapeDtypeStruct((S, N_CHIPS * fb), BF16)],
        scratch_shapes=[pltpu.VMEM((tm, D), F32)], compiler_params=_params(("parallel", "arbitrary")),
    )(x, w1, w2, g, b)


def mlp_bwd(name, dz, relu, x, w1, w2, *, tm=256):
    fb = w1.shape[2]
    n_i = S // tm

    def body(dz_ref, relu_ref, x_ref, w1_ref, w2_ref, dx_ref, g1_ref, g2_ref, acc1_ref, acc2_ref):
        s, i = pl.program_id(0), pl.program_id(1)
        rows = pl.ds(pl.multiple_of(i * tm, tm), tm)
        dz_v = dz_ref[...]

        @pl.when(i == 0)
        def _():
            acc1_ref[...] = jnp.zeros_like(acc1_ref)
            acc2_ref[...] = jnp.zeros_like(acc2_ref)

        @pl.when(s == 0)
        def _():
            dx_ref[rows, :] = ALPHA * dz_v

        dz_b = dz_v.astype(BF16)
        r = relu_ref[...]
        dh = (_dot(dz_b, w2_ref[...], _NT) * (2.0 * r.astype(F32))).astype(BF16)
        p2 = _dot(r * r, dz_b, _TN)
        p1 = _dot(x_ref[...], dh, _TN)
        dx_ref[rows, :] += _dot(dh, w1_ref[...], _NT)
        acc1_ref[...] += p1
        acc2_ref[...] += p2

        @pl.when(i == n_i - 1)
        def _():
            g1_ref[...] = acc1_ref[...].astype(g1_ref.dtype)
            g2_ref[...] = acc2_ref[...].astype(g2_ref.dtype)

    row = pl.BlockSpec((tm, D), lambda s, i: (i, 0))
    return pl.pallas_call(
        body, name=name, grid=(N_CHIPS, n_i),
        in_specs=[row, pl.BlockSpec((tm, fb), lambda s, i: (i, s)), row,
                  pl.BlockSpec((None, D, fb), lambda s, i: (s, 0, 0)), pl.BlockSpec((None, fb, D), lambda s, i: (s, 0, 0))],
        out_specs=[pl.BlockSpec((S, D), lambda s, i: (0, 0)), pl.BlockSpec((None, D, fb), lambda s, i: (s, 0, 0)),
                   pl.BlockSpec((None, fb, D), lambda s, i: (s, 0, 0))],
        out_shape=[jax.ShapeDtypeStruct((S, D), F32), jax.ShapeDtypeStruct((N_CHIPS, D, fb), BF16),
                   jax.ShapeDtypeStruct((N_CHIPS, fb, D), BF16)],
        scratch_shapes=[pltpu.VMEM((D, fb), F32), pltpu.VMEM((fb, D), F32)],
        compiler_params=_params(("arbitrary", "arbitrary")),
    )(dz, relu, x, w1, w2)


def ln_bwd(name, dy, xhat, rstd, g, *, tm=256, after=None):
    n_after = 0 if after is None else 1

    def body(dy_ref, xhat_ref, rstd_ref, g_ref, *rest):
        dz_ref, dg_ref, db_ref = rest[n_after:]
        dy_v = dy_ref[...]
        xh = xhat_ref[...]
        dxh = dy_v * g_ref[...]
        m1 = jnp.mean(dxh, axis=-1, keepdims=True)
        m2 = jnp.mean(dxh * xh, axis=-1, keepdims=True)
        dz_ref[...] = rstd_ref[...] * (dxh - m1 - xh * m2)
        pg = jnp.sum(dy_v * xh, axis=0, keepdims=True)
        pb = jnp.sum(dy_v, axis=0, keepdims=True)
        i = pl.program_id(0)

        @pl.when(i == 0)
        def _():
            dg_ref[...] = pg
            db_ref[...] = pb

        @pl.when(i > 0)
        def _():
            dg_ref[...] += pg
            db_ref[...] += pb

    row = pl.BlockSpec((tm, D), lambda i: (i, 0))
    vec = pl.BlockSpec((1, D), lambda i: (0, 0))
    return pl.pallas_call(
        body, name=name, grid=(S // tm,),
        in_specs=[row, row, pl.BlockSpec((tm, 1), lambda i: (i, 0)), vec] + [pl.BlockSpec(memory_space=pl.ANY)] * n_after,
        out_specs=[row, vec, vec],
        out_shape=[jax.ShapeDtypeStruct((S, D), F32), jax.ShapeDtypeStruct((1, D), F32),
                   jax.ShapeDtypeStruct((1, D), F32)],
        compiler_params=_params(("arbitrary",)),
    )(dy, xhat, rstd, g, *([after] * n_after))


def loss_head(y, target, *, tm=256):
    def body(y_ref, t_ref, dy_ref, loss_ref):
        e = y_ref[...] - t_ref[...]
        dy_ref[...] = e * (1.0 / D)
        part = jnp.sum(jnp.sum(e * e, axis=-1, keepdims=True), axis=0, keepdims=True) * (0.5 / D)
        i = pl.program_id(0)

        @pl.when(i == 0)
        def _():
            loss_ref[...] = jnp.zeros_like(loss_ref)

        loss_ref[...] += jnp.broadcast_to(part, loss_ref.shape)

    row = pl.BlockSpec((tm, D), lambda i: (i, 0))
    return pl.pallas_call(
        body, name="loss_head", grid=(S // tm,), in_specs=[row, row],
        out_specs=[row, pl.BlockSpec((8, 128), lambda i: (0, 0))],
        out_shape=[jax.ShapeDtypeStruct((S, D), F32), jax.ShapeDtypeStruct((8, 128), F32)],
        compiler_params=_params(("arbitrary",)),
    )(y, target)


def _rows(shape):
    return lax.broadcasted_iota(jnp.int32, shape, 0)


def _shift_down(x, k):
    return jnp.where(_rows(x.shape) >= k, pltpu.roll(x, k, 0), 0.0)


def _shift_up(x, k):
    n = x.shape[0]
    return jnp.where(_rows(x.shape) < n - k, pltpu.roll(x, n - k, 0), 0.0)


def _pool_diff(u, w):
    acc, k = u, 1
    while k < w:
        acc = acc + _shift_down(acc, k)
        k *= 2
    cnt = jnp.minimum(_rows(u.shape) + 1, w).astype(F32)
    return acc / cnt - u, cnt


def pool_fwd(name, proj, pool_w, pool_scale):
    def body(u_ref, w_ref, sc_ref, y_ref):
        for g, w in enumerate(POOL_WINDOWS):
            cols = slice(g * HEAD, (g + 1) * HEAD)
            d, _ = _pool_diff(u_ref[:, cols], w)
            z = _dot(d, w_ref[g], _NN)
            y_ref[:, cols] = (z * sc_ref[:, cols]).astype(y_ref.dtype)

    return pl.pallas_call(
        body, name=name, grid=(1,),
        in_specs=[pl.BlockSpec((S, POOL_W), lambda i: (0, 0)),
                  pl.BlockSpec((4, HEAD, HEAD), lambda i: (0, 0, 0)),
                  pl.BlockSpec((1, POOL_W), lambda i: (0, 0))],
        out_specs=pl.BlockSpec((S, POOL_W), lambda i: (0, 0)),
        out_shape=jax.ShapeDtypeStruct((S, POOL_W), BF16),
        compiler_params=_params(("arbitrary",)),
    )(proj, pool_w, pool_scale)


def pool_bwd(name, proj, dycat, pool_w, pool_scale):
    def body(u_ref, dy_ref, w_ref, sc_ref, du_ref, dw_ref, dsc_ref):
        for g, w in enumerate(POOL_WINDOWS):
            cols = slice(g * HEAD, (g + 1) * HEAD)
            d, cnt = _pool_diff(u_ref[:, cols], w)
            dy = dy_ref[:, cols]
            z = _dot(d, w_ref[g], _NN)
            dsc_ref[:, cols] = jnp.sum(dy * z, axis=0, keepdims=True)
            dz = dy * sc_ref[:, cols]
            dw_ref[g] = _dot(d, dz, _TN)
            dd = _dot(dz, w_ref[g], _NT)
            acc, k = dd / cnt, 1
            while k < w:
                acc = acc + _shift_up(acc, k)
                k *= 2
            du_ref[:, cols] = (acc - dd).astype(du_ref.dtype)

    return pl.pallas_call(
        body, name=name, grid=(1,),
        in_specs=[pl.BlockSpec((S, POOL_W), lambda i: (0, 0)),
                  pl.BlockSpec((S, POOL_W), lambda i: (0, 0)),
                  pl.BlockSpec((4, HEAD, HEAD), lambda i: (0, 0, 0)),
                  pl.BlockSpec((1, POOL_W), lambda i: (0, 0))],
        out_specs=[pl.BlockSpec((S, POOL_W), lambda i: (0, 0)),
                   pl.BlockSpec((4, HEAD, HEAD), lambda i: (0, 0, 0)),
                   pl.BlockSpec((1, POOL_W), lambda i: (0, 0))],
        out_shape=[jax.ShapeDtypeStruct((S, POOL_W), BF16), jax.ShapeDtypeStruct((4, HEAD, HEAD), F32),
                   jax.ShapeDtypeStruct((1, POOL_W), F32)],
        compiler_params=_params(("arbitrary",)),
    )(proj, dycat, pool_w, pool_scale)


def _expm1(x):
    series = x * (1.0 + x * (0.5 + x * (1.0 / 6.0 + x * (1.0 / 24.0 + x * (1.0 / 120.0)))))
    return jnp.where(jnp.abs(x) < 0.05, series, jnp.exp(x) - 1.0)


def _softplus_neg(lam):
    e = jnp.exp(-jnp.abs(lam))
    log1p = jnp.where(e < 0.01, e * (1.0 - e * (0.5 - e * (1.0 / 3.0))), jnp.log(1.0 + e))
    return jnp.maximum(-lam, 0.0) + log1p


_GELU_C = math.sqrt(2.0 / math.pi)


def _gelu(x):
    t = jnp.tanh(_GELU_C * (x + 0.044715 * x * x * x))
    return 0.5 * x * (1.0 + t), t


def _gelu_grad(x, t):
    return 0.5 * (1.0 + t) + 0.5 * x * (1.0 - t * t) * _GELU_C * (1.0 + 3.0 * 0.044715 * x * x)


def _conv(u, cw, cb):
    return cw[3:4] * u + cw[2:3] * _shift_down(u, 1) + cw[1:2] * _shift_down(u, 2) + cw[0:1] * _shift_down(u, 3) + cb


def _lru_gates(cu, wa, ba, wx, bx, lam):
    r = jax.nn.sigmoid(_dot(cu, wa, _NN) + ba)
    i = jax.nn.sigmoid(_dot(cu, wx, _NN) + bx)
    sp = _softplus_neg(lam)
    log_a = (-LRU_C) * r * sp
    a = jnp.exp(log_a)
    mult = jnp.sqrt(-_expm1(2.0 * log_a))
    return r, i, sp, a, mult


def _scan(a_ref, b_ref, h_ref, *, reverse):
    n_blk = S // 8
    row8 = lax.broadcasted_iota(jnp.int32, (8, HEAD), 0)

    def step(j, carry):
        blk = (n_blk - 1 - j) if reverse else j
        r0 = pl.multiple_of(blk * 8, 8)
        a = a_ref[pl.ds(r0, 8), :]
        b = b_ref[pl.ds(r0, 8), :]
        for k in (1, 2, 4):
            if reverse:
                keep = row8 < 8 - k
                a_s, b_s = pltpu.roll(a, 8 - k, 0), pltpu.roll(b, 8 - k, 0)
            else:
                keep = row8 >= k
                a_s, b_s = pltpu.roll(a, k, 0), pltpu.roll(b, k, 0)
            b = jnp.where(keep, a * b_s + b, b)
            a = jnp.where(keep, a * a_s, a)
        h = b + a * carry
        h_ref[pl.ds(r0, 8), :] = h
        edge = h[0:1, :] if reverse else h[7:8, :]
        return jnp.broadcast_to(edge, (8, HEAD))

    lax.fori_loop(0, n_blk, step, jnp.zeros((8, HEAD), F32), unroll=4)


def _lru_specs():
    def col(off):
        return pl.BlockSpec((S, HEAD), lambda h: (0, off + h))
    vec = pl.BlockSpec((1, HEAD), lambda h: (0, h))
    mat = pl.BlockSpec((None, HEAD, HEAD), lambda h: (h, 0, 0))
    cw = pl.BlockSpec((4, HEAD), lambda h: (0, h))
    return col, vec, mat, cw


def lru_fwd(name, proj, conv_w, conv_b, w_a, b_a, w_x, b_x, lam):
    def body(u_ref, ug_ref, cw_ref, cb_ref, wa_ref, ba_ref, wx_ref, bx_ref, lam_ref, y_ref, h_ref, a_s, b_s):
        cu = _conv(u_ref[...], cw_ref[...], cb_ref[...])
        _, i, _, a, mult = _lru_gates(cu, wa_ref[...], ba_ref[...], wx_ref[...], bx_ref[...], lam_ref[...])
        a_s[...] = a
        b_s[...] = mult * (i * cu)
        _scan(a_s, b_s, h_ref, reverse=False)
        gl, _ = _gelu(ug_ref[...])
        y_ref[...] = (h_ref[...] * gl).astype(y_ref.dtype)

    col, vec, mat, cw = _lru_specs()
    out = pl.BlockSpec((S, HEAD), lambda h: (0, h))
    return pl.pallas_call(
        body, name=name, grid=(LRU_HEADS,),
        in_specs=[col(4), col(12), cw, vec, mat, vec, mat, vec, vec],
        out_specs=[out, out],
        out_shape=[jax.ShapeDtypeStruct((S, LRU_W), BF16), jax.ShapeDtypeStruct((S, LRU_W), F32)],
        scratch_shapes=[pltpu.VMEM((S, HEAD), F32), pltpu.VMEM((S, HEAD), F32)],
        compiler_params=_params(("parallel",)),
    )(proj, proj, conv_w, conv_b, w_a, b_a, w_x, b_x, lam)


def lru_bwd(name, proj, hstate, dycat, conv_w, conv_b, w_a, b_a, w_x, b_x, lam):
    def body(u_ref, ug_ref, h_ref, dy_ref, cw_ref, cb_ref, wa_ref, ba_ref, wx_ref, bx_ref, lam_ref,
             du_ref, dug_ref, dwa_ref, dwx_ref, dba_ref, dbx_ref, dlam_ref, dcw_ref, dcb_ref, a_s, b_s, g_s):
        u = u_ref[...]
        cw = cw_ref[...]
        cu = _conv(u, cw, cb_ref[...])
        lam_v = lam_ref[...]
        r, i, sp, a, mult = _lru_gates(cu, wa_ref[...], ba_ref[...], wx_ref[...], bx_ref[...], lam_v)
        ug = ug_ref[...]
        gl, t = _gelu(ug)
        dy = dy_ref[...]
        h = h_ref[...]
        dug_ref[...] = (dy * h * _gelu_grad(ug, t)).astype(dug_ref.dtype)
        a_s[...] = _shift_up(a, 1)
        b_s[...] = dy * gl
        _scan(a_s, b_s, g_s, reverse=True)
        dxin = g_s[...]
        da = dxin * _shift_down(h, 1)
        dmult = dxin * (i * cu)
        di = dxin * (mult * cu)
        dlog_a = da * a - dmult * (a * a) / mult
        dr_pre = dlog_a * ((-LRU_C) * sp) * (r * (1.0 - r))
        di_pre = di * (i * (1.0 - i))
        dsp = jnp.sum(dlog_a * ((-LRU_C) * r), axis=0, keepdims=True)
        dlam_ref[...] = dsp * (-jax.nn.sigmoid(-lam_v))
        dba_ref[...] = jnp.sum(dr_pre, axis=0, keepdims=True)
        dbx_ref[...] = jnp.sum(di_pre, axis=0, keepdims=True)
        dwa_ref[...] = _dot(cu, dr_pre, _TN)
        dwx_ref[...] = _dot(cu, di_pre, _TN)
        dcu = dxin * (mult * i) + _dot(dr_pre, wa_ref[...], _NT) + _dot(di_pre, wx_ref[...], _NT)
        dcb_ref[...] = jnp.sum(dcu, axis=0, keepdims=True)
        for k in range(4):
            dcw_ref[k:k + 1, :] = jnp.sum(dcu * (_shift_down(u, 3 - k) if k < 3 else u), axis=0, keepdims=True)
        du = cw[3:4] * dcu + cw[2:3] * _shift_up(dcu, 1) + cw[1:2] * _shift_up(dcu, 2) + cw[0:1] * _shift_up(dcu, 3)
        du_ref[...] = du.astype(du_ref.dtype)

    col, vec, mat, cw = _lru_specs()
    out = pl.BlockSpec((S, HEAD), lambda h: (0, h))
    big = jax.ShapeDtypeStruct((S, LRU_W), BF16)
    vec_shape = jax.ShapeDtypeStruct((1, LRU_W), F32)
    mat_shape = jax.ShapeDtypeStruct((LRU_HEADS, HEAD, HEAD), F32)
    return pl.pallas_call(
        body, name=name, grid=(LRU_HEADS,),
        in_specs=[col(4), col(12), out, col(4), cw, vec, mat, vec, mat, vec, vec],
        out_specs=[out, out, mat, mat, vec, vec, vec, cw, vec],
        out_shape=[big, big, mat_shape, mat_shape, vec_shape, vec_shape, vec_shape,
                   jax.ShapeDtypeStruct((4, LRU_W), F32), vec_shape],
        scratch_shapes=[pltpu.VMEM((S, HEAD), F32)] * 3,
        compiler_params=_params(("parallel",)),
    )(proj, proj, hstate, dycat, conv_w, conv_b, w_a, b_a, w_x, b_x, lam)


def rope_tables(pos_col, inv_freq):
    def body(pos_ref, f_ref, c_ref, s1_ref, s2_ref):
        ang = pos_ref[...].astype(F32) * f_ref[...]
        lane = lax.broadcasted_iota(jnp.int32, ang.shape, 1)
        cos, sin = jnp.cos(ang), jnp.sin(ang)
        c_ref[...] = jnp.where(lane < QK_ROPE, cos, 0.0)
        s1_ref[...] = jnp.where(lane < QK_ROPE // 2, -sin, 0.0)
        s2_ref[...] = jnp.where((lane >= QK_ROPE // 2) & (lane < QK_ROPE), sin, 0.0)

    tab = jax.ShapeDtypeStruct((S, HEAD), F32)
    return pl.pallas_call(
        body, name="rope_tables", grid=(1,),
        in_specs=[pl.BlockSpec((S, 1), lambda i: (0, 0)), pl.BlockSpec((1, HEAD), lambda i: (0, 0))],
        out_specs=[pl.BlockSpec((S, HEAD), lambda i: (0, 0))] * 3, out_shape=[tab, tab, tab],
        compiler_params=_params(("arbitrary",)),
    )(pos_col, inv_freq)


def _rope(v, c, s1, s2):
    return v * c + pltpu.roll(v, HEAD - QK_ROPE // 2, 1) * s1 + pltpu.roll(v, QK_ROPE // 2, 1) * s2


def _unrope(d, c, s1, s2):
    return d * c + pltpu.roll(d * s1, QK_ROPE // 2, 1) + pltpu.roll(d * s2, HEAD - QK_ROPE // 2, 1)


def _rms(x, g):
    rstd = lax.rsqrt(jnp.mean(x * x, axis=-1, keepdims=True) + RMS_EPS)
    return x * rstd, rstd


def mla_prep(name, down, gq, gkv, tabs, *, tm=256):
    def body(dn_ref, gq_ref, gkv_ref, c_ref, s1_ref, s2_ref, cq_ref, ckv_ref, kp_ref):
        xq, _ = _rms(dn_ref[:, :Q_RANK], None)
        cq_ref[...] = (xq * gq_ref[...]).astype(cq_ref.dtype)
        xkv, _ = _rms(dn_ref[:, Q_RANK:Q_RANK + KV_RANK], None)
        ckv_ref[...] = (xkv * gkv_ref[...]).astype(ckv_ref.dtype)
        kp = _rope(dn_ref[:, Q_RANK + KV_RANK:], c_ref[...], s1_ref[...], s2_ref[...])
        kp_ref[...] = kp.astype(kp_ref.dtype)

    tab = pl.BlockSpec((tm, HEAD), lambda i: (i, 0))
    return pl.pallas_call(
        body, name=name, grid=(S // tm,),
        in_specs=[pl.BlockSpec((tm, ODD_IN_PAD), lambda i: (i, 0)), pl.BlockSpec((1, Q_RANK), lambda i: (0, 0)),
                  pl.BlockSpec((1, KV_RANK), lambda i: (0, 0)), tab, tab, tab],
        out_specs=[pl.BlockSpec((tm, Q_RANK), lambda i: (i, 0)), pl.BlockSpec((tm, KV_RANK), lambda i: (i, 0)), tab],
        out_shape=[jax.ShapeDtypeStruct((S, Q_RANK), BF16), jax.ShapeDtypeStruct((S, KV_RANK), BF16),
                   jax.ShapeDtypeStruct((S, HEAD), BF16)],
        compiler_params=_params(("parallel",)),
    )(down, gq, gkv, *tabs)


def mla_prep_bwd(name, down, dcq, dckv, dkp, gq, gkv, tabs, *, tm=256):
    def body(dn_ref, dcq_ref, dckv_ref, dkp_ref, gq_ref, gkv_ref, c_ref, s1_ref, s2_ref, dd_ref, dgq_ref, dgkv_ref):
        i = pl.program_id(0)

        def rms_bwd(x, dy, g, dg_ref):
            xh, rstd = _rms(x, None)
            dxh = dy * g
            dx = rstd * (dxh - xh * jnp.mean(dxh * xh, axis=-1, keepdims=True))
            pg = jnp.sum(dy * xh, axis=0, keepdims=True)

            @pl.when(i == 0)
            def _():
                dg_ref[...] = pg

            @pl.when(i > 0)
            def _():
                dg_ref[...] += pg

            return dx

        dxq = rms_bwd(dn_ref[:, :Q_RANK], dcq_ref[...], gq_ref[...], dgq_ref)
        dd_ref[:, :Q_RANK] = dxq.astype(dd_ref.dtype)
        dxkv = rms_bwd(dn_ref[:, Q_RANK:Q_RANK + KV_RANK], dckv_ref[...], gkv_ref[...], dgkv_ref)
        dd_ref[:, Q_RANK:Q_RANK + KV_RANK] = dxkv.astype(dd_ref.dtype)
        dd_ref[:, Q_RANK + KV_RANK:] = _unrope(dkp_ref[...], c_ref[...], s1_ref[...], s2_ref[...]).astype(dd_ref.dtype)

    tab = pl.BlockSpec((tm, HEAD), lambda i: (i, 0))
    vq = pl.BlockSpec((1, Q_RANK), lambda i: (0, 0))
    vkv = pl.BlockSpec((1, KV_RANK), lambda i: (0, 0))
    return pl.pallas_call(
        body, name=name, grid=(S // tm,),
        in_specs=[pl.BlockSpec((tm, ODD_IN_PAD), lambda i: (i, 0)), pl.BlockSpec((tm, Q_RANK), lambda i: (i, 0)),
                  pl.BlockSpec((tm, KV_RANK), lambda i: (i, 0)), tab, vq, vkv, tab, tab, tab],
        out_specs=[pl.BlockSpec((tm, ODD_IN_PAD), lambda i: (i, 0)), vq, vkv],
        out_shape=[jax.ShapeDtypeStruct((S, ODD_IN_PAD), BF16), jax.ShapeDtypeStruct((1, Q_RANK), F32),
                   jax.ShapeDtypeStruct((1, KV_RANK), F32)],
        compiler_params=_params(("arbitrary",)),
    )(down, dcq, dckv, dkp, gq, gkv, *tabs)


ATT_TQ = 256


def _attn_probs(q_ref, kv_ref, kp_ref, c_ref, s1_ref, s2_ref, i, nk):
    qn = q_ref[:, :HEAD].astype(BF16)
    qp = _rope(q_ref[:, HEAD:], c_ref[...], s1_ref[...], s2_ref[...]).astype(BF16)
    kn = kv_ref[:nk, :HEAD]
    sc = (_dot(qn, kn, _NT) + _dot(qp, kp_ref[:nk, :], _NT)) * ATT_SCALE
    q_chunk = (i * ATT_TQ + lax.broadcasted_iota(jnp.int32, sc.shape, 0)) // CHUNK
    k_chunk = lax.broadcasted_iota(jnp.int32, sc.shape, 1) // CHUNK
    sc = jnp.where(k_chunk <= q_chunk, sc, jnp.finfo(F32).min)
    e = jnp.exp(sc - jnp.max(sc, axis=-1, keepdims=True))
    p = e * (1.0 / jnp.sum(e, axis=-1, keepdims=True))
    return p, qn, qp, kn


def _for_each_prefix(i, fn):
    for k in range(S // ATT_TQ):
        pl.when(i == k)(functools.partial(fn, (k + 1) * ATT_TQ))


def _attn_specs():
    q = pl.BlockSpec((ATT_TQ, QHEAD_PAD), lambda h, i: (i, h))
    kv = pl.BlockSpec((S, QHEAD_PAD), lambda h, i: (0, h))
    kp = pl.BlockSpec((S, HEAD), lambda h, i: (0, 0))
    tab = pl.BlockSpec((ATT_TQ, HEAD), lambda h, i: (i, 0))
    o = pl.BlockSpec((ATT_TQ, HEAD), lambda h, i: (i, h))
    return q, kv, kp, tab, o


def attn_fwd(name, q, kv, kp, tabs):
    def body(q_ref, kv_ref, kp_ref, c_ref, s1_ref, s2_ref, o_ref):
        i = pl.program_id(1)

        def run(nk):
            p, _, _, _ = _attn_probs(q_ref, kv_ref, kp_ref, c_ref, s1_ref, s2_ref, i, nk)
            o_ref[...] = _dot(p, kv_ref[:nk, HEAD:], _NN).astype(o_ref.dtype)

        _for_each_prefix(i, run)

    qs, kvs, kps, tab, os = _attn_specs()
    return pl.pallas_call(
        body, name=name, grid=(MLA_HEADS, S // ATT_TQ), in_specs=[qs, kvs, kps, tab, tab, tab], out_specs=os,
        out_shape=jax.ShapeDtypeStruct((S, MLA_HEADS * HEAD), BF16),
        compiler_params=_params(("parallel", "parallel")),
    )(q, kv, kp, *tabs)


def attn_bwd(name, q, kv, kp, do, tabs):
    def body(q_ref, kv_ref, kp_ref, do_ref, c_ref, s1_ref, s2_ref, dq_ref, dkv_ref, dkp_ref):
        h, i = pl.program_id(0), pl.program_id(1)

        @pl.when(i == 0)
        def _():
            dkv_ref[...] = jnp.zeros_like(dkv_ref)

        @pl.when((i == 0) & (h == 0))
        def _():
            dkp_ref[...] = jnp.zeros_like(dkp_ref)

        def run(nk):
            p, qn, qp, kn = _attn_probs(q_ref, kv_ref, kp_ref, c_ref, s1_ref, s2_ref, i, nk)
            do_v = do_ref[...]
            dp = _dot(do_v, kv_ref[:nk, HEAD:], _NT)
            ds = (p * (dp - jnp.sum(p * dp, axis=-1, keepdims=True)) * ATT_SCALE).astype(BF16)
            dq_ref[:, :HEAD] = _dot(ds, kn, _NN).astype(dq_ref.dtype)
            dqp = _unrope(_dot(ds, kp_ref[:nk, :], _NN), c_ref[...], s1_ref[...], s2_ref[...])
            dq_ref[:, HEAD:] = dqp.astype(dq_ref.dtype)
            dkv_ref[:nk, :HEAD] += _dot(ds, qn, _TN)
            dkv_ref[:nk, HEAD:] += _dot(p, do_v, _TN)
            dkp_ref[:nk, :] += _dot(ds, qp, _TN)

        _for_each_prefix(i, run)

    qs, kvs, kps, tab, os = _attn_specs()
    return pl.pallas_call(
        body, name=name, grid=(MLA_HEADS, S // ATT_TQ), in_specs=[qs, kvs, kps, os, tab, tab, tab],
        out_specs=[qs, kvs, kps],
        out_shape=[jax.ShapeDtypeStruct((S, MLA_HEADS * QHEAD_PAD), BF16),
                   jax.ShapeDtypeStruct((S, MLA_HEADS * QHEAD_PAD), F32), jax.ShapeDtypeStruct((S, HEAD), F32)],
        compiler_params=_params(("arbitrary", "arbitrary")),
    )(q, kv, kp, do, *tabs)


def adamw(name, w, g, m, v):
    rows, cols = w.shape
    tr = rows
    for cand in (512, 256, 128, 64, 32, 16, 8):
        if rows % cand == 0 and cand * cols * 4 <= 2 * 1024 * 1024:
            tr = cand
            break

    def body(w_ref, g_ref, m_ref, v_ref, d_ref, nm_ref, nv_ref):
        g_v = g_ref[...]
        nm = ADAM_B1 * m_ref[...] + (1.0 - ADAM_B1) * g_v
        nv = ADAM_B2 * v_ref[...] + (1.0 - ADAM_B2) * (g_v * g_v)
        m_hat = nm / (1.0 - ADAM_B1 ** ADAM_STEP)
        v_hat = nv / (1.0 - ADAM_B2 ** ADAM_STEP)
        d_ref[...] = (-ADAM_LR) * (m_hat / (jnp.sqrt(v_hat) + ADAM_EPS) + ADAM_WD * w_ref[...])
        nm_ref[...] = nm
        nv_ref[...] = nv

    blk = pl.BlockSpec((tr, cols), lambda i: (i, 0))
    shape = jax.ShapeDtypeStruct((rows, cols), F32)
    return pl.pallas_call(
        body, name=name, grid=(rows // tr,), in_specs=[blk] * 4, out_specs=[blk] * 3, out_shape=[shape] * 3,
        compiler_params=_params(("parallel",)),
    )(w, g, m, v)


def _local_step(x, pos_col, target, sm, weights_of, emit_grads, prefetch):
    inv_freq = ROPE_THETA ** (-jnp.arange(0, QK_ROPE, 2, dtype=F32) / QK_ROPE)
    inv_freq = jnp.concatenate([inv_freq, inv_freq, jnp.zeros((HEAD - QK_ROPE,), F32)])[None, :]
    tabs = rope_tables(pos_col, inv_freq)
    saved, wts = [], {}
    for layer in range(DEPTH):
        j = layer // 2
        n = "l%d_" % layer
        sv = {"x": x}
        wm = wts["mix%d" % layer] = weights_of("mix%d" % layer, x)
        if layer == 0:
            sm = dict(sm, conv_w=wm["conv_w"], gq=wm["gq"], gkv=wm["gkv"])
        if layer % 2 == 0:
            proj = mm_cols(n + "proj", x, wm["w_in"], [F32])[0]
            fetched = prefetch("mlp%d" % layer, proj)
            y_pool = pool_fwd(n + "pool", proj, sm["pool_w"][j], sm["pool_scale"][j][None])
            y_lru, hstate = lru_fwd(n + "lru", proj, sm["conv_w"][j], sm["conv_b"][j][None], sm["w_a"][j],
                                    sm["b_a"][j][None], sm["w_x"][j], sm["b_x"][j][None], sm["lam"][j][None])
            ycat = jnp.concatenate([y_pool, y_lru], axis=1)
            mix_in, w_mix = ycat, wm["w_out"]
            sv.update(proj=proj, hstate=hstate, ycat=ycat)
        else:
            down = mm_rows(n + "down", x, wm["w_down"], F32)
            fetched = prefetch("mlp%d" % layer, down)
            cq, ckv, kp = mla_prep(n + "prep", down, sm["gq"][j][None], sm["gkv"][j][None], tabs)
            q = mm_cols(n + "q", cq, wm["w_qb"], [F32])[0]
            kv = mm_cols(n + "kv", ckv, wm["w_kvb"], [BF16])[0]
            o = attn_fwd(n + "attn", q, kv, kp, tabs)
            mix_in, w_mix = o, wm["w_o"]
            sv.update(down=down, cq=cq, ckv=ckv, kp=kp, q=q, kv=kv, o=o)
        x1, xhat1, rstd1 = mm_rows_ln(n + "mixout", mix_in, w_mix, x, sm["ln_mix_g"][layer][None],
                                      sm["ln_mix_b"][layer][None], after=fetched)
        wf = wts["mlp%d" % layer] = weights_of("mlp%d" % layer, x1)
        x2, xhat2, rstd2, relu = mlp_fwd(n + "mlp", x1, wf["w1"], wf["w2"], sm["ln_ffn_g"][layer][None],
                                         sm["ln_ffn_b"][layer][None])
        sv.update(xhat1=xhat1, rstd1=rstd1, x1=x1, relu=relu, xhat2=xhat2, rstd2=rstd2)
        saved.append(sv)
        x = x2

    dx, loss_tile = loss_head(x, target)
    gs = {k: [None] * (DEPTH if k.startswith("ln_") else DEPTH // 2) for k in sm}
    for layer in reversed(range(DEPTH)):
        j = layer // 2
        n = "l%d_" % layer
        sv = saved[layer]
        wm, wf = wts["mix%d" % layer], wts["mlp%d" % layer]
        dz, gs["ln_ffn_g"][layer], gs["ln_ffn_b"][layer] = ln_bwd(
            n + "ln_ffn_b", dx, sv["xhat2"], sv["rstd2"], sm["ln_ffn_g"][layer][None])
        dx, g_w1, g_w2 = mlp_bwd(n + "mlp_b", dz, sv["relu"], sv["x1"], wf["w1"], wf["w2"])
        sent = emit_grads("mlp%d" % layer, {"w1": g_w1, "w2": g_w2})
        dz, gs["ln_mix_g"][layer], gs["ln_mix_b"][layer] = ln_bwd(
            n + "ln_mix_b", dx, sv["xhat1"], sv["rstd1"], sm["ln_mix_g"][layer][None], after=sent)
        if layer % 2 == 0:
            dycat = mm_t_rows(n + "mixout_dx", dz, wm["w_out"], F32)
            g_out = mm_grad(n + "mixout_dw", sv["ycat"], dz, a_sharded=True)
            du_pool, gs["pool_w"][j], gs["pool_scale"][j] = pool_bwd(
                n + "pool_b", sv["proj"], dycat, sm["pool_w"][j], sm["pool_scale"][j][None])
            (du_lru, du_gate, gs["w_a"][j], gs["w_x"][j], gs["b_a"][j], gs["b_x"][j], gs["lam"][j], gs["conv_w"][j],
             gs["conv_b"][j]) = lru_bwd(n + "lru_b", sv["proj"], sv["hstate"], dycat, sm["conv_w"][j],
                                        sm["conv_b"][j][None], sm["w_a"][j], sm["b_a"][j][None], sm["w_x"][j],
                                        sm["b_x"][j][None], sm["lam"][j][None])
            dproj = jnp.concatenate([du_pool, du_lru, du_gate], axis=1)
            g_in = mm_grad(n + "proj_dw", sv["x"], dproj, a_sharded=False)
            sent = emit_grads("mix%d" % layer, {"w_in": g_in, "w_out": g_out})
            dx = mm_t_cols(n + "proj_dx", dproj, wm["w_in"], F32, resid=dz, after=sent)
        else:
            do = mm_t_rows(n + "attnout_dx", dz, wm["w_o"], BF16)
            g_o = mm_grad(n + "attnout_dw", sv["o"], dz, a_sharded=True)
            dq, dkv, dkp = attn_bwd(n + "attn_b", sv["q"], sv["kv"], sv["kp"], do, tabs)
            g_qb = mm_grad(n + "q_dw", sv["cq"], dq, a_sharded=False)
            dcq = mm_t_cols(n + "q_dx", dq, wm["w_qb"], F32)
            g_kvb = mm_grad(n + "kv_dw", sv["ckv"], dkv, a_sharded=False)
            dckv = mm_t_cols(n + "kv_dx", dkv, wm["w_kvb"], F32)
            ddown, gs["gq"][j], gs["gkv"][j] = mla_prep_bwd(
                n + "prep_b", sv["down"], dcq, dckv, dkp, sm["gq"][j][None], sm["gkv"][j][None], tabs)
            g_down = mm_grad(n + "down_dw", sv["x"], ddown, a_sharded=True)
            sent = emit_grads("mix%d" % layer, {"w_down": g_down, "w_qb": g_qb, "w_kvb": g_kvb, "w_o": g_o})
            dx = mm_t_rows(n + "down_dx", ddown, wm["w_down"], F32, resid=dz, after=sent)
    gs = {k: jnp.stack([a.reshape(sm[k].shape[1:]) for a in v]) for k, v in gs.items()}
    return loss_tile[0, 0], dx, gs


def _place():
    x, y, c = lax.axis_index("x"), lax.axis_index("y"), lax.axis_index("c")
    chips = [(1 - x, y), (x, 1 - y), (1 - x, 1 - y)]
    return x, y, c, chips


def _hbm_call(body, name, args, out_shape, scratch, aliases=None):
    return pl.pallas_call(
        body, name=name, in_specs=[pl.BlockSpec(memory_space=pl.ANY)] * len(args),
        out_specs=[pl.BlockSpec(memory_space=pl.ANY)] * len(out_shape), out_shape=out_shape,
        scratch_shapes=scratch, input_output_aliases=aliases or {},
        compiler_params=pltpu.CompilerParams(has_side_effects=True),
    )(*args)


HBM_SPEC = pl.BlockSpec(memory_space=pltpu.HBM)
SEM_SPEC = pl.BlockSpec(memory_space=pltpu.SEMAPHORE)
EFFECT = pltpu.SideEffectType.DATAFLOW_SIDE_EFFECTING


def _remote(src, dst, send_sem, recv_sem, device):
    return pltpu.make_async_remote_copy(src_ref=src, dst_ref=dst, send_sem=send_sem, recv_sem=recv_sem,
                                        device_id=device, device_id_type=MESH)


def place_own(name, srcs, place, steps=4):
    n = len(srcs)
    in_specs, out_specs, out_shape = [], [], []
    for s in srcs:
        rows, cols = s.shape
        tr = rows // steps
        in_specs.append(pl.BlockSpec((tr, cols), lambda i, p: (i, 0)))
        out_specs.append(pl.BlockSpec((None, tr, cols), lambda i, p: (p[0], i, 0)))
        out_shape.append(jax.ShapeDtypeStruct((N_CHIPS, rows, cols), s.dtype))

    def body(p_ref, *refs):
        for i_ref, o_ref in zip(refs[:n], refs[n:]):
            o_ref[...] = i_ref[...]

    return pl.pallas_call(
        body, name=name, out_shape=out_shape,
        grid_spec=pltpu.PrefetchScalarGridSpec(num_scalar_prefetch=1, grid=(steps,), in_specs=in_specs,
                                               out_specs=out_specs),
        compiler_params=_params(("parallel",)),
    )(place, *srcs)


def split_start(name, groups, n_sems, plan, token_in=None):
    sizes = [len(srcs) for srcs, _ in groups]
    n, n_groups = sum(sizes), len(groups)
    srcs = [pltpu.with_memory_space_constraint(a, pltpu.HBM) for s, _ in groups for a in s]
    lands = [pltpu.with_memory_space_constraint(a, pltpu.HBM) for _, l in groups for a in l]
    extra = [] if token_in is None else [token_in]

    def body(*refs):
        src_refs, land_refs = refs[:n], refs[n:2 * n]
        outs = refs[2 * n + len(extra):]
        off = 0
        for g, size in enumerate(sizes):
            sends, _ = plan(src_refs[off:off + size], land_refs[off:off + size], outs[2 * g], outs[2 * g + 1])
            for cp in sends:
                cp.start()
            off += size
        token = outs[2 * n_groups + 2 * n]
        token[...] = jnp.zeros_like(token)

    sems = [pltpu.SemaphoreType.DMA((size * n_sems,)) for size in sizes for _ in range(2)]
    res = pl.pallas_call(
        body, name=name,
        out_shape=(*sems, *[pltpu.HBM(a.shape, a.dtype) for a in srcs + lands], jax.ShapeDtypeStruct((8, 128), F32)),
        in_specs=[HBM_SPEC] * (2 * n) + [pl.BlockSpec(memory_space=pl.ANY)] * len(extra),
        out_specs=(*[SEM_SPEC] * len(sems), *[HBM_SPEC] * (2 * n), pl.BlockSpec(memory_space=pltpu.VMEM)),
        input_output_aliases={i: len(sems) + i for i in range(2 * n)},
        compiler_params=pltpu.CompilerParams(has_side_effects=EFFECT),
    )(*srcs, *lands, *extra)
    started, off = [], 0
    bufs = res[len(sems):]
    for g, size in enumerate(sizes):
        started.append(dict(send=res[2 * g], recv=res[2 * g + 1], srcs=list(bufs[off:off + size]),
                            lands=list(bufs[n + off:n + off + size]), plan=plan))
        off += size
    return started, res[-1]


def _wait_started(st, bufs, send_sems, recv_sems):
    n = len(st["srcs"])
    sends, expects = st["plan"](bufs[:n], bufs[n:], send_sems, recv_sems)
    for cp in sends:
        cp.wait_send()
    for cp in expects:
        cp.wait_recv()


def split_wait(name, started, after):
    sizes = [len(st["srcs"]) + len(st["lands"]) for st in started]
    n_buf = sum(sizes)

    def body(*refs):
        bufs, sems = refs[:n_buf], refs[n_buf:n_buf + 2 * len(started)]
        off = 0
        for g, (st, n) in enumerate(zip(started, sizes)):
            _wait_started(st, bufs[off:off + n], sems[2 * g], sems[2 * g + 1])
            off += n

    bufs = [a for st in started for a in st["srcs"] + st["lands"]]
    sems = [s for st in started for s in (st["send"], st["recv"])]
    res = pl.pallas_call(
        body, name=name, out_shape=tuple(pltpu.HBM(a.shape, a.dtype) for a in bufs),
        in_specs=[HBM_SPEC] * n_buf + [SEM_SPEC] * len(sems) + [pl.BlockSpec(memory_space=pl.ANY)],
        out_specs=tuple([HBM_SPEC] * n_buf), input_output_aliases={i: i for i in range(n_buf)},
        compiler_params=pltpu.CompilerParams(has_side_effects=EFFECT),
    )(*bufs, *sems, after)
    out, off = [], 0
    for st, n in zip(started, sizes):
        out.append((list(res[off:off + len(st["srcs"])]), list(res[off + len(st["srcs"]):off + n])))
        off += n
    return out


def split_relay(name, st, n_sems, plan, after):
    n_src, n = len(st["srcs"]), len(st["lands"])

    def body(*refs):
        bufs = refs[:n_src + n]
        outs = refs[n_src + n + 3:]
        _wait_started(st, bufs, refs[n_src + n], refs[n_src + n + 1])
        sends, _ = plan((), bufs[n_src:], outs[0], outs[1])
        for cp in sends:
            cp.start()
        outs[2 + n][...] = jnp.zeros((8, 128), F32)

    bufs = st["srcs"] + st["lands"]
    res = pl.pallas_call(
        body, name=name,
        out_shape=(pltpu.SemaphoreType.DMA((n * n_sems,)), pltpu.SemaphoreType.DMA((n * n_sems,)),
                   *[pltpu.HBM(a.shape, a.dtype) for a in st["lands"]], jax.ShapeDtypeStruct((8, 128), F32)),
        in_specs=[HBM_SPEC] * (n_src + n) + [SEM_SPEC] * 2 + [pl.BlockSpec(memory_space=pl.ANY)],
        out_specs=(SEM_SPEC, SEM_SPEC, *[HBM_SPEC] * n, pl.BlockSpec(memory_space=pltpu.VMEM)),
        input_output_aliases={n_src + i: 2 + i for i in range(n)},
        compiler_params=pltpu.CompilerParams(has_side_effects=EFFECT),
    )(*bufs, st["send"], st["recv"], after)
    return dict(send=res[0], recv=res[1], srcs=[], lands=list(res[2:2 + n]), plan=plan), res[-1]


def gather_plan(src_refs, land_refs, send_sems, recv_sems):
    x, y, c, chips = _place()
    me = 2 * x + y
    sends, expects = [], []
    for k, (s, d) in enumerate(zip(src_refs, land_refs)):
        mine = pl.ds(c * (s.shape[0] // 2), s.shape[0] // 2)
        for j, (px, py) in enumerate(chips):
            sem = 3 * k + j
            sends.append(_remote(s.at[mine], d.at[me, mine], send_sems.at[sem], recv_sems.at[sem], (px, py, c)))
            expects.append(_remote(s.at[mine], d.at[2 * px + py, mine], send_sems.at[sem], recv_sems.at[sem], (px, py, c)))
    return sends, expects


def relay_plan(src_refs, land_refs, send_sems, recv_sems):
    x, y, c, chips = _place()
    sends, expects = [], []
    for k, d in enumerate(land_refs):
        hr = d.shape[1] // 2
        mine, theirs = pl.ds(c * hr, hr), pl.ds((1 - c) * hr, hr)
        for j, (px, py) in enumerate(chips):
            sem, chip = 3 * k + j, 2 * px + py
            sends.append(_remote(d.at[chip, mine], d.at[chip, mine], send_sems.at[sem], recv_sems.at[sem], (x, y, 1 - c)))
            expects.append(_remote(d.at[chip, mine], d.at[chip, theirs], send_sems.at[sem], recv_sems.at[sem], (x, y, 1 - c)))
    return sends, expects


def _reduce_part(s, chip_idx, h):
    hr = s.shape[1] // 2
    return s.at[chip_idx if s.shape[0] == N_CHIPS else 0, pl.ds(h * hr, hr)]


def reduce_plan(src_refs, land_refs, send_sems, recv_sems):
    x, y, c, chips = _place()
    me_chip, me_dev = 2 * x + y, 4 * x + 2 * y + c
    sends, expects = [], []
    for k, (s, d) in enumerate(zip(src_refs, land_refs)):
        part = functools.partial(_reduce_part, s)
        for j, (px, py) in enumerate(chips):
            for h in range(2):
                sends.append(_remote(part(2 * px + py, h), d.at[me_dev], send_sems.at[7 * k + 2 * j + h],
                                     recv_sems.at[7 * k + 2 * j + c], (px, py, h)))
                expects.append(_remote(part(me_chip, c), d.at[4 * px + 2 * py + h], send_sems.at[7 * k + 2 * j + h],
                                       recv_sems.at[7 * k + 2 * j + h], (px, py, h)))
        sends.append(_remote(part(me_chip, 1 - c), d.at[me_dev], send_sems.at[7 * k + 6], recv_sems.at[7 * k + 6],
                             (x, y, 1 - c)))
        expects.append(_remote(part(me_chip, c), d.at[me_dev + 1 - 2 * c], send_sems.at[7 * k + 6],
                               recv_sems.at[7 * k + 6], (x, y, 1 - c)))
    return sends, expects


def sibling_swap_halves(name, fulls):
    n = len(fulls)

    def body(*refs):
        outs = refs[n:2 * n]
        send_sems, recv_sems = refs[2 * n:]
        x, y, c, _ = _place()
        copies = []
        for k in range(n):
            nl, rows = fulls[k].shape[:2]
            hr = rows // 2
            mine = outs[k].at[pl.ds(0, nl), pl.ds(c * hr, hr)]
            theirs = outs[k].at[pl.ds(0, nl), pl.ds((1 - c) * hr, hr)]
            copies.append((_remote(mine, mine, send_sems.at[k], recv_sems.at[k], (x, y, 1 - c)),
                           _remote(mine, theirs, send_sems.at[k], recv_sems.at[k], (x, y, 1 - c))))
        for send, _ in copies:
            send.start()
        for send, recv in copies:
            send.wait_send()
            recv.wait_recv()

    out_shape = [jax.ShapeDtypeStruct(f.shape, f.dtype) for f in fulls]
    scratch = [pltpu.SemaphoreType.DMA((n,)), pltpu.SemaphoreType.DMA((n,))]
    return _hbm_call(body, name, fulls, out_shape, scratch, aliases={k: k for k in range(n)})


def _tile_rows(rows, cols, budget_bytes):
    best = None
    for t in range(16, rows + 1, 16):
        if rows % t == 0 and t * cols * 4 <= budget_bytes:
            best = t
    assert best is not None, (rows, cols)
    return best


N_DEV = 8


def sum_devices(name, landed, own, layer, n_layers, prev, place):
    _, hr, cols = landed.shape
    tr = _tile_rows(hr, cols, 512 * 1024)
    n_blk = hr // tr
    own_slot = (lambda r, p: (p[0], p[1] * n_blk + r, 0)) if own.shape[0] == N_CHIPS else (lambda r, p: (0, p[1] * n_blk + r, 0))

    def body(p_ref, r_ref, own_ref, *rest):
        o_ref = rest[-1]
        mine = own_ref[...].astype(F32)
        acc = jnp.zeros_like(mine)
        for d in range(N_DEV):
            acc = acc + jnp.where(p_ref[2] == d, mine, r_ref[d].astype(F32))
        o_ref[...] = acc

    in_specs = [pl.BlockSpec((N_DEV, tr, cols), lambda r, p: (0, r, 0)), pl.BlockSpec((None, tr, cols), own_slot)]
    args = [place, landed, own]
    aliases = {}
    if prev is not None:
        in_specs.append(pl.BlockSpec(memory_space=pl.ANY))
        args.append(prev)
        aliases = {3: 0}
    return pl.pallas_call(
        body, name=name, out_shape=jax.ShapeDtypeStruct((n_layers, 2 * hr, cols), F32),
        grid_spec=pltpu.PrefetchScalarGridSpec(
            num_scalar_prefetch=1, grid=(n_blk,), in_specs=in_specs,
            out_specs=pl.BlockSpec((None, tr, cols), lambda r, p: (layer, p[1] * n_blk + r, 0))),
        input_output_aliases=aliases, compiler_params=_params(("parallel",)),
    )(*args)


def _pack(arrs, rows_multiple):
    flat = []
    for a in arrs:
        v = a.reshape(-1).astype(F32)
        flat.append(jnp.pad(v, (0, (-v.shape[0]) % HEAD)))
    v = jnp.concatenate(flat)
    v = jnp.pad(v, (0, (-v.shape[0]) % (HEAD * rows_multiple)))
    return v.reshape(-1, HEAD)


def _unpack(packed, shapes):
    flat = packed.reshape(-1)
    out, off = [], 0
    for shp in shapes:
        size = int(np.prod(shp))
        out.append(flat[off:off + size].reshape(shp))
        off += size + (-size) % HEAD
    return out


BIG = ["even_w_in", "even_w_out", "mla_w_down", "mla_w_qb", "mla_w_kvb", "mla_w_o", "mlp_w1", "mlp_w2"]
BIG_KEY = {"even_w_in": "w_in", "even_w_out": "w_out", "mla_w_down": "w_down", "mla_w_qb": "w_qb",
           "mla_w_kvb": "w_kvb", "mla_w_o": "w_o", "mlp_w1": "w1", "mlp_w2": "w2"}
SMALL_KEY = {"ln_mix_g": "ln_mix_g", "ln_mix_b": "ln_mix_b", "ln_ffn_g": "ln_ffn_g", "ln_ffn_b": "ln_ffn_b",
             "pool_w": "pool_w", "pool_scale": "pool_scale", "lru_conv_w": "conv_w", "lru_conv_b": "conv_b",
             "lru_w_a": "w_a", "lru_b_a": "b_a", "lru_w_x": "w_x", "lru_b_x": "b_x", "lru_lambda": "lam",
             "mla_q_norm_g": "gq", "mla_kv_norm_g": "gkv"}
SMALL = list(SMALL_KEY)
SMALL_SHARDED = ["lru_conv_w", "mla_q_norm_g", "mla_kv_norm_g"]
WEIGHTS = ["ln_mix_g", "ln_mix_b", "ln_ffn_g", "ln_ffn_b", "even_w_in", "pool_w", "pool_scale", "lru_conv_w",
           "lru_conv_b", "lru_w_a", "lru_b_a", "lru_w_x", "lru_b_x", "lru_lambda", "even_w_out", "mla_w_down",
           "mla_q_norm_g", "mla_kv_norm_g", "mla_w_qb", "mla_w_kvb", "mla_w_o", "mlp_w1", "mlp_w2"]


GROUPS = ["mix0", "mlp0", "mix1", "mlp1", "mix2", "mlp2", "mix3", "mlp3"]


def _group_keys(group):
    layer = int(group[3:])
    if group.startswith("mlp"):
        return [("mlp_w1", "w1", layer), ("mlp_w2", "w2", layer)]
    if layer % 2 == 0:
        return [("even_w_in", "w_in", layer // 2), ("even_w_out", "w_out", layer // 2)]
    return [("mla_w_down", "w_down", layer // 2), ("mla_w_qb", "w_qb", layer // 2),
            ("mla_w_kvb", "w_kvb", layer // 2), ("mla_w_o", "w_o", layer // 2)]


def _pad_q_heads(w):
    lead = w.shape[:-1]
    w = w.reshape(lead + (2, QK_NOPE + QK_ROPE))
    w = jnp.pad(w, ((0, 0),) * len(lead) + ((0, 0), (0, QHEAD_PAD - QK_NOPE - QK_ROPE)))
    return w.reshape(lead + (2 * QHEAD_PAD,))


def _unpad_q_heads(g):
    lead = g.shape[:-1]
    return g.reshape(lead + (2, QHEAD_PAD))[..., :QK_NOPE + QK_ROPE].reshape(lead + (2 * (QK_NOPE + QK_ROPE),))


def _step(x, positions, loss_target, w, m, v):
    cx, cy, cc = lax.axis_index("x"), lax.axis_index("y"), lax.axis_index("c")
    chip = 2 * cx + cy
    place = jnp.stack([chip, cc, 2 * chip + cc]).astype(jnp.int32)

    prepared = dict(w)
    prepared["mla_w_down"] = jnp.pad(w["mla_w_down"], ((0, 0), (0, 0), (0, ODD_IN_PAD - ODD_IN)))
    prepared["mla_w_qb"] = _pad_q_heads(w["mla_w_qb"])
    small_shard_shapes = [w[k].shape for k in SMALL_SHARDED]
    sources = {g: [prepared[name][idx].astype(BF16) for name, _, idx in _group_keys(g)] for g in GROUPS}
    sources[GROUPS[0]].append(_pack([w[k] for k in SMALL_SHARDED], 32))
    gathering, token = {}, None
    for name, part in (("first", GROUPS[:1]), ("rest", GROUPS[1:])):
        zones = place_own("own_" + name, [a for g in part for a in sources[g]], place)
        groups, off = [], 0
        for g in part:
            groups.append((sources[g], zones[off:off + len(sources[g])]))
            off += len(sources[g])
        started, token = split_start("gather_" + name, groups, 3, gather_plan, token)
        gathering.update(zip(part, started))
    relayed, relay_token = {}, {}

    def prefetch(g, after):
        relayed[g], relay_token[g] = split_relay("relay_" + g, gathering[g], 3, relay_plan, after)
        return relay_token[g]

    prefetch(GROUPS[0], token)

    def weights_of(g, after):
        _, lands = split_wait("gathered_" + g, [relayed[g]], relay_token[g] if g == GROUPS[0] else after)[0]
        if g.startswith("mlp") and g != GROUPS[-1]:
            prefetch(GROUPS[GROUPS.index(g) + 1], lands[0])
        out = {key: land for (_, key, _), land in zip(_group_keys(g), lands)}
        if g == GROUPS[0]:
            per_chip = [_unpack(lands[-1][s], small_shard_shapes) for s in range(N_CHIPS)]
            for i, key in enumerate(("conv_w", "gq", "gkv")):
                out[key] = jnp.concatenate([p[i] for p in per_chip], axis=-1)
        return out

    reducing = []

    def reduce_start(g, srcs, token_in=None):
        lands = [lax.empty((N_DEV, s.shape[1] // 2, s.shape[2]), s.dtype) for s in srcs]
        started, token = split_start("reduce_" + g, [(srcs, lands)], 7, reduce_plan, token_in)
        reducing.append((g, started[0]))
        return token

    def emit_grads(g, grads):
        return reduce_start(g, [grads[key] for _, key, _ in _group_keys(g)])

    sm = {SMALL_KEY[k]: w[k] for k in SMALL if k not in SMALL_SHARDED}
    loss, grad_x, gs = _local_step(x[0], positions.reshape(S, 1), loss_target[0], sm, weights_of, emit_grads, prefetch)
    loss = lax.psum(loss, ("x", "y", "c"))

    small_shapes = [gs[SMALL_KEY[k]].shape for k in SMALL]
    gs_pack = _pack([gs[SMALL_KEY[k]] for k in SMALL], 32)[None]
    small_sent = reduce_start("small", [gs_pack], grad_x)

    grad, delta, new_m, new_v = {}, {}, {}, {}
    late_groups = ("mix0", "small")
    stacks, after = {}, small_sent
    for late in (False, True):
        part = [(g, st) for g, st in reducing if (g in late_groups) == late]
        landed = split_wait("reduced_late" if late else "reduced_early", [st for _, st in part], after)
        for (g, _), (owns, lands) in zip(part, landed):
            if g == "small":
                stacks["small"] = sum_devices("sum_small", lands[0], owns[0], 0, 1, None, place)
                continue
            for (name, key, idx), own, land in zip(_group_keys(g), owns, lands):
                stacks[name] = sum_devices("sum_%s%d" % (key, idx), land, own, idx, w[name].shape[0], stacks.get(name), place)
        names = [k for k in BIG if (k in ("even_w_in", "even_w_out")) == late] + (["small"] if late else [])
        reduced = dict(zip(names, sibling_swap_halves("swap_late" if late else "swap_early", [stacks[k] for k in names])))
        if not late:
            reduced["mla_w_down"] = reduced["mla_w_down"][..., :ODD_IN]
            reduced["mla_w_qb"] = _unpad_q_heads(reduced["mla_w_qb"])
        for k in names:
            if k == "small":
                continue
            grad[k] = reduced[k]
            shp = w[k].shape
            view = lambda a: a.reshape(-1, shp[-1])
            d, nm, nv = adamw("adamw_" + BIG_KEY[k], view(w[k]), view(grad[k]), view(m[k]), view(v[k]))
            delta[k], new_m[k], new_v[k] = d.reshape(shp), nm.reshape(shp), nv.reshape(shp)
            after = d
    g_small = dict(zip(SMALL, _unpack(reduced["small"], small_shapes)))
    for k in SMALL_SHARDED:
        width = w[k].shape[-1]
        g_small[k] = lax.dynamic_slice_in_dim(g_small[k], chip * width, width, axis=-1)
    grad.update(g_small)

    shapes = [w[k].shape for k in SMALL]
    d, nm, nv = adamw("adamw_small", *[_pack([t[k] for k in SMALL], 512) for t in (w, grad, m, v)])
    for k, dk, mk, vk in zip(SMALL, _unpack(d, shapes), _unpack(nm, shapes), _unpack(nv, shapes)):
        delta[k], new_m[k], new_v[k] = dk, mk, vk
    return (loss, grad_x[None], *[grad[k] for k in WEIGHTS], *[delta[k] for k in WEIGHTS],
            *[new_m[k] for k in WEIGHTS], *[new_v[k] for k in WEIGHTS])


def kernel(x, positions, ln_mix_g, ln_mix_b, ln_ffn_g, ln_ffn_b, even_w_in, pool_w, pool_scale, lru_conv_w, lru_conv_b, lru_w_a, lru_b_a, lru_w_x, lru_b_x, lru_lambda, even_w_out, mla_w_down, mla_q_norm_g, mla_kv_norm_g, mla_w_qb, mla_w_kvb, mla_w_o, mlp_w1, mlp_w2, loss_target, m_ln_mix_g, m_ln_mix_b, m_ln_ffn_g, m_ln_ffn_b, m_even_w_in, m_pool_w, m_pool_scale, m_lru_conv_w, m_lru_conv_b, m_lru_w_a, m_lru_b_a, m_lru_w_x, m_lru_b_x, m_lru_lambda, m_even_w_out, m_mla_w_down, m_mla_q_norm_g, m_mla_kv_norm_g, m_mla_w_qb, m_mla_w_kvb, m_mla_w_o, m_mlp_w1, m_mlp_w2, v_ln_mix_g, v_ln_mix_b, v_ln_ffn_g, v_ln_ffn_b, v_even_w_in, v_pool_w, v_pool_scale, v_lru_conv_w, v_lru_conv_b, v_lru_w_a, v_lru_b_a, v_lru_w_x, v_lru_b_x, v_lru_lambda, v_even_w_out, v_mla_w_down, v_mla_q_norm_g, v_mla_kv_norm_g, v_mla_w_qb, v_mla_w_kvb, v_mla_w_o, v_mlp_w1, v_mlp_w2):
    args = locals()
    w = {k: args[k] for k in WEIGHTS}
    m = {k: args["m_" + k] for k in WEIGHTS}
    v = {k: args["v_" + k] for k in WEIGHTS}
    return _step(x, positions, loss_target, w, m, v)
```

```python
import functools
import math

import jax
import jax.numpy as jnp
import numpy as np
from jax import lax
from jax.experimental import pallas as pl
from jax.experimental.pallas import tpu as pltpu

F32 = jnp.float32
BF16 = jnp.bfloat16

S = 2048
D = 1024
DEPTH = 4
N_CHIPS = 4
POOL_WINDOWS = (2, 4, 8, 16)
POOL_W = 512
LRU_W = 1024
LRU_HEADS = 8
HEAD = 128
EVEN_IN = 2560
EVEN_MIX = 1536
MLA_HEADS = 8
QK_NOPE = 128
QK_ROPE = 64
Q_RANK = 384
KV_RANK = 256
ODD_IN = 704
ODD_IN_PAD = 768
QHEAD_PAD = 256
D_FF = 4096
CHUNK = 64
ALPHA = (2 * DEPTH) ** 0.25
LN_EPS = 1e-5
RMS_EPS = 1e-6
ATT_SCALE = (QK_NOPE + QK_ROPE) ** -0.5
ROPE_THETA = 10000.0
LRU_C = 8.0
ADAM_LR = 0.001
ADAM_B1 = 0.9
ADAM_B2 = 0.999
ADAM_EPS = 1e-08
ADAM_WD = 0.01
ADAM_STEP = 10

VMEM_LIMIT = 56 * 1024 * 1024
MESH = pl.DeviceIdType.MESH

_NN = (((1,), (0,)), ((), ()))
_NT = (((1,), (1,)), ((), ()))
_TN = (((0,), (0,)), ((), ()))


def _params(sem=None, **kw):
    return pltpu.CompilerParams(dimension_semantics=sem, vmem_limit_bytes=VMEM_LIMIT, **kw)


def _dot(a, b, dims):
    return lax.dot_general(a.astype(BF16), b.astype(BF16), dims, preferred_element_type=F32)


def _mm(name, a, w4, *, dims, a_block, a_map, w_map, grid, sem, outs, out_blocks, out_maps,
        red_axis=None, n_red=1, extras=(), extra_blocks=(), extra_maps=(), epilogue=None, after=None):
    n_extra = len(extras)
    n_out = len(outs)
    n_after = 0 if after is None else 1
    w_block = (None,) + tuple(w4.shape[1:])

    def body(a_ref, w_ref, *rest):
        rest = rest[n_after:]
        extra_refs = rest[:n_extra]
        out_refs = rest[n_extra:n_extra + n_out]

        def finish(acc):
            vals = epilogue(acc, *[r[...] for r in extra_refs]) if epilogue else (acc,)
            for r, v in zip(out_refs, vals):
                r[...] = v.astype(r.dtype)

        if red_axis is None:
            finish(_dot(a_ref[...], w_ref[...], dims))
        else:
            acc_ref = rest[-1]
            k = pl.program_id(red_axis)

            @pl.when(k == 0)
            def _():
                acc_ref[...] = jnp.zeros_like(acc_ref)

            acc_ref[...] += _dot(a_ref[...], w_ref[...], dims)

            @pl.when(k == n_red - 1)
            def _():
                finish(acc_ref[...])

    scratch = [] if red_axis is None else [pltpu.VMEM(out_blocks[0], F32)]
    return pl.pallas_call(
        body, name=name, grid=grid,
        in_specs=[pl.BlockSpec(a_block, a_map), pl.BlockSpec(w_block, w_map)]
        + [pl.BlockSpec(memory_space=pl.ANY)] * n_after + [pl.BlockSpec(b, m) for b, m in zip(extra_blocks, extra_maps)],
        out_specs=[pl.BlockSpec(b, m) for b, m in zip(out_blocks, out_maps)],
        out_shape=outs, scratch_shapes=scratch, compiler_params=_params(sem),
    )(a, w4, *([after] * n_after), *extras)


def _w_map_inner(i, s):
    return (s, 0, 0)


def mm_cols(name, a, w4, out_dtypes, *, tm=512, epilogue=None):
    m, k = a.shape
    nb = w4.shape[2]
    outs = [jax.ShapeDtypeStruct((m, N_CHIPS * nb), dt) for dt in out_dtypes]
    return _mm(name, a, w4, dims=_NN, a_block=(tm, k), a_map=lambda i, s: (i, 0), w_map=_w_map_inner,
               grid=(m // tm, N_CHIPS), sem=("parallel", "parallel"), outs=outs,
               out_blocks=[(tm, nb)] * len(outs), out_maps=[lambda i, s: (i, s)] * len(outs), epilogue=epilogue)


def mm_rows(name, a, w4, out_dtype, *, tm=512):
    m = a.shape[0]
    kb, n = w4.shape[1:]
    outs = [jax.ShapeDtypeStruct((m, n), out_dtype)]
    return _mm(name, a, w4, dims=_NN, a_block=(tm, kb), a_map=lambda i, s: (i, s), w_map=_w_map_inner,
               grid=(m // tm, N_CHIPS), sem=("parallel", "arbitrary"), outs=outs, out_blocks=[(tm, n)],
               out_maps=[lambda i, s: (i, 0)], red_axis=1, n_red=N_CHIPS)[0]


def mm_rows_ln(name, a, w4, x, g, b, *, tm=512, after=None):
    m = a.shape[0]
    kb, n = w4.shape[1:]
    outs = [jax.ShapeDtypeStruct((m, n), F32), jax.ShapeDtypeStruct((m, n), F32), jax.ShapeDtypeStruct((m, 1), F32)]
    return _mm(name, a, w4, dims=_NN, a_block=(tm, kb), a_map=lambda i, s: (i, s), w_map=_w_map_inner,
               grid=(m // tm, N_CHIPS), sem=("parallel", "arbitrary"), outs=outs, out_blocks=[(tm, n), (tm, n), (tm, 1)],
               out_maps=[lambda i, s: (i, 0)] * 3, red_axis=1, n_red=N_CHIPS, extras=(x, g, b),
               extra_blocks=[(tm, n), (1, n), (1, n)],
               extra_maps=[lambda i, s: (i, 0), lambda i, s: (0, 0), lambda i, s: (0, 0)],
               epilogue=lambda acc, xv, gv, bv: _layer_norm(ALPHA * xv + acc, gv, bv), after=after)


def mm_t_cols(name, g, w4, out_dtype, *, tm=512, resid=None, after=None):
    m = g.shape[0]
    k, nb = w4.shape[1:]
    outs = [jax.ShapeDtypeStruct((m, k), out_dtype)]
    extras = () if resid is None else (resid,)
    epi = None if resid is None else (lambda acc, r: (acc + ALPHA * r,))
    return _mm(name, g, w4, dims=_NT, a_block=(tm, nb), a_map=lambda i, s: (i, s), w_map=_w_map_inner,
               grid=(m // tm, N_CHIPS), sem=("parallel", "arbitrary"), outs=outs, out_blocks=[(tm, k)],
               out_maps=[lambda i, s: (i, 0)], red_axis=1, n_red=N_CHIPS, extras=extras,
               extra_blocks=[(tm, k)] * len(extras), extra_maps=[lambda i, s: (i, 0)] * len(extras), epilogue=epi,
               after=after)[0]


def mm_t_rows(name, g, w4, out_dtype, *, tm=512, resid=None, gate=None, after=None):
    m, n = g.shape
    kb = w4.shape[1]
    outs = [jax.ShapeDtypeStruct((m, N_CHIPS * kb), out_dtype)]
    extras, epi = (), None
    if resid is not None:
        extras, epi = (resid,), (lambda acc, r: (acc + ALPHA * r,))
    if gate is not None:
        extras, epi = (gate,), (lambda acc, r: (acc * (2.0 * r.astype(F32)),))
    return _mm(name, g, w4, dims=_NT, a_block=(tm, n), a_map=lambda i, s: (i, 0), w_map=_w_map_inner,
               grid=(m // tm, N_CHIPS), sem=("parallel", "parallel"), outs=outs, out_blocks=[(tm, kb)],
               out_maps=[lambda i, s: (i, s)], extras=extras, extra_blocks=[(tm, kb)] * len(extras),
               extra_maps=[lambda i, s: (i, s)] * len(extras), epilogue=epi, after=after)[0]


def mm_grad(name, a, g, *, a_sharded, tm=512):
    m = a.shape[0]
    ka = a.shape[1] // N_CHIPS if a_sharded else a.shape[1]
    ng = g.shape[1] if a_sharded else g.shape[1] // N_CHIPS
    n_red = m // tm

    def body(a_ref, g_ref, out_ref, acc_ref):
        k = pl.program_id(0)

        @pl.when(k == 0)
        def _():
            acc_ref[...] = jnp.zeros_like(acc_ref)

        a_v, g_v = a_ref[...].astype(BF16), g_ref[...].astype(BF16)
        for s in range(N_CHIPS):
            a_s = a_v[:, s * ka:(s + 1) * ka] if a_sharded else a_v
            g_s = g_v if a_sharded else g_v[:, s * ng:(s + 1) * ng]
            acc_ref[s] += _dot(a_s, g_s, _TN)

        @pl.when(k == n_red - 1)
        def _():
            out_ref[...] = acc_ref[...].astype(out_ref.dtype)

    return pl.pallas_call(
        body, name=name, grid=(n_red,),
        in_specs=[pl.BlockSpec((tm, a.shape[1]), lambda i: (i, 0)), pl.BlockSpec((tm, g.shape[1]), lambda i: (i, 0))],
        out_specs=pl.BlockSpec((N_CHIPS, ka, ng), lambda i: (0, 0, 0)),
        out_shape=jax.ShapeDtypeStruct((N_CHIPS, ka, ng), BF16),
        scratch_shapes=[pltpu.VMEM((N_CHIPS, ka, ng), F32)], compiler_params=_params(("arbitrary",)),
    )(a, g)


def _layer_norm(z, g, b):
    mu = jnp.mean(z, axis=-1, keepdims=True)
    zc = z - mu
    rstd = lax.rsqrt(jnp.mean(zc * zc, axis=-1, keepdims=True) + LN_EPS)
    xhat = zc * rstd
    return xhat * g + b, xhat, rstd


def mlp_fwd(name, x, w1, w2, g, b, *, tm=512):
    fb = w1.shape[2]

    def body(x_ref, w1_ref, w2_ref, g_ref, b_ref, y_ref, xhat_ref, rstd_ref, relu_ref, acc_ref):
        s = pl.program_id(1)

        @pl.when(s == 0)
        def _():
            acc_ref[...] = jnp.zeros_like(acc_ref)

        r = jnp.maximum(_dot(x_ref[...], w1_ref[...], _NN), 0.0)
        relu_ref[...] = r.astype(relu_ref.dtype)
        acc_ref[...] += _dot(r * r, w2_ref[...], _NN)

        @pl.when(s == N_CHIPS - 1)
        def _():
            y_ref[...], xhat_ref[...], rstd_ref[...] = _layer_norm(ALPHA * x_ref[...] + acc_ref[...], g_ref[...], b_ref[...])

    row = pl.BlockSpec((tm, D), lambda i, s: (i, 0))
    vec = pl.BlockSpec((1, D), lambda i, s: (0, 0))
    return pl.pallas_call(
        body, name=name, grid=(S // tm, N_CHIPS),
        in_specs=[row, pl.BlockSpec((None, D, fb), lambda i, s: (s, 0, 0)),
                  pl.BlockSpec((None, fb, D), lambda i, s: (s, 0, 0)), vec, vec],
        out_specs=[row, row, pl.BlockSpec((tm, 1), lambda i, s: (i, 0)), pl.BlockSpec((tm, fb), lambda i, s: (i, s))],
        out_shape=[jax.ShapeDtypeStruct((S, D), F32), jax.ShapeDtypeStruct((S, D), F32),
                   jax.ShapeDtypeStruct((S, 1), F32), jax.ShapeDtypeStruct((S, N_CHIPS * fb), BF16)],
        scratch_shapes=[pltpu.VMEM((tm, D), F32)], compiler_params=_params(("parallel", "arbitrary")),
    )(x, w1, w2, g, b)


def mlp_bwd(name, dz, relu, x, w1, w2, *, tm=256):
    fb = w1.shape[2]
    n_i = S // tm

    def body(dz_ref, relu_ref, x_ref, w1_ref, w2_ref, dx_ref, g1_ref, g2_ref, acc1_ref, acc2_ref):
        s, i = pl.program_id(0), pl.program_id(1)
        rows = pl.ds(pl.multiple_of(i * tm, tm), tm)
        dz_v = dz_ref[...]

        @pl.when(i == 0)
        def _():
            acc1_ref[...] = jnp.zeros_like(acc1_ref)
            acc2_ref[...] = jnp.zeros_like(acc2_ref)

        @pl.when(s == 0)
        def _():
            dx_ref[rows, :] = ALPHA * dz_v

        dz_b = dz_v.astype(BF16)
        r = relu_ref[...]
        dh = (_dot(dz_b, w2_ref[...], _NT) * (2.0 * r.astype(F32))).astype(BF16)
        p2 = _dot(r * r, dz_b, _TN)
        p1 = _dot(x_ref[...], dh, _TN)
        dx_ref[rows, :] += _dot(dh, w1_ref[...], _NT)
        acc1_ref[...] += p1
        acc2_ref[...] += p2

        @pl.when(i == n_i - 1)
        def _():
            g1_ref[...] = acc1_ref[...].astype(g1_ref.dtype)
            g2_ref[...] = acc2_ref[...].astype(g2_ref.dtype)

    row = pl.BlockSpec((tm, D), lambda s, i: (i, 0))
    return pl.pallas_call(
        body, name=name, grid=(N_CHIPS, n_i),
        in_specs=[row, pl.BlockSpec((tm, fb), lambda s, i: (i, s)), row,
                  pl.BlockSpec((None, D, fb), lambda s, i: (s, 0, 0)), pl.BlockSpec((None, fb, D), lambda s, i: (s, 0, 0))],
        out_specs=[pl.BlockSpec((S, D), lambda s, i: (0, 0)), pl.BlockSpec((None, D, fb), lambda s, i: (s, 0, 0)),
                   pl.BlockSpec((None, fb, D), lambda s, i: (s, 0, 0))],
        out_shape=[jax.ShapeDtypeStruct((S, D), F32), jax.ShapeDtypeStruct((N_CHIPS, D, fb), BF16),
                   jax.ShapeDtypeStruct((N_CHIPS, fb, D), BF16)],
        scratch_shapes=[pltpu.VMEM((D, fb), F32), pltpu.VMEM((fb, D), F32)],
        compiler_params=_params(("arbitrary", "arbitrary")),
    )(dz, relu, x, w1, w2)


def ln_bwd(name, dy, xhat, rstd, g, *, tm=256, after=None):
    n_after = 0 if after is None else 1

    def body(dy_ref, xhat_ref, rstd_ref, g_ref, *rest):
        dz_ref, dg_ref, db_ref = rest[n_after:]
        dy_v = dy_ref[...]
        xh = xhat_ref[...]
        dxh = dy_v * g_ref[...]
        m1 = jnp.mean(dxh, axis=-1, keepdims=True)
        m2 = jnp.mean(dxh * xh, axis=-1, keepdims=True)
        dz_ref[...] = rstd_ref[...] * (dxh - m1 - xh * m2)
        pg = jnp.sum(dy_v * xh, axis=0, keepdims=True)
        pb = jnp.sum(dy_v, axis=0, keepdims=True)
        i = pl.program_id(0)

        @pl.when(i == 0)
        def _():
            dg_ref[...] = pg
            db_ref[...] = pb

        @pl.when(i > 0)
        def _():
            dg_ref[...] += pg
            db_ref[...] += pb

    row = pl.BlockSpec((tm, D), lambda i: (i, 0))
    vec = pl.BlockSpec((1, D), lambda i: (0, 0))
    return pl.pallas_call(
        body, name=name, grid=(S // tm,),
        in_specs=[row, row, pl.BlockSpec((tm, 1), lambda i: (i, 0)), vec] + [pl.BlockSpec(memory_space=pl.ANY)] * n_after,
        out_specs=[row, vec, vec],
        out_shape=[jax.ShapeDtypeStruct((S, D), F32), jax.ShapeDtypeStruct((1, D), F32),
                   jax.ShapeDtypeStruct((1, D), F32)],
        compiler_params=_params(("arbitrary",)),
    )(dy, xhat, rstd, g, *([after] * n_after))


def loss_head(y, target, *, tm=256):
    def body(y_ref, t_ref, dy_ref, loss_ref):
        e = y_ref[...] - t_ref[...]
        dy_ref[...] = e * (1.0 / D)
        part = jnp.sum(jnp.sum(e * e, axis=-1, keepdims=True), axis=0, keepdims=True) * (0.5 / D)
        i = pl.program_id(0)

        @pl.when(i == 0)
        def _():
            loss_ref[...] = jnp.zeros_like(loss_ref)

        loss_ref[...] += jnp.broadcast_to(part, loss_ref.shape)

    row = pl.BlockSpec((tm, D), lambda i: (i, 0))
    return pl.pallas_call(
        body, name="loss_head", grid=(S // tm,), in_specs=[row, row],
        out_specs=[row, pl.BlockSpec((8, 128), lambda i: (0, 0))],
        out_shape=[jax.ShapeDtypeStruct((S, D), F32), jax.ShapeDtypeStruct((8, 128), F32)],
        compiler_params=_params(("arbitrary",)),
    )(y, target)


def _rows(shape):
    return lax.broadcasted_iota(jnp.int32, shape, 0)


def _shift_down(x, k):
    return jnp.where(_rows(x.shape) >= k, pltpu.roll(x, k, 0), 0.0)


def _shift_up(x, k):
    n = x.shape[0]
    return jnp.where(_rows(x.shape) < n - k, pltpu.roll(x, n - k, 0), 0.0)


def _pool_diff(u, w):
    acc, k = u, 1
    while k < w:
        acc = acc + _shift_down(acc, k)
        k *= 2
    cnt = jnp.minimum(_rows(u.shape) + 1, w).astype(F32)
    return acc / cnt - u, cnt


def pool_fwd(name, proj, pool_w, pool_scale):
    def body(u_ref, w_ref, sc_ref, y_ref):
        for g, w in enumerate(POOL_WINDOWS):
            cols = slice(g * HEAD, (g + 1) * HEAD)
            d, _ = _pool_diff(u_ref[:, cols], w)
            z = _dot(d, w_ref[g], _NN)
            y_ref[:, cols] = (z * sc_ref[:, cols]).astype(y_ref.dtype)

    return pl.pallas_call(
        body, name=name, grid=(1,),
        in_specs=[pl.BlockSpec((S, POOL_W), lambda i: (0, 0)),
                  pl.BlockSpec((4, HEAD, HEAD), lambda i: (0, 0, 0)),
                  pl.BlockSpec((1, POOL_W), lambda i: (0, 0))],
        out_specs=pl.BlockSpec((S, POOL_W), lambda i: (0, 0)),
        out_shape=jax.ShapeDtypeStruct((S, POOL_W), BF16),
        compiler_params=_params(("arbitrary",)),
    )(proj, pool_w, pool_scale)


def pool_bwd(name, proj, dycat, pool_w, pool_scale):
    def body(u_ref, dy_ref, w_ref, sc_ref, du_ref, dw_ref, dsc_ref):
        for g, w in enumerate(POOL_WINDOWS):
            cols = slice(g * HEAD, (g + 1) * HEAD)
            d, cnt = _pool_diff(u_ref[:, cols], w)
            dy = dy_ref[:, cols]
            z = _dot(d, w_ref[g], _NN)
            dsc_ref[:, cols] = jnp.sum(dy * z, axis=0, keepdims=True)
            dz = dy * sc_ref[:, cols]
            dw_ref[g] = _dot(d, dz, _TN)
            dd = _dot(dz, w_ref[g], _NT)
            acc, k = dd / cnt, 1
            while k < w:
                acc = acc + _shift_up(acc, k)
                k *= 2
            du_ref[:, cols] = (acc - dd).astype(du_ref.dtype)

    return pl.pallas_call(
        body, name=name, grid=(1,),
        in_specs=[pl.BlockSpec((S, POOL_W), lambda i: (0, 0)),
                  pl.BlockSpec((S, POOL_W), lambda i: (0, 0)),
                  pl.BlockSpec((4, HEAD, HEAD), lambda i: (0, 0, 0)),
                  pl.BlockSpec((1, POOL_W), lambda i: (0, 0))],
        out_specs=[pl.BlockSpec((S, POOL_W), lambda i: (0, 0)),
                   pl.BlockSpec((4, HEAD, HEAD), lambda i: (0, 0, 0)),
                   pl.BlockSpec((1, POOL_W), lambda i: (0, 0))],
        out_shape=[jax.ShapeDtypeStruct((S, POOL_W), BF16), jax.ShapeDtypeStruct((4, HEAD, HEAD), F32),
                   jax.ShapeDtypeStruct((1, POOL_W), F32)],
        compiler_params=_params(("arbitrary",)),
    )(proj, dycat, pool_w, pool_scale)


def _expm1(x):
    series = x * (1.0 + x * (0.5 + x * (1.0 / 6.0 + x * (1.0 / 24.0 + x * (1.0 / 120.0)))))
    return jnp.where(jnp.abs(x) < 0.05, series, jnp.exp(x) - 1.0)


def _softplus_neg(lam):
    e = jnp.exp(-jnp.abs(lam))
    log1p = jnp.where(e < 0.01, e * (1.0 - e * (0.5 - e * (1.0 / 3.0))), jnp.log(1.0 + e))
    return jnp.maximum(-lam, 0.0) + log1p


_GELU_C = math.sqrt(2.0 / math.pi)


def _gelu(x):
    t = jnp.tanh(_GELU_C * (x + 0.044715 * x * x * x))
    return 0.5 * x * (1.0 + t), t


def _gelu_grad(x, t):
    return 0.5 * (1.0 + t) + 0.5 * x * (1.0 - t * t) * _GELU_C * (1.0 + 3.0 * 0.044715 * x * x)


def _conv(u, cw, cb):
    return cw[3:4] * u + cw[2:3] * _shift_down(u, 1) + cw[1:2] * _shift_down(u, 2) + cw[0:1] * _shift_down(u, 3) + cb


def _lru_gates(cu, wa, ba, wx, bx, lam):
    r = jax.nn.sigmoid(_dot(cu, wa, _NN) + ba)
    i = jax.nn.sigmoid(_dot(cu, wx, _NN) + bx)
    sp = _softplus_neg(lam)
    log_a = (-LRU_C) * r * sp
    a = jnp.exp(log_a)
    mult = jnp.sqrt(-_expm1(2.0 * log_a))
    return r, i, sp, a, mult


def _scan(a_ref, b_ref, h_ref, *, reverse):
    n_blk = S // 8
    row8 = lax.broadcasted_iota(jnp.int32, (8, HEAD), 0)

    def step(j, carry):
        blk = (n_blk - 1 - j) if reverse else j
        r0 = pl.multiple_of(blk * 8, 8)
        a = a_ref[pl.ds(r0, 8), :]
        b = b_ref[pl.ds(r0, 8), :]
        for k in (1, 2, 4):
            if reverse:
                keep = row8 < 8 - k
                a_s, b_s = pltpu.roll(a, 8 - k, 0), pltpu.roll(b, 8 - k, 0)
            else:
                keep = row8 >= k
                a_s, b_s = pltpu.roll(a, k, 0), pltpu.roll(b, k, 0)
            b = jnp.where(keep, a * b_s + b, b)
            a = jnp.where(keep, a * a_s, a)
        h = b + a * carry
        h_ref[pl.ds(r0, 8), :] = h
        edge = h[0:1, :] if reverse else h[7:8, :]
        return jnp.broadcast_to(edge, (8, HEAD))

    lax.fori_loop(0, n_blk, step, jnp.zeros((8, HEAD), F32), unroll=4)


def _lru_specs():
    def col(off):
        return pl.BlockSpec((S, HEAD), lambda h: (0, off + h))
    vec = pl.BlockSpec((1, HEAD), lambda h: (0, h))
    mat = pl.BlockSpec((None, HEAD, HEAD), lambda h: (h, 0, 0))
    cw = pl.BlockSpec((4, HEAD), lambda h: (0, h))
    return col, vec, mat, cw


def lru_fwd(name, proj, conv_w, conv_b, w_a, b_a, w_x, b_x, lam):
    def body(u_ref, ug_ref, cw_ref, cb_ref, wa_ref, ba_ref, wx_ref, bx_ref, lam_ref, y_ref, h_ref, a_s, b_s):
        cu = _conv(u_ref[...], cw_ref[...], cb_ref[...])
        _, i, _, a, mult = _lru_gates(cu, wa_ref[...], ba_ref[...], wx_ref[...], bx_ref[...], lam_ref[...])
        a_s[...] = a
        b_s[...] = mult * (i * cu)
        _scan(a_s, b_s, h_ref, reverse=False)
        gl, _ = _gelu(ug_ref[...])
        y_ref[...] = (h_ref[...] * gl).astype(y_ref.dtype)

    col, vec, mat, cw = _lru_specs()
    out = pl.BlockSpec((S, HEAD), lambda h: (0, h))
    return pl.pallas_call(
        body, name=name, grid=(LRU_HEADS,),
        in_specs=[col(4), col(12), cw, vec, mat, vec, mat, vec, vec],
        out_specs=[out, out],
        out_shape=[jax.ShapeDtypeStruct((S, LRU_W), BF16), jax.ShapeDtypeStruct((S, LRU_W), F32)],
        scratch_shapes=[pltpu.VMEM((S, HEAD), F32), pltpu.VMEM((S, HEAD), F32)],
        compiler_params=_params(("parallel",)),
    )(proj, proj, conv_w, conv_b, w_a, b_a, w_x, b_x, lam)


def lru_bwd(name, proj, hstate, dycat, conv_w, conv_b, w_a, b_a, w_x, b_x, lam):
    def body(u_ref, ug_ref, h_ref, dy_ref, cw_ref, cb_ref, wa_ref, ba_ref, wx_ref, bx_ref, lam_ref,
             du_ref, dug_ref, dwa_ref, dwx_ref, dba_ref, dbx_ref, dlam_ref, dcw_ref, dcb_ref, a_s, b_s, g_s):
        u = u_ref[...]
        cw = cw_ref[...]
        cu = _conv(u, cw, cb_ref[...])
        lam_v = lam_ref[...]
        r, i, sp, a, mult = _lru_gates(cu, wa_ref[...], ba_ref[...], wx_ref[...], bx_ref[...], lam_v)
        ug = ug_ref[...]
        gl, t = _gelu(ug)
        dy = dy_ref[...]
        h = h_ref[...]
        dug_ref[...] = (dy * h * _gelu_grad(ug, t)).astype(dug_ref.dtype)
        a_s[...] = _shift_up(a, 1)
        b_s[...] = dy * gl
        _scan(a_s, b_s, g_s, reverse=True)
        dxin = g_s[...]
        da = dxin * _shift_down(h, 1)
        dmult = dxin * (i * cu)
        di = dxin * (mult * cu)
        dlog_a = da * a - dmult * (a * a) / mult
        dr_pre = dlog_a * ((-LRU_C) * sp) * (r * (1.0 - r))
        di_pre = di * (i * (1.0 - i))
        dsp = jnp.sum(dlog_a * ((-LRU_C) * r), axis=0, keepdims=True)
        dlam_ref[...] = dsp * (-jax.nn.sigmoid(-lam_v))
        dba_ref[...] = jnp.sum(dr_pre, axis=0, keepdims=True)
        dbx_ref[...] = jnp.sum(di_pre, axis=0, keepdims=True)
        dwa_ref[...] = _dot(cu, dr_pre, _TN)
        dwx_ref[...] = _dot(cu, di_pre, _TN)
        dcu = dxin * (mult * i) + _dot(dr_pre, wa_ref[...], _NT) + _dot(di_pre, wx_ref[...], _NT)
        dcb_ref[...] = jnp.sum(dcu, axis=0, keepdims=True)
        for k in range(4):
            dcw_ref[k:k + 1, :] = jnp.sum(dcu * (_shift_down(u, 3 - k) if k < 3 else u), axis=0, keepdims=True)
        du = cw[3:4] * dcu + cw[2:3] * _shift_up(dcu, 1) + cw[1:2] * _shift_up(dcu, 2) + cw[0:1] * _shift_up(dcu, 3)
        du_ref[...] = du.astype(du_ref.dtype)

    col, vec, mat, cw = _lru_specs()
    out = pl.BlockSpec((S, HEAD), lambda h: (0, h))
    big = jax.ShapeDtypeStruct((S, LRU_W), BF16)
    vec_shape = jax.ShapeDtypeStruct((1, LRU_W), F32)
    mat_shape = jax.ShapeDtypeStruct((LRU_HEADS, HEAD, HEAD), F32)
    return pl.pallas_call(
        body, name=name, grid=(LRU_HEADS,),
        in_specs=[col(4), col(12), out, col(4), cw, vec, mat, vec, mat, vec, vec],
        out_specs=[out, out, mat, mat, vec, vec, vec, cw, vec],
        out_shape=[big, big, mat_shape, mat_shape, vec_shape, vec_shape, vec_shape,
                   jax.ShapeDtypeStruct((4, LRU_W), F32), vec_shape],
        scratch_shapes=[pltpu.VMEM((S, HEAD), F32)] * 3,
        compiler_params=_params(("parallel",)),
    )(proj, proj, hstate, dycat, conv_w, conv_b, w_a, b_a, w_x, b_x, lam)


def rope_tables(pos_col, inv_freq):
    def body(pos_ref, f_ref, c_ref, s1_ref, s2_ref):
        ang = pos_ref[...].astype(F32) * f_ref[...]
        lane = lax.broadcasted_iota(jnp.int32, ang.shape, 1)
        cos, sin = jnp.cos(ang), jnp.sin(ang)
        c_ref[...] = jnp.where(lane < QK_ROPE, cos, 0.0)
        s1_ref[...] = jnp.where(lane < QK_ROPE // 2, -sin, 0.0)
        s2_ref[...] = jnp.where((lane >= QK_ROPE // 2) & (lane < QK_ROPE), sin, 0.0)

    tab = jax.ShapeDtypeStruct((S, HEAD), F32)
    return pl.pallas_call(
        body, name="rope_tables", grid=(1,),
        in_specs=[pl.BlockSpec((S, 1), lambda i: (0, 0)), pl.BlockSpec((1, HEAD), lambda i: (0, 0))],
        out_specs=[pl.BlockSpec((S, HEAD), lambda i: (0, 0))] * 3, out_shape=[tab, tab, tab],
        compiler_params=_params(("arbitrary",)),
    )(pos_col, inv_freq)


def _rope(v, c, s1, s2):
    return v * c + pltpu.roll(v, HEAD - QK_ROPE // 2, 1) * s1 + pltpu.roll(v, QK_ROPE // 2, 1) * s2


def _unrope(d, c, s1, s2):
    return d * c + pltpu.roll(d * s1, QK_ROPE // 2, 1) + pltpu.roll(d * s2, HEAD - QK_ROPE // 2, 1)


def _rms(x, g):
    rstd = lax.rsqrt(jnp.mean(x * x, axis=-1, keepdims=True) + RMS_EPS)
    return x * rstd, rstd


def mla_prep(name, down, gq, gkv, tabs, *, tm=256):
    def body(dn_ref, gq_ref, gkv_ref, c_ref, s1_ref, s2_ref, cq_ref, ckv_ref, kp_ref):
        xq, _ = _rms(dn_ref[:, :Q_RANK], None)
        cq_ref[...] = (xq * gq_ref[...]).astype(cq_ref.dtype)
        xkv, _ = _rms(dn_ref[:, Q_RANK:Q_RANK + KV_RANK], None)
        ckv_ref[...] = (xkv * gkv_ref[...]).astype(ckv_ref.dtype)
        kp = _rope(dn_ref[:, Q_RANK + KV_RANK:], c_ref[...], s1_ref[...], s2_ref[...])
        kp_ref[...] = kp.astype(kp_ref.dtype)

    tab = pl.BlockSpec((tm, HEAD), lambda i: (i, 0))
    return pl.pallas_call(
        body, name=name, grid=(S // tm,),
        in_specs=[pl.BlockSpec((tm, ODD_IN_PAD), lambda i: (i, 0)), pl.BlockSpec((1, Q_RANK), lambda i: (0, 0)),
                  pl.BlockSpec((1, KV_RANK), lambda i: (0, 0)), tab, tab, tab],
        out_specs=[pl.BlockSpec((tm, Q_RANK), lambda i: (i, 0)), pl.BlockSpec((tm, KV_RANK), lambda i: (i, 0)), tab],
        out_shape=[jax.ShapeDtypeStruct((S, Q_RANK), BF16), jax.ShapeDtypeStruct((S, KV_RANK), BF16),
                   jax.ShapeDtypeStruct((S, HEAD), BF16)],
        compiler_params=_params(("parallel",)),
    )(down, gq, gkv, *tabs)


def mla_prep_bwd(name, down, dcq, dckv, dkp, gq, gkv, tabs, *, tm=256):
    def body(dn_ref, dcq_ref, dckv_ref, dkp_ref, gq_ref, gkv_ref, c_ref, s1_ref, s2_ref, dd_ref, dgq_ref, dgkv_ref):
        i = pl.program_id(0)

        def rms_bwd(x, dy, g, dg_ref):
            xh, rstd = _rms(x, None)
            dxh = dy * g
            dx = rstd * (dxh - xh * jnp.mean(dxh * xh, axis=-1, keepdims=True))
            pg = jnp.sum(dy * xh, axis=0, keepdims=True)

            @pl.when(i == 0)
            def _():
                dg_ref[...] = pg

            @pl.when(i > 0)
            def _():
                dg_ref[...] += pg

            return dx

        dxq = rms_bwd(dn_ref[:, :Q_RANK], dcq_ref[...], gq_ref[...], dgq_ref)
        dd_ref[:, :Q_RANK] = dxq.astype(dd_ref.dtype)
        dxkv = rms_bwd(dn_ref[:, Q_RANK:Q_RANK + KV_RANK], dckv_ref[...], gkv_ref[...], dgkv_ref)
        dd_ref[:, Q_RANK:Q_RANK + KV_RANK] = dxkv.astype(dd_ref.dtype)
        dd_ref[:, Q_RANK + KV_RANK:] = _unrope(dkp_ref[...], c_ref[...], s1_ref[...], s2_ref[...]).astype(dd_ref.dtype)

    tab = pl.BlockSpec((tm, HEAD), lambda i: (i, 0))
    vq = pl.BlockSpec((1, Q_RANK), lambda i: (0, 0))
    vkv = pl.BlockSpec((1, KV_RANK), lambda i: (0, 0))
    return pl.pallas_call(
        body, name=name, grid=(S // tm,),
        in_specs=[pl.BlockSpec((tm, ODD_IN_PAD), lambda i: (i, 0)), pl.BlockSpec((tm, Q_RANK), lambda i: (i, 0)),
                  pl.BlockSpec((tm, KV_RANK), lambda i: (i, 0)), tab, vq, vkv, tab, tab, tab],
        out_specs=[pl.BlockSpec((tm, ODD_IN_PAD), lambda i: (i, 0)), vq, vkv],
        out_shape=[jax.ShapeDtypeStruct((S, ODD_IN_PAD), BF16), jax.ShapeDtypeStruct((1, Q_RANK), F32),
                   jax.ShapeDtypeStruct((1, KV_RANK), F32)],
        compiler_params=_params(("arbitrary",)),
    )(down, dcq, dckv, dkp, gq, gkv, *tabs)


ATT_TQ = 256


def _attn_probs(q_ref, kv_ref, kp_ref, c_ref, s1_ref, s2_ref, i, nk):
    qn = q_ref[:, :HEAD].astype(BF16)
    qp = _rope(q_ref[:, HEAD:], c_ref[...], s1_ref[...], s2_ref[...]).astype(BF16)
    kn = kv_ref[:nk, :HEAD]
    sc = (_dot(qn, kn, _NT) + _dot(qp, kp_ref[:nk, :], _NT)) * ATT_SCALE
    q_chunk = (i * ATT_TQ + lax.broadcasted_iota(jnp.int32, sc.shape, 0)) // CHUNK
    k_chunk = lax.broadcasted_iota(jnp.int32, sc.shape, 1) // CHUNK
    sc = jnp.where(k_chunk <= q_chunk, sc, jnp.finfo(F32).min)
    e = jnp.exp(sc - jnp.max(sc, axis=-1, keepdims=True))
    p = e * (1.0 / jnp.sum(e, axis=-1, keepdims=True))
    return p, qn, qp, kn


def _for_each_prefix(i, fn):
    for k in range(S // ATT_TQ):
        pl.when(i == k)(functools.partial(fn, (k + 1) * ATT_TQ))


def _attn_specs():
    q = pl.BlockSpec((ATT_TQ, QHEAD_PAD), lambda h, i: (i, h))
    kv = pl.BlockSpec((S, QHEAD_PAD), lambda h, i: (0, h))
    kp = pl.BlockSpec((S, HEAD), lambda h, i: (0, 0))
    tab = pl.BlockSpec((ATT_TQ, HEAD), lambda h, i: (i, 0))
    o = pl.BlockSpec((ATT_TQ, HEAD), lambda h, i: (i, h))
    return q, kv, kp, tab, o


def attn_fwd(name, q, kv, kp, tabs):
    def body(q_ref, kv_ref, kp_ref, c_ref, s1_ref, s2_ref, o_ref):
        i = pl.program_id(1)

        def run(nk):
            p, _, _, _ = _attn_probs(q_ref, kv_ref, kp_ref, c_ref, s1_ref, s2_ref, i, nk)
            o_ref[...] = _dot(p, kv_ref[:nk, HEAD:], _NN).astype(o_ref.dtype)

        _for_each_prefix(i, run)

    qs, kvs, kps, tab, os = _attn_specs()
    return pl.pallas_call(
        body, name=name, grid=(MLA_HEADS, S // ATT_TQ), in_specs=[qs, kvs, kps, tab, tab, tab], out_specs=os,
        out_shape=jax.ShapeDtypeStruct((S, MLA_HEADS * HEAD), BF16),
        compiler_params=_params(("parallel", "parallel")),
    )(q, kv, kp, *tabs)


def attn_bwd(name, q, kv, kp, do, tabs):
    def body(q_ref, kv_ref, kp_ref, do_ref, c_ref, s1_ref, s2_ref, dq_ref, dkv_ref, dkp_ref):
        h, i = pl.program_id(0), pl.program_id(1)

        @pl.when(i == 0)
        def _():
            dkv_ref[...] = jnp.zeros_like(dkv_ref)

        @pl.when((i == 0) & (h == 0))
        def _():
            dkp_ref[...] = jnp.zeros_like(dkp_ref)

        def run(nk):
            p, qn, qp, kn = _attn_probs(q_ref, kv_ref, kp_ref, c_ref, s1_ref, s2_ref, i, nk)
            do_v = do_ref[...]
            dp = _dot(do_v, kv_ref[:nk, HEAD:], _NT)
            ds = (p * (dp - jnp.sum(p * dp, axis=-1, keepdims=True)) * ATT_SCALE).astype(BF16)
            dq_ref[:, :HEAD] = _dot(ds, kn, _NN).astype(dq_ref.dtype)
            dqp = _unrope(_dot(ds, kp_ref[:nk, :], _NN), c_ref[...], s1_ref[...], s2_ref[...])
            dq_ref[:, HEAD:] = dqp.astype(dq_ref.dtype)
            dkv_ref[:nk, :HEAD] += _dot(ds, qn, _TN)
            dkv_ref[:nk, HEAD:] += _dot(p, do_v, _TN)
            dkp_ref[:nk, :] += _dot(ds, qp, _TN)

        _for_each_prefix(i, run)

    qs, kvs, kps, tab, os = _attn_specs()
    return pl.pallas_call(
        body, name=name, grid=(MLA_HEADS, S // ATT_TQ), in_specs=[qs, kvs, kps, os, tab, tab, tab],
        out_specs=[qs, kvs, kps],
        out_shape=[jax.ShapeDtypeStruct((S, MLA_HEADS * QHEAD_PAD), BF16),
                   jax.ShapeDtypeStruct((S, MLA_HEADS * QHEAD_PAD), F32), jax.ShapeDtypeStruct((S, HEAD), F32)],
        compiler_params=_params(("arbitrary", "arbitrary")),
    )(q, kv, kp, do, *tabs)


def adamw(name, w, g, m, v):
    rows, cols = w.shape
    tr = rows
    for cand in (512, 256, 128, 64, 32, 16, 8):
        if rows % cand == 0 and cand * cols * 4 <= 2 * 1024 * 1024:
            tr = cand
            break

    def body(w_ref, g_ref, m_ref, v_ref, d_ref, nm_ref, nv_ref):
        g_v = g_ref[...]
        nm = ADAM_B1 * m_ref[...] + (1.0 - ADAM_B1) * g_v
        nv = ADAM_B2 * v_ref[...] + (1.0 - ADAM_B2) * (g_v * g_v)
        m_hat = nm / (1.0 - ADAM_B1 ** ADAM_STEP)
        v_hat = nv / (1.0 - ADAM_B2 ** ADAM_STEP)
        d_ref[...] = (-ADAM_LR) * (m_hat / (jnp.sqrt(v_hat) + ADAM_EPS) + ADAM_WD * w_ref[...])
        nm_ref[...] = nm
        nv_ref[...] = nv

    blk = pl.BlockSpec((tr, cols), lambda i: (i, 0))
    shape = jax.ShapeDtypeStruct((rows, cols), F32)
    return pl.pallas_call(
        body, name=name, grid=(rows // tr,), in_specs=[blk] * 4, out_specs=[blk] * 3, out_shape=[shape] * 3,
        compiler_params=_params(("parallel",)),
    )(w, g, m, v)


def _local_step(x, pos_col, target, sm, weights_of, emit_grads, prefetch):
    inv_freq = ROPE_THETA ** (-jnp.arange(0, QK_ROPE, 2, dtype=F32) / QK_ROPE)
    inv_freq = jnp.concatenate([inv_freq, inv_freq, jnp.zeros((HEAD - QK_ROPE,), F32)])[None, :]
    tabs = rope_tables(pos_col, inv_freq)
    saved, wts = [], {}
    for layer in range(DEPTH):
        j = layer // 2
        n = "l%d_" % layer
        sv = {"x": x}
        wm = wts["mix%d" % layer] = weights_of("mix%d" % layer, x)
        if layer == 0:
            sm = dict(sm, conv_w=wm["conv_w"], gq=wm["gq"], gkv=wm["gkv"])
        if layer % 2 == 0:
            proj = mm_cols(n + "proj", x, wm["w_in"], [F32], tm=1024)[0]
            fetched = prefetch("mlp%d" % layer, proj)
            y_pool = pool_fwd(n + "pool", proj, sm["pool_w"][j], sm["pool_scale"][j][None])
            y_lru, hstate = lru_fwd(n + "lru", proj, sm["conv_w"][j], sm["conv_b"][j][None], sm["w_a"][j],
                                    sm["b_a"][j][None], sm["w_x"][j], sm["b_x"][j][None], sm["lam"][j][None])
            ycat = jnp.concatenate([y_pool, y_lru], axis=1)
            mix_in, w_mix = ycat, wm["w_out"]
            sv.update(proj=proj, hstate=hstate, ycat=ycat)
        else:
            down = mm_rows(n + "down", x, wm["w_down"], F32)
            fetched = prefetch("mlp%d" % layer, down)
            cq, ckv, kp = mla_prep(n + "prep", down, sm["gq"][j][None], sm["gkv"][j][None], tabs)
            q = mm_cols(n + "q", cq, wm["w_qb"], [F32])[0]
            kv = mm_cols(n + "kv", ckv, wm["w_kvb"], [BF16])[0]
            o = attn_fwd(n + "attn", q, kv, kp, tabs)
            mix_in, w_mix = o, wm["w_o"]
            sv.update(down=down, cq=cq, ckv=ckv, kp=kp, q=q, kv=kv, o=o)
        x1, xhat1, rstd1 = mm_rows_ln(n + "mixout", mix_in, w_mix, x, sm["ln_mix_g"][layer][None],
                                      sm["ln_mix_b"][layer][None], after=fetched)
        wf = wts["mlp%d" % layer] = weights_of("mlp%d" % layer, x1)
        x2, xhat2, rstd2, relu = mlp_fwd(n + "mlp", x1, wf["w1"], wf["w2"], sm["ln_ffn_g"][layer][None],
                                         sm["ln_ffn_b"][layer][None])
        sv.update(xhat1=xhat1, rstd1=rstd1, x1=x1, relu=relu, xhat2=xhat2, rstd2=rstd2)
        saved.append(sv)
        x = x2

    dx, loss_tile = loss_head(x, target)
    gs = {k: [None] * (DEPTH if k.startswith("ln_") else DEPTH // 2) for k in sm}
    for layer in reversed(range(DEPTH)):
        j = layer // 2
        n = "l%d_" % layer
        sv = saved[layer]
        wm, wf = wts["mix%d" % layer], wts["mlp%d" % layer]
        dz, gs["ln_ffn_g"][layer], gs["ln_ffn_b"][layer] = ln_bwd(
            n + "ln_ffn_b", dx, sv["xhat2"], sv["rstd2"], sm["ln_ffn_g"][layer][None])
        dx, g_w1, g_w2 = mlp_bwd(n + "mlp_b", dz, sv["relu"], sv["x1"], wf["w1"], wf["w2"])
        sent = emit_grads("mlp%d" % layer, {"w1": g_w1, "w2": g_w2})
        dz, gs["ln_mix_g"][layer], gs["ln_mix_b"][layer] = ln_bwd(
            n + "ln_mix_b", dx, sv["xhat1"], sv["rstd1"], sm["ln_mix_g"][layer][None], after=sent)
        if layer % 2 == 0:
            dycat = mm_t_rows(n + "mixout_dx", dz, wm["w_out"], F32)
            g_out = mm_grad(n + "mixout_dw", sv["ycat"], dz, a_sharded=True)
            du_pool, gs["pool_w"][j], gs["pool_scale"][j] = pool_bwd(
                n + "pool_b", sv["proj"], dycat, sm["pool_w"][j], sm["pool_scale"][j][None])
            (du_lru, du_gate, gs["w_a"][j], gs["w_x"][j], gs["b_a"][j], gs["b_x"][j], gs["lam"][j], gs["conv_w"][j],
             gs["conv_b"][j]) = lru_bwd(n + "lru_b", sv["proj"], sv["hstate"], dycat, sm["conv_w"][j],
                                        sm["conv_b"][j][None], sm["w_a"][j], sm["b_a"][j][None], sm["w_x"][j],
                                        sm["b_x"][j][None], sm["lam"][j][None])
            dproj = jnp.concatenate([du_pool, du_lru, du_gate], axis=1)
            g_in = mm_grad(n + "proj_dw", sv["x"], dproj, a_sharded=False)
            sent = emit_grads("mix%d" % layer, {"w_in": g_in, "w_out": g_out})
            dx = mm_t_cols(n + "proj_dx", dproj, wm["w_in"], F32, resid=dz, after=sent)
        else:
            do = mm_t_rows(n + "attnout_dx", dz, wm["w_o"], BF16)
            g_o = mm_grad(n + "attnout_dw", sv["o"], dz, a_sharded=True)
            dq, dkv, dkp = attn_bwd(n + "attn_b", sv["q"], sv["kv"], sv["kp"], do, tabs)
            g_qb = mm_grad(n + "q_dw", sv["cq"], dq, a_sharded=False)
            dcq = mm_t_cols(n + "q_dx", dq, wm["w_qb"], F32)
            g_kvb = mm_grad(n + "kv_dw", sv["ckv"], dkv, a_sharded=False)
            dckv = mm_t_cols(n + "kv_dx", dkv, wm["w_kvb"], F32)
            ddown, gs["gq"][j], gs["gkv"][j] = mla_prep_bwd(
                n + "prep_b", sv["down"], dcq, dckv, dkp, sm["gq"][j][None], sm["gkv"][j][None], tabs)
            g_down = mm_grad(n + "down_dw", sv["x"], ddown, a_sharded=True)
            sent = emit_grads("mix%d" % layer, {"w_down": g_down, "w_qb": g_qb, "w_kvb": g_kvb, "w_o": g_o})
            dx = mm_t_rows(n + "down_dx", ddown, wm["w_down"], F32, resid=dz, after=sent)
    gs = {k: jnp.stack([a.reshape(sm[k].shape[1:]) for a in v]) for k, v in gs.items()}
    return loss_tile[0, 0], dx, gs


def _place():
    x, y, c = lax.axis_index("x"), lax.axis_index("y"), lax.axis_index("c")
    chips = [(1 - x, y), (x, 1 - y), (1 - x, 1 - y)]
    return x, y, c, chips


def _hbm_call(body, name, args, out_shape, scratch, aliases=None):
    return pl.pallas_call(
        body, name=name, in_specs=[pl.BlockSpec(memory_space=pl.ANY)] * len(args),
        out_specs=[pl.BlockSpec(memory_space=pl.ANY)] * len(out_shape), out_shape=out_shape,
        scratch_shapes=scratch, input_output_aliases=aliases or {},
        compiler_params=pltpu.CompilerParams(has_side_effects=True),
    )(*args)


HBM_SPEC = pl.BlockSpec(memory_space=pltpu.HBM)
SEM_SPEC = pl.BlockSpec(memory_space=pltpu.SEMAPHORE)
EFFECT = pltpu.SideEffectType.DATAFLOW_SIDE_EFFECTING


def _remote(src, dst, send_sem, recv_sem, device):
    return pltpu.make_async_remote_copy(src_ref=src, dst_ref=dst, send_sem=send_sem, recv_sem=recv_sem,
                                        device_id=device, device_id_type=MESH)


def place_own(name, srcs, place, steps=4):
    n = len(srcs)
    in_specs, out_specs, out_shape = [], [], []
    for s in srcs:
        rows, cols = s.shape
        tr = rows // steps
        in_specs.append(pl.BlockSpec((tr, cols), lambda i, p: (i, 0)))
        out_specs.append(pl.BlockSpec((None, tr, cols), lambda i, p: (p[0], i, 0)))
        out_shape.append(jax.ShapeDtypeStruct((N_CHIPS, rows, cols), s.dtype))

    def body(p_ref, *refs):
        for i_ref, o_ref in zip(refs[:n], refs[n:]):
            o_ref[...] = i_ref[...]

    return pl.pallas_call(
        body, name=name, out_shape=out_shape,
        grid_spec=pltpu.PrefetchScalarGridSpec(num_scalar_prefetch=1, grid=(steps,), in_specs=in_specs,
                                               out_specs=out_specs),
        compiler_params=_params(("parallel",)),
    )(place, *srcs)


def split_start(name, groups, n_sems, plan, token_in=None):
    sizes = [len(srcs) for srcs, _ in groups]
    n, n_groups = sum(sizes), len(groups)
    srcs = [pltpu.with_memory_space_constraint(a, pltpu.HBM) for s, _ in groups for a in s]
    lands = [pltpu.with_memory_space_constraint(a, pltpu.HBM) for _, l in groups for a in l]
    extra = [] if token_in is None else [token_in]

    def body(*refs):
        src_refs, land_refs = refs[:n], refs[n:2 * n]
        outs = refs[2 * n + len(extra):]
        off = 0
        for g, size in enumerate(sizes):
            sends, _ = plan(src_refs[off:off + size], land_refs[off:off + size], outs[2 * g], outs[2 * g + 1])
            for cp in sends:
                cp.start()
            off += size
        token = outs[2 * n_groups + 2 * n]
        token[...] = jnp.zeros_like(token)

    sems = [pltpu.SemaphoreType.DMA((size * n_sems,)) for size in sizes for _ in range(2)]
    res = pl.pallas_call(
        body, name=name,
        out_shape=(*sems, *[pltpu.HBM(a.shape, a.dtype) for a in srcs + lands], jax.ShapeDtypeStruct((8, 128), F32)),
        in_specs=[HBM_SPEC] * (2 * n) + [pl.BlockSpec(memory_space=pl.ANY)] * len(extra),
        out_specs=(*[SEM_SPEC] * len(sems), *[HBM_SPEC] * (2 * n), pl.BlockSpec(memory_space=pltpu.VMEM)),
        input_output_aliases={i: len(sems) + i for i in range(2 * n)},
        compiler_params=pltpu.CompilerParams(has_side_effects=EFFECT),
    )(*srcs, *lands, *extra)
    started, off = [], 0
    bufs = res[len(sems):]
    for g, size in enumerate(sizes):
        started.append(dict(send=res[2 * g], recv=res[2 * g + 1], srcs=list(bufs[off:off + size]),
                            lands=list(bufs[n + off:n + off + size]), plan=plan))
        off += size
    return started, res[-1]


def _wait_started(st, bufs, send_sems, recv_sems):
    n = len(st["srcs"])
    sends, expects = st["plan"](bufs[:n], bufs[n:], send_sems, recv_sems)
    for cp in sends:
        cp.wait_send()
    for cp in expects:
        cp.wait_recv()


def split_wait(name, started, after):
    sizes = [len(st["srcs"]) + len(st["lands"]) for st in started]
    n_buf = sum(sizes)

    def body(*refs):
        bufs, sems = refs[:n_buf], refs[n_buf:n_buf + 2 * len(started)]
        off = 0
        for g, (st, n) in enumerate(zip(started, sizes)):
            _wait_started(st, bufs[off:off + n], sems[2 * g], sems[2 * g + 1])
            off += n

    bufs = [a for st in started for a in st["srcs"] + st["lands"]]
    sems = [s for st in started for s in (st["send"], st["recv"])]
    res = pl.pallas_call(
        body, name=name, out_shape=tuple(pltpu.HBM(a.shape, a.dtype) for a in bufs),
        in_specs=[HBM_SPEC] * n_buf + [SEM_SPEC] * len(sems) + [pl.BlockSpec(memory_space=pl.ANY)],
        out_specs=tuple([HBM_SPEC] * n_buf), input_output_aliases={i: i for i in range(n_buf)},
        compiler_params=pltpu.CompilerParams(has_side_effects=EFFECT),
    )(*bufs, *sems, after)
    out, off = [], 0
    for st, n in zip(started, sizes):
        out.append((list(res[off:off + len(st["srcs"])]), list(res[off + len(st["srcs"]):off + n])))
        off += n
    return out


def split_relay(name, st, n_sems, plan, after):
    n_src, n = len(st["srcs"]), len(st["lands"])

    def body(*refs):
        bufs = refs[:n_src + n]
        outs = refs[n_src + n + 3:]
        _wait_started(st, bufs, refs[n_src + n], refs[n_src + n + 1])
        sends, _ = plan((), bufs[n_src:], outs[0], outs[1])
        for cp in sends:
            cp.start()
        outs[2 + n][...] = jnp.zeros((8, 128), F32)

    bufs = st["srcs"] + st["lands"]
    res = pl.pallas_call(
        body, name=name,
        out_shape=(pltpu.SemaphoreType.DMA((n * n_sems,)), pltpu.SemaphoreType.DMA((n * n_sems,)),
                   *[pltpu.HBM(a.shape, a.dtype) for a in st["lands"]], jax.ShapeDtypeStruct((8, 128), F32)),
        in_specs=[HBM_SPEC] * (n_src + n) + [SEM_SPEC] * 2 + [pl.BlockSpec(memory_space=pl.ANY)],
        out_specs=(SEM_SPEC, SEM_SPEC, *[HBM_SPEC] * n, pl.BlockSpec(memory_space=pltpu.VMEM)),
        input_output_aliases={n_src + i: 2 + i for i in range(n)},
        compiler_params=pltpu.CompilerParams(has_side_effects=EFFECT),
    )(*bufs, st["send"], st["recv"], after)
    return dict(send=res[0], recv=res[1], srcs=[], lands=list(res[2:2 + n]), plan=plan), res[-1]


def gather_plan(src_refs, land_refs, send_sems, recv_sems):
    x, y, c, chips = _place()
    me = 2 * x + y
    sends, expects = [], []
    for k, (s, d) in enumerate(zip(src_refs, land_refs)):
        mine = pl.ds(c * (s.shape[0] // 2), s.shape[0] // 2)
        for j, (px, py) in enumerate(chips):
            sem = 3 * k + j
            sends.append(_remote(s.at[mine], d.at[me, mine], send_sems.at[sem], recv_sems.at[sem], (px, py, c)))
            expects.append(_remote(s.at[mine], d.at[2 * px + py, mine], send_sems.at[sem], recv_sems.at[sem], (px, py, c)))
    return sends, expects


def relay_plan(src_refs, land_refs, send_sems, recv_sems):
    x, y, c, chips = _place()
    sends, expects = [], []
    for k, d in enumerate(land_refs):
        hr = d.shape[1] // 2
        mine, theirs = pl.ds(c * hr, hr), pl.ds((1 - c) * hr, hr)
        for j, (px, py) in enumerate(chips):
            sem, chip = 3 * k + j, 2 * px + py
            sends.append(_remote(d.at[chip, mine], d.at[chip, mine], send_sems.at[sem], recv_sems.at[sem], (x, y, 1 - c)))
            expects.append(_remote(d.at[chip, mine], d.at[chip, theirs], send_sems.at[sem], recv_sems.at[sem], (x, y, 1 - c)))
    return sends, expects


def _reduce_part(s, chip_idx, h):
    hr = s.shape[1] // 2
    return s.at[chip_idx if s.shape[0] == N_CHIPS else 0, pl.ds(h * hr, hr)]


def reduce_plan(src_refs, land_refs, send_sems, recv_sems):
    x, y, c, chips = _place()
    me_chip, me_dev = 2 * x + y, 4 * x + 2 * y + c
    sends, expects = [], []
    for k, (s, d) in enumerate(zip(src_refs, land_refs)):
        part = functools.partial(_reduce_part, s)
        for j, (px, py) in enumerate(chips):
            for h in range(2):
                sends.append(_remote(part(2 * px + py, h), d.at[me_dev], send_sems.at[7 * k + 2 * j + h],
                                     recv_sems.at[7 * k + 2 * j + c], (px, py, h)))
                expects.append(_remote(part(me_chip, c), d.at[4 * px + 2 * py + h], send_sems.at[7 * k + 2 * j + h],
                                       recv_sems.at[7 * k + 2 * j + h], (px, py, h)))
        sends.append(_remote(part(me_chip, 1 - c), d.at[me_dev], send_sems.at[7 * k + 6], recv_sems.at[7 * k + 6],
                             (x, y, 1 - c)))
        expects.append(_remote(part(me_chip, c), d.at[me_dev + 1 - 2 * c], send_sems.at[7 * k + 6],
                               recv_sems.at[7 * k + 6], (x, y, 1 - c)))
    return sends, expects


def sibling_swap_halves(name, fulls):
    n = len(fulls)

    def body(*refs):
        outs = refs[n:2 * n]
        send_sems, recv_sems = refs[2 * n:]
        x, y, c, _ = _place()
        copies = []
        for k in range(n):
            nl, rows = fulls[k].shape[:2]
            hr = rows // 2
            mine = outs[k].at[pl.ds(0, nl), pl.ds(c * hr, hr)]
            theirs = outs[k].at[pl.ds(0, nl), pl.ds((1 - c) * hr, hr)]
            copies.append((_remote(mine, mine, send_sems.at[k], recv_sems.at[k], (x, y, 1 - c)),
                           _remote(mine, theirs, send_sems.at[k], recv_sems.at[k], (x, y, 1 - c))))
        for send, _ in copies:
            send.start()
        for send, recv in copies:
            send.wait_send()
            recv.wait_recv()

    out_shape = [jax.ShapeDtypeStruct(f.shape, f.dtype) for f in fulls]
    scratch = [pltpu.SemaphoreType.DMA((n,)), pltpu.SemaphoreType.DMA((n,))]
    return _hbm_call(body, name, fulls, out_shape, scratch, aliases={k: k for k in range(n)})


def _tile_rows(rows, cols, budget_bytes):
    best = None
    for t in range(16, rows + 1, 16):
        if rows % t == 0 and t * cols * 4 <= budget_bytes:
            best = t
    assert best is not None, (rows, cols)
    return best


N_DEV = 8


def sum_devices(name, landed, own, layer, n_layers, prev, place):
    _, hr, cols = landed.shape
    tr = _tile_rows(hr, cols, 512 * 1024)
    n_blk = hr // tr
    own_slot = (lambda r, p: (p[0], p[1] * n_blk + r, 0)) if own.shape[0] == N_CHIPS else (lambda r, p: (0, p[1] * n_blk + r, 0))

    def body(p_ref, r_ref, own_ref, *rest):
        o_ref = rest[-1]
        mine = own_ref[...].astype(F32)
        acc = jnp.zeros_like(mine)
        for d in range(N_DEV):
            acc = acc + jnp.where(p_ref[2] == d, mine, r_ref[d].astype(F32))
        o_ref[...] = acc

    in_specs = [pl.BlockSpec((N_DEV, tr, cols), lambda r, p: (0, r, 0)), pl.BlockSpec((None, tr, cols), own_slot)]
    args = [place, landed, own]
    aliases = {}
    if prev is not None:
        in_specs.append(pl.BlockSpec(memory_space=pl.ANY))
        args.append(prev)
        aliases = {3: 0}
    return pl.pallas_call(
        body, name=name, out_shape=jax.ShapeDtypeStruct((n_layers, 2 * hr, cols), F32),
        grid_spec=pltpu.PrefetchScalarGridSpec(
            num_scalar_prefetch=1, grid=(n_blk,), in_specs=in_specs,
            out_specs=pl.BlockSpec((None, tr, cols), lambda r, p: (layer, p[1] * n_blk + r, 0))),
        input_output_aliases=aliases, compiler_params=_params(("parallel",)),
    )(*args)


def _pack(arrs, rows_multiple):
    flat = []
    for a in arrs:
        v = a.reshape(-1).astype(F32)
        flat.append(jnp.pad(v, (0, (-v.shape[0]) % HEAD)))
    v = jnp.concatenate(flat)
    v = jnp.pad(v, (0, (-v.shape[0]) % (HEAD * rows_multiple)))
    return v.reshape(-1, HEAD)


def _unpack(packed, shapes):
    flat = packed.reshape(-1)
    out, off = [], 0
    for shp in shapes:
        size = int(np.prod(shp))
        out.append(flat[off:off + size].reshape(shp))
        off += size + (-size) % HEAD
    return out


BIG = ["even_w_in", "even_w_out", "mla_w_down", "mla_w_qb", "mla_w_kvb", "mla_w_o", "mlp_w1", "mlp_w2"]
BIG_KEY = {"even_w_in": "w_in", "even_w_out": "w_out", "mla_w_down": "w_down", "mla_w_qb": "w_qb",
           "mla_w_kvb": "w_kvb", "mla_w_o": "w_o", "mlp_w1": "w1", "mlp_w2": "w2"}
SMALL_KEY = {"ln_mix_g": "ln_mix_g", "ln_mix_b": "ln_mix_b", "ln_ffn_g": "ln_ffn_g", "ln_ffn_b": "ln_ffn_b",
             "pool_w": "pool_w", "pool_scale": "pool_scale", "lru_conv_w": "conv_w", "lru_conv_b": "conv_b",
             "lru_w_a": "w_a", "lru_b_a": "b_a", "lru_w_x": "w_x", "lru_b_x": "b_x", "lru_lambda": "lam",
             "mla_q_norm_g": "gq", "mla_kv_norm_g": "gkv"}
SMALL = list(SMALL_KEY)
SMALL_SHARDED = ["lru_conv_w", "mla_q_norm_g", "mla_kv_norm_g"]
WEIGHTS = ["ln_mix_g", "ln_mix_b", "ln_ffn_g", "ln_ffn_b", "even_w_in", "pool_w", "pool_scale", "lru_conv_w",
           "lru_conv_b", "lru_w_a", "lru_b_a", "lru_w_x", "lru_b_x", "lru_lambda", "even_w_out", "mla_w_down",
           "mla_q_norm_g", "mla_kv_norm_g", "mla_w_qb", "mla_w_kvb", "mla_w_o", "mlp_w1", "mlp_w2"]


GROUPS = ["mix0", "mlp0", "mix1", "mlp1", "mix2", "mlp2", "mix3", "mlp3"]


def _group_keys(group):
    layer = int(group[3:])
    if group.startswith("mlp"):
        return [("mlp_w1", "w1", layer), ("mlp_w2", "w2", layer)]
    if layer % 2 == 0:
        return [("even_w_in", "w_in", layer // 2), ("even_w_out", "w_out", layer // 2)]
    return [("mla_w_down", "w_down", layer // 2), ("mla_w_qb", "w_qb", layer // 2),
            ("mla_w_kvb", "w_kvb", layer // 2), ("mla_w_o", "w_o", layer // 2)]


def _pad_q_heads(w):
    lead = w.shape[:-1]
    w = w.reshape(lead + (2, QK_NOPE + QK_ROPE))
    w = jnp.pad(w, ((0, 0),) * len(lead) + ((0, 0), (0, QHEAD_PAD - QK_NOPE - QK_ROPE)))
    return w.reshape(lead + (2 * QHEAD_PAD,))


def _unpad_q_heads(g):
    lead = g.shape[:-1]
    return g.reshape(lead + (2, QHEAD_PAD))[..., :QK_NOPE + QK_ROPE].reshape(lead + (2 * (QK_NOPE + QK_ROPE),))


def _step(x, positions, loss_target, w, m, v):
    cx, cy, cc = lax.axis_index("x"), lax.axis_index("y"), lax.axis_index("c")
    chip = 2 * cx + cy
    place = jnp.stack([chip, cc, 2 * chip + cc]).astype(jnp.int32)

    prepared = dict(w)
    prepared["mla_w_down"] = jnp.pad(w["mla_w_down"], ((0, 0), (0, 0), (0, ODD_IN_PAD - ODD_IN)))
    prepared["mla_w_qb"] = _pad_q_heads(w["mla_w_qb"])
    small_shard_shapes = [w[k].shape for k in SMALL_SHARDED]
    sources = {g: [prepared[name][idx].astype(BF16) for name, _, idx in _group_keys(g)] for g in GROUPS}
    sources[GROUPS[0]].append(_pack([w[k] for k in SMALL_SHARDED], 32))
    gathering, token = {}, None
    for name, part in (("first", GROUPS[:1]), ("rest", GROUPS[1:])):
        zones = place_own("own_" + name, [a for g in part for a in sources[g]], place)
        groups, off = [], 0
        for g in part:
            groups.append((sources[g], zones[off:off + len(sources[g])]))
            off += len(sources[g])
        started, token = split_start("gather_" + name, groups, 3, gather_plan, token)
        gathering.update(zip(part, started))
    relayed, relay_token = {}, {}

    def prefetch(g, after):
        relayed[g], relay_token[g] = split_relay("relay_" + g, gathering[g], 3, relay_plan, after)
        return relay_token[g]

    prefetch(GROUPS[0], token)

    def weights_of(g, after):
        _, lands = split_wait("gathered_" + g, [relayed[g]], relay_token[g] if g == GROUPS[0] else after)[0]
        if g.startswith("mlp") and g != GROUPS[-1]:
            prefetch(GROUPS[GROUPS.index(g) + 1], lands[0])
        out = {key: land for (_, key, _), land in zip(_group_keys(g), lands)}
        if g == GROUPS[0]:
            per_chip = [_unpack(lands[-1][s], small_shard_shapes) for s in range(N_CHIPS)]
            for i, key in enumerate(("conv_w", "gq", "gkv")):
                out[key] = jnp.concatenate([p[i] for p in per_chip], axis=-1)
        return out

    reducing = []

    def reduce_start(g, srcs, token_in=None):
        lands = [lax.empty((N_DEV, s.shape[1] // 2, s.shape[2]), s.dtype) for s in srcs]
        started, token = split_start("reduce_" + g, [(srcs, lands)], 7, reduce_plan, token_in)
        reducing.append((g, started[0]))
        return token

    def emit_grads(g, grads):
        return reduce_start(g, [grads[key] for _, key, _ in _group_keys(g)])

    sm = {SMALL_KEY[k]: w[k] for k in SMALL if k not in SMALL_SHARDED}
    loss, grad_x, gs = _local_step(x[0], positions.reshape(S, 1), loss_target[0], sm, weights_of, emit_grads, prefetch)
    loss = lax.psum(loss, ("x", "y", "c"))

    small_shapes = [gs[SMALL_KEY[k]].shape for k in SMALL]
    gs_pack = _pack([gs[SMALL_KEY[k]] for k in SMALL], 32)[None]
    small_sent = reduce_start("small", [gs_pack], grad_x)

    grad, delta, new_m, new_v = {}, {}, {}, {}
    late_groups = ("mix0", "small")
    stacks, after = {}, small_sent
    for late in (False, True):
        part = [(g, st) for g, st in reducing if (g in late_groups) == late]
        landed = split_wait("reduced_late" if late else "reduced_early", [st for _, st in part], after)
        for (g, _), (owns, lands) in zip(part, landed):
            if g == "small":
                stacks["small"] = sum_devices("sum_small", lands[0], owns[0], 0, 1, None, place)
                continue
            for (name, key, idx), own, land in zip(_group_keys(g), owns, lands):
                stacks[name] = sum_devices("sum_%s%d" % (key, idx), land, own, idx, w[name].shape[0], stacks.get(name), place)
        names = [k for k in BIG if (k in ("even_w_in", "even_w_out")) == late] + (["small"] if late else [])
        reduced = dict(zip(names, sibling_swap_halves("swap_late" if late else "swap_early", [stacks[k] for k in names])))
        if not late:
            reduced["mla_w_down"] = reduced["mla_w_down"][..., :ODD_IN]
            reduced["mla_w_qb"] = _unpad_q_heads(reduced["mla_w_qb"])
        for k in names:
            if k == "small":
                continue
            grad[k] = reduced[k]
            shp = w[k].shape
            view = lambda a: a.reshape(-1, shp[-1])
            d, nm, nv = adamw("adamw_" + BIG_KEY[k], view(w[k]), view(grad[k]), view(m[k]), view(v[k]))
            delta[k], new_m[k], new_v[k] = d.reshape(shp), nm.reshape(shp), nv.reshape(shp)
            after = d
    g_small = dict(zip(SMALL, _unpack(reduced["small"], small_shapes)))
    for k in SMALL_SHARDED:
        width = w[k].shape[-1]
        g_small[k] = lax.dynamic_slice_in_dim(g_small[k], chip * width, width, axis=-1)
    grad.update(g_small)

    shapes = [w[k].shape for k in SMALL]
    d, nm, nv = adamw("adamw_small", *[_pack([t[k] for k in SMALL], 512) for t in (w, grad, m, v)])
    for k, dk, mk, vk in zip(SMALL, _unpack(d, shapes), _unpack(nm, shapes), _unpack(nv, shapes)):
        delta[k], new_m[k], new_v[k] = dk, mk, vk
    return (loss, grad_x[None], *[grad[k] for k in WEIGHTS], *[delta[k] for k in WEIGHTS],
            *[new_m[k] for k in WEIGHTS], *[new_v[k] for k in WEIGHTS])


def kernel(x, positions, ln_mix_g, ln_mix_b, ln_ffn_g, ln_ffn_b, even_w_in, pool_w, pool_scale, lru_conv_w, lru_conv_b, lru_w_a, lru_b_a, lru_w_x, lru_b_x, lru_lambda, even_w_out, mla_w_down, mla_q_norm_g, mla_kv_norm_g, mla_w_qb, mla_w_kvb, mla_w_o, mlp_w1, mlp_w2, loss_target, m_ln_mix_g, m_ln_mix_b, m_ln_ffn_g, m_ln_ffn_b, m_even_w_in, m_pool_w, m_pool_scale, m_lru_conv_w, m_lru_conv_b, m_lru_w_a, m_lru_b_a, m_lru_w_x, m_lru_b_x, m_lru_lambda, m_even_w_out, m_mla_w_down, m_mla_q_norm_g, m_mla_kv_norm_g, m_mla_w_qb, m_mla_w_kvb, m_mla_w_o, m_mlp_w1, m_mlp_w2, v_ln_mix_g, v_ln_mix_b, v_ln_ffn_g, v_ln_ffn_b, v_even_w_in, v_pool_w, v_pool_scale, v_lru_conv_w, v_lru_conv_b, v_lru_w_a, v_lru_b_a, v_lru_w_x, v_lru_b_x, v_lru_lambda, v_even_w_out, v_mla_w_down, v_mla_q_norm_g, v_mla_kv_norm_g, v_mla_w_qb, v_mla_w_kvb, v_mla_w_o, v_mlp_w1, v_mlp_w2):
    args = locals()
    w = {k: args[k] for k in WEIGHTS}
    m = {k: args["m_" + k] for k in WEIGHTS}
    v = {k: args["v_" + k] for k in WEIGHTS}
    return _step(x, positions, loss_target, w, m, v)
```

```python
import functools
import math

import jax
import jax.numpy as jnp
import numpy as np
from jax import lax
from jax.experimental import pallas as pl
from jax.experimental.pallas import tpu as pltpu

F32 = jnp.float32
BF16 = jnp.bfloat16

S = 2048
D = 1024
DEPTH = 4
N_CHIPS = 4
POOL_WINDOWS = (2, 4, 8, 16)
POOL_W = 512
LRU_W = 1024
LRU_HEADS = 8
HEAD = 128
EVEN_IN = 2560
EVEN_MIX = 1536
MLA_HEADS = 8
QK_NOPE = 128
QK_ROPE = 64
Q_RANK = 384
KV_RANK = 256
ODD_IN = 704
ODD_IN_PAD = 768
QHEAD_PAD = 256
D_FF = 4096
CHUNK = 64
ALPHA = (2 * DEPTH) ** 0.25
LN_EPS = 1e-5
RMS_EPS = 1e-6
ATT_SCALE = (QK_NOPE + QK_ROPE) ** -0.5
ROPE_THETA = 10000.0
LRU_C = 8.0
ADAM_LR = 0.001
ADAM_B1 = 0.9
ADAM_B2 = 0.999
ADAM_EPS = 1e-08
ADAM_WD = 0.01
ADAM_STEP = 10

VMEM_LIMIT = 56 * 1024 * 1024
MESH = pl.DeviceIdType.MESH

_NN = (((1,), (0,)), ((), ()))
_NT = (((1,), (1,)), ((), ()))
_TN = (((0,), (0,)), ((), ()))


def _params(sem=None, **kw):
    return pltpu.CompilerParams(dimension_semantics=sem, vmem_limit_bytes=VMEM_LIMIT, **kw)


def _dot(a, b, dims):
    return lax.dot_general(a.astype(BF16), b.astype(BF16), dims, preferred_element_type=F32)


def _whole(w4):
    return pl.BlockSpec(tuple(w4.shape), lambda i: (0, 0, 0))


def mm_cols(name, a, w4, out_dtype, *, tm=512):
    m, k = a.shape
    nb = w4.shape[2]

    def body(a_ref, w_ref, o_ref):
        a_v = a_ref[...].astype(BF16)
        for s in range(N_CHIPS):
            o_ref[:, s * nb:(s + 1) * nb] = _dot(a_v, w_ref[s], _NN).astype(o_ref.dtype)

    return pl.pallas_call(
        body, name=name, grid=(m // tm,), in_specs=[pl.BlockSpec((tm, k), lambda i: (i, 0)), _whole(w4)],
        out_specs=pl.BlockSpec((tm, N_CHIPS * nb), lambda i: (i, 0)),
        out_shape=jax.ShapeDtypeStruct((m, N_CHIPS * nb), out_dtype), compiler_params=_params(("parallel",)),
    )(a, w4)


def mm_rows(name, a, w4, out_dtype, *, tm=512, ln=None, after=None):
    m = a.shape[0]
    kb, n = w4.shape[1:]
    n_after = 0 if after is None else 1

    def body(a_ref, w_ref, *rest):
        rest = rest[n_after:]
        a_v = a_ref[...].astype(BF16)
        acc = _dot(a_v[:, :kb], w_ref[0], _NN)
        for s in range(1, N_CHIPS):
            acc = acc + _dot(a_v[:, s * kb:(s + 1) * kb], w_ref[s], _NN)
        if ln is None:
            rest[0][...] = acc.astype(rest[0].dtype)
        else:
            x_ref, g_ref, b_ref, y_ref, xhat_ref, rstd_ref = rest
            y_ref[...], xhat_ref[...], rstd_ref[...] = _layer_norm(ALPHA * x_ref[...] + acc, g_ref[...], b_ref[...])

    row = pl.BlockSpec((tm, n), lambda i: (i, 0))
    vec = pl.BlockSpec((1, n), lambda i: (0, 0))
    in_specs = [pl.BlockSpec((tm, N_CHIPS * kb), lambda i: (i, 0)), _whole(w4)] + [pl.BlockSpec(memory_space=pl.ANY)] * n_after
    if ln is None:
        extras, out_specs, out_shape = (), row, jax.ShapeDtypeStruct((m, n), out_dtype)
    else:
        extras, in_specs = ln, in_specs + [row, vec, vec]
        out_specs = [row, row, pl.BlockSpec((tm, 1), lambda i: (i, 0))]
        out_shape = [jax.ShapeDtypeStruct((m, n), F32), jax.ShapeDtypeStruct((m, n), F32), jax.ShapeDtypeStruct((m, 1), F32)]
    return pl.pallas_call(
        body, name=name, grid=(m // tm,), in_specs=in_specs, out_specs=out_specs, out_shape=out_shape,
        compiler_params=_params(("parallel",)),
    )(a, w4, *([after] * n_after), *extras)


def linear_bwd(name, a, g, w4, *, col_sharded, da_dtype, resid=None, tm=512, after=None):
    m = a.shape[0]
    n_red = m // tm
    ka, ng = w4.shape[1:]
    k_all = a.shape[1]
    n_after = 0 if after is None else 1
    n_resid = 0 if resid is None else 1

    def body(a_ref, g_ref, w_ref, *rest):
        rest = rest[n_after:]
        da_ref, dw_ref, acc_ref = rest[n_resid:]
        k = pl.program_id(0)

        @pl.when(k == 0)
        def _():
            acc_ref[...] = jnp.zeros_like(acc_ref)

        a_v, g_v = a_ref[...].astype(BF16), g_ref[...].astype(BF16)
        extra = ALPHA * rest[0][...] if n_resid else None
        if col_sharded:
            da = extra
            for s in range(N_CHIPS):
                g_s = g_v[:, s * ng:(s + 1) * ng]
                p = _dot(g_s, w_ref[s], _NT)
                da = p if da is None else da + p
                acc_ref[s] += _dot(a_v, g_s, _TN)
            da_ref[...] = da.astype(da_ref.dtype)
        else:
            for s in range(N_CHIPS):
                cols = slice(s * ka, (s + 1) * ka)
                p = _dot(g_v, w_ref[s], _NT)
                da_ref[:, cols] = (p if extra is None else p + extra[:, cols]).astype(da_ref.dtype)
                acc_ref[s] += _dot(a_v[:, cols], g_v, _TN)

        @pl.when(k == n_red - 1)
        def _():
            dw_ref[...] = acc_ref[...].astype(dw_ref.dtype)

    row = lambda width: pl.BlockSpec((tm, width), lambda i: (i, 0))
    whole = pl.BlockSpec((N_CHIPS, ka, ng), lambda i: (0, 0, 0))
    return pl.pallas_call(
        body, name=name, grid=(n_red,),
        in_specs=[row(k_all), row(g.shape[1]), whole] + [pl.BlockSpec(memory_space=pl.ANY)] * n_after
        + [row(k_all)] * n_resid,
        out_specs=[row(k_all), whole],
        out_shape=[jax.ShapeDtypeStruct((m, k_all), da_dtype), jax.ShapeDtypeStruct((N_CHIPS, ka, ng), BF16)],
        scratch_shapes=[pltpu.VMEM((N_CHIPS, ka, ng), F32)], compiler_params=_params(("arbitrary",)),
    )(a, g, w4, *([after] * n_after), *([resid] * n_resid))


def _layer_norm(z, g, b):
    mu = jnp.mean(z, axis=-1, keepdims=True)
    zc = z - mu
    rstd = lax.rsqrt(jnp.mean(zc * zc, axis=-1, keepdims=True) + LN_EPS)
    xhat = zc * rstd
    return xhat * g + b, xhat, rstd


def mlp_fwd(name, x, w1, w2, g, b, *, tm=512):
    fb = w1.shape[2]

    def body(x_ref, w1_ref, w2_ref, g_ref, b_ref, y_ref, xhat_ref, rstd_ref, relu_ref, acc_ref):
        s = pl.program_id(1)

        @pl.when(s == 0)
        def _():
            acc_ref[...] = jnp.zeros_like(acc_ref)

        r = jnp.maximum(_dot(x_ref[...], w1_ref[...], _NN), 0.0)
        relu_ref[...] = r.astype(relu_ref.dtype)
        acc_ref[...] += _dot(r * r, w2_ref[...], _NN)

        @pl.when(s == N_CHIPS - 1)
        def _():
            y_ref[...], xhat_ref[...], rstd_ref[...] = _layer_norm(ALPHA * x_ref[...] + acc_ref[...], g_ref[...], b_ref[...])

    row = pl.BlockSpec((tm, D), lambda i, s: (i, 0))
    vec = pl.BlockSpec((1, D), lambda i, s: (0, 0))
    return pl.pallas_call(
        body, name=name, grid=(S // tm, N_CHIPS),
        in_specs=[row, pl.BlockSpec((None, D, fb), lambda i, s: (s, 0, 0)),
                  pl.BlockSpec((None, fb, D), lambda i, s: (s, 0, 0)), vec, vec],
        out_specs=[row, row, pl.BlockSpec((tm, 1), lambda i, s: (i, 0)), pl.BlockSpec((tm, fb), lambda i, s: (i, s))],
        out_shape=[jax.ShapeDtypeStruct((S, D), F32), jax.ShapeDtypeStruct((S, D), F32),
                   jax.ShapeDtypeStruct((S, 1), F32), jax.ShapeDtypeStruct((S, N_CHIPS * fb), BF16)],
        scratch_shapes=[pltpu.VMEM((tm, D), F32)], compiler_params=_params(("parallel", "arbitrary")),
    )(x, w1, w2, g, b)


def mlp_bwd(name, dz, relu, x, w1, w2, *, tm=256):
    fb = w1.shape[2]
    n_i = S // tm

    def body(dz_ref, relu_ref, x_ref, w1_ref, w2_ref, dx_ref, g1_ref, g2_ref, acc1_ref, acc2_ref):
        s, i = pl.program_id(0), pl.program_id(1)
        rows = pl.ds(pl.multiple_of(i * tm, tm), tm)
        dz_v = dz_ref[...]

        @pl.when(i == 0)
        def _():
            acc1_ref[...] = jnp.zeros_like(acc1_ref)
            acc2_ref[...] = jnp.zeros_like(acc2_ref)

        @pl.when(s == 0)
        def _():
            dx_ref[rows, :] = ALPHA * dz_v

        dz_b = dz_v.astype(BF16)
        r = relu_ref[...]
        dh = (_dot(dz_b, w2_ref[...], _NT) * (2.0 * r.astype(F32))).astype(BF16)
        p2 = _dot(r * r, dz_b, _TN)
        p1 = _dot(x_ref[...], dh, _TN)
        dx_ref[rows, :] += _dot(dh, w1_ref[...], _NT)
        acc1_ref[...] += p1
        acc2_ref[...] += p2

        @pl.when(i == n_i - 1)
        def _():
            g1_ref[...] = acc1_ref[...].astype(g1_ref.dtype)
            g2_ref[...] = acc2_ref[...].astype(g2_ref.dtype)

    row = pl.BlockSpec((tm, D), lambda s, i: (i, 0))
    return pl.pallas_call(
        body, name=name, grid=(N_CHIPS, n_i),
        in_specs=[row, pl.BlockSpec((tm, fb), lambda s, i: (i, s)), row,
                  pl.BlockSpec((None, D, fb), lambda s, i: (s, 0, 0)), pl.BlockSpec((None, fb, D), lambda s, i: (s, 0, 0))],
        out_specs=[pl.BlockSpec((S, D), lambda s, i: (0, 0)), pl.BlockSpec((None, D, fb), lambda s, i: (s, 0, 0)),
                   pl.BlockSpec((None, fb, D), lambda s, i: (s, 0, 0))],
        out_shape=[jax.ShapeDtypeStruct((S, D), F32), jax.ShapeDtypeStruct((N_CHIPS, D, fb), BF16),
                   jax.ShapeDtypeStruct((N_CHIPS, fb, D), BF16)],
        scratch_shapes=[pltpu.VMEM((D, fb), F32), pltpu.VMEM((fb, D), F32)],
        compiler_params=_params(("arbitrary", "arbitrary")),
    )(dz, relu, x, w1, w2)


def ln_bwd(name, dy, xhat, rstd, g, *, tm=256, after=None):
    n_after = 0 if after is None else 1

    def body(dy_ref, xhat_ref, rstd_ref, g_ref, *rest):
        dz_ref, dg_ref, db_ref = rest[n_after:]
        dy_v = dy_ref[...]
        xh = xhat_ref[...]
        dxh = dy_v * g_ref[...]
        m1 = jnp.mean(dxh, axis=-1, keepdims=True)
        m2 = jnp.mean(dxh * xh, axis=-1, keepdims=True)
        dz_ref[...] = rstd_ref[...] * (dxh - m1 - xh * m2)
        pg = jnp.sum(dy_v * xh, axis=0, keepdims=True)
        pb = jnp.sum(dy_v, axis=0, keepdims=True)
        i = pl.program_id(0)

        @pl.when(i == 0)
        def _():
            dg_ref[...] = pg
            db_ref[...] = pb

        @pl.when(i > 0)
        def _():
            dg_ref[...] += pg
            db_ref[...] += pb

    row = pl.BlockSpec((tm, D), lambda i: (i, 0))
    vec = pl.BlockSpec((1, D), lambda i: (0, 0))
    return pl.pallas_call(
        body, name=name, grid=(S // tm,),
        in_specs=[row, row, pl.BlockSpec((tm, 1), lambda i: (i, 0)), vec] + [pl.BlockSpec(memory_space=pl.ANY)] * n_after,
        out_specs=[row, vec, vec],
        out_shape=[jax.ShapeDtypeStruct((S, D), F32), jax.ShapeDtypeStruct((1, D), F32),
                   jax.ShapeDtypeStruct((1, D), F32)],
        compiler_params=_params(("arbitrary",)),
    )(dy, xhat, rstd, g, *([after] * n_after))


def loss_head(y, target, *, tm=256):
    def body(y_ref, t_ref, dy_ref, loss_ref):
        e = y_ref[...] - t_ref[...]
        dy_ref[...] = e * (1.0 / D)
        part = jnp.sum(jnp.sum(e * e, axis=-1, keepdims=True), axis=0, keepdims=True) * (0.5 / D)
        i = pl.program_id(0)

        @pl.when(i == 0)
        def _():
            loss_ref[...] = jnp.zeros_like(loss_ref)

        loss_ref[...] += jnp.broadcast_to(part, loss_ref.shape)

    row = pl.BlockSpec((tm, D), lambda i: (i, 0))
    return pl.pallas_call(
        body, name="loss_head", grid=(S // tm,), in_specs=[row, row],
        out_specs=[row, pl.BlockSpec((8, 128), lambda i: (0, 0))],
        out_shape=[jax.ShapeDtypeStruct((S, D), F32), jax.ShapeDtypeStruct((8, 128), F32)],
        compiler_params=_params(("arbitrary",)),
    )(y, target)


def _rows(shape):
    return lax.broadcasted_iota(jnp.int32, shape, 0)


def _shift_down(x, k):
    return jnp.where(_rows(x.shape) >= k, pltpu.roll(x, k, 0), 0.0)


def _shift_up(x, k):
    n = x.shape[0]
    return jnp.where(_rows(x.shape) < n - k, pltpu.roll(x, n - k, 0), 0.0)


def _pool_diff(u, w):
    acc, k = u, 1
    while k < w:
        acc = acc + _shift_down(acc, k)
        k *= 2
    cnt = jnp.minimum(_rows(u.shape) + 1, w).astype(F32)
    return acc / cnt - u, cnt


def pool_fwd(name, proj, pool_w, pool_scale):
    def body(u_ref, w_ref, sc_ref, y_ref):
        for g, w in enumerate(POOL_WINDOWS):
            cols = slice(g * HEAD, (g + 1) * HEAD)
            d, _ = _pool_diff(u_ref[:, cols], w)
            z = _dot(d, w_ref[g], _NN)
            y_ref[:, cols] = (z * sc_ref[:, cols]).astype(y_ref.dtype)

    return pl.pallas_call(
        body, name=name, grid=(1,),
        in_specs=[pl.BlockSpec((S, POOL_W), lambda i: (0, 0)),
                  pl.BlockSpec((4, HEAD, HEAD), lambda i: (0, 0, 0)),
                  pl.BlockSpec((1, POOL_W), lambda i: (0, 0))],
        out_specs=pl.BlockSpec((S, POOL_W), lambda i: (0, 0)),
        out_shape=jax.ShapeDtypeStruct((S, POOL_W), BF16),
        compiler_params=_params(("arbitrary",)),
    )(proj, pool_w, pool_scale)


def pool_bwd(name, proj, dycat, pool_w, pool_scale):
    def body(u_ref, dy_ref, w_ref, sc_ref, du_ref, dw_ref, dsc_ref):
        for g, w in enumerate(POOL_WINDOWS):
            cols = slice(g * HEAD, (g + 1) * HEAD)
            d, cnt = _pool_diff(u_ref[:, cols], w)
            dy = dy_ref[:, cols]
            z = _dot(d, w_ref[g], _NN)
            dsc_ref[:, cols] = jnp.sum(dy * z, axis=0, keepdims=True)
            dz = dy * sc_ref[:, cols]
            dw_ref[g] = _dot(d, dz, _TN)
            dd = _dot(dz, w_ref[g], _NT)
            acc, k = dd / cnt, 1
            while k < w:
                acc = acc + _shift_up(acc, k)
                k *= 2
            du_ref[:, cols] = (acc - dd).astype(du_ref.dtype)

    return pl.pallas_call(
        body, name=name, grid=(1,),
        in_specs=[pl.BlockSpec((S, POOL_W), lambda i: (0, 0)),
                  pl.BlockSpec((S, POOL_W), lambda i: (0, 0)),
                  pl.BlockSpec((4, HEAD, HEAD), lambda i: (0, 0, 0)),
                  pl.BlockSpec((1, POOL_W), lambda i: (0, 0))],
        out_specs=[pl.BlockSpec((S, POOL_W), lambda i: (0, 0)),
                   pl.BlockSpec((4, HEAD, HEAD), lambda i: (0, 0, 0)),
                   pl.BlockSpec((1, POOL_W), lambda i: (0, 0))],
        out_shape=[jax.ShapeDtypeStruct((S, POOL_W), BF16), jax.ShapeDtypeStruct((4, HEAD, HEAD), F32),
                   jax.ShapeDtypeStruct((1, POOL_W), F32)],
        compiler_params=_params(("arbitrary",)),
    )(proj, dycat, pool_w, pool_scale)


def _expm1(x):
    series = x * (1.0 + x * (0.5 + x * (1.0 / 6.0 + x * (1.0 / 24.0 + x * (1.0 / 120.0)))))
    return jnp.where(jnp.abs(x) < 0.05, series, jnp.exp(x) - 1.0)


def _softplus_neg(lam):
    e = jnp.exp(-jnp.abs(lam))
    log1p = jnp.where(e < 0.01, e * (1.0 - e * (0.5 - e * (1.0 / 3.0))), jnp.log(1.0 + e))
    return jnp.maximum(-lam, 0.0) + log1p


_GELU_C = math.sqrt(2.0 / math.pi)


def _gelu(x):
    t = jnp.tanh(_GELU_C * (x + 0.044715 * x * x * x))
    return 0.5 * x * (1.0 + t), t


def _gelu_grad(x, t):
    return 0.5 * (1.0 + t) + 0.5 * x * (1.0 - t * t) * _GELU_C * (1.0 + 3.0 * 0.044715 * x * x)


def _conv(u, cw, cb):
    return cw[3:4] * u + cw[2:3] * _shift_down(u, 1) + cw[1:2] * _shift_down(u, 2) + cw[0:1] * _shift_down(u, 3) + cb


def _lru_gates(cu, wa, ba, wx, bx, lam):
    r = jax.nn.sigmoid(_dot(cu, wa, _NN) + ba)
    i = jax.nn.sigmoid(_dot(cu, wx, _NN) + bx)
    sp = _softplus_neg(lam)
    log_a = (-LRU_C) * r * sp
    a = jnp.exp(log_a)
    mult = jnp.sqrt(-_expm1(2.0 * log_a))
    return r, i, sp, a, mult


def _scan(a_ref, b_ref, h_ref, *, reverse):
    n_blk = S // 8
    row8 = lax.broadcasted_iota(jnp.int32, (8, HEAD), 0)

    def step(j, carry):
        blk = (n_blk - 1 - j) if reverse else j
        r0 = pl.multiple_of(blk * 8, 8)
        a = a_ref[pl.ds(r0, 8), :]
        b = b_ref[pl.ds(r0, 8), :]
        for k in (1, 2, 4):
            if reverse:
                keep = row8 < 8 - k
                a_s, b_s = pltpu.roll(a, 8 - k, 0), pltpu.roll(b, 8 - k, 0)
            else:
                keep = row8 >= k
                a_s, b_s = pltpu.roll(a, k, 0), pltpu.roll(b, k, 0)
            b = jnp.where(keep, a * b_s + b, b)
            a = jnp.where(keep, a * a_s, a)
        h = b + a * carry
        h_ref[pl.ds(r0, 8), :] = h
        edge = h[0:1, :] if reverse else h[7:8, :]
        return jnp.broadcast_to(edge, (8, HEAD))

    lax.fori_loop(0, n_blk, step, jnp.zeros((8, HEAD), F32), unroll=4)


def _lru_specs():
    def col(off):
        return pl.BlockSpec((S, HEAD), lambda h: (0, off + h))
    vec = pl.BlockSpec((1, HEAD), lambda h: (0, h))
    mat = pl.BlockSpec((None, HEAD, HEAD), lambda h: (h, 0, 0))
    cw = pl.BlockSpec((4, HEAD), lambda h: (0, h))
    return col, vec, mat, cw


def lru_fwd(name, proj, conv_w, conv_b, w_a, b_a, w_x, b_x, lam):
    def body(u_ref, ug_ref, cw_ref, cb_ref, wa_ref, ba_ref, wx_ref, bx_ref, lam_ref, y_ref, h_ref, a_s, b_s):
        cu = _conv(u_ref[...], cw_ref[...], cb_ref[...])
        _, i, _, a, mult = _lru_gates(cu, wa_ref[...], ba_ref[...], wx_ref[...], bx_ref[...], lam_ref[...])
        a_s[...] = a
        b_s[...] = mult * (i * cu)
        _scan(a_s, b_s, h_ref, reverse=False)
        gl, _ = _gelu(ug_ref[...])
        y_ref[...] = (h_ref[...] * gl).astype(y_ref.dtype)

    col, vec, mat, cw = _lru_specs()
    out = pl.BlockSpec((S, HEAD), lambda h: (0, h))
    return pl.pallas_call(
        body, name=name, grid=(LRU_HEADS,),
        in_specs=[col(4), col(12), cw, vec, mat, vec, mat, vec, vec],
        out_specs=[out, out],
        out_shape=[jax.ShapeDtypeStruct((S, LRU_W), BF16), jax.ShapeDtypeStruct((S, LRU_W), F32)],
        scratch_shapes=[pltpu.VMEM((S, HEAD), F32), pltpu.VMEM((S, HEAD), F32)],
        compiler_params=_params(("parallel",)),
    )(proj, proj, conv_w, conv_b, w_a, b_a, w_x, b_x, lam)


def lru_bwd(name, proj, hstate, dycat, conv_w, conv_b, w_a, b_a, w_x, b_x, lam):
    def body(u_ref, ug_ref, h_ref, dy_ref, cw_ref, cb_ref, wa_ref, ba_ref, wx_ref, bx_ref, lam_ref,
             du_ref, dug_ref, dwa_ref, dwx_ref, dba_ref, dbx_ref, dlam_ref, dcw_ref, dcb_ref, a_s, b_s, g_s):
        u = u_ref[...]
        cw = cw_ref[...]
        cu = _conv(u, cw, cb_ref[...])
        lam_v = lam_ref[...]
        r, i, sp, a, mult = _lru_gates(cu, wa_ref[...], ba_ref[...], wx_ref[...], bx_ref[...], lam_v)
        ug = ug_ref[...]
        gl, t = _gelu(ug)
        dy = dy_ref[...]
        h = h_ref[...]
        dug_ref[...] = (dy * h * _gelu_grad(ug, t)).astype(dug_ref.dtype)
        a_s[...] = _shift_up(a, 1)
        b_s[...] = dy * gl
        _scan(a_s, b_s, g_s, reverse=True)
        dxin = g_s[...]
        da = dxin * _shift_down(h, 1)
        dmult = dxin * (i * cu)
        di = dxin * (mult * cu)
        dlog_a = da * a - dmult * (a * a) / mult
        dr_pre = dlog_a * ((-LRU_C) * sp) * (r * (1.0 - r))
        di_pre = di * (i * (1.0 - i))
        dsp = jnp.sum(dlog_a * ((-LRU_C) * r), axis=0, keepdims=True)
        dlam_ref[...] = dsp * (-jax.nn.sigmoid(-lam_v))
        dba_ref[...] = jnp.sum(dr_pre, axis=0, keepdims=True)
        dbx_ref[...] = jnp.sum(di_pre, axis=0, keepdims=True)
        dwa_ref[...] = _dot(cu, dr_pre, _TN)
        dwx_ref[...] = _dot(cu, di_pre, _TN)
        dcu = dxin * (mult * i) + _dot(dr_pre, wa_ref[...], _NT) + _dot(di_pre, wx_ref[...], _NT)
        dcb_ref[...] = jnp.sum(dcu, axis=0, keepdims=True)
        for k in range(4):
            dcw_ref[k:k + 1, :] = jnp.sum(dcu * (_shift_down(u, 3 - k) if k < 3 else u), axis=0, keepdims=True)
        du = cw[3:4] * dcu + cw[2:3] * _shift_up(dcu, 1) + cw[1:2] * _shift_up(dcu, 2) + cw[0:1] * _shift_up(dcu, 3)
        du_ref[...] = du.astype(du_ref.dtype)

    col, vec, mat, cw = _lru_specs()
    out = pl.BlockSpec((S, HEAD), lambda h: (0, h))
    big = jax.ShapeDtypeStruct((S, LRU_W), BF16)
    vec_shape = jax.ShapeDtypeStruct((1, LRU_W), F32)
    mat_shape = jax.ShapeDtypeStruct((LRU_HEADS, HEAD, HEAD), F32)
    return pl.pallas_call(
        body, name=name, grid=(LRU_HEADS,),
        in_specs=[col(4), col(12), out, col(4), cw, vec, mat, vec, mat, vec, vec],
        out_specs=[out, out, mat, mat, vec, vec, vec, cw, vec],
        out_shape=[big, big, mat_shape, mat_shape, vec_shape, vec_shape, vec_shape,
                   jax.ShapeDtypeStruct((4, LRU_W), F32), vec_shape],
        scratch_shapes=[pltpu.VMEM((S, HEAD), F32)] * 3,
        compiler_params=_params(("parallel",)),
    )(proj, proj, hstate, dycat, conv_w, conv_b, w_a, b_a, w_x, b_x, lam)


def rope_tables(pos_col, inv_freq):
    def body(pos_ref, f_ref, c_ref, s1_ref, s2_ref):
        ang = pos_ref[...].astype(F32) * f_ref[...]
        lane = lax.broadcasted_iota(jnp.int32, ang.shape, 1)
        cos, sin = jnp.cos(ang), jnp.sin(ang)
        c_ref[...] = jnp.where(lane < QK_ROPE, cos, 0.0)
        s1_ref[...] = jnp.where(lane < QK_ROPE // 2, -sin, 0.0)
        s2_ref[...] = jnp.where((lane >= QK_ROPE // 2) & (lane < QK_ROPE), sin, 0.0)

    tab = jax.ShapeDtypeStruct((S, HEAD), F32)
    return pl.pallas_call(
        body, name="rope_tables", grid=(1,),
        in_specs=[pl.BlockSpec((S, 1), lambda i: (0, 0)), pl.BlockSpec((1, HEAD), lambda i: (0, 0))],
        out_specs=[pl.BlockSpec((S, HEAD), lambda i: (0, 0))] * 3, out_shape=[tab, tab, tab],
        compiler_params=_params(("arbitrary",)),
    )(pos_col, inv_freq)


def _rope(v, c, s1, s2):
    return v * c + pltpu.roll(v, HEAD - QK_ROPE // 2, 1) * s1 + pltpu.roll(v, QK_ROPE // 2, 1) * s2


def _unrope(d, c, s1, s2):
    return d * c + pltpu.roll(d * s1, QK_ROPE // 2, 1) + pltpu.roll(d * s2, HEAD - QK_ROPE // 2, 1)


def _rms(x, g):
    rstd = lax.rsqrt(jnp.mean(x * x, axis=-1, keepdims=True) + RMS_EPS)
    return x * rstd, rstd


def mla_prep(name, down, gq, gkv, tabs, *, tm=256):
    def body(dn_ref, gq_ref, gkv_ref, c_ref, s1_ref, s2_ref, cq_ref, ckv_ref, kp_ref):
        xq, _ = _rms(dn_ref[:, :Q_RANK], None)
        cq_ref[...] = (xq * gq_ref[...]).astype(cq_ref.dtype)
        xkv, _ = _rms(dn_ref[:, Q_RANK:Q_RANK + KV_RANK], None)
        ckv_ref[...] = (xkv * gkv_ref[...]).astype(ckv_ref.dtype)
        kp = _rope(dn_ref[:, Q_RANK + KV_RANK:], c_ref[...], s1_ref[...], s2_ref[...])
        kp_ref[...] = kp.astype(kp_ref.dtype)

    tab = pl.BlockSpec((tm, HEAD), lambda i: (i, 0))
    return pl.pallas_call(
        body, name=name, grid=(S // tm,),
        in_specs=[pl.BlockSpec((tm, ODD_IN_PAD), lambda i: (i, 0)), pl.BlockSpec((1, Q_RANK), lambda i: (0, 0)),
                  pl.BlockSpec((1, KV_RANK), lambda i: (0, 0)), tab, tab, tab],
        out_specs=[pl.BlockSpec((tm, Q_RANK), lambda i: (i, 0)), pl.BlockSpec((tm, KV_RANK), lambda i: (i, 0)), tab],
        out_shape=[jax.ShapeDtypeStruct((S, Q_RANK), BF16), jax.ShapeDtypeStruct((S, KV_RANK), BF16),
                   jax.ShapeDtypeStruct((S, HEAD), BF16)],
        compiler_params=_params(("parallel",)),
    )(down, gq, gkv, *tabs)


def mla_prep_bwd(name, down, dcq, dckv, dkp, gq, gkv, tabs, *, tm=256):
    def body(dn_ref, dcq_ref, dckv_ref, dkp_ref, gq_ref, gkv_ref, c_ref, s1_ref, s2_ref, dd_ref, dgq_ref, dgkv_ref):
        i = pl.program_id(0)

        def rms_bwd(x, dy, g, dg_ref):
            xh, rstd = _rms(x, None)
            dxh = dy * g
            dx = rstd * (dxh - xh * jnp.mean(dxh * xh, axis=-1, keepdims=True))
            pg = jnp.sum(dy * xh, axis=0, keepdims=True)

            @pl.when(i == 0)
            def _():
                dg_ref[...] = pg

            @pl.when(i > 0)
            def _():
                dg_ref[...] += pg

            return dx

        dxq = rms_bwd(dn_ref[:, :Q_RANK], dcq_ref[...], gq_ref[...], dgq_ref)
        dd_ref[:, :Q_RANK] = dxq.astype(dd_ref.dtype)
        dxkv = rms_bwd(dn_ref[:, Q_RANK:Q_RANK + KV_RANK], dckv_ref[...], gkv_ref[...], dgkv_ref)
        dd_ref[:, Q_RANK:Q_RANK + KV_RANK] = dxkv.astype(dd_ref.dtype)
        dd_ref[:, Q_RANK + KV_RANK:] = _unrope(dkp_ref[...], c_ref[...], s1_ref[...], s2_ref[...]).astype(dd_ref.dtype)

    tab = pl.BlockSpec((tm, HEAD), lambda i: (i, 0))
    vq = pl.BlockSpec((1, Q_RANK), lambda i: (0, 0))
    vkv = pl.BlockSpec((1, KV_RANK), lambda i: (0, 0))
    return pl.pallas_call(
        body, name=name, grid=(S // tm,),
        in_specs=[pl.BlockSpec((tm, ODD_IN_PAD), lambda i: (i, 0)), pl.BlockSpec((tm, Q_RANK), lambda i: (i, 0)),
                  pl.BlockSpec((tm, KV_RANK), lambda i: (i, 0)), tab, vq, vkv, tab, tab, tab],
        out_specs=[pl.BlockSpec((tm, ODD_IN_PAD), lambda i: (i, 0)), vq, vkv],
        out_shape=[jax.ShapeDtypeStruct((S, ODD_IN_PAD), BF16), jax.ShapeDtypeStruct((1, Q_RANK), F32),
                   jax.ShapeDtypeStruct((1, KV_RANK), F32)],
        compiler_params=_params(("arbitrary",)),
    )(down, dcq, dckv, dkp, gq, gkv, *tabs)


ATT_TQ = 256


def _attn_probs(q_ref, kv_ref, kp_ref, c_ref, s1_ref, s2_ref, i, nk):
    qn = q_ref[:, :HEAD].astype(BF16)
    qp = _rope(q_ref[:, HEAD:], c_ref[...], s1_ref[...], s2_ref[...]).astype(BF16)
    kn = kv_ref[:nk, :HEAD]
    sc = (_dot(qn, kn, _NT) + _dot(qp, kp_ref[:nk, :], _NT)) * ATT_SCALE
    q_chunk = (i * ATT_TQ + lax.broadcasted_iota(jnp.int32, sc.shape, 0)) // CHUNK
    k_chunk = lax.broadcasted_iota(jnp.int32, sc.shape, 1) // CHUNK
    sc = jnp.where(k_chunk <= q_chunk, sc, jnp.finfo(F32).min)
    e = jnp.exp(sc - jnp.max(sc, axis=-1, keepdims=True))
    p = e * (1.0 / jnp.sum(e, axis=-1, keepdims=True))
    return p, qn, qp, kn


def _for_each_prefix(i, fn):
    for k in range(S // ATT_TQ):
        pl.when(i == k)(functools.partial(fn, (k + 1) * ATT_TQ))


def _attn_specs():
    q = pl.BlockSpec((ATT_TQ, QHEAD_PAD), lambda h, i: (i, h))
    kv = pl.BlockSpec((S, QHEAD_PAD), lambda h, i: (0, h))
    kp = pl.BlockSpec((S, HEAD), lambda h, i: (0, 0))
    tab = pl.BlockSpec((ATT_TQ, HEAD), lambda h, i: (i, 0))
    o = pl.BlockSpec((ATT_TQ, HEAD), lambda h, i: (i, h))
    return q, kv, kp, tab, o


def attn_fwd(name, q, kv, kp, tabs):
    def body(q_ref, kv_ref, kp_ref, c_ref, s1_ref, s2_ref, o_ref):
        i = pl.program_id(1)

        def run(nk):
            p, _, _, _ = _attn_probs(q_ref, kv_ref, kp_ref, c_ref, s1_ref, s2_ref, i, nk)
            o_ref[...] = _dot(p, kv_ref[:nk, HEAD:], _NN).astype(o_ref.dtype)

        _for_each_prefix(i, run)

    qs, kvs, kps, tab, os = _attn_specs()
    return pl.pallas_call(
        body, name=name, grid=(MLA_HEADS, S // ATT_TQ), in_specs=[qs, kvs, kps, tab, tab, tab], out_specs=os,
        out_shape=jax.ShapeDtypeStruct((S, MLA_HEADS * HEAD), BF16),
        compiler_params=_params(("parallel", "parallel")),
    )(q, kv, kp, *tabs)


def attn_bwd(name, q, kv, kp, do, tabs):
    def body(q_ref, kv_ref, kp_ref, do_ref, c_ref, s1_ref, s2_ref, dq_ref, dkv_ref, dkp_ref):
        h, i = pl.program_id(0), pl.program_id(1)

        @pl.when(i == 0)
        def _():
            dkv_ref[...] = jnp.zeros_like(dkv_ref)

        @pl.when((i == 0) & (h == 0))
        def _():
            dkp_ref[...] = jnp.zeros_like(dkp_ref)

        def run(nk):
            p, qn, qp, kn = _attn_probs(q_ref, kv_ref, kp_ref, c_ref, s1_ref, s2_ref, i, nk)
            do_v = do_ref[...]
            dp = _dot(do_v, kv_ref[:nk, HEAD:], _NT)
            ds = (p * (dp - jnp.sum(p * dp, axis=-1, keepdims=True)) * ATT_SCALE).astype(BF16)
            dq_ref[:, :HEAD] = _dot(ds, kn, _NN).astype(dq_ref.dtype)
            dqp = _unrope(_dot(ds, kp_ref[:nk, :], _NN), c_ref[...], s1_ref[...], s2_ref[...])
            dq_ref[:, HEAD:] = dqp.astype(dq_ref.dtype)
            dkv_ref[:nk, :HEAD] += _dot(ds, qn, _TN)
            dkv_ref[:nk, HEAD:] += _dot(p, do_v, _TN)
            dkp_ref[:nk, :] += _dot(ds, qp, _TN)

        _for_each_prefix(i, run)

    qs, kvs, kps, tab, os = _attn_specs()
    return pl.pallas_call(
        body, name=name, grid=(MLA_HEADS, S // ATT_TQ), in_specs=[qs, kvs, kps, os, tab, tab, tab],
        out_specs=[qs, kvs, kps],
        out_shape=[jax.ShapeDtypeStruct((S, MLA_HEADS * QHEAD_PAD), BF16),
                   jax.ShapeDtypeStruct((S, MLA_HEADS * QHEAD_PAD), F32), jax.ShapeDtypeStruct((S, HEAD), F32)],
        compiler_params=_params(("arbitrary", "arbitrary")),
    )(q, kv, kp, do, *tabs)


def adamw(name, w, g, m, v):
    rows, cols = w.shape
    tr = rows
    for cand in (512, 256, 128, 64, 32, 16, 8):
        if rows % cand == 0 and cand * cols * 4 <= 2 * 1024 * 1024:
            tr = cand
            break

    def body(w_ref, g_ref, m_ref, v_ref, d_ref, nm_ref, nv_ref):
        g_v = g_ref[...]
        nm = ADAM_B1 * m_ref[...] + (1.0 - ADAM_B1) * g_v
        nv = ADAM_B2 * v_ref[...] + (1.0 - ADAM_B2) * (g_v * g_v)
        m_hat = nm / (1.0 - ADAM_B1 ** ADAM_STEP)
        v_hat = nv / (1.0 - ADAM_B2 ** ADAM_STEP)
        d_ref[...] = (-ADAM_LR) * (m_hat / (jnp.sqrt(v_hat) + ADAM_EPS) + ADAM_WD * w_ref[...])
        nm_ref[...] = nm
        nv_ref[...] = nv

    blk = pl.BlockSpec((tr, cols), lambda i: (i, 0))
    shape = jax.ShapeDtypeStruct((rows, cols), F32)
    return pl.pallas_call(
        body, name=name, grid=(rows // tr,), in_specs=[blk] * 4, out_specs=[blk] * 3, out_shape=[shape] * 3,
        compiler_params=_params(("parallel",)),
    )(w, g, m, v)


def _local_step(x, pos_col, target, sm, weights_of, emit_grads, prefetch):
    inv_freq = ROPE_THETA ** (-jnp.arange(0, QK_ROPE, 2, dtype=F32) / QK_ROPE)
    inv_freq = jnp.concatenate([inv_freq, inv_freq, jnp.zeros((HEAD - QK_ROPE,), F32)])[None, :]
    tabs = rope_tables(pos_col, inv_freq)
    saved, wts = [], {}
    for layer in range(DEPTH):
        j = layer // 2
        n = "l%d_" % layer
        sv = {"x": x}
        wm = wts["mix%d" % layer] = weights_of("mix%d" % layer, x)
        if layer == 0:
            sm = dict(sm, conv_w=wm["conv_w"], gq=wm["gq"], gkv=wm["gkv"])
        if layer % 2 == 0:
            proj = mm_cols(n + "proj", x, wm["w_in"], F32)
            fetched = prefetch("mlp%d" % layer, proj)
            y_pool = pool_fwd(n + "pool", proj, sm["pool_w"][j], sm["pool_scale"][j][None])
            y_lru, hstate = lru_fwd(n + "lru", proj, sm["conv_w"][j], sm["conv_b"][j][None], sm["w_a"][j],
                                    sm["b_a"][j][None], sm["w_x"][j], sm["b_x"][j][None], sm["lam"][j][None])
            ycat = jnp.concatenate([y_pool, y_lru], axis=1)
            mix_in, w_mix = ycat, wm["w_out"]
            sv.update(proj=proj, hstate=hstate, ycat=ycat)
        else:
            down = mm_rows(n + "down", x, wm["w_down"], F32)
            fetched = prefetch("mlp%d" % layer, down)
            cq, ckv, kp = mla_prep(n + "prep", down, sm["gq"][j][None], sm["gkv"][j][None], tabs)
            q = mm_cols(n + "q", cq, wm["w_qb"], F32)
            kv = mm_cols(n + "kv", ckv, wm["w_kvb"], BF16)
            o = attn_fwd(n + "attn", q, kv, kp, tabs)
            mix_in, w_mix = o, wm["w_o"]
            sv.update(down=down, cq=cq, ckv=ckv, kp=kp, q=q, kv=kv, o=o)
        x1, xhat1, rstd1 = mm_rows(n + "mixout", mix_in, w_mix, F32, after=fetched,
                                   ln=(x, sm["ln_mix_g"][layer][None], sm["ln_mix_b"][layer][None]))
        wf = wts["mlp%d" % layer] = weights_of("mlp%d" % layer, x1)
        x2, xhat2, rstd2, relu = mlp_fwd(n + "mlp", x1, wf["w1"], wf["w2"], sm["ln_ffn_g"][layer][None],
                                         sm["ln_ffn_b"][layer][None])
        sv.update(xhat1=xhat1, rstd1=rstd1, x1=x1, relu=relu, xhat2=xhat2, rstd2=rstd2)
        saved.append(sv)
        x = x2

    dx, loss_tile = loss_head(x, target)
    gs = {k: [None] * (DEPTH if k.startswith("ln_") else DEPTH // 2) for k in sm}
    sent = None
    for layer in reversed(range(DEPTH)):
        j = layer // 2
        n = "l%d_" % layer
        sv = saved[layer]
        wm, wf = wts["mix%d" % layer], wts["mlp%d" % layer]
        dz, gs["ln_ffn_g"][layer], gs["ln_ffn_b"][layer] = ln_bwd(
            n + "ln_ffn_b", dx, sv["xhat2"], sv["rstd2"], sm["ln_ffn_g"][layer][None], after=sent)
        dx, g_w1, g_w2 = mlp_bwd(n + "mlp_b", dz, sv["relu"], sv["x1"], wf["w1"], wf["w2"])
        sent = emit_grads("mlp%d" % layer, {"w1": g_w1, "w2": g_w2})
        dz, gs["ln_mix_g"][layer], gs["ln_mix_b"][layer] = ln_bwd(
            n + "ln_mix_b", dx, sv["xhat1"], sv["rstd1"], sm["ln_mix_g"][layer][None], after=sent)
        if layer % 2 == 0:
            dycat, g_out = linear_bwd(n + "mixout_b", sv["ycat"], dz, wm["w_out"], col_sharded=False, da_dtype=F32)
            du_pool, gs["pool_w"][j], gs["pool_scale"][j] = pool_bwd(
                n + "pool_b", sv["proj"], dycat, sm["pool_w"][j], sm["pool_scale"][j][None])
            (du_lru, du_gate, gs["w_a"][j], gs["w_x"][j], gs["b_a"][j], gs["b_x"][j], gs["lam"][j], gs["conv_w"][j],
             gs["conv_b"][j]) = lru_bwd(n + "lru_b", sv["proj"], sv["hstate"], dycat, sm["conv_w"][j],
                                        sm["conv_b"][j][None], sm["w_a"][j], sm["b_a"][j][None], sm["w_x"][j],
                                        sm["b_x"][j][None], sm["lam"][j][None])
            dproj = jnp.concatenate([du_pool, du_lru, du_gate], axis=1)
            dx, g_in = linear_bwd(n + "proj_b", sv["x"], dproj, wm["w_in"], col_sharded=True, da_dtype=F32, resid=dz)
            sent = emit_grads("mix%d" % layer, {"w_in": g_in, "w_out": g_out})
        else:
            do, g_o = linear_bwd(n + "attnout_b", sv["o"], dz, wm["w_o"], col_sharded=False, da_dtype=BF16)
            dq, dkv, dkp = attn_bwd(n + "attn_b", sv["q"], sv["kv"], sv["kp"], do, tabs)
            dcq, g_qb = linear_bwd(n + "q_b", sv["cq"], dq, wm["w_qb"], col_sharded=True, da_dtype=F32)
            dckv, g_kvb = linear_bwd(n + "kv_b", sv["ckv"], dkv, wm["w_kvb"], col_sharded=True, da_dtype=F32)
            ddown, gs["gq"][j], gs["gkv"][j] = mla_prep_bwd(
                n + "prep_b", sv["down"], dcq, dckv, dkp, sm["gq"][j][None], sm["gkv"][j][None], tabs)
            dx, g_down = linear_bwd(n + "down_b", sv["x"], ddown, wm["w_down"], col_sharded=False, da_dtype=F32,
                                    resid=dz)
            sent = emit_grads("mix%d" % layer, {"w_down": g_down, "w_qb": g_qb, "w_kvb": g_kvb, "w_o": g_o})
    gs = {k: jnp.stack([a.reshape(sm[k].shape[1:]) for a in v]) for k, v in gs.items()}
    return loss_tile[0, 0], dx, gs, sent


def _place():
    x, y, c = lax.axis_index("x"), lax.axis_index("y"), lax.axis_index("c")
    chips = [(1 - x, y), (x, 1 - y), (1 - x, 1 - y)]
    return x, y, c, chips


def _hbm_call(body, name, args, out_shape, scratch, aliases=None):
    return pl.pallas_call(
        body, name=name, in_specs=[pl.BlockSpec(memory_space=pl.ANY)] * len(args),
        out_specs=[pl.BlockSpec(memory_space=pl.ANY)] * len(out_shape), out_shape=out_shape,
        scratch_shapes=scratch, input_output_aliases=aliases or {},
        compiler_params=pltpu.CompilerParams(has_side_effects=True),
    )(*args)


HBM_SPEC = pl.BlockSpec(memory_space=pltpu.HBM)
SEM_SPEC = pl.BlockSpec(memory_space=pltpu.SEMAPHORE)
EFFECT = pltpu.SideEffectType.DATAFLOW_SIDE_EFFECTING


def _remote(src, dst, send_sem, recv_sem, device):
    return pltpu.make_async_remote_copy(src_ref=src, dst_ref=dst, send_sem=send_sem, recv_sem=recv_sem,
                                        device_id=device, device_id_type=MESH)


def place_own(name, srcs, place, steps=4):
    n = len(srcs)
    in_specs, out_specs, out_shape = [], [], []
    for s in srcs:
        rows, cols = s.shape
        tr = rows // steps
        in_specs.append(pl.BlockSpec((tr, cols), lambda i, p: (i, 0)))
        out_specs.append(pl.BlockSpec((None, tr, cols), lambda i, p: (p[0], i, 0)))
        out_shape.append(jax.ShapeDtypeStruct((N_CHIPS, rows, cols), s.dtype))

    def body(p_ref, *refs):
        for i_ref, o_ref in zip(refs[:n], refs[n:]):
            o_ref[...] = i_ref[...]

    return pl.pallas_call(
        body, name=name, out_shape=out_shape,
        grid_spec=pltpu.PrefetchScalarGridSpec(num_scalar_prefetch=1, grid=(steps,), in_specs=in_specs,
                                               out_specs=out_specs),
        compiler_params=_params(("parallel",)),
    )(place, *srcs)


def split_start(name, groups, n_sems, plan, token_in=None):
    sizes = [len(srcs) for srcs, _ in groups]
    n, n_groups = sum(sizes), len(groups)
    srcs = [pltpu.with_memory_space_constraint(a, pltpu.HBM) for s, _ in groups for a in s]
    lands = [pltpu.with_memory_space_constraint(a, pltpu.HBM) for _, l in groups for a in l]
    extra = [] if token_in is None else [token_in]

    def body(*refs):
        src_refs, land_refs = refs[:n], refs[n:2 * n]
        outs = refs[2 * n + len(extra):]
        off = 0
        for g, size in enumerate(sizes):
            sends, _ = plan(src_refs[off:off + size], land_refs[off:off + size], outs[2 * g], outs[2 * g + 1])
            for cp in sends:
                cp.start()
            off += size
        token = outs[2 * n_groups + 2 * n]
        token[...] = jnp.zeros_like(token)

    sems = [pltpu.SemaphoreType.DMA((size * n_sems,)) for size in sizes for _ in range(2)]
    res = pl.pallas_call(
        body, name=name,
        out_shape=(*sems, *[pltpu.HBM(a.shape, a.dtype) for a in srcs + lands], jax.ShapeDtypeStruct((8, 128), F32)),
        in_specs=[HBM_SPEC] * (2 * n) + [pl.BlockSpec(memory_space=pl.ANY)] * len(extra),
        out_specs=(*[SEM_SPEC] * len(sems), *[HBM_SPEC] * (2 * n), pl.BlockSpec(memory_space=pltpu.VMEM)),
        input_output_aliases={i: len(sems) + i for i in range(2 * n)},
        compiler_params=pltpu.CompilerParams(has_side_effects=EFFECT),
    )(*srcs, *lands, *extra)
    started, off = [], 0
    bufs = res[len(sems):]
    for g, size in enumerate(sizes):
        started.append(dict(send=res[2 * g], recv=res[2 * g + 1], srcs=list(bufs[off:off + size]),
                            lands=list(bufs[n + off:n + off + size]), plan=plan))
        off += size
    return started, res[-1]


def _wait_started(st, bufs, send_sems, recv_sems):
    n = len(st["srcs"])
    sends, expects = st["plan"](bufs[:n], bufs[n:], send_sems, recv_sems)
    for cp in sends:
        cp.wait_send()
    for cp in expects:
        cp.wait_recv()


def split_wait(name, started, after):
    sizes = [len(st["srcs"]) + len(st["lands"]) for st in started]
    n_buf = sum(sizes)

    def body(*refs):
        bufs, sems = refs[:n_buf], refs[n_buf:n_buf + 2 * len(started)]
        off = 0
        for g, (st, n) in enumerate(zip(started, sizes)):
            _wait_started(st, bufs[off:off + n], sems[2 * g], sems[2 * g + 1])
            off += n

    bufs = [a for st in started for a in st["srcs"] + st["lands"]]
    sems = [s for st in started for s in (st["send"], st["recv"])]
    res = pl.pallas_call(
        body, name=name, out_shape=tuple(pltpu.HBM(a.shape, a.dtype) for a in bufs),
        in_specs=[HBM_SPEC] * n_buf + [SEM_SPEC] * len(sems) + [pl.BlockSpec(memory_space=pl.ANY)],
        out_specs=tuple([HBM_SPEC] * n_buf), input_output_aliases={i: i for i in range(n_buf)},
        compiler_params=pltpu.CompilerParams(has_side_effects=EFFECT),
    )(*bufs, *sems, after)
    out, off = [], 0
    for st, n in zip(started, sizes):
        out.append((list(res[off:off + len(st["srcs"])]), list(res[off + len(st["srcs"]):off + n])))
        off += n
    return out


def split_relay(name, st, n_sems, plan, after):
    n_src, n = len(st["srcs"]), len(st["lands"])

    def body(*refs):
        bufs = refs[:n_src + n]
        outs = refs[n_src + n + 3:]
        _wait_started(st, bufs, refs[n_src + n], refs[n_src + n + 1])
        sends, _ = plan((), bufs[n_src:], outs[0], outs[1])
        for cp in sends:
            cp.start()
        outs[2 + n][...] = jnp.zeros((8, 128), F32)

    bufs = st["srcs"] + st["lands"]
    res = pl.pallas_call(
        body, name=name,
        out_shape=(pltpu.SemaphoreType.DMA((n * n_sems,)), pltpu.SemaphoreType.DMA((n * n_sems,)),
                   *[pltpu.HBM(a.shape, a.dtype) for a in st["lands"]], jax.ShapeDtypeStruct((8, 128), F32)),
        in_specs=[HBM_SPEC] * (n_src + n) + [SEM_SPEC] * 2 + [pl.BlockSpec(memory_space=pl.ANY)],
        out_specs=(SEM_SPEC, SEM_SPEC, *[HBM_SPEC] * n, pl.BlockSpec(memory_space=pltpu.VMEM)),
        input_output_aliases={n_src + i: 2 + i for i in range(n)},
        compiler_params=pltpu.CompilerParams(has_side_effects=EFFECT),
    )(*bufs, st["send"], st["recv"], after)
    return dict(send=res[0], recv=res[1], srcs=[], lands=list(res[2:2 + n]), plan=plan), res[-1]


def gather_plan(src_refs, land_refs, send_sems, recv_sems):
    x, y, c, chips = _place()
    me = 2 * x + y
    sends, expects = [], []
    for k, (s, d) in enumerate(zip(src_refs, land_refs)):
        mine = pl.ds(c * (s.shape[0] // 2), s.shape[0] // 2)
        for j, (px, py) in enumerate(chips):
            sem = 3 * k + j
            sends.append(_remote(s.at[mine], d.at[me, mine], send_sems.at[sem], recv_sems.at[sem], (px, py, c)))
            expects.append(_remote(s.at[mine], d.at[2 * px + py, mine], send_sems.at[sem], recv_sems.at[sem], (px, py, c)))
    return sends, expects


def relay_plan(src_refs, land_refs, send_sems, recv_sems):
    x, y, c, chips = _place()
    sends, expects = [], []
    for k, d in enumerate(land_refs):
        hr = d.shape[1] // 2
        mine, theirs = pl.ds(c * hr, hr), pl.ds((1 - c) * hr, hr)
        for j, (px, py) in enumerate(chips):
            sem, chip = 3 * k + j, 2 * px + py
            sends.append(_remote(d.at[chip, mine], d.at[chip, mine], send_sems.at[sem], recv_sems.at[sem], (x, y, 1 - c)))
            expects.append(_remote(d.at[chip, mine], d.at[chip, theirs], send_sems.at[sem], recv_sems.at[sem], (x, y, 1 - c)))
    return sends, expects


def _reduce_part(s, chip_idx, h):
    hr = s.shape[1] // 2
    return s.at[chip_idx if s.shape[0] == N_CHIPS else 0, pl.ds(h * hr, hr)]


def reduce_plan(src_refs, land_refs, send_sems, recv_sems):
    x, y, c, chips = _place()
    me_chip, me_dev = 2 * x + y, 4 * x + 2 * y + c
    sends, expects = [], []
    for k, (s, d) in enumerate(zip(src_refs, land_refs)):
        part = functools.partial(_reduce_part, s)
        for j, (px, py) in enumerate(chips):
            for h in range(2):
                sends.append(_remote(part(2 * px + py, h), d.at[me_dev], send_sems.at[7 * k + 2 * j + h],
                                     recv_sems.at[7 * k + 2 * j + c], (px, py, h)))
                expects.append(_remote(part(me_chip, c), d.at[4 * px + 2 * py + h], send_sems.at[7 * k + 2 * j + h],
                                       recv_sems.at[7 * k + 2 * j + h], (px, py, h)))
        sends.append(_remote(part(me_chip, 1 - c), d.at[me_dev], send_sems.at[7 * k + 6], recv_sems.at[7 * k + 6],
                             (x, y, 1 - c)))
        expects.append(_remote(part(me_chip, c), d.at[me_dev + 1 - 2 * c], send_sems.at[7 * k + 6],
                               recv_sems.at[7 * k + 6], (x, y, 1 - c)))
    return sends, expects


def sibling_swap_halves(name, fulls):
    n = len(fulls)

    def body(*refs):
        outs = refs[n:2 * n]
        send_sems, recv_sems = refs[2 * n:]
        x, y, c, _ = _place()
        copies = []
        for k in range(n):
            nl, rows = fulls[k].shape[:2]
            hr = rows // 2
            mine = outs[k].at[pl.ds(0, nl), pl.ds(c * hr, hr)]
            theirs = outs[k].at[pl.ds(0, nl), pl.ds((1 - c) * hr, hr)]
            copies.append((_remote(mine, mine, send_sems.at[k], recv_sems.at[k], (x, y, 1 - c)),
                           _remote(mine, theirs, send_sems.at[k], recv_sems.at[k], (x, y, 1 - c))))
        for send, _ in copies:
            send.start()
        for send, recv in copies:
            send.wait_send()
            recv.wait_recv()

    out_shape = [jax.ShapeDtypeStruct(f.shape, f.dtype) for f in fulls]
    scratch = [pltpu.SemaphoreType.DMA((n,)), pltpu.SemaphoreType.DMA((n,))]
    return _hbm_call(body, name, fulls, out_shape, scratch, aliases={k: k for k in range(n)})


def _tile_rows(rows, cols, budget_bytes):
    best = None
    for t in range(16, rows + 1, 16):
        if rows % t == 0 and t * cols * 4 <= budget_bytes:
            best = t
    assert best is not None, (rows, cols)
    return best


N_DEV = 8


def sum_devices(name, landed, own, layer, n_layers, prev, place):
    _, hr, cols = landed.shape
    tr = _tile_rows(hr, cols, 512 * 1024)
    n_blk = hr // tr
    own_slot = (lambda r, p: (p[0], p[1] * n_blk + r, 0)) if own.shape[0] == N_CHIPS else (lambda r, p: (0, p[1] * n_blk + r, 0))

    def body(p_ref, r_ref, own_ref, *rest):
        o_ref = rest[-1]
        mine = own_ref[...].astype(F32)
        acc = jnp.zeros_like(mine)
        for d in range(N_DEV):
            acc = acc + jnp.where(p_ref[2] == d, mine, r_ref[d].astype(F32))
        o_ref[...] = acc

    in_specs = [pl.BlockSpec((N_DEV, tr, cols), lambda r, p: (0, r, 0)), pl.BlockSpec((None, tr, cols), own_slot)]
    args = [place, landed, own]
    aliases = {}
    if prev is not None:
        in_specs.append(pl.BlockSpec(memory_space=pl.ANY))
        args.append(prev)
        aliases = {3: 0}
    return pl.pallas_call(
        body, name=name, out_shape=jax.ShapeDtypeStruct((n_layers, 2 * hr, cols), F32),
        grid_spec=pltpu.PrefetchScalarGridSpec(
            num_scalar_prefetch=1, grid=(n_blk,), in_specs=in_specs,
            out_specs=pl.BlockSpec((None, tr, cols), lambda r, p: (layer, p[1] * n_blk + r, 0))),
        input_output_aliases=aliases, compiler_params=_params(("parallel",)),
    )(*args)


def _pack(arrs, rows_multiple):
    flat = []
    for a in arrs:
        v = a.reshape(-1).astype(F32)
        flat.append(jnp.pad(v, (0, (-v.shape[0]) % HEAD)))
    v = jnp.concatenate(flat)
    v = jnp.pad(v, (0, (-v.shape[0]) % (HEAD * rows_multiple)))
    return v.reshape(-1, HEAD)


def _unpack(packed, shapes):
    flat = packed.reshape(-1)
    out, off = [], 0
    for shp in shapes:
        size = int(np.prod(shp))
        out.append(flat[off:off + size].reshape(shp))
        off += size + (-size) % HEAD
    return out


BIG = ["even_w_in", "even_w_out", "mla_w_down", "mla_w_qb", "mla_w_kvb", "mla_w_o", "mlp_w1", "mlp_w2"]
BIG_KEY = {"even_w_in": "w_in", "even_w_out": "w_out", "mla_w_down": "w_down", "mla_w_qb": "w_qb",
           "mla_w_kvb": "w_kvb", "mla_w_o": "w_o", "mlp_w1": "w1", "mlp_w2": "w2"}
SMALL_KEY = {"ln_mix_g": "ln_mix_g", "ln_mix_b": "ln_mix_b", "ln_ffn_g": "ln_ffn_g", "ln_ffn_b": "ln_ffn_b",
             "pool_w": "pool_w", "pool_scale": "pool_scale", "lru_conv_w": "conv_w", "lru_conv_b": "conv_b",
             "lru_w_a": "w_a", "lru_b_a": "b_a", "lru_w_x": "w_x", "lru_b_x": "b_x", "lru_lambda": "lam",
             "mla_q_norm_g": "gq", "mla_kv_norm_g": "gkv"}
SMALL = list(SMALL_KEY)
SMALL_SHARDED = ["lru_conv_w", "mla_q_norm_g", "mla_kv_norm_g"]
WEIGHTS = ["ln_mix_g", "ln_mix_b", "ln_ffn_g", "ln_ffn_b", "even_w_in", "pool_w", "pool_scale", "lru_conv_w",
           "lru_conv_b", "lru_w_a", "lru_b_a", "lru_w_x", "lru_b_x", "lru_lambda", "even_w_out", "mla_w_down",
           "mla_q_norm_g", "mla_kv_norm_g", "mla_w_qb", "mla_w_kvb", "mla_w_o", "mlp_w1", "mlp_w2"]


GROUPS = ["mix0", "mlp0", "mix1", "mlp1", "mix2", "mlp2", "mix3", "mlp3"]


def _group_keys(group):
    layer = int(group[3:])
    if group.startswith("mlp"):
        return [("mlp_w1", "w1", layer), ("mlp_w2", "w2", layer)]
    if layer % 2 == 0:
        return [("even_w_in", "w_in", layer // 2), ("even_w_out", "w_out", layer // 2)]
    return [("mla_w_down", "w_down", layer // 2), ("mla_w_qb", "w_qb", layer // 2),
            ("mla_w_kvb", "w_kvb", layer // 2), ("mla_w_o", "w_o", layer // 2)]


def _pad_q_heads(w):
    lead = w.shape[:-1]
    w = w.reshape(lead + (2, QK_NOPE + QK_ROPE))
    w = jnp.pad(w, ((0, 0),) * len(lead) + ((0, 0), (0, QHEAD_PAD - QK_NOPE - QK_ROPE)))
    return w.reshape(lead + (2 * QHEAD_PAD,))


def _unpad_q_heads(g):
    lead = g.shape[:-1]
    return g.reshape(lead + (2, QHEAD_PAD))[..., :QK_NOPE + QK_ROPE].reshape(lead + (2 * (QK_NOPE + QK_ROPE),))


def _step(x, positions, loss_target, w, m, v):
    cx, cy, cc = lax.axis_index("x"), lax.axis_index("y"), lax.axis_index("c")
    chip = 2 * cx + cy
    place = jnp.stack([chip, cc, 2 * chip + cc]).astype(jnp.int32)

    prepared = dict(w)
    prepared["mla_w_down"] = jnp.pad(w["mla_w_down"], ((0, 0), (0, 0), (0, ODD_IN_PAD - ODD_IN)))
    prepared["mla_w_qb"] = _pad_q_heads(w["mla_w_qb"])
    small_shard_shapes = [w[k].shape for k in SMALL_SHARDED]
    sources = {g: [prepared[name][idx].astype(BF16) for name, _, idx in _group_keys(g)] for g in GROUPS}
    sources[GROUPS[0]].append(_pack([w[k] for k in SMALL_SHARDED], 32))
    gathering, token = {}, None
    for name, part in (("first", GROUPS[:1]), ("rest", GROUPS[1:])):
        zones = place_own("own_" + name, [a for g in part for a in sources[g]], place)
        groups, off = [], 0
        for g in part:
            groups.append((sources[g], zones[off:off + len(sources[g])]))
            off += len(sources[g])
        started, token = split_start("gather_" + name, groups, 3, gather_plan, token)
        gathering.update(zip(part, started))
    relayed, relay_token = {}, {}

    def prefetch(g, after):
        relayed[g], relay_token[g] = split_relay("relay_" + g, gathering[g], 3, relay_plan, after)
        return relay_token[g]

    prefetch(GROUPS[0], token)

    def weights_of(g, after):
        _, lands = split_wait("gathered_" + g, [relayed[g]], relay_token[g] if g == GROUPS[0] else after)[0]
        if g.startswith("mlp") and g != GROUPS[-1]:
            prefetch(GROUPS[GROUPS.index(g) + 1], lands[0])
        out = {key: land for (_, key, _), land in zip(_group_keys(g), lands)}
        if g == GROUPS[0]:
            per_chip = [_unpack(lands[-1][s], small_shard_shapes) for s in range(N_CHIPS)]
            for i, key in enumerate(("conv_w", "gq", "gkv")):
                out[key] = jnp.concatenate([p[i] for p in per_chip], axis=-1)
        return out

    reducing = []

    def reduce_start(g, srcs, token_in=None):
        lands = [lax.empty((N_DEV, s.shape[1] // 2, s.shape[2]), s.dtype) for s in srcs]
        started, token = split_start("reduce_" + g, [(srcs, lands)], 7, reduce_plan, token_in)
        reducing.append((g, started[0]))
        return token

    def emit_grads(g, grads):
        return reduce_start(g, [grads[key] for _, key, _ in _group_keys(g)])

    sm = {SMALL_KEY[k]: w[k] for k in SMALL if k not in SMALL_SHARDED}
    loss, grad_x, gs, last_sent = _local_step(x[0], positions.reshape(S, 1), loss_target[0], sm, weights_of,
                                              emit_grads, prefetch)
    loss = lax.psum(loss, ("x", "y", "c"))

    small_shapes = [gs[SMALL_KEY[k]].shape for k in SMALL]
    gs_pack = _pack([gs[SMALL_KEY[k]] for k in SMALL], 32)[None]
    small_sent = reduce_start("small", [gs_pack], last_sent)

    grad, delta, new_m, new_v = {}, {}, {}, {}
    late_groups = ("mix0", "small")
    stacks, after = {}, small_sent
    for late in (False, True):
        part = [(g, st) for g, st in reducing if (g in late_groups) == late]
        landed = split_wait("reduced_late" if late else "reduced_early", [st for _, st in part], after)
        for (g, _), (owns, lands) in zip(part, landed):
            if g == "small":
                stacks["small"] = sum_devices("sum_small", lands[0], owns[0], 0, 1, None, place)
                continue
            for (name, key, idx), own, land in zip(_group_keys(g), owns, lands):
                stacks[name] = sum_devices("sum_%s%d" % (key, idx), land, own, idx, w[name].shape[0], stacks.get(name), place)
        names = [k for k in BIG if (k in ("even_w_in", "even_w_out")) == late] + (["small"] if late else [])
        reduced = dict(zip(names, sibling_swap_halves("swap_late" if late else "swap_early", [stacks[k] for k in names])))
        if not late:
            reduced["mla_w_down"] = reduced["mla_w_down"][..., :ODD_IN]
            reduced["mla_w_qb"] = _unpad_q_heads(reduced["mla_w_qb"])
        for k in names:
            if k == "small":
                continue
            grad[k] = reduced[k]
            shp = w[k].shape
            view = lambda a: a.reshape(-1, shp[-1])
            d, nm, nv = adamw("adamw_" + BIG_KEY[k], view(w[k]), view(grad[k]), view(m[k]), view(v[k]))
            delta[k], new_m[k], new_v[k] = d.reshape(shp), nm.reshape(shp), nv.reshape(shp)
            after = d
    g_small = dict(zip(SMALL, _unpack(reduced["small"], small_shapes)))
    for k in SMALL_SHARDED:
        width = w[k].shape[-1]
        g_small[k] = lax.dynamic_slice_in_dim(g_small[k], chip * width, width, axis=-1)
    grad.update(g_small)

    shapes = [w[k].shape for k in SMALL]
    d, nm, nv = adamw("adamw_small", *[_pack([t[k] for k in SMALL], 512) for t in (w, grad, m, v)])
    for k, dk, mk, vk in zip(SMALL, _unpack(d, shapes), _unpack(nm, shapes), _unpack(nv, shapes)):
        delta[k], new_m[k], new_v[k] = dk, mk, vk
    return (loss, grad_x[None], *[grad[k] for k in WEIGHTS], *[delta[k] for k in WEIGHTS],
            *[new_m[k] for k in WEIGHTS], *[new_v[k] for k in WEIGHTS])


def kernel(x, positions, ln_mix_g, ln_mix_b, ln_ffn_g, ln_ffn_b, even_w_in, pool_w, pool_scale, lru_conv_w, lru_conv_b, lru_w_a, lru_b_a, lru_w_x, lru_b_x, lru_lambda, even_w_out, mla_w_down, mla_q_norm_g, mla_kv_norm_g, mla_w_qb, mla_w_kvb, mla_w_o, mlp_w1, mlp_w2, loss_target, m_ln_mix_g, m_ln_mix_b, m_ln_ffn_g, m_ln_ffn_b, m_even_w_in, m_pool_w, m_pool_scale, m_lru_conv_w, m_lru_conv_b, m_lru_w_a, m_lru_b_a, m_lru_w_x, m_lru_b_x, m_lru_lambda, m_even_w_out, m_mla_w_down, m_mla_q_norm_g, m_mla_kv_norm_g, m_mla_w_qb, m_mla_w_kvb, m_mla_w_o, m_mlp_w1, m_mlp_w2, v_ln_mix_g, v_ln_mix_b, v_ln_ffn_g, v_ln_ffn_b, v_even_w_in, v_pool_w, v_pool_scale, v_lru_conv_w, v_lru_conv_b, v_lru_w_a, v_lru_b_a, v_lru_w_x, v_lru_b_x, v_lru_lambda, v_even_w_out, v_mla_w_down, v_mla_q_norm_g, v_mla_kv_norm_g, v_mla_w_qb, v_mla_w_kvb, v_mla_w_o, v_mlp_w1, v_mlp_w2):
    args = locals()
    w = {k: args[k] for k in WEIGHTS}
    m = {k: args["m_" + k] for k in WEIGHTS}
    v = {k: args["v_" + k] for k in WEIGHTS}
    return _step(x, positions, loss_target, w, m, v)
```

```python
import functools
import math

import jax
import jax.numpy as jnp
import numpy as np
from jax import lax
from jax.experimental import pallas as pl
from jax.experimental.pallas import tpu as pltpu

F32 = jnp.float32
BF16 = jnp.bfloat16

S = 2048
D = 1024
DEPTH = 4
N_CHIPS = 4
POOL_WINDOWS = (2, 4, 8, 16)
POOL_W = 512
LRU_W = 1024
LRU_HEADS = 8
HEAD = 128
EVEN_IN = 2560
EVEN_MIX = 1536
MLA_HEADS = 8
QK_NOPE = 128
QK_ROPE = 64
Q_RANK = 384
KV_RANK = 256
ODD_IN = 704
ODD_IN_PAD = 768
QHEAD_PAD = 256
D_FF = 4096
CHUNK = 64
ALPHA = (2 * DEPTH) ** 0.25
LN_EPS = 1e-5
RMS_EPS = 1e-6
ATT_SCALE = (QK_NOPE + QK_ROPE) ** -0.5
ROPE_THETA = 10000.0
LRU_C = 8.0
ADAM_LR = 0.001
ADAM_B1 = 0.9
ADAM_B2 = 0.999
ADAM_EPS = 1e-08
ADAM_WD = 0.01
ADAM_STEP = 10

VMEM_LIMIT = 56 * 1024 * 1024
MESH = pl.DeviceIdType.MESH

_NN = (((1,), (0,)), ((), ()))
_NT = (((1,), (1,)), ((), ()))
_TN = (((0,), (0,)), ((), ()))


def _params(sem=None, **kw):
    return pltpu.CompilerParams(dimension_semantics=sem, vmem_limit_bytes=VMEM_LIMIT, **kw)


def _dot(a, b, dims):
    return lax.dot_general(a.astype(BF16), b.astype(BF16), dims, preferred_element_type=F32)


def _whole(w4):
    return pl.BlockSpec(tuple(w4.shape), lambda i: (0, 0, 0))


def mm_cols(name, a, w4, out_dtype, *, tm=512):
    m, k = a.shape
    nb = w4.shape[2]

    def body(a_ref, w_ref, o_ref):
        a_v = a_ref[...].astype(BF16)
        for s in range(N_CHIPS):
            o_ref[:, s * nb:(s + 1) * nb] = _dot(a_v, w_ref[s], _NN).astype(o_ref.dtype)

    return pl.pallas_call(
        body, name=name, grid=(m // tm,), in_specs=[pl.BlockSpec((tm, k), lambda i: (i, 0)), _whole(w4)],
        out_specs=pl.BlockSpec((tm, N_CHIPS * nb), lambda i: (i, 0)),
        out_shape=jax.ShapeDtypeStruct((m, N_CHIPS * nb), out_dtype), compiler_params=_params(("parallel",)),
    )(a, w4)


def mm_rows(name, a, w4, out_dtype, *, tm=512, ln=None, after=None):
    m = a.shape[0]
    kb, n = w4.shape[1:]
    n_after = 0 if after is None else 1

    def body(a_ref, w_ref, *rest):
        rest = rest[n_after:]
        a_v = a_ref[...].astype(BF16)
        acc = _dot(a_v[:, :kb], w_ref[0], _NN)
        for s in range(1, N_CHIPS):
            acc = acc + _dot(a_v[:, s * kb:(s + 1) * kb], w_ref[s], _NN)
        if ln is None:
            rest[0][...] = acc.astype(rest[0].dtype)
        else:
            x_ref, g_ref, b_ref, y_ref, xhat_ref, rstd_ref = rest
            y_ref[...], xhat_ref[...], rstd_ref[...] = _layer_norm(ALPHA * x_ref[...] + acc, g_ref[...], b_ref[...])

    row = pl.BlockSpec((tm, n), lambda i: (i, 0))
    vec = pl.BlockSpec((1, n), lambda i: (0, 0))
    in_specs = [pl.BlockSpec((tm, N_CHIPS * kb), lambda i: (i, 0)), _whole(w4)] + [pl.BlockSpec(memory_space=pl.ANY)] * n_after
    if ln is None:
        extras, out_specs, out_shape = (), row, jax.ShapeDtypeStruct((m, n), out_dtype)
    else:
        extras, in_specs = ln, in_specs + [row, vec, vec]
        out_specs = [row, row, pl.BlockSpec((tm, 1), lambda i: (i, 0))]
        out_shape = [jax.ShapeDtypeStruct((m, n), F32), jax.ShapeDtypeStruct((m, n), F32), jax.ShapeDtypeStruct((m, 1), F32)]
    return pl.pallas_call(
        body, name=name, grid=(m // tm,), in_specs=in_specs, out_specs=out_specs, out_shape=out_shape,
        compiler_params=_params(("parallel",)),
    )(a, w4, *([after] * n_after), *extras)


def _ln_bwd_rows(dy, xhat, rstd, g):
    dxh = dy * g
    m1 = jnp.mean(dxh, axis=-1, keepdims=True)
    m2 = jnp.mean(dxh * xhat, axis=-1, keepdims=True)
    dz = rstd * (dxh - m1 - xhat * m2)
    return dz, jnp.sum(dy * xhat, axis=0, keepdims=True), jnp.sum(dy, axis=0, keepdims=True)


def linear_bwd(name, a, g, w4, *, col_sharded, da_dtype, resid=None, ln=None, tm=512, after=None):
    m = a.shape[0]
    n_red = m // tm
    ka, ng = w4.shape[1:]
    k_all = a.shape[1]
    n_after = 0 if after is None else 1
    n_resid = 0 if resid is None else 1
    n_ln = 0 if ln is None else 3

    def body(a_ref, g_ref, w_ref, *rest):
        rest = rest[n_after:]
        ln_refs = rest[n_resid:n_resid + n_ln]
        da_ref, dw_ref = rest[n_resid + n_ln:n_resid + n_ln + 2]
        acc_ref = rest[-1]
        k = pl.program_id(0)

        @pl.when(k == 0)
        def _():
            acc_ref[...] = jnp.zeros_like(acc_ref)

        a_v, g_v = a_ref[...].astype(BF16), g_ref[...].astype(BF16)
        extra = ALPHA * rest[0][...] if n_resid else None
        if col_sharded:
            da = extra
            for s in range(N_CHIPS):
                g_s = g_v[:, s * ng:(s + 1) * ng]
                p = _dot(g_s, w_ref[s], _NT)
                da = p if da is None else da + p
                acc_ref[s] += _dot(a_v, g_s, _TN)
        else:
            parts = []
            for s in range(N_CHIPS):
                parts.append(_dot(g_v, w_ref[s], _NT))
                acc_ref[s] += _dot(a_v[:, s * ka:(s + 1) * ka], g_v, _TN)
            da = jnp.concatenate(parts, axis=1)
            da = da if extra is None else da + extra
        if n_ln:
            dg_ref, db_ref = rest[n_resid + n_ln + 2:n_resid + n_ln + 4]
            da, pg, pb = _ln_bwd_rows(da, ln_refs[0][...], ln_refs[1][...], ln_refs[2][...])

            @pl.when(k == 0)
            def _():
                dg_ref[...] = jnp.zeros_like(dg_ref)
                db_ref[...] = jnp.zeros_like(db_ref)

            dg_ref[...] += pg
            db_ref[...] += pb
        da_ref[...] = da.astype(da_ref.dtype)

        @pl.when(k == n_red - 1)
        def _():
            dw_ref[...] = acc_ref[...].astype(dw_ref.dtype)

    row = lambda width: pl.BlockSpec((tm, width), lambda i: (i, 0))
    vec = pl.BlockSpec((1, k_all), lambda i: (0, 0))
    whole = pl.BlockSpec((N_CHIPS, ka, ng), lambda i: (0, 0, 0))
    out_specs = [row(k_all), whole] + [vec, vec] * (n_ln // 3)
    out_shape = [jax.ShapeDtypeStruct((m, k_all), da_dtype), jax.ShapeDtypeStruct((N_CHIPS, ka, ng), BF16)]
    out_shape += [jax.ShapeDtypeStruct((1, k_all), F32)] * (2 * (n_ln // 3))
    return pl.pallas_call(
        body, name=name, grid=(n_red,),
        in_specs=[row(k_all), row(g.shape[1]), whole] + [pl.BlockSpec(memory_space=pl.ANY)] * n_after
        + [row(k_all)] * n_resid + ([row(k_all), row(1), vec] if n_ln else []),
        out_specs=out_specs, out_shape=out_shape,
        scratch_shapes=[pltpu.VMEM((N_CHIPS, ka, ng), F32)], compiler_params=_params(("arbitrary",)),
    )(a, g, w4, *([after] * n_after), *([resid] * n_resid), *(ln or ()))


def _layer_norm(z, g, b):
    mu = jnp.mean(z, axis=-1, keepdims=True)
    zc = z - mu
    rstd = lax.rsqrt(jnp.mean(zc * zc, axis=-1, keepdims=True) + LN_EPS)
    xhat = zc * rstd
    return xhat * g + b, xhat, rstd


def mlp_fwd(name, x, w1, w2, g, b, *, tm=512):
    fb = w1.shape[2]

    def body(x_ref, w1_ref, w2_ref, g_ref, b_ref, y_ref, xhat_ref, rstd_ref, relu_ref, acc_ref):
        s = pl.program_id(1)

        @pl.when(s == 0)
        def _():
            acc_ref[...] = jnp.zeros_like(acc_ref)

        r = jnp.maximum(_dot(x_ref[...], w1_ref[...], _NN), 0.0)
        relu_ref[...] = r.astype(relu_ref.dtype)
        acc_ref[...] += _dot(r * r, w2_ref[...], _NN)

        @pl.when(s == N_CHIPS - 1)
        def _():
            y_ref[...], xhat_ref[...], rstd_ref[...] = _layer_norm(ALPHA * x_ref[...] + acc_ref[...], g_ref[...], b_ref[...])

    row = pl.BlockSpec((tm, D), lambda i, s: (i, 0))
    vec = pl.BlockSpec((1, D), lambda i, s: (0, 0))
    return pl.pallas_call(
        body, name=name, grid=(S // tm, N_CHIPS),
        in_specs=[row, pl.BlockSpec((None, D, fb), lambda i, s: (s, 0, 0)),
                  pl.BlockSpec((None, fb, D), lambda i, s: (s, 0, 0)), vec, vec],
        out_specs=[row, row, pl.BlockSpec((tm, 1), lambda i, s: (i, 0)), pl.BlockSpec((tm, fb), lambda i, s: (i, s))],
        out_shape=[jax.ShapeDtypeStruct((S, D), F32), jax.ShapeDtypeStruct((S, D), F32),
                   jax.ShapeDtypeStruct((S, 1), F32), jax.ShapeDtypeStruct((S, N_CHIPS * fb), BF16)],
        scratch_shapes=[pltpu.VMEM((tm, D), F32)], compiler_params=_params(("parallel", "arbitrary")),
    )(x, w1, w2, g, b)


def mlp_bwd(name, dz, relu, x, w1, w2, ln, *, tm=256, after=None):
    fb = w1.shape[2]
    n_i = S // tm
    n_after = 0 if after is None else 1

    def body(dz_ref, relu_ref, x_ref, w1_ref, w2_ref, xhat_ref, rstd_ref, gln_ref, *rest):
        dx_ref, g1_ref, g2_ref, dg_ref, db_ref, acc1_ref, acc2_ref = rest[n_after:]
        s, i = pl.program_id(0), pl.program_id(1)
        rows = pl.ds(pl.multiple_of(i * tm, tm), tm)
        dz_v = dz_ref[...]

        @pl.when(i == 0)
        def _():
            acc1_ref[...] = jnp.zeros_like(acc1_ref)
            acc2_ref[...] = jnp.zeros_like(acc2_ref)

        @pl.when(s == 0)
        def _():
            dx_ref[rows, :] = ALPHA * dz_v

        @pl.when((s == 0) & (i == 0))
        def _():
            dg_ref[...] = jnp.zeros_like(dg_ref)
            db_ref[...] = jnp.zeros_like(db_ref)

        dz_b = dz_v.astype(BF16)
        r = relu_ref[...]
        dh = (_dot(dz_b, w2_ref[...], _NT) * (2.0 * r.astype(F32))).astype(BF16)
        p2 = _dot(r * r, dz_b, _TN)
        p1 = _dot(x_ref[...], dh, _TN)
        dx_ref[rows, :] += _dot(dh, w1_ref[...], _NT)
        acc1_ref[...] += p1
        acc2_ref[...] += p2

        @pl.when(i == n_i - 1)
        def _():
            g1_ref[...] = acc1_ref[...].astype(g1_ref.dtype)
            g2_ref[...] = acc2_ref[...].astype(g2_ref.dtype)

        @pl.when(s == N_CHIPS - 1)
        def _():
            dzx, pg, pb = _ln_bwd_rows(dx_ref[rows, :], xhat_ref[...], rstd_ref[...], gln_ref[...])
            dx_ref[rows, :] = dzx
            dg_ref[...] += pg
            db_ref[...] += pb

    row = pl.BlockSpec((tm, D), lambda s, i: (i, 0))
    last_only = lambda s, i: (jnp.where(s == N_CHIPS - 1, i, 0), 0)
    vec = pl.BlockSpec((1, D), lambda s, i: (0, 0))
    return pl.pallas_call(
        body, name=name, grid=(N_CHIPS, n_i),
        in_specs=[row, pl.BlockSpec((tm, fb), lambda s, i: (i, s)), row,
                  pl.BlockSpec((None, D, fb), lambda s, i: (s, 0, 0)), pl.BlockSpec((None, fb, D), lambda s, i: (s, 0, 0)),
                  pl.BlockSpec((tm, D), last_only), pl.BlockSpec((tm, 1), last_only), vec]
        + [pl.BlockSpec(memory_space=pl.ANY)] * n_after,
        out_specs=[pl.BlockSpec((S, D), lambda s, i: (0, 0)), pl.BlockSpec((None, D, fb), lambda s, i: (s, 0, 0)),
                   pl.BlockSpec((None, fb, D), lambda s, i: (s, 0, 0)), vec, vec],
        out_shape=[jax.ShapeDtypeStruct((S, D), F32), jax.ShapeDtypeStruct((N_CHIPS, D, fb), BF16),
                   jax.ShapeDtypeStruct((N_CHIPS, fb, D), BF16), jax.ShapeDtypeStruct((1, D), F32),
                   jax.ShapeDtypeStruct((1, D), F32)],
        scratch_shapes=[pltpu.VMEM((D, fb), F32), pltpu.VMEM((fb, D), F32)],
        compiler_params=_params(("arbitrary", "arbitrary")),
    )(dz, relu, x, w1, w2, *ln, *([after] * n_after))


def loss_head(y, target, xhat, rstd, g, *, tm=256):
    def body(y_ref, t_ref, xhat_ref, rstd_ref, g_ref, dz_ref, dg_ref, db_ref, loss_ref):
        i = pl.program_id(0)

        @pl.when(i == 0)
        def _():
            loss_ref[...] = jnp.zeros_like(loss_ref)
            dg_ref[...] = jnp.zeros_like(dg_ref)
            db_ref[...] = jnp.zeros_like(db_ref)

        e = y_ref[...] - t_ref[...]
        part = jnp.sum(jnp.sum(e * e, axis=-1, keepdims=True), axis=0, keepdims=True) * (0.5 / D)
        loss_ref[...] += jnp.broadcast_to(part, loss_ref.shape)
        dz_ref[...], pg, pb = _ln_bwd_rows(e * (1.0 / D), xhat_ref[...], rstd_ref[...], g_ref[...])
        dg_ref[...] += pg
        db_ref[...] += pb

    row = pl.BlockSpec((tm, D), lambda i: (i, 0))
    vec = pl.BlockSpec((1, D), lambda i: (0, 0))
    return pl.pallas_call(
        body, name="loss_head", grid=(S // tm,),
        in_specs=[row, row, row, pl.BlockSpec((tm, 1), lambda i: (i, 0)), vec],
        out_specs=[row, vec, vec, pl.BlockSpec((8, 128), lambda i: (0, 0))],
        out_shape=[jax.ShapeDtypeStruct((S, D), F32), jax.ShapeDtypeStruct((1, D), F32),
                   jax.ShapeDtypeStruct((1, D), F32), jax.ShapeDtypeStruct((8, 128), F32)],
        compiler_params=_params(("arbitrary",)),
    )(y, target, xhat, rstd, g)


def _rows(shape):
    return lax.broadcasted_iota(jnp.int32, shape, 0)


def _shift_down(x, k):
    return jnp.where(_rows(x.shape) >= k, pltpu.roll(x, k, 0), 0.0)


def _shift_up(x, k):
    n = x.shape[0]
    return jnp.where(_rows(x.shape) < n - k, pltpu.roll(x, n - k, 0), 0.0)


def _pool_diff(u, w):
    acc, k = u, 1
    while k < w:
        acc = acc + _shift_down(acc, k)
        k *= 2
    cnt = jnp.minimum(_rows(u.shape) + 1, w).astype(F32)
    return acc / cnt - u, cnt


def pool_fwd(name, proj, pool_w, pool_scale):
    def body(u_ref, w_ref, sc_ref, y_ref):
        for g, w in enumerate(POOL_WINDOWS):
            cols = slice(g * HEAD, (g + 1) * HEAD)
            d, _ = _pool_diff(u_ref[:, cols], w)
            z = _dot(d, w_ref[g], _NN)
            y_ref[:, cols] = (z * sc_ref[:, cols]).astype(y_ref.dtype)

    return pl.pallas_call(
        body, name=name, grid=(1,),
        in_specs=[pl.BlockSpec((S, POOL_W), lambda i: (0, 0)),
                  pl.BlockSpec((4, HEAD, HEAD), lambda i: (0, 0, 0)),
                  pl.BlockSpec((1, POOL_W), lambda i: (0, 0))],
        out_specs=pl.BlockSpec((S, POOL_W), lambda i: (0, 0)),
        out_shape=jax.ShapeDtypeStruct((S, POOL_W), BF16),
        compiler_params=_params(("arbitrary",)),
    )(proj, pool_w, pool_scale)


def pool_bwd(name, proj, dycat, pool_w, pool_scale):
    def body(u_ref, dy_ref, w_ref, sc_ref, du_ref, dw_ref, dsc_ref):
        for g, w in enumerate(POOL_WINDOWS):
            cols = slice(g * HEAD, (g + 1) * HEAD)
            d, cnt = _pool_diff(u_ref[:, cols], w)
            dy = dy_ref[:, cols]
            z = _dot(d, w_ref[g], _NN)
            dsc_ref[:, cols] = jnp.sum(dy * z, axis=0, keepdims=True)
            dz = dy * sc_ref[:, cols]
            dw_ref[g] = _dot(d, dz, _TN)
            dd = _dot(dz, w_ref[g], _NT)
            acc, k = dd / cnt, 1
            while k < w:
                acc = acc + _shift_up(acc, k)
                k *= 2
            du_ref[:, cols] = (acc - dd).astype(du_ref.dtype)

    return pl.pallas_call(
        body, name=name, grid=(1,),
        in_specs=[pl.BlockSpec((S, POOL_W), lambda i: (0, 0)),
                  pl.BlockSpec((S, POOL_W), lambda i: (0, 0)),
                  pl.BlockSpec((4, HEAD, HEAD), lambda i: (0, 0, 0)),
                  pl.BlockSpec((1, POOL_W), lambda i: (0, 0))],
        out_specs=[pl.BlockSpec((S, POOL_W), lambda i: (0, 0)),
                   pl.BlockSpec((4, HEAD, HEAD), lambda i: (0, 0, 0)),
                   pl.BlockSpec((1, POOL_W), lambda i: (0, 0))],
        out_shape=[jax.ShapeDtypeStruct((S, POOL_W), BF16), jax.ShapeDtypeStruct((4, HEAD, HEAD), F32),
                   jax.ShapeDtypeStruct((1, POOL_W), F32)],
        compiler_params=_params(("arbitrary",)),
    )(proj, dycat, pool_w, pool_scale)


def _expm1(x):
    series = x * (1.0 + x * (0.5 + x * (1.0 / 6.0 + x * (1.0 / 24.0 + x * (1.0 / 120.0)))))
    return jnp.where(jnp.abs(x) < 0.05, series, jnp.exp(x) - 1.0)


def _softplus_neg(lam):
    e = jnp.exp(-jnp.abs(lam))
    log1p = jnp.where(e < 0.01, e * (1.0 - e * (0.5 - e * (1.0 / 3.0))), jnp.log(1.0 + e))
    return jnp.maximum(-lam, 0.0) + log1p


_GELU_C = math.sqrt(2.0 / math.pi)


def _gelu(x):
    t = jnp.tanh(_GELU_C * (x + 0.044715 * x * x * x))
    return 0.5 * x * (1.0 + t), t


def _gelu_grad(x, t):
    return 0.5 * (1.0 + t) + 0.5 * x * (1.0 - t * t) * _GELU_C * (1.0 + 3.0 * 0.044715 * x * x)


def _conv(u, cw, cb):
    return cw[3:4] * u + cw[2:3] * _shift_down(u, 1) + cw[1:2] * _shift_down(u, 2) + cw[0:1] * _shift_down(u, 3) + cb


def _lru_gates(cu, wa, ba, wx, bx, lam):
    r = jax.nn.sigmoid(_dot(cu, wa, _NN) + ba)
    i = jax.nn.sigmoid(_dot(cu, wx, _NN) + bx)
    sp = _softplus_neg(lam)
    log_a = (-LRU_C) * r * sp
    a = jnp.exp(log_a)
    mult = jnp.sqrt(-_expm1(2.0 * log_a))
    return r, i, sp, a, mult


def _scan(a_ref, b_ref, h_ref, *, reverse):
    n_blk = S // 8
    row8 = lax.broadcasted_iota(jnp.int32, (8, HEAD), 0)

    def step(j, carry):
        blk = (n_blk - 1 - j) if reverse else j
        r0 = pl.multiple_of(blk * 8, 8)
        a = a_ref[pl.ds(r0, 8), :]
        b = b_ref[pl.ds(r0, 8), :]
        for k in (1, 2, 4):
            if reverse:
                keep = row8 < 8 - k
                a_s, b_s = pltpu.roll(a, 8 - k, 0), pltpu.roll(b, 8 - k, 0)
            else:
                keep = row8 >= k
                a_s, b_s = pltpu.roll(a, k, 0), pltpu.roll(b, k, 0)
            b = jnp.where(keep, a * b_s + b, b)
            a = jnp.where(keep, a * a_s, a)
        h = b + a * carry
        h_ref[pl.ds(r0, 8), :] = h
        edge = h[0:1, :] if reverse else h[7:8, :]
        return jnp.broadcast_to(edge, (8, HEAD))

    lax.fori_loop(0, n_blk, step, jnp.zeros((8, HEAD), F32), unroll=4)


def _lru_specs():
    def col(off):
        return pl.BlockSpec((S, HEAD), lambda h: (0, off + h))
    vec = pl.BlockSpec((1, HEAD), lambda h: (0, h))
    mat = pl.BlockSpec((None, HEAD, HEAD), lambda h: (h, 0, 0))
    cw = pl.BlockSpec((4, HEAD), lambda h: (0, h))
    return col, vec, mat, cw


def lru_fwd(name, proj, conv_w, conv_b, w_a, b_a, w_x, b_x, lam):
    def body(u_ref, ug_ref, cw_ref, cb_ref, wa_ref, ba_ref, wx_ref, bx_ref, lam_ref, y_ref, h_ref, a_s, b_s):
        cu = _conv(u_ref[...], cw_ref[...], cb_ref[...])
        _, i, _, a, mult = _lru_gates(cu, wa_ref[...], ba_ref[...], wx_ref[...], bx_ref[...], lam_ref[...])
        a_s[...] = a
        b_s[...] = mult * (i * cu)
        _scan(a_s, b_s, h_ref, reverse=False)
        gl, _ = _gelu(ug_ref[...])
        y_ref[...] = (h_ref[...] * gl).astype(y_ref.dtype)

    col, vec, mat, cw = _lru_specs()
    out = pl.BlockSpec((S, HEAD), lambda h: (0, h))
    return pl.pallas_call(
        body, name=name, grid=(LRU_HEADS,),
        in_specs=[col(4), col(12), cw, vec, mat, vec, mat, vec, vec],
        out_specs=[out, out],
        out_shape=[jax.ShapeDtypeStruct((S, LRU_W), BF16), jax.ShapeDtypeStruct((S, LRU_W), F32)],
        scratch_shapes=[pltpu.VMEM((S, HEAD), F32), pltpu.VMEM((S, HEAD), F32)],
        compiler_params=_params(("parallel",)),
    )(proj, proj, conv_w, conv_b, w_a, b_a, w_x, b_x, lam)


def lru_bwd(name, proj, hstate, dycat, conv_w, conv_b, w_a, b_a, w_x, b_x, lam):
    def body(u_ref, ug_ref, h_ref, dy_ref, cw_ref, cb_ref, wa_ref, ba_ref, wx_ref, bx_ref, lam_ref,
             du_ref, dug_ref, dwa_ref, dwx_ref, dba_ref, dbx_ref, dlam_ref, dcw_ref, dcb_ref, a_s, b_s, g_s):
        u = u_ref[...]
        cw = cw_ref[...]
        cu = _conv(u, cw, cb_ref[...])
        lam_v = lam_ref[...]
        r, i, sp, a, mult = _lru_gates(cu, wa_ref[...], ba_ref[...], wx_ref[...], bx_ref[...], lam_v)
        ug = ug_ref[...]
        gl, t = _gelu(ug)
        dy = dy_ref[...]
        h = h_ref[...]
        dug_ref[...] = (dy * h * _gelu_grad(ug, t)).astype(dug_ref.dtype)
        a_s[...] = _shift_up(a, 1)
        b_s[...] = dy * gl
        _scan(a_s, b_s, g_s, reverse=True)
        dxin = g_s[...]
        da = dxin * _shift_down(h, 1)
        dmult = dxin * (i * cu)
        di = dxin * (mult * cu)
        dlog_a = da * a - dmult * (a * a) / mult
        dr_pre = dlog_a * ((-LRU_C) * sp) * (r * (1.0 - r))
        di_pre = di * (i * (1.0 - i))
        dsp = jnp.sum(dlog_a * ((-LRU_C) * r), axis=0, keepdims=True)
        dlam_ref[...] = dsp * (-jax.nn.sigmoid(-lam_v))
        dba_ref[...] = jnp.sum(dr_pre, axis=0, keepdims=True)
        dbx_ref[...] = jnp.sum(di_pre, axis=0, keepdims=True)
        dwa_ref[...] = _dot(cu, dr_pre, _TN)
        dwx_ref[...] = _dot(cu, di_pre, _TN)
        dcu = dxin * (mult * i) + _dot(dr_pre, wa_ref[...], _NT) + _dot(di_pre, wx_ref[...], _NT)
        dcb_ref[...] = jnp.sum(dcu, axis=0, keepdims=True)
        for k in range(4):
            dcw_ref[k:k + 1, :] = jnp.sum(dcu * (_shift_down(u, 3 - k) if k < 3 else u), axis=0, keepdims=True)
        du = cw[3:4] * dcu + cw[2:3] * _shift_up(dcu, 1) + cw[1:2] * _shift_up(dcu, 2) + cw[0:1] * _shift_up(dcu, 3)
        du_ref[...] = du.astype(du_ref.dtype)

    col, vec, mat, cw = _lru_specs()
    out = pl.BlockSpec((S, HEAD), lambda h: (0, h))
    big = jax.ShapeDtypeStruct((S, LRU_W), BF16)
    vec_shape = jax.ShapeDtypeStruct((1, LRU_W), F32)
    mat_shape = jax.ShapeDtypeStruct((LRU_HEADS, HEAD, HEAD), F32)
    return pl.pallas_call(
        body, name=name, grid=(LRU_HEADS,),
        in_specs=[col(4), col(12), out, col(4), cw, vec, mat, vec, mat, vec, vec],
        out_specs=[out, out, mat, mat, vec, vec, vec, cw, vec],
        out_shape=[big, big, mat_shape, mat_shape, vec_shape, vec_shape, vec_shape,
                   jax.ShapeDtypeStruct((4, LRU_W), F32), vec_shape],
        scratch_shapes=[pltpu.VMEM((S, HEAD), F32)] * 3,
        compiler_params=_params(("parallel",)),
    )(proj, proj, hstate, dycat, conv_w, conv_b, w_a, b_a, w_x, b_x, lam)


def rope_tables(pos_col, inv_freq):
    def body(pos_ref, f_ref, c_ref, s1_ref, s2_ref):
        ang = pos_ref[...].astype(F32) * f_ref[...]
        lane = lax.broadcasted_iota(jnp.int32, ang.shape, 1)
        cos, sin = jnp.cos(ang), jnp.sin(ang)
        c_ref[...] = jnp.where(lane < QK_ROPE, cos, 0.0)
        s1_ref[...] = jnp.where(lane < QK_ROPE // 2, -sin, 0.0)
        s2_ref[...] = jnp.where((lane >= QK_ROPE // 2) & (lane < QK_ROPE), sin, 0.0)

    tab = jax.ShapeDtypeStruct((S, HEAD), F32)
    return pl.pallas_call(
        body, name="rope_tables", grid=(1,),
        in_specs=[pl.BlockSpec((S, 1), lambda i: (0, 0)), pl.BlockSpec((1, HEAD), lambda i: (0, 0))],
        out_specs=[pl.BlockSpec((S, HEAD), lambda i: (0, 0))] * 3, out_shape=[tab, tab, tab],
        compiler_params=_params(("arbitrary",)),
    )(pos_col, inv_freq)


def _rope(v, c, s1, s2):
    return v * c + pltpu.roll(v, HEAD - QK_ROPE // 2, 1) * s1 + pltpu.roll(v, QK_ROPE // 2, 1) * s2


def _unrope(d, c, s1, s2):
    return d * c + pltpu.roll(d * s1, QK_ROPE // 2, 1) + pltpu.roll(d * s2, HEAD - QK_ROPE // 2, 1)


def _rms(x, g):
    rstd = lax.rsqrt(jnp.mean(x * x, axis=-1, keepdims=True) + RMS_EPS)
    return x * rstd, rstd


def mla_prep(name, down, gq, gkv, tabs, *, tm=256):
    def body(dn_ref, gq_ref, gkv_ref, c_ref, s1_ref, s2_ref, cq_ref, ckv_ref, kp_ref):
        xq, _ = _rms(dn_ref[:, :Q_RANK], None)
        cq_ref[...] = (xq * gq_ref[...]).astype(cq_ref.dtype)
        xkv, _ = _rms(dn_ref[:, Q_RANK:Q_RANK + KV_RANK], None)
        ckv_ref[...] = (xkv * gkv_ref[...]).astype(ckv_ref.dtype)
        kp = _rope(dn_ref[:, Q_RANK + KV_RANK:], c_ref[...], s1_ref[...], s2_ref[...])
        kp_ref[...] = kp.astype(kp_ref.dtype)

    tab = pl.BlockSpec((tm, HEAD), lambda i: (i, 0))
    return pl.pallas_call(
        body, name=name, grid=(S // tm,),
        in_specs=[pl.BlockSpec((tm, ODD_IN_PAD), lambda i: (i, 0)), pl.BlockSpec((1, Q_RANK), lambda i: (0, 0)),
                  pl.BlockSpec((1, KV_RANK), lambda i: (0, 0)), tab, tab, tab],
        out_specs=[pl.BlockSpec((tm, Q_RANK), lambda i: (i, 0)), pl.BlockSpec((tm, KV_RANK), lambda i: (i, 0)), tab],
        out_shape=[jax.ShapeDtypeStruct((S, Q_RANK), BF16), jax.ShapeDtypeStruct((S, KV_RANK), BF16),
                   jax.ShapeDtypeStruct((S, HEAD), BF16)],
        compiler_params=_params(("parallel",)),
    )(down, gq, gkv, *tabs)


def mla_prep_bwd(name, down, dcq, dckv, dkp, gq, gkv, tabs, *, tm=256):
    def body(dn_ref, dcq_ref, dckv_ref, dkp_ref, gq_ref, gkv_ref, c_ref, s1_ref, s2_ref, dd_ref, dgq_ref, dgkv_ref):
        i = pl.program_id(0)

        def rms_bwd(x, dy, g, dg_ref):
            xh, rstd = _rms(x, None)
            dxh = dy * g
            dx = rstd * (dxh - xh * jnp.mean(dxh * xh, axis=-1, keepdims=True))
            pg = jnp.sum(dy * xh, axis=0, keepdims=True)

            @pl.when(i == 0)
            def _():
                dg_ref[...] = pg

            @pl.when(i > 0)
            def _():
                dg_ref[...] += pg

            return dx

        dxq = rms_bwd(dn_ref[:, :Q_RANK], dcq_ref[...], gq_ref[...], dgq_ref)
        dd_ref[:, :Q_RANK] = dxq.astype(dd_ref.dtype)
        dxkv = rms_bwd(dn_ref[:, Q_RANK:Q_RANK + KV_RANK], dckv_ref[...], gkv_ref[...], dgkv_ref)
        dd_ref[:, Q_RANK:Q_RANK + KV_RANK] = dxkv.astype(dd_ref.dtype)
        dd_ref[:, Q_RANK + KV_RANK:] = _unrope(dkp_ref[...], c_ref[...], s1_ref[...], s2_ref[...]).astype(dd_ref.dtype)

    tab = pl.BlockSpec((tm, HEAD), lambda i: (i, 0))
    vq = pl.BlockSpec((1, Q_RANK), lambda i: (0, 0))
    vkv = pl.BlockSpec((1, KV_RANK), lambda i: (0, 0))
    return pl.pallas_call(
        body, name=name, grid=(S // tm,),
        in_specs=[pl.BlockSpec((tm, ODD_IN_PAD), lambda i: (i, 0)), pl.BlockSpec((tm, Q_RANK), lambda i: (i, 0)),
                  pl.BlockSpec((tm, KV_RANK), lambda i: (i, 0)), tab, vq, vkv, tab, tab, tab],
        out_specs=[pl.BlockSpec((tm, ODD_IN_PAD), lambda i: (i, 0)), vq, vkv],
        out_shape=[jax.ShapeDtypeStruct((S, ODD_IN_PAD), BF16), jax.ShapeDtypeStruct((1, Q_RANK), F32),
                   jax.ShapeDtypeStruct((1, KV_RANK), F32)],
        compiler_params=_params(("arbitrary",)),
    )(down, dcq, dckv, dkp, gq, gkv, *tabs)


ATT_TQ = 256


def _attn_probs(q_ref, kv_ref, kp_ref, c_ref, s1_ref, s2_ref, i, nk):
    qn = q_ref[:, :HEAD].astype(BF16)
    qp = _rope(q_ref[:, HEAD:], c_ref[...], s1_ref[...], s2_ref[...]).astype(BF16)
    kn = kv_ref[:nk, :HEAD]
    sc = (_dot(qn, kn, _NT) + _dot(qp, kp_ref[:nk, :], _NT)) * ATT_SCALE
    q_chunk = (i * ATT_TQ + lax.broadcasted_iota(jnp.int32, sc.shape, 0)) // CHUNK
    k_chunk = lax.broadcasted_iota(jnp.int32, sc.shape, 1) // CHUNK
    sc = jnp.where(k_chunk <= q_chunk, sc, jnp.finfo(F32).min)
    e = jnp.exp(sc - jnp.max(sc, axis=-1, keepdims=True))
    p = e * (1.0 / jnp.sum(e, axis=-1, keepdims=True))
    return p, qn, qp, kn


def _for_each_prefix(i, fn):
    for k in range(S // ATT_TQ):
        pl.when(i == k)(functools.partial(fn, (k + 1) * ATT_TQ))


def _attn_specs():
    q = pl.BlockSpec((ATT_TQ, QHEAD_PAD), lambda h, i: (i, h))
    kv = pl.BlockSpec((S, QHEAD_PAD), lambda h, i: (0, h))
    kp = pl.BlockSpec((S, HEAD), lambda h, i: (0, 0))
    tab = pl.BlockSpec((ATT_TQ, HEAD), lambda h, i: (i, 0))
    o = pl.BlockSpec((ATT_TQ, HEAD), lambda h, i: (i, h))
    return q, kv, kp, tab, o


def attn_fwd(name, q, kv, kp, tabs):
    def body(q_ref, kv_ref, kp_ref, c_ref, s1_ref, s2_ref, o_ref):
        i = pl.program_id(1)

        def run(nk):
            p, _, _, _ = _attn_probs(q_ref, kv_ref, kp_ref, c_ref, s1_ref, s2_ref, i, nk)
            o_ref[...] = _dot(p, kv_ref[:nk, HEAD:], _NN).astype(o_ref.dtype)

        _for_each_prefix(i, run)

    qs, kvs, kps, tab, os = _attn_specs()
    return pl.pallas_call(
        body, name=name, grid=(MLA_HEADS, S // ATT_TQ), in_specs=[qs, kvs, kps, tab, tab, tab], out_specs=os,
        out_shape=jax.ShapeDtypeStruct((S, MLA_HEADS * HEAD), BF16),
        compiler_params=_params(("parallel", "parallel")),
    )(q, kv, kp, *tabs)


def attn_bwd(name, q, kv, kp, do, tabs):
    def body(q_ref, kv_ref, kp_ref, do_ref, c_ref, s1_ref, s2_ref, dq_ref, dkv_ref, dkp_ref):
        h, i = pl.program_id(0), pl.program_id(1)

        @pl.when(i == 0)
        def _():
            dkv_ref[...] = jnp.zeros_like(dkv_ref)

        @pl.when((i == 0) & (h == 0))
        def _():
            dkp_ref[...] = jnp.zeros_like(dkp_ref)

        def run(nk):
            p, qn, qp, kn = _attn_probs(q_ref, kv_ref, kp_ref, c_ref, s1_ref, s2_ref, i, nk)
            do_v = do_ref[...]
            dp = _dot(do_v, kv_ref[:nk, HEAD:], _NT)
            ds = (p * (dp - jnp.sum(p * dp, axis=-1, keepdims=True)) * ATT_SCALE).astype(BF16)
            dq_ref[:, :HEAD] = _dot(ds, kn, _NN).astype(dq_ref.dtype)
            dqp = _unrope(_dot(ds, kp_ref[:nk, :], _NN), c_ref[...], s1_ref[...], s2_ref[...])
            dq_ref[:, HEAD:] = dqp.astype(dq_ref.dtype)
            dkv_ref[:nk, :HEAD] += _dot(ds, qn, _TN)
            dkv_ref[:nk, HEAD:] += _dot(p, do_v, _TN)
            dkp_ref[:nk, :] += _dot(ds, qp, _TN)

        _for_each_prefix(i, run)

    qs, kvs, kps, tab, os = _attn_specs()
    return pl.pallas_call(
        body, name=name, grid=(MLA_HEADS, S // ATT_TQ), in_specs=[qs, kvs, kps, os, tab, tab, tab],
        out_specs=[qs, kvs, kps],
        out_shape=[jax.ShapeDtypeStruct((S, MLA_HEADS * QHEAD_PAD), BF16),
                   jax.ShapeDtypeStruct((S, MLA_HEADS * QHEAD_PAD), F32), jax.ShapeDtypeStruct((S, HEAD), F32)],
        compiler_params=_params(("arbitrary", "arbitrary")),
    )(q, kv, kp, do, *tabs)


def adamw(name, w, g, m, v):
    rows, cols = w.shape
    tr = rows
    for cand in (512, 256, 128, 64, 32, 16, 8):
        if rows % cand == 0 and cand * cols * 4 <= 2 * 1024 * 1024:
            tr = cand
            break

    def body(w_ref, g_ref, m_ref, v_ref, d_ref, nm_ref, nv_ref):
        g_v = g_ref[...]
        nm = ADAM_B1 * m_ref[...] + (1.0 - ADAM_B1) * g_v
        nv = ADAM_B2 * v_ref[...] + (1.0 - ADAM_B2) * (g_v * g_v)
        m_hat = nm / (1.0 - ADAM_B1 ** ADAM_STEP)
        v_hat = nv / (1.0 - ADAM_B2 ** ADAM_STEP)
        d_ref[...] = (-ADAM_LR) * (m_hat / (jnp.sqrt(v_hat) + ADAM_EPS) + ADAM_WD * w_ref[...])
        nm_ref[...] = nm
        nv_ref[...] = nv

    blk = pl.BlockSpec((tr, cols), lambda i: (i, 0))
    shape = jax.ShapeDtypeStruct((rows, cols), F32)
    return pl.pallas_call(
        body, name=name, grid=(rows // tr,), in_specs=[blk] * 4, out_specs=[blk] * 3, out_shape=[shape] * 3,
        compiler_params=_params(("parallel",)),
    )(w, g, m, v)


def _local_step(x, pos_col, target, sm, weights_of, emit_grads, prefetch):
    inv_freq = ROPE_THETA ** (-jnp.arange(0, QK_ROPE, 2, dtype=F32) / QK_ROPE)
    inv_freq = jnp.concatenate([inv_freq, inv_freq, jnp.zeros((HEAD - QK_ROPE,), F32)])[None, :]
    tabs = rope_tables(pos_col, inv_freq)
    saved, wts = [], {}
    for layer in range(DEPTH):
        j = layer // 2
        n = "l%d_" % layer
        sv = {"x": x}
        wm = wts["mix%d" % layer] = weights_of("mix%d" % layer, x)
        if layer == 0:
            sm = dict(sm, conv_w=wm["conv_w"], gq=wm["gq"], gkv=wm["gkv"])
        if layer % 2 == 0:
            proj = mm_cols(n + "proj", x, wm["w_in"], F32)
            fetched = prefetch("mlp%d" % layer, proj)
            y_pool = pool_fwd(n + "pool", proj, sm["pool_w"][j], sm["pool_scale"][j][None])
            y_lru, hstate = lru_fwd(n + "lru", proj, sm["conv_w"][j], sm["conv_b"][j][None], sm["w_a"][j],
                                    sm["b_a"][j][None], sm["w_x"][j], sm["b_x"][j][None], sm["lam"][j][None])
            ycat = jnp.concatenate([y_pool, y_lru], axis=1)
            mix_in, w_mix = ycat, wm["w_out"]
            sv.update(proj=proj, hstate=hstate, ycat=ycat)
        else:
            down = mm_rows(n + "down", x, wm["w_down"], F32)
            fetched = prefetch("mlp%d" % layer, down)
            cq, ckv, kp = mla_prep(n + "prep", down, sm["gq"][j][None], sm["gkv"][j][None], tabs)
            q = mm_cols(n + "q", cq, wm["w_qb"], F32)
            kv = mm_cols(n + "kv", ckv, wm["w_kvb"], BF16)
            o = attn_fwd(n + "attn", q, kv, kp, tabs)
            mix_in, w_mix = o, wm["w_o"]
            sv.update(down=down, cq=cq, ckv=ckv, kp=kp, q=q, kv=kv, o=o)
        x1, xhat1, rstd1 = mm_rows(n + "mixout", mix_in, w_mix, F32, after=fetched,
                                   ln=(x, sm["ln_mix_g"][layer][None], sm["ln_mix_b"][layer][None]))
        wf = wts["mlp%d" % layer] = weights_of("mlp%d" % layer, x1)
        x2, xhat2, rstd2, relu = mlp_fwd(n + "mlp", x1, wf["w1"], wf["w2"], sm["ln_ffn_g"][layer][None],
                                         sm["ln_ffn_b"][layer][None])
        sv.update(xhat1=xhat1, rstd1=rstd1, x1=x1, relu=relu, xhat2=xhat2, rstd2=rstd2)
        saved.append(sv)
        x = x2

    gs = {k: [None] * (DEPTH if k.startswith("ln_") else DEPTH // 2) for k in sm}
    last = saved[DEPTH - 1]
    dz, gs["ln_ffn_g"][DEPTH - 1], gs["ln_ffn_b"][DEPTH - 1], loss_tile = loss_head(
        x, target, last["xhat2"], last["rstd2"], sm["ln_ffn_g"][DEPTH - 1][None])
    sent = None
    for layer in reversed(range(DEPTH)):
        j = layer // 2
        n = "l%d_" % layer
        sv = saved[layer]
        wm, wf = wts["mix%d" % layer], wts["mlp%d" % layer]
        dz, g_w1, g_w2, gs["ln_mix_g"][layer], gs["ln_mix_b"][layer] = mlp_bwd(
            n + "mlp_b", dz, sv["relu"], sv["x1"], wf["w1"], wf["w2"],
            (sv["xhat1"], sv["rstd1"], sm["ln_mix_g"][layer][None]), after=sent)
        sent = emit_grads("mlp%d" % layer, {"w1": g_w1, "w2": g_w2})
        below = saved[layer - 1] if layer else None
        ln_below = (below["xhat2"], below["rstd2"], sm["ln_ffn_g"][layer - 1][None]) if layer else None
        if layer % 2 == 0:
            dycat, g_out = linear_bwd(n + "mixout_b", sv["ycat"], dz, wm["w_out"], col_sharded=False, da_dtype=F32,
                                      after=sent)
            du_pool, gs["pool_w"][j], gs["pool_scale"][j] = pool_bwd(
                n + "pool_b", sv["proj"], dycat, sm["pool_w"][j], sm["pool_scale"][j][None])
            (du_lru, du_gate, gs["w_a"][j], gs["w_x"][j], gs["b_a"][j], gs["b_x"][j], gs["lam"][j], gs["conv_w"][j],
             gs["conv_b"][j]) = lru_bwd(n + "lru_b", sv["proj"], sv["hstate"], dycat, sm["conv_w"][j],
                                        sm["conv_b"][j][None], sm["w_a"][j], sm["b_a"][j][None], sm["w_x"][j],
                                        sm["b_x"][j][None], sm["lam"][j][None])
            dproj = jnp.concatenate([du_pool, du_lru, du_gate], axis=1)
            res = linear_bwd(n + "proj_b", sv["x"], dproj, wm["w_in"], col_sharded=True, da_dtype=F32, resid=dz,
                             ln=ln_below)
            grads = {"w_in": res[1], "w_out": g_out}
        else:
            do, g_o = linear_bwd(n + "attnout_b", sv["o"], dz, wm["w_o"], col_sharded=False, da_dtype=BF16, after=sent)
            dq, dkv, dkp = attn_bwd(n + "attn_b", sv["q"], sv["kv"], sv["kp"], do, tabs)
            dcq, g_qb = linear_bwd(n + "q_b", sv["cq"], dq, wm["w_qb"], col_sharded=True, da_dtype=F32)
            dckv, g_kvb = linear_bwd(n + "kv_b", sv["ckv"], dkv, wm["w_kvb"], col_sharded=True, da_dtype=F32)
            ddown, gs["gq"][j], gs["gkv"][j] = mla_prep_bwd(
                n + "prep_b", sv["down"], dcq, dckv, dkp, sm["gq"][j][None], sm["gkv"][j][None], tabs)
            res = linear_bwd(n + "down_b", sv["x"], ddown, wm["w_down"], col_sharded=False, da_dtype=F32, resid=dz,
                             ln=ln_below)
            grads = {"w_down": res[1], "w_qb": g_qb, "w_kvb": g_kvb, "w_o": g_o}
        dz = res[0]
        if layer:
            gs["ln_ffn_g"][layer - 1], gs["ln_ffn_b"][layer - 1] = res[2], res[3]
        sent = emit_grads("mix%d" % layer, grads)
    gs = {k: jnp.stack([a.reshape(sm[k].shape[1:]) for a in v]) for k, v in gs.items()}
    return loss_tile[0, 0], dz, gs, sent


def _place():
    x, y, c = lax.axis_index("x"), lax.axis_index("y"), lax.axis_index("c")
    chips = [(1 - x, y), (x, 1 - y), (1 - x, 1 - y)]
    return x, y, c, chips


def _hbm_call(body, name, args, out_shape, scratch, aliases=None):
    return pl.pallas_call(
        body, name=name, in_specs=[pl.BlockSpec(memory_space=pl.ANY)] * len(args),
        out_specs=[pl.BlockSpec(memory_space=pl.ANY)] * len(out_shape), out_shape=out_shape,
        scratch_shapes=scratch, input_output_aliases=aliases or {},
        compiler_params=pltpu.CompilerParams(has_side_effects=True),
    )(*args)


HBM_SPEC = pl.BlockSpec(memory_space=pltpu.HBM)
SEM_SPEC = pl.BlockSpec(memory_space=pltpu.SEMAPHORE)
EFFECT = pltpu.SideEffectType.DATAFLOW_SIDE_EFFECTING


def _remote(src, dst, send_sem, recv_sem, device):
    return pltpu.make_async_remote_copy(src_ref=src, dst_ref=dst, send_sem=send_sem, recv_sem=recv_sem,
                                        device_id=device, device_id_type=MESH)


def place_own(name, srcs, place, steps=4):
    n = len(srcs)
    in_specs, out_specs, out_shape = [], [], []
    for s in srcs:
        rows, cols = s.shape
        tr = rows // steps
        in_specs.append(pl.BlockSpec((tr, cols), lambda i, p: (i, 0)))
        out_specs.append(pl.BlockSpec((None, tr, cols), lambda i, p: (p[0], i, 0)))
        out_shape.append(jax.ShapeDtypeStruct((N_CHIPS, rows, cols), s.dtype))

    def body(p_ref, *refs):
        for i_ref, o_ref in zip(refs[:n], refs[n:]):
            o_ref[...] = i_ref[...]

    return pl.pallas_call(
        body, name=name, out_shape=out_shape,
        grid_spec=pltpu.PrefetchScalarGridSpec(num_scalar_prefetch=1, grid=(steps,), in_specs=in_specs,
                                               out_specs=out_specs),
        compiler_params=_params(("parallel",)),
    )(place, *srcs)


def split_start(name, groups, n_sems, plan, token_in=None):
    sizes = [len(srcs) for srcs, _ in groups]
    n, n_groups = sum(sizes), len(groups)
    srcs = [pltpu.with_memory_space_constraint(a, pltpu.HBM) for s, _ in groups for a in s]
    lands = [pltpu.with_memory_space_constraint(a, pltpu.HBM) for _, l in groups for a in l]
    extra = [] if token_in is None else [token_in]

    def body(*refs):
        src_refs, land_refs = refs[:n], refs[n:2 * n]
        outs = refs[2 * n + len(extra):]
        off = 0
        for g, size in enumerate(sizes):
            sends, _ = plan(src_refs[off:off + size], land_refs[off:off + size], outs[2 * g], outs[2 * g + 1])
            for cp in sends:
                cp.start()
            off += size
        token = outs[2 * n_groups + 2 * n]
        token[...] = jnp.zeros_like(token)

    sems = [pltpu.SemaphoreType.DMA((size * n_sems,)) for size in sizes for _ in range(2)]
    res = pl.pallas_call(
        body, name=name,
        out_shape=(*sems, *[pltpu.HBM(a.shape, a.dtype) for a in srcs + lands], jax.ShapeDtypeStruct((8, 128), F32)),
        in_specs=[HBM_SPEC] * (2 * n) + [pl.BlockSpec(memory_space=pl.ANY)] * len(extra),
        out_specs=(*[SEM_SPEC] * len(sems), *[HBM_SPEC] * (2 * n), pl.BlockSpec(memory_space=pltpu.VMEM)),
        input_output_aliases={i: len(sems) + i for i in range(2 * n)},
        compiler_params=pltpu.CompilerParams(has_side_effects=EFFECT),
    )(*srcs, *lands, *extra)
    started, off = [], 0
    bufs = res[len(sems):]
    for g, size in enumerate(sizes):
        started.append(dict(send=res[2 * g], recv=res[2 * g + 1], srcs=list(bufs[off:off + size]),
                            lands=list(bufs[n + off:n + off + size]), plan=plan))
        off += size
    return started, res[-1]


def _wait_started(st, bufs, send_sems, recv_sems):
    n = len(st["srcs"])
    sends, expects = st["plan"](bufs[:n], bufs[n:], send_sems, recv_sems)
    for cp in sends:
        cp.wait_send()
    for cp in expects:
        cp.wait_recv()


def split_wait(name, started, after):
    sizes = [len(st["srcs"]) + len(st["lands"]) for st in started]
    n_buf = sum(sizes)

    def body(*refs):
        bufs, sems = refs[:n_buf], refs[n_buf:n_buf + 2 * len(started)]
        off = 0
        for g, (st, n) in enumerate(zip(started, sizes)):
            _wait_started(st, bufs[off:off + n], sems[2 * g], sems[2 * g + 1])
            off += n

    bufs = [a for st in started for a in st["srcs"] + st["lands"]]
    sems = [s for st in started for s in (st["send"], st["recv"])]
    res = pl.pallas_call(
        body, name=name, out_shape=tuple(pltpu.HBM(a.shape, a.dtype) for a in bufs),
        in_specs=[HBM_SPEC] * n_buf + [SEM_SPEC] * len(sems) + [pl.BlockSpec(memory_space=pl.ANY)],
        out_specs=tuple([HBM_SPEC] * n_buf), input_output_aliases={i: i for i in range(n_buf)},
        compiler_params=pltpu.CompilerParams(has_side_effects=EFFECT),
    )(*bufs, *sems, after)
    out, off = [], 0
    for st, n in zip(started, sizes):
        out.append((list(res[off:off + len(st["srcs"])]), list(res[off + len(st["srcs"]):off + n])))
        off += n
    return out


def split_relay(name, st, n_sems, plan, after):
    n_src, n = len(st["srcs"]), len(st["lands"])

    def body(*refs):
        bufs = refs[:n_src + n]
        outs = refs[n_src + n + 3:]
        _wait_started(st, bufs, refs[n_src + n], refs[n_src + n + 1])
        sends, _ = plan((), bufs[n_src:], outs[0], outs[1])
        for cp in sends:
            cp.start()
        outs[2 + n][...] = jnp.zeros((8, 128), F32)

    bufs = st["srcs"] + st["lands"]
    res = pl.pallas_call(
        body, name=name,
        out_shape=(pltpu.SemaphoreType.DMA((n * n_sems,)), pltpu.SemaphoreType.DMA((n * n_sems,)),
                   *[pltpu.HBM(a.shape, a.dtype) for a in st["lands"]], jax.ShapeDtypeStruct((8, 128), F32)),
        in_specs=[HBM_SPEC] * (n_src + n) + [SEM_SPEC] * 2 + [pl.BlockSpec(memory_space=pl.ANY)],
        out_specs=(SEM_SPEC, SEM_SPEC, *[HBM_SPEC] * n, pl.BlockSpec(memory_space=pltpu.VMEM)),
        input_output_aliases={n_src + i: 2 + i for i in range(n)},
        compiler_params=pltpu.CompilerParams(has_side_effects=EFFECT),
    )(*bufs, st["send"], st["recv"], after)
    return dict(send=res[0], recv=res[1], srcs=[], lands=list(res[2:2 + n]), plan=plan), res[-1]


def gather_plan(src_refs, land_refs, send_sems, recv_sems):
    x, y, c, chips = _place()
    me = 2 * x + y
    sends, expects = [], []
    for k, (s, d) in enumerate(zip(src_refs, land_refs)):
        mine = pl.ds(c * (s.shape[0] // 2), s.shape[0] // 2)
        for j, (px, py) in enumerate(chips):
            sem = 3 * k + j
            sends.append(_remote(s.at[mine], d.at[me, mine], send_sems.at[sem], recv_sems.at[sem], (px, py, c)))
            expects.append(_remote(s.at[mine], d.at[2 * px + py, mine], send_sems.at[sem], recv_sems.at[sem], (px, py, c)))
    return sends, expects


def relay_plan(src_refs, land_refs, send_sems, recv_sems):
    x, y, c, chips = _place()
    sends, expects = [], []
    for k, d in enumerate(land_refs):
        hr = d.shape[1] // 2
        mine, theirs = pl.ds(c * hr, hr), pl.ds((1 - c) * hr, hr)
        for j, (px, py) in enumerate(chips):
            sem, chip = 3 * k + j, 2 * px + py
            sends.append(_remote(d.at[chip, mine], d.at[chip, mine], send_sems.at[sem], recv_sems.at[sem], (x, y, 1 - c)))
            expects.append(_remote(d.at[chip, mine], d.at[chip, theirs], send_sems.at[sem], recv_sems.at[sem], (x, y, 1 - c)))
    return sends, expects


def _reduce_part(s, chip_idx, h):
    hr = s.shape[1] // 2
    return s.at[chip_idx if s.shape[0] == N_CHIPS else 0, pl.ds(h * hr, hr)]


def reduce_plan(src_refs, land_refs, send_sems, recv_sems):
    x, y, c, chips = _place()
    me_chip, me_dev = 2 * x + y, 4 * x + 2 * y + c
    sends, expects = [], []
    for k, (s, d) in enumerate(zip(src_refs, land_refs)):
        part = functools.partial(_reduce_part, s)
        for j, (px, py) in enumerate(chips):
            for h in range(2):
                sends.append(_remote(part(2 * px + py, h), d.at[me_dev], send_sems.at[7 * k + 2 * j + h],
                                     recv_sems.at[7 * k + 2 * j + c], (px, py, h)))
                expects.append(_remote(part(me_chip, c), d.at[4 * px + 2 * py + h], send_sems.at[7 * k + 2 * j + h],
                                       recv_sems.at[7 * k + 2 * j + h], (px, py, h)))
        sends.append(_remote(part(me_chip, 1 - c), d.at[me_dev], send_sems.at[7 * k + 6], recv_sems.at[7 * k + 6],
                             (x, y, 1 - c)))
        expects.append(_remote(part(me_chip, c), d.at[me_dev + 1 - 2 * c], send_sems.at[7 * k + 6],
                               recv_sems.at[7 * k + 6], (x, y, 1 - c)))
    return sends, expects


def sibling_swap_halves(name, fulls):
    n = len(fulls)

    def body(*refs):
        outs = refs[n:2 * n]
        send_sems, recv_sems = refs[2 * n:]
        x, y, c, _ = _place()
        copies = []
        for k in range(n):
            nl, rows = fulls[k].shape[:2]
            hr = rows // 2
            mine = outs[k].at[pl.ds(0, nl), pl.ds(c * hr, hr)]
            theirs = outs[k].at[pl.ds(0, nl), pl.ds((1 - c) * hr, hr)]
            copies.append((_remote(mine, mine, send_sems.at[k], recv_sems.at[k], (x, y, 1 - c)),
                           _remote(mine, theirs, send_sems.at[k], recv_sems.at[k], (x, y, 1 - c))))
        for send, _ in copies:
            send.start()
        for send, recv in copies:
            send.wait_send()
            recv.wait_recv()

    out_shape = [jax.ShapeDtypeStruct(f.shape, f.dtype) for f in fulls]
    scratch = [pltpu.SemaphoreType.DMA((n,)), pltpu.SemaphoreType.DMA((n,))]
    return _hbm_call(body, name, fulls, out_shape, scratch, aliases={k: k for k in range(n)})


def _tile_rows(rows, cols, budget_bytes):
    best = None
    for t in range(16, rows + 1, 16):
        if rows % t == 0 and t * cols * 4 <= budget_bytes:
            best = t
    assert best is not None, (rows, cols)
    return best


N_DEV = 8


def sum_devices(name, landed, own, layer, n_layers, prev, place):
    _, hr, cols = landed.shape
    tr = _tile_rows(hr, cols, 512 * 1024)
    n_blk = hr // tr
    own_slot = (lambda r, p: (p[0], p[1] * n_blk + r, 0)) if own.shape[0] == N_CHIPS else (lambda r, p: (0, p[1] * n_blk + r, 0))

    def body(p_ref, r_ref, own_ref, *rest):
        o_ref = rest[-1]
        mine = own_ref[...].astype(F32)
        acc = jnp.zeros_like(mine)
        for d in range(N_DEV):
            acc = acc + jnp.where(p_ref[2] == d, mine, r_ref[d].astype(F32))
        o_ref[...] = acc

    in_specs = [pl.BlockSpec((N_DEV, tr, cols), lambda r, p: (0, r, 0)), pl.BlockSpec((None, tr, cols), own_slot)]
    args = [place, landed, own]
    aliases = {}
    if prev is not None:
        in_specs.append(pl.BlockSpec(memory_space=pl.ANY))
        args.append(prev)
        aliases = {3: 0}
    return pl.pallas_call(
        body, name=name, out_shape=jax.ShapeDtypeStruct((n_layers, 2 * hr, cols), F32),
        grid_spec=pltpu.PrefetchScalarGridSpec(
            num_scalar_prefetch=1, grid=(n_blk,), in_specs=in_specs,
            out_specs=pl.BlockSpec((None, tr, cols), lambda r, p: (layer, p[1] * n_blk + r, 0))),
        input_output_aliases=aliases, compiler_params=_params(("parallel",)),
    )(*args)


def _pack(arrs, rows_multiple):
    flat = []
    for a in arrs:
        v = a.reshape(-1).astype(F32)
        flat.append(jnp.pad(v, (0, (-v.shape[0]) % HEAD)))
    v = jnp.concatenate(flat)
    v = jnp.pad(v, (0, (-v.shape[0]) % (HEAD * rows_multiple)))
    return v.reshape(-1, HEAD)


def _unpack(packed, shapes):
    flat = packed.reshape(-1)
    out, off = [], 0
    for shp in shapes:
        size = int(np.prod(shp))
        out.append(flat[off:off + size].reshape(shp))
        off += size + (-size) % HEAD
    return out


BIG = ["even_w_in", "even_w_out", "mla_w_down", "mla_w_qb", "mla_w_kvb", "mla_w_o", "mlp_w1", "mlp_w2"]
BIG_KEY = {"even_w_in": "w_in", "even_w_out": "w_out", "mla_w_down": "w_down", "mla_w_qb": "w_qb",
           "mla_w_kvb": "w_kvb", "mla_w_o": "w_o", "mlp_w1": "w1", "mlp_w2": "w2"}
SMALL_KEY = {"ln_mix_g": "ln_mix_g", "ln_mix_b": "ln_mix_b", "ln_ffn_g": "ln_ffn_g", "ln_ffn_b": "ln_ffn_b",
             "pool_w": "pool_w", "pool_scale": "pool_scale", "lru_conv_w": "conv_w", "lru_conv_b": "conv_b",
             "lru_w_a": "w_a", "lru_b_a": "b_a", "lru_w_x": "w_x", "lru_b_x": "b_x", "lru_lambda": "lam",
             "mla_q_norm_g": "gq", "mla_kv_norm_g": "gkv"}
SMALL = list(SMALL_KEY)
SMALL_SHARDED = ["lru_conv_w", "mla_q_norm_g", "mla_kv_norm_g"]
WEIGHTS = ["ln_mix_g", "ln_mix_b", "ln_ffn_g", "ln_ffn_b", "even_w_in", "pool_w", "pool_scale", "lru_conv_w",
           "lru_conv_b", "lru_w_a", "lru_b_a", "lru_w_x", "lru_b_x", "lru_lambda", "even_w_out", "mla_w_down",
           "mla_q_norm_g", "mla_kv_norm_g", "mla_w_qb", "mla_w_kvb", "mla_w_o", "mlp_w1", "mlp_w2"]


GROUPS = ["mix0", "mlp0", "mix1", "mlp1", "mix2", "mlp2", "mix3", "mlp3"]


def _group_keys(group):
    layer = int(group[3:])
    if group.startswith("mlp"):
        return [("mlp_w1", "w1", layer), ("mlp_w2", "w2", layer)]
    if layer % 2 == 0:
        return [("even_w_in", "w_in", layer // 2), ("even_w_out", "w_out", layer // 2)]
    return [("mla_w_down", "w_down", layer // 2), ("mla_w_qb", "w_qb", layer // 2),
            ("mla_w_kvb", "w_kvb", layer // 2), ("mla_w_o", "w_o", layer // 2)]


def _pad_q_heads(w):
    lead = w.shape[:-1]
    w = w.reshape(lead + (2, QK_NOPE + QK_ROPE))
    w = jnp.pad(w, ((0, 0),) * len(lead) + ((0, 0), (0, QHEAD_PAD - QK_NOPE - QK_ROPE)))
    return w.reshape(lead + (2 * QHEAD_PAD,))


def _unpad_q_heads(g):
    lead = g.shape[:-1]
    return g.reshape(lead + (2, QHEAD_PAD))[..., :QK_NOPE + QK_ROPE].reshape(lead + (2 * (QK_NOPE + QK_ROPE),))


def _step(x, positions, loss_target, w, m, v):
    cx, cy, cc = lax.axis_index("x"), lax.axis_index("y"), lax.axis_index("c")
    chip = 2 * cx + cy
    place = jnp.stack([chip, cc, 2 * chip + cc]).astype(jnp.int32)

    prepared = dict(w)
    prepared["mla_w_down"] = jnp.pad(w["mla_w_down"], ((0, 0), (0, 0), (0, ODD_IN_PAD - ODD_IN)))
    prepared["mla_w_qb"] = _pad_q_heads(w["mla_w_qb"])
    small_shard_shapes = [w[k].shape for k in SMALL_SHARDED]
    sources = {g: [prepared[name][idx].astype(BF16) for name, _, idx in _group_keys(g)] for g in GROUPS}
    sources[GROUPS[0]].append(_pack([w[k] for k in SMALL_SHARDED], 32))
    gathering, token = {}, None
    for name, part in (("first", GROUPS[:1]), ("rest", GROUPS[1:])):
        zones = place_own("own_" + name, [a for g in part for a in sources[g]], place)
        groups, off = [], 0
        for g in part:
            groups.append((sources[g], zones[off:off + len(sources[g])]))
            off += len(sources[g])
        started, token = split_start("gather_" + name, groups, 3, gather_plan, token)
        gathering.update(zip(part, started))
    relayed, relay_token = {}, {}

    def prefetch(g, after):
        relayed[g], relay_token[g] = split_relay("relay_" + g, gathering[g], 3, relay_plan, after)
        return relay_token[g]

    prefetch(GROUPS[0], token)

    def weights_of(g, after):
        _, lands = split_wait("gathered_" + g, [relayed[g]], relay_token[g] if g == GROUPS[0] else after)[0]
        if g.startswith("mlp") and g != GROUPS[-1]:
            prefetch(GROUPS[GROUPS.index(g) + 1], lands[0])
        out = {key: land for (_, key, _), land in zip(_group_keys(g), lands)}
        if g == GROUPS[0]:
            per_chip = [_unpack(lands[-1][s], small_shard_shapes) for s in range(N_CHIPS)]
            for i, key in enumerate(("conv_w", "gq", "gkv")):
                out[key] = jnp.concatenate([p[i] for p in per_chip], axis=-1)
        return out

    reducing = []

    def reduce_start(g, srcs, token_in=None):
        lands = [lax.empty((N_DEV, s.shape[1] // 2, s.shape[2]), s.dtype) for s in srcs]
        started, token = split_start("reduce_" + g, [(srcs, lands)], 7, reduce_plan, token_in)
        reducing.append((g, started[0]))
        return token

    def emit_grads(g, grads):
        return reduce_start(g, [grads[key] for _, key, _ in _group_keys(g)])

    sm = {SMALL_KEY[k]: w[k] for k in SMALL if k not in SMALL_SHARDED}
    loss, grad_x, gs, last_sent = _local_step(x[0], positions.reshape(S, 1), loss_target[0], sm, weights_of,
                                              emit_grads, prefetch)
    loss = lax.psum(loss, ("x", "y", "c"))

    small_shapes = [gs[SMALL_KEY[k]].shape for k in SMALL]
    gs_pack = _pack([gs[SMALL_KEY[k]] for k in SMALL], 32)[None]
    small_sent = reduce_start("small", [gs_pack], last_sent)

    grad, delta, new_m, new_v = {}, {}, {}, {}
    late_groups = ("mix0", "small")
    stacks, after = {}, small_sent
    for late in (False, True):
        part = [(g, st) for g, st in reducing if (g in late_groups) == late]
        landed = split_wait("reduced_late" if late else "reduced_early", [st for _, st in part], after)
        for (g, _), (owns, lands) in zip(part, landed):
            if g == "small":
                stacks["small"] = sum_devices("sum_small", lands[0], owns[0], 0, 1, None, place)
                continue
            for (name, key, idx), own, land in zip(_group_keys(g), owns, lands):
                stacks[name] = sum_devices("sum_%s%d" % (key, idx), land, own, idx, w[name].shape[0], stacks.get(name), place)
        names = [k for k in BIG if (k in ("even_w_in", "even_w_out")) == late] + (["small"] if late else [])
        reduced = dict(zip(names, sibling_swap_halves("swap_late" if late else "swap_early", [stacks[k] for k in names])))
        if not late:
            reduced["mla_w_down"] = reduced["mla_w_down"][..., :ODD_IN]
            reduced["mla_w_qb"] = _unpad_q_heads(reduced["mla_w_qb"])
        for k in names:
            if k == "small":
                continue
            grad[k] = reduced[k]
            shp = w[k].shape
            view = lambda a: a.reshape(-1, shp[-1])
            d, nm, nv = adamw("adamw_" + BIG_KEY[k], view(w[k]), view(grad[k]), view(m[k]), view(v[k]))
            delta[k], new_m[k], new_v[k] = d.reshape(shp), nm.reshape(shp), nv.reshape(shp)
            after = d
    g_small = dict(zip(SMALL, _unpack(reduced["small"], small_shapes)))
    for k in SMALL_SHARDED:
        width = w[k].shape[-1]
        g_small[k] = lax.dynamic_slice_in_dim(g_small[k], chip * width, width, axis=-1)
    grad.update(g_small)

    shapes = [w[k].shape for k in SMALL]
    d, nm, nv = adamw("adamw_small", *[_pack([t[k] for k in SMALL], 512) for t in (w, grad, m, v)])
    for k, dk, mk, vk in zip(SMALL, _unpack(d, shapes), _unpack(nm, shapes), _unpack(nv, shapes)):
        delta[k], new_m[k], new_v[k] = dk, mk, vk
    return (loss, grad_x[None], *[grad[k] for k in WEIGHTS], *[delta[k] for k in WEIGHTS],
            *[new_m[k] for k in WEIGHTS], *[new_v[k] for k in WEIGHTS])


def kernel(x, positions, ln_mix_g, ln_mix_b, ln_ffn_g, ln_ffn_b, even_w_in, pool_w, pool_scale, lru_conv_w, lru_conv_b, lru_w_a, lru_b_a, lru_w_x, lru_b_x, lru_lambda, even_w_out, mla_w_down, mla_q_norm_g, mla_kv_norm_g, mla_w_qb, mla_w_kvb, mla_w_o, mlp_w1, mlp_w2, loss_target, m_ln_mix_g, m_ln_mix_b, m_ln_ffn_g, m_ln_ffn_b, m_even_w_in, m_pool_w, m_pool_scale, m_lru_conv_w, m_lru_conv_b, m_lru_w_a, m_lru_b_a, m_lru_w_x, m_lru_b_x, m_lru_lambda, m_even_w_out, m_mla_w_down, m_mla_q_norm_g, m_mla_kv_norm_g, m_mla_w_qb, m_mla_w_kvb, m_mla_w_o, m_mlp_w1, m_mlp_w2, v_ln_mix_g, v_ln_mix_b, v_ln_ffn_g, v_ln_ffn_b, v_even_w_in, v_pool_w, v_pool_scale, v_lru_conv_w, v_lru_conv_b, v_lru_w_a, v_lru_b_a, v_lru_w_x, v_lru_b_x, v_lru_lambda, v_even_w_out, v_mla_w_down, v_mla_q_norm_g, v_mla_kv_norm_g, v_mla_w_qb, v_mla_w_kvb, v_mla_w_o, v_mlp_w1, v_mlp_w2):
    args = locals()
    w = {k: args[k] for k in WEIGHTS}
    m = {k: args["m_" + k] for k in WEIGHTS}
    v = {k: args["v_" + k] for k in WEIGHTS}
    return _step(x, positions, loss_target, w, m, v)
```

```python
import functools
import math

import jax
import jax.numpy as jnp
import numpy as np
from jax import lax
from jax.experimental import pallas as pl
from jax.experimental.pallas import tpu as pltpu

F32 = jnp.float32
BF16 = jnp.bfloat16

S = 2048
D = 1024
DEPTH = 4
N_CHIPS = 4
POOL_WINDOWS = (2, 4, 8, 16)
POOL_W = 512
LRU_W = 1024
LRU_HEADS = 8
HEAD = 128
EVEN_IN = 2560
EVEN_MIX = 1536
MLA_HEADS = 8
QK_NOPE = 128
QK_ROPE = 64
Q_RANK = 384
KV_RANK = 256
ODD_IN = 704
ODD_IN_PAD = 768
QHEAD_PAD = 256
D_FF = 4096
CHUNK = 64
ALPHA = (2 * DEPTH) ** 0.25
LN_EPS = 1e-5
RMS_EPS = 1e-6
ATT_SCALE = (QK_NOPE + QK_ROPE) ** -0.5
ROPE_THETA = 10000.0
LRU_C = 8.0
ADAM_LR = 0.001
ADAM_B1 = 0.9
ADAM_B2 = 0.999
ADAM_EPS = 1e-08
ADAM_WD = 0.01
ADAM_STEP = 10

VMEM_LIMIT = 56 * 1024 * 1024
MESH = pl.DeviceIdType.MESH

_NN = (((1,), (0,)), ((), ()))
_NT = (((1,), (1,)), ((), ()))
_TN = (((0,), (0,)), ((), ()))


def _params(sem=None, **kw):
    return pltpu.CompilerParams(dimension_semantics=sem, vmem_limit_bytes=VMEM_LIMIT, **kw)


def _dot(a, b, dims):
    return lax.dot_general(a.astype(BF16), b.astype(BF16), dims, preferred_element_type=F32)


def _whole(w4):
    return pl.BlockSpec(tuple(w4.shape), lambda i: (0, 0, 0))


def mm_cols(name, a, w4, out_dtype, *, tm=512):
    m, k = a.shape
    nb = w4.shape[2]

    def body(a_ref, w_ref, o_ref):
        a_v = a_ref[...].astype(BF16)
        for s in range(N_CHIPS):
            o_ref[:, s * nb:(s + 1) * nb] = _dot(a_v, w_ref[s], _NN).astype(o_ref.dtype)

    return pl.pallas_call(
        body, name=name, grid=(m // tm,), in_specs=[pl.BlockSpec((tm, k), lambda i: (i, 0)), _whole(w4)],
        out_specs=pl.BlockSpec((tm, N_CHIPS * nb), lambda i: (i, 0)),
        out_shape=jax.ShapeDtypeStruct((m, N_CHIPS * nb), out_dtype), compiler_params=_params(("parallel",)),
    )(a, w4)


def mm_rows(name, a, w4, out_dtype, *, tm=512, ln=None, after=None):
    m = a.shape[0]
    kb, n = w4.shape[1:]
    n_after = 0 if after is None else 1

    def body(a_ref, w_ref, *rest):
        rest = rest[n_after:]
        a_v = a_ref[...].astype(BF16)
        acc = _dot(a_v[:, :kb], w_ref[0], _NN)
        for s in range(1, N_CHIPS):
            acc = acc + _dot(a_v[:, s * kb:(s + 1) * kb], w_ref[s], _NN)
        if ln is None:
            rest[0][...] = acc.astype(rest[0].dtype)
        else:
            x_ref, g_ref, b_ref, y_ref, xhat_ref, rstd_ref = rest
            y_ref[...], xhat_ref[...], rstd_ref[...] = _layer_norm(ALPHA * x_ref[...] + acc, g_ref[...], b_ref[...])

    row = pl.BlockSpec((tm, n), lambda i: (i, 0))
    vec = pl.BlockSpec((1, n), lambda i: (0, 0))
    in_specs = [pl.BlockSpec((tm, N_CHIPS * kb), lambda i: (i, 0)), _whole(w4)] + [pl.BlockSpec(memory_space=pl.ANY)] * n_after
    if ln is None:
        extras, out_specs, out_shape = (), row, jax.ShapeDtypeStruct((m, n), out_dtype)
    else:
        extras, in_specs = ln, in_specs + [row, vec, vec]
        out_specs = [row, row, pl.BlockSpec((tm, 1), lambda i: (i, 0))]
        out_shape = [jax.ShapeDtypeStruct((m, n), F32), jax.ShapeDtypeStruct((m, n), F32), jax.ShapeDtypeStruct((m, 1), F32)]
    return pl.pallas_call(
        body, name=name, grid=(m // tm,), in_specs=in_specs, out_specs=out_specs, out_shape=out_shape,
        compiler_params=_params(("parallel",)),
    )(a, w4, *([after] * n_after), *extras)


def _ln_bwd_rows(dy, xhat, rstd, g):
    dxh = dy * g
    m1 = jnp.mean(dxh, axis=-1, keepdims=True)
    m2 = jnp.mean(dxh * xhat, axis=-1, keepdims=True)
    dz = rstd * (dxh - m1 - xhat * m2)
    return dz, jnp.sum(dy * xhat, axis=0, keepdims=True), jnp.sum(dy, axis=0, keepdims=True)


def linear_bwd(name, a, g, w4, *, col_sharded, da_dtype, resid=None, ln=None, tm=512, after=None):
    m = a.shape[0]
    n_red = m // tm
    ka, ng = w4.shape[1:]
    k_all = a.shape[1]
    n_after = 0 if after is None else 1
    n_resid = 0 if resid is None else 1
    n_ln = 0 if ln is None else 3

    def body(a_ref, g_ref, w_ref, *rest):
        rest = rest[n_after:]
        ln_refs = rest[n_resid:n_resid + n_ln]
        da_ref, dw_ref = rest[n_resid + n_ln:n_resid + n_ln + 2]
        acc_ref = rest[-1]
        k = pl.program_id(0)

        @pl.when(k == 0)
        def _():
            acc_ref[...] = jnp.zeros_like(acc_ref)

        a_v, g_v = a_ref[...].astype(BF16), g_ref[...].astype(BF16)
        extra = ALPHA * rest[0][...] if n_resid else None
        if col_sharded:
            da = extra
            for s in range(N_CHIPS):
                g_s = g_v[:, s * ng:(s + 1) * ng]
                p = _dot(g_s, w_ref[s], _NT)
                da = p if da is None else da + p
                acc_ref[s] += _dot(a_v, g_s, _TN)
        else:
            parts = []
            for s in range(N_CHIPS):
                parts.append(_dot(g_v, w_ref[s], _NT))
                acc_ref[s] += _dot(a_v[:, s * ka:(s + 1) * ka], g_v, _TN)
            da = jnp.concatenate(parts, axis=1)
            da = da if extra is None else da + extra
        if n_ln:
            dg_ref, db_ref = rest[n_resid + n_ln + 2:n_resid + n_ln + 4]
            da, pg, pb = _ln_bwd_rows(da, ln_refs[0][...], ln_refs[1][...], ln_refs[2][...])

            @pl.when(k == 0)
            def _():
                dg_ref[...] = jnp.zeros_like(dg_ref)
                db_ref[...] = jnp.zeros_like(db_ref)

            dg_ref[...] += pg
            db_ref[...] += pb
        da_ref[...] = da.astype(da_ref.dtype)

        @pl.when(k == n_red - 1)
        def _():
            dw_ref[...] = acc_ref[...].astype(dw_ref.dtype)

    row = lambda width: pl.BlockSpec((tm, width), lambda i: (i, 0))
    vec = pl.BlockSpec((1, k_all), lambda i: (0, 0))
    whole = pl.BlockSpec((N_CHIPS, ka, ng), lambda i: (0, 0, 0))
    out_specs = [row(k_all), whole] + [vec, vec] * (n_ln // 3)
    out_shape = [jax.ShapeDtypeStruct((m, k_all), da_dtype), jax.ShapeDtypeStruct((N_CHIPS, ka, ng), BF16)]
    out_shape += [jax.ShapeDtypeStruct((1, k_all), F32)] * (2 * (n_ln // 3))
    return pl.pallas_call(
        body, name=name, grid=(n_red,),
        in_specs=[row(k_all), row(g.shape[1]), whole] + [pl.BlockSpec(memory_space=pl.ANY)] * n_after
        + [row(k_all)] * n_resid + ([row(k_all), row(1), vec] if n_ln else []),
        out_specs=out_specs, out_shape=out_shape,
        scratch_shapes=[pltpu.VMEM((N_CHIPS, ka, ng), F32)], compiler_params=_params(("arbitrary",)),
    )(a, g, w4, *([after] * n_after), *([resid] * n_resid), *(ln or ()))


def _layer_norm(z, g, b):
    mu = jnp.mean(z, axis=-1, keepdims=True)
    zc = z - mu
    rstd = lax.rsqrt(jnp.mean(zc * zc, axis=-1, keepdims=True) + LN_EPS)
    xhat = zc * rstd
    return xhat * g + b, xhat, rstd


def mlp_fwd(name, x, w1, w2, g, b, *, tm=512):
    fb = w1.shape[2]

    def body(x_ref, w1_ref, w2_ref, g_ref, b_ref, y_ref, xhat_ref, rstd_ref, relu_ref, acc_ref):
        s = pl.program_id(1)

        @pl.when(s == 0)
        def _():
            acc_ref[...] = jnp.zeros_like(acc_ref)

        r = jnp.maximum(_dot(x_ref[...], w1_ref[...], _NN), 0.0)
        relu_ref[...] = r.astype(relu_ref.dtype)
        acc_ref[...] += _dot(r * r, w2_ref[...], _NN)

        @pl.when(s == N_CHIPS - 1)
        def _():
            y_ref[...], xhat_ref[...], rstd_ref[...] = _layer_norm(ALPHA * x_ref[...] + acc_ref[...], g_ref[...], b_ref[...])

    row = pl.BlockSpec((tm, D), lambda i, s: (i, 0))
    vec = pl.BlockSpec((1, D), lambda i, s: (0, 0))
    return pl.pallas_call(
        body, name=name, grid=(S // tm, N_CHIPS),
        in_specs=[row, pl.BlockSpec((None, D, fb), lambda i, s: (s, 0, 0)),
                  pl.BlockSpec((None, fb, D), lambda i, s: (s, 0, 0)), vec, vec],
        out_specs=[row, row, pl.BlockSpec((tm, 1), lambda i, s: (i, 0)), pl.BlockSpec((tm, fb), lambda i, s: (i, s))],
        out_shape=[jax.ShapeDtypeStruct((S, D), F32), jax.ShapeDtypeStruct((S, D), F32),
                   jax.ShapeDtypeStruct((S, 1), F32), jax.ShapeDtypeStruct((S, N_CHIPS * fb), BF16)],
        scratch_shapes=[pltpu.VMEM((tm, D), F32)], compiler_params=_params(("parallel", "arbitrary")),
    )(x, w1, w2, g, b)


def mlp_bwd(name, dz, relu, x, w1, w2, ln, *, tm=256, after=None):
    fb = w1.shape[2]
    n_i = S // tm
    n_after = 0 if after is None else 1

    def body(dz_ref, relu_ref, x_ref, w1_ref, w2_ref, xhat_ref, rstd_ref, gln_ref, *rest):
        dx_ref, g1_ref, g2_ref, dg_ref, db_ref, acc1_ref, acc2_ref = rest[n_after:]
        s, i = pl.program_id(0), pl.program_id(1)
        rows = pl.ds(pl.multiple_of(i * tm, tm), tm)
        dz_v = dz_ref[...]

        @pl.when(i == 0)
        def _():
            acc1_ref[...] = jnp.zeros_like(acc1_ref)
            acc2_ref[...] = jnp.zeros_like(acc2_ref)

        @pl.when(s == 0)
        def _():
            dx_ref[rows, :] = ALPHA * dz_v

        @pl.when((s == 0) & (i == 0))
        def _():
            dg_ref[...] = jnp.zeros_like(dg_ref)
            db_ref[...] = jnp.zeros_like(db_ref)

        dz_b = dz_v.astype(BF16)
        r = relu_ref[...]
        dh = (_dot(dz_b, w2_ref[...], _NT) * (2.0 * r.astype(F32))).astype(BF16)
        p2 = _dot(r * r, dz_b, _TN)
        p1 = _dot(x_ref[...], dh, _TN)
        dx_ref[rows, :] += _dot(dh, w1_ref[...], _NT)
        acc1_ref[...] += p1
        acc2_ref[...] += p2

        @pl.when(i == n_i - 1)
        def _():
            g1_ref[...] = acc1_ref[...].astype(g1_ref.dtype)
            g2_ref[...] = acc2_ref[...].astype(g2_ref.dtype)

        @pl.when(s == N_CHIPS - 1)
        def _():
            dzx, pg, pb = _ln_bwd_rows(dx_ref[rows, :], xhat_ref[...], rstd_ref[...], gln_ref[...])
            dx_ref[rows, :] = dzx
            dg_ref[...] += pg
            db_ref[...] += pb

    row = pl.BlockSpec((tm, D), lambda s, i: (i, 0))
    last_only = lambda s, i: (jnp.where(s == N_CHIPS - 1, i, 0), 0)
    vec = pl.BlockSpec((1, D), lambda s, i: (0, 0))
    return pl.pallas_call(
        body, name=name, grid=(N_CHIPS, n_i),
        in_specs=[row, pl.BlockSpec((tm, fb), lambda s, i: (i, s)), row,
                  pl.BlockSpec((None, D, fb), lambda s, i: (s, 0, 0)), pl.BlockSpec((None, fb, D), lambda s, i: (s, 0, 0)),
                  pl.BlockSpec((tm, D), last_only), pl.BlockSpec((tm, 1), last_only), vec]
        + [pl.BlockSpec(memory_space=pl.ANY)] * n_after,
        out_specs=[pl.BlockSpec((S, D), lambda s, i: (0, 0)), pl.BlockSpec((None, D, fb), lambda s, i: (s, 0, 0)),
                   pl.BlockSpec((None, fb, D), lambda s, i: (s, 0, 0)), vec, vec],
        out_shape=[jax.ShapeDtypeStruct((S, D), F32), jax.ShapeDtypeStruct((N_CHIPS, D, fb), BF16),
                   jax.ShapeDtypeStruct((N_CHIPS, fb, D), BF16), jax.ShapeDtypeStruct((1, D), F32),
                   jax.ShapeDtypeStruct((1, D), F32)],
        scratch_shapes=[pltpu.VMEM((D, fb), F32), pltpu.VMEM((fb, D), F32)],
        compiler_params=_params(("arbitrary", "arbitrary")),
    )(dz, relu, x, w1, w2, *ln, *([after] * n_after))


def loss_head(y, target, xhat, rstd, g, *, tm=256):
    def body(y_ref, t_ref, xhat_ref, rstd_ref, g_ref, dz_ref, dg_ref, db_ref, loss_ref):
        i = pl.program_id(0)

        @pl.when(i == 0)
        def _():
            loss_ref[...] = jnp.zeros_like(loss_ref)
            dg_ref[...] = jnp.zeros_like(dg_ref)
            db_ref[...] = jnp.zeros_like(db_ref)

        e = y_ref[...] - t_ref[...]
        part = jnp.sum(jnp.sum(e * e, axis=-1, keepdims=True), axis=0, keepdims=True) * (0.5 / D)
        loss_ref[...] += jnp.broadcast_to(part, loss_ref.shape)
        dz_ref[...], pg, pb = _ln_bwd_rows(e * (1.0 / D), xhat_ref[...], rstd_ref[...], g_ref[...])
        dg_ref[...] += pg
        db_ref[...] += pb

    row = pl.BlockSpec((tm, D), lambda i: (i, 0))
    vec = pl.BlockSpec((1, D), lambda i: (0, 0))
    return pl.pallas_call(
        body, name="loss_head", grid=(S // tm,),
        in_specs=[row, row, row, pl.BlockSpec((tm, 1), lambda i: (i, 0)), vec],
        out_specs=[row, vec, vec, pl.BlockSpec((8, 128), lambda i: (0, 0))],
        out_shape=[jax.ShapeDtypeStruct((S, D), F32), jax.ShapeDtypeStruct((1, D), F32),
                   jax.ShapeDtypeStruct((1, D), F32), jax.ShapeDtypeStruct((8, 128), F32)],
        compiler_params=_params(("arbitrary",)),
    )(y, target, xhat, rstd, g)


def _rows(shape):
    return lax.broadcasted_iota(jnp.int32, shape, 0)


def _shift_down(x, k):
    return jnp.where(_rows(x.shape) >= k, pltpu.roll(x, k, 0), 0.0)


def _shift_up(x, k):
    n = x.shape[0]
    return jnp.where(_rows(x.shape) < n - k, pltpu.roll(x, n - k, 0), 0.0)


def _pool_diff(u, w):
    acc, k = u, 1
    while k < w:
        acc = acc + _shift_down(acc, k)
        k *= 2
    cnt = jnp.minimum(_rows(u.shape) + 1, w).astype(F32)
    return acc / cnt - u, cnt


def pool_fwd(name, proj, pool_w, pool_scale):
    def body(u_ref, w_ref, sc_ref, y_ref):
        for g, w in enumerate(POOL_WINDOWS):
            cols = slice(g * HEAD, (g + 1) * HEAD)
            d, _ = _pool_diff(u_ref[:, cols], w)
            z = _dot(d, w_ref[g], _NN)
            y_ref[:, cols] = (z * sc_ref[:, cols]).astype(y_ref.dtype)

    return pl.pallas_call(
        body, name=name, grid=(1,),
        in_specs=[pl.BlockSpec((S, POOL_W), lambda i: (0, 0)),
                  pl.BlockSpec((4, HEAD, HEAD), lambda i: (0, 0, 0)),
                  pl.BlockSpec((1, POOL_W), lambda i: (0, 0))],
        out_specs=pl.BlockSpec((S, POOL_W), lambda i: (0, 0)),
        out_shape=jax.ShapeDtypeStruct((S, POOL_W), BF16),
        compiler_params=_params(("arbitrary",)),
    )(proj, pool_w, pool_scale)


def pool_bwd(name, proj, dycat, pool_w, pool_scale):
    def body(u_ref, dy_ref, w_ref, sc_ref, du_ref, dw_ref, dsc_ref):
        for g, w in enumerate(POOL_WINDOWS):
            cols = slice(g * HEAD, (g + 1) * HEAD)
            d, cnt = _pool_diff(u_ref[:, cols], w)
            dy = dy_ref[:, cols]
            z = _dot(d, w_ref[g], _NN)
            dsc_ref[:, cols] = jnp.sum(dy * z, axis=0, keepdims=True)
            dz = dy * sc_ref[:, cols]
            dw_ref[g] = _dot(d, dz, _TN)
            dd = _dot(dz, w_ref[g], _NT)
            acc, k = dd / cnt, 1
            while k < w:
                acc = acc + _shift_up(acc, k)
                k *= 2
            du_ref[:, cols] = (acc - dd).astype(du_ref.dtype)

    return pl.pallas_call(
        body, name=name, grid=(1,),
        in_specs=[pl.BlockSpec((S, POOL_W), lambda i: (0, 0)),
                  pl.BlockSpec((S, POOL_W), lambda i: (0, 0)),
                  pl.BlockSpec((4, HEAD, HEAD), lambda i: (0, 0, 0)),
                  pl.BlockSpec((1, POOL_W), lambda i: (0, 0))],
        out_specs=[pl.BlockSpec((S, POOL_W), lambda i: (0, 0)),
                   pl.BlockSpec((4, HEAD, HEAD), lambda i: (0, 0, 0)),
                   pl.BlockSpec((1, POOL_W), lambda i: (0, 0))],
        out_shape=[jax.ShapeDtypeStruct((S, POOL_W), BF16), jax.ShapeDtypeStruct((4, HEAD, HEAD), F32),
                   jax.ShapeDtypeStruct((1, POOL_W), F32)],
        compiler_params=_params(("arbitrary",)),
    )(proj, dycat, pool_w, pool_scale)


def _expm1(x):
    series = x * (1.0 + x * (0.5 + x * (1.0 / 6.0 + x * (1.0 / 24.0 + x * (1.0 / 120.0)))))
    return jnp.where(jnp.abs(x) < 0.05, series, jnp.exp(x) - 1.0)


def _softplus_neg(lam):
    e = jnp.exp(-jnp.abs(lam))
    log1p = jnp.where(e < 0.01, e * (1.0 - e * (0.5 - e * (1.0 / 3.0))), jnp.log(1.0 + e))
    return jnp.maximum(-lam, 0.0) + log1p


_GELU_C = math.sqrt(2.0 / math.pi)


def _gelu(x):
    t = jnp.tanh(_GELU_C * (x + 0.044715 * x * x * x))
    return 0.5 * x * (1.0 + t), t


def _gelu_grad(x, t):
    return 0.5 * (1.0 + t) + 0.5 * x * (1.0 - t * t) * _GELU_C * (1.0 + 3.0 * 0.044715 * x * x)


def _conv(u, cw, cb):
    return cw[3:4] * u + cw[2:3] * _shift_down(u, 1) + cw[1:2] * _shift_down(u, 2) + cw[0:1] * _shift_down(u, 3) + cb


def _lru_gates(cu, wa, ba, wx, bx, lam):
    r = jax.nn.sigmoid(_dot(cu, wa, _NN) + ba)
    i = jax.nn.sigmoid(_dot(cu, wx, _NN) + bx)
    sp = _softplus_neg(lam)
    log_a = (-LRU_C) * r * sp
    a = jnp.exp(log_a)
    mult = jnp.sqrt(-_expm1(2.0 * log_a))
    return r, i, sp, a, mult


def _scan(a_ref, b_ref, h_ref, *, reverse):
    n_blk = S // 8
    row8 = lax.broadcasted_iota(jnp.int32, (8, HEAD), 0)

    def step(j, carry):
        blk = (n_blk - 1 - j) if reverse else j
        r0 = pl.multiple_of(blk * 8, 8)
        a = a_ref[pl.ds(r0, 8), :]
        b = b_ref[pl.ds(r0, 8), :]
        for k in (1, 2, 4):
            if reverse:
                keep = row8 < 8 - k
                a_s, b_s = pltpu.roll(a, 8 - k, 0), pltpu.roll(b, 8 - k, 0)
            else:
                keep = row8 >= k
                a_s, b_s = pltpu.roll(a, k, 0), pltpu.roll(b, k, 0)
            b = jnp.where(keep, a * b_s + b, b)
            a = jnp.where(keep, a * a_s, a)
        h = b + a * carry
        h_ref[pl.ds(r0, 8), :] = h
        edge = h[0:1, :] if reverse else h[7:8, :]
        return jnp.broadcast_to(edge, (8, HEAD))

    lax.fori_loop(0, n_blk, step, jnp.zeros((8, HEAD), F32), unroll=4)


def _lru_specs():
    def col(off):
        return pl.BlockSpec((S, HEAD), lambda h: (0, off + h))
    vec = pl.BlockSpec((1, HEAD), lambda h: (0, h))
    mat = pl.BlockSpec((None, HEAD, HEAD), lambda h: (h, 0, 0))
    cw = pl.BlockSpec((4, HEAD), lambda h: (0, h))
    return col, vec, mat, cw


def lru_fwd(name, proj, conv_w, conv_b, w_a, b_a, w_x, b_x, lam):
    def body(u_ref, ug_ref, cw_ref, cb_ref, wa_ref, ba_ref, wx_ref, bx_ref, lam_ref, y_ref, h_ref, a_s, b_s):
        cu = _conv(u_ref[...], cw_ref[...], cb_ref[...])
        _, i, _, a, mult = _lru_gates(cu, wa_ref[...], ba_ref[...], wx_ref[...], bx_ref[...], lam_ref[...])
        a_s[...] = a
        b_s[...] = mult * (i * cu)
        _scan(a_s, b_s, h_ref, reverse=False)
        gl, _ = _gelu(ug_ref[...])
        y_ref[...] = (h_ref[...] * gl).astype(y_ref.dtype)

    col, vec, mat, cw = _lru_specs()
    out = pl.BlockSpec((S, HEAD), lambda h: (0, h))
    return pl.pallas_call(
        body, name=name, grid=(LRU_HEADS,),
        in_specs=[col(4), col(12), cw, vec, mat, vec, mat, vec, vec],
        out_specs=[out, out],
        out_shape=[jax.ShapeDtypeStruct((S, LRU_W), BF16), jax.ShapeDtypeStruct((S, LRU_W), F32)],
        scratch_shapes=[pltpu.VMEM((S, HEAD), F32), pltpu.VMEM((S, HEAD), F32)],
        compiler_params=_params(("parallel",)),
    )(proj, proj, conv_w, conv_b, w_a, b_a, w_x, b_x, lam)


def lru_bwd(name, proj, hstate, dycat, conv_w, conv_b, w_a, b_a, w_x, b_x, lam):
    def body(u_ref, ug_ref, h_ref, dy_ref, cw_ref, cb_ref, wa_ref, ba_ref, wx_ref, bx_ref, lam_ref,
             du_ref, dug_ref, dwa_ref, dwx_ref, dba_ref, dbx_ref, dlam_ref, dcw_ref, dcb_ref, a_s, b_s, g_s):
        u = u_ref[...]
        cw = cw_ref[...]
        cu = _conv(u, cw, cb_ref[...])
        lam_v = lam_ref[...]
        r, i, sp, a, mult = _lru_gates(cu, wa_ref[...], ba_ref[...], wx_ref[...], bx_ref[...], lam_v)
        ug = ug_ref[...]
        gl, t = _gelu(ug)
        dy = dy_ref[...]
        h = h_ref[...]
        dug_ref[...] = (dy * h * _gelu_grad(ug, t)).astype(dug_ref.dtype)
        a_s[...] = _shift_up(a, 1)
        b_s[...] = dy * gl
        _scan(a_s, b_s, g_s, reverse=True)
        dxin = g_s[...]
        da = dxin * _shift_down(h, 1)
        dmult = dxin * (i * cu)
        di = dxin * (mult * cu)
        dlog_a = da * a - dmult * (a * a) / mult
        dr_pre = dlog_a * ((-LRU_C) * sp) * (r * (1.0 - r))
        di_pre = di * (i * (1.0 - i))
        dsp = jnp.sum(dlog_a * ((-LRU_C) * r), axis=0, keepdims=True)
        dlam_ref[...] = dsp * (-jax.nn.sigmoid(-lam_v))
        dba_ref[...] = jnp.sum(dr_pre, axis=0, keepdims=True)
        dbx_ref[...] = jnp.sum(di_pre, axis=0, keepdims=True)
        dwa_ref[...] = _dot(cu, dr_pre, _TN)
        dwx_ref[...] = _dot(cu, di_pre, _TN)
        dcu = dxin * (mult * i) + _dot(dr_pre, wa_ref[...], _NT) + _dot(di_pre, wx_ref[...], _NT)
        dcb_ref[...] = jnp.sum(dcu, axis=0, keepdims=True)
        for k in range(4):
            dcw_ref[k:k + 1, :] = jnp.sum(dcu * (_shift_down(u, 3 - k) if k < 3 else u), axis=0, keepdims=True)
        du = cw[3:4] * dcu + cw[2:3] * _shift_up(dcu, 1) + cw[1:2] * _shift_up(dcu, 2) + cw[0:1] * _shift_up(dcu, 3)
        du_ref[...] = du.astype(du_ref.dtype)

    col, vec, mat, cw = _lru_specs()
    out = pl.BlockSpec((S, HEAD), lambda h: (0, h))
    big = jax.ShapeDtypeStruct((S, LRU_W), BF16)
    vec_shape = jax.ShapeDtypeStruct((1, LRU_W), F32)
    mat_shape = jax.ShapeDtypeStruct((LRU_HEADS, HEAD, HEAD), F32)
    return pl.pallas_call(
        body, name=name, grid=(LRU_HEADS,),
        in_specs=[col(4), col(12), out, col(4), cw, vec, mat, vec, mat, vec, vec],
        out_specs=[out, out, mat, mat, vec, vec, vec, cw, vec],
        out_shape=[big, big, mat_shape, mat_shape, vec_shape, vec_shape, vec_shape,
                   jax.ShapeDtypeStruct((4, LRU_W), F32), vec_shape],
        scratch_shapes=[pltpu.VMEM((S, HEAD), F32)] * 3,
        compiler_params=_params(("parallel",)),
    )(proj, proj, hstate, dycat, conv_w, conv_b, w_a, b_a, w_x, b_x, lam)


def rope_tables(pos_col, inv_freq):
    def body(pos_ref, f_ref, c_ref, s1_ref, s2_ref):
        ang = pos_ref[...].astype(F32) * f_ref[...]
        lane = lax.broadcasted_iota(jnp.int32, ang.shape, 1)
        cos, sin = jnp.cos(ang), jnp.sin(ang)
        c_ref[...] = jnp.where(lane < QK_ROPE, cos, 0.0)
        s1_ref[...] = jnp.where(lane < QK_ROPE // 2, -sin, 0.0)
        s2_ref[...] = jnp.where((lane >= QK_ROPE // 2) & (lane < QK_ROPE), sin, 0.0)

    tab = jax.ShapeDtypeStruct((S, HEAD), F32)
    return pl.pallas_call(
        body, name="rope_tables", grid=(1,),
        in_specs=[pl.BlockSpec((S, 1), lambda i: (0, 0)), pl.BlockSpec((1, HEAD), lambda i: (0, 0))],
        out_specs=[pl.BlockSpec((S, HEAD), lambda i: (0, 0))] * 3, out_shape=[tab, tab, tab],
        compiler_params=_params(("arbitrary",)),
    )(pos_col, inv_freq)


def _rope(v, c, s1, s2):
    return v * c + pltpu.roll(v, HEAD - QK_ROPE // 2, 1) * s1 + pltpu.roll(v, QK_ROPE // 2, 1) * s2


def _unrope(d, c, s1, s2):
    return d * c + pltpu.roll(d * s1, QK_ROPE // 2, 1) + pltpu.roll(d * s2, HEAD - QK_ROPE // 2, 1)


def _rms(x, g):
    rstd = lax.rsqrt(jnp.mean(x * x, axis=-1, keepdims=True) + RMS_EPS)
    return x * rstd, rstd


def mla_prep(name, down, gq, gkv, tabs, *, tm=256):
    def body(dn_ref, gq_ref, gkv_ref, c_ref, s1_ref, s2_ref, cq_ref, ckv_ref, kp_ref):
        xq, _ = _rms(dn_ref[:, :Q_RANK], None)
        cq_ref[...] = (xq * gq_ref[...]).astype(cq_ref.dtype)
        xkv, _ = _rms(dn_ref[:, Q_RANK:Q_RANK + KV_RANK], None)
        ckv_ref[...] = (xkv * gkv_ref[...]).astype(ckv_ref.dtype)
        kp = _rope(dn_ref[:, Q_RANK + KV_RANK:], c_ref[...], s1_ref[...], s2_ref[...])
        kp_ref[...] = kp.astype(kp_ref.dtype)

    tab = pl.BlockSpec((tm, HEAD), lambda i: (i, 0))
    return pl.pallas_call(
        body, name=name, grid=(S // tm,),
        in_specs=[pl.BlockSpec((tm, ODD_IN_PAD), lambda i: (i, 0)), pl.BlockSpec((1, Q_RANK), lambda i: (0, 0)),
                  pl.BlockSpec((1, KV_RANK), lambda i: (0, 0)), tab, tab, tab],
        out_specs=[pl.BlockSpec((tm, Q_RANK), lambda i: (i, 0)), pl.BlockSpec((tm, KV_RANK), lambda i: (i, 0)), tab],
        out_shape=[jax.ShapeDtypeStruct((S, Q_RANK), BF16), jax.ShapeDtypeStruct((S, KV_RANK), BF16),
                   jax.ShapeDtypeStruct((S, HEAD), BF16)],
        compiler_params=_params(("parallel",)),
    )(down, gq, gkv, *tabs)


def mla_prep_bwd(name, down, dcq, dckv, dkp, gq, gkv, tabs, *, tm=256):
    def body(dn_ref, dcq_ref, dckv_ref, dkp_ref, gq_ref, gkv_ref, c_ref, s1_ref, s2_ref, dd_ref, dgq_ref, dgkv_ref):
        i = pl.program_id(0)

        def rms_bwd(x, dy, g, dg_ref):
            xh, rstd = _rms(x, None)
            dxh = dy * g
            dx = rstd * (dxh - xh * jnp.mean(dxh * xh, axis=-1, keepdims=True))
            pg = jnp.sum(dy * xh, axis=0, keepdims=True)

            @pl.when(i == 0)
            def _():
                dg_ref[...] = pg

            @pl.when(i > 0)
            def _():
                dg_ref[...] += pg

            return dx

        dxq = rms_bwd(dn_ref[:, :Q_RANK], dcq_ref[...], gq_ref[...], dgq_ref)
        dd_ref[:, :Q_RANK] = dxq.astype(dd_ref.dtype)
        dxkv = rms_bwd(dn_ref[:, Q_RANK:Q_RANK + KV_RANK], dckv_ref[...], gkv_ref[...], dgkv_ref)
        dd_ref[:, Q_RANK:Q_RANK + KV_RANK] = dxkv.astype(dd_ref.dtype)
        dd_ref[:, Q_RANK + KV_RANK:] = _unrope(dkp_ref[...], c_ref[...], s1_ref[...], s2_ref[...]).astype(dd_ref.dtype)

    tab = pl.BlockSpec((tm, HEAD), lambda i: (i, 0))
    vq = pl.BlockSpec((1, Q_RANK), lambda i: (0, 0))
    vkv = pl.BlockSpec((1, KV_RANK), lambda i: (0, 0))
    return pl.pallas_call(
        body, name=name, grid=(S // tm,),
        in_specs=[pl.BlockSpec((tm, ODD_IN_PAD), lambda i: (i, 0)), pl.BlockSpec((tm, Q_RANK), lambda i: (i, 0)),
                  pl.BlockSpec((tm, KV_RANK), lambda i: (i, 0)), tab, vq, vkv, tab, tab, tab],
        out_specs=[pl.BlockSpec((tm, ODD_IN_PAD), lambda i: (i, 0)), vq, vkv],
        out_shape=[jax.ShapeDtypeStruct((S, ODD_IN_PAD), BF16), jax.ShapeDtypeStruct((1, Q_RANK), F32),
                   jax.ShapeDtypeStruct((1, KV_RANK), F32)],
        compiler_params=_params(("arbitrary",)),
    )(down, dcq, dckv, dkp, gq, gkv, *tabs)


ATT_TQ = 256


def _attn_scores(q_ref, kv_ref, kp_ref, c_ref, s1_ref, s2_ref, nk):
    qn = q_ref[:, :HEAD].astype(BF16)
    qp = _rope(q_ref[:, HEAD:], c_ref[...], s1_ref[...], s2_ref[...]).astype(BF16)
    kn = kv_ref[:nk, :HEAD]
    sc = (_dot(qn, kn, _NT) + _dot(qp, kp_ref[:nk, :], _NT)) * ATT_SCALE
    q_chunk = lax.broadcasted_iota(jnp.int32, (ATT_TQ, ATT_TQ), 0) // CHUNK
    k_chunk = lax.broadcasted_iota(jnp.int32, (ATT_TQ, ATT_TQ), 1) // CHUNK
    own = jnp.where(k_chunk <= q_chunk, sc[:, nk - ATT_TQ:], jnp.finfo(F32).min)
    sc = own if nk == ATT_TQ else jnp.concatenate([sc[:, :nk - ATT_TQ], own], axis=1)
    return sc, qn, qp, kn


def _for_each_prefix(i, fn):
    for k in range(S // ATT_TQ):
        pl.when(i == k)(functools.partial(fn, (k + 1) * ATT_TQ))


def _attn_specs():
    q = pl.BlockSpec((ATT_TQ, QHEAD_PAD), lambda h, i: (i, h))
    kv = pl.BlockSpec((S, QHEAD_PAD), lambda h, i: (0, h))
    kp = pl.BlockSpec((S, HEAD), lambda h, i: (0, 0))
    tab = pl.BlockSpec((ATT_TQ, HEAD), lambda h, i: (i, 0))
    o = pl.BlockSpec((ATT_TQ, HEAD), lambda h, i: (i, h))
    lse = pl.BlockSpec((None, ATT_TQ, 1), lambda h, i: (h, i, 0))
    return q, kv, kp, tab, o, lse


def attn_fwd(name, q, kv, kp, tabs):
    def body(q_ref, kv_ref, kp_ref, c_ref, s1_ref, s2_ref, o_ref, lse_ref):
        i = pl.program_id(1)

        def run(nk):
            sc, _, _, _ = _attn_scores(q_ref, kv_ref, kp_ref, c_ref, s1_ref, s2_ref, nk)
            m = jnp.max(sc, axis=-1, keepdims=True)
            e = jnp.exp(sc - m)
            total = jnp.sum(e, axis=-1, keepdims=True)
            o_ref[...] = (_dot(e, kv_ref[:nk, HEAD:], _NN) * (1.0 / total)).astype(o_ref.dtype)
            lse_ref[...] = m + jnp.log(total)

        _for_each_prefix(i, run)

    qs, kvs, kps, tab, os, ls = _attn_specs()
    return pl.pallas_call(
        body, name=name, grid=(MLA_HEADS, S // ATT_TQ), in_specs=[qs, kvs, kps, tab, tab, tab], out_specs=[os, ls],
        out_shape=[jax.ShapeDtypeStruct((S, MLA_HEADS * HEAD), BF16), jax.ShapeDtypeStruct((MLA_HEADS, S, 1), F32)],
        compiler_params=_params(("parallel", "parallel")),
    )(q, kv, kp, *tabs)


def attn_bwd(name, q, kv, kp, o, lse, do, tabs):
    def body(q_ref, kv_ref, kp_ref, o_ref, lse_ref, do_ref, c_ref, s1_ref, s2_ref, dq_ref, dkv_ref, dkp_ref):
        h, i = pl.program_id(0), pl.program_id(1)

        @pl.when(i == 0)
        def _():
            dkv_ref[...] = jnp.zeros_like(dkv_ref)

        @pl.when((i == 0) & (h == 0))
        def _():
            dkp_ref[...] = jnp.zeros_like(dkp_ref)

        def run(nk):
            sc, qn, qp, kn = _attn_scores(q_ref, kv_ref, kp_ref, c_ref, s1_ref, s2_ref, nk)
            p = jnp.exp(sc - lse_ref[...])
            do_v = do_ref[...]
            delta = jnp.sum(do_v.astype(F32) * o_ref[...].astype(F32), axis=-1, keepdims=True)
            dp = _dot(do_v, kv_ref[:nk, HEAD:], _NT)
            ds = (p * ((dp - delta) * ATT_SCALE)).astype(BF16)
            dq_ref[:, :HEAD] = _dot(ds, kn, _NN).astype(dq_ref.dtype)
            dqp = _unrope(_dot(ds, kp_ref[:nk, :], _NN), c_ref[...], s1_ref[...], s2_ref[...])
            dq_ref[:, HEAD:] = dqp.astype(dq_ref.dtype)
            dkv_ref[:nk, :HEAD] += _dot(ds, qn, _TN)
            dkv_ref[:nk, HEAD:] += _dot(p, do_v, _TN)
            dkp_ref[:nk, :] += _dot(ds, qp, _TN)

        _for_each_prefix(i, run)

    qs, kvs, kps, tab, os, ls = _attn_specs()
    return pl.pallas_call(
        body, name=name, grid=(MLA_HEADS, S // ATT_TQ), in_specs=[qs, kvs, kps, os, ls, os, tab, tab, tab],
        out_specs=[qs, kvs, kps],
        out_shape=[jax.ShapeDtypeStruct((S, MLA_HEADS * QHEAD_PAD), BF16),
                   jax.ShapeDtypeStruct((S, MLA_HEADS * QHEAD_PAD), F32), jax.ShapeDtypeStruct((S, HEAD), F32)],
        compiler_params=_params(("arbitrary", "arbitrary")),
    )(q, kv, kp, o, lse, do, *tabs)


def adamw(name, w, g, m, v):
    rows, cols = w.shape
    tr = rows
    for cand in (512, 256, 128, 64, 32, 16, 8):
        if rows % cand == 0 and cand * cols * 4 <= 2 * 1024 * 1024:
            tr = cand
            break

    def body(w_ref, g_ref, m_ref, v_ref, d_ref, nm_ref, nv_ref):
        g_v = g_ref[...]
        nm = ADAM_B1 * m_ref[...] + (1.0 - ADAM_B1) * g_v
        nv = ADAM_B2 * v_ref[...] + (1.0 - ADAM_B2) * (g_v * g_v)
        m_hat = nm / (1.0 - ADAM_B1 ** ADAM_STEP)
        v_hat = nv / (1.0 - ADAM_B2 ** ADAM_STEP)
        d_ref[...] = (-ADAM_LR) * (m_hat / (jnp.sqrt(v_hat) + ADAM_EPS) + ADAM_WD * w_ref[...])
        nm_ref[...] = nm
        nv_ref[...] = nv

    blk = pl.BlockSpec((tr, cols), lambda i: (i, 0))
    shape = jax.ShapeDtypeStruct((rows, cols), F32)
    return pl.pallas_call(
        body, name=name, grid=(rows // tr,), in_specs=[blk] * 4, out_specs=[blk] * 3, out_shape=[shape] * 3,
        compiler_params=_params(("parallel",)),
    )(w, g, m, v)


def _local_step(x, pos_col, target, sm, weights_of, emit_grads, prefetch):
    inv_freq = ROPE_THETA ** (-jnp.arange(0, QK_ROPE, 2, dtype=F32) / QK_ROPE)
    inv_freq = jnp.concatenate([inv_freq, inv_freq, jnp.zeros((HEAD - QK_ROPE,), F32)])[None, :]
    tabs = rope_tables(pos_col, inv_freq)
    saved, wts = [], {}
    for layer in range(DEPTH):
        j = layer // 2
        n = "l%d_" % layer
        sv = {"x": x}
        wm = wts["mix%d" % layer] = weights_of("mix%d" % layer, x)
        if layer == 0:
            sm = dict(sm, conv_w=wm["conv_w"], gq=wm["gq"], gkv=wm["gkv"])
        if layer % 2 == 0:
            proj = mm_cols(n + "proj", x, wm["w_in"], F32)
            fetched = prefetch("mlp%d" % layer, proj)
            y_pool = pool_fwd(n + "pool", proj, sm["pool_w"][j], sm["pool_scale"][j][None])
            y_lru, hstate = lru_fwd(n + "lru", proj, sm["conv_w"][j], sm["conv_b"][j][None], sm["w_a"][j],
                                    sm["b_a"][j][None], sm["w_x"][j], sm["b_x"][j][None], sm["lam"][j][None])
            ycat = jnp.concatenate([y_pool, y_lru], axis=1)
            mix_in, w_mix = ycat, wm["w_out"]
            sv.update(proj=proj, hstate=hstate, ycat=ycat)
        else:
            down = mm_rows(n + "down", x, wm["w_down"], F32)
            fetched = prefetch("mlp%d" % layer, down)
            cq, ckv, kp = mla_prep(n + "prep", down, sm["gq"][j][None], sm["gkv"][j][None], tabs)
            q = mm_cols(n + "q", cq, wm["w_qb"], F32)
            kv = mm_cols(n + "kv", ckv, wm["w_kvb"], BF16)
            o, lse = attn_fwd(n + "attn", q, kv, kp, tabs)
            mix_in, w_mix = o, wm["w_o"]
            sv.update(down=down, cq=cq, ckv=ckv, kp=kp, q=q, kv=kv, o=o, lse=lse)
        x1, xhat1, rstd1 = mm_rows(n + "mixout", mix_in, w_mix, F32, after=fetched,
                                   ln=(x, sm["ln_mix_g"][layer][None], sm["ln_mix_b"][layer][None]))
        wf = wts["mlp%d" % layer] = weights_of("mlp%d" % layer, x1)
        x2, xhat2, rstd2, relu = mlp_fwd(n + "mlp", x1, wf["w1"], wf["w2"], sm["ln_ffn_g"][layer][None],
                                         sm["ln_ffn_b"][layer][None])
        sv.update(xhat1=xhat1, rstd1=rstd1, x1=x1, relu=relu, xhat2=xhat2, rstd2=rstd2)
        saved.append(sv)
        x = x2

    gs = {k: [None] * (DEPTH if k.startswith("ln_") else DEPTH // 2) for k in sm}
    last = saved[DEPTH - 1]
    dz, gs["ln_ffn_g"][DEPTH - 1], gs["ln_ffn_b"][DEPTH - 1], loss_tile = loss_head(
        x, target, last["xhat2"], last["rstd2"], sm["ln_ffn_g"][DEPTH - 1][None])
    sent = None
    for layer in reversed(range(DEPTH)):
        j = layer // 2
        n = "l%d_" % layer
        sv = saved[layer]
        wm, wf = wts["mix%d" % layer], wts["mlp%d" % layer]
        dz, g_w1, g_w2, gs["ln_mix_g"][layer], gs["ln_mix_b"][layer] = mlp_bwd(
            n + "mlp_b", dz, sv["relu"], sv["x1"], wf["w1"], wf["w2"],
            (sv["xhat1"], sv["rstd1"], sm["ln_mix_g"][layer][None]), after=sent)
        sent = emit_grads("mlp%d" % layer, {"w1": g_w1, "w2": g_w2})
        below = saved[layer - 1] if layer else None
        ln_below = (below["xhat2"], below["rstd2"], sm["ln_ffn_g"][layer - 1][None]) if layer else None
        if layer % 2 == 0:
            dycat, g_out = linear_bwd(n + "mixout_b", sv["ycat"], dz, wm["w_out"], col_sharded=False, da_dtype=F32,
                                      after=sent)
            du_pool, gs["pool_w"][j], gs["pool_scale"][j] = pool_bwd(
                n + "pool_b", sv["proj"], dycat, sm["pool_w"][j], sm["pool_scale"][j][None])
            (du_lru, du_gate, gs["w_a"][j], gs["w_x"][j], gs["b_a"][j], gs["b_x"][j], gs["lam"][j], gs["conv_w"][j],
             gs["conv_b"][j]) = lru_bwd(n + "lru_b", sv["proj"], sv["hstate"], dycat, sm["conv_w"][j],
                                        sm["conv_b"][j][None], sm["w_a"][j], sm["b_a"][j][None], sm["w_x"][j],
                                        sm["b_x"][j][None], sm["lam"][j][None])
            dproj = jnp.concatenate([du_pool, du_lru, du_gate], axis=1)
            res = linear_bwd(n + "proj_b", sv["x"], dproj, wm["w_in"], col_sharded=True, da_dtype=F32, resid=dz,
                             ln=ln_below)
            grads = {"w_in": res[1], "w_out": g_out}
        else:
            do, g_o = linear_bwd(n + "attnout_b", sv["o"], dz, wm["w_o"], col_sharded=False, da_dtype=BF16, after=sent)
            dq, dkv, dkp = attn_bwd(n + "attn_b", sv["q"], sv["kv"], sv["kp"], sv["o"], sv["lse"], do, tabs)
            dcq, g_qb = linear_bwd(n + "q_b", sv["cq"], dq, wm["w_qb"], col_sharded=True, da_dtype=F32)
            dckv, g_kvb = linear_bwd(n + "kv_b", sv["ckv"], dkv, wm["w_kvb"], col_sharded=True, da_dtype=F32)
            ddown, gs["gq"][j], gs["gkv"][j] = mla_prep_bwd(
                n + "prep_b", sv["down"], dcq, dckv, dkp, sm["gq"][j][None], sm["gkv"][j][None], tabs)
            res = linear_bwd(n + "down_b", sv["x"], ddown, wm["w_down"], col_sharded=False, da_dtype=F32, resid=dz,
                             ln=ln_below)
            grads = {"w_down": res[1], "w_qb": g_qb, "w_kvb": g_kvb, "w_o": g_o}
        dz = res[0]
        if layer:
            gs["ln_ffn_g"][layer - 1], gs["ln_ffn_b"][layer - 1] = res[2], res[3]
        sent = emit_grads("mix%d" % layer, grads)
    gs = {k: jnp.stack([a.reshape(sm[k].shape[1:]) for a in v]) for k, v in gs.items()}
    return loss_tile[0, 0], dz, gs, sent


def _place():
    x, y, c = lax.axis_index("x"), lax.axis_index("y"), lax.axis_index("c")
    chips = [(1 - x, y), (x, 1 - y), (1 - x, 1 - y)]
    return x, y, c, chips


def _hbm_call(body, name, args, out_shape, scratch, aliases=None):
    return pl.pallas_call(
        body, name=name, in_specs=[pl.BlockSpec(memory_space=pl.ANY)] * len(args),
        out_specs=[pl.BlockSpec(memory_space=pl.ANY)] * len(out_shape), out_shape=out_shape,
        scratch_shapes=scratch, input_output_aliases=aliases or {},
        compiler_params=pltpu.CompilerParams(has_side_effects=True),
    )(*args)


HBM_SPEC = pl.BlockSpec(memory_space=pltpu.HBM)
SEM_SPEC = pl.BlockSpec(memory_space=pltpu.SEMAPHORE)
EFFECT = pltpu.SideEffectType.DATAFLOW_SIDE_EFFECTING


def _remote(src, dst, send_sem, recv_sem, device):
    return pltpu.make_async_remote_copy(src_ref=src, dst_ref=dst, send_sem=send_sem, recv_sem=recv_sem,
                                        device_id=device, device_id_type=MESH)


def place_own(name, srcs, place, steps=4):
    n = len(srcs)
    in_specs, out_specs, out_shape = [], [], []
    for s in srcs:
        rows, cols = s.shape
        tr = rows // steps
        in_specs.append(pl.BlockSpec((tr, cols), lambda i, p: (i, 0)))
        out_specs.append(pl.BlockSpec((None, tr, cols), lambda i, p: (p[0], i, 0)))
        out_shape.append(jax.ShapeDtypeStruct((N_CHIPS, rows, cols), s.dtype))

    def body(p_ref, *refs):
        for i_ref, o_ref in zip(refs[:n], refs[n:]):
            o_ref[...] = i_ref[...]

    return pl.pallas_call(
        body, name=name, out_shape=out_shape,
        grid_spec=pltpu.PrefetchScalarGridSpec(num_scalar_prefetch=1, grid=(steps,), in_specs=in_specs,
                                               out_specs=out_specs),
        compiler_params=_params(("parallel",)),
    )(place, *srcs)


def split_start(name, groups, n_sems, plan, token_in=None):
    sizes = [len(srcs) for srcs, _ in groups]
    n, n_groups = sum(sizes), len(groups)
    srcs = [pltpu.with_memory_space_constraint(a, pltpu.HBM) for s, _ in groups for a in s]
    lands = [pltpu.with_memory_space_constraint(a, pltpu.HBM) for _, l in groups for a in l]
    extra = [] if token_in is None else [token_in]

    def body(*refs):
        src_refs, land_refs = refs[:n], refs[n:2 * n]
        outs = refs[2 * n + len(extra):]
        off = 0
        for g, size in enumerate(sizes):
            sends, _ = plan(src_refs[off:off + size], land_refs[off:off + size], outs[2 * g], outs[2 * g + 1])
            for cp in sends:
                cp.start()
            off += size
        token = outs[2 * n_groups + 2 * n]
        token[...] = jnp.zeros_like(token)

    sems = [pltpu.SemaphoreType.DMA((size * n_sems,)) for size in sizes for _ in range(2)]
    res = pl.pallas_call(
        body, name=name,
        out_shape=(*sems, *[pltpu.HBM(a.shape, a.dtype) for a in srcs + lands], jax.ShapeDtypeStruct((8, 128), F32)),
        in_specs=[HBM_SPEC] * (2 * n) + [pl.BlockSpec(memory_space=pl.ANY)] * len(extra),
        out_specs=(*[SEM_SPEC] * len(sems), *[HBM_SPEC] * (2 * n), pl.BlockSpec(memory_space=pltpu.VMEM)),
        input_output_aliases={i: len(sems) + i for i in range(2 * n)},
        compiler_params=pltpu.CompilerParams(has_side_effects=EFFECT),
    )(*srcs, *lands, *extra)
    started, off = [], 0
    bufs = res[len(sems):]
    for g, size in enumerate(sizes):
        started.append(dict(send=res[2 * g], recv=res[2 * g + 1], srcs=list(bufs[off:off + size]),
                            lands=list(bufs[n + off:n + off + size]), plan=plan))
        off += size
    return started, res[-1]


def _wait_started(st, bufs, send_sems, recv_sems):
    n = len(st["srcs"])
    sends, expects = st["plan"](bufs[:n], bufs[n:], send_sems, recv_sems)
    for cp in sends:
        cp.wait_send()
    for cp in expects:
        cp.wait_recv()


def split_wait(name, started, after):
    sizes = [len(st["srcs"]) + len(st["lands"]) for st in started]
    n_buf = sum(sizes)

    def body(*refs):
        bufs, sems = refs[:n_buf], refs[n_buf:n_buf + 2 * len(started)]
        off = 0
        for g, (st, n) in enumerate(zip(started, sizes)):
            _wait_started(st, bufs[off:off + n], sems[2 * g], sems[2 * g + 1])
            off += n

    bufs = [a for st in started for a in st["srcs"] + st["lands"]]
    sems = [s for st in started for s in (st["send"], st["recv"])]
    res = pl.pallas_call(
        body, name=name, out_shape=tuple(pltpu.HBM(a.shape, a.dtype) for a in bufs),
        in_specs=[HBM_SPEC] * n_buf + [SEM_SPEC] * len(sems) + [pl.BlockSpec(memory_space=pl.ANY)],
        out_specs=tuple([HBM_SPEC] * n_buf), input_output_aliases={i: i for i in range(n_buf)},
        compiler_params=pltpu.CompilerParams(has_side_effects=EFFECT),
    )(*bufs, *sems, after)
    out, off = [], 0
    for st, n in zip(started, sizes):
        out.append((list(res[off:off + len(st["srcs"])]), list(res[off + len(st["srcs"]):off + n])))
        off += n
    return out


def split_relay(name, st, n_sems, plan, after):
    n_src, n = len(st["srcs"]), len(st["lands"])

    def body(*refs):
        bufs = refs[:n_src + n]
        outs = refs[n_src + n + 3:]
        _wait_started(st, bufs, refs[n_src + n], refs[n_src + n + 1])
        sends, _ = plan((), bufs[n_src:], outs[0], outs[1])
        for cp in sends:
            cp.start()
        outs[2 + n][...] = jnp.zeros((8, 128), F32)

    bufs = st["srcs"] + st["lands"]
    res = pl.pallas_call(
        body, name=name,
        out_shape=(pltpu.SemaphoreType.DMA((n * n_sems,)), pltpu.SemaphoreType.DMA((n * n_sems,)),
                   *[pltpu.HBM(a.shape, a.dtype) for a in st["lands"]], jax.ShapeDtypeStruct((8, 128), F32)),
        in_specs=[HBM_SPEC] * (n_src + n) + [SEM_SPEC] * 2 + [pl.BlockSpec(memory_space=pl.ANY)],
        out_specs=(SEM_SPEC, SEM_SPEC, *[HBM_SPEC] * n, pl.BlockSpec(memory_space=pltpu.VMEM)),
        input_output_aliases={n_src + i: 2 + i for i in range(n)},
        compiler_params=pltpu.CompilerParams(has_side_effects=EFFECT),
    )(*bufs, st["send"], st["recv"], after)
    return dict(send=res[0], recv=res[1], srcs=[], lands=list(res[2:2 + n]), plan=plan), res[-1]


def gather_plan(src_refs, land_refs, send_sems, recv_sems):
    x, y, c, chips = _place()
    me = 2 * x + y
    sends, expects = [], []
    for k, (s, d) in enumerate(zip(src_refs, land_refs)):
        mine = pl.ds(c * (s.shape[0] // 2), s.shape[0] // 2)
        for j, (px, py) in enumerate(chips):
            sem = 3 * k + j
            sends.append(_remote(s.at[mine], d.at[me, mine], send_sems.at[sem], recv_sems.at[sem], (px, py, c)))
            expects.append(_remote(s.at[mine], d.at[2 * px + py, mine], send_sems.at[sem], recv_sems.at[sem], (px, py, c)))
    return sends, expects


def relay_plan(src_refs, land_refs, send_sems, recv_sems):
    x, y, c, chips = _place()
    sends, expects = [], []
    for k, d in enumerate(land_refs):
        hr = d.shape[1] // 2
        mine, theirs = pl.ds(c * hr, hr), pl.ds((1 - c) * hr, hr)
        for j, (px, py) in enumerate(chips):
            sem, chip = 3 * k + j, 2 * px + py
            sends.append(_remote(d.at[chip, mine], d.at[chip, mine], send_sems.at[sem], recv_sems.at[sem], (x, y, 1 - c)))
            expects.append(_remote(d.at[chip, mine], d.at[chip, theirs], send_sems.at[sem], recv_sems.at[sem], (x, y, 1 - c)))
    return sends, expects


def _reduce_part(s, chip_idx, h):
    hr = s.shape[1] // 2
    return s.at[chip_idx if s.shape[0] == N_CHIPS else 0, pl.ds(h * hr, hr)]


def reduce_plan(src_refs, land_refs, send_sems, recv_sems):
    x, y, c, chips = _place()
    me_chip, me_dev = 2 * x + y, 4 * x + 2 * y + c
    sends, expects = [], []
    for k, (s, d) in enumerate(zip(src_refs, land_refs)):
        part = functools.partial(_reduce_part, s)
        for j, (px, py) in enumerate(chips):
            for h in range(2):
                sends.append(_remote(part(2 * px + py, h), d.at[me_dev], send_sems.at[7 * k + 2 * j + h],
                                     recv_sems.at[7 * k + 2 * j + c], (px, py, h)))
                expects.append(_remote(part(me_chip, c), d.at[4 * px + 2 * py + h], send_sems.at[7 * k + 2 * j + h],
                                       recv_sems.at[7 * k + 2 * j + h], (px, py, h)))
        sends.append(_remote(part(me_chip, 1 - c), d.at[me_dev], send_sems.at[7 * k + 6], recv_sems.at[7 * k + 6],
                             (x, y, 1 - c)))
        expects.append(_remote(part(me_chip, c), d.at[me_dev + 1 - 2 * c], send_sems.at[7 * k + 6],
                               recv_sems.at[7 * k + 6], (x, y, 1 - c)))
    return sends, expects


def sibling_swap_halves(name, fulls):
    n = len(fulls)

    def body(*refs):
        outs = refs[n:2 * n]
        send_sems, recv_sems = refs[2 * n:]
        x, y, c, _ = _place()
        copies = []
        for k in range(n):
            nl, rows = fulls[k].shape[:2]
            hr = rows // 2
            mine = outs[k].at[pl.ds(0, nl), pl.ds(c * hr, hr)]
            theirs = outs[k].at[pl.ds(0, nl), pl.ds((1 - c) * hr, hr)]
            copies.append((_remote(mine, mine, send_sems.at[k], recv_sems.at[k], (x, y, 1 - c)),
                           _remote(mine, theirs, send_sems.at[k], recv_sems.at[k], (x, y, 1 - c))))
        for send, _ in copies:
            send.start()
        for send, recv in copies:
            send.wait_send()
            recv.wait_recv()

    out_shape = [jax.ShapeDtypeStruct(f.shape, f.dtype) for f in fulls]
    scratch = [pltpu.SemaphoreType.DMA((n,)), pltpu.SemaphoreType.DMA((n,))]
    return _hbm_call(body, name, fulls, out_shape, scratch, aliases={k: k for k in range(n)})


def _tile_rows(rows, cols, budget_bytes):
    best = None
    for t in range(16, rows + 1, 16):
        if rows % t == 0 and t * cols * 4 <= budget_bytes:
            best = t
    assert best is not None, (rows, cols)
    return best


N_DEV = 8


def sum_devices(name, items, place, steps):
    n = len(items)
    in_specs, args, out_specs, out_shape, aliases = [], [place], [], [], {}
    for landed, own, layer, n_layers, _ in items:
        _, hr, cols = landed.shape
        tr = hr // steps
        slot = (lambda p: p[0]) if own.shape[0] == N_CHIPS else (lambda p: 0)
        in_specs += [pl.BlockSpec((N_DEV, tr, cols), lambda r, p: (0, r, 0)),
                     pl.BlockSpec((None, tr, cols), lambda r, p, slot=slot: (slot(p), p[1] * steps + r, 0))]
        args += [landed, own]
        out_specs.append(pl.BlockSpec((None, tr, cols), lambda r, p, layer=layer: (layer, p[1] * steps + r, 0)))
        out_shape.append(jax.ShapeDtypeStruct((n_layers, 2 * hr, cols), F32))
    for k, item in enumerate(items):
        if item[4] is not None:
            in_specs.append(pl.BlockSpec(memory_space=pl.ANY))
            aliases[len(args)] = k
            args.append(item[4])
    n_prev = len(aliases)

    def body(p_ref, *refs):
        for k in range(n):
            r_ref, own_ref, o_ref = refs[2 * k], refs[2 * k + 1], refs[2 * n + n_prev + k]
            mine = own_ref[...].astype(F32)
            acc = jnp.zeros_like(mine)
            for d in range(N_DEV):
                acc = acc + jnp.where(p_ref[2] == d, mine, r_ref[d].astype(F32))
            o_ref[...] = acc

    return pl.pallas_call(
        body, name=name, out_shape=out_shape,
        grid_spec=pltpu.PrefetchScalarGridSpec(num_scalar_prefetch=1, grid=(steps,), in_specs=in_specs,
                                               out_specs=out_specs),
        input_output_aliases=aliases, compiler_params=_params(("parallel",)),
    )(*args)


def _pack(arrs, rows_multiple):
    flat = []
    for a in arrs:
        v = a.reshape(-1).astype(F32)
        flat.append(jnp.pad(v, (0, (-v.shape[0]) % HEAD)))
    v = jnp.concatenate(flat)
    v = jnp.pad(v, (0, (-v.shape[0]) % (HEAD * rows_multiple)))
    return v.reshape(-1, HEAD)


def _unpack(packed, shapes):
    flat = packed.reshape(-1)
    out, off = [], 0
    for shp in shapes:
        size = int(np.prod(shp))
        out.append(flat[off:off + size].reshape(shp))
        off += size + (-size) % HEAD
    return out


BIG = ["even_w_in", "even_w_out", "mla_w_down", "mla_w_qb", "mla_w_kvb", "mla_w_o", "mlp_w1", "mlp_w2"]
BIG_KEY = {"even_w_in": "w_in", "even_w_out": "w_out", "mla_w_down": "w_down", "mla_w_qb": "w_qb",
           "mla_w_kvb": "w_kvb", "mla_w_o": "w_o", "mlp_w1": "w1", "mlp_w2": "w2"}
SMALL_KEY = {"ln_mix_g": "ln_mix_g", "ln_mix_b": "ln_mix_b", "ln_ffn_g": "ln_ffn_g", "ln_ffn_b": "ln_ffn_b",
             "pool_w": "pool_w", "pool_scale": "pool_scale", "lru_conv_w": "conv_w", "lru_conv_b": "conv_b",
             "lru_w_a": "w_a", "lru_b_a": "b_a", "lru_w_x": "w_x", "lru_b_x": "b_x", "lru_lambda": "lam",
             "mla_q_norm_g": "gq", "mla_kv_norm_g": "gkv"}
SMALL = list(SMALL_KEY)
SMALL_SHARDED = ["lru_conv_w", "mla_q_norm_g", "mla_kv_norm_g"]
WEIGHTS = ["ln_mix_g", "ln_mix_b", "ln_ffn_g", "ln_ffn_b", "even_w_in", "pool_w", "pool_scale", "lru_conv_w",
           "lru_conv_b", "lru_w_a", "lru_b_a", "lru_w_x", "lru_b_x", "lru_lambda", "even_w_out", "mla_w_down",
           "mla_q_norm_g", "mla_kv_norm_g", "mla_w_qb", "mla_w_kvb", "mla_w_o", "mlp_w1", "mlp_w2"]


GROUPS = ["mix0", "mlp0", "mix1", "mlp1", "mix2", "mlp2", "mix3", "mlp3"]


def _group_keys(group):
    layer = int(group[3:])
    if group.startswith("mlp"):
        return [("mlp_w1", "w1", layer), ("mlp_w2", "w2", layer)]
    if layer % 2 == 0:
        return [("even_w_in", "w_in", layer // 2), ("even_w_out", "w_out", layer // 2)]
    return [("mla_w_down", "w_down", layer // 2), ("mla_w_qb", "w_qb", layer // 2),
            ("mla_w_kvb", "w_kvb", layer // 2), ("mla_w_o", "w_o", layer // 2)]


def _pad_q_heads(w):
    lead = w.shape[:-1]
    w = w.reshape(lead + (2, QK_NOPE + QK_ROPE))
    w = jnp.pad(w, ((0, 0),) * len(lead) + ((0, 0), (0, QHEAD_PAD - QK_NOPE - QK_ROPE)))
    return w.reshape(lead + (2 * QHEAD_PAD,))


def _unpad_q_heads(g):
    lead = g.shape[:-1]
    return g.reshape(lead + (2, QHEAD_PAD))[..., :QK_NOPE + QK_ROPE].reshape(lead + (2 * (QK_NOPE + QK_ROPE),))


def _step(x, positions, loss_target, w, m, v):
    cx, cy, cc = lax.axis_index("x"), lax.axis_index("y"), lax.axis_index("c")
    chip = 2 * cx + cy
    place = jnp.stack([chip, cc, 2 * chip + cc]).astype(jnp.int32)

    prepared = dict(w)
    prepared["mla_w_down"] = jnp.pad(w["mla_w_down"], ((0, 0), (0, 0), (0, ODD_IN_PAD - ODD_IN)))
    prepared["mla_w_qb"] = _pad_q_heads(w["mla_w_qb"])
    small_shard_shapes = [w[k].shape for k in SMALL_SHARDED]
    sources = {g: [prepared[name][idx].astype(BF16) for name, _, idx in _group_keys(g)] for g in GROUPS}
    sources[GROUPS[0]].append(_pack([w[k] for k in SMALL_SHARDED], 32))
    gathering, token = {}, None
    for name, part in (("first", GROUPS[:1]), ("rest", GROUPS[1:])):
        zones = place_own("own_" + name, [a for g in part for a in sources[g]], place)
        groups, off = [], 0
        for g in part:
            groups.append((sources[g], zones[off:off + len(sources[g])]))
            off += len(sources[g])
        started, token = split_start("gather_" + name, groups, 3, gather_plan, token)
        gathering.update(zip(part, started))
    relayed, relay_token = {}, {}

    def prefetch(g, after):
        relayed[g], relay_token[g] = split_relay("relay_" + g, gathering[g], 3, relay_plan, after)
        return relay_token[g]

    prefetch(GROUPS[0], token)

    def weights_of(g, after):
        _, lands = split_wait("gathered_" + g, [relayed[g]], relay_token[g] if g == GROUPS[0] else after)[0]
        if g.startswith("mlp") and g != GROUPS[-1]:
            prefetch(GROUPS[GROUPS.index(g) + 1], lands[0])
        out = {key: land for (_, key, _), land in zip(_group_keys(g), lands)}
        if g == GROUPS[0]:
            per_chip = [_unpack(lands[-1][s], small_shard_shapes) for s in range(N_CHIPS)]
            for i, key in enumerate(("conv_w", "gq", "gkv")):
                out[key] = jnp.concatenate([p[i] for p in per_chip], axis=-1)
        return out

    reducing = []

    def reduce_start(g, srcs, token_in=None):
        lands = [lax.empty((N_DEV, s.shape[1] // 2, s.shape[2]), s.dtype) for s in srcs]
        started, token = split_start("reduce_" + g, [(srcs, lands)], 7, reduce_plan, token_in)
        reducing.append((g, started[0]))
        return token

    def emit_grads(g, grads):
        return reduce_start(g, [grads[key] for _, key, _ in _group_keys(g)])

    sm = {SMALL_KEY[k]: w[k] for k in SMALL if k not in SMALL_SHARDED}
    loss, grad_x, gs, last_sent = _local_step(x[0], positions.reshape(S, 1), loss_target[0], sm, weights_of,
                                              emit_grads, prefetch)
    loss = lax.psum(loss, ("x", "y", "c"))

    small_shapes = [gs[SMALL_KEY[k]].shape for k in SMALL]
    gs_pack = _pack([gs[SMALL_KEY[k]] for k in SMALL], 32)[None]
    small_sent = reduce_start("small", [gs_pack], last_sent)

    grad, delta, new_m, new_v = {}, {}, {}, {}
    late_groups = ("mix0", "small")
    stacks, after = {}, small_sent
    for late in (False, True):
        part = [(g, st) for g, st in reducing if (g in late_groups) == late]
        landed = split_wait("reduced_late" if late else "reduced_early", [st for _, st in part], after)
        for (g, _), (owns, lands) in zip(part, landed):
            if g == "small":
                half = lands[0].shape[1]
                stacks["small"] = sum_devices("sum_small", [(lands[0], owns[0], 0, 1, None)], place,
                                              half // _tile_rows(half, HEAD, 512 * 1024))[0]
                continue
            keys = _group_keys(g)
            items = [(land, own, idx, w[name].shape[0], stacks.get(name))
                     for (name, _, idx), own, land in zip(keys, owns, lands)]
            for (name, _, _), stack in zip(keys, sum_devices("sum_" + g, items, place, 4)):
                stacks[name] = stack
        names = [k for k in BIG if (k in ("even_w_in", "even_w_out")) == late] + (["small"] if late else [])
        reduced = dict(zip(names, sibling_swap_halves("swap_late" if late else "swap_early", [stacks[k] for k in names])))
        if not late:
            reduced["mla_w_down"] = reduced["mla_w_down"][..., :ODD_IN]
            reduced["mla_w_qb"] = _unpad_q_heads(reduced["mla_w_qb"])
        for k in names:
            if k == "small":
                continue
            grad[k] = reduced[k]
            shp = w[k].shape
            view = lambda a: a.reshape(-1, shp[-1])
            d, nm, nv = adamw("adamw_" + BIG_KEY[k], view(w[k]), view(grad[k]), view(m[k]), view(v[k]))
            delta[k], new_m[k], new_v[k] = d.reshape(shp), nm.reshape(shp), nv.reshape(shp)
            after = d
    g_small = dict(zip(SMALL, _unpack(reduced["small"], small_shapes)))
    for k in SMALL_SHARDED:
        width = w[k].shape[-1]
        g_small[k] = lax.dynamic_slice_in_dim(g_small[k], chip * width, width, axis=-1)
    grad.update(g_small)

    shapes = [w[k].shape for k in SMALL]
    d, nm, nv = adamw("adamw_small", *[_pack([t[k] for k in SMALL], 512) for t in (w, grad, m, v)])
    for k, dk, mk, vk in zip(SMALL, _unpack(d, shapes), _unpack(nm, shapes), _unpack(nv, shapes)):
        delta[k], new_m[k], new_v[k] = dk, mk, vk
    return (loss, grad_x[None], *[grad[k] for k in WEIGHTS], *[delta[k] for k in WEIGHTS],
            *[new_m[k] for k in WEIGHTS], *[new_v[k] for k in WEIGHTS])


def kernel(x, positions, ln_mix_g, ln_mix_b, ln_ffn_g, ln_ffn_b, even_w_in, pool_w, pool_scale, lru_conv_w, lru_conv_b, lru_w_a, lru_b_a, lru_w_x, lru_b_x, lru_lambda, even_w_out, mla_w_down, mla_q_norm_g, mla_kv_norm_g, mla_w_qb, mla_w_kvb, mla_w_o, mlp_w1, mlp_w2, loss_target, m_ln_mix_g, m_ln_mix_b, m_ln_ffn_g, m_ln_ffn_b, m_even_w_in, m_pool_w, m_pool_scale, m_lru_conv_w, m_lru_conv_b, m_lru_w_a, m_lru_b_a, m_lru_w_x, m_lru_b_x, m_lru_lambda, m_even_w_out, m_mla_w_down, m_mla_q_norm_g, m_mla_kv_norm_g, m_mla_w_qb, m_mla_w_kvb, m_mla_w_o, m_mlp_w1, m_mlp_w2, v_ln_mix_g, v_ln_mix_b, v_ln_ffn_g, v_ln_ffn_b, v_even_w_in, v_pool_w, v_pool_scale, v_lru_conv_w, v_lru_conv_b, v_lru_w_a, v_lru_b_a, v_lru_w_x, v_lru_b_x, v_lru_lambda, v_even_w_out, v_mla_w_down, v_mla_q_norm_g, v_mla_kv_norm_g, v_mla_w_qb, v_mla_w_kvb, v_mla_w_o, v_mlp_w1, v_mlp_w2):
    args = locals()
    w = {k: args[k] for k in WEIGHTS}
    m = {k: args["m_" + k] for k in WEIGHTS}
    v = {k: args["v_" + k] for k in WEIGHTS}
    return _step(x, positions, loss_target, w, m, v)
```

```python
import functools
import math

import jax
import jax.numpy as jnp
import numpy as np
from jax import lax
from jax.experimental import pallas as pl
from jax.experimental.pallas import tpu as pltpu

F32 = jnp.float32
BF16 = jnp.bfloat16

S = 2048
D = 1024
DEPTH = 4
N_CHIPS = 4
POOL_WINDOWS = (2, 4, 8, 16)
POOL_W = 512
LRU_W = 1024
LRU_HEADS = 8
HEAD = 128
EVEN_IN = 2560
EVEN_MIX = 1536
MLA_HEADS = 8
QK_NOPE = 128
QK_ROPE = 64
Q_RANK = 384
KV_RANK = 256
ODD_IN = 704
ODD_IN_PAD = 768
QHEAD_PAD = 256
D_FF = 4096
CHUNK = 64
ALPHA = (2 * DEPTH) ** 0.25
LN_EPS = 1e-5
RMS_EPS = 1e-6
ATT_SCALE = (QK_NOPE + QK_ROPE) ** -0.5
ROPE_THETA = 10000.0
LRU_C = 8.0
ADAM_LR = 0.001
ADAM_B1 = 0.9
ADAM_B2 = 0.999
ADAM_EPS = 1e-08
ADAM_WD = 0.01
ADAM_STEP = 10

VMEM_LIMIT = 56 * 1024 * 1024
MESH = pl.DeviceIdType.MESH

_NN = (((1,), (0,)), ((), ()))
_NT = (((1,), (1,)), ((), ()))
_TN = (((0,), (0,)), ((), ()))


def _params(sem=None, **kw):
    return pltpu.CompilerParams(dimension_semantics=sem, vmem_limit_bytes=VMEM_LIMIT, **kw)


def _dot(a, b, dims):
    return lax.dot_general(a.astype(BF16), b.astype(BF16), dims, preferred_element_type=F32)


def _whole(w4):
    return pl.BlockSpec(tuple(w4.shape), lambda i: (0, 0, 0))


def mm_cols(name, a, w4, out_dtype, *, tm=512):
    m, k = a.shape
    nb = w4.shape[2]

    def body(a_ref, w_ref, o_ref):
        a_v = a_ref[...].astype(BF16)
        for s in range(N_CHIPS):
            o_ref[:, s * nb:(s + 1) * nb] = _dot(a_v, w_ref[s], _NN).astype(o_ref.dtype)

    return pl.pallas_call(
        body, name=name, grid=(m // tm,), in_specs=[pl.BlockSpec((tm, k), lambda i: (i, 0)), _whole(w4)],
        out_specs=pl.BlockSpec((tm, N_CHIPS * nb), lambda i: (i, 0)),
        out_shape=jax.ShapeDtypeStruct((m, N_CHIPS * nb), out_dtype), compiler_params=_params(("parallel",)),
    )(a, w4)


def mm_rows(name, a, w4, out_dtype, *, tm=512, ln=None, after=None):
    m = a.shape[0]
    kb, n = w4.shape[1:]
    n_after = 0 if after is None else 1

    def body(a_ref, w_ref, *rest):
        rest = rest[n_after:]
        a_v = a_ref[...].astype(BF16)
        acc = _dot(a_v[:, :kb], w_ref[0], _NN)
        for s in range(1, N_CHIPS):
            acc = acc + _dot(a_v[:, s * kb:(s + 1) * kb], w_ref[s], _NN)
        if ln is None:
            rest[0][...] = acc.astype(rest[0].dtype)
        else:
            x_ref, g_ref, b_ref, y_ref, xhat_ref, rstd_ref = rest
            y_ref[...], xhat_ref[...], rstd_ref[...] = _layer_norm(ALPHA * x_ref[...] + acc, g_ref[...], b_ref[...])

    row = pl.BlockSpec((tm, n), lambda i: (i, 0))
    vec = pl.BlockSpec((1, n), lambda i: (0, 0))
    in_specs = [pl.BlockSpec((tm, N_CHIPS * kb), lambda i: (i, 0)), _whole(w4)] + [pl.BlockSpec(memory_space=pl.ANY)] * n_after
    if ln is None:
        extras, out_specs, out_shape = (), row, jax.ShapeDtypeStruct((m, n), out_dtype)
    else:
        extras, in_specs = ln, in_specs + [row, vec, vec]
        out_specs = [row, row, pl.BlockSpec((tm, 1), lambda i: (i, 0))]
        out_shape = [jax.ShapeDtypeStruct((m, n), F32), jax.ShapeDtypeStruct((m, n), F32), jax.ShapeDtypeStruct((m, 1), F32)]
    return pl.pallas_call(
        body, name=name, grid=(m // tm,), in_specs=in_specs, out_specs=out_specs, out_shape=out_shape,
        compiler_params=_params(("parallel",)),
    )(a, w4, *([after] * n_after), *extras)


def _ln_bwd_rows(dy, xhat, rstd, g):
    dxh = dy * g
    m1 = jnp.mean(dxh, axis=-1, keepdims=True)
    m2 = jnp.mean(dxh * xhat, axis=-1, keepdims=True)
    dz = rstd * (dxh - m1 - xhat * m2)
    return dz, jnp.sum(dy * xhat, axis=0, keepdims=True), jnp.sum(dy, axis=0, keepdims=True)


def linear_bwd(name, a, g, w4, *, col_sharded, da_dtype, resid=None, ln=None, tm=512, after=None):
    m = a.shape[0]
    n_red = m // tm
    ka, ng = w4.shape[1:]
    k_all = a.shape[1]
    n_after = 0 if after is None else 1
    n_resid = 0 if resid is None else 1
    n_ln = 0 if ln is None else 3

    def body(a_ref, g_ref, w_ref, *rest):
        rest = rest[n_after:]
        ln_refs = rest[n_resid:n_resid + n_ln]
        da_ref, dw_ref = rest[n_resid + n_ln:n_resid + n_ln + 2]
        acc_ref = rest[-1]
        k = pl.program_id(0)

        @pl.when(k == 0)
        def _():
            acc_ref[...] = jnp.zeros_like(acc_ref)

        a_v, g_v = a_ref[...].astype(BF16), g_ref[...].astype(BF16)
        extra = ALPHA * rest[0][...] if n_resid else None
        if col_sharded:
            da = extra
            for s in range(N_CHIPS):
                g_s = g_v[:, s * ng:(s + 1) * ng]
                p = _dot(g_s, w_ref[s], _NT)
                da = p if da is None else da + p
                acc_ref[s] += _dot(a_v, g_s, _TN)
        else:
            parts = []
            for s in range(N_CHIPS):
                parts.append(_dot(g_v, w_ref[s], _NT))
                acc_ref[s] += _dot(a_v[:, s * ka:(s + 1) * ka], g_v, _TN)
            da = jnp.concatenate(parts, axis=1)
            da = da if extra is None else da + extra
        if n_ln:
            dg_ref, db_ref = rest[n_resid + n_ln + 2:n_resid + n_ln + 4]
            da, pg, pb = _ln_bwd_rows(da, ln_refs[0][...], ln_refs[1][...], ln_refs[2][...])

            @pl.when(k == 0)
            def _():
                dg_ref[...] = jnp.zeros_like(dg_ref)
                db_ref[...] = jnp.zeros_like(db_ref)

            dg_ref[...] += pg
            db_ref[...] += pb
        da_ref[...] = da.astype(da_ref.dtype)

        @pl.when(k == n_red - 1)
        def _():
            dw_ref[...] = acc_ref[...].astype(dw_ref.dtype)

    row = lambda width: pl.BlockSpec((tm, width), lambda i: (i, 0))
    vec = pl.BlockSpec((1, k_all), lambda i: (0, 0))
    whole = pl.BlockSpec((N_CHIPS, ka, ng), lambda i: (0, 0, 0))
    out_specs = [row(k_all), whole] + [vec, vec] * (n_ln // 3)
    out_shape = [jax.ShapeDtypeStruct((m, k_all), da_dtype), jax.ShapeDtypeStruct((N_CHIPS, ka, ng), BF16)]
    out_shape += [jax.ShapeDtypeStruct((1, k_all), F32)] * (2 * (n_ln // 3))
    return pl.pallas_call(
        body, name=name, grid=(n_red,),
        in_specs=[row(k_all), row(g.shape[1]), whole] + [pl.BlockSpec(memory_space=pl.ANY)] * n_after
        + [row(k_all)] * n_resid + ([row(k_all), row(1), vec] if n_ln else []),
        out_specs=out_specs, out_shape=out_shape,
        scratch_shapes=[pltpu.VMEM((N_CHIPS, ka, ng), F32)], compiler_params=_params(("arbitrary",)),
    )(a, g, w4, *([after] * n_after), *([resid] * n_resid), *(ln or ()))


def _layer_norm(z, g, b):
    mu = jnp.mean(z, axis=-1, keepdims=True)
    zc = z - mu
    rstd = lax.rsqrt(jnp.mean(zc * zc, axis=-1, keepdims=True) + LN_EPS)
    xhat = zc * rstd
    return xhat * g + b, xhat, rstd


def mlp_fwd(name, x, w1, w2, g, b, *, tm=512):
    fb = w1.shape[2]

    def body(x_ref, w1_ref, w2_ref, g_ref, b_ref, y_ref, xhat_ref, rstd_ref, relu_ref, acc_ref):
        s = pl.program_id(1)

        @pl.when(s == 0)
        def _():
            acc_ref[...] = jnp.zeros_like(acc_ref)

        r = jnp.maximum(_dot(x_ref[...], w1_ref[...], _NN), 0.0)
        relu_ref[...] = r.astype(relu_ref.dtype)
        acc_ref[...] += _dot(r * r, w2_ref[...], _NN)

        @pl.when(s == N_CHIPS - 1)
        def _():
            y_ref[...], xhat_ref[...], rstd_ref[...] = _layer_norm(ALPHA * x_ref[...] + acc_ref[...], g_ref[...], b_ref[...])

    row = pl.BlockSpec((tm, D), lambda i, s: (i, 0))
    vec = pl.BlockSpec((1, D), lambda i, s: (0, 0))
    return pl.pallas_call(
        body, name=name, grid=(S // tm, N_CHIPS),
        in_specs=[row, pl.BlockSpec((None, D, fb), lambda i, s: (s, 0, 0)),
                  pl.BlockSpec((None, fb, D), lambda i, s: (s, 0, 0)), vec, vec],
        out_specs=[row, row, pl.BlockSpec((tm, 1), lambda i, s: (i, 0)), pl.BlockSpec((tm, fb), lambda i, s: (i, s))],
        out_shape=[jax.ShapeDtypeStruct((S, D), F32), jax.ShapeDtypeStruct((S, D), F32),
                   jax.ShapeDtypeStruct((S, 1), F32), jax.ShapeDtypeStruct((S, N_CHIPS * fb), BF16)],
        scratch_shapes=[pltpu.VMEM((tm, D), F32)], compiler_params=_params(("parallel", "arbitrary")),
    )(x, w1, w2, g, b)


def mlp_bwd(name, dz, relu, x, w1, w2, ln, *, tm=256, after=None):
    fb = w1.shape[2]
    n_i = S // tm
    n_after = 0 if after is None else 1

    def body(dz_ref, relu_ref, x_ref, w1_ref, w2_ref, xhat_ref, rstd_ref, gln_ref, *rest):
        dx_ref, g1_ref, g2_ref, dg_ref, db_ref, acc1_ref, acc2_ref = rest[n_after:]
        s, i = pl.program_id(0), pl.program_id(1)
        rows = pl.ds(pl.multiple_of(i * tm, tm), tm)
        dz_v = dz_ref[...]

        @pl.when(i == 0)
        def _():
            acc1_ref[...] = jnp.zeros_like(acc1_ref)
            acc2_ref[...] = jnp.zeros_like(acc2_ref)

        @pl.when(s == 0)
        def _():
            dx_ref[rows, :] = ALPHA * dz_v

        @pl.when((s == 0) & (i == 0))
        def _():
            dg_ref[...] = jnp.zeros_like(dg_ref)
            db_ref[...] = jnp.zeros_like(db_ref)

        dz_b = dz_v.astype(BF16)
        r = relu_ref[...]
        dh = (_dot(dz_b, w2_ref[...], _NT) * (2.0 * r.astype(F32))).astype(BF16)
        p2 = _dot(r * r, dz_b, _TN)
        p1 = _dot(x_ref[...], dh, _TN)
        dx_ref[rows, :] += _dot(dh, w1_ref[...], _NT)
        acc1_ref[...] += p1
        acc2_ref[...] += p2

        @pl.when(i == n_i - 1)
        def _():
            g1_ref[...] = acc1_ref[...].astype(g1_ref.dtype)
            g2_ref[...] = acc2_ref[...].astype(g2_ref.dtype)

        @pl.when(s == N_CHIPS - 1)
        def _():
            dzx, pg, pb = _ln_bwd_rows(dx_ref[rows, :], xhat_ref[...], rstd_ref[...], gln_ref[...])
            dx_ref[rows, :] = dzx
            dg_ref[...] += pg
            db_ref[...] += pb

    row = pl.BlockSpec((tm, D), lambda s, i: (i, 0))
    last_only = lambda s, i: (jnp.where(s == N_CHIPS - 1, i, 0), 0)
    vec = pl.BlockSpec((1, D), lambda s, i: (0, 0))
    return pl.pallas_call(
        body, name=name, grid=(N_CHIPS, n_i),
        in_specs=[row, pl.BlockSpec((tm, fb), lambda s, i: (i, s)), row,
                  pl.BlockSpec((None, D, fb), lambda s, i: (s, 0, 0)), pl.BlockSpec((None, fb, D), lambda s, i: (s, 0, 0)),
                  pl.BlockSpec((tm, D), last_only), pl.BlockSpec((tm, 1), last_only), vec]
        + [pl.BlockSpec(memory_space=pl.ANY)] * n_after,
        out_specs=[pl.BlockSpec((S, D), lambda s, i: (0, 0)), pl.BlockSpec((None, D, fb), lambda s, i: (s, 0, 0)),
                   pl.BlockSpec((None, fb, D), lambda s, i: (s, 0, 0)), vec, vec],
        out_shape=[jax.ShapeDtypeStruct((S, D), F32), jax.ShapeDtypeStruct((N_CHIPS, D, fb), BF16),
                   jax.ShapeDtypeStruct((N_CHIPS, fb, D), BF16), jax.ShapeDtypeStruct((1, D), F32),
                   jax.ShapeDtypeStruct((1, D), F32)],
        scratch_shapes=[pltpu.VMEM((D, fb), F32), pltpu.VMEM((fb, D), F32)],
        compiler_params=_params(("arbitrary", "arbitrary")),
    )(dz, relu, x, w1, w2, *ln, *([after] * n_after))


def loss_head(y, target, xhat, rstd, g, *, tm=256):
    def body(y_ref, t_ref, xhat_ref, rstd_ref, g_ref, dz_ref, dg_ref, db_ref, loss_ref):
        i = pl.program_id(0)

        @pl.when(i == 0)
        def _():
            loss_ref[...] = jnp.zeros_like(loss_ref)
            dg_ref[...] = jnp.zeros_like(dg_ref)
            db_ref[...] = jnp.zeros_like(db_ref)

        e = y_ref[...] - t_ref[...]
        part = jnp.sum(jnp.sum(e * e, axis=-1, keepdims=True), axis=0, keepdims=True) * (0.5 / D)
        loss_ref[...] += jnp.broadcast_to(part, loss_ref.shape)
        dz_ref[...], pg, pb = _ln_bwd_rows(e * (1.0 / D), xhat_ref[...], rstd_ref[...], g_ref[...])
        dg_ref[...] += pg
        db_ref[...] += pb

    row = pl.BlockSpec((tm, D), lambda i: (i, 0))
    vec = pl.BlockSpec((1, D), lambda i: (0, 0))
    return pl.pallas_call(
        body, name="loss_head", grid=(S // tm,),
        in_specs=[row, row, row, pl.BlockSpec((tm, 1), lambda i: (i, 0)), vec],
        out_specs=[row, vec, vec, pl.BlockSpec((8, 128), lambda i: (0, 0))],
        out_shape=[jax.ShapeDtypeStruct((S, D), F32), jax.ShapeDtypeStruct((1, D), F32),
                   jax.ShapeDtypeStruct((1, D), F32), jax.ShapeDtypeStruct((8, 128), F32)],
        compiler_params=_params(("arbitrary",)),
    )(y, target, xhat, rstd, g)


def _rows(shape):
    return lax.broadcasted_iota(jnp.int32, shape, 0)


def _shift_down(x, k):
    return jnp.where(_rows(x.shape) >= k, pltpu.roll(x, k, 0), 0.0)


def _shift_up(x, k):
    n = x.shape[0]
    return jnp.where(_rows(x.shape) < n - k, pltpu.roll(x, n - k, 0), 0.0)


def _pool_diff(u, w):
    acc, k = u, 1
    while k < w:
        acc = acc + _shift_down(acc, k)
        k *= 2
    cnt = jnp.minimum(_rows(u.shape) + 1, w).astype(F32)
    return acc / cnt - u, cnt


def pool_fwd(name, proj, pool_w, pool_scale):
    def body(u_ref, w_ref, sc_ref, y_ref):
        for g, w in enumerate(POOL_WINDOWS):
            cols = slice(g * HEAD, (g + 1) * HEAD)
            d, _ = _pool_diff(u_ref[:, cols], w)
            z = _dot(d, w_ref[g], _NN)
            y_ref[:, cols] = (z * sc_ref[:, cols]).astype(y_ref.dtype)

    return pl.pallas_call(
        body, name=name, grid=(1,),
        in_specs=[pl.BlockSpec((S, POOL_W), lambda i: (0, 0)),
                  pl.BlockSpec((4, HEAD, HEAD), lambda i: (0, 0, 0)),
                  pl.BlockSpec((1, POOL_W), lambda i: (0, 0))],
        out_specs=pl.BlockSpec((S, POOL_W), lambda i: (0, 0)),
        out_shape=jax.ShapeDtypeStruct((S, POOL_W), BF16),
        compiler_params=_params(("arbitrary",)),
    )(proj, pool_w, pool_scale)


def pool_bwd(name, proj, dycat, pool_w, pool_scale):
    def body(u_ref, dy_ref, w_ref, sc_ref, du_ref, dw_ref, dsc_ref):
        for g, w in enumerate(POOL_WINDOWS):
            cols = slice(g * HEAD, (g + 1) * HEAD)
            d, cnt = _pool_diff(u_ref[:, cols], w)
            dy = dy_ref[:, cols]
            z = _dot(d, w_ref[g], _NN)
            dsc_ref[:, cols] = jnp.sum(dy * z, axis=0, keepdims=True)
            dz = dy * sc_ref[:, cols]
            dw_ref[g] = _dot(d, dz, _TN)
            dd = _dot(dz, w_ref[g], _NT)
            acc, k = dd / cnt, 1
            while k < w:
                acc = acc + _shift_up(acc, k)
                k *= 2
            du_ref[:, cols] = (acc - dd).astype(du_ref.dtype)

    return pl.pallas_call(
        body, name=name, grid=(1,),
        in_specs=[pl.BlockSpec((S, POOL_W), lambda i: (0, 0)),
                  pl.BlockSpec((S, POOL_W), lambda i: (0, 0)),
                  pl.BlockSpec((4, HEAD, HEAD), lambda i: (0, 0, 0)),
                  pl.BlockSpec((1, POOL_W), lambda i: (0, 0))],
        out_specs=[pl.BlockSpec((S, POOL_W), lambda i: (0, 0)),
                   pl.BlockSpec((4, HEAD, HEAD), lambda i: (0, 0, 0)),
                   pl.BlockSpec((1, POOL_W), lambda i: (0, 0))],
        out_shape=[jax.ShapeDtypeStruct((S, POOL_W), BF16), jax.ShapeDtypeStruct((4, HEAD, HEAD), F32),
                   jax.ShapeDtypeStruct((1, POOL_W), F32)],
        compiler_params=_params(("arbitrary",)),
    )(proj, dycat, pool_w, pool_scale)


def _expm1(x):
    series = x * (1.0 + x * (0.5 + x * (1.0 / 6.0 + x * (1.0 / 24.0 + x * (1.0 / 120.0)))))
    return jnp.where(jnp.abs(x) < 0.05, series, jnp.exp(x) - 1.0)


def _softplus_neg(lam):
    e = jnp.exp(-jnp.abs(lam))
    log1p = jnp.where(e < 0.01, e * (1.0 - e * (0.5 - e * (1.0 / 3.0))), jnp.log(1.0 + e))
    return jnp.maximum(-lam, 0.0) + log1p


_GELU_C = math.sqrt(2.0 / math.pi)


def _gelu(x):
    t = jnp.tanh(_GELU_C * (x + 0.044715 * x * x * x))
    return 0.5 * x * (1.0 + t), t


def _gelu_grad(x, t):
    return 0.5 * (1.0 + t) + 0.5 * x * (1.0 - t * t) * _GELU_C * (1.0 + 3.0 * 0.044715 * x * x)


def _conv(u, cw, cb):
    return cw[3:4] * u + cw[2:3] * _shift_down(u, 1) + cw[1:2] * _shift_down(u, 2) + cw[0:1] * _shift_down(u, 3) + cb


def _lru_gates(cu, wa, ba, wx, bx, lam):
    r = jax.nn.sigmoid(_dot(cu, wa, _NN) + ba)
    i = jax.nn.sigmoid(_dot(cu, wx, _NN) + bx)
    sp = _softplus_neg(lam)
    log_a = (-LRU_C) * r * sp
    a = jnp.exp(log_a)
    mult = jnp.sqrt(-_expm1(2.0 * log_a))
    return r, i, sp, a, mult


def _scan(a_ref, b_ref, h_ref, *, reverse):
    n_blk = S // 8
    row8 = lax.broadcasted_iota(jnp.int32, (8, HEAD), 0)

    def step(j, carry):
        blk = (n_blk - 1 - j) if reverse else j
        r0 = pl.multiple_of(blk * 8, 8)
        a = a_ref[pl.ds(r0, 8), :]
        b = b_ref[pl.ds(r0, 8), :]
        for k in (1, 2, 4):
            if reverse:
                keep = row8 < 8 - k
                a_s, b_s = pltpu.roll(a, 8 - k, 0), pltpu.roll(b, 8 - k, 0)
            else:
                keep = row8 >= k
                a_s, b_s = pltpu.roll(a, k, 0), pltpu.roll(b, k, 0)
            b = jnp.where(keep, a * b_s + b, b)
            a = jnp.where(keep, a * a_s, a)
        h = b + a * carry
        h_ref[pl.ds(r0, 8), :] = h
        edge = h[0:1, :] if reverse else h[7:8, :]
        return jnp.broadcast_to(edge, (8, HEAD))

    lax.fori_loop(0, n_blk, step, jnp.zeros((8, HEAD), F32), unroll=4)


def _lru_specs():
    def col(off):
        return pl.BlockSpec((S, HEAD), lambda h: (0, off + h))
    vec = pl.BlockSpec((1, HEAD), lambda h: (0, h))
    mat = pl.BlockSpec((None, HEAD, HEAD), lambda h: (h, 0, 0))
    cw = pl.BlockSpec((4, HEAD), lambda h: (0, h))
    return col, vec, mat, cw


def lru_fwd(name, proj, conv_w, conv_b, w_a, b_a, w_x, b_x, lam):
    def body(u_ref, ug_ref, cw_ref, cb_ref, wa_ref, ba_ref, wx_ref, bx_ref, lam_ref, y_ref, h_ref, a_s, b_s):
        cu = _conv(u_ref[...], cw_ref[...], cb_ref[...])
        _, i, _, a, mult = _lru_gates(cu, wa_ref[...], ba_ref[...], wx_ref[...], bx_ref[...], lam_ref[...])
        a_s[...] = a
        b_s[...] = mult * (i * cu)
        _scan(a_s, b_s, h_ref, reverse=False)
        gl, _ = _gelu(ug_ref[...])
        y_ref[...] = (h_ref[...] * gl).astype(y_ref.dtype)

    col, vec, mat, cw = _lru_specs()
    out = pl.BlockSpec((S, HEAD), lambda h: (0, h))
    return pl.pallas_call(
        body, name=name, grid=(LRU_HEADS,),
        in_specs=[col(4), col(12), cw, vec, mat, vec, mat, vec, vec],
        out_specs=[out, out],
        out_shape=[jax.ShapeDtypeStruct((S, LRU_W), BF16), jax.ShapeDtypeStruct((S, LRU_W), F32)],
        scratch_shapes=[pltpu.VMEM((S, HEAD), F32), pltpu.VMEM((S, HEAD), F32)],
        compiler_params=_params(("parallel",)),
    )(proj, proj, conv_w, conv_b, w_a, b_a, w_x, b_x, lam)


def lru_bwd(name, proj, hstate, dycat, conv_w, conv_b, w_a, b_a, w_x, b_x, lam):
    def body(u_ref, ug_ref, h_ref, dy_ref, cw_ref, cb_ref, wa_ref, ba_ref, wx_ref, bx_ref, lam_ref,
             du_ref, dug_ref, dwa_ref, dwx_ref, dba_ref, dbx_ref, dlam_ref, dcw_ref, dcb_ref, a_s, b_s, g_s):
        u = u_ref[...]
        cw = cw_ref[...]
        cu = _conv(u, cw, cb_ref[...])
        lam_v = lam_ref[...]
        r, i, sp, a, mult = _lru_gates(cu, wa_ref[...], ba_ref[...], wx_ref[...], bx_ref[...], lam_v)
        ug = ug_ref[...]
        gl, t = _gelu(ug)
        dy = dy_ref[...]
        h = h_ref[...]
        dug_ref[...] = (dy * h * _gelu_grad(ug, t)).astype(dug_ref.dtype)
        a_s[...] = _shift_up(a, 1)
        b_s[...] = dy * gl
        _scan(a_s, b_s, g_s, reverse=True)
        dxin = g_s[...]
        da = dxin * _shift_down(h, 1)
        dmult = dxin * (i * cu)
        di = dxin * (mult * cu)
        dlog_a = da * a - dmult * (a * a) / mult
        dr_pre = dlog_a * ((-LRU_C) * sp) * (r * (1.0 - r))
        di_pre = di * (i * (1.0 - i))
        dsp = jnp.sum(dlog_a * ((-LRU_C) * r), axis=0, keepdims=True)
        dlam_ref[...] = dsp * (-jax.nn.sigmoid(-lam_v))
        dba_ref[...] = jnp.sum(dr_pre, axis=0, keepdims=True)
        dbx_ref[...] = jnp.sum(di_pre, axis=0, keepdims=True)
        dwa_ref[...] = _dot(cu, dr_pre, _TN)
        dwx_ref[...] = _dot(cu, di_pre, _TN)
        dcu = dxin * (mult * i) + _dot(dr_pre, wa_ref[...], _NT) + _dot(di_pre, wx_ref[...], _NT)
        dcb_ref[...] = jnp.sum(dcu, axis=0, keepdims=True)
        for k in range(4):
            dcw_ref[k:k + 1, :] = jnp.sum(dcu * (_shift_down(u, 3 - k) if k < 3 else u), axis=0, keepdims=True)
        du = cw[3:4] * dcu + cw[2:3] * _shift_up(dcu, 1) + cw[1:2] * _shift_up(dcu, 2) + cw[0:1] * _shift_up(dcu, 3)
        du_ref[...] = du.astype(du_ref.dtype)

    col, vec, mat, cw = _lru_specs()
    out = pl.BlockSpec((S, HEAD), lambda h: (0, h))
    big = jax.ShapeDtypeStruct((S, LRU_W), BF16)
    vec_shape = jax.ShapeDtypeStruct((1, LRU_W), F32)
    mat_shape = jax.ShapeDtypeStruct((LRU_HEADS, HEAD, HEAD), F32)
    return pl.pallas_call(
        body, name=name, grid=(LRU_HEADS,),
        in_specs=[col(4), col(12), out, col(4), cw, vec, mat, vec, mat, vec, vec],
        out_specs=[out, out, mat, mat, vec, vec, vec, cw, vec],
        out_shape=[big, big, mat_shape, mat_shape, vec_shape, vec_shape, vec_shape,
                   jax.ShapeDtypeStruct((4, LRU_W), F32), vec_shape],
        scratch_shapes=[pltpu.VMEM((S, HEAD), F32)] * 3,
        compiler_params=_params(("parallel",)),
    )(proj, proj, hstate, dycat, conv_w, conv_b, w_a, b_a, w_x, b_x, lam)


def rope_tables(pos_col, inv_freq):
    def body(pos_ref, f_ref, c_ref, s1_ref, s2_ref):
        ang = pos_ref[...].astype(F32) * f_ref[...]
        lane = lax.broadcasted_iota(jnp.int32, ang.shape, 1)
        cos, sin = jnp.cos(ang), jnp.sin(ang)
        c_ref[...] = jnp.where(lane < QK_ROPE, cos, 0.0)
        s1_ref[...] = jnp.where(lane < QK_ROPE // 2, -sin, 0.0)
        s2_ref[...] = jnp.where((lane >= QK_ROPE // 2) & (lane < QK_ROPE), sin, 0.0)

    tab = jax.ShapeDtypeStruct((S, HEAD), F32)
    return pl.pallas_call(
        body, name="rope_tables", grid=(1,),
        in_specs=[pl.BlockSpec((S, 1), lambda i: (0, 0)), pl.BlockSpec((1, HEAD), lambda i: (0, 0))],
        out_specs=[pl.BlockSpec((S, HEAD), lambda i: (0, 0))] * 3, out_shape=[tab, tab, tab],
        compiler_params=_params(("arbitrary",)),
    )(pos_col, inv_freq)


def _rope(v, c, s1, s2):
    return v * c + pltpu.roll(v, HEAD - QK_ROPE // 2, 1) * s1 + pltpu.roll(v, QK_ROPE // 2, 1) * s2


def _unrope(d, c, s1, s2):
    return d * c + pltpu.roll(d * s1, QK_ROPE // 2, 1) + pltpu.roll(d * s2, HEAD - QK_ROPE // 2, 1)


def _rms(x, g):
    rstd = lax.rsqrt(jnp.mean(x * x, axis=-1, keepdims=True) + RMS_EPS)
    return x * rstd, rstd


def mla_prep(name, down, gq, gkv, tabs, *, tm=256):
    def body(dn_ref, gq_ref, gkv_ref, c_ref, s1_ref, s2_ref, cq_ref, ckv_ref, kp_ref):
        xq, _ = _rms(dn_ref[:, :Q_RANK], None)
        cq_ref[...] = (xq * gq_ref[...]).astype(cq_ref.dtype)
        xkv, _ = _rms(dn_ref[:, Q_RANK:Q_RANK + KV_RANK], None)
        ckv_ref[...] = (xkv * gkv_ref[...]).astype(ckv_ref.dtype)
        kp = _rope(dn_ref[:, Q_RANK + KV_RANK:], c_ref[...], s1_ref[...], s2_ref[...])
        kp_ref[...] = kp.astype(kp_ref.dtype)

    tab = pl.BlockSpec((tm, HEAD), lambda i: (i, 0))
    return pl.pallas_call(
        body, name=name, grid=(S // tm,),
        in_specs=[pl.BlockSpec((tm, ODD_IN_PAD), lambda i: (i, 0)), pl.BlockSpec((1, Q_RANK), lambda i: (0, 0)),
                  pl.BlockSpec((1, KV_RANK), lambda i: (0, 0)), tab, tab, tab],
        out_specs=[pl.BlockSpec((tm, Q_RANK), lambda i: (i, 0)), pl.BlockSpec((tm, KV_RANK), lambda i: (i, 0)), tab],
        out_shape=[jax.ShapeDtypeStruct((S, Q_RANK), BF16), jax.ShapeDtypeStruct((S, KV_RANK), BF16),
                   jax.ShapeDtypeStruct((S, HEAD), BF16)],
        compiler_params=_params(("parallel",)),
    )(down, gq, gkv, *tabs)


def mla_prep_bwd(name, down, dcq, dckv, dkp, gq, gkv, tabs, *, tm=256):
    def body(dn_ref, dcq_ref, dckv_ref, dkp_ref, gq_ref, gkv_ref, c_ref, s1_ref, s2_ref, dd_ref, dgq_ref, dgkv_ref):
        i = pl.program_id(0)

        def rms_bwd(x, dy, g, dg_ref):
            xh, rstd = _rms(x, None)
            dxh = dy * g
            dx = rstd * (dxh - xh * jnp.mean(dxh * xh, axis=-1, keepdims=True))
            pg = jnp.sum(dy * xh, axis=0, keepdims=True)

            @pl.when(i == 0)
            def _():
                dg_ref[...] = pg

            @pl.when(i > 0)
            def _():
                dg_ref[...] += pg

            return dx

        dxq = rms_bwd(dn_ref[:, :Q_RANK], dcq_ref[...], gq_ref[...], dgq_ref)
        dd_ref[:, :Q_RANK] = dxq.astype(dd_ref.dtype)
        dxkv = rms_bwd(dn_ref[:, Q_RANK:Q_RANK + KV_RANK], dckv_ref[...], gkv_ref[...], dgkv_ref)
        dd_ref[:, Q_RANK:Q_RANK + KV_RANK] = dxkv.astype(dd_ref.dtype)
        dd_ref[:, Q_RANK + KV_RANK:] = _unrope(dkp_ref[...], c_ref[...], s1_ref[...], s2_ref[...]).astype(dd_ref.dtype)

    tab = pl.BlockSpec((tm, HEAD), lambda i: (i, 0))
    vq = pl.BlockSpec((1, Q_RANK), lambda i: (0, 0))
    vkv = pl.BlockSpec((1, KV_RANK), lambda i: (0, 0))
    return pl.pallas_call(
        body, name=name, grid=(S // tm,),
        in_specs=[pl.BlockSpec((tm, ODD_IN_PAD), lambda i: (i, 0)), pl.BlockSpec((tm, Q_RANK), lambda i: (i, 0)),
                  pl.BlockSpec((tm, KV_RANK), lambda i: (i, 0)), tab, vq, vkv, tab, tab, tab],
        out_specs=[pl.BlockSpec((tm, ODD_IN_PAD), lambda i: (i, 0)), vq, vkv],
        out_shape=[jax.ShapeDtypeStruct((S, ODD_IN_PAD), BF16), jax.ShapeDtypeStruct((1, Q_RANK), F32),
                   jax.ShapeDtypeStruct((1, KV_RANK), F32)],
        compiler_params=_params(("arbitrary",)),
    )(down, dcq, dckv, dkp, gq, gkv, *tabs)


ATT_TQ = 256


def _attn_scores(q_ref, kv_ref, kp_ref, c_ref, s1_ref, s2_ref, nk):
    qn = q_ref[:, :HEAD].astype(BF16)
    qp = _rope(q_ref[:, HEAD:], c_ref[...], s1_ref[...], s2_ref[...]).astype(BF16)
    kn = kv_ref[:nk, :HEAD]
    sc = (_dot(qn, kn, _NT) + _dot(qp, kp_ref[:nk, :], _NT)) * ATT_SCALE
    q_chunk = lax.broadcasted_iota(jnp.int32, (ATT_TQ, ATT_TQ), 0) // CHUNK
    k_chunk = lax.broadcasted_iota(jnp.int32, (ATT_TQ, ATT_TQ), 1) // CHUNK
    own = jnp.where(k_chunk <= q_chunk, sc[:, nk - ATT_TQ:], jnp.finfo(F32).min)
    sc = own if nk == ATT_TQ else jnp.concatenate([sc[:, :nk - ATT_TQ], own], axis=1)
    return sc, qn, qp, kn


def _for_each_prefix(i, fn):
    for k in range(S // ATT_TQ):
        pl.when(i == k)(functools.partial(fn, (k + 1) * ATT_TQ))


def _attn_specs():
    q = pl.BlockSpec((ATT_TQ, QHEAD_PAD), lambda h, i: (i, h))
    kv = pl.BlockSpec((S, QHEAD_PAD), lambda h, i: (0, h))
    kp = pl.BlockSpec((S, HEAD), lambda h, i: (0, 0))
    tab = pl.BlockSpec((ATT_TQ, HEAD), lambda h, i: (i, 0))
    o = pl.BlockSpec((ATT_TQ, HEAD), lambda h, i: (i, h))
    lse = pl.BlockSpec((None, ATT_TQ, 1), lambda h, i: (h, i, 0))
    return q, kv, kp, tab, o, lse


def attn_fwd(name, q, kv, kp, tabs):
    def body(q_ref, kv_ref, kp_ref, c_ref, s1_ref, s2_ref, o_ref, lse_ref):
        i = pl.program_id(1)

        def run(nk):
            sc, _, _, _ = _attn_scores(q_ref, kv_ref, kp_ref, c_ref, s1_ref, s2_ref, nk)
            m = jnp.max(sc, axis=-1, keepdims=True)
            e = jnp.exp(sc - m)
            total = jnp.sum(e, axis=-1, keepdims=True)
            o_ref[...] = (_dot(e, kv_ref[:nk, HEAD:], _NN) * (1.0 / total)).astype(o_ref.dtype)
            lse_ref[...] = m + jnp.log(total)

        _for_each_prefix(i, run)

    qs, kvs, kps, tab, os, ls = _attn_specs()
    return pl.pallas_call(
        body, name=name, grid=(MLA_HEADS, S // ATT_TQ), in_specs=[qs, kvs, kps, tab, tab, tab], out_specs=[os, ls],
        out_shape=[jax.ShapeDtypeStruct((S, MLA_HEADS * HEAD), BF16), jax.ShapeDtypeStruct((MLA_HEADS, S, 1), F32)],
        compiler_params=_params(("parallel", "parallel")),
    )(q, kv, kp, *tabs)


def attn_bwd(name, q, kv, kp, o, lse, do, tabs):
    def body(q_ref, kv_ref, kp_ref, o_ref, lse_ref, do_ref, c_ref, s1_ref, s2_ref, dq_ref, dkv_ref, dkp_ref):
        h, i = pl.program_id(0), pl.program_id(1)

        @pl.when(i == 0)
        def _():
            dkv_ref[...] = jnp.zeros_like(dkv_ref)

        @pl.when((i == 0) & (h == 0))
        def _():
            dkp_ref[...] = jnp.zeros_like(dkp_ref)

        def run(nk):
            sc, qn, qp, kn = _attn_scores(q_ref, kv_ref, kp_ref, c_ref, s1_ref, s2_ref, nk)
            p = jnp.exp(sc - lse_ref[...])
            do_v = do_ref[...]
            delta = jnp.sum(do_v.astype(F32) * o_ref[...].astype(F32), axis=-1, keepdims=True)
            dp = _dot(do_v, kv_ref[:nk, HEAD:], _NT)
            ds = (p * ((dp - delta) * ATT_SCALE)).astype(BF16)
            dq_ref[:, :HEAD] = _dot(ds, kn, _NN).astype(dq_ref.dtype)
            dqp = _unrope(_dot(ds, kp_ref[:nk, :], _NN), c_ref[...], s1_ref[...], s2_ref[...])
            dq_ref[:, HEAD:] = dqp.astype(dq_ref.dtype)
            dkv_ref[:nk, :HEAD] += _dot(ds, qn, _TN)
            dkv_ref[:nk, HEAD:] += _dot(p, do_v, _TN)
            dkp_ref[:nk, :] += _dot(ds, qp, _TN)

        _for_each_prefix(i, run)

    qs, kvs, kps, tab, os, ls = _attn_specs()
    return pl.pallas_call(
        body, name=name, grid=(MLA_HEADS, S // ATT_TQ), in_specs=[qs, kvs, kps, os, ls, os, tab, tab, tab],
        out_specs=[qs, kvs, kps],
        out_shape=[jax.ShapeDtypeStruct((S, MLA_HEADS * QHEAD_PAD), BF16),
                   jax.ShapeDtypeStruct((S, MLA_HEADS * QHEAD_PAD), F32), jax.ShapeDtypeStruct((S, HEAD), F32)],
        compiler_params=_params(("arbitrary", "arbitrary")),
    )(q, kv, kp, o, lse, do, *tabs)


def adamw(name, w, g, m, v):
    rows, cols = w.shape
    tr = rows
    for cand in (512, 256, 128, 64, 32, 16, 8):
        if rows % cand == 0 and cand * cols * 4 <= 2 * 1024 * 1024:
            tr = cand
            break

    def body(w_ref, g_ref, m_ref, v_ref, d_ref, nm_ref, nv_ref):
        g_v = g_ref[...]
        nm = ADAM_B1 * m_ref[...] + (1.0 - ADAM_B1) * g_v
        nv = ADAM_B2 * v_ref[...] + (1.0 - ADAM_B2) * (g_v * g_v)
        m_hat = nm / (1.0 - ADAM_B1 ** ADAM_STEP)
        v_hat = nv / (1.0 - ADAM_B2 ** ADAM_STEP)
        d_ref[...] = (-ADAM_LR) * (m_hat / (jnp.sqrt(v_hat) + ADAM_EPS) + ADAM_WD * w_ref[...])
        nm_ref[...] = nm
        nv_ref[...] = nv

    blk = pl.BlockSpec((tr, cols), lambda i: (i, 0))
    shape = jax.ShapeDtypeStruct((rows, cols), F32)
    return pl.pallas_call(
        body, name=name, grid=(rows // tr,), in_specs=[blk] * 4, out_specs=[blk] * 3, out_shape=[shape] * 3,
        compiler_params=_params(("parallel",)),
    )(w, g, m, v)


def _local_step(x, pos_col, target, sm, weights_of, emit_grads, prefetch):
    inv_freq = ROPE_THETA ** (-jnp.arange(0, QK_ROPE, 2, dtype=F32) / QK_ROPE)
    inv_freq = jnp.concatenate([inv_freq, inv_freq, jnp.zeros((HEAD - QK_ROPE,), F32)])[None, :]
    tabs = rope_tables(pos_col, inv_freq)
    saved, wts = [], {}
    for layer in range(DEPTH):
        j = layer // 2
        n = "l%d_" % layer
        sv = {"x": x}
        wm = wts["mix%d" % layer] = weights_of("mix%d" % layer, x)
        if layer == 0:
            sm = dict(sm, conv_w=wm["conv_w"], gq=wm["gq"], gkv=wm["gkv"])
        if layer % 2 == 0:
            proj = mm_cols(n + "proj", x, wm["w_in"], F32)
            fetched = prefetch("mlp%d" % layer, proj)
            y_pool = pool_fwd(n + "pool", proj, sm["pool_w"][j], sm["pool_scale"][j][None])
            y_lru, hstate = lru_fwd(n + "lru", proj, sm["conv_w"][j], sm["conv_b"][j][None], sm["w_a"][j],
                                    sm["b_a"][j][None], sm["w_x"][j], sm["b_x"][j][None], sm["lam"][j][None])
            ycat = jnp.concatenate([y_pool, y_lru], axis=1)
            mix_in, w_mix = ycat, wm["w_out"]
            sv.update(proj=proj, hstate=hstate, ycat=ycat)
        else:
            down = mm_rows(n + "down", x, wm["w_down"], F32)
            fetched = prefetch("mlp%d" % layer, down)
            cq, ckv, kp = mla_prep(n + "prep", down, sm["gq"][j][None], sm["gkv"][j][None], tabs)
            q = mm_cols(n + "q", cq, wm["w_qb"], F32)
            kv = mm_cols(n + "kv", ckv, wm["w_kvb"], BF16)
            o, lse = attn_fwd(n + "attn", q, kv, kp, tabs)
            mix_in, w_mix = o, wm["w_o"]
            sv.update(down=down, cq=cq, ckv=ckv, kp=kp, q=q, kv=kv, o=o, lse=lse)
        x1, xhat1, rstd1 = mm_rows(n + "mixout", mix_in, w_mix, F32, after=fetched,
                                   ln=(x, sm["ln_mix_g"][layer][None], sm["ln_mix_b"][layer][None]))
        wf = wts["mlp%d" % layer] = weights_of("mlp%d" % layer, x1)
        x2, xhat2, rstd2, relu = mlp_fwd(n + "mlp", x1, wf["w1"], wf["w2"], sm["ln_ffn_g"][layer][None],
                                         sm["ln_ffn_b"][layer][None])
        sv.update(xhat1=xhat1, rstd1=rstd1, x1=x1, relu=relu, xhat2=xhat2, rstd2=rstd2)
        saved.append(sv)
        x = x2

    gs = {k: [None] * (DEPTH if k.startswith("ln_") else DEPTH // 2) for k in sm}
    last = saved[DEPTH - 1]
    dz, gs["ln_ffn_g"][DEPTH - 1], gs["ln_ffn_b"][DEPTH - 1], loss_tile = loss_head(
        x, target, last["xhat2"], last["rstd2"], sm["ln_ffn_g"][DEPTH - 1][None])
    sent = None
    for layer in reversed(range(DEPTH)):
        j = layer // 2
        n = "l%d_" % layer
        sv = saved[layer]
        wm, wf = wts["mix%d" % layer], wts["mlp%d" % layer]
        dz, g_w1, g_w2, gs["ln_mix_g"][layer], gs["ln_mix_b"][layer] = mlp_bwd(
            n + "mlp_b", dz, sv["relu"], sv["x1"], wf["w1"], wf["w2"],
            (sv["xhat1"], sv["rstd1"], sm["ln_mix_g"][layer][None]), after=sent)
        sent = emit_grads("mlp%d" % layer, {"w1": g_w1, "w2": g_w2})
        below = saved[layer - 1] if layer else None
        ln_below = (below["xhat2"], below["rstd2"], sm["ln_ffn_g"][layer - 1][None]) if layer else None
        if layer % 2 == 0:
            dycat, g_out = linear_bwd(n + "mixout_b", sv["ycat"], dz, wm["w_out"], col_sharded=False, da_dtype=F32,
                                      after=sent)
            du_pool, gs["pool_w"][j], gs["pool_scale"][j] = pool_bwd(
                n + "pool_b", sv["proj"], dycat, sm["pool_w"][j], sm["pool_scale"][j][None])
            (du_lru, du_gate, gs["w_a"][j], gs["w_x"][j], gs["b_a"][j], gs["b_x"][j], gs["lam"][j], gs["conv_w"][j],
             gs["conv_b"][j]) = lru_bwd(n + "lru_b", sv["proj"], sv["hstate"], dycat, sm["conv_w"][j],
                                        sm["conv_b"][j][None], sm["w_a"][j], sm["b_a"][j][None], sm["w_x"][j],
                                        sm["b_x"][j][None], sm["lam"][j][None])
            dproj = jnp.concatenate([du_pool, du_lru, du_gate], axis=1)
            res = linear_bwd(n + "proj_b", sv["x"], dproj, wm["w_in"], col_sharded=True, da_dtype=F32, resid=dz,
                             ln=ln_below)
            grads = {"w_in": res[1], "w_out": g_out}
        else:
            do, g_o = linear_bwd(n + "attnout_b", sv["o"], dz, wm["w_o"], col_sharded=False, da_dtype=BF16, after=sent)
            dq, dkv, dkp = attn_bwd(n + "attn_b", sv["q"], sv["kv"], sv["kp"], sv["o"], sv["lse"], do, tabs)
            dcq, g_qb = linear_bwd(n + "q_b", sv["cq"], dq, wm["w_qb"], col_sharded=True, da_dtype=F32)
            dckv, g_kvb = linear_bwd(n + "kv_b", sv["ckv"], dkv, wm["w_kvb"], col_sharded=True, da_dtype=F32)
            ddown, gs["gq"][j], gs["gkv"][j] = mla_prep_bwd(
                n + "prep_b", sv["down"], dcq, dckv, dkp, sm["gq"][j][None], sm["gkv"][j][None], tabs)
            res = linear_bwd(n + "down_b", sv["x"], ddown, wm["w_down"], col_sharded=False, da_dtype=F32, resid=dz,
                             ln=ln_below)
            grads = {"w_down": res[1], "w_qb": g_qb, "w_kvb": g_kvb, "w_o": g_o}
        dz = res[0]
        if layer:
            gs["ln_ffn_g"][layer - 1], gs["ln_ffn_b"][layer - 1] = res[2], res[3]
        sent = emit_grads("mix%d" % layer, grads)
    gs = {k: jnp.stack([a.reshape(sm[k].shape[1:]) for a in v]) for k, v in gs.items()}
    return loss_tile[0, 0], dz, gs, sent


def _place():
    x, y, c = lax.axis_index("x"), lax.axis_index("y"), lax.axis_index("c")
    chips = [(1 - x, y), (x, 1 - y), (1 - x, 1 - y)]
    return x, y, c, chips


def _hbm_call(body, name, args, out_shape, scratch, aliases=None):
    return pl.pallas_call(
        body, name=name, in_specs=[pl.BlockSpec(memory_space=pl.ANY)] * len(args),
        out_specs=[pl.BlockSpec(memory_space=pl.ANY)] * len(out_shape), out_shape=out_shape,
        scratch_shapes=scratch, input_output_aliases=aliases or {},
        compiler_params=pltpu.CompilerParams(has_side_effects=True),
    )(*args)


HBM_SPEC = pl.BlockSpec(memory_space=pltpu.HBM)
SEM_SPEC = pl.BlockSpec(memory_space=pltpu.SEMAPHORE)
EFFECT = pltpu.SideEffectType.DATAFLOW_SIDE_EFFECTING


def _remote(src, dst, send_sem, recv_sem, device):
    return pltpu.make_async_remote_copy(src_ref=src, dst_ref=dst, send_sem=send_sem, recv_sem=recv_sem,
                                        device_id=device, device_id_type=MESH)


def place_own(name, srcs, place, steps=4):
    n = len(srcs)
    in_specs, out_specs, out_shape = [], [], []
    for s in srcs:
        rows, cols = s.shape
        tr = rows // steps
        in_specs.append(pl.BlockSpec((tr, cols), lambda i, p: (i, 0)))
        out_specs.append(pl.BlockSpec((None, tr, cols), lambda i, p: (p[0], i, 0)))
        out_shape.append(jax.ShapeDtypeStruct((N_CHIPS, rows, cols), s.dtype))

    def body(p_ref, *refs):
        for i_ref, o_ref in zip(refs[:n], refs[n:]):
            o_ref[...] = i_ref[...]

    return pl.pallas_call(
        body, name=name, out_shape=out_shape,
        grid_spec=pltpu.PrefetchScalarGridSpec(num_scalar_prefetch=1, grid=(steps,), in_specs=in_specs,
                                               out_specs=out_specs),
        compiler_params=_params(("parallel",)),
    )(place, *srcs)


def split_start(name, groups, n_sems, plan, token_in=None):
    sizes = [len(srcs) for srcs, _ in groups]
    n, n_groups = sum(sizes), len(groups)
    srcs = [pltpu.with_memory_space_constraint(a, pltpu.HBM) for s, _ in groups for a in s]
    lands = [pltpu.with_memory_space_constraint(a, pltpu.HBM) for _, l in groups for a in l]
    extra = [] if token_in is None else [token_in]

    def body(*refs):
        src_refs, land_refs = refs[:n], refs[n:2 * n]
        outs = refs[2 * n + len(extra):]
        off = 0
        for g, size in enumerate(sizes):
            sends, _ = plan(src_refs[off:off + size], land_refs[off:off + size], outs[2 * g], outs[2 * g + 1])
            for cp in sends:
                cp.start()
            off += size
        token = outs[2 * n_groups + 2 * n]
        token[...] = jnp.zeros_like(token)

    sems = [pltpu.SemaphoreType.DMA((size * n_sems,)) for size in sizes for _ in range(2)]
    res = pl.pallas_call(
        body, name=name,
        out_shape=(*sems, *[pltpu.HBM(a.shape, a.dtype) for a in srcs + lands], jax.ShapeDtypeStruct((8, 128), F32)),
        in_specs=[HBM_SPEC] * (2 * n) + [pl.BlockSpec(memory_space=pl.ANY)] * len(extra),
        out_specs=(*[SEM_SPEC] * len(sems), *[HBM_SPEC] * (2 * n), pl.BlockSpec(memory_space=pltpu.VMEM)),
        input_output_aliases={i: len(sems) + i for i in range(2 * n)},
        compiler_params=pltpu.CompilerParams(has_side_effects=EFFECT),
    )(*srcs, *lands, *extra)
    started, off = [], 0
    bufs = res[len(sems):]
    for g, size in enumerate(sizes):
        started.append(dict(send=res[2 * g], recv=res[2 * g + 1], srcs=list(bufs[off:off + size]),
                            lands=list(bufs[n + off:n + off + size]), plan=plan))
        off += size
    return started, res[-1]


def _wait_started(st, bufs, send_sems, recv_sems):
    n = len(st["srcs"])
    sends, expects = st["plan"](bufs[:n], bufs[n:], send_sems, recv_sems)
    for cp in sends:
        cp.wait_send()
    for cp in expects:
        cp.wait_recv()


def split_wait(name, started, after):
    sizes = [len(st["srcs"]) + len(st["lands"]) for st in started]
    n_buf = sum(sizes)

    def body(*refs):
        bufs, sems = refs[:n_buf], refs[n_buf:n_buf + 2 * len(started)]
        off = 0
        for g, (st, n) in enumerate(zip(started, sizes)):
            _wait_started(st, bufs[off:off + n], sems[2 * g], sems[2 * g + 1])
            off += n

    bufs = [a for st in started for a in st["srcs"] + st["lands"]]
    sems = [s for st in started for s in (st["send"], st["recv"])]
    res = pl.pallas_call(
        body, name=name, out_shape=tuple(pltpu.HBM(a.shape, a.dtype) for a in bufs),
        in_specs=[HBM_SPEC] * n_buf + [SEM_SPEC] * len(sems) + [pl.BlockSpec(memory_space=pl.ANY)],
        out_specs=tuple([HBM_SPEC] * n_buf), input_output_aliases={i: i for i in range(n_buf)},
        compiler_params=pltpu.CompilerParams(has_side_effects=EFFECT),
    )(*bufs, *sems, after)
    out, off = [], 0
    for st, n in zip(started, sizes):
        out.append((list(res[off:off + len(st["srcs"])]), list(res[off + len(st["srcs"]):off + n])))
        off += n
    return out


def split_relay(name, st, n_sems, plan, after):
    n_src, n = len(st["srcs"]), len(st["lands"])

    def body(*refs):
        bufs = refs[:n_src + n]
        outs = refs[n_src + n + 3:]
        _wait_started(st, bufs, refs[n_src + n], refs[n_src + n + 1])
        sends, _ = plan((), bufs[n_src:], outs[0], outs[1])
        for cp in sends:
            cp.start()
        outs[2 + n][...] = jnp.zeros((8, 128), F32)

    bufs = st["srcs"] + st["lands"]
    res = pl.pallas_call(
        body, name=name,
        out_shape=(pltpu.SemaphoreType.DMA((n * n_sems,)), pltpu.SemaphoreType.DMA((n * n_sems,)),
                   *[pltpu.HBM(a.shape, a.dtype) for a in st["lands"]], jax.ShapeDtypeStruct((8, 128), F32)),
        in_specs=[HBM_SPEC] * (n_src + n) + [SEM_SPEC] * 2 + [pl.BlockSpec(memory_space=pl.ANY)],
        out_specs=(SEM_SPEC, SEM_SPEC, *[HBM_SPEC] * n, pl.BlockSpec(memory_space=pltpu.VMEM)),
        input_output_aliases={n_src + i: 2 + i for i in range(n)},
        compiler_params=pltpu.CompilerParams(has_side_effects=EFFECT),
    )(*bufs, st["send"], st["recv"], after)
    return dict(send=res[0], recv=res[1], srcs=[], lands=list(res[2:2 + n]), plan=plan), res[-1]


def gather_plan(src_refs, land_refs, send_sems, recv_sems):
    x, y, c, chips = _place()
    me = 2 * x + y
    sends, expects = [], []
    for k, (s, d) in enumerate(zip(src_refs, land_refs)):
        mine = pl.ds(c * (s.shape[0] // 2), s.shape[0] // 2)
        for j, (px, py) in enumerate(chips):
            sem = 3 * k + j
            sends.append(_remote(s.at[mine], d.at[me, mine], send_sems.at[sem], recv_sems.at[sem], (px, py, c)))
            expects.append(_remote(s.at[mine], d.at[2 * px + py, mine], send_sems.at[sem], recv_sems.at[sem], (px, py, c)))
    return sends, expects


def relay_plan(src_refs, land_refs, send_sems, recv_sems):
    x, y, c, chips = _place()
    sends, expects = [], []
    for k, d in enumerate(land_refs):
        hr = d.shape[1] // 2
        mine, theirs = pl.ds(c * hr, hr), pl.ds((1 - c) * hr, hr)
        for j, (px, py) in enumerate(chips):
            sem, chip = 3 * k + j, 2 * px + py
            sends.append(_remote(d.at[chip, mine], d.at[chip, mine], send_sems.at[sem], recv_sems.at[sem], (x, y, 1 - c)))
            expects.append(_remote(d.at[chip, mine], d.at[chip, theirs], send_sems.at[sem], recv_sems.at[sem], (x, y, 1 - c)))
    return sends, expects


def _reduce_part(s, chip_idx, h):
    hr = s.shape[1] // 2
    return s.at[chip_idx if s.shape[0] == N_CHIPS else 0, pl.ds(h * hr, hr)]


def reduce_plan(src_refs, land_refs, send_sems, recv_sems):
    x, y, c, chips = _place()
    me_chip, me_dev = 2 * x + y, 4 * x + 2 * y + c
    sends, expects = [], []
    for k, (s, d) in enumerate(zip(src_refs, land_refs)):
        part = functools.partial(_reduce_part, s)
        for j, (px, py) in enumerate(chips):
            for h in range(2):
                sends.append(_remote(part(2 * px + py, h), d.at[me_dev], send_sems.at[7 * k + 2 * j + h],
                                     recv_sems.at[7 * k + 2 * j + c], (px, py, h)))
                expects.append(_remote(part(me_chip, c), d.at[4 * px + 2 * py + h], send_sems.at[7 * k + 2 * j + h],
                                       recv_sems.at[7 * k + 2 * j + h], (px, py, h)))
        sends.append(_remote(part(me_chip, 1 - c), d.at[me_dev], send_sems.at[7 * k + 6], recv_sems.at[7 * k + 6],
                             (x, y, 1 - c)))
        expects.append(_remote(part(me_chip, c), d.at[me_dev + 1 - 2 * c], send_sems.at[7 * k + 6],
                               recv_sems.at[7 * k + 6], (x, y, 1 - c)))
    return sends, expects


def sibling_swap_halves(name, fulls):
    n = len(fulls)

    def body(*refs):
        outs = refs[n:2 * n]
        send_sems, recv_sems = refs[2 * n:]
        x, y, c, _ = _place()
        copies = []
        for k in range(n):
            nl, rows = fulls[k].shape[:2]
            hr = rows // 2
            mine = outs[k].at[pl.ds(0, nl), pl.ds(c * hr, hr)]
            theirs = outs[k].at[pl.ds(0, nl), pl.ds((1 - c) * hr, hr)]
            copies.append((_remote(mine, mine, send_sems.at[k], recv_sems.at[k], (x, y, 1 - c)),
                           _remote(mine, theirs, send_sems.at[k], recv_sems.at[k], (x, y, 1 - c))))
        for send, _ in copies:
            send.start()
        for send, recv in copies:
            send.wait_send()
            recv.wait_recv()

    out_shape = [jax.ShapeDtypeStruct(f.shape, f.dtype) for f in fulls]
    scratch = [pltpu.SemaphoreType.DMA((n,)), pltpu.SemaphoreType.DMA((n,))]
    return _hbm_call(body, name, fulls, out_shape, scratch, aliases={k: k for k in range(n)})


N_DEV = 8


def sum_devices(name, items, place, steps):
    n = len(items)
    in_specs, args, out_specs, out_shape, aliases = [], [place], [], [], {}
    for landed, own, layer, n_layers, _ in items:
        _, hr, cols = landed.shape
        tr = hr // steps
        slot = (lambda p: p[0]) if own.shape[0] == N_CHIPS else (lambda p: 0)
        in_specs += [pl.BlockSpec((N_DEV, tr, cols), lambda r, p: (0, r, 0)),
                     pl.BlockSpec((None, tr, cols), lambda r, p, slot=slot: (slot(p), p[1] * steps + r, 0))]
        args += [landed, own]
        out_specs.append(pl.BlockSpec((None, tr, cols), lambda r, p, layer=layer: (layer, p[1] * steps + r, 0)))
        out_shape.append(jax.ShapeDtypeStruct((n_layers, 2 * hr, cols), F32))
    for k, item in enumerate(items):
        if item[4] is not None:
            in_specs.append(pl.BlockSpec(memory_space=pl.ANY))
            aliases[len(args)] = k
            args.append(item[4])
    n_prev = len(aliases)

    def body(p_ref, *refs):
        for k in range(n):
            r_ref, own_ref, o_ref = refs[2 * k], refs[2 * k + 1], refs[2 * n + n_prev + k]
            mine = own_ref[...].astype(F32)
            acc = jnp.zeros_like(mine)
            for d in range(N_DEV):
                acc = acc + jnp.where(p_ref[2] == d, mine, r_ref[d].astype(F32))
            o_ref[...] = acc

    return pl.pallas_call(
        body, name=name, out_shape=out_shape,
        grid_spec=pltpu.PrefetchScalarGridSpec(num_scalar_prefetch=1, grid=(steps,), in_specs=in_specs,
                                               out_specs=out_specs),
        input_output_aliases=aliases, compiler_params=_params(("parallel",)),
    )(*args)


def _pack(arrs, rows_multiple):
    flat = []
    for a in arrs:
        v = a.reshape(-1).astype(F32)
        flat.append(jnp.pad(v, (0, (-v.shape[0]) % HEAD)))
    v = jnp.concatenate(flat)
    v = jnp.pad(v, (0, (-v.shape[0]) % (HEAD * rows_multiple)))
    return v.reshape(-1, HEAD)


def _unpack(packed, shapes):
    flat = packed.reshape(-1)
    out, off = [], 0
    for shp in shapes:
        size = int(np.prod(shp))
        out.append(flat[off:off + size].reshape(shp))
        off += size + (-size) % HEAD
    return out


BIG = ["even_w_in", "even_w_out", "mla_w_down", "mla_w_qb", "mla_w_kvb", "mla_w_o", "mlp_w1", "mlp_w2"]
BIG_KEY = {"even_w_in": "w_in", "even_w_out": "w_out", "mla_w_down": "w_down", "mla_w_qb": "w_qb",
           "mla_w_kvb": "w_kvb", "mla_w_o": "w_o", "mlp_w1": "w1", "mlp_w2": "w2"}
SMALL_KEY = {"ln_mix_g": "ln_mix_g", "ln_mix_b": "ln_mix_b", "ln_ffn_g": "ln_ffn_g", "ln_ffn_b": "ln_ffn_b",
             "pool_w": "pool_w", "pool_scale": "pool_scale", "lru_conv_w": "conv_w", "lru_conv_b": "conv_b",
             "lru_w_a": "w_a", "lru_b_a": "b_a", "lru_w_x": "w_x", "lru_b_x": "b_x", "lru_lambda": "lam",
             "mla_q_norm_g": "gq", "mla_kv_norm_g": "gkv"}
SMALL = list(SMALL_KEY)
SMALL_SHARDED = ["lru_conv_w", "mla_q_norm_g", "mla_kv_norm_g"]
SMALL_MATRICES = ["pool_w", "lru_w_a", "lru_w_x"]
WEIGHTS = ["ln_mix_g", "ln_mix_b", "ln_ffn_g", "ln_ffn_b", "even_w_in", "pool_w", "pool_scale", "lru_conv_w",
           "lru_conv_b", "lru_w_a", "lru_b_a", "lru_w_x", "lru_b_x", "lru_lambda", "even_w_out", "mla_w_down",
           "mla_q_norm_g", "mla_kv_norm_g", "mla_w_qb", "mla_w_kvb", "mla_w_o", "mlp_w1", "mlp_w2"]


GROUPS = ["mix0", "mlp0", "mix1", "mlp1", "mix2", "mlp2", "mix3", "mlp3"]


def _group_keys(group):
    layer = int(group[3:])
    if group.startswith("mlp"):
        return [("mlp_w1", "w1", layer), ("mlp_w2", "w2", layer)]
    if layer % 2 == 0:
        return [("even_w_in", "w_in", layer // 2), ("even_w_out", "w_out", layer // 2)]
    return [("mla_w_down", "w_down", layer // 2), ("mla_w_qb", "w_qb", layer // 2),
            ("mla_w_kvb", "w_kvb", layer // 2), ("mla_w_o", "w_o", layer // 2)]


def _pad_q_heads(w):
    lead = w.shape[:-1]
    w = w.reshape(lead + (2, QK_NOPE + QK_ROPE))
    w = jnp.pad(w, ((0, 0),) * len(lead) + ((0, 0), (0, QHEAD_PAD - QK_NOPE - QK_ROPE)))
    return w.reshape(lead + (2 * QHEAD_PAD,))


def _unpad_q_heads(g):
    lead = g.shape[:-1]
    return g.reshape(lead + (2, QHEAD_PAD))[..., :QK_NOPE + QK_ROPE].reshape(lead + (2 * (QK_NOPE + QK_ROPE),))


def _step(x, positions, loss_target, w, m, v):
    cx, cy, cc = lax.axis_index("x"), lax.axis_index("y"), lax.axis_index("c")
    chip = 2 * cx + cy
    place = jnp.stack([chip, cc, 2 * chip + cc]).astype(jnp.int32)

    prepared = dict(w)
    prepared["mla_w_down"] = jnp.pad(w["mla_w_down"], ((0, 0), (0, 0), (0, ODD_IN_PAD - ODD_IN)))
    prepared["mla_w_qb"] = _pad_q_heads(w["mla_w_qb"])
    small_shard_shapes = [w[k].shape for k in SMALL_SHARDED]
    sources = {g: [prepared[name][idx].astype(BF16) for name, _, idx in _group_keys(g)] for g in GROUPS}
    sources[GROUPS[0]].append(_pack([w[k] for k in SMALL_SHARDED], 32))
    gathering, token = {}, None
    for name, part in (("first", GROUPS[:1]), ("rest", GROUPS[1:])):
        zones = place_own("own_" + name, [a for g in part for a in sources[g]], place)
        groups, off = [], 0
        for g in part:
            groups.append((sources[g], zones[off:off + len(sources[g])]))
            off += len(sources[g])
        started, token = split_start("gather_" + name, groups, 3, gather_plan, token)
        gathering.update(zip(part, started))
    relayed, relay_token = {}, {}

    def prefetch(g, after):
        relayed[g], relay_token[g] = split_relay("relay_" + g, gathering[g], 3, relay_plan, after)
        return relay_token[g]

    prefetch(GROUPS[0], token)

    def weights_of(g, after):
        _, lands = split_wait("gathered_" + g, [relayed[g]], relay_token[g] if g == GROUPS[0] else after)[0]
        if g.startswith("mlp") and g != GROUPS[-1]:
            prefetch(GROUPS[GROUPS.index(g) + 1], lands[0])
        out = {key: land for (_, key, _), land in zip(_group_keys(g), lands)}
        if g == GROUPS[0]:
            per_chip = [_unpack(lands[-1][s], small_shard_shapes) for s in range(N_CHIPS)]
            for i, key in enumerate(("conv_w", "gq", "gkv")):
                out[key] = jnp.concatenate([p[i] for p in per_chip], axis=-1)
        return out

    reducing = []

    def reduce_start(g, srcs, token_in=None):
        lands = [lax.empty((N_DEV, s.shape[1] // 2, s.shape[2]), s.dtype) for s in srcs]
        started, token = split_start("reduce_" + g, [(srcs, lands)], 7, reduce_plan, token_in)
        reducing.append((g, started[0]))
        return token

    def emit_grads(g, grads):
        return reduce_start(g, [grads[key] for _, key, _ in _group_keys(g)])

    sm = {SMALL_KEY[k]: w[k] for k in SMALL if k not in SMALL_SHARDED}
    loss, grad_x, gs, last_sent = _local_step(x[0], positions.reshape(S, 1), loss_target[0], sm, weights_of,
                                              emit_grads, prefetch)
    loss = lax.psum(loss, ("x", "y", "c"))

    packs = [_pack([gs[SMALL_KEY[k]] for k in SMALL if (k in SMALL_MATRICES) == mat], 128)[None] for mat in (False, True)]
    small_sent = reduce_start("small", [packs[0], packs[1].astype(BF16)], last_sent)

    grad, delta, new_m, new_v = {}, {}, {}, {}
    late_groups = ("mix0", "small")
    stacks, after = {}, small_sent
    for late in (False, True):
        part = [(g, st) for g, st in reducing if (g in late_groups) == late]
        landed = split_wait("reduced_late" if late else "reduced_early", [st for _, st in part], after)
        for (g, _), (owns, lands) in zip(part, landed):
            if g == "small":
                items = [(land, own, 0, 1, None) for own, land in zip(owns, lands)]
                stacks["small_vec"], stacks["small_mat"] = sum_devices("sum_small", items, place, 4)
                continue
            keys = _group_keys(g)
            items = [(land, own, idx, w[name].shape[0], stacks.get(name))
                     for (name, _, idx), own, land in zip(keys, owns, lands)]
            for (name, _, _), stack in zip(keys, sum_devices("sum_" + g, items, place, 4)):
                stacks[name] = stack
        names = [k for k in BIG if (k in ("even_w_in", "even_w_out")) == late] + (["small_vec", "small_mat"] if late else [])
        reduced = dict(zip(names, sibling_swap_halves("swap_late" if late else "swap_early", [stacks[k] for k in names])))
        if not late:
            reduced["mla_w_down"] = reduced["mla_w_down"][..., :ODD_IN]
            reduced["mla_w_qb"] = _unpad_q_heads(reduced["mla_w_qb"])
        for k in names:
            if k.startswith("small"):
                continue
            grad[k] = reduced[k]
            shp = w[k].shape
            view = lambda a: a.reshape(-1, shp[-1])
            d, nm, nv = adamw("adamw_" + BIG_KEY[k], view(w[k]), view(grad[k]), view(m[k]), view(v[k]))
            delta[k], new_m[k], new_v[k] = d.reshape(shp), nm.reshape(shp), nv.reshape(shp)
            after = d
    g_small = {}
    for mat, key in ((False, "small_vec"), (True, "small_mat")):
        names = [k for k in SMALL if (k in SMALL_MATRICES) == mat]
        g_small.update(zip(names, _unpack(reduced[key], [gs[SMALL_KEY[k]].shape for k in names])))
    for k in SMALL_SHARDED:
        width = w[k].shape[-1]
        g_small[k] = lax.dynamic_slice_in_dim(g_small[k], chip * width, width, axis=-1)
    grad.update(g_small)

    shapes = [w[k].shape for k in SMALL]
    d, nm, nv = adamw("adamw_small", *[_pack([t[k] for k in SMALL], 512) for t in (w, grad, m, v)])
    for k, dk, mk, vk in zip(SMALL, _unpack(d, shapes), _unpack(nm, shapes), _unpack(nv, shapes)):
        delta[k], new_m[k], new_v[k] = dk, mk, vk
    return (loss, grad_x[None], *[grad[k] for k in WEIGHTS], *[delta[k] for k in WEIGHTS],
            *[new_m[k] for k in WEIGHTS], *[new_v[k] for k in WEIGHTS])


def kernel(x, positions, ln_mix_g, ln_mix_b, ln_ffn_g, ln_ffn_b, even_w_in, pool_w, pool_scale, lru_conv_w, lru_conv_b, lru_w_a, lru_b_a, lru_w_x, lru_b_x, lru_lambda, even_w_out, mla_w_down, mla_q_norm_g, mla_kv_norm_g, mla_w_qb, mla_w_kvb, mla_w_o, mlp_w1, mlp_w2, loss_target, m_ln_mix_g, m_ln_mix_b, m_ln_ffn_g, m_ln_ffn_b, m_even_w_in, m_pool_w, m_pool_scale, m_lru_conv_w, m_lru_conv_b, m_lru_w_a, m_lru_b_a, m_lru_w_x, m_lru_b_x, m_lru_lambda, m_even_w_out, m_mla_w_down, m_mla_q_norm_g, m_mla_kv_norm_g, m_mla_w_qb, m_mla_w_kvb, m_mla_w_o, m_mlp_w1, m_mlp_w2, v_ln_mix_g, v_ln_mix_b, v_ln_ffn_g, v_ln_ffn_b, v_even_w_in, v_pool_w, v_pool_scale, v_lru_conv_w, v_lru_conv_b, v_lru_w_a, v_lru_b_a, v_lru_w_x, v_lru_b_x, v_lru_lambda, v_even_w_out, v_mla_w_down, v_mla_q_norm_g, v_mla_kv_norm_g, v_mla_w_qb, v_mla_w_kvb, v_mla_w_o, v_mlp_w1, v_mlp_w2):
    args = locals()
    w = {k: args[k] for k in WEIGHTS}
    m = {k: args["m_" + k] for k in WEIGHTS}
    v = {k: args["v_" + k] for k in WEIGHTS}
    return _step(x, positions, loss_target, w, m, v)
```

```python
import functools
import math

import jax
import jax.numpy as jnp
import numpy as np
from jax import lax
from jax.experimental import pallas as pl
from jax.experimental.pallas import tpu as pltpu

F32 = jnp.float32
BF16 = jnp.bfloat16

S = 2048
D = 1024
DEPTH = 4
N_CHIPS = 4
POOL_WINDOWS = (2, 4, 8, 16)
POOL_W = 512
LRU_W = 1024
LRU_HEADS = 8
HEAD = 128
EVEN_IN = 2560
EVEN_MIX = 1536
MLA_HEADS = 8
QK_NOPE = 128
QK_ROPE = 64
Q_RANK = 384
KV_RANK = 256
ODD_IN = 704
ODD_IN_PAD = 768
QHEAD_PAD = 256
D_FF = 4096
CHUNK = 64
ALPHA = (2 * DEPTH) ** 0.25
LN_EPS = 1e-5
RMS_EPS = 1e-6
ATT_SCALE = (QK_NOPE + QK_ROPE) ** -0.5
ROPE_THETA = 10000.0
LRU_C = 8.0
ADAM_LR = 0.001
ADAM_B1 = 0.9
ADAM_B2 = 0.999
ADAM_EPS = 1e-08
ADAM_WD = 0.01
ADAM_STEP = 10

VMEM_LIMIT = 56 * 1024 * 1024
MESH = pl.DeviceIdType.MESH

_NN = (((1,), (0,)), ((), ()))
_NT = (((1,), (1,)), ((), ()))
_TN = (((0,), (0,)), ((), ()))


def _params(sem=None, **kw):
    return pltpu.CompilerParams(dimension_semantics=sem, vmem_limit_bytes=VMEM_LIMIT, **kw)


def _dot(a, b, dims):
    return lax.dot_general(a.astype(BF16), b.astype(BF16), dims, preferred_element_type=F32)


def _whole(w4):
    return pl.BlockSpec(tuple(w4.shape), lambda i: (0, 0, 0))


def mm_cols(name, a, w4, out_dtype, *, tm=512):
    m, k = a.shape
    nb = w4.shape[2]

    def body(a_ref, w_ref, o_ref):
        a_v = a_ref[...].astype(BF16)
        for s in range(N_CHIPS):
            o_ref[:, s * nb:(s + 1) * nb] = _dot(a_v, w_ref[s], _NN).astype(o_ref.dtype)

    return pl.pallas_call(
        body, name=name, grid=(m // tm,), in_specs=[pl.BlockSpec((tm, k), lambda i: (i, 0)), _whole(w4)],
        out_specs=pl.BlockSpec((tm, N_CHIPS * nb), lambda i: (i, 0)),
        out_shape=jax.ShapeDtypeStruct((m, N_CHIPS * nb), out_dtype), compiler_params=_params(("parallel",)),
    )(a, w4)


def mm_rows(name, a, w4, out_dtype, *, tm=512, ln=None, after=None):
    m = a.shape[0]
    kb, n = w4.shape[1:]
    n_after = 0 if after is None else 1

    def body(a_ref, w_ref, *rest):
        rest = rest[n_after:]
        acc = _dot(a_ref[...], w_ref[...].reshape(N_CHIPS * kb, n), _NN)
        if ln is None:
            rest[0][...] = acc.astype(rest[0].dtype)
        else:
            x_ref, g_ref, b_ref, y_ref, xhat_ref, rstd_ref = rest
            y_ref[...], xhat_ref[...], rstd_ref[...] = _layer_norm(ALPHA * x_ref[...] + acc, g_ref[...], b_ref[...])

    row = pl.BlockSpec((tm, n), lambda i: (i, 0))
    vec = pl.BlockSpec((1, n), lambda i: (0, 0))
    in_specs = [pl.BlockSpec((tm, N_CHIPS * kb), lambda i: (i, 0)), _whole(w4)] + [pl.BlockSpec(memory_space=pl.ANY)] * n_after
    if ln is None:
        extras, out_specs, out_shape = (), row, jax.ShapeDtypeStruct((m, n), out_dtype)
    else:
        extras, in_specs = ln, in_specs + [row, vec, vec]
        out_specs = [row, row, pl.BlockSpec((tm, 1), lambda i: (i, 0))]
        out_shape = [jax.ShapeDtypeStruct((m, n), F32), jax.ShapeDtypeStruct((m, n), F32), jax.ShapeDtypeStruct((m, 1), F32)]
    return pl.pallas_call(
        body, name=name, grid=(m // tm,), in_specs=in_specs, out_specs=out_specs, out_shape=out_shape,
        compiler_params=_params(("parallel",)),
    )(a, w4, *([after] * n_after), *extras)


def _ln_bwd_rows(dy, xhat, rstd, g):
    dxh = dy * g
    m1 = jnp.mean(dxh, axis=-1, keepdims=True)
    m2 = jnp.mean(dxh * xhat, axis=-1, keepdims=True)
    dz = rstd * (dxh - m1 - xhat * m2)
    return dz, jnp.sum(dy * xhat, axis=0, keepdims=True), jnp.sum(dy, axis=0, keepdims=True)


def linear_bwd(name, a, g, w4, *, col_sharded, da_dtype, resid=None, ln=None, tm=512, after=None):
    m = a.shape[0]
    n_red = m // tm
    ka, ng = w4.shape[1:]
    k_all = a.shape[1]
    n_after = 0 if after is None else 1
    n_resid = 0 if resid is None else 1
    n_ln = 0 if ln is None else 3

    def body(a_ref, g_ref, w_ref, *rest):
        rest = rest[n_after:]
        ln_refs = rest[n_resid:n_resid + n_ln]
        da_ref, dw_ref = rest[n_resid + n_ln:n_resid + n_ln + 2]
        acc_ref = rest[-1]
        k = pl.program_id(0)

        @pl.when(k == 0)
        def _():
            acc_ref[...] = jnp.zeros_like(acc_ref)

        a_v, g_v = a_ref[...].astype(BF16), g_ref[...].astype(BF16)
        extra = ALPHA * rest[0][...] if n_resid else None
        if col_sharded:
            da = extra
            for s in range(N_CHIPS):
                g_s = g_v[:, s * ng:(s + 1) * ng]
                p = _dot(g_s, w_ref[s], _NT)
                da = p if da is None else da + p
                acc_ref[s] += _dot(a_v, g_s, _TN)
        else:
            parts = []
            for s in range(N_CHIPS):
                parts.append(_dot(g_v, w_ref[s], _NT))
                acc_ref[s] += _dot(a_v[:, s * ka:(s + 1) * ka], g_v, _TN)
            da = jnp.concatenate(parts, axis=1)
            da = da if extra is None else da + extra
        if n_ln:
            dg_ref, db_ref = rest[n_resid + n_ln + 2:n_resid + n_ln + 4]
            da, pg, pb = _ln_bwd_rows(da, ln_refs[0][...], ln_refs[1][...], ln_refs[2][...])

            @pl.when(k == 0)
            def _():
                dg_ref[...] = jnp.zeros_like(dg_ref)
                db_ref[...] = jnp.zeros_like(db_ref)

            dg_ref[...] += pg
            db_ref[...] += pb
        da_ref[...] = da.astype(da_ref.dtype)

        @pl.when(k == n_red - 1)
        def _():
            dw_ref[...] = acc_ref[...].astype(dw_ref.dtype)

    row = lambda width: pl.BlockSpec((tm, width), lambda i: (i, 0))
    vec = pl.BlockSpec((1, k_all), lambda i: (0, 0))
    whole = pl.BlockSpec((N_CHIPS, ka, ng), lambda i: (0, 0, 0))
    out_specs = [row(k_all), whole] + [vec, vec] * (n_ln // 3)
    out_shape = [jax.ShapeDtypeStruct((m, k_all), da_dtype), jax.ShapeDtypeStruct((N_CHIPS, ka, ng), BF16)]
    out_shape += [jax.ShapeDtypeStruct((1, k_all), F32)] * (2 * (n_ln // 3))
    return pl.pallas_call(
        body, name=name, grid=(n_red,),
        in_specs=[row(k_all), row(g.shape[1]), whole] + [pl.BlockSpec(memory_space=pl.ANY)] * n_after
        + [row(k_all)] * n_resid + ([row(k_all), row(1), vec] if n_ln else []),
        out_specs=out_specs, out_shape=out_shape,
        scratch_shapes=[pltpu.VMEM((N_CHIPS, ka, ng), F32)], compiler_params=_params(("arbitrary",)),
    )(a, g, w4, *([after] * n_after), *([resid] * n_resid), *(ln or ()))


def _layer_norm(z, g, b):
    mu = jnp.mean(z, axis=-1, keepdims=True)
    zc = z - mu
    rstd = lax.rsqrt(jnp.mean(zc * zc, axis=-1, keepdims=True) + LN_EPS)
    xhat = zc * rstd
    return xhat * g + b, xhat, rstd


def mlp_fwd(name, x, w1, w2, g, b, *, tm=256):
    fb = w1.shape[2]
    f_all = N_CHIPS * fb

    def body(x_ref, w1_ref, w2_ref, g_ref, b_ref, y_ref, xhat_ref, rstd_ref, relu_ref):
        x_b = x_ref[...].astype(BF16)
        acts = []
        for s in range(N_CHIPS):
            r = jnp.maximum(_dot(x_b, w1_ref[s], _NN), 0.0)
            relu_ref[:, s * fb:(s + 1) * fb] = r.astype(relu_ref.dtype)
            acts.append((r * r).astype(BF16))
        mlp = _dot(jnp.concatenate(acts, axis=1), w2_ref[...].reshape(f_all, D), _NN)
        y_ref[...], xhat_ref[...], rstd_ref[...] = _layer_norm(ALPHA * x_ref[...] + mlp, g_ref[...], b_ref[...])

    row = pl.BlockSpec((tm, D), lambda i: (i, 0))
    vec = pl.BlockSpec((1, D), lambda i: (0, 0))
    once = pl.Buffered(1)
    return pl.pallas_call(
        body, name=name, grid=(S // tm,),
        in_specs=[row, pl.BlockSpec(tuple(w1.shape), lambda i: (0, 0, 0), pipeline_mode=once),
                  pl.BlockSpec(tuple(w2.shape), lambda i: (0, 0, 0), pipeline_mode=once), vec, vec],
        out_specs=[row, row, pl.BlockSpec((tm, 1), lambda i: (i, 0)), pl.BlockSpec((tm, f_all), lambda i: (i, 0))],
        out_shape=[jax.ShapeDtypeStruct((S, D), F32), jax.ShapeDtypeStruct((S, D), F32),
                   jax.ShapeDtypeStruct((S, 1), F32), jax.ShapeDtypeStruct((S, f_all), BF16)],
        compiler_params=_params(("parallel",)),
    )(x, w1, w2, g, b)


def mlp_bwd(name, dz, relu, x, w1, w2, ln, *, tm=256, after=None):
    fb = w1.shape[2]
    n_i = S // tm
    n_after = 0 if after is None else 1

    def body(dz_ref, relu_ref, x_ref, w1_ref, w2_ref, xhat_ref, rstd_ref, gln_ref, *rest):
        dx_ref, g1_ref, g2_ref, dg_ref, db_ref, acc1_ref, acc2_ref = rest[n_after:]
        s, i = pl.program_id(0), pl.program_id(1)
        rows = pl.ds(pl.multiple_of(i * tm, tm), tm)
        dz_v = dz_ref[...]

        @pl.when(i == 0)
        def _():
            acc1_ref[...] = jnp.zeros_like(acc1_ref)
            acc2_ref[...] = jnp.zeros_like(acc2_ref)

        @pl.when(s == 0)
        def _():
            dx_ref[rows, :] = ALPHA * dz_v

        @pl.when((s == 0) & (i == 0))
        def _():
            dg_ref[...] = jnp.zeros_like(dg_ref)
            db_ref[...] = jnp.zeros_like(db_ref)

        dz_b = dz_v.astype(BF16)
        r = relu_ref[...]
        dh = (_dot(dz_b, w2_ref[...], _NT) * (2.0 * r.astype(F32))).astype(BF16)
        p2 = _dot(r * r, dz_b, _TN)
        p1 = _dot(x_ref[...], dh, _TN)
        dx_ref[rows, :] += _dot(dh, w1_ref[...], _NT)
        acc1_ref[...] += p1
        acc2_ref[...] += p2

        @pl.when(i == n_i - 1)
        def _():
            g1_ref[...] = acc1_ref[...].astype(g1_ref.dtype)
            g2_ref[...] = acc2_ref[...].astype(g2_ref.dtype)

        @pl.when(s == N_CHIPS - 1)
        def _():
            dzx, pg, pb = _ln_bwd_rows(dx_ref[rows, :], xhat_ref[...], rstd_ref[...], gln_ref[...])
            dx_ref[rows, :] = dzx
            dg_ref[...] += pg
            db_ref[...] += pb

    row = pl.BlockSpec((tm, D), lambda s, i: (i, 0))
    last_only = lambda s, i: (jnp.where(s == N_CHIPS - 1, i, 0), 0)
    vec = pl.BlockSpec((1, D), lambda s, i: (0, 0))
    return pl.pallas_call(
        body, name=name, grid=(N_CHIPS, n_i),
        in_specs=[row, pl.BlockSpec((tm, fb), lambda s, i: (i, s)), row,
                  pl.BlockSpec((None, D, fb), lambda s, i: (s, 0, 0)), pl.BlockSpec((None, fb, D), lambda s, i: (s, 0, 0)),
                  pl.BlockSpec((tm, D), last_only), pl.BlockSpec((tm, 1), last_only), vec]
        + [pl.BlockSpec(memory_space=pl.ANY)] * n_after,
        out_specs=[pl.BlockSpec((S, D), lambda s, i: (0, 0)), pl.BlockSpec((None, D, fb), lambda s, i: (s, 0, 0)),
                   pl.BlockSpec((None, fb, D), lambda s, i: (s, 0, 0)), vec, vec],
        out_shape=[jax.ShapeDtypeStruct((S, D), F32), jax.ShapeDtypeStruct((N_CHIPS, D, fb), BF16),
                   jax.ShapeDtypeStruct((N_CHIPS, fb, D), BF16), jax.ShapeDtypeStruct((1, D), F32),
                   jax.ShapeDtypeStruct((1, D), F32)],
        scratch_shapes=[pltpu.VMEM((D, fb), F32), pltpu.VMEM((fb, D), F32)],
        compiler_params=_params(("arbitrary", "arbitrary")),
    )(dz, relu, x, w1, w2, *ln, *([after] * n_after))


def loss_head(y, target, xhat, rstd, g, *, tm=256):
    def body(y_ref, t_ref, xhat_ref, rstd_ref, g_ref, dz_ref, dg_ref, db_ref, loss_ref):
        i = pl.program_id(0)

        @pl.when(i == 0)
        def _():
            loss_ref[...] = jnp.zeros_like(loss_ref)
            dg_ref[...] = jnp.zeros_like(dg_ref)
            db_ref[...] = jnp.zeros_like(db_ref)

        e = y_ref[...] - t_ref[...]
        part = jnp.sum(jnp.sum(e * e, axis=-1, keepdims=True), axis=0, keepdims=True) * (0.5 / D)
        loss_ref[...] += jnp.broadcast_to(part, loss_ref.shape)
        dz_ref[...], pg, pb = _ln_bwd_rows(e * (1.0 / D), xhat_ref[...], rstd_ref[...], g_ref[...])
        dg_ref[...] += pg
        db_ref[...] += pb

    row = pl.BlockSpec((tm, D), lambda i: (i, 0))
    vec = pl.BlockSpec((1, D), lambda i: (0, 0))
    return pl.pallas_call(
        body, name="loss_head", grid=(S // tm,),
        in_specs=[row, row, row, pl.BlockSpec((tm, 1), lambda i: (i, 0)), vec],
        out_specs=[row, vec, vec, pl.BlockSpec((8, 128), lambda i: (0, 0))],
        out_shape=[jax.ShapeDtypeStruct((S, D), F32), jax.ShapeDtypeStruct((1, D), F32),
                   jax.ShapeDtypeStruct((1, D), F32), jax.ShapeDtypeStruct((8, 128), F32)],
        compiler_params=_params(("arbitrary",)),
    )(y, target, xhat, rstd, g)


def _rows(shape):
    return lax.broadcasted_iota(jnp.int32, shape, 0)


def _shift_down(x, k):
    return jnp.where(_rows(x.shape) >= k, pltpu.roll(x, k, 0), 0.0)


def _shift_up(x, k):
    n = x.shape[0]
    return jnp.where(_rows(x.shape) < n - k, pltpu.roll(x, n - k, 0), 0.0)


def _pool_diff(u, w):
    acc, k = u, 1
    while k < w:
        acc = acc + _shift_down(acc, k)
        k *= 2
    cnt = jnp.minimum(_rows(u.shape) + 1, w).astype(F32)
    return acc / cnt - u, cnt


def pool_fwd(name, proj, pool_w, pool_scale):
    def body(u_ref, w_ref, sc_ref, y_ref):
        for g, w in enumerate(POOL_WINDOWS):
            cols = slice(g * HEAD, (g + 1) * HEAD)
            d, _ = _pool_diff(u_ref[:, cols], w)
            z = _dot(d, w_ref[g], _NN)
            y_ref[:, cols] = (z * sc_ref[:, cols]).astype(y_ref.dtype)

    return pl.pallas_call(
        body, name=name, grid=(1,),
        in_specs=[pl.BlockSpec((S, POOL_W), lambda i: (0, 0)),
                  pl.BlockSpec((4, HEAD, HEAD), lambda i: (0, 0, 0)),
                  pl.BlockSpec((1, POOL_W), lambda i: (0, 0))],
        out_specs=pl.BlockSpec((S, POOL_W), lambda i: (0, 0)),
        out_shape=jax.ShapeDtypeStruct((S, POOL_W), BF16),
        compiler_params=_params(("arbitrary",)),
    )(proj, pool_w, pool_scale)


def pool_bwd(name, proj, dycat, pool_w, pool_scale):
    def body(u_ref, dy_ref, w_ref, sc_ref, du_ref, dw_ref, dsc_ref):
        for g, w in enumerate(POOL_WINDOWS):
            cols = slice(g * HEAD, (g + 1) * HEAD)
            d, cnt = _pool_diff(u_ref[:, cols], w)
            dy = dy_ref[:, cols]
            z = _dot(d, w_ref[g], _NN)
            dsc_ref[:, cols] = jnp.sum(dy * z, axis=0, keepdims=True)
            dz = dy * sc_ref[:, cols]
            dw_ref[g] = _dot(d, dz, _TN)
            dd = _dot(dz, w_ref[g], _NT)
            acc, k = dd / cnt, 1
            while k < w:
                acc = acc + _shift_up(acc, k)
                k *= 2
            du_ref[:, cols] = (acc - dd).astype(du_ref.dtype)

    return pl.pallas_call(
        body, name=name, grid=(1,),
        in_specs=[pl.BlockSpec((S, POOL_W), lambda i: (0, 0)),
                  pl.BlockSpec((S, POOL_W), lambda i: (0, 0)),
                  pl.BlockSpec((4, HEAD, HEAD), lambda i: (0, 0, 0)),
                  pl.BlockSpec((1, POOL_W), lambda i: (0, 0))],
        out_specs=[pl.BlockSpec((S, POOL_W), lambda i: (0, 0)),
                   pl.BlockSpec((4, HEAD, HEAD), lambda i: (0, 0, 0)),
                   pl.BlockSpec((1, POOL_W), lambda i: (0, 0))],
        out_shape=[jax.ShapeDtypeStruct((S, POOL_W), BF16), jax.ShapeDtypeStruct((4, HEAD, HEAD), F32),
                   jax.ShapeDtypeStruct((1, POOL_W), F32)],
        compiler_params=_params(("arbitrary",)),
    )(proj, dycat, pool_w, pool_scale)


def _expm1(x):
    series = x * (1.0 + x * (0.5 + x * (1.0 / 6.0 + x * (1.0 / 24.0 + x * (1.0 / 120.0)))))
    return jnp.where(jnp.abs(x) < 0.05, series, jnp.exp(x) - 1.0)


def _softplus_neg(lam):
    e = jnp.exp(-jnp.abs(lam))
    log1p = jnp.where(e < 0.01, e * (1.0 - e * (0.5 - e * (1.0 / 3.0))), jnp.log(1.0 + e))
    return jnp.maximum(-lam, 0.0) + log1p


_GELU_C = math.sqrt(2.0 / math.pi)


def _gelu(x):
    t = jnp.tanh(_GELU_C * (x + 0.044715 * x * x * x))
    return 0.5 * x * (1.0 + t), t


def _gelu_grad(x, t):
    return 0.5 * (1.0 + t) + 0.5 * x * (1.0 - t * t) * _GELU_C * (1.0 + 3.0 * 0.044715 * x * x)


def _conv(u, cw, cb):
    return cw[3:4] * u + cw[2:3] * _shift_down(u, 1) + cw[1:2] * _shift_down(u, 2) + cw[0:1] * _shift_down(u, 3) + cb


def _lru_gates(cu, wa, ba, wx, bx, lam):
    r = jax.nn.sigmoid(_dot(cu, wa, _NN) + ba)
    i = jax.nn.sigmoid(_dot(cu, wx, _NN) + bx)
    sp = _softplus_neg(lam)
    log_a = (-LRU_C) * r * sp
    a = jnp.exp(log_a)
    mult = jnp.sqrt(-_expm1(2.0 * log_a))
    return r, i, sp, a, mult


def _scan(a_ref, b_ref, h_ref, *, reverse):
    n_blk = S // 8
    row8 = lax.broadcasted_iota(jnp.int32, (8, HEAD), 0)

    def step(j, carry):
        blk = (n_blk - 1 - j) if reverse else j
        r0 = pl.multiple_of(blk * 8, 8)
        a = a_ref[pl.ds(r0, 8), :]
        b = b_ref[pl.ds(r0, 8), :]
        for k in (1, 2, 4):
            if reverse:
                keep = row8 < 8 - k
                a_s, b_s = pltpu.roll(a, 8 - k, 0), pltpu.roll(b, 8 - k, 0)
            else:
                keep = row8 >= k
                a_s, b_s = pltpu.roll(a, k, 0), pltpu.roll(b, k, 0)
            b = jnp.where(keep, a * b_s + b, b)
            a = jnp.where(keep, a * a_s, a)
        h = b + a * carry
        h_ref[pl.ds(r0, 8), :] = h
        edge = h[0:1, :] if reverse else h[7:8, :]
        return jnp.broadcast_to(edge, (8, HEAD))

    lax.fori_loop(0, n_blk, step, jnp.zeros((8, HEAD), F32), unroll=4)


def _lru_specs():
    def col(off):
        return pl.BlockSpec((S, HEAD), lambda h: (0, off + h))
    vec = pl.BlockSpec((1, HEAD), lambda h: (0, h))
    mat = pl.BlockSpec((None, HEAD, HEAD), lambda h: (h, 0, 0))
    cw = pl.BlockSpec((4, HEAD), lambda h: (0, h))
    return col, vec, mat, cw


def lru_fwd(name, proj, conv_w, conv_b, w_a, b_a, w_x, b_x, lam):
    def body(u_ref, ug_ref, cw_ref, cb_ref, wa_ref, ba_ref, wx_ref, bx_ref, lam_ref, y_ref, h_ref, a_s, b_s):
        cu = _conv(u_ref[...], cw_ref[...], cb_ref[...])
        _, i, _, a, mult = _lru_gates(cu, wa_ref[...], ba_ref[...], wx_ref[...], bx_ref[...], lam_ref[...])
        a_s[...] = a
        b_s[...] = mult * (i * cu)
        _scan(a_s, b_s, h_ref, reverse=False)
        gl, _ = _gelu(ug_ref[...])
        y_ref[...] = (h_ref[...] * gl).astype(y_ref.dtype)

    col, vec, mat, cw = _lru_specs()
    out = pl.BlockSpec((S, HEAD), lambda h: (0, h))
    return pl.pallas_call(
        body, name=name, grid=(LRU_HEADS,),
        in_specs=[col(4), col(12), cw, vec, mat, vec, mat, vec, vec],
        out_specs=[out, out],
        out_shape=[jax.ShapeDtypeStruct((S, LRU_W), BF16), jax.ShapeDtypeStruct((S, LRU_W), F32)],
        scratch_shapes=[pltpu.VMEM((S, HEAD), F32), pltpu.VMEM((S, HEAD), F32)],
        compiler_params=_params(("parallel",)),
    )(proj, proj, conv_w, conv_b, w_a, b_a, w_x, b_x, lam)


def lru_bwd(name, proj, hstate, dycat, conv_w, conv_b, w_a, b_a, w_x, b_x, lam):
    def body(u_ref, ug_ref, h_ref, dy_ref, cw_ref, cb_ref, wa_ref, ba_ref, wx_ref, bx_ref, lam_ref,
             du_ref, dug_ref, dwa_ref, dwx_ref, dba_ref, dbx_ref, dlam_ref, dcw_ref, dcb_ref, a_s, b_s, g_s):
        u = u_ref[...]
        cw = cw_ref[...]
        cu = _conv(u, cw, cb_ref[...])
        lam_v = lam_ref[...]
        r, i, sp, a, mult = _lru_gates(cu, wa_ref[...], ba_ref[...], wx_ref[...], bx_ref[...], lam_v)
        ug = ug_ref[...]
        gl, t = _gelu(ug)
        dy = dy_ref[...]
        h = h_ref[...]
        dug_ref[...] = (dy * h * _gelu_grad(ug, t)).astype(dug_ref.dtype)
        a_s[...] = _shift_up(a, 1)
        b_s[...] = dy * gl
        _scan(a_s, b_s, g_s, reverse=True)
        dxin = g_s[...]
        da = dxin * _shift_down(h, 1)
        dmult = dxin * (i * cu)
        di = dxin * (mult * cu)
        dlog_a = da * a - dmult * (a * a) / mult
        dr_pre = dlog_a * ((-LRU_C) * sp) * (r * (1.0 - r))
        di_pre = di * (i * (1.0 - i))
        dsp = jnp.sum(dlog_a * ((-LRU_C) * r), axis=0, keepdims=True)
        dlam_ref[...] = dsp * (-jax.nn.sigmoid(-lam_v))
        dba_ref[...] = jnp.sum(dr_pre, axis=0, keepdims=True)
        dbx_ref[...] = jnp.sum(di_pre, axis=0, keepdims=True)
        dwa_ref[...] = _dot(cu, dr_pre, _TN)
        dwx_ref[...] = _dot(cu, di_pre, _TN)
        dcu = dxin * (mult * i) + _dot(dr_pre, wa_ref[...], _NT) + _dot(di_pre, wx_ref[...], _NT)
        dcb_ref[...] = jnp.sum(dcu, axis=0, keepdims=True)
        for k in range(4):
            dcw_ref[k:k + 1, :] = jnp.sum(dcu * (_shift_down(u, 3 - k) if k < 3 else u), axis=0, keepdims=True)
        du = cw[3:4] * dcu + cw[2:3] * _shift_up(dcu, 1) + cw[1:2] * _shift_up(dcu, 2) + cw[0:1] * _shift_up(dcu, 3)
        du_ref[...] = du.astype(du_ref.dtype)

    col, vec, mat, cw = _lru_specs()
    out = pl.BlockSpec((S, HEAD), lambda h: (0, h))
    big = jax.ShapeDtypeStruct((S, LRU_W), BF16)
    vec_shape = jax.ShapeDtypeStruct((1, LRU_W), F32)
    mat_shape = jax.ShapeDtypeStruct((LRU_HEADS, HEAD, HEAD), F32)
    return pl.pallas_call(
        body, name=name, grid=(LRU_HEADS,),
        in_specs=[col(4), col(12), out, col(4), cw, vec, mat, vec, mat, vec, vec],
        out_specs=[out, out, mat, mat, vec, vec, vec, cw, vec],
        out_shape=[big, big, mat_shape, mat_shape, vec_shape, vec_shape, vec_shape,
                   jax.ShapeDtypeStruct((4, LRU_W), F32), vec_shape],
        scratch_shapes=[pltpu.VMEM((S, HEAD), F32)] * 3,
        compiler_params=_params(("parallel",)),
    )(proj, proj, hstate, dycat, conv_w, conv_b, w_a, b_a, w_x, b_x, lam)


def rope_tables(pos_col, inv_freq):
    def body(pos_ref, f_ref, c_ref, s1_ref, s2_ref):
        ang = pos_ref[...].astype(F32) * f_ref[...]
        lane = lax.broadcasted_iota(jnp.int32, ang.shape, 1)
        cos, sin = jnp.cos(ang), jnp.sin(ang)
        c_ref[...] = jnp.where(lane < QK_ROPE, cos, 0.0)
        s1_ref[...] = jnp.where(lane < QK_ROPE // 2, -sin, 0.0)
        s2_ref[...] = jnp.where((lane >= QK_ROPE // 2) & (lane < QK_ROPE), sin, 0.0)

    tab = jax.ShapeDtypeStruct((S, HEAD), F32)
    return pl.pallas_call(
        body, name="rope_tables", grid=(1,),
        in_specs=[pl.BlockSpec((S, 1), lambda i: (0, 0)), pl.BlockSpec((1, HEAD), lambda i: (0, 0))],
        out_specs=[pl.BlockSpec((S, HEAD), lambda i: (0, 0))] * 3, out_shape=[tab, tab, tab],
        compiler_params=_params(("arbitrary",)),
    )(pos_col, inv_freq)


def _rope(v, c, s1, s2):
    return v * c + pltpu.roll(v, HEAD - QK_ROPE // 2, 1) * s1 + pltpu.roll(v, QK_ROPE // 2, 1) * s2


def _unrope(d, c, s1, s2):
    return d * c + pltpu.roll(d * s1, QK_ROPE // 2, 1) + pltpu.roll(d * s2, HEAD - QK_ROPE // 2, 1)


def _rms(x, g):
    rstd = lax.rsqrt(jnp.mean(x * x, axis=-1, keepdims=True) + RMS_EPS)
    return x * rstd, rstd


def mla_prep(name, down, gq, gkv, tabs, *, tm=256):
    def body(dn_ref, gq_ref, gkv_ref, c_ref, s1_ref, s2_ref, cq_ref, ckv_ref, kp_ref):
        xq, _ = _rms(dn_ref[:, :Q_RANK], None)
        cq_ref[...] = (xq * gq_ref[...]).astype(cq_ref.dtype)
        xkv, _ = _rms(dn_ref[:, Q_RANK:Q_RANK + KV_RANK], None)
        ckv_ref[...] = (xkv * gkv_ref[...]).astype(ckv_ref.dtype)
        kp = _rope(dn_ref[:, Q_RANK + KV_RANK:], c_ref[...], s1_ref[...], s2_ref[...])
        kp_ref[...] = kp.astype(kp_ref.dtype)

    tab = pl.BlockSpec((tm, HEAD), lambda i: (i, 0))
    return pl.pallas_call(
        body, name=name, grid=(S // tm,),
        in_specs=[pl.BlockSpec((tm, ODD_IN_PAD), lambda i: (i, 0)), pl.BlockSpec((1, Q_RANK), lambda i: (0, 0)),
                  pl.BlockSpec((1, KV_RANK), lambda i: (0, 0)), tab, tab, tab],
        out_specs=[pl.BlockSpec((tm, Q_RANK), lambda i: (i, 0)), pl.BlockSpec((tm, KV_RANK), lambda i: (i, 0)), tab],
        out_shape=[jax.ShapeDtypeStruct((S, Q_RANK), BF16), jax.ShapeDtypeStruct((S, KV_RANK), BF16),
                   jax.ShapeDtypeStruct((S, HEAD), BF16)],
        compiler_params=_params(("parallel",)),
    )(down, gq, gkv, *tabs)


def mla_prep_bwd(name, down, dcq, dckv, dkp, gq, gkv, tabs, *, tm=256):
    def body(dn_ref, dcq_ref, dckv_ref, dkp_ref, gq_ref, gkv_ref, c_ref, s1_ref, s2_ref, dd_ref, dgq_ref, dgkv_ref):
        i = pl.program_id(0)

        def rms_bwd(x, dy, g, dg_ref):
            xh, rstd = _rms(x, None)
            dxh = dy * g
            dx = rstd * (dxh - xh * jnp.mean(dxh * xh, axis=-1, keepdims=True))
            pg = jnp.sum(dy * xh, axis=0, keepdims=True)

            @pl.when(i == 0)
            def _():
                dg_ref[...] = pg

            @pl.when(i > 0)
            def _():
                dg_ref[...] += pg

            return dx

        dxq = rms_bwd(dn_ref[:, :Q_RANK], dcq_ref[...], gq_ref[...], dgq_ref)
        dd_ref[:, :Q_RANK] = dxq.astype(dd_ref.dtype)
        dxkv = rms_bwd(dn_ref[:, Q_RANK:Q_RANK + KV_RANK], dckv_ref[...], gkv_ref[...], dgkv_ref)
        dd_ref[:, Q_RANK:Q_RANK + KV_RANK] = dxkv.astype(dd_ref.dtype)
        dd_ref[:, Q_RANK + KV_RANK:] = _unrope(dkp_ref[...], c_ref[...], s1_ref[...], s2_ref[...]).astype(dd_ref.dtype)

    tab = pl.BlockSpec((tm, HEAD), lambda i: (i, 0))
    vq = pl.BlockSpec((1, Q_RANK), lambda i: (0, 0))
    vkv = pl.BlockSpec((1, KV_RANK), lambda i: (0, 0))
    return pl.pallas_call(
        body, name=name, grid=(S // tm,),
        in_specs=[pl.BlockSpec((tm, ODD_IN_PAD), lambda i: (i, 0)), pl.BlockSpec((tm, Q_RANK), lambda i: (i, 0)),
                  pl.BlockSpec((tm, KV_RANK), lambda i: (i, 0)), tab, vq, vkv, tab, tab, tab],
        out_specs=[pl.BlockSpec((tm, ODD_IN_PAD), lambda i: (i, 0)), vq, vkv],
        out_shape=[jax.ShapeDtypeStruct((S, ODD_IN_PAD), BF16), jax.ShapeDtypeStruct((1, Q_RANK), F32),
                   jax.ShapeDtypeStruct((1, KV_RANK), F32)],
        compiler_params=_params(("arbitrary",)),
    )(down, dcq, dckv, dkp, gq, gkv, *tabs)


ATT_TQ = 256


def _attn_scores(q_ref, kv_ref, kp_ref, c_ref, s1_ref, s2_ref, nk):
    qn = q_ref[:, :HEAD].astype(BF16)
    qp = _rope(q_ref[:, HEAD:], c_ref[...], s1_ref[...], s2_ref[...]).astype(BF16)
    kn = kv_ref[:nk, :HEAD]
    sc = (_dot(qn, kn, _NT) + _dot(qp, kp_ref[:nk, :], _NT)) * ATT_SCALE
    q_chunk = lax.broadcasted_iota(jnp.int32, (ATT_TQ, ATT_TQ), 0) // CHUNK
    k_chunk = lax.broadcasted_iota(jnp.int32, (ATT_TQ, ATT_TQ), 1) // CHUNK
    own = jnp.where(k_chunk <= q_chunk, sc[:, nk - ATT_TQ:], jnp.finfo(F32).min)
    sc = own if nk == ATT_TQ else jnp.concatenate([sc[:, :nk - ATT_TQ], own], axis=1)
    return sc, qn, qp, kn


def _for_each_prefix(i, fn):
    for k in range(S // ATT_TQ):
        pl.when(i == k)(functools.partial(fn, (k + 1) * ATT_TQ))


def _attn_specs():
    q = pl.BlockSpec((ATT_TQ, QHEAD_PAD), lambda h, i: (i, h))
    kv = pl.BlockSpec((S, QHEAD_PAD), lambda h, i: (0, h))
    kp = pl.BlockSpec((S, HEAD), lambda h, i: (0, 0))
    tab = pl.BlockSpec((ATT_TQ, HEAD), lambda h, i: (i, 0))
    o = pl.BlockSpec((ATT_TQ, HEAD), lambda h, i: (i, h))
    lse = pl.BlockSpec((None, ATT_TQ, 1), lambda h, i: (h, i, 0))
    return q, kv, kp, tab, o, lse


def attn_fwd(name, q, kv, kp, tabs):
    def body(q_ref, kv_ref, kp_ref, c_ref, s1_ref, s2_ref, o_ref, lse_ref):
        i = pl.program_id(1)

        def run(nk):
            sc, _, _, _ = _attn_scores(q_ref, kv_ref, kp_ref, c_ref, s1_ref, s2_ref, nk)
            m = jnp.max(sc, axis=-1, keepdims=True)
            e = jnp.exp(sc - m)
            total = jnp.sum(e, axis=-1, keepdims=True)
            o_ref[...] = (_dot(e, kv_ref[:nk, HEAD:], _NN) * (1.0 / total)).astype(o_ref.dtype)
            lse_ref[...] = m + jnp.log(total)

        _for_each_prefix(i, run)

    qs, kvs, kps, tab, os, ls = _attn_specs()
    return pl.pallas_call(
        body, name=name, grid=(MLA_HEADS, S // ATT_TQ), in_specs=[qs, kvs, kps, tab, tab, tab], out_specs=[os, ls],
        out_shape=[jax.ShapeDtypeStruct((S, MLA_HEADS * HEAD), BF16), jax.ShapeDtypeStruct((MLA_HEADS, S, 1), F32)],
        compiler_params=_params(("parallel", "parallel")),
    )(q, kv, kp, *tabs)


def attn_bwd(name, q, kv, kp, o, lse, do, tabs):
    def body(q_ref, kv_ref, kp_ref, o_ref, lse_ref, do_ref, c_ref, s1_ref, s2_ref, dq_ref, dkv_ref, dkp_ref):
        h, i = pl.program_id(0), pl.program_id(1)

        @pl.when(i == 0)
        def _():
            dkv_ref[...] = jnp.zeros_like(dkv_ref)

        @pl.when((i == 0) & (h == 0))
        def _():
            dkp_ref[...] = jnp.zeros_like(dkp_ref)

        def run(nk):
            sc, qn, qp, kn = _attn_scores(q_ref, kv_ref, kp_ref, c_ref, s1_ref, s2_ref, nk)
            p = jnp.exp(sc - lse_ref[...])
            do_v = do_ref[...]
            delta = jnp.sum(do_v.astype(F32) * o_ref[...].astype(F32), axis=-1, keepdims=True)
            dp = _dot(do_v, kv_ref[:nk, HEAD:], _NT)
            ds = (p * ((dp - delta) * ATT_SCALE)).astype(BF16)
            dq_ref[:, :HEAD] = _dot(ds, kn, _NN).astype(dq_ref.dtype)
            dqp = _unrope(_dot(ds, kp_ref[:nk, :], _NN), c_ref[...], s1_ref[...], s2_ref[...])
            dq_ref[:, HEAD:] = dqp.astype(dq_ref.dtype)
            dkv_ref[:nk, :HEAD] += _dot(ds, qn, _TN)
            dkv_ref[:nk, HEAD:] += _dot(p, do_v, _TN)
            dkp_ref[:nk, :] += _dot(ds, qp, _TN)

        _for_each_prefix(i, run)

    qs, kvs, kps, tab, os, ls = _attn_specs()
    return pl.pallas_call(
        body, name=name, grid=(MLA_HEADS, S // ATT_TQ), in_specs=[qs, kvs, kps, os, ls, os, tab, tab, tab],
        out_specs=[qs, kvs, kps],
        out_shape=[jax.ShapeDtypeStruct((S, MLA_HEADS * QHEAD_PAD), BF16),
                   jax.ShapeDtypeStruct((S, MLA_HEADS * QHEAD_PAD), F32), jax.ShapeDtypeStruct((S, HEAD), F32)],
        compiler_params=_params(("arbitrary", "arbitrary")),
    )(q, kv, kp, o, lse, do, *tabs)


def adamw(name, w, g, m, v):
    rows, cols = w.shape
    tr = rows
    for cand in (512, 256, 128, 64, 32, 16, 8):
        if rows % cand == 0 and cand * cols * 4 <= 2 * 1024 * 1024:
            tr = cand
            break

    def body(w_ref, g_ref, m_ref, v_ref, d_ref, nm_ref, nv_ref):
        g_v = g_ref[...]
        nm = ADAM_B1 * m_ref[...] + (1.0 - ADAM_B1) * g_v
        nv = ADAM_B2 * v_ref[...] + (1.0 - ADAM_B2) * (g_v * g_v)
        m_hat = nm / (1.0 - ADAM_B1 ** ADAM_STEP)
        v_hat = nv / (1.0 - ADAM_B2 ** ADAM_STEP)
        d_ref[...] = (-ADAM_LR) * (m_hat / (jnp.sqrt(v_hat) + ADAM_EPS) + ADAM_WD * w_ref[...])
        nm_ref[...] = nm
        nv_ref[...] = nv

    blk = pl.BlockSpec((tr, cols), lambda i: (i, 0))
    shape = jax.ShapeDtypeStruct((rows, cols), F32)
    return pl.pallas_call(
        body, name=name, grid=(rows // tr,), in_specs=[blk] * 4, out_specs=[blk] * 3, out_shape=[shape] * 3,
        compiler_params=_params(("parallel",)),
    )(w, g, m, v)


def _local_step(x, pos_col, target, sm, weights_of, emit_grads, prefetch):
    inv_freq = ROPE_THETA ** (-jnp.arange(0, QK_ROPE, 2, dtype=F32) / QK_ROPE)
    inv_freq = jnp.concatenate([inv_freq, inv_freq, jnp.zeros((HEAD - QK_ROPE,), F32)])[None, :]
    tabs = rope_tables(pos_col, inv_freq)
    saved, wts = [], {}
    for layer in range(DEPTH):
        j = layer // 2
        n = "l%d_" % layer
        sv = {"x": x}
        wm = wts["mix%d" % layer] = weights_of("mix%d" % layer, x)
        if layer == 0:
            sm = dict(sm, conv_w=wm["conv_w"], gq=wm["gq"], gkv=wm["gkv"])
        if layer % 2 == 0:
            proj = mm_cols(n + "proj", x, wm["w_in"], F32)
            fetched = prefetch("mlp%d" % layer, proj)
            y_pool = pool_fwd(n + "pool", proj, sm["pool_w"][j], sm["pool_scale"][j][None])
            y_lru, hstate = lru_fwd(n + "lru", proj, sm["conv_w"][j], sm["conv_b"][j][None], sm["w_a"][j],
                                    sm["b_a"][j][None], sm["w_x"][j], sm["b_x"][j][None], sm["lam"][j][None])
            ycat = jnp.concatenate([y_pool, y_lru], axis=1)
            mix_in, w_mix = ycat, wm["w_out"]
            sv.update(proj=proj, hstate=hstate, ycat=ycat)
        else:
            down = mm_rows(n + "down", x, wm["w_down"], F32)
            fetched = prefetch("mlp%d" % layer, down)
            cq, ckv, kp = mla_prep(n + "prep", down, sm["gq"][j][None], sm["gkv"][j][None], tabs)
            q = mm_cols(n + "q", cq, wm["w_qb"], F32)
            kv = mm_cols(n + "kv", ckv, wm["w_kvb"], BF16)
            o, lse = attn_fwd(n + "attn", q, kv, kp, tabs)
            mix_in, w_mix = o, wm["w_o"]
            sv.update(down=down, cq=cq, ckv=ckv, kp=kp, q=q, kv=kv, o=o, lse=lse)
        x1, xhat1, rstd1 = mm_rows(n + "mixout", mix_in, w_mix, F32, after=fetched,
                                   ln=(x, sm["ln_mix_g"][layer][None], sm["ln_mix_b"][layer][None]))
        wf = wts["mlp%d" % layer] = weights_of("mlp%d" % layer, x1)
        x2, xhat2, rstd2, relu = mlp_fwd(n + "mlp", x1, wf["w1"], wf["w2"], sm["ln_ffn_g"][layer][None],
                                         sm["ln_ffn_b"][layer][None])
        sv.update(xhat1=xhat1, rstd1=rstd1, x1=x1, relu=relu, xhat2=xhat2, rstd2=rstd2)
        saved.append(sv)
        x = x2

    gs = {k: [None] * (DEPTH if k.startswith("ln_") else DEPTH // 2) for k in sm}
    last = saved[DEPTH - 1]
    dz, gs["ln_ffn_g"][DEPTH - 1], gs["ln_ffn_b"][DEPTH - 1], loss_tile = loss_head(
        x, target, last["xhat2"], last["rstd2"], sm["ln_ffn_g"][DEPTH - 1][None])
    sent = None
    for layer in reversed(range(DEPTH)):
        j = layer // 2
        n = "l%d_" % layer
        sv = saved[layer]
        wm, wf = wts["mix%d" % layer], wts["mlp%d" % layer]
        dz, g_w1, g_w2, gs["ln_mix_g"][layer], gs["ln_mix_b"][layer] = mlp_bwd(
            n + "mlp_b", dz, sv["relu"], sv["x1"], wf["w1"], wf["w2"],
            (sv["xhat1"], sv["rstd1"], sm["ln_mix_g"][layer][None]), after=sent)
        sent = emit_grads("mlp%d" % layer, {"w1": g_w1, "w2": g_w2})
        below = saved[layer - 1] if layer else None
        ln_below = (below["xhat2"], below["rstd2"], sm["ln_ffn_g"][layer - 1][None]) if layer else None
        if layer % 2 == 0:
            dycat, g_out = linear_bwd(n + "mixout_b", sv["ycat"], dz, wm["w_out"], col_sharded=False, da_dtype=F32,
                                      after=sent)
            du_pool, gs["pool_w"][j], gs["pool_scale"][j] = pool_bwd(
                n + "pool_b", sv["proj"], dycat, sm["pool_w"][j], sm["pool_scale"][j][None])
            (du_lru, du_gate, gs["w_a"][j], gs["w_x"][j], gs["b_a"][j], gs["b_x"][j], gs["lam"][j], gs["conv_w"][j],
             gs["conv_b"][j]) = lru_bwd(n + "lru_b", sv["proj"], sv["hstate"], dycat, sm["conv_w"][j],
                                        sm["conv_b"][j][None], sm["w_a"][j], sm["b_a"][j][None], sm["w_x"][j],
                                        sm["b_x"][j][None], sm["lam"][j][None])
            dproj = jnp.concatenate([du_pool, du_lru, du_gate], axis=1)
            res = linear_bwd(n + "proj_b", sv["x"], dproj, wm["w_in"], col_sharded=True, da_dtype=F32, resid=dz,
                             ln=ln_below)
            grads = {"w_in": res[1], "w_out": g_out}
        else:
            do, g_o = linear_bwd(n + "attnout_b", sv["o"], dz, wm["w_o"], col_sharded=False, da_dtype=BF16, after=sent)
            dq, dkv, dkp = attn_bwd(n + "attn_b", sv["q"], sv["kv"], sv["kp"], sv["o"], sv["lse"], do, tabs)
            dcq, g_qb = linear_bwd(n + "q_b", sv["cq"], dq, wm["w_qb"], col_sharded=True, da_dtype=F32)
            dckv, g_kvb = linear_bwd(n + "kv_b", sv["ckv"], dkv, wm["w_kvb"], col_sharded=True, da_dtype=F32)
            ddown, gs["gq"][j], gs["gkv"][j] = mla_prep_bwd(
                n + "prep_b", sv["down"], dcq, dckv, dkp, sm["gq"][j][None], sm["gkv"][j][None], tabs)
            res = linear_bwd(n + "down_b", sv["x"], ddown, wm["w_down"], col_sharded=False, da_dtype=F32, resid=dz,
                             ln=ln_below)
            grads = {"w_down": res[1], "w_qb": g_qb, "w_kvb": g_kvb, "w_o": g_o}
        dz = res[0]
        if layer:
            gs["ln_ffn_g"][layer - 1], gs["ln_ffn_b"][layer - 1] = res[2], res[3]
        sent = emit_grads("mix%d" % layer, grads)
    gs = {k: jnp.stack([a.reshape(sm[k].shape[1:]) for a in v]) for k, v in gs.items()}
    return loss_tile[0, 0], dz, gs, sent


def _place():
    x, y, c = lax.axis_index("x"), lax.axis_index("y"), lax.axis_index("c")
    chips = [(1 - x, y), (x, 1 - y), (1 - x, 1 - y)]
    return x, y, c, chips


def _hbm_call(body, name, args, out_shape, scratch, aliases=None):
    return pl.pallas_call(
        body, name=name, in_specs=[pl.BlockSpec(memory_space=pl.ANY)] * len(args),
        out_specs=[pl.BlockSpec(memory_space=pl.ANY)] * len(out_shape), out_shape=out_shape,
        scratch_shapes=scratch, input_output_aliases=aliases or {},
        compiler_params=pltpu.CompilerParams(has_side_effects=True),
    )(*args)


HBM_SPEC = pl.BlockSpec(memory_space=pltpu.HBM)
SEM_SPEC = pl.BlockSpec(memory_space=pltpu.SEMAPHORE)
EFFECT = pltpu.SideEffectType.DATAFLOW_SIDE_EFFECTING


def _remote(src, dst, send_sem, recv_sem, device):
    return pltpu.make_async_remote_copy(src_ref=src, dst_ref=dst, send_sem=send_sem, recv_sem=recv_sem,
                                        device_id=device, device_id_type=MESH)


def place_own(name, srcs, place, steps=4):
    n = len(srcs)
    in_specs, out_specs, out_shape = [], [], []
    for s in srcs:
        rows, cols = s.shape
        tr = rows // steps
        in_specs.append(pl.BlockSpec((tr, cols), lambda i, p: (i, 0)))
        out_specs.append(pl.BlockSpec((None, tr, cols), lambda i, p: (p[0], i, 0)))
        out_shape.append(jax.ShapeDtypeStruct((N_CHIPS, rows, cols), s.dtype))

    def body(p_ref, *refs):
        for i_ref, o_ref in zip(refs[:n], refs[n:]):
            o_ref[...] = i_ref[...]

    return pl.pallas_call(
        body, name=name, out_shape=out_shape,
        grid_spec=pltpu.PrefetchScalarGridSpec(num_scalar_prefetch=1, grid=(steps,), in_specs=in_specs,
                                               out_specs=out_specs),
        compiler_params=_params(("parallel",)),
    )(place, *srcs)


def split_start(name, groups, n_sems, plan, token_in=None):
    sizes = [len(srcs) for srcs, _ in groups]
    n, n_groups = sum(sizes), len(groups)
    srcs = [pltpu.with_memory_space_constraint(a, pltpu.HBM) for s, _ in groups for a in s]
    lands = [pltpu.with_memory_space_constraint(a, pltpu.HBM) for _, l in groups for a in l]
    extra = [] if token_in is None else [token_in]

    def body(*refs):
        src_refs, land_refs = refs[:n], refs[n:2 * n]
        outs = refs[2 * n + len(extra):]
        off = 0
        for g, size in enumerate(sizes):
            sends, _ = plan(src_refs[off:off + size], land_refs[off:off + size], outs[2 * g], outs[2 * g + 1])
            for cp in sends:
                cp.start()
            off += size
        token = outs[2 * n_groups + 2 * n]
        token[...] = jnp.zeros_like(token)

    sems = [pltpu.SemaphoreType.DMA((size * n_sems,)) for size in sizes for _ in range(2)]
    res = pl.pallas_call(
        body, name=name,
        out_shape=(*sems, *[pltpu.HBM(a.shape, a.dtype) for a in srcs + lands], jax.ShapeDtypeStruct((8, 128), F32)),
        in_specs=[HBM_SPEC] * (2 * n) + [pl.BlockSpec(memory_space=pl.ANY)] * len(extra),
        out_specs=(*[SEM_SPEC] * len(sems), *[HBM_SPEC] * (2 * n), pl.BlockSpec(memory_space=pltpu.VMEM)),
        input_output_aliases={i: len(sems) + i for i in range(2 * n)},
        compiler_params=pltpu.CompilerParams(has_side_effects=EFFECT),
    )(*srcs, *lands, *extra)
    started, off = [], 0
    bufs = res[len(sems):]
    for g, size in enumerate(sizes):
        started.append(dict(send=res[2 * g], recv=res[2 * g + 1], srcs=list(bufs[off:off + size]),
                            lands=list(bufs[n + off:n + off + size]), plan=plan))
        off += size
    return started, res[-1]


def _wait_started(st, bufs, send_sems, recv_sems):
    n = len(st["srcs"])
    sends, expects = st["plan"](bufs[:n], bufs[n:], send_sems, recv_sems)
    for cp in sends:
        cp.wait_send()
    for cp in expects:
        cp.wait_recv()


def split_wait(name, started, after):
    sizes = [len(st["srcs"]) + len(st["lands"]) for st in started]
    n_buf = sum(sizes)

    def body(*refs):
        bufs, sems = refs[:n_buf], refs[n_buf:n_buf + 2 * len(started)]
        off = 0
        for g, (st, n) in enumerate(zip(started, sizes)):
            _wait_started(st, bufs[off:off + n], sems[2 * g], sems[2 * g + 1])
            off += n

    bufs = [a for st in started for a in st["srcs"] + st["lands"]]
    sems = [s for st in started for s in (st["send"], st["recv"])]
    res = pl.pallas_call(
        body, name=name, out_shape=tuple(pltpu.HBM(a.shape, a.dtype) for a in bufs),
        in_specs=[HBM_SPEC] * n_buf + [SEM_SPEC] * len(sems) + [pl.BlockSpec(memory_space=pl.ANY)],
        out_specs=tuple([HBM_SPEC] * n_buf), input_output_aliases={i: i for i in range(n_buf)},
        compiler_params=pltpu.CompilerParams(has_side_effects=EFFECT),
    )(*bufs, *sems, after)
    out, off = [], 0
    for st, n in zip(started, sizes):
        out.append((list(res[off:off + len(st["srcs"])]), list(res[off + len(st["srcs"]):off + n])))
        off += n
    return out


def split_relay(name, st, n_sems, plan, after):
    n_src, n = len(st["srcs"]), len(st["lands"])

    def body(*refs):
        bufs = refs[:n_src + n]
        outs = refs[n_src + n + 3:]
        _wait_started(st, bufs, refs[n_src + n], refs[n_src + n + 1])
        sends, _ = plan((), bufs[n_src:], outs[0], outs[1])
        for cp in sends:
            cp.start()
        outs[2 + n][...] = jnp.zeros((8, 128), F32)

    bufs = st["srcs"] + st["lands"]
    res = pl.pallas_call(
        body, name=name,
        out_shape=(pltpu.SemaphoreType.DMA((n * n_sems,)), pltpu.SemaphoreType.DMA((n * n_sems,)),
                   *[pltpu.HBM(a.shape, a.dtype) for a in st["lands"]], jax.ShapeDtypeStruct((8, 128), F32)),
        in_specs=[HBM_SPEC] * (n_src + n) + [SEM_SPEC] * 2 + [pl.BlockSpec(memory_space=pl.ANY)],
        out_specs=(SEM_SPEC, SEM_SPEC, *[HBM_SPEC] * n, pl.BlockSpec(memory_space=pltpu.VMEM)),
        input_output_aliases={n_src + i: 2 + i for i in range(n)},
        compiler_params=pltpu.CompilerParams(has_side_effects=EFFECT),
    )(*bufs, st["send"], st["recv"], after)
    return dict(send=res[0], recv=res[1], srcs=[], lands=list(res[2:2 + n]), plan=plan), res[-1]


def gather_plan(src_refs, land_refs, send_sems, recv_sems):
    x, y, c, chips = _place()
    me = 2 * x + y
    sends, expects = [], []
    for k, (s, d) in enumerate(zip(src_refs, land_refs)):
        mine = pl.ds(c * (s.shape[0] // 2), s.shape[0] // 2)
        for j, (px, py) in enumerate(chips):
            sem = 3 * k + j
            sends.append(_remote(s.at[mine], d.at[me, mine], send_sems.at[sem], recv_sems.at[sem], (px, py, c)))
            expects.append(_remote(s.at[mine], d.at[2 * px + py, mine], send_sems.at[sem], recv_sems.at[sem], (px, py, c)))
    return sends, expects


def relay_plan(src_refs, land_refs, send_sems, recv_sems):
    x, y, c, chips = _place()
    sends, expects = [], []
    for k, d in enumerate(land_refs):
        hr = d.shape[1] // 2
        mine, theirs = pl.ds(c * hr, hr), pl.ds((1 - c) * hr, hr)
        for j, (px, py) in enumerate(chips):
            sem, chip = 3 * k + j, 2 * px + py
            sends.append(_remote(d.at[chip, mine], d.at[chip, mine], send_sems.at[sem], recv_sems.at[sem], (x, y, 1 - c)))
            expects.append(_remote(d.at[chip, mine], d.at[chip, theirs], send_sems.at[sem], recv_sems.at[sem], (x, y, 1 - c)))
    return sends, expects


def _reduce_part(s, chip_idx, h):
    hr = s.shape[1] // 2
    return s.at[chip_idx if s.shape[0] == N_CHIPS else 0, pl.ds(h * hr, hr)]


def reduce_plan(src_refs, land_refs, send_sems, recv_sems):
    x, y, c, chips = _place()
    me_chip, me_dev = 2 * x + y, 4 * x + 2 * y + c
    sends, expects = [], []
    for k, (s, d) in enumerate(zip(src_refs, land_refs)):
        part = functools.partial(_reduce_part, s)
        for j, (px, py) in enumerate(chips):
            for h in range(2):
                sends.append(_remote(part(2 * px + py, h), d.at[me_dev], send_sems.at[7 * k + 2 * j + h],
                                     recv_sems.at[7 * k + 2 * j + c], (px, py, h)))
                expects.append(_remote(part(me_chip, c), d.at[4 * px + 2 * py + h], send_sems.at[7 * k + 2 * j + h],
                                       recv_sems.at[7 * k + 2 * j + h], (px, py, h)))
        sends.append(_remote(part(me_chip, 1 - c), d.at[me_dev], send_sems.at[7 * k + 6], recv_sems.at[7 * k + 6],
                             (x, y, 1 - c)))
        expects.append(_remote(part(me_chip, c), d.at[me_dev + 1 - 2 * c], send_sems.at[7 * k + 6],
                               recv_sems.at[7 * k + 6], (x, y, 1 - c)))
    return sends, expects


def sibling_swap_halves(name, fulls):
    n = len(fulls)

    def body(*refs):
        outs = refs[n:2 * n]
        send_sems, recv_sems = refs[2 * n:]
        x, y, c, _ = _place()
        copies = []
        for k in range(n):
            nl, rows = fulls[k].shape[:2]
            hr = rows // 2
            mine = outs[k].at[pl.ds(0, nl), pl.ds(c * hr, hr)]
            theirs = outs[k].at[pl.ds(0, nl), pl.ds((1 - c) * hr, hr)]
            copies.append((_remote(mine, mine, send_sems.at[k], recv_sems.at[k], (x, y, 1 - c)),
                           _remote(mine, theirs, send_sems.at[k], recv_sems.at[k], (x, y, 1 - c))))
        for send, _ in copies:
            send.start()
        for send, recv in copies:
            send.wait_send()
            recv.wait_recv()

    out_shape = [jax.ShapeDtypeStruct(f.shape, f.dtype) for f in fulls]
    scratch = [pltpu.SemaphoreType.DMA((n,)), pltpu.SemaphoreType.DMA((n,))]
    return _hbm_call(body, name, fulls, out_shape, scratch, aliases={k: k for k in range(n)})


N_DEV = 8


def sum_devices(name, items, place, steps):
    n = len(items)
    in_specs, args, out_specs, out_shape, aliases = [], [place], [], [], {}
    for landed, own, layer, n_layers, _ in items:
        _, hr, cols = landed.shape
        tr = hr // steps
        slot = (lambda p: p[0]) if own.shape[0] == N_CHIPS else (lambda p: 0)
        in_specs += [pl.BlockSpec((N_DEV, tr, cols), lambda r, p: (0, r, 0)),
                     pl.BlockSpec((None, tr, cols), lambda r, p, slot=slot: (slot(p), p[1] * steps + r, 0))]
        args += [landed, own]
        out_specs.append(pl.BlockSpec((None, tr, cols), lambda r, p, layer=layer: (layer, p[1] * steps + r, 0)))
        out_shape.append(jax.ShapeDtypeStruct((n_layers, 2 * hr, cols), F32))
    for k, item in enumerate(items):
        if item[4] is not None:
            in_specs.append(pl.BlockSpec(memory_space=pl.ANY))
            aliases[len(args)] = k
            args.append(item[4])
    n_prev = len(aliases)

    def body(p_ref, *refs):
        for k in range(n):
            r_ref, own_ref, o_ref = refs[2 * k], refs[2 * k + 1], refs[2 * n + n_prev + k]
            mine = own_ref[...].astype(F32)
            acc = jnp.zeros_like(mine)
            for d in range(N_DEV):
                acc = acc + jnp.where(p_ref[2] == d, mine, r_ref[d].astype(F32))
            o_ref[...] = acc

    return pl.pallas_call(
        body, name=name, out_shape=out_shape,
        grid_spec=pltpu.PrefetchScalarGridSpec(num_scalar_prefetch=1, grid=(steps,), in_specs=in_specs,
                                               out_specs=out_specs),
        input_output_aliases=aliases, compiler_params=_params(("parallel",)),
    )(*args)


def _pack(arrs, rows_multiple):
    flat = []
    for a in arrs:
        v = a.reshape(-1).astype(F32)
        flat.append(jnp.pad(v, (0, (-v.shape[0]) % HEAD)))
    v = jnp.concatenate(flat)
    v = jnp.pad(v, (0, (-v.shape[0]) % (HEAD * rows_multiple)))
    return v.reshape(-1, HEAD)


def _unpack(packed, shapes):
    flat = packed.reshape(-1)
    out, off = [], 0
    for shp in shapes:
        size = int(np.prod(shp))
        out.append(flat[off:off + size].reshape(shp))
        off += size + (-size) % HEAD
    return out


BIG = ["even_w_in", "even_w_out", "mla_w_down", "mla_w_qb", "mla_w_kvb", "mla_w_o", "mlp_w1", "mlp_w2"]
BIG_KEY = {"even_w_in": "w_in", "even_w_out": "w_out", "mla_w_down": "w_down", "mla_w_qb": "w_qb",
           "mla_w_kvb": "w_kvb", "mla_w_o": "w_o", "mlp_w1": "w1", "mlp_w2": "w2"}
SMALL_KEY = {"ln_mix_g": "ln_mix_g", "ln_mix_b": "ln_mix_b", "ln_ffn_g": "ln_ffn_g", "ln_ffn_b": "ln_ffn_b",
             "pool_w": "pool_w", "pool_scale": "pool_scale", "lru_conv_w": "conv_w", "lru_conv_b": "conv_b",
             "lru_w_a": "w_a", "lru_b_a": "b_a", "lru_w_x": "w_x", "lru_b_x": "b_x", "lru_lambda": "lam",
             "mla_q_norm_g": "gq", "mla_kv_norm_g": "gkv"}
SMALL = list(SMALL_KEY)
SMALL_SHARDED = ["lru_conv_w", "mla_q_norm_g", "mla_kv_norm_g"]
SMALL_MATRICES = ["pool_w", "lru_w_a", "lru_w_x"]
WEIGHTS = ["ln_mix_g", "ln_mix_b", "ln_ffn_g", "ln_ffn_b", "even_w_in", "pool_w", "pool_scale", "lru_conv_w",
           "lru_conv_b", "lru_w_a", "lru_b_a", "lru_w_x", "lru_b_x", "lru_lambda", "even_w_out", "mla_w_down",
           "mla_q_norm_g", "mla_kv_norm_g", "mla_w_qb", "mla_w_kvb", "mla_w_o", "mlp_w1", "mlp_w2"]


GROUPS = ["mix0", "mlp0", "mix1", "mlp1", "mix2", "mlp2", "mix3", "mlp3"]


def _group_keys(group):
    layer = int(group[3:])
    if group.startswith("mlp"):
        return [("mlp_w1", "w1", layer), ("mlp_w2", "w2", layer)]
    if layer % 2 == 0:
        return [("even_w_in", "w_in", layer // 2), ("even_w_out", "w_out", layer // 2)]
    return [("mla_w_down", "w_down", layer // 2), ("mla_w_qb", "w_qb", layer // 2),
            ("mla_w_kvb", "w_kvb", layer // 2), ("mla_w_o", "w_o", layer // 2)]


def _pad_q_heads(w):
    lead = w.shape[:-1]
    w = w.reshape(lead + (2, QK_NOPE + QK_ROPE))
    w = jnp.pad(w, ((0, 0),) * len(lead) + ((0, 0), (0, QHEAD_PAD - QK_NOPE - QK_ROPE)))
    return w.reshape(lead + (2 * QHEAD_PAD,))


def _unpad_q_heads(g):
    lead = g.shape[:-1]
    return g.reshape(lead + (2, QHEAD_PAD))[..., :QK_NOPE + QK_ROPE].reshape(lead + (2 * (QK_NOPE + QK_ROPE),))


def _step(x, positions, loss_target, w, m, v):
    cx, cy, cc = lax.axis_index("x"), lax.axis_index("y"), lax.axis_index("c")
    chip = 2 * cx + cy
    place = jnp.stack([chip, cc, 2 * chip + cc]).astype(jnp.int32)

    prepared = dict(w)
    prepared["mla_w_down"] = jnp.pad(w["mla_w_down"], ((0, 0), (0, 0), (0, ODD_IN_PAD - ODD_IN)))
    prepared["mla_w_qb"] = _pad_q_heads(w["mla_w_qb"])
    small_shard_shapes = [w[k].shape for k in SMALL_SHARDED]
    sources = {g: [prepared[name][idx].astype(BF16) for name, _, idx in _group_keys(g)] for g in GROUPS}
    sources[GROUPS[0]].append(_pack([w[k] for k in SMALL_SHARDED], 32))
    gathering, token = {}, None
    for name, part in (("first", GROUPS[:1]), ("rest", GROUPS[1:])):
        zones = place_own("own_" + name, [a for g in part for a in sources[g]], place)
        groups, off = [], 0
        for g in part:
            groups.append((sources[g], zones[off:off + len(sources[g])]))
            off += len(sources[g])
        started, token = split_start("gather_" + name, groups, 3, gather_plan, token)
        gathering.update(zip(part, started))
    relayed, relay_token = {}, {}

    def prefetch(g, after):
        relayed[g], relay_token[g] = split_relay("relay_" + g, gathering[g], 3, relay_plan, after)
        return relay_token[g]

    prefetch(GROUPS[0], token)

    def weights_of(g, after):
        _, lands = split_wait("gathered_" + g, [relayed[g]], relay_token[g] if g == GROUPS[0] else after)[0]
        if g.startswith("mlp") and g != GROUPS[-1]:
            prefetch(GROUPS[GROUPS.index(g) + 1], lands[0])
        out = {key: land for (_, key, _), land in zip(_group_keys(g), lands)}
        if g == GROUPS[0]:
            per_chip = [_unpack(lands[-1][s], small_shard_shapes) for s in range(N_CHIPS)]
            for i, key in enumerate(("conv_w", "gq", "gkv")):
                out[key] = jnp.concatenate([p[i] for p in per_chip], axis=-1)
        return out

    reducing = []

    def reduce_start(g, srcs, token_in=None):
        lands = [lax.empty((N_DEV, s.shape[1] // 2, s.shape[2]), s.dtype) for s in srcs]
        started, token = split_start("reduce_" + g, [(srcs, lands)], 7, reduce_plan, token_in)
        reducing.append((g, started[0]))
        return token

    def emit_grads(g, grads):
        return reduce_start(g, [grads[key] for _, key, _ in _group_keys(g)])

    sm = {SMALL_KEY[k]: w[k] for k in SMALL if k not in SMALL_SHARDED}
    loss, grad_x, gs, last_sent = _local_step(x[0], positions.reshape(S, 1), loss_target[0], sm, weights_of,
                                              emit_grads, prefetch)
    loss = lax.psum(loss, ("x", "y", "c"))

    packs = [_pack([gs[SMALL_KEY[k]] for k in SMALL if (k in SMALL_MATRICES) == mat], 128)[None] for mat in (False, True)]
    small_sent = reduce_start("small", [packs[0], packs[1].astype(BF16)], last_sent)

    grad, delta, new_m, new_v = {}, {}, {}, {}
    late_groups = ("mix0", "small")
    stacks, after = {}, small_sent
    for late in (False, True):
        part = [(g, st) for g, st in reducing if (g in late_groups) == late]
        landed = split_wait("reduced_late" if late else "reduced_early", [st for _, st in part], after)
        for (g, _), (owns, lands) in zip(part, landed):
            if g == "small":
                items = [(land, own, 0, 1, None) for own, land in zip(owns, lands)]
                stacks["small_vec"], stacks["small_mat"] = sum_devices("sum_small", items, place, 4)
                continue
            keys = _group_keys(g)
            items = [(land, own, idx, w[name].shape[0], stacks.get(name))
                     for (name, _, idx), own, land in zip(keys, owns, lands)]
            for (name, _, _), stack in zip(keys, sum_devices("sum_" + g, items, place, 4)):
                stacks[name] = stack
        names = [k for k in BIG if (k in ("even_w_in", "even_w_out")) == late] + (["small_vec", "small_mat"] if late else [])
        reduced = dict(zip(names, sibling_swap_halves("swap_late" if late else "swap_early", [stacks[k] for k in names])))
        if not late:
            reduced["mla_w_down"] = reduced["mla_w_down"][..., :ODD_IN]
            reduced["mla_w_qb"] = _unpad_q_heads(reduced["mla_w_qb"])
        for k in names:
            if k.startswith("small"):
                continue
            grad[k] = reduced[k]
            shp = w[k].shape
            view = lambda a: a.reshape(-1, shp[-1])
            d, nm, nv = adamw("adamw_" + BIG_KEY[k], view(w[k]), view(grad[k]), view(m[k]), view(v[k]))
            delta[k], new_m[k], new_v[k] = d.reshape(shp), nm.reshape(shp), nv.reshape(shp)
            after = d
    g_small = {}
    for mat, key in ((False, "small_vec"), (True, "small_mat")):
        names = [k for k in SMALL if (k in SMALL_MATRICES) == mat]
        g_small.update(zip(names, _unpack(reduced[key], [gs[SMALL_KEY[k]].shape for k in names])))
    for k in SMALL_SHARDED:
        width = w[k].shape[-1]
        g_small[k] = lax.dynamic_slice_in_dim(g_small[k], chip * width, width, axis=-1)
    grad.update(g_small)

    shapes = [w[k].shape for k in SMALL]
    d, nm, nv = adamw("adamw_small", *[_pack([t[k] for k in SMALL], 512) for t in (w, grad, m, v)])
    for k, dk, mk, vk in zip(SMALL, _unpack(d, shapes), _unpack(nm, shapes), _unpack(nv, shapes)):
        delta[k], new_m[k], new_v[k] = dk, mk, vk
    return (loss, grad_x[None], *[grad[k] for k in WEIGHTS], *[delta[k] for k in WEIGHTS],
            *[new_m[k] for k in WEIGHTS], *[new_v[k] for k in WEIGHTS])


def kernel(x, positions, ln_mix_g, ln_mix_b, ln_ffn_g, ln_ffn_b, even_w_in, pool_w, pool_scale, lru_conv_w, lru_conv_b, lru_w_a, lru_b_a, lru_w_x, lru_b_x, lru_lambda, even_w_out, mla_w_down, mla_q_norm_g, mla_kv_norm_g, mla_w_qb, mla_w_kvb, mla_w_o, mlp_w1, mlp_w2, loss_target, m_ln_mix_g, m_ln_mix_b, m_ln_ffn_g, m_ln_ffn_b, m_even_w_in, m_pool_w, m_pool_scale, m_lru_conv_w, m_lru_conv_b, m_lru_w_a, m_lru_b_a, m_lru_w_x, m_lru_b_x, m_lru_lambda, m_even_w_out, m_mla_w_down, m_mla_q_norm_g, m_mla_kv_norm_g, m_mla_w_qb, m_mla_w_kvb, m_mla_w_o, m_mlp_w1, m_mlp_w2, v_ln_mix_g, v_ln_mix_b, v_ln_ffn_g, v_ln_ffn_b, v_even_w_in, v_pool_w, v_pool_scale, v_lru_conv_w, v_lru_conv_b, v_lru_w_a, v_lru_b_a, v_lru_w_x, v_lru_b_x, v_lru_lambda, v_even_w_out, v_mla_w_down, v_mla_q_norm_g, v_mla_kv_norm_g, v_mla_w_qb, v_mla_w_kvb, v_mla_w_o, v_mlp_w1, v_mlp_w2):
    args = locals()
    w = {k: args[k] for k in WEIGHTS}
    m = {k: args["m_" + k] for k in WEIGHTS}
    v = {k: args["v_" + k] for k in WEIGHTS}
    return _step(x, positions, loss_target, w, m, v)
```

```python
import functools
import math

import jax
import jax.numpy as jnp
import numpy as np
from jax import lax
from jax.experimental import pallas as pl
from jax.experimental.pallas import tpu as pltpu

F32 = jnp.float32
BF16 = jnp.bfloat16

S = 2048
D = 1024
DEPTH = 4
N_CHIPS = 4
POOL_WINDOWS = (2, 4, 8, 16)
POOL_W = 512
LRU_W = 1024
LRU_HEADS = 8
HEAD = 128
MLA_HEADS = 8
QK_NOPE = 128
QK_ROPE = 64
Q_RANK = 384
KV_RANK = 256
ODD_IN = 704
ODD_IN_PAD = 768
QHEAD_PAD = 256
CHUNK = 64
ALPHA = (2 * DEPTH) ** 0.25
LN_EPS = 1e-5
RMS_EPS = 1e-6
ATT_SCALE = (QK_NOPE + QK_ROPE) ** -0.5
ROPE_THETA = 10000.0
LRU_C = 8.0
ADAM_LR = 0.001
ADAM_B1 = 0.9
ADAM_B2 = 0.999
ADAM_EPS = 1e-08
ADAM_WD = 0.01
ADAM_STEP = 10

VMEM_LIMIT = 56 * 1024 * 1024
MESH = pl.DeviceIdType.MESH

_NN = (((1,), (0,)), ((), ()))
_NT = (((1,), (1,)), ((), ()))
_TN = (((0,), (0,)), ((), ()))


def _params(sem=None, **kw):
    return pltpu.CompilerParams(dimension_semantics=sem, vmem_limit_bytes=VMEM_LIMIT, **kw)


def _dot(a, b, dims):
    return lax.dot_general(a.astype(BF16), b.astype(BF16), dims, preferred_element_type=F32)


def _whole(w4):
    return pl.BlockSpec(tuple(w4.shape), lambda i: (0, 0, 0))


def mm_cols(name, a, w4, out_dtype, *, tm=512):
    m, k = a.shape
    nb = w4.shape[2]

    def body(a_ref, w_ref, o_ref):
        a_v = a_ref[...].astype(BF16)
        for s in range(N_CHIPS):
            o_ref[:, s * nb:(s + 1) * nb] = _dot(a_v, w_ref[s], _NN).astype(o_ref.dtype)

    return pl.pallas_call(
        body, name=name, grid=(m // tm,), in_specs=[pl.BlockSpec((tm, k), lambda i: (i, 0)), _whole(w4)],
        out_specs=pl.BlockSpec((tm, N_CHIPS * nb), lambda i: (i, 0)),
        out_shape=jax.ShapeDtypeStruct((m, N_CHIPS * nb), out_dtype), compiler_params=_params(("parallel",)),
    )(a, w4)


def mm_rows(name, a, w4, out_dtype, *, tm=512, ln=None, after=None):
    m = a.shape[0]
    kb, n = w4.shape[1:]
    n_after = 0 if after is None else 1

    def body(a_ref, w_ref, *rest):
        rest = rest[n_after:]
        acc = _dot(a_ref[...], w_ref[...].reshape(N_CHIPS * kb, n), _NN)
        if ln is None:
            rest[0][...] = acc.astype(rest[0].dtype)
        else:
            x_ref, g_ref, b_ref, y_ref, xhat_ref, rstd_ref = rest
            y_ref[...], xhat_ref[...], rstd_ref[...] = _layer_norm(ALPHA * x_ref[...] + acc, g_ref[...], b_ref[...])

    row = pl.BlockSpec((tm, n), lambda i: (i, 0))
    vec = pl.BlockSpec((1, n), lambda i: (0, 0))
    in_specs = [pl.BlockSpec((tm, N_CHIPS * kb), lambda i: (i, 0)), _whole(w4)] + [pl.BlockSpec(memory_space=pl.ANY)] * n_after
    if ln is None:
        extras, out_specs, out_shape = (), row, jax.ShapeDtypeStruct((m, n), out_dtype)
    else:
        extras, in_specs = ln, in_specs + [row, vec, vec]
        out_specs = [row, row, pl.BlockSpec((tm, 1), lambda i: (i, 0))]
        out_shape = [jax.ShapeDtypeStruct((m, n), F32), jax.ShapeDtypeStruct((m, n), F32), jax.ShapeDtypeStruct((m, 1), F32)]
    return pl.pallas_call(
        body, name=name, grid=(m // tm,), in_specs=in_specs, out_specs=out_specs, out_shape=out_shape,
        compiler_params=_params(("parallel",)),
    )(a, w4, *([after] * n_after), *extras)


def _ln_bwd_rows(dy, xhat, rstd, g):
    dxh = dy * g
    m1 = jnp.mean(dxh, axis=-1, keepdims=True)
    m2 = jnp.mean(dxh * xhat, axis=-1, keepdims=True)
    dz = rstd * (dxh - m1 - xhat * m2)
    return dz, jnp.sum(dy * xhat, axis=0, keepdims=True), jnp.sum(dy, axis=0, keepdims=True)


def linear_bwd(name, a, g, w4, *, col_sharded, da_dtype, resid=None, ln=None, tm=512, after=None):
    m = a.shape[0]
    n_red = m // tm
    ka, ng = w4.shape[1:]
    k_all = a.shape[1]
    n_after = 0 if after is None else 1
    n_resid = 0 if resid is None else 1
    n_ln = 0 if ln is None else 3

    def body(a_ref, g_ref, w_ref, *rest):
        rest = rest[n_after:]
        ln_refs = rest[n_resid:n_resid + n_ln]
        da_ref, dw_ref = rest[n_resid + n_ln:n_resid + n_ln + 2]
        acc_ref = rest[-1]
        k = pl.program_id(0)

        @pl.when(k == 0)
        def _():
            acc_ref[...] = jnp.zeros_like(acc_ref)

        a_v, g_v = a_ref[...].astype(BF16), g_ref[...].astype(BF16)
        extra = ALPHA * rest[0][...] if n_resid else None
        if col_sharded:
            da = extra
            for s in range(N_CHIPS):
                g_s = g_v[:, s * ng:(s + 1) * ng]
                p = _dot(g_s, w_ref[s], _NT)
                da = p if da is None else da + p
                acc_ref[s] += _dot(a_v, g_s, _TN)
        else:
            parts = []
            for s in range(N_CHIPS):
                parts.append(_dot(g_v, w_ref[s], _NT))
                acc_ref[s] += _dot(a_v[:, s * ka:(s + 1) * ka], g_v, _TN)
            da = jnp.concatenate(parts, axis=1)
            da = da if extra is None else da + extra
        if n_ln:
            dg_ref, db_ref = rest[n_resid + n_ln + 2:n_resid + n_ln + 4]
            da, pg, pb = _ln_bwd_rows(da, ln_refs[0][...], ln_refs[1][...], ln_refs[2][...])

            @pl.when(k == 0)
            def _():
                dg_ref[...] = jnp.zeros_like(dg_ref)
                db_ref[...] = jnp.zeros_like(db_ref)

            dg_ref[...] += pg
            db_ref[...] += pb
        da_ref[...] = da.astype(da_ref.dtype)

        @pl.when(k == n_red - 1)
        def _():
            dw_ref[...] = acc_ref[...].astype(dw_ref.dtype)

    row = lambda width: pl.BlockSpec((tm, width), lambda i: (i, 0))
    vec = pl.BlockSpec((1, k_all), lambda i: (0, 0))
    whole = pl.BlockSpec((N_CHIPS, ka, ng), lambda i: (0, 0, 0))
    out_specs = [row(k_all), whole] + [vec, vec] * (n_ln // 3)
    out_shape = [jax.ShapeDtypeStruct((m, k_all), da_dtype), jax.ShapeDtypeStruct((N_CHIPS, ka, ng), BF16)]
    out_shape += [jax.ShapeDtypeStruct((1, k_all), F32)] * (2 * (n_ln // 3))
    return pl.pallas_call(
        body, name=name, grid=(n_red,),
        in_specs=[row(k_all), row(g.shape[1]), whole] + [pl.BlockSpec(memory_space=pl.ANY)] * n_after
        + [row(k_all)] * n_resid + ([row(k_all), row(1), vec] if n_ln else []),
        out_specs=out_specs, out_shape=out_shape,
        scratch_shapes=[pltpu.VMEM((N_CHIPS, ka, ng), F32)], compiler_params=_params(("arbitrary",)),
    )(a, g, w4, *([after] * n_after), *([resid] * n_resid), *(ln or ()))


def _layer_norm(z, g, b):
    mu = jnp.mean(z, axis=-1, keepdims=True)
    zc = z - mu
    rstd = lax.rsqrt(jnp.mean(zc * zc, axis=-1, keepdims=True) + LN_EPS)
    xhat = zc * rstd
    return xhat * g + b, xhat, rstd


def mlp_fwd(name, x, w1, w2, g, b, *, tm=256):
    fb = w1.shape[2]
    f_all = N_CHIPS * fb

    def body(x_ref, w1_ref, w2_ref, g_ref, b_ref, y_ref, xhat_ref, rstd_ref, relu_ref):
        x_b = x_ref[...].astype(BF16)
        acts = []
        for s in range(N_CHIPS):
            r = jnp.maximum(_dot(x_b, w1_ref[s], _NN), 0.0)
            relu_ref[:, s * fb:(s + 1) * fb] = r.astype(relu_ref.dtype)
            acts.append((r * r).astype(BF16))
        mlp = _dot(jnp.concatenate(acts, axis=1), w2_ref[...].reshape(f_all, D), _NN)
        y_ref[...], xhat_ref[...], rstd_ref[...] = _layer_norm(ALPHA * x_ref[...] + mlp, g_ref[...], b_ref[...])

    row = pl.BlockSpec((tm, D), lambda i: (i, 0))
    vec = pl.BlockSpec((1, D), lambda i: (0, 0))
    once = pl.Buffered(1)
    return pl.pallas_call(
        body, name=name, grid=(S // tm,),
        in_specs=[row, pl.BlockSpec(tuple(w1.shape), lambda i: (0, 0, 0), pipeline_mode=once),
                  pl.BlockSpec(tuple(w2.shape), lambda i: (0, 0, 0), pipeline_mode=once), vec, vec],
        out_specs=[row, row, pl.BlockSpec((tm, 1), lambda i: (i, 0)), pl.BlockSpec((tm, f_all), lambda i: (i, 0))],
        out_shape=[jax.ShapeDtypeStruct((S, D), F32), jax.ShapeDtypeStruct((S, D), F32),
                   jax.ShapeDtypeStruct((S, 1), F32), jax.ShapeDtypeStruct((S, f_all), BF16)],
        compiler_params=_params(("parallel",)),
    )(x, w1, w2, g, b)


def mlp_bwd(name, dz, relu, x, w1, w2, ln, *, tm=512, after=None):
    fb = w1.shape[2]
    n_i = S // tm
    n_after = 0 if after is None else 1

    def body(dz_ref, relu_ref, x_ref, w1_ref, w2_ref, xhat_ref, rstd_ref, gln_ref, *rest):
        out_ref, g1_ref, g2_ref, dg_ref, db_ref, acc1_ref, acc2_ref, dx_ref = rest[n_after:]
        s, i = pl.program_id(0), pl.program_id(1)
        rows = pl.ds(pl.multiple_of(i * tm, tm), tm)
        dz_v = dz_ref[...]

        @pl.when(i == 0)
        def _():
            acc1_ref[...] = jnp.zeros_like(acc1_ref)
            acc2_ref[...] = jnp.zeros_like(acc2_ref)

        @pl.when(s == 0)
        def _():
            dx_ref[rows, :] = ALPHA * dz_v

        @pl.when((s == 0) & (i == 0))
        def _():
            dg_ref[...] = jnp.zeros_like(dg_ref)
            db_ref[...] = jnp.zeros_like(db_ref)

        dz_b = dz_v.astype(BF16)
        r = relu_ref[...]
        dh = (_dot(dz_b, w2_ref[...], _NT) * (2.0 * r.astype(F32))).astype(BF16)
        p2 = _dot(r * r, dz_b, _TN)
        p1 = _dot(x_ref[...], dh, _TN)
        dx_ref[rows, :] += _dot(dh, w1_ref[...], _NT)
        acc1_ref[...] += p1
        acc2_ref[...] += p2

        @pl.when(i == n_i - 1)
        def _():
            g1_ref[...] = acc1_ref[...].astype(g1_ref.dtype)
            g2_ref[...] = acc2_ref[...].astype(g2_ref.dtype)

        @pl.when(s == N_CHIPS - 1)
        def _():
            out_ref[...], pg, pb = _ln_bwd_rows(dx_ref[rows, :], xhat_ref[...], rstd_ref[...], gln_ref[...])
            dg_ref[...] += pg
            db_ref[...] += pb

    row = pl.BlockSpec((tm, D), lambda s, i: (i, 0))
    last_only = lambda s, i: (jnp.where(s == N_CHIPS - 1, i, 0), 0)
    vec = pl.BlockSpec((1, D), lambda s, i: (0, 0))
    return pl.pallas_call(
        body, name=name, grid=(N_CHIPS, n_i),
        in_specs=[row, pl.BlockSpec((tm, fb), lambda s, i: (i, s)), row,
                  pl.BlockSpec((None, D, fb), lambda s, i: (s, 0, 0)), pl.BlockSpec((None, fb, D), lambda s, i: (s, 0, 0)),
                  pl.BlockSpec((tm, D), last_only), pl.BlockSpec((tm, 1), last_only), vec]
        + [pl.BlockSpec(memory_space=pl.ANY)] * n_after,
        out_specs=[pl.BlockSpec((tm, D), last_only), pl.BlockSpec((None, D, fb), lambda s, i: (s, 0, 0)),
                   pl.BlockSpec((None, fb, D), lambda s, i: (s, 0, 0)), vec, vec],
        out_shape=[jax.ShapeDtypeStruct((S, D), F32), jax.ShapeDtypeStruct((N_CHIPS, D, fb), BF16),
                   jax.ShapeDtypeStruct((N_CHIPS, fb, D), BF16), jax.ShapeDtypeStruct((1, D), F32),
                   jax.ShapeDtypeStruct((1, D), F32)],
        scratch_shapes=[pltpu.VMEM((D, fb), F32), pltpu.VMEM((fb, D), F32), pltpu.VMEM((S, D), F32)],
        compiler_params=_params(("arbitrary", "arbitrary")),
    )(dz, relu, x, w1, w2, *ln, *([after] * n_after))


def loss_head(y, target, xhat, rstd, g, *, tm=256):
    def body(y_ref, t_ref, xhat_ref, rstd_ref, g_ref, dz_ref, dg_ref, db_ref, loss_ref):
        i = pl.program_id(0)

        @pl.when(i == 0)
        def _():
            loss_ref[...] = jnp.zeros_like(loss_ref)
            dg_ref[...] = jnp.zeros_like(dg_ref)
            db_ref[...] = jnp.zeros_like(db_ref)

        e = y_ref[...] - t_ref[...]
        part = jnp.sum(jnp.sum(e * e, axis=-1, keepdims=True), axis=0, keepdims=True) * (0.5 / D)
        loss_ref[...] += jnp.broadcast_to(part, loss_ref.shape)
        dz_ref[...], pg, pb = _ln_bwd_rows(e * (1.0 / D), xhat_ref[...], rstd_ref[...], g_ref[...])
        dg_ref[...] += pg
        db_ref[...] += pb

    row = pl.BlockSpec((tm, D), lambda i: (i, 0))
    vec = pl.BlockSpec((1, D), lambda i: (0, 0))
    return pl.pallas_call(
        body, name="loss_head", grid=(S // tm,),
        in_specs=[row, row, row, pl.BlockSpec((tm, 1), lambda i: (i, 0)), vec],
        out_specs=[row, vec, vec, pl.BlockSpec((8, 128), lambda i: (0, 0))],
        out_shape=[jax.ShapeDtypeStruct((S, D), F32), jax.ShapeDtypeStruct((1, D), F32),
                   jax.ShapeDtypeStruct((1, D), F32), jax.ShapeDtypeStruct((8, 128), F32)],
        compiler_params=_params(("arbitrary",)),
    )(y, target, xhat, rstd, g)


def _rows(shape):
    return lax.broadcasted_iota(jnp.int32, shape, 0)


def _shift_down(x, k):
    return jnp.where(_rows(x.shape) >= k, pltpu.roll(x, k, 0), 0.0)


def _shift_up(x, k):
    n = x.shape[0]
    return jnp.where(_rows(x.shape) < n - k, pltpu.roll(x, n - k, 0), 0.0)


def _pool_diff(u, w):
    acc, k = u, 1
    while k < w:
        acc = acc + _shift_down(acc, k)
        k *= 2
    cnt = jnp.minimum(_rows(u.shape) + 1, w).astype(F32)
    return acc / cnt - u, cnt


def pool_fwd(name, proj, pool_w, pool_scale):
    def body(u_ref, w_ref, sc_ref, y_ref):
        for g, w in enumerate(POOL_WINDOWS):
            cols = slice(g * HEAD, (g + 1) * HEAD)
            d, _ = _pool_diff(u_ref[:, cols], w)
            z = _dot(d, w_ref[g], _NN)
            y_ref[:, cols] = (z * sc_ref[:, cols]).astype(y_ref.dtype)

    return pl.pallas_call(
        body, name=name, grid=(1,),
        in_specs=[pl.BlockSpec((S, POOL_W), lambda i: (0, 0)),
                  pl.BlockSpec((4, HEAD, HEAD), lambda i: (0, 0, 0)),
                  pl.BlockSpec((1, POOL_W), lambda i: (0, 0))],
        out_specs=pl.BlockSpec((S, POOL_W), lambda i: (0, 0)),
        out_shape=jax.ShapeDtypeStruct((S, POOL_W), BF16),
        compiler_params=_params(("arbitrary",)),
    )(proj, pool_w, pool_scale)


def pool_bwd(name, proj, dycat, pool_w, pool_scale):
    def body(u_ref, dy_ref, w_ref, sc_ref, du_ref, dw_ref, dsc_ref):
        for g, w in enumerate(POOL_WINDOWS):
            cols = slice(g * HEAD, (g + 1) * HEAD)
            d, cnt = _pool_diff(u_ref[:, cols], w)
            dy = dy_ref[:, cols]
            z = _dot(d, w_ref[g], _NN)
            dsc_ref[:, cols] = jnp.sum(dy * z, axis=0, keepdims=True)
            dz = dy * sc_ref[:, cols]
            dw_ref[g] = _dot(d, dz, _TN)
            dd = _dot(dz, w_ref[g], _NT)
            acc, k = dd / cnt, 1
            while k < w:
                acc = acc + _shift_up(acc, k)
                k *= 2
            du_ref[:, cols] = (acc - dd).astype(du_ref.dtype)

    return pl.pallas_call(
        body, name=name, grid=(1,),
        in_specs=[pl.BlockSpec((S, POOL_W), lambda i: (0, 0)),
                  pl.BlockSpec((S, POOL_W), lambda i: (0, 0)),
                  pl.BlockSpec((4, HEAD, HEAD), lambda i: (0, 0, 0)),
                  pl.BlockSpec((1, POOL_W), lambda i: (0, 0))],
        out_specs=[pl.BlockSpec((S, POOL_W), lambda i: (0, 0)),
                   pl.BlockSpec((4, HEAD, HEAD), lambda i: (0, 0, 0)),
                   pl.BlockSpec((1, POOL_W), lambda i: (0, 0))],
        out_shape=[jax.ShapeDtypeStruct((S, POOL_W), BF16), jax.ShapeDtypeStruct((4, HEAD, HEAD), F32),
                   jax.ShapeDtypeStruct((1, POOL_W), F32)],
        compiler_params=_params(("arbitrary",)),
    )(proj, dycat, pool_w, pool_scale)


def _expm1(x):
    series = x * (1.0 + x * (0.5 + x * (1.0 / 6.0 + x * (1.0 / 24.0 + x * (1.0 / 120.0)))))
    return jnp.where(jnp.abs(x) < 0.05, series, jnp.exp(x) - 1.0)


def _softplus_neg(lam):
    e = jnp.exp(-jnp.abs(lam))
    log1p = jnp.where(e < 0.01, e * (1.0 - e * (0.5 - e * (1.0 / 3.0))), jnp.log(1.0 + e))
    return jnp.maximum(-lam, 0.0) + log1p


_GELU_C = math.sqrt(2.0 / math.pi)


def _gelu(x):
    t = jnp.tanh(_GELU_C * (x + 0.044715 * x * x * x))
    return 0.5 * x * (1.0 + t), t


def _gelu_grad(x, t):
    return 0.5 * (1.0 + t) + 0.5 * x * (1.0 - t * t) * _GELU_C * (1.0 + 3.0 * 0.044715 * x * x)


def _conv(u, cw, cb):
    return cw[3:4] * u + cw[2:3] * _shift_down(u, 1) + cw[1:2] * _shift_down(u, 2) + cw[0:1] * _shift_down(u, 3) + cb


def _lru_gates(cu, wa, ba, wx, bx, lam):
    r = jax.nn.sigmoid(_dot(cu, wa, _NN) + ba)
    i = jax.nn.sigmoid(_dot(cu, wx, _NN) + bx)
    sp = _softplus_neg(lam)
    log_a = (-LRU_C) * r * sp
    a = jnp.exp(log_a)
    mult = jnp.sqrt(-_expm1(2.0 * log_a))
    return r, i, sp, a, mult


def _scan(a_ref, b_ref, h_ref, *, reverse):
    n_blk = S // 8
    row8 = lax.broadcasted_iota(jnp.int32, (8, HEAD), 0)

    def step(j, carry):
        blk = (n_blk - 1 - j) if reverse else j
        r0 = pl.multiple_of(blk * 8, 8)
        a = a_ref[pl.ds(r0, 8), :]
        b = b_ref[pl.ds(r0, 8), :]
        for k in (1, 2, 4):
            if reverse:
                keep = row8 < 8 - k
                a_s, b_s = pltpu.roll(a, 8 - k, 0), pltpu.roll(b, 8 - k, 0)
            else:
                keep = row8 >= k
                a_s, b_s = pltpu.roll(a, k, 0), pltpu.roll(b, k, 0)
            b = jnp.where(keep, a * b_s + b, b)
            a = jnp.where(keep, a * a_s, a)
        h = b + a * carry
        h_ref[pl.ds(r0, 8), :] = h
        edge = h[0:1, :] if reverse else h[7:8, :]
        return jnp.broadcast_to(edge, (8, HEAD))

    lax.fori_loop(0, n_blk, step, jnp.zeros((8, HEAD), F32), unroll=4)


def _lru_specs():
    def col(off):
        return pl.BlockSpec((S, HEAD), lambda h: (0, off + h))
    vec = pl.BlockSpec((1, HEAD), lambda h: (0, h))
    mat = pl.BlockSpec((None, HEAD, HEAD), lambda h: (h, 0, 0))
    cw = pl.BlockSpec((4, HEAD), lambda h: (0, h))
    return col, vec, mat, cw


def lru_fwd(name, proj, conv_w, conv_b, w_a, b_a, w_x, b_x, lam):
    def body(u_ref, ug_ref, cw_ref, cb_ref, wa_ref, ba_ref, wx_ref, bx_ref, lam_ref, y_ref, h_ref, a_s, b_s):
        cu = _conv(u_ref[...], cw_ref[...], cb_ref[...])
        _, i, _, a, mult = _lru_gates(cu, wa_ref[...], ba_ref[...], wx_ref[...], bx_ref[...], lam_ref[...])
        a_s[...] = a
        b_s[...] = mult * (i * cu)
        _scan(a_s, b_s, h_ref, reverse=False)
        gl, _ = _gelu(ug_ref[...])
        y_ref[...] = (h_ref[...] * gl).astype(y_ref.dtype)

    col, vec, mat, cw = _lru_specs()
    out = pl.BlockSpec((S, HEAD), lambda h: (0, h))
    return pl.pallas_call(
        body, name=name, grid=(LRU_HEADS,),
        in_specs=[col(4), col(12), cw, vec, mat, vec, mat, vec, vec],
        out_specs=[out, out],
        out_shape=[jax.ShapeDtypeStruct((S, LRU_W), BF16), jax.ShapeDtypeStruct((S, LRU_W), F32)],
        scratch_shapes=[pltpu.VMEM((S, HEAD), F32), pltpu.VMEM((S, HEAD), F32)],
        compiler_params=_params(("parallel",)),
    )(proj, proj, conv_w, conv_b, w_a, b_a, w_x, b_x, lam)


def lru_bwd(name, proj, hstate, dycat, conv_w, conv_b, w_a, b_a, w_x, b_x, lam):
    def body(u_ref, ug_ref, h_ref, dy_ref, cw_ref, cb_ref, wa_ref, ba_ref, wx_ref, bx_ref, lam_ref,
             du_ref, dug_ref, dwa_ref, dwx_ref, dba_ref, dbx_ref, dlam_ref, dcw_ref, dcb_ref, a_s, b_s, g_s):
        u = u_ref[...]
        cw = cw_ref[...]
        cu = _conv(u, cw, cb_ref[...])
        lam_v = lam_ref[...]
        r, i, sp, a, mult = _lru_gates(cu, wa_ref[...], ba_ref[...], wx_ref[...], bx_ref[...], lam_v)
        ug = ug_ref[...]
        gl, t = _gelu(ug)
        dy = dy_ref[...]
        h = h_ref[...]
        dug_ref[...] = (dy * h * _gelu_grad(ug, t)).astype(dug_ref.dtype)
        a_s[...] = _shift_up(a, 1)
        b_s[...] = dy * gl
        _scan(a_s, b_s, g_s, reverse=True)
        dxin = g_s[...]
        da = dxin * _shift_down(h, 1)
        dmult = dxin * (i * cu)
        di = dxin * (mult * cu)
        dlog_a = da * a - dmult * (a * a) / mult
        dr_pre = dlog_a * ((-LRU_C) * sp) * (r * (1.0 - r))
        di_pre = di * (i * (1.0 - i))
        dsp = jnp.sum(dlog_a * ((-LRU_C) * r), axis=0, keepdims=True)
        dlam_ref[...] = dsp * (-jax.nn.sigmoid(-lam_v))
        dba_ref[...] = jnp.sum(dr_pre, axis=0, keepdims=True)
        dbx_ref[...] = jnp.sum(di_pre, axis=0, keepdims=True)
        dwa_ref[...] = _dot(cu, dr_pre, _TN)
        dwx_ref[...] = _dot(cu, di_pre, _TN)
        dcu = dxin * (mult * i) + _dot(dr_pre, wa_ref[...], _NT) + _dot(di_pre, wx_ref[...], _NT)
        dcb_ref[...] = jnp.sum(dcu, axis=0, keepdims=True)
        for k in range(4):
            dcw_ref[k:k + 1, :] = jnp.sum(dcu * (_shift_down(u, 3 - k) if k < 3 else u), axis=0, keepdims=True)
        du = cw[3:4] * dcu + cw[2:3] * _shift_up(dcu, 1) + cw[1:2] * _shift_up(dcu, 2) + cw[0:1] * _shift_up(dcu, 3)
        du_ref[...] = du.astype(du_ref.dtype)

    col, vec, mat, cw = _lru_specs()
    out = pl.BlockSpec((S, HEAD), lambda h: (0, h))
    big = jax.ShapeDtypeStruct((S, LRU_W), BF16)
    vec_shape = jax.ShapeDtypeStruct((1, LRU_W), F32)
    mat_shape = jax.ShapeDtypeStruct((LRU_HEADS, HEAD, HEAD), F32)
    return pl.pallas_call(
        body, name=name, grid=(LRU_HEADS,),
        in_specs=[col(4), col(12), out, col(4), cw, vec, mat, vec, mat, vec, vec],
        out_specs=[out, out, mat, mat, vec, vec, vec, cw, vec],
        out_shape=[big, big, mat_shape, mat_shape, vec_shape, vec_shape, vec_shape,
                   jax.ShapeDtypeStruct((4, LRU_W), F32), vec_shape],
        scratch_shapes=[pltpu.VMEM((S, HEAD), F32)] * 3,
        compiler_params=_params(("parallel",)),
    )(proj, proj, hstate, dycat, conv_w, conv_b, w_a, b_a, w_x, b_x, lam)


def rope_tables(pos_col, inv_freq):
    def body(pos_ref, f_ref, c_ref, s1_ref, s2_ref):
        ang = pos_ref[...].astype(F32) * f_ref[...]
        lane = lax.broadcasted_iota(jnp.int32, ang.shape, 1)
        cos, sin = jnp.cos(ang), jnp.sin(ang)
        c_ref[...] = jnp.where(lane < QK_ROPE, cos, 0.0)
        s1_ref[...] = jnp.where(lane < QK_ROPE // 2, -sin, 0.0)
        s2_ref[...] = jnp.where((lane >= QK_ROPE // 2) & (lane < QK_ROPE), sin, 0.0)

    tab = jax.ShapeDtypeStruct((S, HEAD), F32)
    return pl.pallas_call(
        body, name="rope_tables", grid=(1,),
        in_specs=[pl.BlockSpec((S, 1), lambda i: (0, 0)), pl.BlockSpec((1, HEAD), lambda i: (0, 0))],
        out_specs=[pl.BlockSpec((S, HEAD), lambda i: (0, 0))] * 3, out_shape=[tab, tab, tab],
        compiler_params=_params(("arbitrary",)),
    )(pos_col, inv_freq)


def _rope(v, c, s1, s2):
    return v * c + pltpu.roll(v, HEAD - QK_ROPE // 2, 1) * s1 + pltpu.roll(v, QK_ROPE // 2, 1) * s2


def _unrope(d, c, s1, s2):
    return d * c + pltpu.roll(d * s1, QK_ROPE // 2, 1) + pltpu.roll(d * s2, HEAD - QK_ROPE // 2, 1)


def _rms(x):
    rstd = lax.rsqrt(jnp.mean(x * x, axis=-1, keepdims=True) + RMS_EPS)
    return x * rstd, rstd


def mla_prep(name, down, gq, gkv, tabs, *, tm=256):
    def body(dn_ref, gq_ref, gkv_ref, c_ref, s1_ref, s2_ref, cq_ref, ckv_ref, kp_ref):
        xq, _ = _rms(dn_ref[:, :Q_RANK])
        cq_ref[...] = (xq * gq_ref[...]).astype(cq_ref.dtype)
        xkv, _ = _rms(dn_ref[:, Q_RANK:Q_RANK + KV_RANK])
        ckv_ref[...] = (xkv * gkv_ref[...]).astype(ckv_ref.dtype)
        kp = _rope(dn_ref[:, Q_RANK + KV_RANK:], c_ref[...], s1_ref[...], s2_ref[...])
        kp_ref[...] = kp.astype(kp_ref.dtype)

    tab = pl.BlockSpec((tm, HEAD), lambda i: (i, 0))
    return pl.pallas_call(
        body, name=name, grid=(S // tm,),
        in_specs=[pl.BlockSpec((tm, ODD_IN_PAD), lambda i: (i, 0)), pl.BlockSpec((1, Q_RANK), lambda i: (0, 0)),
                  pl.BlockSpec((1, KV_RANK), lambda i: (0, 0)), tab, tab, tab],
        out_specs=[pl.BlockSpec((tm, Q_RANK), lambda i: (i, 0)), pl.BlockSpec((tm, KV_RANK), lambda i: (i, 0)), tab],
        out_shape=[jax.ShapeDtypeStruct((S, Q_RANK), BF16), jax.ShapeDtypeStruct((S, KV_RANK), BF16),
                   jax.ShapeDtypeStruct((S, HEAD), BF16)],
        compiler_params=_params(("parallel",)),
    )(down, gq, gkv, *tabs)


def mla_prep_bwd(name, down, dcq, dckv, dkp, gq, gkv, tabs, *, tm=256):
    def body(dn_ref, dcq_ref, dckv_ref, dkp_ref, gq_ref, gkv_ref, c_ref, s1_ref, s2_ref, dd_ref, dgq_ref, dgkv_ref):
        i = pl.program_id(0)

        def rms_bwd(x, dy, g, dg_ref):
            xh, rstd = _rms(x)
            dxh = dy * g
            dx = rstd * (dxh - xh * jnp.mean(dxh * xh, axis=-1, keepdims=True))
            pg = jnp.sum(dy * xh, axis=0, keepdims=True)

            @pl.when(i == 0)
            def _():
                dg_ref[...] = pg

            @pl.when(i > 0)
            def _():
                dg_ref[...] += pg

            return dx

        dxq = rms_bwd(dn_ref[:, :Q_RANK], dcq_ref[...], gq_ref[...], dgq_ref)
        dd_ref[:, :Q_RANK] = dxq.astype(dd_ref.dtype)
        dxkv = rms_bwd(dn_ref[:, Q_RANK:Q_RANK + KV_RANK], dckv_ref[...], gkv_ref[...], dgkv_ref)
        dd_ref[:, Q_RANK:Q_RANK + KV_RANK] = dxkv.astype(dd_ref.dtype)
        dd_ref[:, Q_RANK + KV_RANK:] = _unrope(dkp_ref[...], c_ref[...], s1_ref[...], s2_ref[...]).astype(dd_ref.dtype)

    tab = pl.BlockSpec((tm, HEAD), lambda i: (i, 0))
    vq = pl.BlockSpec((1, Q_RANK), lambda i: (0, 0))
    vkv = pl.BlockSpec((1, KV_RANK), lambda i: (0, 0))
    return pl.pallas_call(
        body, name=name, grid=(S // tm,),
        in_specs=[pl.BlockSpec((tm, ODD_IN_PAD), lambda i: (i, 0)), pl.BlockSpec((tm, Q_RANK), lambda i: (i, 0)),
                  pl.BlockSpec((tm, KV_RANK), lambda i: (i, 0)), tab, vq, vkv, tab, tab, tab],
        out_specs=[pl.BlockSpec((tm, ODD_IN_PAD), lambda i: (i, 0)), vq, vkv],
        out_shape=[jax.ShapeDtypeStruct((S, ODD_IN_PAD), BF16), jax.ShapeDtypeStruct((1, Q_RANK), F32),
                   jax.ShapeDtypeStruct((1, KV_RANK), F32)],
        compiler_params=_params(("arbitrary",)),
    )(down, dcq, dckv, dkp, gq, gkv, *tabs)


ATT_TQ = 256


def _attn_scores(q_ref, kv_ref, kp_ref, c_ref, s1_ref, s2_ref, nk):
    qn = q_ref[:, :HEAD].astype(BF16)
    qp = _rope(q_ref[:, HEAD:], c_ref[...], s1_ref[...], s2_ref[...]).astype(BF16)
    kn = kv_ref[:nk, :HEAD]
    sc = (_dot(qn, kn, _NT) + _dot(qp, kp_ref[:nk, :], _NT)) * ATT_SCALE
    q_chunk = lax.broadcasted_iota(jnp.int32, (ATT_TQ, ATT_TQ), 0) // CHUNK
    k_chunk = lax.broadcasted_iota(jnp.int32, (ATT_TQ, ATT_TQ), 1) // CHUNK
    own = jnp.where(k_chunk <= q_chunk, sc[:, nk - ATT_TQ:], jnp.finfo(F32).min)
    sc = own if nk == ATT_TQ else jnp.concatenate([sc[:, :nk - ATT_TQ], own], axis=1)
    return sc, qn, qp, kn


def _for_each_prefix(i, fn):
    for k in range(S // ATT_TQ):
        pl.when(i == k)(functools.partial(fn, (k + 1) * ATT_TQ))


def _attn_specs():
    q = pl.BlockSpec((ATT_TQ, QHEAD_PAD), lambda h, i: (i, h))
    kv = pl.BlockSpec((S, QHEAD_PAD), lambda h, i: (0, h))
    kp = pl.BlockSpec((S, HEAD), lambda h, i: (0, 0))
    tab = pl.BlockSpec((ATT_TQ, HEAD), lambda h, i: (i, 0))
    o = pl.BlockSpec((ATT_TQ, HEAD), lambda h, i: (i, h))
    lse = pl.BlockSpec((None, ATT_TQ, 1), lambda h, i: (h, i, 0))
    return q, kv, kp, tab, o, lse


def attn_fwd(name, q, kv, kp, tabs):
    def body(q_ref, kv_ref, kp_ref, c_ref, s1_ref, s2_ref, o_ref, lse_ref):
        i = pl.program_id(1)

        def run(nk):
            sc, _, _, _ = _attn_scores(q_ref, kv_ref, kp_ref, c_ref, s1_ref, s2_ref, nk)
            m = jnp.max(sc, axis=-1, keepdims=True)
            e = jnp.exp(sc - m)
            total = jnp.sum(e, axis=-1, keepdims=True)
            o_ref[...] = (_dot(e, kv_ref[:nk, HEAD:], _NN) * (1.0 / total)).astype(o_ref.dtype)
            lse_ref[...] = m + jnp.log(total)

        _for_each_prefix(i, run)

    qs, kvs, kps, tab, os, ls = _attn_specs()
    return pl.pallas_call(
        body, name=name, grid=(MLA_HEADS, S // ATT_TQ), in_specs=[qs, kvs, kps, tab, tab, tab], out_specs=[os, ls],
        out_shape=[jax.ShapeDtypeStruct((S, MLA_HEADS * HEAD), BF16), jax.ShapeDtypeStruct((MLA_HEADS, S, 1), F32)],
        compiler_params=_params(("parallel", "parallel")),
    )(q, kv, kp, *tabs)


def attn_bwd(name, q, kv, kp, o, lse, do, tabs):
    def body(q_ref, kv_ref, kp_ref, o_ref, lse_ref, do_ref, c_ref, s1_ref, s2_ref, dq_ref, dkv_ref, dkp_ref):
        h, i = pl.program_id(0), pl.program_id(1)

        @pl.when(i == 0)
        def _():
            dkv_ref[...] = jnp.zeros_like(dkv_ref)

        @pl.when((i == 0) & (h == 0))
        def _():
            dkp_ref[...] = jnp.zeros_like(dkp_ref)

        def run(nk):
            sc, qn, qp, kn = _attn_scores(q_ref, kv_ref, kp_ref, c_ref, s1_ref, s2_ref, nk)
            p = jnp.exp(sc - lse_ref[...])
            do_v = do_ref[...]
            delta = jnp.sum(do_v.astype(F32) * o_ref[...].astype(F32), axis=-1, keepdims=True)
            dp = _dot(do_v, kv_ref[:nk, HEAD:], _NT)
            ds = (p * ((dp - delta) * ATT_SCALE)).astype(BF16)
            dq_ref[:, :HEAD] = _dot(ds, kn, _NN).astype(dq_ref.dtype)
            dqp = _unrope(_dot(ds, kp_ref[:nk, :], _NN), c_ref[...], s1_ref[...], s2_ref[...])
            dq_ref[:, HEAD:] = dqp.astype(dq_ref.dtype)
            dkv_ref[:nk, :HEAD] += _dot(ds, qn, _TN)
            dkv_ref[:nk, HEAD:] += _dot(p, do_v, _TN)
            dkp_ref[:nk, :] += _dot(ds, qp, _TN)

        _for_each_prefix(i, run)

    qs, kvs, kps, tab, os, ls = _attn_specs()
    return pl.pallas_call(
        body, name=name, grid=(MLA_HEADS, S // ATT_TQ), in_specs=[qs, kvs, kps, os, ls, os, tab, tab, tab],
        out_specs=[qs, kvs, kps],
        out_shape=[jax.ShapeDtypeStruct((S, MLA_HEADS * QHEAD_PAD), BF16),
                   jax.ShapeDtypeStruct((S, MLA_HEADS * QHEAD_PAD), F32), jax.ShapeDtypeStruct((S, HEAD), F32)],
        compiler_params=_params(("arbitrary", "arbitrary")),
    )(q, kv, kp, o, lse, do, *tabs)


def adamw(name, w, g, m, v):
    rows, cols = w.shape
    tr = rows
    for cand in (512, 256, 128, 64, 32, 16, 8):
        if rows % cand == 0 and cand * cols * 4 <= 2 * 1024 * 1024:
            tr = cand
            break

    def body(w_ref, g_ref, m_ref, v_ref, d_ref, nm_ref, nv_ref):
        g_v = g_ref[...]
        nm = ADAM_B1 * m_ref[...] + (1.0 - ADAM_B1) * g_v
        nv = ADAM_B2 * v_ref[...] + (1.0 - ADAM_B2) * (g_v * g_v)
        m_hat = nm / (1.0 - ADAM_B1 ** ADAM_STEP)
        v_hat = nv / (1.0 - ADAM_B2 ** ADAM_STEP)
        d_ref[...] = (-ADAM_LR) * (m_hat / (jnp.sqrt(v_hat) + ADAM_EPS) + ADAM_WD * w_ref[...])
        nm_ref[...] = nm
        nv_ref[...] = nv

    blk = pl.BlockSpec((tr, cols), lambda i: (i, 0))
    shape = jax.ShapeDtypeStruct((rows, cols), F32)
    return pl.pallas_call(
        body, name=name, grid=(rows // tr,), in_specs=[blk] * 4, out_specs=[blk] * 3, out_shape=[shape] * 3,
        compiler_params=_params(("parallel",)),
    )(w, g, m, v)


def _local_step(x, pos_col, target, sm, weights_of, emit_grads, prefetch):
    inv_freq = ROPE_THETA ** (-jnp.arange(0, QK_ROPE, 2, dtype=F32) / QK_ROPE)
    inv_freq = jnp.concatenate([inv_freq, inv_freq, jnp.zeros((HEAD - QK_ROPE,), F32)])[None, :]
    tabs = rope_tables(pos_col, inv_freq)
    saved, wts = [], {}
    for layer in range(DEPTH):
        j = layer // 2
        n = "l%d_" % layer
        sv = {"x": x}
        wm = wts["mix%d" % layer] = weights_of("mix%d" % layer, x)
        if layer == 0:
            sm = dict(sm, conv_w=wm["conv_w"], gq=wm["gq"], gkv=wm["gkv"])
        if layer % 2 == 0:
            proj = mm_cols(n + "proj", x, wm["w_in"], F32)
            fetched = prefetch("mlp%d" % layer, proj)
            y_pool = pool_fwd(n + "pool", proj, sm["pool_w"][j], sm["pool_scale"][j][None])
            y_lru, hstate = lru_fwd(n + "lru", proj, sm["conv_w"][j], sm["conv_b"][j][None], sm["w_a"][j],
                                    sm["b_a"][j][None], sm["w_x"][j], sm["b_x"][j][None], sm["lam"][j][None])
            ycat = jnp.concatenate([y_pool, y_lru], axis=1)
            mix_in, w_mix = ycat, wm["w_out"]
            sv.update(proj=proj, hstate=hstate, ycat=ycat)
        else:
            down = mm_rows(n + "down", x, wm["w_down"], F32)
            fetched = prefetch("mlp%d" % layer, down)
            cq, ckv, kp = mla_prep(n + "prep", down, sm["gq"][j][None], sm["gkv"][j][None], tabs)
            q = mm_cols(n + "q", cq, wm["w_qb"], F32)
            kv = mm_cols(n + "kv", ckv, wm["w_kvb"], BF16)
            o, lse = attn_fwd(n + "attn", q, kv, kp, tabs)
            mix_in, w_mix = o, wm["w_o"]
            sv.update(down=down, cq=cq, ckv=ckv, kp=kp, q=q, kv=kv, o=o, lse=lse)
        x1, xhat1, rstd1 = mm_rows(n + "mixout", mix_in, w_mix, F32, after=fetched,
                                   ln=(x, sm["ln_mix_g"][layer][None], sm["ln_mix_b"][layer][None]))
        wf = wts["mlp%d" % layer] = weights_of("mlp%d" % layer, x1)
        x2, xhat2, rstd2, relu = mlp_fwd(n + "mlp", x1, wf["w1"], wf["w2"], sm["ln_ffn_g"][layer][None],
                                         sm["ln_ffn_b"][layer][None])
        sv.update(xhat1=xhat1, rstd1=rstd1, x1=x1, relu=relu, xhat2=xhat2, rstd2=rstd2)
        saved.append(sv)
        x = x2

    gs = {k: [None] * (DEPTH if k.startswith("ln_") else DEPTH // 2) for k in sm}
    last = saved[DEPTH - 1]
    dz, gs["ln_ffn_g"][DEPTH - 1], gs["ln_ffn_b"][DEPTH - 1], loss_tile = loss_head(
        x, target, last["xhat2"], last["rstd2"], sm["ln_ffn_g"][DEPTH - 1][None])
    sent = None
    for layer in reversed(range(DEPTH)):
        j = layer // 2
        n = "l%d_" % layer
        sv = saved[layer]
        wm, wf = wts["mix%d" % layer], wts["mlp%d" % layer]
        dz, g_w1, g_w2, gs["ln_mix_g"][layer], gs["ln_mix_b"][layer] = mlp_bwd(
            n + "mlp_b", dz, sv["relu"], sv["x1"], wf["w1"], wf["w2"],
            (sv["xhat1"], sv["rstd1"], sm["ln_mix_g"][layer][None]), after=sent)
        sent = emit_grads("mlp%d" % layer, {"w1": g_w1, "w2": g_w2})
        below = saved[layer - 1] if layer else None
        ln_below = (below["xhat2"], below["rstd2"], sm["ln_ffn_g"][layer - 1][None]) if layer else None
        if layer % 2 == 0:
            dycat, g_out = linear_bwd(n + "mixout_b", sv["ycat"], dz, wm["w_out"], col_sharded=False, da_dtype=F32,
                                      after=sent)
            du_pool, gs["pool_w"][j], gs["pool_scale"][j] = pool_bwd(
                n + "pool_b", sv["proj"], dycat, sm["pool_w"][j], sm["pool_scale"][j][None])
            (du_lru, du_gate, gs["w_a"][j], gs["w_x"][j], gs["b_a"][j], gs["b_x"][j], gs["lam"][j], gs["conv_w"][j],
             gs["conv_b"][j]) = lru_bwd(n + "lru_b", sv["proj"], sv["hstate"], dycat, sm["conv_w"][j],
                                        sm["conv_b"][j][None], sm["w_a"][j], sm["b_a"][j][None], sm["w_x"][j],
                                        sm["b_x"][j][None], sm["lam"][j][None])
            dproj = jnp.concatenate([du_pool, du_lru, du_gate], axis=1)
            res = linear_bwd(n + "proj_b", sv["x"], dproj, wm["w_in"], col_sharded=True, da_dtype=F32, resid=dz,
                             ln=ln_below)
            grads = {"w_in": res[1], "w_out": g_out}
        else:
            do, g_o = linear_bwd(n + "attnout_b", sv["o"], dz, wm["w_o"], col_sharded=False, da_dtype=BF16, after=sent)
            dq, dkv, dkp = attn_bwd(n + "attn_b", sv["q"], sv["kv"], sv["kp"], sv["o"], sv["lse"], do, tabs)
            dcq, g_qb = linear_bwd(n + "q_b", sv["cq"], dq, wm["w_qb"], col_sharded=True, da_dtype=F32)
            dckv, g_kvb = linear_bwd(n + "kv_b", sv["ckv"], dkv, wm["w_kvb"], col_sharded=True, da_dtype=F32)
            ddown, gs["gq"][j], gs["gkv"][j] = mla_prep_bwd(
                n + "prep_b", sv["down"], dcq, dckv, dkp, sm["gq"][j][None], sm["gkv"][j][None], tabs)
            res = linear_bwd(n + "down_b", sv["x"], ddown, wm["w_down"], col_sharded=False, da_dtype=F32, resid=dz,
                             ln=ln_below)
            grads = {"w_down": res[1], "w_qb": g_qb, "w_kvb": g_kvb, "w_o": g_o}
        dz = res[0]
        if layer:
            gs["ln_ffn_g"][layer - 1], gs["ln_ffn_b"][layer - 1] = res[2], res[3]
        sent = emit_grads("mix%d" % layer, grads)
    gs = {k: jnp.stack([a.reshape(sm[k].shape[1:]) for a in v]) for k, v in gs.items()}
    return loss_tile[0, 0], dz, gs, sent


def _place():
    x, y, c = lax.axis_index("x"), lax.axis_index("y"), lax.axis_index("c")
    chips = [(1 - x, y), (x, 1 - y), (1 - x, 1 - y)]
    return x, y, c, chips


def _hbm_call(body, name, args, out_shape, scratch, aliases=None):
    return pl.pallas_call(
        body, name=name, in_specs=[pl.BlockSpec(memory_space=pl.ANY)] * len(args),
        out_specs=[pl.BlockSpec(memory_space=pl.ANY)] * len(out_shape), out_shape=out_shape,
        scratch_shapes=scratch, input_output_aliases=aliases or {},
        compiler_params=pltpu.CompilerParams(has_side_effects=True),
    )(*args)


HBM_SPEC = pl.BlockSpec(memory_space=pltpu.HBM)
SEM_SPEC = pl.BlockSpec(memory_space=pltpu.SEMAPHORE)
EFFECT = pltpu.SideEffectType.DATAFLOW_SIDE_EFFECTING


def _remote(src, dst, send_sem, recv_sem, device):
    return pltpu.make_async_remote_copy(src_ref=src, dst_ref=dst, send_sem=send_sem, recv_sem=recv_sem,
                                        device_id=device, device_id_type=MESH)


def place_own(name, srcs, place, steps=4):
    n = len(srcs)
    in_specs, out_specs, out_shape = [], [], []
    for s in srcs:
        rows, cols = s.shape
        tr = rows // steps
        in_specs.append(pl.BlockSpec((tr, cols), lambda i, p: (i, 0)))
        out_specs.append(pl.BlockSpec((None, tr, cols), lambda i, p: (p[0], i, 0)))
        out_shape.append(jax.ShapeDtypeStruct((N_CHIPS, rows, cols), s.dtype))

    def body(p_ref, *refs):
        for i_ref, o_ref in zip(refs[:n], refs[n:]):
            o_ref[...] = i_ref[...]

    return pl.pallas_call(
        body, name=name, out_shape=out_shape,
        grid_spec=pltpu.PrefetchScalarGridSpec(num_scalar_prefetch=1, grid=(steps,), in_specs=in_specs,
                                               out_specs=out_specs),
        compiler_params=_params(("parallel",)),
    )(place, *srcs)


def split_start(name, groups, n_sems, plan, token_in=None):
    sizes = [len(srcs) for srcs, _ in groups]
    n, n_groups = sum(sizes), len(groups)
    srcs = [pltpu.with_memory_space_constraint(a, pltpu.HBM) for s, _ in groups for a in s]
    lands = [pltpu.with_memory_space_constraint(a, pltpu.HBM) for _, l in groups for a in l]
    extra = [] if token_in is None else [token_in]

    def body(*refs):
        src_refs, land_refs = refs[:n], refs[n:2 * n]
        outs = refs[2 * n + len(extra):]
        off = 0
        for g, size in enumerate(sizes):
            sends, _ = plan(src_refs[off:off + size], land_refs[off:off + size], outs[2 * g], outs[2 * g + 1])
            for cp in sends:
                cp.start()
            off += size
        token = outs[2 * n_groups + 2 * n]
        token[...] = jnp.zeros_like(token)

    sems = [pltpu.SemaphoreType.DMA((size * n_sems,)) for size in sizes for _ in range(2)]
    res = pl.pallas_call(
        body, name=name,
        out_shape=(*sems, *[pltpu.HBM(a.shape, a.dtype) for a in srcs + lands], jax.ShapeDtypeStruct((8, 128), F32)),
        in_specs=[HBM_SPEC] * (2 * n) + [pl.BlockSpec(memory_space=pl.ANY)] * len(extra),
        out_specs=(*[SEM_SPEC] * len(sems), *[HBM_SPEC] * (2 * n), pl.BlockSpec(memory_space=pltpu.VMEM)),
        input_output_aliases={i: len(sems) + i for i in range(2 * n)},
        compiler_params=pltpu.CompilerParams(has_side_effects=EFFECT),
    )(*srcs, *lands, *extra)
    started, off = [], 0
    bufs = res[len(sems):]
    for g, size in enumerate(sizes):
        started.append(dict(send=res[2 * g], recv=res[2 * g + 1], srcs=list(bufs[off:off + size]),
                            lands=list(bufs[n + off:n + off + size]), plan=plan))
        off += size
    return started, res[-1]


def _wait_started(st, bufs, send_sems, recv_sems):
    n = len(st["srcs"])
    sends, expects = st["plan"](bufs[:n], bufs[n:], send_sems, recv_sems)
    for cp in sends:
        cp.wait_send()
    for cp in expects:
        cp.wait_recv()


def split_wait(name, started, after):
    sizes = [len(st["srcs"]) + len(st["lands"]) for st in started]
    n_buf = sum(sizes)

    def body(*refs):
        bufs, sems = refs[:n_buf], refs[n_buf:n_buf + 2 * len(started)]
        off = 0
        for g, (st, n) in enumerate(zip(started, sizes)):
            _wait_started(st, bufs[off:off + n], sems[2 * g], sems[2 * g + 1])
            off += n

    bufs = [a for st in started for a in st["srcs"] + st["lands"]]
    sems = [s for st in started for s in (st["send"], st["recv"])]
    res = pl.pallas_call(
        body, name=name, out_shape=tuple(pltpu.HBM(a.shape, a.dtype) for a in bufs),
        in_specs=[HBM_SPEC] * n_buf + [SEM_SPEC] * len(sems) + [pl.BlockSpec(memory_space=pl.ANY)],
        out_specs=tuple([HBM_SPEC] * n_buf), input_output_aliases={i: i for i in range(n_buf)},
        compiler_params=pltpu.CompilerParams(has_side_effects=EFFECT),
    )(*bufs, *sems, after)
    out, off = [], 0
    for st, n in zip(started, sizes):
        out.append((list(res[off:off + len(st["srcs"])]), list(res[off + len(st["srcs"]):off + n])))
        off += n
    return out


def split_relay(name, st, n_sems, plan, after):
    n_src, n = len(st["srcs"]), len(st["lands"])

    def body(*refs):
        bufs = refs[:n_src + n]
        outs = refs[n_src + n + 3:]
        _wait_started(st, bufs, refs[n_src + n], refs[n_src + n + 1])
        sends, _ = plan((), bufs[n_src:], outs[0], outs[1])
        for cp in sends:
            cp.start()
        outs[2 + n][...] = jnp.zeros((8, 128), F32)

    bufs = st["srcs"] + st["lands"]
    res = pl.pallas_call(
        body, name=name,
        out_shape=(pltpu.SemaphoreType.DMA((n * n_sems,)), pltpu.SemaphoreType.DMA((n * n_sems,)),
                   *[pltpu.HBM(a.shape, a.dtype) for a in st["lands"]], jax.ShapeDtypeStruct((8, 128), F32)),
        in_specs=[HBM_SPEC] * (n_src + n) + [SEM_SPEC] * 2 + [pl.BlockSpec(memory_space=pl.ANY)],
        out_specs=(SEM_SPEC, SEM_SPEC, *[HBM_SPEC] * n, pl.BlockSpec(memory_space=pltpu.VMEM)),
        input_output_aliases={n_src + i: 2 + i for i in range(n)},
        compiler_params=pltpu.CompilerParams(has_side_effects=EFFECT),
    )(*bufs, st["send"], st["recv"], after)
    return dict(send=res[0], recv=res[1], srcs=[], lands=list(res[2:2 + n]), plan=plan), res[-1]


def gather_plan(src_refs, land_refs, send_sems, recv_sems):
    x, y, c, chips = _place()
    me = 2 * x + y
    sends, expects = [], []
    for k, (s, d) in enumerate(zip(src_refs, land_refs)):
        mine = pl.ds(c * (s.shape[0] // 2), s.shape[0] // 2)
        for j, (px, py) in enumerate(chips):
            sem = 3 * k + j
            sends.append(_remote(s.at[mine], d.at[me, mine], send_sems.at[sem], recv_sems.at[sem], (px, py, c)))
            expects.append(_remote(s.at[mine], d.at[2 * px + py, mine], send_sems.at[sem], recv_sems.at[sem], (px, py, c)))
    return sends, expects


def relay_plan(src_refs, land_refs, send_sems, recv_sems):
    x, y, c, chips = _place()
    sends, expects = [], []
    for k, d in enumerate(land_refs):
        hr = d.shape[1] // 2
        mine, theirs = pl.ds(c * hr, hr), pl.ds((1 - c) * hr, hr)
        for j, (px, py) in enumerate(chips):
            sem, chip = 3 * k + j, 2 * px + py
            sends.append(_remote(d.at[chip, mine], d.at[chip, mine], send_sems.at[sem], recv_sems.at[sem], (x, y, 1 - c)))
            expects.append(_remote(d.at[chip, mine], d.at[chip, theirs], send_sems.at[sem], recv_sems.at[sem], (x, y, 1 - c)))
    return sends, expects


def _reduce_part(s, chip_idx, h):
    hr = s.shape[1] // 2
    return s.at[chip_idx if s.shape[0] == N_CHIPS else 0, pl.ds(h * hr, hr)]


def reduce_plan(src_refs, land_refs, send_sems, recv_sems):
    x, y, c, chips = _place()
    me_chip, me_dev = 2 * x + y, 4 * x + 2 * y + c
    sends, expects = [], []
    for k, (s, d) in enumerate(zip(src_refs, land_refs)):
        part = functools.partial(_reduce_part, s)
        for j, (px, py) in enumerate(chips):
            for h in range(2):
                sends.append(_remote(part(2 * px + py, h), d.at[me_dev], send_sems.at[7 * k + 2 * j + h],
                                     recv_sems.at[7 * k + 2 * j + c], (px, py, h)))
                expects.append(_remote(part(me_chip, c), d.at[4 * px + 2 * py + h], send_sems.at[7 * k + 2 * j + h],
                                       recv_sems.at[7 * k + 2 * j + h], (px, py, h)))
        sends.append(_remote(part(me_chip, 1 - c), d.at[me_dev], send_sems.at[7 * k + 6], recv_sems.at[7 * k + 6],
                             (x, y, 1 - c)))
        expects.append(_remote(part(me_chip, c), d.at[me_dev + 1 - 2 * c], send_sems.at[7 * k + 6],
                               recv_sems.at[7 * k + 6], (x, y, 1 - c)))
    return sends, expects


def sibling_swap_halves(name, fulls):
    n = len(fulls)

    def body(*refs):
        outs = refs[n:2 * n]
        send_sems, recv_sems = refs[2 * n:]
        x, y, c, _ = _place()
        copies = []
        for k in range(n):
            nl, rows = fulls[k].shape[:2]
            hr = rows // 2
            mine = outs[k].at[pl.ds(0, nl), pl.ds(c * hr, hr)]
            theirs = outs[k].at[pl.ds(0, nl), pl.ds((1 - c) * hr, hr)]
            copies.append((_remote(mine, mine, send_sems.at[k], recv_sems.at[k], (x, y, 1 - c)),
                           _remote(mine, theirs, send_sems.at[k], recv_sems.at[k], (x, y, 1 - c))))
        for send, _ in copies:
            send.start()
        for send, recv in copies:
            send.wait_send()
            recv.wait_recv()

    out_shape = [jax.ShapeDtypeStruct(f.shape, f.dtype) for f in fulls]
    scratch = [pltpu.SemaphoreType.DMA((n,)), pltpu.SemaphoreType.DMA((n,))]
    return _hbm_call(body, name, fulls, out_shape, scratch, aliases={k: k for k in range(n)})


N_DEV = 8


def sum_devices(name, items, place, steps):
    n = len(items)
    in_specs, args, out_specs, out_shape, aliases = [], [place], [], [], {}
    for landed, own, layer, n_layers, _ in items:
        _, hr, cols = landed.shape
        tr = hr // steps
        slot = (lambda p: p[0]) if own.shape[0] == N_CHIPS else (lambda p: 0)
        in_specs += [pl.BlockSpec((N_DEV, tr, cols), lambda r, p: (0, r, 0)),
                     pl.BlockSpec((None, tr, cols), lambda r, p, slot=slot: (slot(p), p[1] * steps + r, 0))]
        args += [landed, own]
        out_specs.append(pl.BlockSpec((None, tr, cols), lambda r, p, layer=layer: (layer, p[1] * steps + r, 0)))
        out_shape.append(jax.ShapeDtypeStruct((n_layers, 2 * hr, cols), F32))
    for k, item in enumerate(items):
        if item[4] is not None:
            in_specs.append(pl.BlockSpec(memory_space=pl.ANY))
            aliases[len(args)] = k
            args.append(item[4])
    n_prev = len(aliases)

    def body(p_ref, *refs):
        for k in range(n):
            r_ref, own_ref, o_ref = refs[2 * k], refs[2 * k + 1], refs[2 * n + n_prev + k]
            mine = own_ref[...].astype(F32)
            acc = jnp.zeros_like(mine)
            for d in range(N_DEV):
                acc = acc + jnp.where(p_ref[2] == d, mine, r_ref[d].astype(F32))
            o_ref[...] = acc

    return pl.pallas_call(
        body, name=name, out_shape=out_shape,
        grid_spec=pltpu.PrefetchScalarGridSpec(num_scalar_prefetch=1, grid=(steps,), in_specs=in_specs,
                                               out_specs=out_specs),
        input_output_aliases=aliases, compiler_params=_params(("parallel",)),
    )(*args)


def _pack(arrs, rows_multiple):
    flat = []
    for a in arrs:
        v = a.reshape(-1).astype(F32)
        flat.append(jnp.pad(v, (0, (-v.shape[0]) % HEAD)))
    v = jnp.concatenate(flat)
    v = jnp.pad(v, (0, (-v.shape[0]) % (HEAD * rows_multiple)))
    return v.reshape(-1, HEAD)


def _unpack(packed, shapes):
    flat = packed.reshape(-1)
    out, off = [], 0
    for shp in shapes:
        size = int(np.prod(shp))
        out.append(flat[off:off + size].reshape(shp))
        off += size + (-size) % HEAD
    return out


BIG = ["even_w_in", "even_w_out", "mla_w_down", "mla_w_qb", "mla_w_kvb", "mla_w_o", "mlp_w1", "mlp_w2"]
BIG_KEY = {"even_w_in": "w_in", "even_w_out": "w_out", "mla_w_down": "w_down", "mla_w_qb": "w_qb",
           "mla_w_kvb": "w_kvb", "mla_w_o": "w_o", "mlp_w1": "w1", "mlp_w2": "w2"}
SMALL_KEY = {"ln_mix_g": "ln_mix_g", "ln_mix_b": "ln_mix_b", "ln_ffn_g": "ln_ffn_g", "ln_ffn_b": "ln_ffn_b",
             "pool_w": "pool_w", "pool_scale": "pool_scale", "lru_conv_w": "conv_w", "lru_conv_b": "conv_b",
             "lru_w_a": "w_a", "lru_b_a": "b_a", "lru_w_x": "w_x", "lru_b_x": "b_x", "lru_lambda": "lam",
             "mla_q_norm_g": "gq", "mla_kv_norm_g": "gkv"}
SMALL = list(SMALL_KEY)
SMALL_SHARDED = ["lru_conv_w", "mla_q_norm_g", "mla_kv_norm_g"]
SMALL_MATRICES = ["pool_w", "lru_w_a", "lru_w_x"]
WEIGHTS = ["ln_mix_g", "ln_mix_b", "ln_ffn_g", "ln_ffn_b", "even_w_in", "pool_w", "pool_scale", "lru_conv_w",
           "lru_conv_b", "lru_w_a", "lru_b_a", "lru_w_x", "lru_b_x", "lru_lambda", "even_w_out", "mla_w_down",
           "mla_q_norm_g", "mla_kv_norm_g", "mla_w_qb", "mla_w_kvb", "mla_w_o", "mlp_w1", "mlp_w2"]


GROUPS = ["mix0", "mlp0", "mix1", "mlp1", "mix2", "mlp2", "mix3", "mlp3"]


def _group_keys(group):
    layer = int(group[3:])
    if group.startswith("mlp"):
        return [("mlp_w1", "w1", layer), ("mlp_w2", "w2", layer)]
    if layer % 2 == 0:
        return [("even_w_in", "w_in", layer // 2), ("even_w_out", "w_out", layer // 2)]
    return [("mla_w_down", "w_down", layer // 2), ("mla_w_qb", "w_qb", layer // 2),
            ("mla_w_kvb", "w_kvb", layer // 2), ("mla_w_o", "w_o", layer // 2)]


def _pad_q_heads(w):
    lead = w.shape[:-1]
    w = w.reshape(lead + (2, QK_NOPE + QK_ROPE))
    w = jnp.pad(w, ((0, 0),) * len(lead) + ((0, 0), (0, QHEAD_PAD - QK_NOPE - QK_ROPE)))
    return w.reshape(lead + (2 * QHEAD_PAD,))


def _unpad_q_heads(g):
    lead = g.shape[:-1]
    return g.reshape(lead + (2, QHEAD_PAD))[..., :QK_NOPE + QK_ROPE].reshape(lead + (2 * (QK_NOPE + QK_ROPE),))


def _step(x, positions, loss_target, w, m, v):
    cx, cy, cc = lax.axis_index("x"), lax.axis_index("y"), lax.axis_index("c")
    chip = 2 * cx + cy
    place = jnp.stack([chip, cc, 2 * chip + cc]).astype(jnp.int32)

    prepared = dict(w)
    prepared["mla_w_down"] = jnp.pad(w["mla_w_down"], ((0, 0), (0, 0), (0, ODD_IN_PAD - ODD_IN)))
    prepared["mla_w_qb"] = _pad_q_heads(w["mla_w_qb"])
    small_shard_shapes = [w[k].shape for k in SMALL_SHARDED]
    sources = {g: [prepared[name][idx].astype(BF16) for name, _, idx in _group_keys(g)] for g in GROUPS}
    sources[GROUPS[0]].append(_pack([w[k] for k in SMALL_SHARDED], 32))
    gathering, token = {}, None
    for name, part in (("first", GROUPS[:1]), ("rest", GROUPS[1:])):
        zones = place_own("own_" + name, [a for g in part for a in sources[g]], place)
        groups, off = [], 0
        for g in part:
            groups.append((sources[g], zones[off:off + len(sources[g])]))
            off += len(sources[g])
        started, token = split_start("gather_" + name, groups, 3, gather_plan, token)
        gathering.update(zip(part, started))
    relayed, relay_token = {}, {}

    def prefetch(g, after):
        relayed[g], relay_token[g] = split_relay("relay_" + g, gathering[g], 3, relay_plan, after)
        return relay_token[g]

    prefetch(GROUPS[0], token)

    def weights_of(g, after):
        _, lands = split_wait("gathered_" + g, [relayed[g]], relay_token[g] if g == GROUPS[0] else after)[0]
        if g.startswith("mlp") and g != GROUPS[-1]:
            prefetch(GROUPS[GROUPS.index(g) + 1], lands[0])
        out = {key: land for (_, key, _), land in zip(_group_keys(g), lands)}
        if g == GROUPS[0]:
            per_chip = [_unpack(lands[-1][s], small_shard_shapes) for s in range(N_CHIPS)]
            for i, key in enumerate(("conv_w", "gq", "gkv")):
                out[key] = jnp.concatenate([p[i] for p in per_chip], axis=-1)
        return out

    reducing = []

    def reduce_start(g, srcs, token_in=None):
        lands = [lax.empty((N_DEV, s.shape[1] // 2, s.shape[2]), s.dtype) for s in srcs]
        started, token = split_start("reduce_" + g, [(srcs, lands)], 7, reduce_plan, token_in)
        reducing.append((g, started[0]))
        return token

    def emit_grads(g, grads):
        return reduce_start(g, [grads[key] for _, key, _ in _group_keys(g)])

    sm = {SMALL_KEY[k]: w[k] for k in SMALL if k not in SMALL_SHARDED}
    loss, grad_x, gs, last_sent = _local_step(x[0], positions.reshape(S, 1), loss_target[0], sm, weights_of,
                                              emit_grads, prefetch)
    loss = lax.psum(loss, ("x", "y", "c"))

    packs = [_pack([gs[SMALL_KEY[k]] for k in SMALL if (k in SMALL_MATRICES) == mat], 128)[None] for mat in (False, True)]
    small_sent = reduce_start("small", [packs[0], packs[1].astype(BF16)], last_sent)

    grad, delta, new_m, new_v = {}, {}, {}, {}
    late_groups = ("mix0", "small")
    stacks, after = {}, small_sent
    for late in (False, True):
        part = [(g, st) for g, st in reducing if (g in late_groups) == late]
        landed = split_wait("reduced_late" if late else "reduced_early", [st for _, st in part], after)
        for (g, _), (owns, lands) in zip(part, landed):
            if g == "small":
                items = [(land, own, 0, 1, None) for own, land in zip(owns, lands)]
                stacks["small_vec"], stacks["small_mat"] = sum_devices("sum_small", items, place, 4)
                continue
            keys = _group_keys(g)
            items = [(land, own, idx, w[name].shape[0], stacks.get(name))
                     for (name, _, idx), own, land in zip(keys, owns, lands)]
            for (name, _, _), stack in zip(keys, sum_devices("sum_" + g, items, place, 4)):
                stacks[name] = stack
        names = [k for k in BIG if (k in ("even_w_in", "even_w_out")) == late] + (["small_vec", "small_mat"] if late else [])
        reduced = dict(zip(names, sibling_swap_halves("swap_late" if late else "swap_early", [stacks[k] for k in names])))
        if not late:
            reduced["mla_w_down"] = reduced["mla_w_down"][..., :ODD_IN]
            reduced["mla_w_qb"] = _unpad_q_heads(reduced["mla_w_qb"])
        for k in names:
            if k.startswith("small"):
                continue
            grad[k] = reduced[k]
            shp = w[k].shape
            view = lambda a: a.reshape(-1, shp[-1])
            d, nm, nv = adamw("adamw_" + BIG_KEY[k], view(w[k]), view(grad[k]), view(m[k]), view(v[k]))
            delta[k], new_m[k], new_v[k] = d.reshape(shp), nm.reshape(shp), nv.reshape(shp)
            after = d
    g_small = {}
    for mat, key in ((False, "small_vec"), (True, "small_mat")):
        names = [k for k in SMALL if (k in SMALL_MATRICES) == mat]
        g_small.update(zip(names, _unpack(reduced[key], [gs[SMALL_KEY[k]].shape for k in names])))
    for k in SMALL_SHARDED:
        width = w[k].shape[-1]
        g_small[k] = lax.dynamic_slice_in_dim(g_small[k], chip * width, width, axis=-1)
    grad.update(g_small)

    shapes = [w[k].shape for k in SMALL]
    d, nm, nv = adamw("adamw_small", *[_pack([t[k] for k in SMALL], 512) for t in (w, grad, m, v)])
    for k, dk, mk, vk in zip(SMALL, _unpack(d, shapes), _unpack(nm, shapes), _unpack(nv, shapes)):
        delta[k], new_m[k], new_v[k] = dk, mk, vk
    return (loss, grad_x[None], *[grad[k] for k in WEIGHTS], *[delta[k] for k in WEIGHTS],
            *[new_m[k] for k in WEIGHTS], *[new_v[k] for k in WEIGHTS])


def kernel(x, positions, ln_mix_g, ln_mix_b, ln_ffn_g, ln_ffn_b, even_w_in, pool_w, pool_scale, lru_conv_w, lru_conv_b, lru_w_a, lru_b_a, lru_w_x, lru_b_x, lru_lambda, even_w_out, mla_w_down, mla_q_norm_g, mla_kv_norm_g, mla_w_qb, mla_w_kvb, mla_w_o, mlp_w1, mlp_w2, loss_target, m_ln_mix_g, m_ln_mix_b, m_ln_ffn_g, m_ln_ffn_b, m_even_w_in, m_pool_w, m_pool_scale, m_lru_conv_w, m_lru_conv_b, m_lru_w_a, m_lru_b_a, m_lru_w_x, m_lru_b_x, m_lru_lambda, m_even_w_out, m_mla_w_down, m_mla_q_norm_g, m_mla_kv_norm_g, m_mla_w_qb, m_mla_w_kvb, m_mla_w_o, m_mlp_w1, m_mlp_w2, v_ln_mix_g, v_ln_mix_b, v_ln_ffn_g, v_ln_ffn_b, v_even_w_in, v_pool_w, v_pool_scale, v_lru_conv_w, v_lru_conv_b, v_lru_w_a, v_lru_b_a, v_lru_w_x, v_lru_b_x, v_lru_lambda, v_even_w_out, v_mla_w_down, v_mla_q_norm_g, v_mla_kv_norm_g, v_mla_w_qb, v_mla_w_kvb, v_mla_w_o, v_mlp_w1, v_mlp_w2):
    args = locals()
    w = {k: args[k] for k in WEIGHTS}
    m = {k: args["m_" + k] for k in WEIGHTS}
    v = {k: args["v_" + k] for k in WEIGHTS}
    return _step(x, positions, loss_target, w, m, v)
```

```python
import functools
import math

import jax
import jax.numpy as jnp
import numpy as np
from jax import lax
from jax.experimental import pallas as pl
from jax.experimental.pallas import tpu as pltpu

F32 = jnp.float32
BF16 = jnp.bfloat16

S = 2048
D = 1024
DEPTH = 4
N_CHIPS = 4
POOL_WINDOWS = (2, 4, 8, 16)
POOL_W = 512
LRU_W = 1024
LRU_HEADS = 8
HEAD = 128
MLA_HEADS = 8
QK_NOPE = 128
QK_ROPE = 64
Q_RANK = 384
KV_RANK = 256
ODD_IN = 704
ODD_IN_PAD = 768
QHEAD_PAD = 256
CHUNK = 64
ALPHA = (2 * DEPTH) ** 0.25
LN_EPS = 1e-5
RMS_EPS = 1e-6
ATT_SCALE = (QK_NOPE + QK_ROPE) ** -0.5
ROPE_THETA = 10000.0
LRU_C = 8.0
ADAM_LR = 0.001
ADAM_B1 = 0.9
ADAM_B2 = 0.999
ADAM_EPS = 1e-08
ADAM_WD = 0.01
ADAM_STEP = 10

VMEM_LIMIT = 56 * 1024 * 1024
MESH = pl.DeviceIdType.MESH

_NN = (((1,), (0,)), ((), ()))
_NT = (((1,), (1,)), ((), ()))
_TN = (((0,), (0,)), ((), ()))


def _params(sem=None, **kw):
    return pltpu.CompilerParams(dimension_semantics=sem, vmem_limit_bytes=VMEM_LIMIT, **kw)


def _dot(a, b, dims):
    return lax.dot_general(a.astype(BF16), b.astype(BF16), dims, preferred_element_type=F32)


def _whole(w4):
    return pl.BlockSpec(tuple(w4.shape), lambda i: (0, 0, 0))


def mm_cols(name, a, w4, out_dtype, *, tm=512):
    m, k = a.shape
    nb = w4.shape[2]

    def body(a_ref, w_ref, o_ref):
        a_v = a_ref[...].astype(BF16)
        for s in range(N_CHIPS):
            o_ref[:, s * nb:(s + 1) * nb] = _dot(a_v, w_ref[s], _NN).astype(o_ref.dtype)

    return pl.pallas_call(
        body, name=name, grid=(m // tm,), in_specs=[pl.BlockSpec((tm, k), lambda i: (i, 0)), _whole(w4)],
        out_specs=pl.BlockSpec((tm, N_CHIPS * nb), lambda i: (i, 0)),
        out_shape=jax.ShapeDtypeStruct((m, N_CHIPS * nb), out_dtype), compiler_params=_params(("parallel",)),
    )(a, w4)


def mm_rows(name, a, w4, out_dtype, *, tm=512, ln=None, after=None):
    m = a.shape[0]
    kb, n = w4.shape[1:]
    n_after = 0 if after is None else 1

    def body(a_ref, w_ref, *rest):
        rest = rest[n_after:]
        acc = _dot(a_ref[...], w_ref[...].reshape(N_CHIPS * kb, n), _NN)
        if ln is None:
            rest[0][...] = acc.astype(rest[0].dtype)
        else:
            x_ref, g_ref, b_ref, y_ref, xhat_ref, rstd_ref = rest
            y_ref[...], xhat_ref[...], rstd_ref[...] = _layer_norm(ALPHA * x_ref[...] + acc, g_ref[...], b_ref[...])

    row = pl.BlockSpec((tm, n), lambda i: (i, 0))
    vec = pl.BlockSpec((1, n), lambda i: (0, 0))
    in_specs = [pl.BlockSpec((tm, N_CHIPS * kb), lambda i: (i, 0)), _whole(w4)] + [pl.BlockSpec(memory_space=pl.ANY)] * n_after
    if ln is None:
        extras, out_specs, out_shape = (), row, jax.ShapeDtypeStruct((m, n), out_dtype)
    else:
        extras, in_specs = ln, in_specs + [row, vec, vec]
        out_specs = [row, row, pl.BlockSpec((tm, 1), lambda i: (i, 0))]
        out_shape = [jax.ShapeDtypeStruct((m, n), F32), jax.ShapeDtypeStruct((m, n), F32), jax.ShapeDtypeStruct((m, 1), F32)]
    return pl.pallas_call(
        body, name=name, grid=(m // tm,), in_specs=in_specs, out_specs=out_specs, out_shape=out_shape,
        compiler_params=_params(("parallel",)),
    )(a, w4, *([after] * n_after), *extras)


def _ln_bwd_rows(dy, xhat, rstd, g):
    dxh = dy * g
    m1 = jnp.mean(dxh, axis=-1, keepdims=True)
    m2 = jnp.mean(dxh * xhat, axis=-1, keepdims=True)
    dz = rstd * (dxh - m1 - xhat * m2)
    return dz, jnp.sum(dy * xhat, axis=0, keepdims=True), jnp.sum(dy, axis=0, keepdims=True)


def linear_bwd(name, a, g, w4, *, col_sharded, da_dtype, resid=None, ln=None, tm=512, after=None, g_more=None):
    m = a.shape[0]
    n_red = m // tm
    ka, ng = w4.shape[1:]
    k_all = a.shape[1]
    n_after = 0 if after is None else 1
    n_resid = 0 if resid is None else 1
    n_ln = 0 if ln is None else 3
    n_more = 0 if g_more is None else 1

    def body(a_ref, g_ref, w_ref, *rest):
        more_ref = rest[0] if n_more else None
        rest = rest[n_more + n_after:]
        ln_refs = rest[n_resid:n_resid + n_ln]
        da_ref, dw_ref = rest[n_resid + n_ln:n_resid + n_ln + 2]
        acc_ref = rest[-1]
        k = pl.program_id(0)

        @pl.when(k == 0)
        def _():
            acc_ref[...] = jnp.zeros_like(acc_ref)

        a_v, g_v = a_ref[...].astype(BF16), g_ref[...].astype(BF16)
        if n_more:
            g_v = jnp.concatenate([g_v, more_ref[...].astype(BF16)], axis=1)
        extra = ALPHA * rest[0][...] if n_resid else None
        if col_sharded:
            da = extra
            for s in range(N_CHIPS):
                g_s = g_v[:, s * ng:(s + 1) * ng]
                p = _dot(g_s, w_ref[s], _NT)
                da = p if da is None else da + p
                acc_ref[s] += _dot(a_v, g_s, _TN)
        else:
            parts = []
            for s in range(N_CHIPS):
                parts.append(_dot(g_v, w_ref[s], _NT))
                acc_ref[s] += _dot(a_v[:, s * ka:(s + 1) * ka], g_v, _TN)
            da = jnp.concatenate(parts, axis=1)
            da = da if extra is None else da + extra
        if n_ln:
            dg_ref, db_ref = rest[n_resid + n_ln + 2:n_resid + n_ln + 4]
            da, pg, pb = _ln_bwd_rows(da, ln_refs[0][...], ln_refs[1][...], ln_refs[2][...])

            @pl.when(k == 0)
            def _():
                dg_ref[...] = jnp.zeros_like(dg_ref)
                db_ref[...] = jnp.zeros_like(db_ref)

            dg_ref[...] += pg
            db_ref[...] += pb
        da_ref[...] = da.astype(da_ref.dtype)

        @pl.when(k == n_red - 1)
        def _():
            dw_ref[...] = acc_ref[...].astype(dw_ref.dtype)

    row = lambda width: pl.BlockSpec((tm, width), lambda i: (i, 0))
    vec = pl.BlockSpec((1, k_all), lambda i: (0, 0))
    whole = pl.BlockSpec((N_CHIPS, ka, ng), lambda i: (0, 0, 0))
    out_specs = [row(k_all), whole] + [vec, vec] * (n_ln // 3)
    out_shape = [jax.ShapeDtypeStruct((m, k_all), da_dtype), jax.ShapeDtypeStruct((N_CHIPS, ka, ng), BF16)]
    out_shape += [jax.ShapeDtypeStruct((1, k_all), F32)] * (2 * (n_ln // 3))
    return pl.pallas_call(
        body, name=name, grid=(n_red,),
        in_specs=[row(k_all), row(g.shape[1]), whole] + ([row(g_more.shape[1])] if n_more else [])
        + [pl.BlockSpec(memory_space=pl.ANY)] * n_after
        + [row(k_all)] * n_resid + ([row(k_all), row(1), vec] if n_ln else []),
        out_specs=out_specs, out_shape=out_shape,
        scratch_shapes=[pltpu.VMEM((N_CHIPS, ka, ng), F32)], compiler_params=_params(("arbitrary",)),
    )(a, g, w4, *([g_more] * n_more), *([after] * n_after), *([resid] * n_resid), *(ln or ()))


def _layer_norm(z, g, b):
    mu = jnp.mean(z, axis=-1, keepdims=True)
    zc = z - mu
    rstd = lax.rsqrt(jnp.mean(zc * zc, axis=-1, keepdims=True) + LN_EPS)
    xhat = zc * rstd
    return xhat * g + b, xhat, rstd


def mlp_fwd(name, x, w1, w2, g, b, *, tm=256):
    fb = w1.shape[2]
    f_all = N_CHIPS * fb

    def body(x_ref, w1_ref, w2_ref, g_ref, b_ref, y_ref, xhat_ref, rstd_ref, relu_ref):
        x_b = x_ref[...].astype(BF16)
        acts = []
        for s in range(N_CHIPS):
            r = jnp.maximum(_dot(x_b, w1_ref[s], _NN), 0.0)
            relu_ref[:, s * fb:(s + 1) * fb] = r.astype(relu_ref.dtype)
            acts.append((r * r).astype(BF16))
        mlp = _dot(jnp.concatenate(acts, axis=1), w2_ref[...].reshape(f_all, D), _NN)
        y_ref[...], xhat_ref[...], rstd_ref[...] = _layer_norm(ALPHA * x_ref[...] + mlp, g_ref[...], b_ref[...])

    row = pl.BlockSpec((tm, D), lambda i: (i, 0))
    vec = pl.BlockSpec((1, D), lambda i: (0, 0))
    once = pl.Buffered(1)
    return pl.pallas_call(
        body, name=name, grid=(S // tm,),
        in_specs=[row, pl.BlockSpec(tuple(w1.shape), lambda i: (0, 0, 0), pipeline_mode=once),
                  pl.BlockSpec(tuple(w2.shape), lambda i: (0, 0, 0), pipeline_mode=once), vec, vec],
        out_specs=[row, row, pl.BlockSpec((tm, 1), lambda i: (i, 0)), pl.BlockSpec((tm, f_all), lambda i: (i, 0))],
        out_shape=[jax.ShapeDtypeStruct((S, D), F32), jax.ShapeDtypeStruct((S, D), F32),
                   jax.ShapeDtypeStruct((S, 1), F32), jax.ShapeDtypeStruct((S, f_all), BF16)],
        compiler_params=_params(("parallel",)),
    )(x, w1, w2, g, b)


def mlp_bwd(name, dz, relu, x, w1, w2, ln, *, tm=512, after=None):
    fb = w1.shape[2]
    n_i = S // tm
    n_after = 0 if after is None else 1

    def body(dz_ref, relu_ref, x_ref, w1_ref, w2_ref, xhat_ref, rstd_ref, gln_ref, *rest):
        out_ref, g1_ref, g2_ref, dg_ref, db_ref, acc1_ref, acc2_ref, dx_ref = rest[n_after:]
        s, i = pl.program_id(0), pl.program_id(1)
        rows = pl.ds(pl.multiple_of(i * tm, tm), tm)
        dz_v = dz_ref[...]

        @pl.when(i == 0)
        def _():
            acc1_ref[...] = jnp.zeros_like(acc1_ref)
            acc2_ref[...] = jnp.zeros_like(acc2_ref)

        @pl.when(s == 0)
        def _():
            dx_ref[rows, :] = ALPHA * dz_v

        @pl.when((s == 0) & (i == 0))
        def _():
            dg_ref[...] = jnp.zeros_like(dg_ref)
            db_ref[...] = jnp.zeros_like(db_ref)

        dz_b = dz_v.astype(BF16)
        r = relu_ref[...]
        dh = (_dot(dz_b, w2_ref[...], _NT) * (2.0 * r.astype(F32))).astype(BF16)
        p2 = _dot(r * r, dz_b, _TN)
        p1 = _dot(x_ref[...], dh, _TN)
        dx_ref[rows, :] += _dot(dh, w1_ref[...], _NT)
        acc1_ref[...] += p1
        acc2_ref[...] += p2

        @pl.when(i == n_i - 1)
        def _():
            g1_ref[...] = acc1_ref[...].astype(g1_ref.dtype)
            g2_ref[...] = acc2_ref[...].astype(g2_ref.dtype)

        @pl.when(s == N_CHIPS - 1)
        def _():
            out_ref[...], pg, pb = _ln_bwd_rows(dx_ref[rows, :], xhat_ref[...], rstd_ref[...], gln_ref[...])
            dg_ref[...] += pg
            db_ref[...] += pb

    row = pl.BlockSpec((tm, D), lambda s, i: (i, 0))
    last_only = lambda s, i: (jnp.where(s == N_CHIPS - 1, i, 0), 0)
    vec = pl.BlockSpec((1, D), lambda s, i: (0, 0))
    return pl.pallas_call(
        body, name=name, grid=(N_CHIPS, n_i),
        in_specs=[row, pl.BlockSpec((tm, fb), lambda s, i: (i, s)), row,
                  pl.BlockSpec((None, D, fb), lambda s, i: (s, 0, 0)), pl.BlockSpec((None, fb, D), lambda s, i: (s, 0, 0)),
                  pl.BlockSpec((tm, D), last_only), pl.BlockSpec((tm, 1), last_only), vec]
        + [pl.BlockSpec(memory_space=pl.ANY)] * n_after,
        out_specs=[pl.BlockSpec((tm, D), last_only), pl.BlockSpec((None, D, fb), lambda s, i: (s, 0, 0)),
                   pl.BlockSpec((None, fb, D), lambda s, i: (s, 0, 0)), vec, vec],
        out_shape=[jax.ShapeDtypeStruct((S, D), F32), jax.ShapeDtypeStruct((N_CHIPS, D, fb), BF16),
                   jax.ShapeDtypeStruct((N_CHIPS, fb, D), BF16), jax.ShapeDtypeStruct((1, D), F32),
                   jax.ShapeDtypeStruct((1, D), F32)],
        scratch_shapes=[pltpu.VMEM((D, fb), F32), pltpu.VMEM((fb, D), F32), pltpu.VMEM((S, D), F32)],
        compiler_params=_params(("arbitrary", "arbitrary")),
    )(dz, relu, x, w1, w2, *ln, *([after] * n_after))


def loss_head(y, target, xhat, rstd, g, *, tm=256):
    def body(y_ref, t_ref, xhat_ref, rstd_ref, g_ref, dz_ref, dg_ref, db_ref, loss_ref):
        i = pl.program_id(0)

        @pl.when(i == 0)
        def _():
            loss_ref[...] = jnp.zeros_like(loss_ref)
            dg_ref[...] = jnp.zeros_like(dg_ref)
            db_ref[...] = jnp.zeros_like(db_ref)

        e = y_ref[...] - t_ref[...]
        part = jnp.sum(jnp.sum(e * e, axis=-1, keepdims=True), axis=0, keepdims=True) * (0.5 / D)
        loss_ref[...] += jnp.broadcast_to(part, loss_ref.shape)
        dz_ref[...], pg, pb = _ln_bwd_rows(e * (1.0 / D), xhat_ref[...], rstd_ref[...], g_ref[...])
        dg_ref[...] += pg
        db_ref[...] += pb

    row = pl.BlockSpec((tm, D), lambda i: (i, 0))
    vec = pl.BlockSpec((1, D), lambda i: (0, 0))
    return pl.pallas_call(
        body, name="loss_head", grid=(S // tm,),
        in_specs=[row, row, row, pl.BlockSpec((tm, 1), lambda i: (i, 0)), vec],
        out_specs=[row, vec, vec, pl.BlockSpec((8, 128), lambda i: (0, 0))],
        out_shape=[jax.ShapeDtypeStruct((S, D), F32), jax.ShapeDtypeStruct((1, D), F32),
                   jax.ShapeDtypeStruct((1, D), F32), jax.ShapeDtypeStruct((8, 128), F32)],
        compiler_params=_params(("arbitrary",)),
    )(y, target, xhat, rstd, g)


def _rows(shape):
    return lax.broadcasted_iota(jnp.int32, shape, 0)


def _shift_down(x, k):
    return jnp.where(_rows(x.shape) >= k, pltpu.roll(x, k, 0), 0.0)


def _shift_up(x, k):
    n = x.shape[0]
    return jnp.where(_rows(x.shape) < n - k, pltpu.roll(x, n - k, 0), 0.0)


def _pool_diff(u, w):
    acc, k = u, 1
    while k < w:
        acc = acc + _shift_down(acc, k)
        k *= 2
    cnt = jnp.minimum(_rows(u.shape) + 1, w).astype(F32)
    return acc / cnt - u, cnt


def pool_fwd(name, proj, pool_w, pool_scale):
    def body(u_ref, w_ref, sc_ref, y_ref):
        for g, w in enumerate(POOL_WINDOWS):
            cols = slice(g * HEAD, (g + 1) * HEAD)
            d, _ = _pool_diff(u_ref[:, cols], w)
            z = _dot(d, w_ref[g], _NN)
            y_ref[:, cols] = (z * sc_ref[:, cols]).astype(y_ref.dtype)

    return pl.pallas_call(
        body, name=name, grid=(1,),
        in_specs=[pl.BlockSpec((S, POOL_W), lambda i: (0, 0)),
                  pl.BlockSpec((4, HEAD, HEAD), lambda i: (0, 0, 0)),
                  pl.BlockSpec((1, POOL_W), lambda i: (0, 0))],
        out_specs=pl.BlockSpec((S, POOL_W), lambda i: (0, 0)),
        out_shape=jax.ShapeDtypeStruct((S, POOL_W + LRU_W), BF16),
        compiler_params=_params(("arbitrary",)),
    )(proj, pool_w, pool_scale)


def pool_bwd(name, proj, dycat, pool_w, pool_scale):
    def body(u_ref, dy_ref, w_ref, sc_ref, du_ref, dw_ref, dsc_ref):
        for g, w in enumerate(POOL_WINDOWS):
            cols = slice(g * HEAD, (g + 1) * HEAD)
            d, cnt = _pool_diff(u_ref[:, cols], w)
            dy = dy_ref[:, cols]
            z = _dot(d, w_ref[g], _NN)
            dsc_ref[:, cols] = jnp.sum(dy * z, axis=0, keepdims=True)
            dz = dy * sc_ref[:, cols]
            dw_ref[g] = _dot(d, dz, _TN)
            dd = _dot(dz, w_ref[g], _NT)
            acc, k = dd / cnt, 1
            while k < w:
                acc = acc + _shift_up(acc, k)
                k *= 2
            du_ref[:, cols] = (acc - dd).astype(du_ref.dtype)

    return pl.pallas_call(
        body, name=name, grid=(1,),
        in_specs=[pl.BlockSpec((S, POOL_W), lambda i: (0, 0)),
                  pl.BlockSpec((S, POOL_W), lambda i: (0, 0)),
                  pl.BlockSpec((4, HEAD, HEAD), lambda i: (0, 0, 0)),
                  pl.BlockSpec((1, POOL_W), lambda i: (0, 0))],
        out_specs=[pl.BlockSpec((S, POOL_W), lambda i: (0, 0)),
                   pl.BlockSpec((4, HEAD, HEAD), lambda i: (0, 0, 0)),
                   pl.BlockSpec((1, POOL_W), lambda i: (0, 0))],
        out_shape=[jax.ShapeDtypeStruct((S, POOL_W + LRU_W), BF16), jax.ShapeDtypeStruct((4, HEAD, HEAD), F32),
                   jax.ShapeDtypeStruct((1, POOL_W), F32)],
        compiler_params=_params(("arbitrary",)),
    )(proj, dycat, pool_w, pool_scale)


def _expm1(x):
    series = x * (1.0 + x * (0.5 + x * (1.0 / 6.0 + x * (1.0 / 24.0 + x * (1.0 / 120.0)))))
    return jnp.where(jnp.abs(x) < 0.05, series, jnp.exp(x) - 1.0)


def _softplus_neg(lam):
    e = jnp.exp(-jnp.abs(lam))
    log1p = jnp.where(e < 0.01, e * (1.0 - e * (0.5 - e * (1.0 / 3.0))), jnp.log(1.0 + e))
    return jnp.maximum(-lam, 0.0) + log1p


_GELU_C = math.sqrt(2.0 / math.pi)


def _gelu(x):
    t = jnp.tanh(_GELU_C * (x + 0.044715 * x * x * x))
    return 0.5 * x * (1.0 + t), t


def _gelu_grad(x, t):
    return 0.5 * (1.0 + t) + 0.5 * x * (1.0 - t * t) * _GELU_C * (1.0 + 3.0 * 0.044715 * x * x)


def _conv(u, cw, cb):
    return cw[3:4] * u + cw[2:3] * _shift_down(u, 1) + cw[1:2] * _shift_down(u, 2) + cw[0:1] * _shift_down(u, 3) + cb


def _lru_gates(cu, wa, ba, wx, bx, lam):
    r = jax.nn.sigmoid(_dot(cu, wa, _NN) + ba)
    i = jax.nn.sigmoid(_dot(cu, wx, _NN) + bx)
    sp = _softplus_neg(lam)
    log_a = (-LRU_C) * r * sp
    a = jnp.exp(log_a)
    mult = jnp.sqrt(-_expm1(2.0 * log_a))
    return r, i, sp, a, mult


def _scan(a_ref, b_ref, h_ref, *, reverse):
    n_blk = S // 8
    row8 = lax.broadcasted_iota(jnp.int32, (8, HEAD), 0)

    def step(j, carry):
        blk = (n_blk - 1 - j) if reverse else j
        r0 = pl.multiple_of(blk * 8, 8)
        a = a_ref[pl.ds(r0, 8), :]
        b = b_ref[pl.ds(r0, 8), :]
        for k in (1, 2, 4):
            if reverse:
                keep = row8 < 8 - k
                a_s, b_s = pltpu.roll(a, 8 - k, 0), pltpu.roll(b, 8 - k, 0)
            else:
                keep = row8 >= k
                a_s, b_s = pltpu.roll(a, k, 0), pltpu.roll(b, k, 0)
            b = jnp.where(keep, a * b_s + b, b)
            a = jnp.where(keep, a * a_s, a)
        h = b + a * carry
        h_ref[pl.ds(r0, 8), :] = h
        edge = h[0:1, :] if reverse else h[7:8, :]
        return jnp.broadcast_to(edge, (8, HEAD))

    lax.fori_loop(0, n_blk, step, jnp.zeros((8, HEAD), F32), unroll=4)


def _lru_specs():
    def col(off):
        return pl.BlockSpec((S, HEAD), lambda h: (0, off + h))
    vec = pl.BlockSpec((1, HEAD), lambda h: (0, h))
    mat = pl.BlockSpec((None, HEAD, HEAD), lambda h: (h, 0, 0))
    cw = pl.BlockSpec((4, HEAD), lambda h: (0, h))
    return col, vec, mat, cw


def lru_fwd(name, proj, conv_w, conv_b, w_a, b_a, w_x, b_x, lam, ycat):
    def body(u_ref, ug_ref, cw_ref, cb_ref, wa_ref, ba_ref, wx_ref, bx_ref, lam_ref, _, y_ref, h_ref, a_s, b_s):
        cu = _conv(u_ref[...], cw_ref[...], cb_ref[...])
        _, i, _, a, mult = _lru_gates(cu, wa_ref[...], ba_ref[...], wx_ref[...], bx_ref[...], lam_ref[...])
        a_s[...] = a
        b_s[...] = mult * (i * cu)
        _scan(a_s, b_s, h_ref, reverse=False)
        gl, _ = _gelu(ug_ref[...])
        y_ref[...] = (h_ref[...] * gl).astype(y_ref.dtype)

    col, vec, mat, cw = _lru_specs()
    out = pl.BlockSpec((S, HEAD), lambda h: (0, h))
    return pl.pallas_call(
        body, name=name, grid=(LRU_HEADS,),
        in_specs=[col(4), col(12), cw, vec, mat, vec, mat, vec, vec, pl.BlockSpec(memory_space=pl.ANY)],
        out_specs=[col(4), out],
        out_shape=[jax.ShapeDtypeStruct(ycat.shape, BF16), jax.ShapeDtypeStruct((S, LRU_W), F32)],
        scratch_shapes=[pltpu.VMEM((S, HEAD), F32), pltpu.VMEM((S, HEAD), F32)],
        input_output_aliases={9: 0}, compiler_params=_params(("parallel",)),
    )(proj, proj, conv_w, conv_b, w_a, b_a, w_x, b_x, lam, ycat)


def lru_bwd(name, proj, hstate, dycat, conv_w, conv_b, w_a, b_a, w_x, b_x, lam, du_cat):
    def body(u_ref, ug_ref, h_ref, dy_ref, cw_ref, cb_ref, wa_ref, ba_ref, wx_ref, bx_ref, lam_ref, _,
             du_ref, dug_ref, dwa_ref, dwx_ref, dba_ref, dbx_ref, dlam_ref, dcw_ref, dcb_ref, a_s, b_s, g_s):
        u = u_ref[...]
        cw = cw_ref[...]
        cu = _conv(u, cw, cb_ref[...])
        lam_v = lam_ref[...]
        r, i, sp, a, mult = _lru_gates(cu, wa_ref[...], ba_ref[...], wx_ref[...], bx_ref[...], lam_v)
        ug = ug_ref[...]
        gl, t = _gelu(ug)
        dy = dy_ref[...]
        h = h_ref[...]
        dug_ref[...] = (dy * h * _gelu_grad(ug, t)).astype(dug_ref.dtype)
        a_s[...] = _shift_up(a, 1)
        b_s[...] = dy * gl
        _scan(a_s, b_s, g_s, reverse=True)
        dxin = g_s[...]
        da = dxin * _shift_down(h, 1)
        dmult = dxin * (i * cu)
        di = dxin * (mult * cu)
        dlog_a = da * a - dmult * (a * a) / mult
        dr_pre = dlog_a * ((-LRU_C) * sp) * (r * (1.0 - r))
        di_pre = di * (i * (1.0 - i))
        dsp = jnp.sum(dlog_a * ((-LRU_C) * r), axis=0, keepdims=True)
        dlam_ref[...] = dsp * (-jax.nn.sigmoid(-lam_v))
        dba_ref[...] = jnp.sum(dr_pre, axis=0, keepdims=True)
        dbx_ref[...] = jnp.sum(di_pre, axis=0, keepdims=True)
        dwa_ref[...] = _dot(cu, dr_pre, _TN)
        dwx_ref[...] = _dot(cu, di_pre, _TN)
        dcu = dxin * (mult * i) + _dot(dr_pre, wa_ref[...], _NT) + _dot(di_pre, wx_ref[...], _NT)
        dcb_ref[...] = jnp.sum(dcu, axis=0, keepdims=True)
        for k in range(4):
            dcw_ref[k:k + 1, :] = jnp.sum(dcu * (_shift_down(u, 3 - k) if k < 3 else u), axis=0, keepdims=True)
        du = cw[3:4] * dcu + cw[2:3] * _shift_up(dcu, 1) + cw[1:2] * _shift_up(dcu, 2) + cw[0:1] * _shift_up(dcu, 3)
        du_ref[...] = du.astype(du_ref.dtype)

    col, vec, mat, cw = _lru_specs()
    out = pl.BlockSpec((S, HEAD), lambda h: (0, h))
    big = jax.ShapeDtypeStruct((S, LRU_W), BF16)
    vec_shape = jax.ShapeDtypeStruct((1, LRU_W), F32)
    mat_shape = jax.ShapeDtypeStruct((LRU_HEADS, HEAD, HEAD), F32)
    return pl.pallas_call(
        body, name=name, grid=(LRU_HEADS,),
        in_specs=[col(4), col(12), out, col(4), cw, vec, mat, vec, mat, vec, vec, pl.BlockSpec(memory_space=pl.ANY)],
        out_specs=[col(4), out, mat, mat, vec, vec, vec, cw, vec],
        out_shape=[jax.ShapeDtypeStruct(du_cat.shape, BF16), big, mat_shape, mat_shape, vec_shape, vec_shape, vec_shape,
                   jax.ShapeDtypeStruct((4, LRU_W), F32), vec_shape],
        scratch_shapes=[pltpu.VMEM((S, HEAD), F32)] * 3,
        input_output_aliases={11: 0}, compiler_params=_params(("parallel",)),
    )(proj, proj, hstate, dycat, conv_w, conv_b, w_a, b_a, w_x, b_x, lam, du_cat)


def rope_tables(pos_col, inv_freq):
    def body(pos_ref, f_ref, c_ref, s1_ref, s2_ref):
        ang = pos_ref[...].astype(F32) * f_ref[...]
        lane = lax.broadcasted_iota(jnp.int32, ang.shape, 1)
        cos, sin = jnp.cos(ang), jnp.sin(ang)
        c_ref[...] = jnp.where(lane < QK_ROPE, cos, 0.0)
        s1_ref[...] = jnp.where(lane < QK_ROPE // 2, -sin, 0.0)
        s2_ref[...] = jnp.where((lane >= QK_ROPE // 2) & (lane < QK_ROPE), sin, 0.0)

    tab = jax.ShapeDtypeStruct((S, HEAD), F32)
    return pl.pallas_call(
        body, name="rope_tables", grid=(1,),
        in_specs=[pl.BlockSpec((S, 1), lambda i: (0, 0)), pl.BlockSpec((1, HEAD), lambda i: (0, 0))],
        out_specs=[pl.BlockSpec((S, HEAD), lambda i: (0, 0))] * 3, out_shape=[tab, tab, tab],
        compiler_params=_params(("arbitrary",)),
    )(pos_col, inv_freq)


def _rope(v, c, s1, s2):
    return v * c + pltpu.roll(v, HEAD - QK_ROPE // 2, 1) * s1 + pltpu.roll(v, QK_ROPE // 2, 1) * s2


def _unrope(d, c, s1, s2):
    return d * c + pltpu.roll(d * s1, QK_ROPE // 2, 1) + pltpu.roll(d * s2, HEAD - QK_ROPE // 2, 1)


def _rms(x):
    rstd = lax.rsqrt(jnp.mean(x * x, axis=-1, keepdims=True) + RMS_EPS)
    return x * rstd, rstd


def mla_prep(name, down, gq, gkv, tabs, *, tm=256):
    def body(dn_ref, gq_ref, gkv_ref, c_ref, s1_ref, s2_ref, cq_ref, ckv_ref, kp_ref):
        xq, _ = _rms(dn_ref[:, :Q_RANK])
        cq_ref[...] = (xq * gq_ref[...]).astype(cq_ref.dtype)
        xkv, _ = _rms(dn_ref[:, Q_RANK:Q_RANK + KV_RANK])
        ckv_ref[...] = (xkv * gkv_ref[...]).astype(ckv_ref.dtype)
        kp = _rope(dn_ref[:, Q_RANK + KV_RANK:], c_ref[...], s1_ref[...], s2_ref[...])
        kp_ref[...] = kp.astype(kp_ref.dtype)

    tab = pl.BlockSpec((tm, HEAD), lambda i: (i, 0))
    return pl.pallas_call(
        body, name=name, grid=(S // tm,),
        in_specs=[pl.BlockSpec((tm, ODD_IN_PAD), lambda i: (i, 0)), pl.BlockSpec((1, Q_RANK), lambda i: (0, 0)),
                  pl.BlockSpec((1, KV_RANK), lambda i: (0, 0)), tab, tab, tab],
        out_specs=[pl.BlockSpec((tm, Q_RANK), lambda i: (i, 0)), pl.BlockSpec((tm, KV_RANK), lambda i: (i, 0)), tab],
        out_shape=[jax.ShapeDtypeStruct((S, Q_RANK), BF16), jax.ShapeDtypeStruct((S, KV_RANK), BF16),
                   jax.ShapeDtypeStruct((S, HEAD), BF16)],
        compiler_params=_params(("parallel",)),
    )(down, gq, gkv, *tabs)


def mla_prep_bwd(name, down, dcq, dckv, dkp, gq, gkv, tabs, *, tm=256):
    def body(dn_ref, dcq_ref, dckv_ref, dkp_ref, gq_ref, gkv_ref, c_ref, s1_ref, s2_ref, dd_ref, dgq_ref, dgkv_ref):
        i = pl.program_id(0)

        def rms_bwd(x, dy, g, dg_ref):
            xh, rstd = _rms(x)
            dxh = dy * g
            dx = rstd * (dxh - xh * jnp.mean(dxh * xh, axis=-1, keepdims=True))
            pg = jnp.sum(dy * xh, axis=0, keepdims=True)

            @pl.when(i == 0)
            def _():
                dg_ref[...] = pg

            @pl.when(i > 0)
            def _():
                dg_ref[...] += pg

            return dx

        dxq = rms_bwd(dn_ref[:, :Q_RANK], dcq_ref[...], gq_ref[...], dgq_ref)
        dd_ref[:, :Q_RANK] = dxq.astype(dd_ref.dtype)
        dxkv = rms_bwd(dn_ref[:, Q_RANK:Q_RANK + KV_RANK], dckv_ref[...], gkv_ref[...], dgkv_ref)
        dd_ref[:, Q_RANK:Q_RANK + KV_RANK] = dxkv.astype(dd_ref.dtype)
        dd_ref[:, Q_RANK + KV_RANK:] = _unrope(dkp_ref[...], c_ref[...], s1_ref[...], s2_ref[...]).astype(dd_ref.dtype)

    tab = pl.BlockSpec((tm, HEAD), lambda i: (i, 0))
    vq = pl.BlockSpec((1, Q_RANK), lambda i: (0, 0))
    vkv = pl.BlockSpec((1, KV_RANK), lambda i: (0, 0))
    return pl.pallas_call(
        body, name=name, grid=(S // tm,),
        in_specs=[pl.BlockSpec((tm, ODD_IN_PAD), lambda i: (i, 0)), pl.BlockSpec((tm, Q_RANK), lambda i: (i, 0)),
                  pl.BlockSpec((tm, KV_RANK), lambda i: (i, 0)), tab, vq, vkv, tab, tab, tab],
        out_specs=[pl.BlockSpec((tm, ODD_IN_PAD), lambda i: (i, 0)), vq, vkv],
        out_shape=[jax.ShapeDtypeStruct((S, ODD_IN_PAD), BF16), jax.ShapeDtypeStruct((1, Q_RANK), F32),
                   jax.ShapeDtypeStruct((1, KV_RANK), F32)],
        compiler_params=_params(("arbitrary",)),
    )(down, dcq, dckv, dkp, gq, gkv, *tabs)


ATT_TQ = 256


def _attn_scores(q_ref, kv_ref, kp_ref, c_ref, s1_ref, s2_ref, nk):
    qn = q_ref[:, :HEAD].astype(BF16)
    qp = _rope(q_ref[:, HEAD:], c_ref[...], s1_ref[...], s2_ref[...]).astype(BF16)
    kn = kv_ref[:nk, :HEAD]
    sc = (_dot(qn, kn, _NT) + _dot(qp, kp_ref[:nk, :], _NT)) * ATT_SCALE
    q_chunk = lax.broadcasted_iota(jnp.int32, (ATT_TQ, ATT_TQ), 0) // CHUNK
    k_chunk = lax.broadcasted_iota(jnp.int32, (ATT_TQ, ATT_TQ), 1) // CHUNK
    own = jnp.where(k_chunk <= q_chunk, sc[:, nk - ATT_TQ:], jnp.finfo(F32).min)
    sc = own if nk == ATT_TQ else jnp.concatenate([sc[:, :nk - ATT_TQ], own], axis=1)
    return sc, qn, qp, kn


def _for_each_prefix(i, fn):
    for k in range(S // ATT_TQ):
        pl.when(i == k)(functools.partial(fn, (k + 1) * ATT_TQ))


def _attn_specs():
    q = pl.BlockSpec((ATT_TQ, QHEAD_PAD), lambda h, i: (i, h))
    kv = pl.BlockSpec((S, QHEAD_PAD), lambda h, i: (0, h))
    kp = pl.BlockSpec((S, HEAD), lambda h, i: (0, 0))
    tab = pl.BlockSpec((ATT_TQ, HEAD), lambda h, i: (i, 0))
    o = pl.BlockSpec((ATT_TQ, HEAD), lambda h, i: (i, h))
    lse = pl.BlockSpec((None, ATT_TQ, 1), lambda h, i: (h, i, 0))
    return q, kv, kp, tab, o, lse


def attn_fwd(name, q, kv, kp, tabs):
    def body(q_ref, kv_ref, kp_ref, c_ref, s1_ref, s2_ref, o_ref, lse_ref):
        i = pl.program_id(1)

        def run(nk):
            sc, _, _, _ = _attn_scores(q_ref, kv_ref, kp_ref, c_ref, s1_ref, s2_ref, nk)
            m = jnp.max(sc, axis=-1, keepdims=True)
            e = jnp.exp(sc - m)
            total = jnp.sum(e, axis=-1, keepdims=True)
            o_ref[...] = (_dot(e, kv_ref[:nk, HEAD:], _NN) * (1.0 / total)).astype(o_ref.dtype)
            lse_ref[...] = m + jnp.log(total)

        _for_each_prefix(i, run)

    qs, kvs, kps, tab, os, ls = _attn_specs()
    return pl.pallas_call(
        body, name=name, grid=(MLA_HEADS, S // ATT_TQ), in_specs=[qs, kvs, kps, tab, tab, tab], out_specs=[os, ls],
        out_shape=[jax.ShapeDtypeStruct((S, MLA_HEADS * HEAD), BF16), jax.ShapeDtypeStruct((MLA_HEADS, S, 1), F32)],
        compiler_params=_params(("parallel", "parallel")),
    )(q, kv, kp, *tabs)


def attn_bwd(name, q, kv, kp, o, lse, do, tabs):
    def body(q_ref, kv_ref, kp_ref, o_ref, lse_ref, do_ref, c_ref, s1_ref, s2_ref, dq_ref, dkv_ref, dkp_ref):
        h, i = pl.program_id(0), pl.program_id(1)

        @pl.when(i == 0)
        def _():
            dkv_ref[...] = jnp.zeros_like(dkv_ref)

        @pl.when((i == 0) & (h == 0))
        def _():
            dkp_ref[...] = jnp.zeros_like(dkp_ref)

        def run(nk):
            sc, qn, qp, kn = _attn_scores(q_ref, kv_ref, kp_ref, c_ref, s1_ref, s2_ref, nk)
            p = jnp.exp(sc - lse_ref[...])
            do_v = do_ref[...]
            delta = jnp.sum(do_v.astype(F32) * o_ref[...].astype(F32), axis=-1, keepdims=True)
            dp = _dot(do_v, kv_ref[:nk, HEAD:], _NT)
            ds = (p * ((dp - delta) * ATT_SCALE)).astype(BF16)
            dq_ref[:, :HEAD] = _dot(ds, kn, _NN).astype(dq_ref.dtype)
            dqp = _unrope(_dot(ds, kp_ref[:nk, :], _NN), c_ref[...], s1_ref[...], s2_ref[...])
            dq_ref[:, HEAD:] = dqp.astype(dq_ref.dtype)
            dkv_ref[:nk, :HEAD] += _dot(ds, qn, _TN)
            dkv_ref[:nk, HEAD:] += _dot(p, do_v, _TN)
            dkp_ref[:nk, :] += _dot(ds, qp, _TN)

        _for_each_prefix(i, run)

    qs, kvs, kps, tab, os, ls = _attn_specs()
    return pl.pallas_call(
        body, name=name, grid=(MLA_HEADS, S // ATT_TQ), in_specs=[qs, kvs, kps, os, ls, os, tab, tab, tab],
        out_specs=[qs, kvs, kps],
        out_shape=[jax.ShapeDtypeStruct((S, MLA_HEADS * QHEAD_PAD), BF16),
                   jax.ShapeDtypeStruct((S, MLA_HEADS * QHEAD_PAD), F32), jax.ShapeDtypeStruct((S, HEAD), F32)],
        compiler_params=_params(("arbitrary", "arbitrary")),
    )(q, kv, kp, o, lse, do, *tabs)


def adamw(name, w, g, m, v):
    rows, cols = w.shape
    tr = rows
    for cand in (512, 256, 128, 64, 32, 16, 8):
        if rows % cand == 0 and cand * cols * 4 <= 2 * 1024 * 1024:
            tr = cand
            break

    def body(w_ref, g_ref, m_ref, v_ref, d_ref, nm_ref, nv_ref):
        g_v = g_ref[...]
        nm = ADAM_B1 * m_ref[...] + (1.0 - ADAM_B1) * g_v
        nv = ADAM_B2 * v_ref[...] + (1.0 - ADAM_B2) * (g_v * g_v)
        m_hat = nm / (1.0 - ADAM_B1 ** ADAM_STEP)
        v_hat = nv / (1.0 - ADAM_B2 ** ADAM_STEP)
        d_ref[...] = (-ADAM_LR) * (m_hat / (jnp.sqrt(v_hat) + ADAM_EPS) + ADAM_WD * w_ref[...])
        nm_ref[...] = nm
        nv_ref[...] = nv

    blk = pl.BlockSpec((tr, cols), lambda i: (i, 0))
    shape = jax.ShapeDtypeStruct((rows, cols), F32)
    return pl.pallas_call(
        body, name=name, grid=(rows // tr,), in_specs=[blk] * 4, out_specs=[blk] * 3, out_shape=[shape] * 3,
        compiler_params=_params(("parallel",)),
    )(w, g, m, v)


def _local_step(x, pos_col, target, sm, weights_of, emit_grads, prefetch):
    inv_freq = ROPE_THETA ** (-jnp.arange(0, QK_ROPE, 2, dtype=F32) / QK_ROPE)
    inv_freq = jnp.concatenate([inv_freq, inv_freq, jnp.zeros((HEAD - QK_ROPE,), F32)])[None, :]
    tabs = rope_tables(pos_col, inv_freq)
    saved, wts = [], {}
    for layer in range(DEPTH):
        j = layer // 2
        n = "l%d_" % layer
        sv = {"x": x}
        wm = wts["mix%d" % layer] = weights_of("mix%d" % layer, x)
        if layer == 0:
            sm = dict(sm, conv_w=wm["conv_w"], gq=wm["gq"], gkv=wm["gkv"])
        if layer % 2 == 0:
            proj = mm_cols(n + "proj", x, wm["w_in"], F32)
            fetched = prefetch("mlp%d" % layer, proj)
            ycat = pool_fwd(n + "pool", proj, sm["pool_w"][j], sm["pool_scale"][j][None])
            ycat, hstate = lru_fwd(n + "lru", proj, sm["conv_w"][j], sm["conv_b"][j][None], sm["w_a"][j],
                                   sm["b_a"][j][None], sm["w_x"][j], sm["b_x"][j][None], sm["lam"][j][None], ycat)
            mix_in, w_mix = ycat, wm["w_out"]
            sv.update(proj=proj, hstate=hstate, ycat=ycat)
        else:
            down = mm_rows(n + "down", x, wm["w_down"], F32)
            fetched = prefetch("mlp%d" % layer, down)
            cq, ckv, kp = mla_prep(n + "prep", down, sm["gq"][j][None], sm["gkv"][j][None], tabs)
            q = mm_cols(n + "q", cq, wm["w_qb"], F32)
            kv = mm_cols(n + "kv", ckv, wm["w_kvb"], BF16)
            o, lse = attn_fwd(n + "attn", q, kv, kp, tabs)
            mix_in, w_mix = o, wm["w_o"]
            sv.update(down=down, cq=cq, ckv=ckv, kp=kp, q=q, kv=kv, o=o, lse=lse)
        x1, xhat1, rstd1 = mm_rows(n + "mixout", mix_in, w_mix, F32, after=fetched,
                                   ln=(x, sm["ln_mix_g"][layer][None], sm["ln_mix_b"][layer][None]))
        wf = wts["mlp%d" % layer] = weights_of("mlp%d" % layer, x1)
        x2, xhat2, rstd2, relu = mlp_fwd(n + "mlp", x1, wf["w1"], wf["w2"], sm["ln_ffn_g"][layer][None],
                                         sm["ln_ffn_b"][layer][None])
        sv.update(xhat1=xhat1, rstd1=rstd1, x1=x1, relu=relu, xhat2=xhat2, rstd2=rstd2)
        saved.append(sv)
        x = x2

    gs = {k: [None] * (DEPTH if k.startswith("ln_") else DEPTH // 2) for k in sm}
    last = saved[DEPTH - 1]
    dz, gs["ln_ffn_g"][DEPTH - 1], gs["ln_ffn_b"][DEPTH - 1], loss_tile = loss_head(
        x, target, last["xhat2"], last["rstd2"], sm["ln_ffn_g"][DEPTH - 1][None])
    sent = None
    for layer in reversed(range(DEPTH)):
        j = layer // 2
        n = "l%d_" % layer
        sv = saved[layer]
        wm, wf = wts["mix%d" % layer], wts["mlp%d" % layer]
        dz, g_w1, g_w2, gs["ln_mix_g"][layer], gs["ln_mix_b"][layer] = mlp_bwd(
            n + "mlp_b", dz, sv["relu"], sv["x1"], wf["w1"], wf["w2"],
            (sv["xhat1"], sv["rstd1"], sm["ln_mix_g"][layer][None]), after=sent)
        sent = emit_grads("mlp%d" % layer, {"w1": g_w1, "w2": g_w2})
        below = saved[layer - 1] if layer else None
        ln_below = (below["xhat2"], below["rstd2"], sm["ln_ffn_g"][layer - 1][None]) if layer else None
        if layer % 2 == 0:
            dycat, g_out = linear_bwd(n + "mixout_b", sv["ycat"], dz, wm["w_out"], col_sharded=False, da_dtype=F32,
                                      after=sent)
            du_cat, gs["pool_w"][j], gs["pool_scale"][j] = pool_bwd(
                n + "pool_b", sv["proj"], dycat, sm["pool_w"][j], sm["pool_scale"][j][None])
            (du_cat, du_gate, gs["w_a"][j], gs["w_x"][j], gs["b_a"][j], gs["b_x"][j], gs["lam"][j], gs["conv_w"][j],
             gs["conv_b"][j]) = lru_bwd(n + "lru_b", sv["proj"], sv["hstate"], dycat, sm["conv_w"][j],
                                        sm["conv_b"][j][None], sm["w_a"][j], sm["b_a"][j][None], sm["w_x"][j],
                                        sm["b_x"][j][None], sm["lam"][j][None], du_cat)
            res = linear_bwd(n + "proj_b", sv["x"], du_cat, wm["w_in"], col_sharded=True, da_dtype=F32, resid=dz,
                             ln=ln_below, g_more=du_gate)
            grads = {"w_in": res[1], "w_out": g_out}
        else:
            do, g_o = linear_bwd(n + "attnout_b", sv["o"], dz, wm["w_o"], col_sharded=False, da_dtype=BF16, after=sent)
            dq, dkv, dkp = attn_bwd(n + "attn_b", sv["q"], sv["kv"], sv["kp"], sv["o"], sv["lse"], do, tabs)
            dcq, g_qb = linear_bwd(n + "q_b", sv["cq"], dq, wm["w_qb"], col_sharded=True, da_dtype=F32)
            dckv, g_kvb = linear_bwd(n + "kv_b", sv["ckv"], dkv, wm["w_kvb"], col_sharded=True, da_dtype=F32)
            ddown, gs["gq"][j], gs["gkv"][j] = mla_prep_bwd(
                n + "prep_b", sv["down"], dcq, dckv, dkp, sm["gq"][j][None], sm["gkv"][j][None], tabs)
            res = linear_bwd(n + "down_b", sv["x"], ddown, wm["w_down"], col_sharded=False, da_dtype=F32, resid=dz,
                             ln=ln_below)
            grads = {"w_down": res[1], "w_qb": g_qb, "w_kvb": g_kvb, "w_o": g_o}
        dz = res[0]
        if layer:
            gs["ln_ffn_g"][layer - 1], gs["ln_ffn_b"][layer - 1] = res[2], res[3]
        sent = emit_grads("mix%d" % layer, grads)
    gs = {k: jnp.stack([a.reshape(sm[k].shape[1:]) for a in v]) for k, v in gs.items()}
    return loss_tile[0, 0], dz, gs, sent


def _place():
    x, y, c = lax.axis_index("x"), lax.axis_index("y"), lax.axis_index("c")
    chips = [(1 - x, y), (x, 1 - y), (1 - x, 1 - y)]
    return x, y, c, chips


def _hbm_call(body, name, args, out_shape, scratch, aliases=None):
    return pl.pallas_call(
        body, name=name, in_specs=[pl.BlockSpec(memory_space=pl.ANY)] * len(args),
        out_specs=[pl.BlockSpec(memory_space=pl.ANY)] * len(out_shape), out_shape=out_shape,
        scratch_shapes=scratch, input_output_aliases=aliases or {},
        compiler_params=pltpu.CompilerParams(has_side_effects=True),
    )(*args)


HBM_SPEC = pl.BlockSpec(memory_space=pltpu.HBM)
SEM_SPEC = pl.BlockSpec(memory_space=pltpu.SEMAPHORE)
EFFECT = pltpu.SideEffectType.DATAFLOW_SIDE_EFFECTING


def _remote(src, dst, send_sem, recv_sem, device):
    return pltpu.make_async_remote_copy(src_ref=src, dst_ref=dst, send_sem=send_sem, recv_sem=recv_sem,
                                        device_id=device, device_id_type=MESH)


def place_own(name, srcs, place, steps=4):
    n = len(srcs)
    in_specs, out_specs, out_shape = [], [], []
    for s in srcs:
        rows, cols = s.shape
        tr = rows // steps
        in_specs.append(pl.BlockSpec((tr, cols), lambda i, p: (i, 0)))
        out_specs.append(pl.BlockSpec((None, tr, cols), lambda i, p: (p[0], i, 0)))
        out_shape.append(jax.ShapeDtypeStruct((N_CHIPS, rows, cols), s.dtype))

    def body(p_ref, *refs):
        for i_ref, o_ref in zip(refs[:n], refs[n:]):
            o_ref[...] = i_ref[...]

    return pl.pallas_call(
        body, name=name, out_shape=out_shape,
        grid_spec=pltpu.PrefetchScalarGridSpec(num_scalar_prefetch=1, grid=(steps,), in_specs=in_specs,
                                               out_specs=out_specs),
        compiler_params=_params(("parallel",)),
    )(place, *srcs)


def split_start(name, groups, n_sems, plan, token_in=None):
    sizes = [len(srcs) for srcs, _ in groups]
    n, n_groups = sum(sizes), len(groups)
    srcs = [pltpu.with_memory_space_constraint(a, pltpu.HBM) for s, _ in groups for a in s]
    lands = [pltpu.with_memory_space_constraint(a, pltpu.HBM) for _, l in groups for a in l]
    extra = [] if token_in is None else [token_in]

    def body(*refs):
        src_refs, land_refs = refs[:n], refs[n:2 * n]
        outs = refs[2 * n + len(extra):]
        off = 0
        for g, size in enumerate(sizes):
            sends, _ = plan(src_refs[off:off + size], land_refs[off:off + size], outs[2 * g], outs[2 * g + 1])
            for cp in sends:
                cp.start()
            off += size
        token = outs[2 * n_groups + 2 * n]
        token[...] = jnp.zeros_like(token)

    sems = [pltpu.SemaphoreType.DMA((size * n_sems,)) for size in sizes for _ in range(2)]
    res = pl.pallas_call(
        body, name=name,
        out_shape=(*sems, *[pltpu.HBM(a.shape, a.dtype) for a in srcs + lands], jax.ShapeDtypeStruct((8, 128), F32)),
        in_specs=[HBM_SPEC] * (2 * n) + [pl.BlockSpec(memory_space=pl.ANY)] * len(extra),
        out_specs=(*[SEM_SPEC] * len(sems), *[HBM_SPEC] * (2 * n), pl.BlockSpec(memory_space=pltpu.VMEM)),
        input_output_aliases={i: len(sems) + i for i in range(2 * n)},
        compiler_params=pltpu.CompilerParams(has_side_effects=EFFECT),
    )(*srcs, *lands, *extra)
    started, off = [], 0
    bufs = res[len(sems):]
    for g, size in enumerate(sizes):
        started.append(dict(send=res[2 * g], recv=res[2 * g + 1], srcs=list(bufs[off:off + size]),
                            lands=list(bufs[n + off:n + off + size]), plan=plan))
        off += size
    return started, res[-1]


def _wait_started(st, bufs, send_sems, recv_sems):
    n = len(st["srcs"])
    sends, expects = st["plan"](bufs[:n], bufs[n:], send_sems, recv_sems)
    for cp in sends:
        cp.wait_send()
    for cp in expects:
        cp.wait_recv()


def split_wait(name, started, after):
    sizes = [len(st["srcs"]) + len(st["lands"]) for st in started]
    n_buf = sum(sizes)

    def body(*refs):
        bufs, sems = refs[:n_buf], refs[n_buf:n_buf + 2 * len(started)]
        off = 0
        for g, (st, n) in enumerate(zip(started, sizes)):
            _wait_started(st, bufs[off:off + n], sems[2 * g], sems[2 * g + 1])
            off += n

    bufs = [a for st in started for a in st["srcs"] + st["lands"]]
    sems = [s for st in started for s in (st["send"], st["recv"])]
    res = pl.pallas_call(
        body, name=name, out_shape=tuple(pltpu.HBM(a.shape, a.dtype) for a in bufs),
        in_specs=[HBM_SPEC] * n_buf + [SEM_SPEC] * len(sems) + [pl.BlockSpec(memory_space=pl.ANY)],
        out_specs=tuple([HBM_SPEC] * n_buf), input_output_aliases={i: i for i in range(n_buf)},
        compiler_params=pltpu.CompilerParams(has_side_effects=EFFECT),
    )(*bufs, *sems, after)
    out, off = [], 0
    for st, n in zip(started, sizes):
        out.append((list(res[off:off + len(st["srcs"])]), list(res[off + len(st["srcs"]):off + n])))
        off += n
    return out


def split_relay(name, st, n_sems, plan, after):
    n_src, n = len(st["srcs"]), len(st["lands"])

    def body(*refs):
        bufs = refs[:n_src + n]
        outs = refs[n_src + n + 3:]
        _wait_started(st, bufs, refs[n_src + n], refs[n_src + n + 1])
        sends, _ = plan((), bufs[n_src:], outs[0], outs[1])
        for cp in sends:
            cp.start()
        outs[2 + n][...] = jnp.zeros((8, 128), F32)

    bufs = st["srcs"] + st["lands"]
    res = pl.pallas_call(
        body, name=name,
        out_shape=(pltpu.SemaphoreType.DMA((n * n_sems,)), pltpu.SemaphoreType.DMA((n * n_sems,)),
                   *[pltpu.HBM(a.shape, a.dtype) for a in st["lands"]], jax.ShapeDtypeStruct((8, 128), F32)),
        in_specs=[HBM_SPEC] * (n_src + n) + [SEM_SPEC] * 2 + [pl.BlockSpec(memory_space=pl.ANY)],
        out_specs=(SEM_SPEC, SEM_SPEC, *[HBM_SPEC] * n, pl.BlockSpec(memory_space=pltpu.VMEM)),
        input_output_aliases={n_src + i: 2 + i for i in range(n)},
        compiler_params=pltpu.CompilerParams(has_side_effects=EFFECT),
    )(*bufs, st["send"], st["recv"], after)
    return dict(send=res[0], recv=res[1], srcs=[], lands=list(res[2:2 + n]), plan=plan), res[-1]


def gather_plan(src_refs, land_refs, send_sems, recv_sems):
    x, y, c, chips = _place()
    me = 2 * x + y
    sends, expects = [], []
    for k, (s, d) in enumerate(zip(src_refs, land_refs)):
        mine = pl.ds(c * (s.shape[0] // 2), s.shape[0] // 2)
        for j, (px, py) in enumerate(chips):
            sem = 3 * k + j
            sends.append(_remote(s.at[mine], d.at[me, mine], send_sems.at[sem], recv_sems.at[sem], (px, py, c)))
            expects.append(_remote(s.at[mine], d.at[2 * px + py, mine], send_sems.at[sem], recv_sems.at[sem], (px, py, c)))
    return sends, expects


def relay_plan(src_refs, land_refs, send_sems, recv_sems):
    x, y, c, chips = _place()
    sends, expects = [], []
    for k, d in enumerate(land_refs):
        hr = d.shape[1] // 2
        mine, theirs = pl.ds(c * hr, hr), pl.ds((1 - c) * hr, hr)
        for j, (px, py) in enumerate(chips):
            sem, chip = 3 * k + j, 2 * px + py
            sends.append(_remote(d.at[chip, mine], d.at[chip, mine], send_sems.at[sem], recv_sems.at[sem], (x, y, 1 - c)))
            expects.append(_remote(d.at[chip, mine], d.at[chip, theirs], send_sems.at[sem], recv_sems.at[sem], (x, y, 1 - c)))
    return sends, expects


def _reduce_part(s, chip_idx, h):
    hr = s.shape[1] // 2
    return s.at[chip_idx if s.shape[0] == N_CHIPS else 0, pl.ds(h * hr, hr)]


def reduce_plan(src_refs, land_refs, send_sems, recv_sems):
    x, y, c, chips = _place()
    me_chip, me_dev = 2 * x + y, 4 * x + 2 * y + c
    sends, expects = [], []
    for k, (s, d) in enumerate(zip(src_refs, land_refs)):
        part = functools.partial(_reduce_part, s)
        for j, (px, py) in enumerate(chips):
            for h in range(2):
                sends.append(_remote(part(2 * px + py, h), d.at[me_dev], send_sems.at[7 * k + 2 * j + h],
                                     recv_sems.at[7 * k + 2 * j + c], (px, py, h)))
                expects.append(_remote(part(me_chip, c), d.at[4 * px + 2 * py + h], send_sems.at[7 * k + 2 * j + h],
                                       recv_sems.at[7 * k + 2 * j + h], (px, py, h)))
        sends.append(_remote(part(me_chip, 1 - c), d.at[me_dev], send_sems.at[7 * k + 6], recv_sems.at[7 * k + 6],
                             (x, y, 1 - c)))
        expects.append(_remote(part(me_chip, c), d.at[me_dev + 1 - 2 * c], send_sems.at[7 * k + 6],
                               recv_sems.at[7 * k + 6], (x, y, 1 - c)))
    return sends, expects


def sibling_swap_halves(name, fulls):
    n = len(fulls)

    def body(*refs):
        outs = refs[n:2 * n]
        send_sems, recv_sems = refs[2 * n:]
        x, y, c, _ = _place()
        copies = []
        for k in range(n):
            nl, rows = fulls[k].shape[:2]
            hr = rows // 2
            mine = outs[k].at[pl.ds(0, nl), pl.ds(c * hr, hr)]
            theirs = outs[k].at[pl.ds(0, nl), pl.ds((1 - c) * hr, hr)]
            copies.append((_remote(mine, mine, send_sems.at[k], recv_sems.at[k], (x, y, 1 - c)),
                           _remote(mine, theirs, send_sems.at[k], recv_sems.at[k], (x, y, 1 - c))))
        for send, _ in copies:
            send.start()
        for send, recv in copies:
            send.wait_send()
            recv.wait_recv()

    out_shape = [jax.ShapeDtypeStruct(f.shape, f.dtype) for f in fulls]
    scratch = [pltpu.SemaphoreType.DMA((n,)), pltpu.SemaphoreType.DMA((n,))]
    return _hbm_call(body, name, fulls, out_shape, scratch, aliases={k: k for k in range(n)})


N_DEV = 8


def sum_devices(name, items, place, steps):
    n = len(items)
    in_specs, args, out_specs, out_shape, aliases = [], [place], [], [], {}
    for landed, own, layer, n_layers, _ in items:
        _, hr, cols = landed.shape
        tr = hr // steps
        slot = (lambda p: p[0]) if own.shape[0] == N_CHIPS else (lambda p: 0)
        in_specs += [pl.BlockSpec((N_DEV, tr, cols), lambda r, p: (0, r, 0)),
                     pl.BlockSpec((None, tr, cols), lambda r, p, slot=slot: (slot(p), p[1] * steps + r, 0))]
        args += [landed, own]
        out_specs.append(pl.BlockSpec((None, tr, cols), lambda r, p, layer=layer: (layer, p[1] * steps + r, 0)))
        out_shape.append(jax.ShapeDtypeStruct((n_layers, 2 * hr, cols), F32))
    for k, item in enumerate(items):
        if item[4] is not None:
            in_specs.append(pl.BlockSpec(memory_space=pl.ANY))
            aliases[len(args)] = k
            args.append(item[4])
    n_prev = len(aliases)

    def body(p_ref, *refs):
        for k in range(n):
            r_ref, own_ref, o_ref = refs[2 * k], refs[2 * k + 1], refs[2 * n + n_prev + k]
            mine = own_ref[...].astype(F32)
            acc = jnp.zeros_like(mine)
            for d in range(N_DEV):
                acc = acc + jnp.where(p_ref[2] == d, mine, r_ref[d].astype(F32))
            o_ref[...] = acc

    return pl.pallas_call(
        body, name=name, out_shape=out_shape,
        grid_spec=pltpu.PrefetchScalarGridSpec(num_scalar_prefetch=1, grid=(steps,), in_specs=in_specs,
                                               out_specs=out_specs),
        input_output_aliases=aliases, compiler_params=_params(("parallel",)),
    )(*args)


def _pack(arrs, rows_multiple):
    flat = []
    for a in arrs:
        v = a.reshape(-1).astype(F32)
        flat.append(jnp.pad(v, (0, (-v.shape[0]) % HEAD)))
    v = jnp.concatenate(flat)
    v = jnp.pad(v, (0, (-v.shape[0]) % (HEAD * rows_multiple)))
    return v.reshape(-1, HEAD)


def _unpack(packed, shapes):
    flat = packed.reshape(-1)
    out, off = [], 0
    for shp in shapes:
        size = int(np.prod(shp))
        out.append(flat[off:off + size].reshape(shp))
        off += size + (-size) % HEAD
    return out


BIG = ["even_w_in", "even_w_out", "mla_w_down", "mla_w_qb", "mla_w_kvb", "mla_w_o", "mlp_w1", "mlp_w2"]
BIG_KEY = {"even_w_in": "w_in", "even_w_out": "w_out", "mla_w_down": "w_down", "mla_w_qb": "w_qb",
           "mla_w_kvb": "w_kvb", "mla_w_o": "w_o", "mlp_w1": "w1", "mlp_w2": "w2"}
SMALL_KEY = {"ln_mix_g": "ln_mix_g", "ln_mix_b": "ln_mix_b", "ln_ffn_g": "ln_ffn_g", "ln_ffn_b": "ln_ffn_b",
             "pool_w": "pool_w", "pool_scale": "pool_scale", "lru_conv_w": "conv_w", "lru_conv_b": "conv_b",
             "lru_w_a": "w_a", "lru_b_a": "b_a", "lru_w_x": "w_x", "lru_b_x": "b_x", "lru_lambda": "lam",
             "mla_q_norm_g": "gq", "mla_kv_norm_g": "gkv"}
SMALL = list(SMALL_KEY)
SMALL_SHARDED = ["lru_conv_w", "mla_q_norm_g", "mla_kv_norm_g"]
SMALL_MATRICES = ["pool_w", "lru_w_a", "lru_w_x"]
WEIGHTS = ["ln_mix_g", "ln_mix_b", "ln_ffn_g", "ln_ffn_b", "even_w_in", "pool_w", "pool_scale", "lru_conv_w",
           "lru_conv_b", "lru_w_a", "lru_b_a", "lru_w_x", "lru_b_x", "lru_lambda", "even_w_out", "mla_w_down",
           "mla_q_norm_g", "mla_kv_norm_g", "mla_w_qb", "mla_w_kvb", "mla_w_o", "mlp_w1", "mlp_w2"]


GROUPS = ["mix0", "mlp0", "mix1", "mlp1", "mix2", "mlp2", "mix3", "mlp3"]


def _group_keys(group):
    layer = int(group[3:])
    if group.startswith("mlp"):
        return [("mlp_w1", "w1", layer), ("mlp_w2", "w2", layer)]
    if layer % 2 == 0:
        return [("even_w_in", "w_in", layer // 2), ("even_w_out", "w_out", layer // 2)]
    return [("mla_w_down", "w_down", layer // 2), ("mla_w_qb", "w_qb", layer // 2),
            ("mla_w_kvb", "w_kvb", layer // 2), ("mla_w_o", "w_o", layer // 2)]


def _pad_q_heads(w):
    lead = w.shape[:-1]
    w = w.reshape(lead + (2, QK_NOPE + QK_ROPE))
    w = jnp.pad(w, ((0, 0),) * len(lead) + ((0, 0), (0, QHEAD_PAD - QK_NOPE - QK_ROPE)))
    return w.reshape(lead + (2 * QHEAD_PAD,))


def _unpad_q_heads(g):
    lead = g.shape[:-1]
    return g.reshape(lead + (2, QHEAD_PAD))[..., :QK_NOPE + QK_ROPE].reshape(lead + (2 * (QK_NOPE + QK_ROPE),))


def _step(x, positions, loss_target, w, m, v):
    cx, cy, cc = lax.axis_index("x"), lax.axis_index("y"), lax.axis_index("c")
    chip = 2 * cx + cy
    place = jnp.stack([chip, cc, 2 * chip + cc]).astype(jnp.int32)

    prepared = dict(w)
    prepared["mla_w_down"] = jnp.pad(w["mla_w_down"], ((0, 0), (0, 0), (0, ODD_IN_PAD - ODD_IN)))
    prepared["mla_w_qb"] = _pad_q_heads(w["mla_w_qb"])
    small_shard_shapes = [w[k].shape for k in SMALL_SHARDED]
    sources = {g: [prepared[name][idx].astype(BF16) for name, _, idx in _group_keys(g)] for g in GROUPS}
    sources[GROUPS[0]].append(_pack([w[k] for k in SMALL_SHARDED], 32))
    gathering, token = {}, None
    for name, part in (("first", GROUPS[:1]), ("rest", GROUPS[1:])):
        zones = place_own("own_" + name, [a for g in part for a in sources[g]], place)
        groups, off = [], 0
        for g in part:
            groups.append((sources[g], zones[off:off + len(sources[g])]))
            off += len(sources[g])
        started, token = split_start("gather_" + name, groups, 3, gather_plan, token)
        gathering.update(zip(part, started))
    relayed, relay_token = {}, {}

    def prefetch(g, after):
        relayed[g], relay_token[g] = split_relay("relay_" + g, gathering[g], 3, relay_plan, after)
        return relay_token[g]

    prefetch(GROUPS[0], token)

    def weights_of(g, after):
        _, lands = split_wait("gathered_" + g, [relayed[g]], relay_token[g] if g == GROUPS[0] else after)[0]
        if g.startswith("mlp") and g != GROUPS[-1]:
            prefetch(GROUPS[GROUPS.index(g) + 1], lands[0])
        out = {key: land for (_, key, _), land in zip(_group_keys(g), lands)}
        if g == GROUPS[0]:
            per_chip = [_unpack(lands[-1][s], small_shard_shapes) for s in range(N_CHIPS)]
            for i, key in enumerate(("conv_w", "gq", "gkv")):
                out[key] = jnp.concatenate([p[i] for p in per_chip], axis=-1)
        return out

    reducing = []

    def reduce_start(g, srcs, token_in=None):
        lands = [lax.empty((N_DEV, s.shape[1] // 2, s.shape[2]), s.dtype) for s in srcs]
        started, token = split_start("reduce_" + g, [(srcs, lands)], 7, reduce_plan, token_in)
        reducing.append((g, started[0]))
        return token

    def emit_grads(g, grads):
        return reduce_start(g, [grads[key] for _, key, _ in _group_keys(g)])

    sm = {SMALL_KEY[k]: w[k] for k in SMALL if k not in SMALL_SHARDED}
    loss, grad_x, gs, last_sent = _local_step(x[0], positions.reshape(S, 1), loss_target[0], sm, weights_of,
                                              emit_grads, prefetch)
    loss = lax.psum(loss, ("x", "y", "c"))

    packs = [_pack([gs[SMALL_KEY[k]] for k in SMALL if (k in SMALL_MATRICES) == mat], 128)[None] for mat in (False, True)]
    small_sent = reduce_start("small", [packs[0], packs[1].astype(BF16)], last_sent)

    grad, delta, new_m, new_v = {}, {}, {}, {}
    late_groups = ("mix0", "small")
    stacks, after = {}, small_sent
    for late in (False, True):
        part = [(g, st) for g, st in reducing if (g in late_groups) == late]
        landed = split_wait("reduced_late" if late else "reduced_early", [st for _, st in part], after)
        for (g, _), (owns, lands) in zip(part, landed):
            if g == "small":
                items = [(land, own, 0, 1, None) for own, land in zip(owns, lands)]
                stacks["small_vec"], stacks["small_mat"] = sum_devices("sum_small", items, place, 4)
                continue
            keys = _group_keys(g)
            items = [(land, own, idx, w[name].shape[0], stacks.get(name))
                     for (name, _, idx), own, land in zip(keys, owns, lands)]
            for (name, _, _), stack in zip(keys, sum_devices("sum_" + g, items, place, 4)):
                stacks[name] = stack
        names = [k for k in BIG if (k in ("even_w_in", "even_w_out")) == late] + (["small_vec", "small_mat"] if late else [])
        reduced = dict(zip(names, sibling_swap_halves("swap_late" if late else "swap_early", [stacks[k] for k in names])))
        if not late:
            reduced["mla_w_down"] = reduced["mla_w_down"][..., :ODD_IN]
            reduced["mla_w_qb"] = _unpad_q_heads(reduced["mla_w_qb"])
        for k in names:
            if k.startswith("small"):
                continue
            grad[k] = reduced[k]
            shp = w[k].shape
            view = lambda a: a.reshape(-1, shp[-1])
            d, nm, nv = adamw("adamw_" + BIG_KEY[k], view(w[k]), view(grad[k]), view(m[k]), view(v[k]))
            delta[k], new_m[k], new_v[k] = d.reshape(shp), nm.reshape(shp), nv.reshape(shp)
            after = d
    g_small = {}
    for mat, key in ((False, "small_vec"), (True, "small_mat")):
        names = [k for k in SMALL if (k in SMALL_MATRICES) == mat]
        g_small.update(zip(names, _unpack(reduced[key], [gs[SMALL_KEY[k]].shape for k in names])))
    for k in SMALL_SHARDED:
        width = w[k].shape[-1]
        g_small[k] = lax.dynamic_slice_in_dim(g_small[k], chip * width, width, axis=-1)
    grad.update(g_small)

    shapes = [w[k].shape for k in SMALL]
    d, nm, nv = adamw("adamw_small", *[_pack([t[k] for k in SMALL], 512) for t in (w, grad, m, v)])
    for k, dk, mk, vk in zip(SMALL, _unpack(d, shapes), _unpack(nm, shapes), _unpack(nv, shapes)):
        delta[k], new_m[k], new_v[k] = dk, mk, vk
    return (loss, grad_x[None], *[grad[k] for k in WEIGHTS], *[delta[k] for k in WEIGHTS],
            *[new_m[k] for k in WEIGHTS], *[new_v[k] for k in WEIGHTS])


def kernel(x, positions, ln_mix_g, ln_mix_b, ln_ffn_g, ln_ffn_b, even_w_in, pool_w, pool_scale, lru_conv_w, lru_conv_b, lru_w_a, lru_b_a, lru_w_x, lru_b_x, lru_lambda, even_w_out, mla_w_down, mla_q_norm_g, mla_kv_norm_g, mla_w_qb, mla_w_kvb, mla_w_o, mlp_w1, mlp_w2, loss_target, m_ln_mix_g, m_ln_mix_b, m_ln_ffn_g, m_ln_ffn_b, m_even_w_in, m_pool_w, m_pool_scale, m_lru_conv_w, m_lru_conv_b, m_lru_w_a, m_lru_b_a, m_lru_w_x, m_lru_b_x, m_lru_lambda, m_even_w_out, m_mla_w_down, m_mla_q_norm_g, m_mla_kv_norm_g, m_mla_w_qb, m_mla_w_kvb, m_mla_w_o, m_mlp_w1, m_mlp_w2, v_ln_mix_g, v_ln_mix_b, v_ln_ffn_g, v_ln_ffn_b, v_even_w_in, v_pool_w, v_pool_scale, v_lru_conv_w, v_lru_conv_b, v_lru_w_a, v_lru_b_a, v_lru_w_x, v_lru_b_x, v_lru_lambda, v_even_w_out, v_mla_w_down, v_mla_q_norm_g, v_mla_kv_norm_g, v_mla_w_qb, v_mla_w_kvb, v_mla_w_o, v_mlp_w1, v_mlp_w2):
    args = locals()
    w = {k: args[k] for k in WEIGHTS}
    m = {k: args["m_" + k] for k in WEIGHTS}
    v = {k: args["v_" + k] for k in WEIGHTS}
    return _step(x, positions, loss_target, w, m, v)
```
